```python
import jax, jax.numpy as jnp
from jax import lax
import numpy as np

D_MODEL = 1024
BATCH = 8
SEQ = 2048
DEPTH = 4

HEAD_DIM = 128
N_Q_HEADS = D_MODEL // HEAD_DIM
N_KV_HEADS = N_Q_HEADS // 4
W_ATT = N_Q_HEADS * HEAD_DIM
W_KV = N_KV_HEADS * HEAD_DIM
W_SC = D_MODEL
SC_WIDTH = 3
W_CF = D_MODEL
CF_WIDTH = 31
GRID_W = 64
Q_BLOCK = 128
ROPE_THETA = 10000.0
RMS_EPS = 1e-6
LN_EPS = 1e-5
SPLITS = (W_SC, W_SC, W_SC, W_SC,
          W_ATT, W_KV, W_KV, W_ATT,
          W_CF, W_CF, W_CF,
          D_MODEL, D_MODEL, D_MODEL)
P_IN = sum(SPLITS)

kernel_name = "hybrid_conv_gqa_conformer_encoder"


def rms_norm(x, g):
    x32 = x.astype(jnp.float32)
    y = x32 * lax.rsqrt(jnp.mean(x32 * x32, axis=-1, keepdims=True) + RMS_EPS)
    return (y * g.astype(jnp.float32)).astype(x.dtype)


def layer_norm(x, g, b):
    x32 = x.astype(jnp.float32)
    mu = jnp.mean(x32, axis=-1, keepdims=True)
    xc = x32 - mu
    y = xc * lax.rsqrt(jnp.mean(xc * xc, axis=-1, keepdims=True) + LN_EPS)
    return (y * g.astype(jnp.float32) + b.astype(jnp.float32)).astype(x.dtype)


def depthwise_conv(x, w):
    k = w.shape[0]
    return lax.conv_general_dilated(
        x, w[:, None, :], window_strides=(1,), padding=[(k // 2, k // 2)],
        dimension_numbers=("NWC", "WIO", "NWC"), feature_group_count=x.shape[-1])


def axial_rope_tables(seq_len):
    n_rows = seq_len // GRID_W
    rows = jnp.repeat(jnp.arange(n_rows), GRID_W)
    cols = jnp.tile(jnp.arange(GRID_W), n_rows)
    n_freq = HEAD_DIM // 4
    inv_freq = ROPE_THETA ** (-jnp.arange(n_freq, dtype=jnp.float32) / n_freq)
    pos = jnp.stack([rows, cols], axis=-1).astype(jnp.float32)
    ang = pos[:, :, None] * inv_freq
    return jnp.cos(ang), jnp.sin(ang)


def apply_axial_rope(x, cos, sin):
    b, s, h, d = x.shape
    n_freq = d // 4
    xr = x.astype(jnp.float32).reshape(b, s, h, 2, 2, n_freq)
    x1, x2 = xr[..., 0, :], xr[..., 1, :]
    c = cos[None, :, None]
    sn = sin[None, :, None]
    out = jnp.stack([x1 * c - x2 * sn, x2 * c + x1 * sn], axis=-2)
    return out.reshape(b, s, h, d).astype(x.dtype)


def block_gqa_attention(q, k, v):
    b, s, hq, hd = q.shape
    hkv = k.shape[2]
    g = hq // hkv
    nb = s // Q_BLOCK
    qb = q.reshape(b, nb, Q_BLOCK, hkv, g, hd).transpose(1, 0, 2, 3, 4, 5)
    scale = hd ** -0.5

    def one_block(qi):
        sc = jnp.einsum("bqkgd,bskd->bkgqs", qi, k).astype(jnp.float32) * scale
        p = jax.nn.softmax(sc, axis=-1).astype(v.dtype)
        return jnp.einsum("bkgqs,bskd->bqkgd", p, v)

    o = lax.map(one_block, qb)
    return o.transpose(1, 0, 2, 3, 4, 5).reshape(b, s, hq * hd)


def hybrid_layer(x, cos, sin, norm_pre, norm_post, w_in, conv_a_w, q_norm, k_norm,
                 conv_c_w, conv_c_b, ln_c_g, ln_c_b, w_out_a, w_out_b, w_out_c, w_o):
    b, s, _ = x.shape
    h = rms_norm(x, norm_pre)
    proj = h @ w_in
    offsets = []
    acc = 0
    for w in SPLITS[:-1]:
        acc += w
        offsets.append(acc)
    (a_b, a_c, a_x, a_g, q, k, v, b_g, c_u, c_v, c_g,
     m_a, m_b, m_c) = jnp.split(proj, offsets, axis=-1)

    ya = a_b * depthwise_conv(a_c * a_x, conv_a_w)
    ya = (ya * jax.nn.silu(a_g)) @ w_out_a

    q = apply_axial_rope(rms_norm(q.reshape(b, s, N_Q_HEADS, HEAD_DIM), q_norm), cos, sin)
    k = apply_axial_rope(rms_norm(k.reshape(b, s, N_KV_HEADS, HEAD_DIM), k_norm), cos, sin)
    v = v.reshape(b, s, N_KV_HEADS, HEAD_DIM)
    yb = (block_gqa_attention(q, k, v) * jax.nn.silu(b_g)) @ w_out_b

    u = c_u * jax.nn.sigmoid(c_v)
    u = depthwise_conv(u, conv_c_w) + conv_c_b
    u = jax.nn.silu(layer_norm(u, ln_c_g, ln_c_b))
    yc = (u * jax.nn.silu(c_g)) @ w_out_c

    y = jax.nn.sigmoid(m_a) * ya + jax.nn.sigmoid(m_b) * yb + jax.nn.sigmoid(m_c) * yc
    return x + rms_norm(y @ w_o, norm_post)


def _fwd_setup_inputs(seed: int = 0) -> dict:
    key = jax.random.key(seed)
    ks = jax.random.split(key, 16)
    f32 = jnp.float32
    L, D = DEPTH, D_MODEL

    def nrm(k, shape, scale):
        return jax.random.normal(k, shape, f32) * scale

    return {
        "x": nrm(ks[0], (BATCH, SEQ, D), 1.0),
        "norm_pre": 1.0 + nrm(ks[1], (L, D), 0.05),
        "norm_post": 1.0 + nrm(ks[2], (L, D), 0.05),
        "w_in": nrm(ks[3], (L, D, P_IN), D ** -0.5),
        "conv_a_w": nrm(ks[4], (L, SC_WIDTH, W_SC), SC_WIDTH ** -0.5),
        "q_norm": 1.0 + nrm(ks[5], (L, HEAD_DIM), 0.05),
        "k_norm": 1.0 + nrm(ks[6], (L, HEAD_DIM), 0.05),
        "conv_c_w": nrm(ks[7], (L, CF_WIDTH, W_CF), CF_WIDTH ** -0.5),
        "conv_c_b": nrm(ks[8], (L, W_CF), 0.02),
        "ln_c_g": 1.0 + nrm(ks[9], (L, W_CF), 0.05),
        "ln_c_b": nrm(ks[10], (L, W_CF), 0.02),
        "w_out_a": nrm(ks[11], (L, W_SC, D), W_SC ** -0.5),
        "w_out_b": nrm(ks[12], (L, W_ATT, D), W_ATT ** -0.5),
        "w_out_c": nrm(ks[13], (L, W_CF, D), W_CF ** -0.5),
        "w_o": nrm(ks[14], (L, D, D), D ** -0.5),
    }


def _fwd_reference(x, norm_pre, norm_post, w_in, conv_a_w, q_norm, k_norm, conv_c_w, conv_c_b,
              ln_c_g, ln_c_b, w_out_a, w_out_b, w_out_c, w_o):
    cos, sin = axial_rope_tables(x.shape[1])
    for l in range(DEPTH):
        x = hybrid_layer(x, cos, sin, norm_pre[l], norm_post[l], w_in[l], conv_a_w[l],
                         q_norm[l], k_norm[l], conv_c_w[l], conv_c_b[l], ln_c_g[l], ln_c_b[l],
                         w_out_a[l], w_out_b[l], w_out_c[l], w_o[l])
    return x


import jax as _jax
import jax.numpy as _jnp

TWIN_FORMAT = 'train_step'
FWD_PARAMS = ['x', 'norm_pre', 'norm_post', 'w_in', 'conv_a_w', 'q_norm', 'k_norm', 'conv_c_w', 'conv_c_b', 'ln_c_g', 'ln_c_b', 'w_out_a', 'w_out_b', 'w_out_c', 'w_o']
TWIN_WEIGHTS = ['norm_pre', 'norm_post', 'w_in', 'conv_a_w', 'q_norm', 'k_norm', 'conv_c_w', 'conv_c_b', 'ln_c_g', 'ln_c_b', 'w_out_a', 'w_out_b', 'w_out_c', 'w_o']
TWIN_DIFF_INPUT = 'x'
TWIN_INPUTS = ['x', 'norm_pre', 'norm_post', 'w_in', 'conv_a_w', 'q_norm', 'k_norm', 'conv_c_w', 'conv_c_b', 'ln_c_g', 'ln_c_b', 'w_out_a', 'w_out_b', 'w_out_c', 'w_o', 'loss_target', 'm_norm_pre', 'm_norm_post', 'm_w_in', 'm_conv_a_w', 'm_q_norm', 'm_k_norm', 'm_conv_c_w', 'm_conv_c_b', 'm_ln_c_g', 'm_ln_c_b', 'm_w_out_a', 'm_w_out_b', 'm_w_out_c', 'm_w_o', 'v_norm_pre', 'v_norm_post', 'v_w_in', 'v_conv_a_w', 'v_q_norm', 'v_k_norm', 'v_conv_c_w', 'v_conv_c_b', 'v_ln_c_g', 'v_ln_c_b', 'v_w_out_a', 'v_w_out_b', 'v_w_out_c', 'v_w_o']
TWIN_OUTPUTS = ['loss', 'grad_x', 'grad_norm_pre', 'grad_norm_post', 'grad_w_in', 'grad_conv_a_w', 'grad_q_norm', 'grad_k_norm', 'grad_conv_c_w', 'grad_conv_c_b', 'grad_ln_c_g', 'grad_ln_c_b', 'grad_w_out_a', 'grad_w_out_b', 'grad_w_out_c', 'grad_w_o', 'delta_norm_pre', 'delta_norm_post', 'delta_w_in', 'delta_conv_a_w', 'delta_q_norm', 'delta_k_norm', 'delta_conv_c_w', 'delta_conv_c_b', 'delta_ln_c_g', 'delta_ln_c_b', 'delta_w_out_a', 'delta_w_out_b', 'delta_w_out_c', 'delta_w_o', 'new_m_norm_pre', 'new_m_norm_post', 'new_m_w_in', 'new_m_conv_a_w', 'new_m_q_norm', 'new_m_k_norm', 'new_m_conv_c_w', 'new_m_conv_c_b', 'new_m_ln_c_g', 'new_m_ln_c_b', 'new_m_w_out_a', 'new_m_w_out_b', 'new_m_w_out_c', 'new_m_w_o', 'new_v_norm_pre', 'new_v_norm_post', 'new_v_w_in', 'new_v_conv_a_w', 'new_v_q_norm', 'new_v_k_norm', 'new_v_conv_c_w', 'new_v_conv_c_b', 'new_v_ln_c_g', 'new_v_ln_c_b', 'new_v_w_out_a', 'new_v_w_out_b', 'new_v_w_out_c', 'new_v_w_o']
TWIN_LEAF_KINDS = {'loss': 'loss', 'grad_x': 'grad_x', 'grad_norm_pre': 'grad_w', 'grad_norm_post': 'grad_w', 'grad_w_in': 'grad_w', 'grad_conv_a_w': 'grad_w', 'grad_q_norm': 'grad_w', 'grad_k_norm': 'grad_w', 'grad_conv_c_w': 'grad_w', 'grad_conv_c_b': 'grad_w', 'grad_ln_c_g': 'grad_w', 'grad_ln_c_b': 'grad_w', 'grad_w_out_a': 'grad_w', 'grad_w_out_b': 'grad_w', 'grad_w_out_c': 'grad_w', 'grad_w_o': 'grad_w', 'delta_norm_pre': 'delta_w', 'delta_norm_post': 'delta_w', 'delta_w_in': 'delta_w', 'delta_conv_a_w': 'delta_w', 'delta_q_norm': 'delta_w', 'delta_k_norm': 'delta_w', 'delta_conv_c_w': 'delta_w', 'delta_conv_c_b': 'delta_w', 'delta_ln_c_g': 'delta_w', 'delta_ln_c_b': 'delta_w', 'delta_w_out_a': 'delta_w', 'delta_w_out_b': 'delta_w', 'delta_w_out_c': 'delta_w', 'delta_w_o': 'delta_w', 'new_m_norm_pre': 'new_m', 'new_m_norm_post': 'new_m', 'new_m_w_in': 'new_m', 'new_m_conv_a_w': 'new_m', 'new_m_q_norm': 'new_m', 'new_m_k_norm': 'new_m', 'new_m_conv_c_w': 'new_m', 'new_m_conv_c_b': 'new_m', 'new_m_ln_c_g': 'new_m', 'new_m_ln_c_b': 'new_m', 'new_m_w_out_a': 'new_m', 'new_m_w_out_b': 'new_m', 'new_m_w_out_c': 'new_m', 'new_m_w_o': 'new_m', 'new_v_norm_pre': 'new_v', 'new_v_norm_post': 'new_v', 'new_v_w_in': 'new_v', 'new_v_conv_a_w': 'new_v', 'new_v_q_norm': 'new_v', 'new_v_k_norm': 'new_v', 'new_v_conv_c_w': 'new_v', 'new_v_conv_c_b': 'new_v', 'new_v_ln_c_g': 'new_v', 'new_v_ln_c_b': 'new_v', 'new_v_w_out_a': 'new_v', 'new_v_w_out_b': 'new_v', 'new_v_w_out_c': 'new_v', 'new_v_w_o': 'new_v'}


def _forward(args):
    return _fwd_reference(*[args[k] for k in FWD_PARAMS])


def _output_shape():
    out = _jax.eval_shape(lambda: _forward(_fwd_setup_inputs(0)))
    return out.shape, out.dtype

N_MICROBATCH = 1
ADAM_LR = 0.001
ADAM_B1 = 0.9
ADAM_B2 = 0.999
ADAM_EPS = 1e-08
ADAM_WD = 0.01
ADAM_STEP = 10
PER_EXAMPLE_BATCH_AXIS = {'x': 0, 'loss_target': 0}
SHARED_INPUTS = []
_WEIGHT_DTYPES = {'norm_pre': _jnp.float32, 'norm_post': _jnp.float32, 'w_in': _jnp.float32, 'conv_a_w': _jnp.float32, 'q_norm': _jnp.float32, 'k_norm': _jnp.float32, 'conv_c_w': _jnp.float32, 'conv_c_b': _jnp.float32, 'ln_c_g': _jnp.float32, 'ln_c_b': _jnp.float32, 'w_out_a': _jnp.float32, 'w_out_b': _jnp.float32, 'w_out_c': _jnp.float32, 'w_o': _jnp.float32}
MOMENT_SCALE = {'norm_pre': 1.262439e+00, 'norm_post': 1.596630e+01, 'w_in': 3.530653e-01, 'conv_a_w': 5.493917e-01, 'q_norm': 1.609599e-01, 'k_norm': 1.586323e-01, 'conv_c_w': 3.502910e-01, 'conv_c_b': 8.305330e-01, 'ln_c_g': 4.577041e-01, 'ln_c_b': 4.538603e-01, 'w_out_a': 5.433925e-01, 'w_out_b': 6.330867e-02, 'w_out_c': 3.572057e-01, 'w_o': 6.523428e-01}


def _to_microbatches(a, axis):
    t = _jnp.moveaxis(a, axis, 0)
    t = t.reshape((N_MICROBATCH, t.shape[0] // N_MICROBATCH) + t.shape[1:])
    return _jnp.moveaxis(t, 1, axis + 1)


def setup_inputs(seed: int = 0) -> dict:
    inp = _fwd_setup_inputs(seed)
    key = _jax.random.fold_in(_jax.random.key(seed), 7919)
    shape, _ = _output_shape()
    out = dict(inp)
    out["loss_target"] = _jax.random.normal(_jax.random.fold_in(key, 0), shape, _jnp.float32)
    for i, name in enumerate(TWIN_WEIGHTS):
        w = inp[name].astype(_jnp.float32)
        if MOMENT_SCALE is None:
            s = _jnp.sqrt(_jnp.mean(_jnp.square(w)) + 1e-30)
        else:
            s = MOMENT_SCALE[name]
        km, kv = _jax.random.split(_jax.random.fold_in(key, i + 1))
        out[name] = w
        out["m_" + name] = s * _jax.random.normal(km, w.shape, _jnp.float32)
        out["v_" + name] = (s * s) * _jax.random.uniform(kv, w.shape, _jnp.float32, 0.5, 1.5)
    if N_MICROBATCH > 1:
        for name, axis in PER_EXAMPLE_BATCH_AXIS.items():
            out[name] = _to_microbatches(out[name], axis)
    return {'x': out['x'], 'norm_pre': out['norm_pre'], 'norm_post': out['norm_post'], 'w_in': out['w_in'], 'conv_a_w': out['conv_a_w'], 'q_norm': out['q_norm'], 'k_norm': out['k_norm'], 'conv_c_w': out['conv_c_w'], 'conv_c_b': out['conv_c_b'], 'ln_c_g': out['ln_c_g'], 'ln_c_b': out['ln_c_b'], 'w_out_a': out['w_out_a'], 'w_out_b': out['w_out_b'], 'w_out_c': out['w_out_c'], 'w_o': out['w_o'], 'loss_target': out['loss_target'], 'm_norm_pre': out['m_norm_pre'], 'm_norm_post': out['m_norm_post'], 'm_w_in': out['m_w_in'], 'm_conv_a_w': out['m_conv_a_w'], 'm_q_norm': out['m_q_norm'], 'm_k_norm': out['m_k_norm'], 'm_conv_c_w': out['m_conv_c_w'], 'm_conv_c_b': out['m_conv_c_b'], 'm_ln_c_g': out['m_ln_c_g'], 'm_ln_c_b': out['m_ln_c_b'], 'm_w_out_a': out['m_w_out_a'], 'm_w_out_b': out['m_w_out_b'], 'm_w_out_c': out['m_w_out_c'], 'm_w_o': out['m_w_o'], 'v_norm_pre': out['v_norm_pre'], 'v_norm_post': out['v_norm_post'], 'v_w_in': out['v_w_in'], 'v_conv_a_w': out['v_conv_a_w'], 'v_q_norm': out['v_q_norm'], 'v_k_norm': out['v_k_norm'], 'v_conv_c_w': out['v_conv_c_w'], 'v_conv_c_b': out['v_conv_c_b'], 'v_ln_c_g': out['v_ln_c_g'], 'v_ln_c_b': out['v_ln_c_b'], 'v_w_out_a': out['v_w_out_a'], 'v_w_out_b': out['v_w_out_b'], 'v_w_out_c': out['v_w_out_c'], 'v_w_o': out['v_w_o']}


def _loss(weights, diff, rest, loss_target):
    with _jax.named_scope("forward"):
        args = {**rest, TWIN_DIFF_INPUT: diff, **{k: w.astype(_WEIGHT_DTYPES[k]) for k, w in weights.items()}}
        y = _forward(args)
    with _jax.named_scope("loss_head"):
        err = _jnp.square(y.astype(_jnp.float32) - loss_target)
        return 0.5 * _jnp.sum(_jnp.mean(err, axis=-1)) if err.ndim else 0.5 * err


def _adamw(w, g, m, v):
    m = ADAM_B1 * m + (1.0 - ADAM_B1) * g
    v = ADAM_B2 * v + (1.0 - ADAM_B2) * _jnp.square(g)
    m_hat = m / (1.0 - ADAM_B1 ** ADAM_STEP)
    v_hat = v / (1.0 - ADAM_B2 ** ADAM_STEP)
    delta = -ADAM_LR * (m_hat / (_jnp.sqrt(v_hat) + ADAM_EPS) + ADAM_WD * w)
    return delta, m, v


def reference(x, norm_pre, norm_post, w_in, conv_a_w, q_norm, k_norm, conv_c_w, conv_c_b, ln_c_g, ln_c_b, w_out_a, w_out_b, w_out_c, w_o, loss_target, m_norm_pre, m_norm_post, m_w_in, m_conv_a_w, m_q_norm, m_k_norm, m_conv_c_w, m_conv_c_b, m_ln_c_g, m_ln_c_b, m_w_out_a, m_w_out_b, m_w_out_c, m_w_o, v_norm_pre, v_norm_post, v_w_in, v_conv_a_w, v_q_norm, v_k_norm, v_conv_c_w, v_conv_c_b, v_ln_c_g, v_ln_c_b, v_w_out_a, v_w_out_b, v_w_out_c, v_w_o):
    given = dict(x=x, norm_pre=norm_pre, norm_post=norm_post, w_in=w_in, conv_a_w=conv_a_w, q_norm=q_norm, k_norm=k_norm, conv_c_w=conv_c_w, conv_c_b=conv_c_b, ln_c_g=ln_c_g, ln_c_b=ln_c_b, w_out_a=w_out_a, w_out_b=w_out_b, w_out_c=w_out_c, w_o=w_o, loss_target=loss_target, m_norm_pre=m_norm_pre, m_norm_post=m_norm_post, m_w_in=m_w_in, m_conv_a_w=m_conv_a_w, m_q_norm=m_q_norm, m_k_norm=m_k_norm, m_conv_c_w=m_conv_c_w, m_conv_c_b=m_conv_c_b, m_ln_c_g=m_ln_c_g, m_ln_c_b=m_ln_c_b, m_w_out_a=m_w_out_a, m_w_out_b=m_w_out_b, m_w_out_c=m_w_out_c, m_w_o=m_w_o, v_norm_pre=v_norm_pre, v_norm_post=v_norm_post, v_w_in=v_w_in, v_conv_a_w=v_conv_a_w, v_q_norm=v_q_norm, v_k_norm=v_k_norm, v_conv_c_w=v_conv_c_w, v_conv_c_b=v_conv_c_b, v_ln_c_g=v_ln_c_g, v_ln_c_b=v_ln_c_b, v_w_out_a=v_w_out_a, v_w_out_b=v_w_out_b, v_w_out_c=v_w_out_c, v_w_o=v_w_o)
    weights = {n: given[n] for n in TWIN_WEIGHTS}
    shared = {n: given[n] for n in SHARED_INPUTS}
    per_example = {n: given[n] for n in ['x']}
    grad_fn = _jax.value_and_grad(_loss, argnums=(0, 1))

    def one_microbatch(ex, loss_target):
        ex = dict(ex)
        diff = ex.pop(TWIN_DIFF_INPUT)
        return grad_fn(weights, diff, {**shared, **ex}, loss_target)

    if N_MICROBATCH == 1:
        loss, (grad_w, grad_x) = one_microbatch(per_example, given["loss_target"])
    else:
        def body(carry, xs):
            loss_sum, grad_sum = carry
            l_k, (gw_k, gx_k) = one_microbatch(xs[0], xs[1])
            with _jax.named_scope("update"):
                return (loss_sum + l_k, _jax.tree.map(_jnp.add, grad_sum, gw_k)), gx_k

        init = (_jnp.zeros((), _jnp.float32), _jax.tree.map(_jnp.zeros_like, weights))
        (loss, grad_w), grad_x = _jax.lax.scan(body, init, (per_example, given["loss_target"]))
    with _jax.named_scope("update"):
        delta_w, new_m, new_v = {}, {}, {}
        for n in TWIN_WEIGHTS:
            delta_w[n], new_m[n], new_v[n] = _adamw(weights[n], grad_w[n], given["m_" + n], given["v_" + n])
    return (loss, grad_x, *[grad_w[n] for n in TWIN_WEIGHTS], *[delta_w[n] for n in TWIN_WEIGHTS],
            *[new_m[n] for n in TWIN_WEIGHTS], *[new_v[n] for n in TWIN_WEIGHTS])
```

```python
import math

import jax
import jax.numpy as jnp
from jax import lax
from jax.experimental import pallas as pl
from jax.experimental.pallas import tpu as pltpu

f32, bf16 = jnp.float32, jnp.bfloat16

D = 1024
S = 2048
L = 4
HD = 128
NQ = D // HD
NKV = NQ // 4
G = NQ // NKV
WKV = NKV * HD
GRID_W = 64
ROPE_THETA = 10000.0
RMS_EPS = 1e-6
LN_EPS = 1e-5
NDEV = 8
CA_W, CC_W = 3, 31
P = 12 * D + 2 * WKV
PSH = P // NDEV
PAIR = 2 * PSH
CT = 128
ADAM_LR, ADAM_B1, ADAM_B2, ADAM_EPS, ADAM_WD, ADAM_STEP = 0.001, 0.9, 0.999, 1e-08, 0.01, 10
VMEM_LIMIT = 56 * 1024 * 1024
MESH = pl.DeviceIdType.MESH

_OFF = {}
_o = 0
for _n, _w in (("a_b", D), ("a_c", D), ("a_x", D), ("a_g", D), ("q", D), ("k", WKV), ("v", WKV), ("b_g", D),
               ("c_u", D), ("c_v", D), ("c_g", D), ("m_a", D), ("m_b", D), ("m_c", D)):
    _OFF[_n] = (_o, _w)
    _o += _w
PIECES = tuple(_OFF)


def _cp(sem=None, **kw):
    return pltpu.CompilerParams(dimension_semantics=sem, vmem_limit_bytes=VMEM_LIMIT, **kw)


def _sig(x):
    return 1.0 / (1.0 + jnp.exp(-x))


def _silu(x):
    return x * _sig(x)


def _dsilu(x):
    s = _sig(x)
    return s * (1.0 + x * (1.0 - s))


def _row_specs(name, tm):
    off, w = _OFF[name]
    bw = math.gcd(off, w) if off else w
    return [pl.BlockSpec((tm, bw), (lambda i, *_, b=off // bw + t: (i, b))) for t in range(w // bw)]


def _cat(refs):
    return refs[0][...] if len(refs) == 1 else jnp.concatenate([r[...] for r in refs], axis=1)


def _chan_spec(name):
    off, _ = _OFF[name]
    return pl.BlockSpec((S, CT), lambda j, b=off // CT: (0, b + j))


def _full(shape):
    return pl.BlockSpec(shape, lambda *_: (0,) * len(shape))


def _coords():
    return lax.axis_index("x"), lax.axis_index("y"), lax.axis_index("c")


def all_gather(shards, name):
    n = len(shards)

    def body(*refs):
        ins, outs = refs[:n], refs[n:2 * n]
        send_sems, recv_sems, local_sems = refs[2 * n:]
        x, y, c = _coords()
        me, sibling = (x, y, c), (x, y, 1 - c)
        chips = [(1 - x, y), (x, 1 - y), (1 - x, 1 - y)]

        def slot(a, p):
            return outs[a].at[4 * p[0] + 2 * p[1] + p[2]]

        def copy(a, k, block, to, src=None):
            return pltpu.make_async_remote_copy(
                src_ref=slot(a, block) if src is None else src, dst_ref=slot(a, block),
                send_sem=send_sems.at[7 * a + k], recv_sem=recv_sems.at[7 * a + k], device_id=to, device_id_type=MESH)

        mine = [pltpu.make_async_copy(ins[a], slot(a, me), local_sems.at[a]) for a in range(n)]
        for cp in mine:
            cp.start()
        first = []
        for a in range(n):
            first.append(copy(a, 0, me, sibling, src=ins[a]))
            first += [copy(a, 1 + j, me, (*chip, c), src=ins[a]) for j, chip in enumerate(chips)]
        for cp in first:
            cp.start()
        passed = []
        for j, chip in enumerate(chips):
            for a in range(n):
                copy(a, 1 + j, (*chip, c), me).wait_recv()
                fw = copy(a, 4 + j, (*chip, c), sibling)
                fw.start()
                passed.append(fw)
        for a in range(n):
            copy(a, 0, sibling, me).wait_recv()
            for j, chip in enumerate(chips):
                copy(a, 4 + j, (*chip, 1 - c), me).wait_recv()
        for cp in first + passed:
            cp.wait_send()
        for cp in mine:
            cp.wait()

    anyspec = pl.BlockSpec(memory_space=pl.ANY)
    return pl.pallas_call(
        body, name=name,
        out_shape=[jax.ShapeDtypeStruct((NDEV,) + s.shape, s.dtype) for s in shards],
        in_specs=[anyspec] * n, out_specs=[anyspec] * n,
        scratch_shapes=[pltpu.SemaphoreType.DMA((7 * n,)), pltpu.SemaphoreType.DMA((7 * n,)), pltpu.SemaphoreType.DMA((n,))],
    )(*shards)


def scatter_parts(parts, name):
    n = len(parts)

    def body(*refs):
        ins, outs = refs[:n], refs[n:2 * n]
        send_sems, recv_sems, local_sems = refs[2 * n:]
        x, y, c = _coords()
        me = 4 * x + 2 * y + c
        mine = [pltpu.make_async_copy(ins[a].at[me], outs[a].at[me], local_sems.at[a]) for a in range(n)]
        for cp in mine:
            cp.start()
        sends = []
        for a in range(n):
            for k in range(1, NDEV):
                kx, ky, kc = (k >> 2) & 1, (k >> 1) & 1, k & 1
                px = 1 - x if kx else x
                py = 1 - y if ky else y
                pc = 1 - c if kc else c
                peer = 4 * px + 2 * py + pc
                sends.append(pltpu.make_async_remote_copy(
                    src_ref=ins[a].at[peer], dst_ref=outs[a].at[me],
                    send_sem=send_sems.at[7 * a + k - 1], recv_sem=recv_sems.at[7 * a + k - 1],
                    device_id=(px, py, pc), device_id_type=MESH))
        for cp in sends:
            cp.start()
        for cp in sends:
            cp.wait_recv()
        for cp in sends:
            cp.wait_send()
        for cp in mine:
            cp.wait()

    anyspec = pl.BlockSpec(memory_space=pl.ANY)
    return pl.pallas_call(
        body, name=name,
        out_shape=[jax.ShapeDtypeStruct(p.shape, p.dtype) for p in parts],
        in_specs=[anyspec] * n, out_specs=[anyspec] * n,
        scratch_shapes=[pltpu.SemaphoreType.DMA((7 * n,)), pltpu.SemaphoreType.DMA((7 * n,)), pltpu.SemaphoreType.DMA((n,))],
    )(*parts)


def cast_bf16(x, name):
    r, c = x.shape
    tr = min(r, 256)

    def body(x_ref, o_ref):
        o_ref[...] = x_ref[...].astype(bf16)

    return pl.pallas_call(
        body, name=name, grid=(r // tr,), out_shape=jax.ShapeDtypeStruct((r, c), bf16),
        in_specs=[pl.BlockSpec((tr, c), lambda i: (i, 0))], out_specs=pl.BlockSpec((tr, c), lambda i: (i, 0)),
        compiler_params=_cp(("parallel",)))(x)


def relayout_win(wg, name):
    tr = min(D, 512)

    def body(w_ref, o_ref):
        o_ref[:, 0:PSH] = w_ref[0]
        o_ref[:, PSH:PAIR] = w_ref[1]

    return pl.pallas_call(
        body, name=name, grid=(NDEV // 2, D // tr), out_shape=jax.ShapeDtypeStruct((D, P), bf16),
        in_specs=[pl.BlockSpec((2, tr, PSH), lambda p, i: (p, i, 0))],
        out_specs=pl.BlockSpec((tr, PAIR), lambda p, i: (i, p)),
        compiler_params=_cp(("parallel", "parallel")))(wg)


def proj_fwd(xin, g_pre, wfull, name):
    tm, tn = min(S, 512), 1280

    def body(x_ref, g_ref, w_ref, proj_ref, h_ref, hs):
        @pl.when(pl.program_id(1) == 0)
        def _():
            x = x_ref[...]
            r = lax.rsqrt(jnp.mean(x * x, axis=-1, keepdims=True) + RMS_EPS)
            h = (x * r * g_ref[...]).astype(bf16)
            hs[...] = h
            h_ref[...] = h
        proj_ref[...] = jnp.dot(hs[...], w_ref[...], preferred_element_type=f32)

    return pl.pallas_call(
        body, name=name, grid=(S // tm, P // tn),
        out_shape=[jax.ShapeDtypeStruct((S, P), f32), jax.ShapeDtypeStruct((S, D), bf16)],
        in_specs=[pl.BlockSpec((tm, D), lambda i, j: (i, 0)), _full((1, D)), pl.BlockSpec((D, tn), lambda i, j: (0, j))],
        out_specs=[pl.BlockSpec((tm, tn), lambda i, j: (i, j)), pl.BlockSpec((tm, D), lambda i, j: (i, 0))],
        scratch_shapes=[pltpu.VMEM((tm, D), bf16)],
        compiler_params=_cp(("parallel", "arbitrary")))(xin, g_pre, wfull)


RC = 128


def _fill_pad(pad, halo, val_fn):
    pad[0:halo, :] = jnp.zeros((halo, CT), f32)
    pad[S + halo:S + 2 * halo, :] = jnp.zeros((halo, CT), f32)

    def step(i, carry):
        rows = pl.ds(pl.multiple_of(i * RC, RC), RC)
        pad[pl.ds(pl.multiple_of(i * RC, RC) + halo, RC), :] = val_fn(rows)
        return carry
    lax.fori_loop(0, S // RC, step, 0)


def brancha_fwd(proj, convw, name):
    def body(ab, ac, ax, ag, w_ref, o_ref, pad):
        _fill_pad(pad, 8, lambda rows: ac[rows, :] * ax[rows, :])
        w = [w_ref[0, k:k + 1, :] for k in range(CA_W)]

        def step(i, carry):
            base = pl.multiple_of(i * RC, RC)
            rows = pl.ds(base, RC)
            t = sum(w[k] * pad[pl.ds(base + 7 + k, RC), :] for k in range(CA_W))
            o_ref[rows, :] = (ab[rows, :] * t * _silu(ag[rows, :])).astype(bf16)
            return carry
        lax.fori_loop(0, S // RC, step, 0)

    return pl.pallas_call(
        body, name=name, grid=(D // CT,), out_shape=jax.ShapeDtypeStruct((S, D), bf16),
        in_specs=[_chan_spec("a_b"), _chan_spec("a_c"), _chan_spec("a_x"), _chan_spec("a_g"),
                  pl.BlockSpec((1, 40, CT), lambda j: (j, 0, 0))],
        out_specs=pl.BlockSpec((S, CT), lambda j: (0, j)),
        scratch_shapes=[pltpu.VMEM((S + 16, CT), f32)],
        compiler_params=_cp(("parallel",)))(proj, proj, proj, proj, convw)


def branchc1_fwd(proj, convw, cbias, name):
    def body(cu, cv, w_ref, b_ref, o_ref, pad):
        _fill_pad(pad, 16, lambda rows: cu[rows, :] * _sig(cv[rows, :]))

        def step(i, carry):
            base = pl.multiple_of(i * RC, RC)
            acc = jnp.zeros((RC, CT), f32) + b_ref[...]
            for k in range(CC_W):
                acc = acc + w_ref[0, 8 + k:9 + k, :] * pad[pl.ds(base + k + 1, RC), :]
            o_ref[pl.ds(base, RC), :] = acc
            return carry
        lax.fori_loop(0, S // RC, step, 0)

    return pl.pallas_call(
        body, name=name, grid=(D // CT,), out_shape=jax.ShapeDtypeStruct((S, D), f32),
        in_specs=[_chan_spec("c_u"), _chan_spec("c_v"), pl.BlockSpec((1, 40, CT), lambda j: (j, 0, 0)),
                  pl.BlockSpec((1, CT), lambda j: (0, j))],
        out_specs=pl.BlockSpec((S, CT), lambda j: (0, j)),
        scratch_shapes=[pltpu.VMEM((S + 32, CT), f32)],
        compiler_params=_cp(("parallel",)))(proj, proj, convw, cbias)


def _swap32(x):
    lane = lax.broadcasted_iota(jnp.int32, x.shape, 1)
    return jnp.where((lane // 32) % 2 == 1, pltpu.roll(x, 32, 1), pltpu.roll(x, HD - 32, 1))


def _rope(y, cos, sin):
    return y * cos + _swap32(y) * sin


def qkv_fwd(proj, qn, kn, cos, sin, name):
    tm = min(S, 256)
    nq, nk, nv = len(_row_specs("q", tm)), len(_row_specs("k", tm)), len(_row_specs("v", tm))

    def body(*refs):
        q = _cat(refs[:nq])
        k = _cat(refs[nq:nq + nk])
        v = _cat(refs[nq + nk:nq + nk + nv])
        qn_ref, kn_ref, cos_ref, sin_ref, qh_ref, kh_ref, vh_ref = refs[nq + nk + nv:]
        cos, sin = cos_ref[...], sin_ref[...]

        def heads(xx, gn, out_ref, n):
            for h in range(n):
                xh = xx[:, h * HD:(h + 1) * HD]
                r = lax.rsqrt(jnp.mean(xh * xh, axis=-1, keepdims=True) + RMS_EPS)
                out_ref[:, h * HD:(h + 1) * HD] = _rope(xh * r * gn, cos, sin).astype(bf16)
        heads(q, qn_ref[...], qh_ref, NQ)
        heads(k, kn_ref[...], kh_ref, NKV)
        vh_ref[...] = v.astype(bf16)

    row = lambda w: pl.BlockSpec((tm, w), lambda i: (i, 0))
    return pl.pallas_call(
        body, name=name, grid=(S // tm,),
        out_shape=[jax.ShapeDtypeStruct((S, D), bf16), jax.ShapeDtypeStruct((S, WKV), bf16), jax.ShapeDtypeStruct((S, WKV), bf16)],
        in_specs=_row_specs("q", tm) + _row_specs("k", tm) + _row_specs("v", tm) + [_full((1, HD)), _full((1, HD)), row(HD), row(HD)],
        out_specs=[row(D), row(WKV), row(WKV)],
        compiler_params=_cp(("parallel",)))(*([proj] * (nq + nk + nv)), qn, kn, cos, sin)


def _softmax_rows(q, k):
    s = lax.dot_general(q, k, (((1,), (1,)), ((), ())), preferred_element_type=f32) * (HD ** -0.5)
    p = jnp.exp(s - jnp.max(s, axis=-1, keepdims=True))
    return p / jnp.sum(p, axis=-1, keepdims=True)


def attn_fwd(qh, kh, vh, proj, name):
    tq = min(S, 256)
    bg_off = _OFF["b_g"][0] // HD

    def body(q_ref, k_ref, v_ref, bg_ref, o_ref, y_ref):
        pn = _softmax_rows(q_ref[...], k_ref[...])
        o = jnp.dot(pn.astype(bf16), v_ref[...], preferred_element_type=f32)
        o_ref[...] = o
        y_ref[...] = (o * _silu(bg_ref[...])).astype(bf16)

    head = lambda kv, g, i: (i, kv * G + g)
    return pl.pallas_call(
        body, name=name, grid=(NKV, G, S // tq),
        out_shape=[jax.ShapeDtypeStruct((S, D), f32), jax.ShapeDtypeStruct((S, D), bf16)],
        in_specs=[pl.BlockSpec((tq, HD), head), pl.BlockSpec((S, HD), lambda kv, g, i: (0, kv)),
                  pl.BlockSpec((S, HD), lambda kv, g, i: (0, kv)),
                  pl.BlockSpec((tq, HD), lambda kv, g, i: (i, bg_off + kv * G + g))],
        out_specs=[pl.BlockSpec((tq, HD), head), pl.BlockSpec((tq, HD), head)],
        compiler_params=_cp(("parallel", "parallel", "parallel")))(qh, kh, vh, proj)


def _ln_parts(u1):
    mu = jnp.mean(u1, axis=-1, keepdims=True)
    xc = u1 - mu
    rstd = lax.rsqrt(jnp.mean(xc * xc, axis=-1, keepdims=True) + LN_EPS)
    return xc * rstd, rstd


def branchc2_fwd(u1, proj, lng, lnb, name):
    tm = min(S, 256)
    ncg = len(_row_specs("c_g", tm))

    def body(*refs):
        u_ref = refs[0]
        cg = _cat(refs[1:1 + ncg])
        g_ref, b_ref, o_ref = refs[1 + ncg:]
        xh, _ = _ln_parts(u_ref[...])
        o_ref[...] = (_silu(xh * g_ref[...] + b_ref[...]) * _silu(cg)).astype(bf16)

    row = pl.BlockSpec((tm, D), lambda i: (i, 0))
    return pl.pallas_call(
        body, name=name, grid=(S // tm,), out_shape=jax.ShapeDtypeStruct((S, D), bf16),
        in_specs=[row] + _row_specs("c_g", tm) + [_full((1, D)), _full((1, D))], out_specs=row,
        compiler_params=_cp(("parallel",)))(u1, *([proj] * ncg), lng, lnb)


def _wmat(w_ref, kind):
    return w_ref[:, kind].reshape(D, D)


def merge_fwd(xin, yah, ybh, ych, proj, wsq, g_post, name):
    tm = min(S, 256)
    nm = len(_row_specs("m_a", tm))

    def body(*refs):
        x_ref, a_ref, b_ref, c_ref = refs[:4]
        ms = [_cat(refs[4 + t * nm:4 + (t + 1) * nm]) for t in range(3)]
        w_ref, g_ref, ya_ref, yb_ref, yc_ref, y_ref, z_ref, o_ref = refs[4 + 3 * nm:]
        y = jnp.zeros((tm, D), f32)
        for t, (h_ref, out_ref) in enumerate(((a_ref, ya_ref), (b_ref, yb_ref), (c_ref, yc_ref))):
            yt = jnp.dot(h_ref[...], _wmat(w_ref, t), preferred_element_type=f32)
            out_ref[...] = yt
            y = y + _sig(ms[t]) * yt
        yb16 = y.astype(bf16)
        y_ref[...] = yb16
        z = jnp.dot(yb16, _wmat(w_ref, 3), preferred_element_type=f32)
        z_ref[...] = z
        r = lax.rsqrt(jnp.mean(z * z, axis=-1, keepdims=True) + RMS_EPS)
        o_ref[...] = x_ref[...] + z * r * g_ref[...]

    row = pl.BlockSpec((tm, D), lambda i: (i, 0))
    sd = lambda dt: jax.ShapeDtypeStruct((S, D), dt)
    return pl.pallas_call(
        body, name=name, grid=(S // tm,),
        out_shape=[sd(f32), sd(f32), sd(f32), sd(bf16), sd(f32), sd(f32)],
        in_specs=[row] * 4 + _row_specs("m_a", tm) + _row_specs("m_b", tm) + _row_specs("m_c", tm)
        + [_full((NDEV, 4, D // NDEV, D)), _full((1, D))],
        out_specs=[row] * 6,
        compiler_params=_cp(("parallel",)))(xin, yah, ybh, ych, *([proj] * (3 * nm)), wsq, g_post)


def loss_fwd(y, target, name):
    tm = min(S, 256)

    def body(y_ref, t_ref, dy_ref, l_ref):
        e = y_ref[...] - t_ref[...]
        dy_ref[...] = e / D

        @pl.when(pl.program_id(0) == 0)
        def _():
            l_ref[...] = jnp.zeros((1, 128), f32)
        l_ref[...] += (0.5 / D) * jnp.sum(e * e)

    row = pl.BlockSpec((tm, D), lambda i: (i, 0))
    return pl.pallas_call(
        body, name=name, grid=(S // tm,),
        out_shape=[jax.ShapeDtypeStruct((S, D), f32), jax.ShapeDtypeStruct((1, 128), f32)],
        in_specs=[row, row], out_specs=[row, _full((1, 128))],
        compiler_params=_cp(("arbitrary",)))(y, target)


def _acc(ref, val):
    @pl.when(pl.program_id(0) == 0)
    def _():
        ref[...] = jnp.zeros(ref.shape, f32)
    ref[...] += val


def merge_bwd(dout, z, ya, yb, yc, proj, wsq, g_post, name):
    tm = min(S, 128)
    nm = len(_row_specs("m_a", tm))

    def body(*refs):
        do_ref, z_ref, ya_ref, yb_ref, yc_ref = refs[:5]
        ms = [_cat(refs[5 + t * nm:5 + (t + 1) * nm]) for t in range(3)]
        w_ref, g_ref = refs[5 + 3 * nm:7 + 3 * nm]
        dh_refs = refs[7 + 3 * nm:10 + 3 * nm]
        dm_refs = refs[10 + 3 * nm:13 + 3 * nm]
        dzb_ref = refs[13 + 3 * nm]
        dyb_refs = refs[14 + 3 * nm:17 + 3 * nm]
        dg_ref = refs[17 + 3 * nm]
        nt = (((1,), (1,)), ((), ()))
        z, dout = z_ref[...], do_ref[...]
        r = lax.rsqrt(jnp.mean(z * z, axis=-1, keepdims=True) + RMS_EPS)
        zh = z * r
        _acc(dg_ref, jnp.sum(dout * zh, axis=0, keepdims=True))
        dzh = dout * g_ref[...]
        dz = (r * (dzh - zh * jnp.mean(dzh * zh, axis=-1, keepdims=True))).astype(bf16)
        dzb_ref[...] = dz
        dy = lax.dot_general(dz, _wmat(w_ref, 3), nt, preferred_element_type=f32)
        for t, yt_ref in enumerate((ya_ref, yb_ref, yc_ref)):
            sg = _sig(ms[t])
            dyt = (dy * sg).astype(bf16)
            dyb_refs[t][...] = dyt
            dm_refs[t][...] = (dy * yt_ref[...] * sg * (1.0 - sg)).astype(bf16)
            dh_refs[t][...] = lax.dot_general(dyt, _wmat(w_ref, t), nt, preferred_element_type=f32)

    row = pl.BlockSpec((tm, D), lambda i: (i, 0))
    sd = lambda dt: jax.ShapeDtypeStruct((S, D), dt)
    return pl.pallas_call(
        body, name=name, grid=(S // tm,),
        out_shape=[sd(f32)] * 3 + [sd(bf16)] * 7 + [jax.ShapeDtypeStruct((1, D), f32)],
        in_specs=[row] * 5 + _row_specs("m_a", tm) + _row_specs("m_b", tm) + _row_specs("m_c", tm)
        + [_full((NDEV, 4, D // NDEV, D)), _full((1, D))],
        out_specs=[row] * 10 + [_full((1, D))],
        compiler_params=_cp(("arbitrary",)))(dout, z, ya, yb, yc, *([proj] * (3 * nm)), wsq, g_post)


def tn_matmul(a, b, name):
    m, n = a.shape[1], b.shape[1]
    tmm = min(m, 512)

    def body(a_ref, b_ref, o_ref):
        o_ref[...] = lax.dot_general(a_ref[...], b_ref[...], (((0,), (0,)), ((), ())), preferred_element_type=f32).astype(bf16)

    return pl.pallas_call(
        body, name=name, grid=(m // tmm,), out_shape=jax.ShapeDtypeStruct((m, n), bf16),
        in_specs=[pl.BlockSpec((S, tmm), lambda i: (0, i)), _full((S, n))],
        out_specs=pl.BlockSpec((tmm, n), lambda i: (i, 0)),
        compiler_params=_cp(("parallel",)))(a, b)


def dwin_parts(h, dproj, name):
    tmm = min(D, 256)

    def body(h_ref, d_ref, o_ref, acc):
        acc[...] = lax.dot_general(h_ref[...], d_ref[...], (((0,), (0,)), ((), ())), preferred_element_type=f32)
        o_ref[0] = acc[:, 0:PSH].astype(bf16)
        o_ref[1] = acc[:, PSH:PAIR].astype(bf16)

    return pl.pallas_call(
        body, name=name, grid=(NDEV // 2, D // tmm), out_shape=jax.ShapeDtypeStruct((NDEV, D, PSH), bf16),
        in_specs=[pl.BlockSpec((S, tmm), lambda p, i: (0, i)), pl.BlockSpec((S, PAIR), lambda p, i: (0, p))],
        out_specs=pl.BlockSpec((2, tmm, PSH), lambda p, i: (p, i, 0)),
        scratch_shapes=[pltpu.VMEM((tmm, PAIR), f32)],
        compiler_params=_cp(("parallel", "arbitrary")))(h, dproj)


def brancha_bwd(dyah, proj, convw, name):
    def body(d_ref, ab, ac, ax, ag, w_ref, dab, dac, dax, dag, dw_ref, padp, padt, accw):
        _fill_pad(padp, 8, lambda rows: ac[rows, :] * ax[rows, :])
        _fill_pad(padt, 8, lambda rows: d_ref[rows, :] * ab[rows, :] * _silu(ag[rows, :]))
        accw[...] = jnp.zeros(accw.shape, f32)
        w = [w_ref[0, k:k + 1, :] for k in range(CA_W)]

        def step(i, carry):
            base = pl.multiple_of(i * RC, RC)
            rows = pl.ds(base, RC)
            ps = [padp[pl.ds(base + 7 + k, RC), :] for k in range(CA_W)]
            t = sum(w[k] * ps[k] for k in range(CA_W))
            dp = sum(w[k] * padt[pl.ds(base + 9 - k, RC), :] for k in range(CA_W))
            d, a_b, a_g = d_ref[rows, :], ab[rows, :], ag[rows, :]
            dab[rows, :] = (d * t * _silu(a_g)).astype(bf16)
            dag[rows, :] = (d * a_b * t * _dsilu(a_g)).astype(bf16)
            dac[rows, :] = (dp * ax[rows, :]).astype(bf16)
            dax[rows, :] = (dp * ac[rows, :]).astype(bf16)
            dt = padt[pl.ds(base + 8, RC), :]
            for k in range(CA_W):
                accw[8 * k:8 * k + 8, :] += jnp.sum((dt * ps[k]).reshape(RC // 8, 8, CT), axis=0)
            return carry
        lax.fori_loop(0, S // RC, step, 0)
        dw_ref[0] = jnp.zeros((8, CT), f32)
        for k in range(CA_W):
            dw_ref[0, k:k + 1, :] = jnp.sum(accw[8 * k:8 * k + 8, :], axis=0, keepdims=True)

    tile = pl.BlockSpec((S, CT), lambda j: (0, j))
    sd = jax.ShapeDtypeStruct((S, D), bf16)
    return pl.pallas_call(
        body, name=name, grid=(D // CT,),
        out_shape=[sd, sd, sd, sd, jax.ShapeDtypeStruct((NDEV, 8, CT), f32)],
        in_specs=[tile, _chan_spec("a_b"), _chan_spec("a_c"), _chan_spec("a_x"), _chan_spec("a_g"),
                  pl.BlockSpec((1, 40, CT), lambda j: (j, 0, 0))],
        out_specs=[tile] * 4 + [pl.BlockSpec((1, 8, CT), lambda j: (j, 0, 0))],
        scratch_shapes=[pltpu.VMEM((S + 16, CT), f32), pltpu.VMEM((S + 16, CT), f32), pltpu.VMEM((8 * CA_W, CT), f32)],
        compiler_params=_cp(("parallel",)))(dyah, proj, proj, proj, proj, convw)


def branchc2_bwd(dych, u1, proj, lng, lnb, name):
    tm = min(S, 256)
    ncg = len(_row_specs("c_g", tm))

    def body(*refs):
        d_ref, u_ref = refs[:2]
        cg = _cat(refs[2:2 + ncg])
        g_ref, b_ref, du_ref, dcg_ref, dlg_ref, dlb_ref, dcb_ref = refs[2 + ncg:]
        d = d_ref[...]
        xh, rstd = _ln_parts(u_ref[...])
        ln = xh * g_ref[...] + b_ref[...]
        dcg_ref[...] = (d * _silu(ln) * _dsilu(cg)).astype(bf16)
        dln = d * _silu(cg) * _dsilu(ln)
        _acc(dlg_ref, jnp.sum(dln * xh, axis=0, keepdims=True))
        _acc(dlb_ref, jnp.sum(dln, axis=0, keepdims=True))
        dxh = dln * g_ref[...]
        du = rstd * (dxh - jnp.mean(dxh, axis=-1, keepdims=True) - xh * jnp.mean(dxh * xh, axis=-1, keepdims=True))
        du_ref[...] = du
        _acc(dcb_ref, jnp.sum(du, axis=0, keepdims=True))

    row = pl.BlockSpec((tm, D), lambda i: (i, 0))
    vec = jax.ShapeDtypeStruct((1, D), f32)
    return pl.pallas_call(
        body, name=name, grid=(S // tm,),
        out_shape=[jax.ShapeDtypeStruct((S, D), f32), jax.ShapeDtypeStruct((S, D), bf16), vec, vec, vec],
        in_specs=[row, row] + _row_specs("c_g", tm) + [_full((1, D)), _full((1, D))],
        out_specs=[row, row, _full((1, D)), _full((1, D)), _full((1, D))],
        compiler_params=_cp(("arbitrary",)))(dych, u1, *([proj] * ncg), lng, lnb)


def branchc1_bwd(du1, proj, convw, name):
    def body(d_ref, cu, cv, w_ref, dcu, dcv, dw_ref, padu, padd, accw):
        _fill_pad(padu, 16, lambda rows: cu[rows, :] * _sig(cv[rows, :]))
        _fill_pad(padd, 16, lambda rows: d_ref[rows, :])
        accw[...] = jnp.zeros(accw.shape, f32)

        def step(i, carry):
            base = pl.multiple_of(i * RC, RC)
            rows = pl.ds(base, RC)
            d = d_ref[rows, :]
            du0 = jnp.zeros((RC, CT), f32)
            for k in range(CC_W):
                du0 = du0 + w_ref[0, 8 + k:9 + k, :] * padd[pl.ds(base + 31 - k, RC), :]
                accw[8 * k:8 * k + 8, :] += jnp.sum((d * padu[pl.ds(base + k + 1, RC), :]).reshape(RC // 8, 8, CT), axis=0)
            sg = _sig(cv[rows, :])
            dcu[rows, :] = (du0 * sg).astype(bf16)
            dcv[rows, :] = (du0 * cu[rows, :] * sg * (1.0 - sg)).astype(bf16)
            return carry
        lax.fori_loop(0, S // RC, step, 0)
        dw_ref[0] = jnp.zeros((32, CT), f32)
        for k in range(CC_W):
            dw_ref[0, k:k + 1, :] = jnp.sum(accw[8 * k:8 * k + 8, :], axis=0, keepdims=True)

    tile = pl.BlockSpec((S, CT), lambda j: (0, j))
    sd = jax.ShapeDtypeStruct((S, D), bf16)
    return pl.pallas_call(
        body, name=name, grid=(D // CT,),
        out_shape=[sd, sd, jax.ShapeDtypeStruct((NDEV, 32, CT), f32)],
        in_specs=[tile, _chan_spec("c_u"), _chan_spec("c_v"), pl.BlockSpec((1, 40, CT), lambda j: (j, 0, 0))],
        out_specs=[tile, tile, pl.BlockSpec((1, 32, CT), lambda j: (j, 0, 0))],
        scratch_shapes=[pltpu.VMEM((S + 32, CT), f32), pltpu.VMEM((S + 32, CT), f32), pltpu.VMEM((8 * 32, CT), f32)],
        compiler_params=_cp(("parallel",)))(du1, proj, proj, convw)


def attn_bwd(dybh, o, qh, kh, vh, proj, name):
    tq = min(S, 256)
    bg_off = _OFF["b_g"][0] // HD

    def body(d_ref, o_ref, q_ref, k_ref, v_ref, bg_ref, dq_ref, dk_ref, dv_ref, dbg_ref):
        @pl.when((pl.program_id(1) == 0) & (pl.program_id(2) == 0))
        def _():
            dk_ref[...] = jnp.zeros(dk_ref.shape, f32)
            dv_ref[...] = jnp.zeros(dv_ref.shape, f32)
        d, bg = d_ref[...], bg_ref[...]
        dbg_ref[...] = (d * o_ref[...] * _dsilu(bg)).astype(bf16)
        do = (d * _silu(bg)).astype(bf16)
        q, k = q_ref[...], k_ref[...]
        pn = _softmax_rows(q, k)
        tn = (((0,), (0,)), ((), ()))
        dv_ref[...] += lax.dot_general(pn.astype(bf16), do, tn, preferred_element_type=f32)
        dp = lax.dot_general(do, v_ref[...], (((1,), (1,)), ((), ())), preferred_element_type=f32)
        ds = (pn * (dp - jnp.sum(pn * dp, axis=-1, keepdims=True)) * (HD ** -0.5)).astype(bf16)
        dq_ref[...] = jnp.dot(ds, k, preferred_element_type=f32)
        dk_ref[...] += lax.dot_general(ds, q, tn, preferred_element_type=f32)

    head = lambda kv, g, i: (i, kv * G + g)
    kvs = pl.BlockSpec((S, HD), lambda kv, g, i: (0, kv))
    return pl.pallas_call(
        body, name=name, grid=(NKV, G, S // tq),
        out_shape=[jax.ShapeDtypeStruct((S, D), f32), jax.ShapeDtypeStruct((S, WKV), f32),
                   jax.ShapeDtypeStruct((S, WKV), f32), jax.ShapeDtypeStruct((S, D), bf16)],
        in_specs=[pl.BlockSpec((tq, HD), head), pl.BlockSpec((tq, HD), head), pl.BlockSpec((tq, HD), head), kvs, kvs,
                  pl.BlockSpec((tq, HD), lambda kv, g, i: (i, bg_off + kv * G + g))],
        out_specs=[pl.BlockSpec((tq, HD), head), kvs, kvs, pl.BlockSpec((tq, HD), head)],
        compiler_params=_cp(("parallel", "arbitrary", "arbitrary")))(dybh, o, qh, kh, vh, proj)


def qkv_bwd(dqh, dkh, dvh, proj, qn, kn, cos, sin, name):
    tm = min(S, 256)
    nq, nk = len(_row_specs("q", tm)), len(_row_specs("k", tm))

    def body(*refs):
        dqh_ref, dkh_ref, dvh_ref = refs[:3]
        q = _cat(refs[3:3 + nq])
        k = _cat(refs[3 + nq:3 + nq + nk])
        qn_ref, kn_ref, cos_ref, sin_ref, dq_ref, dk_ref, dv_ref, dqn_ref, dkn_ref = refs[3 + nq + nk:]
        cos, sin = cos_ref[...], sin_ref[...]

        def heads(xx, dd, gn, out_ref, dgn_ref, n):
            dg = jnp.zeros((1, HD), f32)
            for h in range(n):
                xh = xx[:, h * HD:(h + 1) * HD]
                dh = dd[:, h * HD:(h + 1) * HD]
                r = lax.rsqrt(jnp.mean(xh * xh, axis=-1, keepdims=True) + RMS_EPS)
                xn = xh * r
                dy = dh * cos + _swap32(dh * sin)
                dg = dg + jnp.sum(dy * xn, axis=0, keepdims=True)
                dxn = dy * gn
                out_ref[:, h * HD:(h + 1) * HD] = (r * (dxn - xn * jnp.mean(dxn * xn, axis=-1, keepdims=True))).astype(bf16)
            _acc(dgn_ref, dg)
        heads(q, dqh_ref[...], qn_ref[...], dq_ref, dqn_ref, NQ)
        heads(k, dkh_ref[...], kn_ref[...], dk_ref, dkn_ref, NKV)
        dv_ref[...] = dvh_ref[...].astype(bf16)

    row = lambda w: pl.BlockSpec((tm, w), lambda i: (i, 0))
    vec = jax.ShapeDtypeStruct((1, HD), f32)
    return pl.pallas_call(
        body, name=name, grid=(S // tm,),
        out_shape=[jax.ShapeDtypeStruct((S, D), bf16), jax.ShapeDtypeStruct((S, WKV), bf16),
                   jax.ShapeDtypeStruct((S, WKV), bf16), vec, vec],
        in_specs=[row(D), row(WKV), row(WKV)] + _row_specs("q", tm) + _row_specs("k", tm)
        + [_full((1, HD)), _full((1, HD)), row(HD), row(HD)],
        out_specs=[row(D), row(WKV), row(WKV), _full((1, HD)), _full((1, HD))],
        compiler_params=_cp(("arbitrary",)))(dqh, dkh, dvh, *([proj] * (nq + nk)), qn, kn, cos, sin)


def dh_bwd(dproj, wfull, xin, dout, g_pre, name):
    tm, tk = min(S, 512), 1280
    nk = P // tk

    def body(d_ref, w_ref, x_ref, do_ref, g_ref, dx_ref, dg_ref, acc):
        kk = pl.program_id(1)

        @pl.when(kk == 0)
        def _():
            acc[...] = jnp.zeros(acc.shape, f32)
        acc[...] += lax.dot_general(d_ref[...], w_ref[...], (((1,), (1,)), ((), ())), preferred_element_type=f32)

        @pl.when((kk == 0) & (pl.program_id(0) == 0))
        def _():
            dg_ref[...] = jnp.zeros(dg_ref.shape, f32)

        @pl.when(kk == nk - 1)
        def _():
            x, dh = x_ref[...], acc[...]
            r = lax.rsqrt(jnp.mean(x * x, axis=-1, keepdims=True) + RMS_EPS)
            xn = x * r
            dg_ref[...] += jnp.sum(dh * xn, axis=0, keepdims=True)
            dxn = dh * g_ref[...]
            dx_ref[...] = do_ref[...] + r * (dxn - xn * jnp.mean(dxn * xn, axis=-1, keepdims=True))

    row = pl.BlockSpec((tm, D), lambda i, k: (i, 0))
    return pl.pallas_call(
        body, name=name, grid=(S // tm, nk),
        out_shape=[jax.ShapeDtypeStruct((S, D), f32), jax.ShapeDtypeStruct((1, D), f32)],
        in_specs=[pl.BlockSpec((tm, tk), lambda i, k: (i, k)), pl.BlockSpec((D, tk), lambda i, k: (0, k)), row, row, _full((1, D))],
        out_specs=[row, _full((1, D))],
        scratch_shapes=[pltpu.VMEM((tm, D), f32)],
        compiler_params=_cp(("arbitrary", "arbitrary")))(dproj, wfull, xin, dout, g_pre)


def adam_update(parts, w, m, v, name):
    r, c = w.shape
    tr = r if r <= 128 else 128

    def body(p_ref, w_ref, m_ref, v_ref, g_ref, d_ref, nm_ref, nv_ref):
        g = p_ref[0].astype(f32)
        for s in range(1, NDEV):
            g = g + p_ref[s].astype(f32)
        nm = ADAM_B1 * m_ref[...] + (1.0 - ADAM_B1) * g
        nv = ADAM_B2 * v_ref[...] + (1.0 - ADAM_B2) * (g * g)
        m_hat = nm / (1.0 - ADAM_B1 ** ADAM_STEP)
        v_hat = nv / (1.0 - ADAM_B2 ** ADAM_STEP)
        g_ref[...] = g
        d_ref[...] = -ADAM_LR * (m_hat / (jnp.sqrt(v_hat) + ADAM_EPS) + ADAM_WD * w_ref[...])
        nm_ref[...] = nm
        nv_ref[...] = nv

    blk = pl.BlockSpec((tr, c), lambda i: (i, 0))
    sd = jax.ShapeDtypeStruct((r, c), f32)
    return pl.pallas_call(
        body, name=name, grid=(r // tr,), out_shape=[sd] * 4,
        in_specs=[pl.BlockSpec((NDEV, tr, c), lambda i: (0, i, 0)), blk, blk, blk], out_specs=[blk] * 4,
        compiler_params=_cp(("parallel",)))(parts, w, m, v)


def _rope_tables():
    t = jnp.arange(S)
    rows, cols = (t // GRID_W).astype(f32), (t % GRID_W).astype(f32)
    nf = HD // 4
    inv = ROPE_THETA ** (-jnp.arange(nf, dtype=f32) / nf)
    ar, ac = rows[:, None] * inv, cols[:, None] * inv
    cos = jnp.concatenate([jnp.cos(ar), jnp.cos(ar), jnp.cos(ac), jnp.cos(ac)], axis=1)
    sin = jnp.concatenate([-jnp.sin(ar), jnp.sin(ar), -jnp.sin(ac), jnp.sin(ac)], axis=1)
    return cos, sin


def _pack_conv(ca, cc):
    z = lambda n: jnp.zeros((L, n, CT), f32)
    return jnp.concatenate([ca, z(5), cc, z(1)], axis=1)


def _pack_small(npre, npost, ccb, lng, lnb, qn, kn):
    wide = lambda a: jnp.pad(a, ((0, 0), (0, D - HD)))
    return jnp.stack([npre, npost, ccb, lng, lnb, wide(qn), wide(kn), jnp.zeros((L, D), f32)], axis=1).reshape(L * 8, D)


def kernel(x, norm_pre, norm_post, w_in, conv_a_w, q_norm, k_norm, conv_c_w, conv_c_b, ln_c_g, ln_c_b, w_out_a, w_out_b, w_out_c, w_o, loss_target, m_norm_pre, m_norm_post, m_w_in, m_conv_a_w, m_q_norm, m_k_norm, m_conv_c_w, m_conv_c_b, m_ln_c_g, m_ln_c_b, m_w_out_a, m_w_out_b, m_w_out_c, m_w_o, v_norm_pre, v_norm_post, v_w_in, v_conv_a_w, v_q_norm, v_k_norm, v_conv_c_w, v_conv_c_b, v_ln_c_g, v_ln_c_b, v_w_out_a, v_w_out_b, v_w_out_c, v_w_o):
    cos, sin = _rope_tables()
    rs = D // NDEV
    stack_sq = lambda a, b, c, d: jnp.stack([a, b, c, d], axis=1)
    wsq32 = stack_sq(w_out_a, w_out_b, w_out_c, w_o)
    win_bf = cast_bf16(w_in.reshape(L * D, PSH), "cast_win").reshape(L, D, PSH)
    wsq_bf = cast_bf16(wsq32.reshape(L * 4 * rs, D), "cast_wsq").reshape(L, 4, rs, D)
    conv_pack = _pack_conv(conv_a_w, conv_c_w)
    vec = lambda a, l: a[l][None, :]

    xs, saved = x[0], []
    for l in range(L):
        wg, wsq, convw = all_gather([win_bf[l], wsq_bf[l], conv_pack[l]], f"ag{l}")
        wsq = wsq.reshape(NDEV, 4, rs, D)
        wfull = relayout_win(wg, f"relayout{l}")
        proj, h = proj_fwd(xs, vec(norm_pre, l), wfull, f"proj{l}")
        yah = brancha_fwd(proj, convw, f"bra{l}")
        u1 = branchc1_fwd(proj, convw, vec(conv_c_b, l), f"brc1_{l}")
        qh, kh, vh = qkv_fwd(proj, vec(q_norm, l), vec(k_norm, l), cos, sin, f"qkv{l}")
        o, ybh = attn_fwd(qh, kh, vh, proj, f"attn{l}")
        ych = branchc2_fwd(u1, proj, vec(ln_c_g, l), vec(ln_c_b, l), f"brc2_{l}")
        ya, yb, yc, y16, z, xo = merge_fwd(xs, yah, ybh, ych, proj, wsq, vec(norm_post, l), f"merge{l}")
        saved.append(dict(x=xs, wfull=wfull, wsq=wsq, convw=convw, proj=proj, h=h, yah=yah, ybh=ybh, ych=ych, u1=u1,
                          qh=qh, kh=kh, vh=vh, o=o, ya=ya, yb=yb, yc=yc, y16=y16, z=z))
        xs = xo
    dx, loss_part = loss_fwd(xs, loss_target[0], "loss")
    loss = lax.psum(loss_part[0, 0], ("x", "y", "c"))

    outs = {k: [None] * L for k in ("w_in", "conv_a_w", "conv_c_w", "w_out_a", "w_out_b", "w_out_c", "w_o")}
    small_parts = [None] * L
    for l in reversed(range(L)):
        sv = saved[l]
        proj = sv["proj"]
        (dyah, dybh, dych, dma, dmb, dmc, dzb, dyab, dybb, dycb, dgpost) = merge_bwd(
            dx, sv["z"], sv["ya"], sv["yb"], sv["yc"], proj, sv["wsq"], vec(norm_post, l), f"merge_bwd{l}")
        gsq = [tn_matmul(a, b, f"dwsq{t}_{l}") for t, (a, b) in enumerate(
            ((sv["yah"], dyab), (sv["ybh"], dybb), (sv["ych"], dycb), (sv["y16"], dzb)))]
        dab, dac, dax, dag, gca = brancha_bwd(dyah, proj, sv["convw"], f"bra_bwd{l}")
        du1, dcg, dlg, dlb, dcb = branchc2_bwd(dych, sv["u1"], proj, vec(ln_c_g, l), vec(ln_c_b, l), f"brc2_bwd{l}")
        dcu, dcv, gcc = branchc1_bwd(du1, proj, sv["convw"], f"brc1_bwd{l}")
        dqh, dkh, dvh, dbg = attn_bwd(dybh, sv["o"], sv["qh"], sv["kh"], sv["vh"], proj, f"attn_bwd{l}")
        dq, dk, dv, dqn, dkn = qkv_bwd(dqh, dkh, dvh, proj, vec(q_norm, l), vec(k_norm, l), cos, sin, f"qkv_bwd{l}")
        dproj = jnp.concatenate([dab, dac, dax, dag, dq, dk, dv, dbg, dcu, dcv, dcg, dma, dmb, dmc], axis=1)
        dx, dgpre = dh_bwd(dproj, sv["wfull"], sv["x"], dx, vec(norm_pre, l), f"dh{l}")
        gwin = dwin_parts(sv["h"], dproj, f"dwin{l}")
        gsq_parts = jnp.stack([g.reshape(NDEV, rs, D) for g in gsq], axis=1)
        gconv = jnp.concatenate([gca, gcc], axis=1)
        rwin, rsq, rconv = scatter_parts([gwin, gsq_parts, gconv], f"rs{l}")
        outs["w_in"][l] = adam_update(rwin, w_in[l], m_w_in[l], v_w_in[l], f"adam_win{l}")
        sq = adam_update(rsq.reshape(NDEV, 4 * rs, D), wsq32[l].reshape(4 * rs, D),
                         stack_sq(m_w_out_a, m_w_out_b, m_w_out_c, m_w_o)[l].reshape(4 * rs, D),
                         stack_sq(v_w_out_a, v_w_out_b, v_w_out_c, v_w_o)[l].reshape(4 * rs, D), f"adam_wsq{l}")
        for t, nme in enumerate(("w_out_a", "w_out_b", "w_out_c", "w_o")):
            outs[nme][l] = [a[t * rs:(t + 1) * rs] for a in sq]
        cv = adam_update(rconv, conv_pack[l], _pack_conv(m_conv_a_w, m_conv_c_w)[l], _pack_conv(v_conv_a_w, v_conv_c_w)[l],
                         f"adam_conv{l}")
        outs["conv_a_w"][l] = [a[0:CA_W] for a in cv]
        outs["conv_c_w"][l] = [a[8:8 + CC_W] for a in cv]
        wide = lambda a: jnp.pad(a, ((0, 0), (0, D - HD)))
        small_parts[l] = jnp.concatenate([dgpre, dgpost, dcb, dlg, dlb, wide(dqn), wide(dkn), jnp.zeros((1, D), f32)], axis=0)

    (small_all,) = all_gather([jnp.concatenate(small_parts, axis=0)], "ag_small")
    sm = adam_update(small_all,
                     _pack_small(norm_pre, norm_post, conv_c_b, ln_c_g, ln_c_b, q_norm, k_norm),
                     _pack_small(m_norm_pre, m_norm_post, m_conv_c_b, m_ln_c_g, m_ln_c_b, m_q_norm, m_k_norm),
                     _pack_small(v_norm_pre, v_norm_post, v_conv_c_b, v_ln_c_g, v_ln_c_b, v_q_norm, v_k_norm), "adam_small")
    sm = [a.reshape(L, 8, D) for a in sm]
    small_rows = dict(norm_pre=(0, D), norm_post=(1, D), conv_c_b=(2, D), ln_c_g=(3, D), ln_c_b=(4, D), q_norm=(5, HD), k_norm=(6, HD))

    order = ["norm_pre", "norm_post", "w_in", "conv_a_w", "q_norm", "k_norm", "conv_c_w", "conv_c_b", "ln_c_g", "ln_c_b",
             "w_out_a", "w_out_b", "w_out_c", "w_o"]
    result = [loss, dx[None]]
    for kind in range(4):
        for nme in order:
            if nme in small_rows:
                rw, wd = small_rows[nme]
                result.append(sm[kind][:, rw, :wd])
            else:
                result.append(jnp.stack([outs[nme][l][kind] for l in range(L)], axis=0))
    return tuple(result)
```

```python
import math

import jax
import jax.numpy as jnp
from jax import lax
from jax.experimental import pallas as pl
from jax.experimental.pallas import tpu as pltpu

f32, bf16 = jnp.float32, jnp.bfloat16

D = 1024
S = 2048
L = 4
HD = 128
NQ = D // HD
NKV = NQ // 4
G = NQ // NKV
WKV = NKV * HD
GRID_W = 64
ROPE_THETA = 10000.0
RMS_EPS = 1e-6
LN_EPS = 1e-5
NDEV = 8
CA_W, CC_W = 3, 31
P = 12 * D + 2 * WKV
PSH = P // NDEV
PAIR = 2 * PSH
CT = 128
ADAM_LR, ADAM_B1, ADAM_B2, ADAM_EPS, ADAM_WD, ADAM_STEP = 0.001, 0.9, 0.999, 1e-08, 0.01, 10
VMEM_LIMIT = 56 * 1024 * 1024
MESH = pl.DeviceIdType.MESH

_OFF = {}
_o = 0
for _n, _w in (("a_b", D), ("a_c", D), ("a_x", D), ("a_g", D), ("q", D), ("k", WKV), ("v", WKV), ("b_g", D),
               ("c_u", D), ("c_v", D), ("c_g", D), ("m_a", D), ("m_b", D), ("m_c", D)):
    _OFF[_n] = (_o, _w)
    _o += _w
PIECES = tuple(_OFF)


def _cp(sem=None, **kw):
    return pltpu.CompilerParams(dimension_semantics=sem, vmem_limit_bytes=VMEM_LIMIT, **kw)


def _sig(x):
    return 1.0 / (1.0 + jnp.exp(-x))


def _silu(x):
    return x * _sig(x)


def _dsilu(x):
    s = _sig(x)
    return s * (1.0 + x * (1.0 - s))


def _row_specs(name, tm):
    off, w = _OFF[name]
    bw = math.gcd(off, w) if off else w
    return [pl.BlockSpec((tm, bw), (lambda i, *_, b=off // bw + t: (i, b))) for t in range(w // bw)]


def _cat(refs):
    return refs[0][...] if len(refs) == 1 else jnp.concatenate([r[...] for r in refs], axis=1)


def _chan_spec(name):
    off, _ = _OFF[name]
    return pl.BlockSpec((S, CT), lambda j, b=off // CT: (0, b + j))


def _full(shape):
    return pl.BlockSpec(shape, lambda *_: (0,) * len(shape))


def _coords():
    return lax.axis_index("x"), lax.axis_index("y"), lax.axis_index("c")


def all_gather(shards, name):
    n = len(shards)

    def body(*refs):
        ins, outs = refs[:n], refs[n:2 * n]
        send_sems, recv_sems, local_sems = refs[2 * n:]
        x, y, c = _coords()
        me, sibling = (x, y, c), (x, y, 1 - c)
        chips = [(1 - x, y), (x, 1 - y), (1 - x, 1 - y)]

        def slot(a, p):
            return outs[a].at[4 * p[0] + 2 * p[1] + p[2]]

        def copy(a, k, block, to, src=None):
            return pltpu.make_async_remote_copy(
                src_ref=slot(a, block) if src is None else src, dst_ref=slot(a, block),
                send_sem=send_sems.at[7 * a + k], recv_sem=recv_sems.at[7 * a + k], device_id=to, device_id_type=MESH)

        mine = [pltpu.make_async_copy(ins[a], slot(a, me), local_sems.at[a]) for a in range(n)]
        for cp in mine:
            cp.start()
        first = []
        for a in range(n):
            first.append(copy(a, 0, me, sibling, src=ins[a]))
            first += [copy(a, 1 + j, me, (*chip, c), src=ins[a]) for j, chip in enumerate(chips)]
        for cp in first:
            cp.start()
        passed = []
        for j, chip in enumerate(chips):
            for a in range(n):
                copy(a, 1 + j, (*chip, c), me).wait_recv()
                fw = copy(a, 4 + j, (*chip, c), sibling)
                fw.start()
                passed.append(fw)
        for a in range(n):
            copy(a, 0, sibling, me).wait_recv()
            for j, chip in enumerate(chips):
                copy(a, 4 + j, (*chip, 1 - c), me).wait_recv()
        for cp in first + passed:
            cp.wait_send()
        for cp in mine:
            cp.wait()

    anyspec = pl.BlockSpec(memory_space=pl.ANY)
    return pl.pallas_call(
        body, name=name,
        out_shape=[jax.ShapeDtypeStruct((NDEV,) + s.shape, s.dtype) for s in shards],
        in_specs=[anyspec] * n, out_specs=[anyspec] * n,
        scratch_shapes=[pltpu.SemaphoreType.DMA((7 * n,)), pltpu.SemaphoreType.DMA((7 * n,)), pltpu.SemaphoreType.DMA((n,))],
    )(*shards)


def scatter_parts(parts, name):
    n = len(parts)

    def body(*refs):
        ins, outs = refs[:n], refs[n:2 * n]
        send_sems, recv_sems, local_sems = refs[2 * n:]
        x, y, c = _coords()
        me = 4 * x + 2 * y + c
        mine = [pltpu.make_async_copy(ins[a].at[me], outs[a].at[me], local_sems.at[a]) for a in range(n)]
        for cp in mine:
            cp.start()
        sends = []
        for a in range(n):
            for k in range(1, NDEV):
                kx, ky, kc = (k >> 2) & 1, (k >> 1) & 1, k & 1
                px = 1 - x if kx else x
                py = 1 - y if ky else y
                pc = 1 - c if kc else c
                peer = 4 * px + 2 * py + pc
                sends.append(pltpu.make_async_remote_copy(
                    src_ref=ins[a].at[peer], dst_ref=outs[a].at[me],
                    send_sem=send_sems.at[7 * a + k - 1], recv_sem=recv_sems.at[7 * a + k - 1],
                    device_id=(px, py, pc), device_id_type=MESH))
        for cp in sends:
            cp.start()
        for cp in sends:
            cp.wait_recv()
        for cp in sends:
            cp.wait_send()
        for cp in mine:
            cp.wait()

    anyspec = pl.BlockSpec(memory_space=pl.ANY)
    return pl.pallas_call(
        body, name=name,
        out_shape=[jax.ShapeDtypeStruct(p.shape, p.dtype) for p in parts],
        in_specs=[anyspec] * n, out_specs=[anyspec] * n,
        scratch_shapes=[pltpu.SemaphoreType.DMA((7 * n,)), pltpu.SemaphoreType.DMA((7 * n,)), pltpu.SemaphoreType.DMA((n,))],
    )(*parts)


_HBM = pl.BlockSpec(memory_space=pltpu.HBM)
_SEM = pl.BlockSpec(memory_space=pltpu.SEMAPHORE)
_EFFECT = pltpu.SideEffectType.DATAFLOW_SIDE_EFFECTING


def _scatter_copies(ins, outs, send_sems, recv_sems):
    x, y, c = _coords()
    me = 4 * x + 2 * y + c
    copies = []
    for a in range(len(ins)):
        for k in range(1, NDEV):
            px = 1 - x if (k >> 2) & 1 else x
            py = 1 - y if (k >> 1) & 1 else y
            pc = 1 - c if k & 1 else c
            copies.append(pltpu.make_async_remote_copy(
                src_ref=ins[a].at[4 * px + 2 * py + pc], dst_ref=outs[a].at[me],
                send_sem=send_sems.at[7 * a + k - 1], recv_sem=recv_sems.at[7 * a + k - 1],
                device_id=(px, py, pc), device_id_type=MESH))
    return copies


def scatter_start(parts, name):
    n = len(parts)

    def body(*refs):
        ins, lands = refs[:n], refs[n:2 * n]
        send_sems, recv_sems = refs[2 * n:2 * n + 2]
        token = refs[-1]
        for cp in _scatter_copies(ins, lands, send_sems, recv_sems):
            cp.start()
        token[...] = jnp.zeros(token.shape, f32)

    hbm = lambda p: pltpu.HBM(p.shape, p.dtype)
    res = pl.pallas_call(
        body, name=name,
        out_shape=(pltpu.SemaphoreType.DMA((7 * n,)), pltpu.SemaphoreType.DMA((7 * n,)), *[hbm(p) for p in parts],
                   *[hbm(p) for p in parts], jax.ShapeDtypeStruct((8, 128), f32)),
        in_specs=[_HBM] * (2 * n),
        out_specs=(_SEM, _SEM, *([_HBM] * (2 * n)), pl.BlockSpec(memory_space=pltpu.VMEM)),
        input_output_aliases={i: 2 + i for i in range(2 * n)},
        compiler_params=pltpu.CompilerParams(has_side_effects=_EFFECT),
    )(*[pltpu.with_memory_space_constraint(p, pltpu.HBM) for p in parts],
      *[pltpu.with_memory_space_constraint(lax.empty(p.shape, p.dtype), pltpu.HBM) for p in parts])
    return res[0], res[1], res[2:2 + n], res[2 + n:2 + 2 * n], res[-1]


def scatter_wait(send_sems, recv_sems, parts_thru, lands_thru, after, name):
    n = len(parts_thru)

    def body(*refs):
        ins, lands = refs[:n], refs[n:2 * n]
        s_sems, r_sems = refs[2 * n:2 * n + 2]
        for cp in _scatter_copies(ins, lands, s_sems, r_sems):
            cp.wait_send()
            cp.wait_recv()

    hbm = lambda p: pltpu.HBM(p.shape, p.dtype)
    res = pl.pallas_call(
        body, name=name,
        out_shape=(*[hbm(p) for p in parts_thru], *[hbm(p) for p in lands_thru]),
        in_specs=[_HBM] * (2 * n) + [_SEM, _SEM, pl.BlockSpec(memory_space=pl.ANY)],
        out_specs=[_HBM] * (2 * n),
        input_output_aliases={i: i for i in range(2 * n)},
        compiler_params=pltpu.CompilerParams(has_side_effects=_EFFECT),
    )(*parts_thru, *lands_thru, send_sems, recv_sems, after)
    return res[:n], res[n:]


def cast_bf16(x, name):
    r, c = x.shape
    tr = min(r, 256)

    def body(x_ref, o_ref):
        o_ref[...] = x_ref[...].astype(bf16)

    return pl.pallas_call(
        body, name=name, grid=(r // tr,), out_shape=jax.ShapeDtypeStruct((r, c), bf16),
        in_specs=[pl.BlockSpec((tr, c), lambda i: (i, 0))], out_specs=pl.BlockSpec((tr, c), lambda i: (i, 0)),
        compiler_params=_cp(("parallel",)))(x)


def relayout_win(wg, name):
    tr = min(D, 512)

    def body(w_ref, o_ref):
        o_ref[:, 0:PSH] = w_ref[0]
        o_ref[:, PSH:PAIR] = w_ref[1]

    return pl.pallas_call(
        body, name=name, grid=(NDEV // 2, D // tr), out_shape=jax.ShapeDtypeStruct((D, P), bf16),
        in_specs=[pl.BlockSpec((2, tr, PSH), lambda p, i: (p, i, 0))],
        out_specs=pl.BlockSpec((tr, PAIR), lambda p, i: (i, p)),
        compiler_params=_cp(("parallel", "parallel")))(wg)


def proj_fwd(xin, g_pre, wfull, name):
    tm, tn = min(S, 512), 1280

    def body(x_ref, g_ref, w_ref, proj_ref, h_ref, hs):
        @pl.when(pl.program_id(1) == 0)
        def _():
            x = x_ref[...]
            r = lax.rsqrt(jnp.mean(x * x, axis=-1, keepdims=True) + RMS_EPS)
            h = (x * r * g_ref[...]).astype(bf16)
            hs[...] = h
            h_ref[...] = h
        proj_ref[...] = jnp.dot(hs[...], w_ref[...], preferred_element_type=f32)

    return pl.pallas_call(
        body, name=name, grid=(S // tm, P // tn),
        out_shape=[jax.ShapeDtypeStruct((S, P), f32), jax.ShapeDtypeStruct((S, D), bf16)],
        in_specs=[pl.BlockSpec((tm, D), lambda i, j: (i, 0)), _full((1, D)), pl.BlockSpec((D, tn), lambda i, j: (0, j))],
        out_specs=[pl.BlockSpec((tm, tn), lambda i, j: (i, j)), pl.BlockSpec((tm, D), lambda i, j: (i, 0))],
        scratch_shapes=[pltpu.VMEM((tm, D), bf16)],
        compiler_params=_cp(("parallel", "arbitrary")))(xin, g_pre, wfull)


RC = 128


def _fill_pad(pad, halo, val_fn):
    pad[0:halo, :] = jnp.zeros((halo, CT), f32)
    pad[S + halo:S + 2 * halo, :] = jnp.zeros((halo, CT), f32)

    def step(i, carry):
        rows = pl.ds(pl.multiple_of(i * RC, RC), RC)
        pad[pl.ds(pl.multiple_of(i * RC, RC) + halo, RC), :] = val_fn(rows)
        return carry
    lax.fori_loop(0, S // RC, step, 0)


def brancha_fwd(proj, convw, name):
    def body(ab, ac, ax, ag, w_ref, o_ref, pad):
        _fill_pad(pad, 8, lambda rows: ac[rows, :] * ax[rows, :])
        w = [w_ref[0, k:k + 1, :] for k in range(CA_W)]

        def step(i, carry):
            base = pl.multiple_of(i * RC, RC)
            rows = pl.ds(base, RC)
            t = sum(w[k] * pad[pl.ds(base + 7 + k, RC), :] for k in range(CA_W))
            o_ref[rows, :] = (ab[rows, :] * t * _silu(ag[rows, :])).astype(bf16)
            return carry
        lax.fori_loop(0, S // RC, step, 0)

    return pl.pallas_call(
        body, name=name, grid=(D // CT,), out_shape=jax.ShapeDtypeStruct((S, D), bf16),
        in_specs=[_chan_spec("a_b"), _chan_spec("a_c"), _chan_spec("a_x"), _chan_spec("a_g"),
                  pl.BlockSpec((1, 40, CT), lambda j: (j, 0, 0))],
        out_specs=pl.BlockSpec((S, CT), lambda j: (0, j)),
        scratch_shapes=[pltpu.VMEM((S + 16, CT), f32)],
        compiler_params=_cp(("parallel",)))(proj, proj, proj, proj, convw)


def branchc1_fwd(proj, convw, cbias, name):
    def body(cu, cv, w_ref, b_ref, o_ref, pad):
        _fill_pad(pad, 16, lambda rows: cu[rows, :] * _sig(cv[rows, :]))

        def step(i, carry):
            base = pl.multiple_of(i * RC, RC)
            acc = jnp.zeros((RC, CT), f32) + b_ref[...]
            for k in range(CC_W):
                acc = acc + w_ref[0, 8 + k:9 + k, :] * pad[pl.ds(base + k + 1, RC), :]
            o_ref[pl.ds(base, RC), :] = acc
            return carry
        lax.fori_loop(0, S // RC, step, 0)

    return pl.pallas_call(
        body, name=name, grid=(D // CT,), out_shape=jax.ShapeDtypeStruct((S, D), f32),
        in_specs=[_chan_spec("c_u"), _chan_spec("c_v"), pl.BlockSpec((1, 40, CT), lambda j: (j, 0, 0)),
                  pl.BlockSpec((1, CT), lambda j: (0, j))],
        out_specs=pl.BlockSpec((S, CT), lambda j: (0, j)),
        scratch_shapes=[pltpu.VMEM((S + 32, CT), f32)],
        compiler_params=_cp(("parallel",)))(proj, proj, convw, cbias)


def _swap32(x):
    lane = lax.broadcasted_iota(jnp.int32, x.shape, 1)
    return jnp.where((lane // 32) % 2 == 1, pltpu.roll(x, 32, 1), pltpu.roll(x, HD - 32, 1))


def _rope(y, cos, sin):
    return y * cos + _swap32(y) * sin


def qkv_fwd(proj, qn, kn, cos, sin, name):
    tm = min(S, 256)
    nq, nk, nv = len(_row_specs("q", tm)), len(_row_specs("k", tm)), len(_row_specs("v", tm))

    def body(*refs):
        q = _cat(refs[:nq])
        k = _cat(refs[nq:nq + nk])
        v = _cat(refs[nq + nk:nq + nk + nv])
        qn_ref, kn_ref, cos_ref, sin_ref, qh_ref, kh_ref, vh_ref = refs[nq + nk + nv:]
        cos, sin = cos_ref[...], sin_ref[...]

        def heads(xx, gn, out_ref, n):
            for h in range(n):
                xh = xx[:, h * HD:(h + 1) * HD]
                r = lax.rsqrt(jnp.mean(xh * xh, axis=-1, keepdims=True) + RMS_EPS)
                out_ref[:, h * HD:(h + 1) * HD] = _rope(xh * r * gn, cos, sin).astype(bf16)
        heads(q, qn_ref[...], qh_ref, NQ)
        heads(k, kn_ref[...], kh_ref, NKV)
        vh_ref[...] = v.astype(bf16)

    row = lambda w: pl.BlockSpec((tm, w), lambda i: (i, 0))
    return pl.pallas_call(
        body, name=name, grid=(S // tm,),
        out_shape=[jax.ShapeDtypeStruct((S, D), bf16), jax.ShapeDtypeStruct((S, WKV), bf16), jax.ShapeDtypeStruct((S, WKV), bf16)],
        in_specs=_row_specs("q", tm) + _row_specs("k", tm) + _row_specs("v", tm) + [_full((1, HD)), _full((1, HD)), row(HD), row(HD)],
        out_specs=[row(D), row(WKV), row(WKV)],
        compiler_params=_cp(("parallel",)))(*([proj] * (nq + nk + nv)), qn, kn, cos, sin)


def _softmax_rows(q, k):
    s = lax.dot_general(q, k, (((1,), (1,)), ((), ())), preferred_element_type=f32) * (HD ** -0.5)
    p = jnp.exp(s - jnp.max(s, axis=-1, keepdims=True))
    return p / jnp.sum(p, axis=-1, keepdims=True)


def attn_fwd(qh, kh, vh, proj, name):
    tq = min(S, 256)
    bg_off = _OFF["b_g"][0] // HD

    def body(q_ref, k_ref, v_ref, bg_ref, o_ref, y_ref):
        pn = _softmax_rows(q_ref[...], k_ref[...])
        o = jnp.dot(pn.astype(bf16), v_ref[...], preferred_element_type=f32)
        o_ref[...] = o
        y_ref[...] = (o * _silu(bg_ref[...])).astype(bf16)

    head = lambda kv, g, i: (i, kv * G + g)
    return pl.pallas_call(
        body, name=name, grid=(NKV, G, S // tq),
        out_shape=[jax.ShapeDtypeStruct((S, D), f32), jax.ShapeDtypeStruct((S, D), bf16)],
        in_specs=[pl.BlockSpec((tq, HD), head), pl.BlockSpec((S, HD), lambda kv, g, i: (0, kv)),
                  pl.BlockSpec((S, HD), lambda kv, g, i: (0, kv)),
                  pl.BlockSpec((tq, HD), lambda kv, g, i: (i, bg_off + kv * G + g))],
        out_specs=[pl.BlockSpec((tq, HD), head), pl.BlockSpec((tq, HD), head)],
        compiler_params=_cp(("parallel", "parallel", "parallel")))(qh, kh, vh, proj)


def _ln_parts(u1):
    mu = jnp.mean(u1, axis=-1, keepdims=True)
    xc = u1 - mu
    rstd = lax.rsqrt(jnp.mean(xc * xc, axis=-1, keepdims=True) + LN_EPS)
    return xc * rstd, rstd


def branchc2_fwd(u1, proj, lng, lnb, name):
    tm = min(S, 256)
    ncg = len(_row_specs("c_g", tm))

    def body(*refs):
        u_ref = refs[0]
        cg = _cat(refs[1:1 + ncg])
        g_ref, b_ref, o_ref = refs[1 + ncg:]
        xh, _ = _ln_parts(u_ref[...])
        o_ref[...] = (_silu(xh * g_ref[...] + b_ref[...]) * _silu(cg)).astype(bf16)

    row = pl.BlockSpec((tm, D), lambda i: (i, 0))
    return pl.pallas_call(
        body, name=name, grid=(S // tm,), out_shape=jax.ShapeDtypeStruct((S, D), bf16),
        in_specs=[row] + _row_specs("c_g", tm) + [_full((1, D)), _full((1, D))], out_specs=row,
        compiler_params=_cp(("parallel",)))(u1, *([proj] * ncg), lng, lnb)


def _wmat(w_ref, kind):
    return w_ref[:, kind].reshape(D, D)


def merge_fwd(xin, yah, ybh, ych, proj, wsq, g_post, name):
    tm = min(S, 256)
    nm = len(_row_specs("m_a", tm))

    def body(*refs):
        x_ref, a_ref, b_ref, c_ref = refs[:4]
        ms = [_cat(refs[4 + t * nm:4 + (t + 1) * nm]) for t in range(3)]
        w_ref, g_ref, ya_ref, yb_ref, yc_ref, y_ref, z_ref, o_ref = refs[4 + 3 * nm:]
        y = jnp.zeros((tm, D), f32)
        for t, (h_ref, out_ref) in enumerate(((a_ref, ya_ref), (b_ref, yb_ref), (c_ref, yc_ref))):
            yt = jnp.dot(h_ref[...], _wmat(w_ref, t), preferred_element_type=f32)
            out_ref[...] = yt
            y = y + _sig(ms[t]) * yt
        yb16 = y.astype(bf16)
        y_ref[...] = yb16
        z = jnp.dot(yb16, _wmat(w_ref, 3), preferred_element_type=f32)
        z_ref[...] = z
        r = lax.rsqrt(jnp.mean(z * z, axis=-1, keepdims=True) + RMS_EPS)
        o_ref[...] = x_ref[...] + z * r * g_ref[...]

    row = pl.BlockSpec((tm, D), lambda i: (i, 0))
    sd = lambda dt: jax.ShapeDtypeStruct((S, D), dt)
    return pl.pallas_call(
        body, name=name, grid=(S // tm,),
        out_shape=[sd(f32), sd(f32), sd(f32), sd(bf16), sd(f32), sd(f32)],
        in_specs=[row] * 4 + _row_specs("m_a", tm) + _row_specs("m_b", tm) + _row_specs("m_c", tm)
        + [_full((NDEV, 4, D // NDEV, D)), _full((1, D))],
        out_specs=[row] * 6,
        compiler_params=_cp(("parallel",)))(xin, yah, ybh, ych, *([proj] * (3 * nm)), wsq, g_post)


def loss_fwd(y, target, name):
    tm = min(S, 256)

    def body(y_ref, t_ref, dy_ref, l_ref):
        e = y_ref[...] - t_ref[...]
        dy_ref[...] = e / D

        @pl.when(pl.program_id(0) == 0)
        def _():
            l_ref[...] = jnp.zeros((1, 128), f32)
        l_ref[...] += (0.5 / D) * jnp.sum(e * e)

    row = pl.BlockSpec((tm, D), lambda i: (i, 0))
    return pl.pallas_call(
        body, name=name, grid=(S // tm,),
        out_shape=[jax.ShapeDtypeStruct((S, D), f32), jax.ShapeDtypeStruct((1, 128), f32)],
        in_specs=[row, row], out_specs=[row, _full((1, 128))],
        compiler_params=_cp(("arbitrary",)))(y, target)


def _acc(ref, val):
    @pl.when(pl.program_id(0) == 0)
    def _():
        ref[...] = jnp.zeros(ref.shape, f32)
    ref[...] += val


def merge_bwd(dout, z, ya, yb, yc, proj, wsq, g_post, name):
    tm = min(S, 128)
    nm = len(_row_specs("m_a", tm))

    def body(*refs):
        do_ref, z_ref, ya_ref, yb_ref, yc_ref = refs[:5]
        ms = [_cat(refs[5 + t * nm:5 + (t + 1) * nm]) for t in range(3)]
        w_ref, g_ref = refs[5 + 3 * nm:7 + 3 * nm]
        dh_refs = refs[7 + 3 * nm:10 + 3 * nm]
        dm_refs = refs[10 + 3 * nm:13 + 3 * nm]
        dzb_ref = refs[13 + 3 * nm]
        dyb_refs = refs[14 + 3 * nm:17 + 3 * nm]
        dg_ref = refs[17 + 3 * nm]
        nt = (((1,), (1,)), ((), ()))
        z, dout = z_ref[...], do_ref[...]
        r = lax.rsqrt(jnp.mean(z * z, axis=-1, keepdims=True) + RMS_EPS)
        zh = z * r
        _acc(dg_ref, jnp.sum(dout * zh, axis=0, keepdims=True))
        dzh = dout * g_ref[...]
        dz = (r * (dzh - zh * jnp.mean(dzh * zh, axis=-1, keepdims=True))).astype(bf16)
        dzb_ref[...] = dz
        dy = lax.dot_general(dz, _wmat(w_ref, 3), nt, preferred_element_type=f32)
        for t, yt_ref in enumerate((ya_ref, yb_ref, yc_ref)):
            sg = _sig(ms[t])
            dyt = (dy * sg).astype(bf16)
            dyb_refs[t][...] = dyt
            dm_refs[t][...] = (dy * yt_ref[...] * sg * (1.0 - sg)).astype(bf16)
            dh_refs[t][...] = lax.dot_general(dyt, _wmat(w_ref, t), nt, preferred_element_type=f32)

    row = pl.BlockSpec((tm, D), lambda i: (i, 0))
    sd = lambda dt: jax.ShapeDtypeStruct((S, D), dt)
    return pl.pallas_call(
        body, name=name, grid=(S // tm,),
        out_shape=[sd(f32)] * 3 + [sd(bf16)] * 7 + [jax.ShapeDtypeStruct((1, D), f32)],
        in_specs=[row] * 5 + _row_specs("m_a", tm) + _row_specs("m_b", tm) + _row_specs("m_c", tm)
        + [_full((NDEV, 4, D // NDEV, D)), _full((1, D))],
        out_specs=[row] * 10 + [_full((1, D))],
        compiler_params=_cp(("arbitrary",)))(dout, z, ya, yb, yc, *([proj] * (3 * nm)), wsq, g_post)


def tn_matmul(a, b, name):
    m, n = a.shape[1], b.shape[1]
    tmm = min(m, 512)

    def body(a_ref, b_ref, o_ref):
        o_ref[...] = lax.dot_general(a_ref[...], b_ref[...], (((0,), (0,)), ((), ())), preferred_element_type=f32).astype(bf16)

    return pl.pallas_call(
        body, name=name, grid=(m // tmm,), out_shape=jax.ShapeDtypeStruct((m, n), bf16),
        in_specs=[pl.BlockSpec((S, tmm), lambda i: (0, i)), _full((S, n))],
        out_specs=pl.BlockSpec((tmm, n), lambda i: (i, 0)),
        compiler_params=_cp(("parallel",)))(a, b)


def dwin_parts(h, dproj, name):
    tmm = min(D, 256)

    def body(h_ref, d_ref, o_ref, acc):
        acc[...] = lax.dot_general(h_ref[...], d_ref[...], (((0,), (0,)), ((), ())), preferred_element_type=f32)
        o_ref[0] = acc[:, 0:PSH].astype(bf16)
        o_ref[1] = acc[:, PSH:PAIR].astype(bf16)

    return pl.pallas_call(
        body, name=name, grid=(NDEV // 2, D // tmm), out_shape=jax.ShapeDtypeStruct((NDEV, D, PSH), bf16),
        in_specs=[pl.BlockSpec((S, tmm), lambda p, i: (0, i)), pl.BlockSpec((S, PAIR), lambda p, i: (0, p))],
        out_specs=pl.BlockSpec((2, tmm, PSH), lambda p, i: (p, i, 0)),
        scratch_shapes=[pltpu.VMEM((tmm, PAIR), f32)],
        compiler_params=_cp(("parallel", "arbitrary")))(h, dproj)


def brancha_bwd(dyah, proj, convw, name):
    def body(d_ref, ab, ac, ax, ag, w_ref, dab, dac, dax, dag, dw_ref, padp, padt, accw):
        _fill_pad(padp, 8, lambda rows: ac[rows, :] * ax[rows, :])
        _fill_pad(padt, 8, lambda rows: d_ref[rows, :] * ab[rows, :] * _silu(ag[rows, :]))
        accw[...] = jnp.zeros(accw.shape, f32)
        w = [w_ref[0, k:k + 1, :] for k in range(CA_W)]

        def step(i, carry):
            base = pl.multiple_of(i * RC, RC)
            rows = pl.ds(base, RC)
            ps = [padp[pl.ds(base + 7 + k, RC), :] for k in range(CA_W)]
            t = sum(w[k] * ps[k] for k in range(CA_W))
            dp = sum(w[k] * padt[pl.ds(base + 9 - k, RC), :] for k in range(CA_W))
            d, a_b, a_g = d_ref[rows, :], ab[rows, :], ag[rows, :]
            dab[rows, :] = (d * t * _silu(a_g)).astype(bf16)
            dag[rows, :] = (d * a_b * t * _dsilu(a_g)).astype(bf16)
            dac[rows, :] = (dp * ax[rows, :]).astype(bf16)
            dax[rows, :] = (dp * ac[rows, :]).astype(bf16)
            dt = padt[pl.ds(base + 8, RC), :]
            for k in range(CA_W):
                accw[8 * k:8 * k + 8, :] += jnp.sum((dt * ps[k]).reshape(RC // 8, 8, CT), axis=0)
            return carry
        lax.fori_loop(0, S // RC, step, 0)
        dw_ref[0] = jnp.zeros((8, CT), f32)
        for k in range(CA_W):
            dw_ref[0, k:k + 1, :] = jnp.sum(accw[8 * k:8 * k + 8, :], axis=0, keepdims=True)

    tile = pl.BlockSpec((S, CT), lambda j: (0, j))
    sd = jax.ShapeDtypeStruct((S, D), bf16)
    return pl.pallas_call(
        body, name=name, grid=(D // CT,),
        out_shape=[sd, sd, sd, sd, jax.ShapeDtypeStruct((NDEV, 8, CT), f32)],
        in_specs=[tile, _chan_spec("a_b"), _chan_spec("a_c"), _chan_spec("a_x"), _chan_spec("a_g"),
                  pl.BlockSpec((1, 40, CT), lambda j: (j, 0, 0))],
        out_specs=[tile] * 4 + [pl.BlockSpec((1, 8, CT), lambda j: (j, 0, 0))],
        scratch_shapes=[pltpu.VMEM((S + 16, CT), f32), pltpu.VMEM((S + 16, CT), f32), pltpu.VMEM((8 * CA_W, CT), f32)],
        compiler_params=_cp(("parallel",)))(dyah, proj, proj, proj, proj, convw)


def branchc2_bwd(dych, u1, proj, lng, lnb, name):
    tm = min(S, 256)
    ncg = len(_row_specs("c_g", tm))

    def body(*refs):
        d_ref, u_ref = refs[:2]
        cg = _cat(refs[2:2 + ncg])
        g_ref, b_ref, du_ref, dcg_ref, dlg_ref, dlb_ref, dcb_ref = refs[2 + ncg:]
        d = d_ref[...]
        xh, rstd = _ln_parts(u_ref[...])
        ln = xh * g_ref[...] + b_ref[...]
        dcg_ref[...] = (d * _silu(ln) * _dsilu(cg)).astype(bf16)
        dln = d * _silu(cg) * _dsilu(ln)
        _acc(dlg_ref, jnp.sum(dln * xh, axis=0, keepdims=True))
        _acc(dlb_ref, jnp.sum(dln, axis=0, keepdims=True))
        dxh = dln * g_ref[...]
        du = rstd * (dxh - jnp.mean(dxh, axis=-1, keepdims=True) - xh * jnp.mean(dxh * xh, axis=-1, keepdims=True))
        du_ref[...] = du
        _acc(dcb_ref, jnp.sum(du, axis=0, keepdims=True))

    row = pl.BlockSpec((tm, D), lambda i: (i, 0))
    vec = jax.ShapeDtypeStruct((1, D), f32)
    return pl.pallas_call(
        body, name=name, grid=(S // tm,),
        out_shape=[jax.ShapeDtypeStruct((S, D), f32), jax.ShapeDtypeStruct((S, D), bf16), vec, vec, vec],
        in_specs=[row, row] + _row_specs("c_g", tm) + [_full((1, D)), _full((1, D))],
        out_specs=[row, row, _full((1, D)), _full((1, D)), _full((1, D))],
        compiler_params=_cp(("arbitrary",)))(dych, u1, *([proj] * ncg), lng, lnb)


def branchc1_bwd(du1, proj, convw, name):
    def body(d_ref, cu, cv, w_ref, dcu, dcv, dw_ref, padu, padd, accw):
        _fill_pad(padu, 16, lambda rows: cu[rows, :] * _sig(cv[rows, :]))
        _fill_pad(padd, 16, lambda rows: d_ref[rows, :])
        accw[...] = jnp.zeros(accw.shape, f32)

        def step(i, carry):
            base = pl.multiple_of(i * RC, RC)
            rows = pl.ds(base, RC)
            d = d_ref[rows, :]
            du0 = jnp.zeros((RC, CT), f32)
            for k in range(CC_W):
                du0 = du0 + w_ref[0, 8 + k:9 + k, :] * padd[pl.ds(base + 31 - k, RC), :]
                accw[8 * k:8 * k + 8, :] += jnp.sum((d * padu[pl.ds(base + k + 1, RC), :]).reshape(RC // 8, 8, CT), axis=0)
            sg = _sig(cv[rows, :])
            dcu[rows, :] = (du0 * sg).astype(bf16)
            dcv[rows, :] = (du0 * cu[rows, :] * sg * (1.0 - sg)).astype(bf16)
            return carry
        lax.fori_loop(0, S // RC, step, 0)
        dw_ref[0] = jnp.zeros((32, CT), f32)
        for k in range(CC_W):
            dw_ref[0, k:k + 1, :] = jnp.sum(accw[8 * k:8 * k + 8, :], axis=0, keepdims=True)

    tile = pl.BlockSpec((S, CT), lambda j: (0, j))
    sd = jax.ShapeDtypeStruct((S, D), bf16)
    return pl.pallas_call(
        body, name=name, grid=(D // CT,),
        out_shape=[sd, sd, jax.ShapeDtypeStruct((NDEV, 32, CT), f32)],
        in_specs=[tile, _chan_spec("c_u"), _chan_spec("c_v"), pl.BlockSpec((1, 40, CT), lambda j: (j, 0, 0))],
        out_specs=[tile, tile, pl.BlockSpec((1, 32, CT), lambda j: (j, 0, 0))],
        scratch_shapes=[pltpu.VMEM((S + 32, CT), f32), pltpu.VMEM((S + 32, CT), f32), pltpu.VMEM((8 * 32, CT), f32)],
        compiler_params=_cp(("parallel",)))(du1, proj, proj, convw)


def attn_bwd(dybh, o, qh, kh, vh, proj, name):
    tq = min(S, 256)
    bg_off = _OFF["b_g"][0] // HD

    def body(d_ref, o_ref, q_ref, k_ref, v_ref, bg_ref, dq_ref, dk_ref, dv_ref, dbg_ref):
        @pl.when((pl.program_id(1) == 0) & (pl.program_id(2) == 0))
        def _():
            dk_ref[...] = jnp.zeros(dk_ref.shape, f32)
            dv_ref[...] = jnp.zeros(dv_ref.shape, f32)
        d, bg = d_ref[...], bg_ref[...]
        dbg_ref[...] = (d * o_ref[...] * _dsilu(bg)).astype(bf16)
        do = (d * _silu(bg)).astype(bf16)
        q, k = q_ref[...], k_ref[...]
        pn = _softmax_rows(q, k)
        tn = (((0,), (0,)), ((), ()))
        dv_ref[...] += lax.dot_general(pn.astype(bf16), do, tn, preferred_element_type=f32)
        dp = lax.dot_general(do, v_ref[...], (((1,), (1,)), ((), ())), preferred_element_type=f32)
        ds = (pn * (dp - jnp.sum(pn * dp, axis=-1, keepdims=True)) * (HD ** -0.5)).astype(bf16)
        dq_ref[...] = jnp.dot(ds, k, preferred_element_type=f32)
        dk_ref[...] += lax.dot_general(ds, q, tn, preferred_element_type=f32)

    head = lambda kv, g, i: (i, kv * G + g)
    kvs = pl.BlockSpec((S, HD), lambda kv, g, i: (0, kv))
    return pl.pallas_call(
        body, name=name, grid=(NKV, G, S // tq),
        out_shape=[jax.ShapeDtypeStruct((S, D), f32), jax.ShapeDtypeStruct((S, WKV), f32),
                   jax.ShapeDtypeStruct((S, WKV), f32), jax.ShapeDtypeStruct((S, D), bf16)],
        in_specs=[pl.BlockSpec((tq, HD), head), pl.BlockSpec((tq, HD), head), pl.BlockSpec((tq, HD), head), kvs, kvs,
                  pl.BlockSpec((tq, HD), lambda kv, g, i: (i, bg_off + kv * G + g))],
        out_specs=[pl.BlockSpec((tq, HD), head), kvs, kvs, pl.BlockSpec((tq, HD), head)],
        compiler_params=_cp(("parallel", "arbitrary", "arbitrary")))(dybh, o, qh, kh, vh, proj)


def qkv_bwd(dqh, dkh, dvh, proj, qn, kn, cos, sin, name):
    tm = min(S, 256)
    nq, nk = len(_row_specs("q", tm)), len(_row_specs("k", tm))

    def body(*refs):
        dqh_ref, dkh_ref, dvh_ref = refs[:3]
        q = _cat(refs[3:3 + nq])
        k = _cat(refs[3 + nq:3 + nq + nk])
        qn_ref, kn_ref, cos_ref, sin_ref, dq_ref, dk_ref, dv_ref, dqn_ref, dkn_ref = refs[3 + nq + nk:]
        cos, sin = cos_ref[...], sin_ref[...]

        def heads(xx, dd, gn, out_ref, dgn_ref, n):
            dg = jnp.zeros((1, HD), f32)
            for h in range(n):
                xh = xx[:, h * HD:(h + 1) * HD]
                dh = dd[:, h * HD:(h + 1) * HD]
                r = lax.rsqrt(jnp.mean(xh * xh, axis=-1, keepdims=True) + RMS_EPS)
                xn = xh * r
                dy = dh * cos + _swap32(dh * sin)
                dg = dg + jnp.sum(dy * xn, axis=0, keepdims=True)
                dxn = dy * gn
                out_ref[:, h * HD:(h + 1) * HD] = (r * (dxn - xn * jnp.mean(dxn * xn, axis=-1, keepdims=True))).astype(bf16)
            _acc(dgn_ref, dg)
        heads(q, dqh_ref[...], qn_ref[...], dq_ref, dqn_ref, NQ)
        heads(k, dkh_ref[...], kn_ref[...], dk_ref, dkn_ref, NKV)
        dv_ref[...] = dvh_ref[...].astype(bf16)

    row = lambda w: pl.BlockSpec((tm, w), lambda i: (i, 0))
    vec = jax.ShapeDtypeStruct((1, HD), f32)
    return pl.pallas_call(
        body, name=name, grid=(S // tm,),
        out_shape=[jax.ShapeDtypeStruct((S, D), bf16), jax.ShapeDtypeStruct((S, WKV), bf16),
                   jax.ShapeDtypeStruct((S, WKV), bf16), vec, vec],
        in_specs=[row(D), row(WKV), row(WKV)] + _row_specs("q", tm) + _row_specs("k", tm)
        + [_full((1, HD)), _full((1, HD)), row(HD), row(HD)],
        out_specs=[row(D), row(WKV), row(WKV), _full((1, HD)), _full((1, HD))],
        compiler_params=_cp(("arbitrary",)))(dqh, dkh, dvh, *([proj] * (nq + nk)), qn, kn, cos, sin)


def dh_bwd(dproj, wfull, xin, dout, g_pre, name):
    tm, tk = min(S, 512), 1280
    nk = P // tk

    def body(d_ref, w_ref, x_ref, do_ref, g_ref, dx_ref, dg_ref, acc):
        kk = pl.program_id(1)

        @pl.when(kk == 0)
        def _():
            acc[...] = jnp.zeros(acc.shape, f32)
        acc[...] += lax.dot_general(d_ref[...], w_ref[...], (((1,), (1,)), ((), ())), preferred_element_type=f32)

        @pl.when((kk == 0) & (pl.program_id(0) == 0))
        def _():
            dg_ref[...] = jnp.zeros(dg_ref.shape, f32)

        @pl.when(kk == nk - 1)
        def _():
            x, dh = x_ref[...], acc[...]
            r = lax.rsqrt(jnp.mean(x * x, axis=-1, keepdims=True) + RMS_EPS)
            xn = x * r
            dg_ref[...] += jnp.sum(dh * xn, axis=0, keepdims=True)
            dxn = dh * g_ref[...]
            dx_ref[...] = do_ref[...] + r * (dxn - xn * jnp.mean(dxn * xn, axis=-1, keepdims=True))

    row = pl.BlockSpec((tm, D), lambda i, k: (i, 0))
    return pl.pallas_call(
        body, name=name, grid=(S // tm, nk),
        out_shape=[jax.ShapeDtypeStruct((S, D), f32), jax.ShapeDtypeStruct((1, D), f32)],
        in_specs=[pl.BlockSpec((tm, tk), lambda i, k: (i, k)), pl.BlockSpec((D, tk), lambda i, k: (0, k)), row, row, _full((1, D))],
        out_specs=[row, _full((1, D))],
        scratch_shapes=[pltpu.VMEM((tm, D), f32)],
        compiler_params=_cp(("arbitrary", "arbitrary")))(dproj, wfull, xin, dout, g_pre)


def adam_update(parts, w, m, v, name, own=None, me=None):
    r, c = w.shape
    tr = r if r <= 128 else 128

    def body(*refs):
        if own is None:
            p_ref, w_ref, m_ref, v_ref, g_ref, d_ref, nm_ref, nv_ref = refs
            slot = lambda s: p_ref[s].astype(f32)
        else:
            me_ref, p_ref, own_ref, w_ref, m_ref, v_ref, g_ref, d_ref, nm_ref, nv_ref = refs
            slot = lambda s: jnp.where(me_ref[0] == s, own_ref[0], p_ref[s]).astype(f32)
        g = slot(0)
        for s in range(1, NDEV):
            g = g + slot(s)
        nm = ADAM_B1 * m_ref[...] + (1.0 - ADAM_B1) * g
        nv = ADAM_B2 * v_ref[...] + (1.0 - ADAM_B2) * (g * g)
        m_hat = nm / (1.0 - ADAM_B1 ** ADAM_STEP)
        v_hat = nv / (1.0 - ADAM_B2 ** ADAM_STEP)
        g_ref[...] = g
        d_ref[...] = -ADAM_LR * (m_hat / (jnp.sqrt(v_hat) + ADAM_EPS) + ADAM_WD * w_ref[...])
        nm_ref[...] = nm
        nv_ref[...] = nv

    sd = jax.ShapeDtypeStruct((r, c), f32)
    if own is None:
        blk = pl.BlockSpec((tr, c), lambda i: (i, 0))
        return pl.pallas_call(
            body, name=name, grid=(r // tr,), out_shape=[sd] * 4,
            in_specs=[pl.BlockSpec((NDEV, tr, c), lambda i: (0, i, 0)), blk, blk, blk], out_specs=[blk] * 4,
            compiler_params=_cp(("parallel",)))(parts, w, m, v)
    blk = pl.BlockSpec((tr, c), lambda i, me_ref: (i, 0))
    return pl.pallas_call(
        body, name=name, out_shape=[sd] * 4,
        grid_spec=pltpu.PrefetchScalarGridSpec(
            num_scalar_prefetch=1, grid=(r // tr,),
            in_specs=[pl.BlockSpec((NDEV, tr, c), lambda i, me_ref: (0, i, 0)),
                      pl.BlockSpec((1, tr, c), lambda i, me_ref: (me_ref[0], i, 0)), blk, blk, blk],
            out_specs=[blk] * 4),
        compiler_params=_cp(("parallel",)))(me, parts, own, w, m, v)


def _rope_tables():
    t = jnp.arange(S)
    rows, cols = (t // GRID_W).astype(f32), (t % GRID_W).astype(f32)
    nf = HD // 4
    inv = ROPE_THETA ** (-jnp.arange(nf, dtype=f32) / nf)
    ar, ac = rows[:, None] * inv, cols[:, None] * inv
    cos = jnp.concatenate([jnp.cos(ar), jnp.cos(ar), jnp.cos(ac), jnp.cos(ac)], axis=1)
    sin = jnp.concatenate([-jnp.sin(ar), jnp.sin(ar), -jnp.sin(ac), jnp.sin(ac)], axis=1)
    return cos, sin


def _pack_conv(ca, cc):
    z = lambda n: jnp.zeros((L, n, CT), f32)
    return jnp.concatenate([ca, z(5), cc, z(1)], axis=1)


def _pack_small(npre, npost, ccb, lng, lnb, qn, kn):
    wide = lambda a: jnp.pad(a, ((0, 0), (0, D - HD)))
    return jnp.stack([npre, npost, ccb, lng, lnb, wide(qn), wide(kn), jnp.zeros((L, D), f32)], axis=1).reshape(L * 8, D)


def kernel(x, norm_pre, norm_post, w_in, conv_a_w, q_norm, k_norm, conv_c_w, conv_c_b, ln_c_g, ln_c_b, w_out_a, w_out_b, w_out_c, w_o, loss_target, m_norm_pre, m_norm_post, m_w_in, m_conv_a_w, m_q_norm, m_k_norm, m_conv_c_w, m_conv_c_b, m_ln_c_g, m_ln_c_b, m_w_out_a, m_w_out_b, m_w_out_c, m_w_o, v_norm_pre, v_norm_post, v_w_in, v_conv_a_w, v_q_norm, v_k_norm, v_conv_c_w, v_conv_c_b, v_ln_c_g, v_ln_c_b, v_w_out_a, v_w_out_b, v_w_out_c, v_w_o):
    cos, sin = _rope_tables()
    rs = D // NDEV
    stack_sq = lambda a, b, c, d: jnp.stack([a, b, c, d], axis=1)
    wsq32 = stack_sq(w_out_a, w_out_b, w_out_c, w_o)
    win_bf = cast_bf16(w_in.reshape(L * D, PSH), "cast_win").reshape(L, D, PSH)
    wsq_bf = cast_bf16(wsq32.reshape(L * 4 * rs, D), "cast_wsq").reshape(L, 4, rs, D)
    conv_pack = _pack_conv(conv_a_w, conv_c_w)
    vec = lambda a, l: a[l][None, :]

    xs, saved = x[0], []
    for l in range(L):
        wg, wsq, convw = all_gather([win_bf[l], wsq_bf[l], conv_pack[l]], f"ag{l}")
        wsq = wsq.reshape(NDEV, 4, rs, D)
        wfull = relayout_win(wg, f"relayout{l}")
        proj, h = proj_fwd(xs, vec(norm_pre, l), wfull, f"proj{l}")
        yah = brancha_fwd(proj, convw, f"bra{l}")
        u1 = branchc1_fwd(proj, convw, vec(conv_c_b, l), f"brc1_{l}")
        qh, kh, vh = qkv_fwd(proj, vec(q_norm, l), vec(k_norm, l), cos, sin, f"qkv{l}")
        o, ybh = attn_fwd(qh, kh, vh, proj, f"attn{l}")
        ych = branchc2_fwd(u1, proj, vec(ln_c_g, l), vec(ln_c_b, l), f"brc2_{l}")
        ya, yb, yc, y16, z, xo = merge_fwd(xs, yah, ybh, ych, proj, wsq, vec(norm_post, l), f"merge{l}")
        saved.append(dict(x=xs, wfull=wfull, wsq=wsq, convw=convw, proj=proj, h=h, yah=yah, ybh=ybh, ych=ych, u1=u1,
                          qh=qh, kh=kh, vh=vh, o=o, ya=ya, yb=yb, yc=yc, y16=y16, z=z))
        xs = xo
    dx, loss_part = loss_fwd(xs, loss_target[0], "loss")
    loss = lax.psum(loss_part[0, 0], ("x", "y", "c"))

    outs = {k: [None] * L for k in ("w_in", "conv_a_w", "conv_c_w", "w_out_a", "w_out_b", "w_out_c", "w_o")}
    small_parts = [None] * L
    me = (4 * lax.axis_index("x") + 2 * lax.axis_index("y") + lax.axis_index("c")).astype(jnp.int32).reshape(1)
    msq32 = stack_sq(m_w_out_a, m_w_out_b, m_w_out_c, m_w_o)
    vsq32 = stack_sq(v_w_out_a, v_w_out_b, v_w_out_c, v_w_o)
    mconv, vconv = _pack_conv(m_conv_a_w, m_conv_c_w), _pack_conv(v_conv_a_w, v_conv_c_w)

    def finish(l, started, after):
        (s1, r1, p1, z1), (s2, r2, p2, z2) = started
        (gsq_own,), (rsq,) = scatter_wait(s1, r1, p1, z1, after, f"rs_sq_wait{l}")
        (gwin_own, gconv_own), (rwin, rconv) = scatter_wait(s2, r2, p2, z2, after, f"rs_win_wait{l}")
        outs["w_in"][l] = adam_update(rwin, w_in[l], m_w_in[l], v_w_in[l], f"adam_win{l}", own=gwin_own, me=me)
        sq = adam_update(rsq.reshape(NDEV, 4 * rs, D), wsq32[l].reshape(4 * rs, D), msq32[l].reshape(4 * rs, D),
                         vsq32[l].reshape(4 * rs, D), f"adam_wsq{l}", own=gsq_own.reshape(NDEV, 4 * rs, D), me=me)
        for t, nme in enumerate(("w_out_a", "w_out_b", "w_out_c", "w_o")):
            outs[nme][l] = [a[t * rs:(t + 1) * rs] for a in sq]
        cv = adam_update(rconv, conv_pack[l], mconv[l], vconv[l], f"adam_conv{l}", own=gconv_own, me=me)
        outs["conv_a_w"][l] = [a[0:CA_W] for a in cv]
        outs["conv_c_w"][l] = [a[8:8 + CC_W] for a in cv]

    pending = None
    for l in reversed(range(L)):
        sv = saved[l]
        proj = sv["proj"]
        (dyah, dybh, dych, dma, dmb, dmc, dzb, dyab, dybb, dycb, dgpost) = merge_bwd(
            dx, sv["z"], sv["ya"], sv["yb"], sv["yc"], proj, sv["wsq"], vec(norm_post, l), f"merge_bwd{l}")
        gsq = [tn_matmul(a, b, f"dwsq{t}_{l}") for t, (a, b) in enumerate(
            ((sv["yah"], dyab), (sv["ybh"], dybb), (sv["ych"], dycb), (sv["y16"], dzb)))]
        gsq_parts = jnp.stack([g.reshape(NDEV, rs, D) for g in gsq], axis=1)
        s1, r1, p1, z1, tok1 = scatter_start([gsq_parts], f"rs_sq_start{l}")
        convw = sv["convw"] + tok1[0, 0]
        dab, dac, dax, dag, gca = brancha_bwd(dyah, proj, convw, f"bra_bwd{l}")
        du1, dcg, dlg, dlb, dcb = branchc2_bwd(dych, sv["u1"], proj, vec(ln_c_g, l), vec(ln_c_b, l), f"brc2_bwd{l}")
        dcu, dcv, gcc = branchc1_bwd(du1, proj, convw, f"brc1_bwd{l}")
        dqh, dkh, dvh, dbg = attn_bwd(dybh, sv["o"], sv["qh"], sv["kh"], sv["vh"], proj, f"attn_bwd{l}")
        dq, dk, dv, dqn, dkn = qkv_bwd(dqh, dkh, dvh, proj, vec(q_norm, l), vec(k_norm, l), cos, sin, f"qkv_bwd{l}")
        dproj = jnp.concatenate([dab, dac, dax, dag, dq, dk, dv, dbg, dcu, dcv, dcg, dma, dmb, dmc], axis=1)
        gwin = dwin_parts(sv["h"], dproj, f"dwin{l}")
        gconv = jnp.concatenate([gca, gcc], axis=1)
        s2, r2, p2, z2, tok2 = scatter_start([gwin, gconv], f"rs_win_start{l}")
        dx, dgpre = dh_bwd(dproj, sv["wfull"], sv["x"], dx, vec(norm_pre, l) + tok2[0, 0], f"dh{l}")
        wide = lambda a: jnp.pad(a, ((0, 0), (0, D - HD)))
        small_parts[l] = jnp.concatenate([dgpre, dgpost, dcb, dlg, dlb, wide(dqn), wide(dkn), jnp.zeros((1, D), f32)], axis=0)
        if pending is not None:
            finish(*pending, after=dx)
        pending = (l, ((s1, r1, p1, z1), (s2, r2, p2, z2)))
    finish(*pending, after=dx)

    (small_all,) = all_gather([jnp.concatenate(small_parts, axis=0)], "ag_small")
    sm = adam_update(small_all,
                     _pack_small(norm_pre, norm_post, conv_c_b, ln_c_g, ln_c_b, q_norm, k_norm),
                     _pack_small(m_norm_pre, m_norm_post, m_conv_c_b, m_ln_c_g, m_ln_c_b, m_q_norm, m_k_norm),
                     _pack_small(v_norm_pre, v_norm_post, v_conv_c_b, v_ln_c_g, v_ln_c_b, v_q_norm, v_k_norm), "adam_small")
    sm = [a.reshape(L, 8, D) for a in sm]
    small_rows = dict(norm_pre=(0, D), norm_post=(1, D), conv_c_b=(2, D), ln_c_g=(3, D), ln_c_b=(4, D), q_norm=(5, HD), k_norm=(6, HD))

    order = ["norm_pre", "norm_post", "w_in", "conv_a_w", "q_norm", "k_norm", "conv_c_w", "conv_c_b", "ln_c_g", "ln_c_b",
             "w_out_a", "w_out_b", "w_out_c", "w_o"]
    result = [loss, dx[None]]
    for kind in range(4):
        for nme in order:
            if nme in small_rows:
                rw, wd = small_rows[nme]
                result.append(sm[kind][:, rw, :wd])
            else:
                result.append(jnp.stack([outs[nme][l][kind] for l in range(L)], axis=0))
    return tuple(result)
```

```python
import math

import jax
import jax.numpy as jnp
from jax import lax
from jax.experimental import pallas as pl
from jax.experimental.pallas import tpu as pltpu

f32, bf16 = jnp.float32, jnp.bfloat16

D = 1024
S = 2048
L = 4
HD = 128
NQ = D // HD
NKV = NQ // 4
G = NQ // NKV
WKV = NKV * HD
GRID_W = 64
ROPE_THETA = 10000.0
RMS_EPS = 1e-6
LN_EPS = 1e-5
NDEV = 8
CA_W, CC_W = 3, 31
P = 12 * D + 2 * WKV
PSH = P // NDEV
PAIR = 2 * PSH
CT = 128
ADAM_LR, ADAM_B1, ADAM_B2, ADAM_EPS, ADAM_WD, ADAM_STEP = 0.001, 0.9, 0.999, 1e-08, 0.01, 10
VMEM_LIMIT = 56 * 1024 * 1024
MESH = pl.DeviceIdType.MESH

_OFF = {}
_o = 0
for _n, _w in (("a_b", D), ("a_c", D), ("a_x", D), ("a_g", D), ("q", D), ("k", WKV), ("v", WKV), ("b_g", D),
               ("c_u", D), ("c_v", D), ("c_g", D), ("m_a", D), ("m_b", D), ("m_c", D)):
    _OFF[_n] = (_o, _w)
    _o += _w
PIECES = tuple(_OFF)


def _cp(sem=None, **kw):
    return pltpu.CompilerParams(dimension_semantics=sem, vmem_limit_bytes=VMEM_LIMIT, **kw)


def _sig(x):
    return 1.0 / (1.0 + jnp.exp(-x))


def _silu(x):
    return x * _sig(x)


def _dsilu(x):
    s = _sig(x)
    return s * (1.0 + x * (1.0 - s))


def _row_specs(name, tm):
    off, w = _OFF[name]
    bw = math.gcd(off, w) if off else w
    return [pl.BlockSpec((tm, bw), (lambda i, *_, b=off // bw + t: (i, b))) for t in range(w // bw)]


def _cat(refs):
    return refs[0][...] if len(refs) == 1 else jnp.concatenate([r[...] for r in refs], axis=1)


def _chan_spec(name):
    off, _ = _OFF[name]
    return pl.BlockSpec((S, CT), lambda j, b=off // CT: (0, b + j))


def _full(shape):
    return pl.BlockSpec(shape, lambda *_: (0,) * len(shape))


def _coords():
    return lax.axis_index("x"), lax.axis_index("y"), lax.axis_index("c")


def all_gather(shards, name):
    n = len(shards)

    def body(*refs):
        ins, outs = refs[:n], refs[n:2 * n]
        send_sems, recv_sems, local_sems = refs[2 * n:]
        x, y, c = _coords()
        me, sibling = (x, y, c), (x, y, 1 - c)
        chips = [(1 - x, y), (x, 1 - y), (1 - x, 1 - y)]

        def slot(a, p):
            return outs[a].at[4 * p[0] + 2 * p[1] + p[2]]

        def copy(a, k, block, to, src=None):
            return pltpu.make_async_remote_copy(
                src_ref=slot(a, block) if src is None else src, dst_ref=slot(a, block),
                send_sem=send_sems.at[7 * a + k], recv_sem=recv_sems.at[7 * a + k], device_id=to, device_id_type=MESH)

        mine = [pltpu.make_async_copy(ins[a], slot(a, me), local_sems.at[a]) for a in range(n)]
        for cp in mine:
            cp.start()
        first = []
        for a in range(n):
            first.append(copy(a, 0, me, sibling, src=ins[a]))
            first += [copy(a, 1 + j, me, (*chip, c), src=ins[a]) for j, chip in enumerate(chips)]
        for cp in first:
            cp.start()
        passed = []
        for j, chip in enumerate(chips):
            for a in range(n):
                copy(a, 1 + j, (*chip, c), me).wait_recv()
                fw = copy(a, 4 + j, (*chip, c), sibling)
                fw.start()
                passed.append(fw)
        for a in range(n):
            copy(a, 0, sibling, me).wait_recv()
            for j, chip in enumerate(chips):
                copy(a, 4 + j, (*chip, 1 - c), me).wait_recv()
        for cp in first + passed:
            cp.wait_send()
        for cp in mine:
            cp.wait()

    anyspec = pl.BlockSpec(memory_space=pl.ANY)
    return pl.pallas_call(
        body, name=name,
        out_shape=[jax.ShapeDtypeStruct((NDEV,) + s.shape, s.dtype) for s in shards],
        in_specs=[anyspec] * n, out_specs=[anyspec] * n,
        scratch_shapes=[pltpu.SemaphoreType.DMA((7 * n,)), pltpu.SemaphoreType.DMA((7 * n,)), pltpu.SemaphoreType.DMA((n,))],
    )(*shards)


def scatter_parts(parts, name):
    n = len(parts)

    def body(*refs):
        ins, outs = refs[:n], refs[n:2 * n]
        send_sems, recv_sems, local_sems = refs[2 * n:]
        x, y, c = _coords()
        me = 4 * x + 2 * y + c
        mine = [pltpu.make_async_copy(ins[a].at[me], outs[a].at[me], local_sems.at[a]) for a in range(n)]
        for cp in mine:
            cp.start()
        sends = []
        for a in range(n):
            for k in range(1, NDEV):
                kx, ky, kc = (k >> 2) & 1, (k >> 1) & 1, k & 1
                px = 1 - x if kx else x
                py = 1 - y if ky else y
                pc = 1 - c if kc else c
                peer = 4 * px + 2 * py + pc
                sends.append(pltpu.make_async_remote_copy(
                    src_ref=ins[a].at[peer], dst_ref=outs[a].at[me],
                    send_sem=send_sems.at[7 * a + k - 1], recv_sem=recv_sems.at[7 * a + k - 1],
                    device_id=(px, py, pc), device_id_type=MESH))
        for cp in sends:
            cp.start()
        for cp in sends:
            cp.wait_recv()
        for cp in sends:
            cp.wait_send()
        for cp in mine:
            cp.wait()

    anyspec = pl.BlockSpec(memory_space=pl.ANY)
    return pl.pallas_call(
        body, name=name,
        out_shape=[jax.ShapeDtypeStruct(p.shape, p.dtype) for p in parts],
        in_specs=[anyspec] * n, out_specs=[anyspec] * n,
        scratch_shapes=[pltpu.SemaphoreType.DMA((7 * n,)), pltpu.SemaphoreType.DMA((7 * n,)), pltpu.SemaphoreType.DMA((n,))],
    )(*parts)


_HBM = pl.BlockSpec(memory_space=pltpu.HBM)
_SEM = pl.BlockSpec(memory_space=pltpu.SEMAPHORE)
_EFFECT = pltpu.SideEffectType.DATAFLOW_SIDE_EFFECTING


def _scatter_copies(ins, outs, send_sems, recv_sems):
    x, y, c = _coords()
    me = 4 * x + 2 * y + c
    copies = []
    for a in range(len(ins)):
        for k in range(1, NDEV):
            px = 1 - x if (k >> 2) & 1 else x
            py = 1 - y if (k >> 1) & 1 else y
            pc = 1 - c if k & 1 else c
            copies.append(pltpu.make_async_remote_copy(
                src_ref=ins[a].at[4 * px + 2 * py + pc], dst_ref=outs[a].at[me],
                send_sem=send_sems.at[7 * a + k - 1], recv_sem=recv_sems.at[7 * a + k - 1],
                device_id=(px, py, pc), device_id_type=MESH))
    return copies


def scatter_start(parts, name):
    n = len(parts)

    def body(*refs):
        ins, lands = refs[:n], refs[n:2 * n]
        send_sems, recv_sems = refs[2 * n:2 * n + 2]
        token = refs[-1]
        for cp in _scatter_copies(ins, lands, send_sems, recv_sems):
            cp.start()
        token[...] = jnp.zeros(token.shape, f32)

    hbm = lambda p: pltpu.HBM(p.shape, p.dtype)
    res = pl.pallas_call(
        body, name=name,
        out_shape=(pltpu.SemaphoreType.DMA((7 * n,)), pltpu.SemaphoreType.DMA((7 * n,)), *[hbm(p) for p in parts],
                   *[hbm(p) for p in parts], jax.ShapeDtypeStruct((8, 128), f32)),
        in_specs=[_HBM] * (2 * n),
        out_specs=(_SEM, _SEM, *([_HBM] * (2 * n)), pl.BlockSpec(memory_space=pltpu.VMEM)),
        input_output_aliases={i: 2 + i for i in range(2 * n)},
        compiler_params=pltpu.CompilerParams(has_side_effects=_EFFECT),
    )(*[pltpu.with_memory_space_constraint(p, pltpu.HBM) for p in parts],
      *[pltpu.with_memory_space_constraint(lax.empty(p.shape, p.dtype), pltpu.HBM) for p in parts])
    return res[0], res[1], res[2:2 + n], res[2 + n:2 + 2 * n], res[-1]


def scatter_wait(send_sems, recv_sems, parts_thru, lands_thru, after, name):
    n = len(parts_thru)

    def body(*refs):
        ins, lands = refs[:n], refs[n:2 * n]
        s_sems, r_sems = refs[2 * n:2 * n + 2]
        for cp in _scatter_copies(ins, lands, s_sems, r_sems):
            cp.wait_send()
            cp.wait_recv()

    hbm = lambda p: pltpu.HBM(p.shape, p.dtype)
    res = pl.pallas_call(
        body, name=name,
        out_shape=(*[hbm(p) for p in parts_thru], *[hbm(p) for p in lands_thru]),
        in_specs=[_HBM] * (2 * n) + [_SEM, _SEM, pl.BlockSpec(memory_space=pl.ANY)],
        out_specs=[_HBM] * (2 * n),
        input_output_aliases={i: i for i in range(2 * n)},
        compiler_params=pltpu.CompilerParams(has_side_effects=_EFFECT),
    )(*parts_thru, *lands_thru, send_sems, recv_sems, after)
    return res[:n], res[n:]


def _gather_copies(ins, lands, send_sems, recv_sems):
    x, y, c = _coords()
    targets = [(x, y, 1 - c), (1 - x, y, c), (x, 1 - y, c), (1 - x, 1 - y, c)]
    return [pltpu.make_async_remote_copy(
        src_ref=ins[a], dst_ref=lands[a].at[4 * x + 2 * y + c], send_sem=send_sems.at[4 * a + k],
        recv_sem=recv_sems.at[4 * a + k], device_id=to, device_id_type=MESH)
        for a in range(len(ins)) for k, to in enumerate(targets)]


def gather_start(shards, name):
    n = len(shards)

    def body(*refs):
        ins, lands = refs[:n], refs[n:2 * n]
        send_sems, recv_sems = refs[2 * n:2 * n + 2]
        token = refs[-1]
        for cp in _gather_copies(ins, lands, send_sems, recv_sems):
            cp.start()
        token[...] = jnp.zeros(token.shape, f32)

    land = lambda s: pltpu.HBM((NDEV,) + s.shape, s.dtype)
    res = pl.pallas_call(
        body, name=name,
        out_shape=(pltpu.SemaphoreType.DMA((4 * n,)), pltpu.SemaphoreType.DMA((4 * n,)),
                   *[pltpu.HBM(s.shape, s.dtype) for s in shards], *[land(s) for s in shards], jax.ShapeDtypeStruct((8, 128), f32)),
        in_specs=[_HBM] * (2 * n),
        out_specs=(_SEM, _SEM, *([_HBM] * (2 * n)), pl.BlockSpec(memory_space=pltpu.VMEM)),
        input_output_aliases={i: 2 + i for i in range(2 * n)},
        compiler_params=pltpu.CompilerParams(has_side_effects=_EFFECT),
    )(*[pltpu.with_memory_space_constraint(s, pltpu.HBM) for s in shards],
      *[pltpu.with_memory_space_constraint(lax.empty((NDEV,) + s.shape, s.dtype), pltpu.HBM) for s in shards])
    return res[0], res[1], res[2:2 + n], res[2 + n:2 + 2 * n], res[-1]


def gather_wait(send_sems, recv_sems, shards_thru, lands_thru, after, name):
    n = len(shards_thru)

    def body(*refs):
        ins, lands = refs[:n], refs[n:2 * n]
        s_sems, r_sems = refs[2 * n:2 * n + 2]
        for cp in _gather_copies(ins, lands, s_sems, r_sems):
            cp.wait_send()
            cp.wait_recv()

    res = pl.pallas_call(
        body, name=name,
        out_shape=(*[pltpu.HBM(s.shape, s.dtype) for s in shards_thru], *[pltpu.HBM(z.shape, z.dtype) for z in lands_thru]),
        in_specs=[_HBM] * (2 * n) + [_SEM, _SEM, pl.BlockSpec(memory_space=pl.ANY)],
        out_specs=[_HBM] * (2 * n),
        input_output_aliases={i: i for i in range(2 * n)},
        compiler_params=pltpu.CompilerParams(has_side_effects=_EFFECT),
    )(*shards_thru, *lands_thru, send_sems, recv_sems, after)
    return res[:n], res[n:]


def gather_forward(shards, lands, name):
    n = len(shards)

    def body(*refs):
        ins, outs = refs[:n], refs[2 * n:3 * n]
        send_sems, recv_sems, local_sems = refs[3 * n:]
        x, y, c = _coords()
        chips = [(1 - x, y), (x, 1 - y), (1 - x, 1 - y)]
        mine = [pltpu.make_async_copy(ins[a], outs[a].at[4 * x + 2 * y + c], local_sems.at[a]) for a in range(n)]
        for cp in mine:
            cp.start()

        def fwd(a, j, core):
            blk = outs[a].at[4 * chips[j][0] + 2 * chips[j][1] + core]
            return pltpu.make_async_remote_copy(src_ref=blk, dst_ref=blk, send_sem=send_sems.at[3 * a + j],
                                                recv_sem=recv_sems.at[3 * a + j], device_id=(x, y, 1 - c), device_id_type=MESH)
        sends = [fwd(a, j, c) for a in range(n) for j in range(3)]
        for cp in sends:
            cp.start()
        for a in range(n):
            for j in range(3):
                fwd(a, j, 1 - c).wait_recv()
        for cp in sends:
            cp.wait_send()
        for cp in mine:
            cp.wait()

    anyspec = pl.BlockSpec(memory_space=pl.ANY)
    return pl.pallas_call(
        body, name=name,
        out_shape=[jax.ShapeDtypeStruct(z.shape, z.dtype) for z in lands],
        in_specs=[anyspec] * (2 * n), out_specs=[anyspec] * n,
        input_output_aliases={n + a: a for a in range(n)},
        scratch_shapes=[pltpu.SemaphoreType.DMA((3 * n,)), pltpu.SemaphoreType.DMA((3 * n,)), pltpu.SemaphoreType.DMA((n,))],
    )(*shards, *lands)


def cast_bf16(x, name):
    r, c = x.shape
    tr = min(r, 256)

    def body(x_ref, o_ref):
        o_ref[...] = x_ref[...].astype(bf16)

    return pl.pallas_call(
        body, name=name, grid=(r // tr,), out_shape=jax.ShapeDtypeStruct((r, c), bf16),
        in_specs=[pl.BlockSpec((tr, c), lambda i: (i, 0))], out_specs=pl.BlockSpec((tr, c), lambda i: (i, 0)),
        compiler_params=_cp(("parallel",)))(x)


def relayout_win(wg, name):
    tr = min(D, 512)

    def body(w_ref, o_ref):
        o_ref[:, 0:PSH] = w_ref[0]
        o_ref[:, PSH:PAIR] = w_ref[1]

    return pl.pallas_call(
        body, name=name, grid=(NDEV // 2, D // tr), out_shape=jax.ShapeDtypeStruct((D, P), bf16),
        in_specs=[pl.BlockSpec((2, tr, PSH), lambda p, i: (p, i, 0))],
        out_specs=pl.BlockSpec((tr, PAIR), lambda p, i: (i, p)),
        compiler_params=_cp(("parallel", "parallel")))(wg)


def proj_fwd(xin, g_pre, wfull, name):
    tm, tn = min(S, 512), 1280

    def body(x_ref, g_ref, w_ref, proj_ref, h_ref, hs):
        @pl.when(pl.program_id(1) == 0)
        def _():
            x = x_ref[...]
            r = lax.rsqrt(jnp.mean(x * x, axis=-1, keepdims=True) + RMS_EPS)
            h = (x * r * g_ref[...]).astype(bf16)
            hs[...] = h
            h_ref[...] = h
        proj_ref[...] = jnp.dot(hs[...], w_ref[...], preferred_element_type=f32)

    return pl.pallas_call(
        body, name=name, grid=(S // tm, P // tn),
        out_shape=[jax.ShapeDtypeStruct((S, P), f32), jax.ShapeDtypeStruct((S, D), bf16)],
        in_specs=[pl.BlockSpec((tm, D), lambda i, j: (i, 0)), _full((1, D)), pl.BlockSpec((D, tn), lambda i, j: (0, j))],
        out_specs=[pl.BlockSpec((tm, tn), lambda i, j: (i, j)), pl.BlockSpec((tm, D), lambda i, j: (i, 0))],
        scratch_shapes=[pltpu.VMEM((tm, D), bf16)],
        compiler_params=_cp(("parallel", "arbitrary")))(xin, g_pre, wfull)


RC = 128


def _fill_pad(pad, halo, val_fn):
    pad[0:halo, :] = jnp.zeros((halo, CT), f32)
    pad[S + halo:S + 2 * halo, :] = jnp.zeros((halo, CT), f32)

    def step(i, carry):
        rows = pl.ds(pl.multiple_of(i * RC, RC), RC)
        pad[pl.ds(pl.multiple_of(i * RC, RC) + halo, RC), :] = val_fn(rows)
        return carry
    lax.fori_loop(0, S // RC, step, 0)


def brancha_fwd(proj, convw, name):
    def body(ab, ac, ax, ag, w_ref, o_ref, pad):
        _fill_pad(pad, 8, lambda rows: ac[rows, :] * ax[rows, :])
        w = [w_ref[0, k:k + 1, :] for k in range(CA_W)]

        def step(i, carry):
            base = pl.multiple_of(i * RC, RC)
            rows = pl.ds(base, RC)
            t = sum(w[k] * pad[pl.ds(base + 7 + k, RC), :] for k in range(CA_W))
            o_ref[rows, :] = (ab[rows, :] * t * _silu(ag[rows, :])).astype(bf16)
            return carry
        lax.fori_loop(0, S // RC, step, 0)

    return pl.pallas_call(
        body, name=name, grid=(D // CT,), out_shape=jax.ShapeDtypeStruct((S, D), bf16),
        in_specs=[_chan_spec("a_b"), _chan_spec("a_c"), _chan_spec("a_x"), _chan_spec("a_g"),
                  pl.BlockSpec((1, 40, CT), lambda j: (j, 0, 0))],
        out_specs=pl.BlockSpec((S, CT), lambda j: (0, j)),
        scratch_shapes=[pltpu.VMEM((S + 16, CT), f32)],
        compiler_params=_cp(("parallel",)))(proj, proj, proj, proj, convw)


def branchc1_fwd(proj, convw, cbias, name):
    def body(cu, cv, w_ref, b_ref, o_ref, pad):
        _fill_pad(pad, 16, lambda rows: cu[rows, :] * _sig(cv[rows, :]))

        def step(i, carry):
            base = pl.multiple_of(i * RC, RC)
            acc = jnp.zeros((RC, CT), f32) + b_ref[...]
            for k in range(CC_W):
                acc = acc + w_ref[0, 8 + k:9 + k, :] * pad[pl.ds(base + k + 1, RC), :]
            o_ref[pl.ds(base, RC), :] = acc
            return carry
        lax.fori_loop(0, S // RC, step, 0)

    return pl.pallas_call(
        body, name=name, grid=(D // CT,), out_shape=jax.ShapeDtypeStruct((S, D), f32),
        in_specs=[_chan_spec("c_u"), _chan_spec("c_v"), pl.BlockSpec((1, 40, CT), lambda j: (j, 0, 0)),
                  pl.BlockSpec((1, CT), lambda j: (0, j))],
        out_specs=pl.BlockSpec((S, CT), lambda j: (0, j)),
        scratch_shapes=[pltpu.VMEM((S + 32, CT), f32)],
        compiler_params=_cp(("parallel",)))(proj, proj, convw, cbias)


def _swap32(x):
    lane = lax.broadcasted_iota(jnp.int32, x.shape, 1)
    return jnp.where((lane // 32) % 2 == 1, pltpu.roll(x, 32, 1), pltpu.roll(x, HD - 32, 1))


def _rope(y, cos, sin):
    return y * cos + _swap32(y) * sin


def qkv_fwd(proj, qn, kn, cos, sin, name):
    tm = min(S, 256)
    nq, nk, nv = len(_row_specs("q", tm)), len(_row_specs("k", tm)), len(_row_specs("v", tm))

    def body(*refs):
        q = _cat(refs[:nq])
        k = _cat(refs[nq:nq + nk])
        v = _cat(refs[nq + nk:nq + nk + nv])
        qn_ref, kn_ref, cos_ref, sin_ref, qh_ref, kh_ref, vh_ref = refs[nq + nk + nv:]
        cos, sin = cos_ref[...], sin_ref[...]

        def heads(xx, gn, out_ref, n):
            for h in range(n):
                xh = xx[:, h * HD:(h + 1) * HD]
                r = lax.rsqrt(jnp.mean(xh * xh, axis=-1, keepdims=True) + RMS_EPS)
                out_ref[:, h * HD:(h + 1) * HD] = _rope(xh * r * gn, cos, sin).astype(bf16)
        heads(q, qn_ref[...], qh_ref, NQ)
        heads(k, kn_ref[...], kh_ref, NKV)
        vh_ref[...] = v.astype(bf16)

    row = lambda w: pl.BlockSpec((tm, w), lambda i: (i, 0))
    return pl.pallas_call(
        body, name=name, grid=(S // tm,),
        out_shape=[jax.ShapeDtypeStruct((S, D), bf16), jax.ShapeDtypeStruct((S, WKV), bf16), jax.ShapeDtypeStruct((S, WKV), bf16)],
        in_specs=_row_specs("q", tm) + _row_specs("k", tm) + _row_specs("v", tm) + [_full((1, HD)), _full((1, HD)), row(HD), row(HD)],
        out_specs=[row(D), row(WKV), row(WKV)],
        compiler_params=_cp(("parallel",)))(*([proj] * (nq + nk + nv)), qn, kn, cos, sin)


def _softmax_rows(q, k):
    s = lax.dot_general(q, k, (((1,), (1,)), ((), ())), preferred_element_type=f32) * (HD ** -0.5)
    p = jnp.exp(s - jnp.max(s, axis=-1, keepdims=True))
    return p / jnp.sum(p, axis=-1, keepdims=True)


def attn_fwd(qh, kh, vh, proj, name):
    tq = min(S, 256)
    bg_off = _OFF["b_g"][0] // HD

    def body(q_ref, k_ref, v_ref, bg_ref, o_ref, y_ref):
        pn = _softmax_rows(q_ref[...], k_ref[...])
        o = jnp.dot(pn.astype(bf16), v_ref[...], preferred_element_type=f32)
        o_ref[...] = o
        y_ref[...] = (o * _silu(bg_ref[...])).astype(bf16)

    head = lambda kv, g, i: (i, kv * G + g)
    return pl.pallas_call(
        body, name=name, grid=(NKV, G, S // tq),
        out_shape=[jax.ShapeDtypeStruct((S, D), f32), jax.ShapeDtypeStruct((S, D), bf16)],
        in_specs=[pl.BlockSpec((tq, HD), head), pl.BlockSpec((S, HD), lambda kv, g, i: (0, kv)),
                  pl.BlockSpec((S, HD), lambda kv, g, i: (0, kv)),
                  pl.BlockSpec((tq, HD), lambda kv, g, i: (i, bg_off + kv * G + g))],
        out_specs=[pl.BlockSpec((tq, HD), head), pl.BlockSpec((tq, HD), head)],
        compiler_params=_cp(("parallel", "parallel", "parallel")))(qh, kh, vh, proj)


def _ln_parts(u1):
    mu = jnp.mean(u1, axis=-1, keepdims=True)
    xc = u1 - mu
    rstd = lax.rsqrt(jnp.mean(xc * xc, axis=-1, keepdims=True) + LN_EPS)
    return xc * rstd, rstd


def branchc2_fwd(u1, proj, lng, lnb, name):
    tm = min(S, 256)
    ncg = len(_row_specs("c_g", tm))

    def body(*refs):
        u_ref = refs[0]
        cg = _cat(refs[1:1 + ncg])
        g_ref, b_ref, o_ref = refs[1 + ncg:]
        xh, _ = _ln_parts(u_ref[...])
        o_ref[...] = (_silu(xh * g_ref[...] + b_ref[...]) * _silu(cg)).astype(bf16)

    row = pl.BlockSpec((tm, D), lambda i: (i, 0))
    return pl.pallas_call(
        body, name=name, grid=(S // tm,), out_shape=jax.ShapeDtypeStruct((S, D), bf16),
        in_specs=[row] + _row_specs("c_g", tm) + [_full((1, D)), _full((1, D))], out_specs=row,
        compiler_params=_cp(("parallel",)))(u1, *([proj] * ncg), lng, lnb)


def _wmat(w_ref, kind):
    return w_ref[:, kind].reshape(D, D)


def merge_fwd(xin, yah, ybh, ych, proj, wsq, g_post, name):
    tm = min(S, 256)
    nm = len(_row_specs("m_a", tm))

    def body(*refs):
        x_ref, a_ref, b_ref, c_ref = refs[:4]
        ms = [_cat(refs[4 + t * nm:4 + (t + 1) * nm]) for t in range(3)]
        w_ref, g_ref, ya_ref, yb_ref, yc_ref, y_ref, z_ref, o_ref = refs[4 + 3 * nm:]
        y = jnp.zeros((tm, D), f32)
        for t, (h_ref, out_ref) in enumerate(((a_ref, ya_ref), (b_ref, yb_ref), (c_ref, yc_ref))):
            yt = jnp.dot(h_ref[...], _wmat(w_ref, t), preferred_element_type=f32)
            out_ref[...] = yt
            y = y + _sig(ms[t]) * yt
        yb16 = y.astype(bf16)
        y_ref[...] = yb16
        z = jnp.dot(yb16, _wmat(w_ref, 3), preferred_element_type=f32)
        z_ref[...] = z
        r = lax.rsqrt(jnp.mean(z * z, axis=-1, keepdims=True) + RMS_EPS)
        o_ref[...] = x_ref[...] + z * r * g_ref[...]

    row = pl.BlockSpec((tm, D), lambda i: (i, 0))
    sd = lambda dt: jax.ShapeDtypeStruct((S, D), dt)
    return pl.pallas_call(
        body, name=name, grid=(S // tm,),
        out_shape=[sd(f32), sd(f32), sd(f32), sd(bf16), sd(f32), sd(f32)],
        in_specs=[row] * 4 + _row_specs("m_a", tm) + _row_specs("m_b", tm) + _row_specs("m_c", tm)
        + [_full((NDEV, 4, D // NDEV, D)), _full((1, D))],
        out_specs=[row] * 6,
        compiler_params=_cp(("parallel",)))(xin, yah, ybh, ych, *([proj] * (3 * nm)), wsq, g_post)


def loss_fwd(y, target, name):
    tm = min(S, 256)

    def body(y_ref, t_ref, dy_ref, l_ref):
        e = y_ref[...] - t_ref[...]
        dy_ref[...] = e / D

        @pl.when(pl.program_id(0) == 0)
        def _():
            l_ref[...] = jnp.zeros((1, 128), f32)
        l_ref[...] += (0.5 / D) * jnp.sum(e * e)

    row = pl.BlockSpec((tm, D), lambda i: (i, 0))
    return pl.pallas_call(
        body, name=name, grid=(S // tm,),
        out_shape=[jax.ShapeDtypeStruct((S, D), f32), jax.ShapeDtypeStruct((1, 128), f32)],
        in_specs=[row, row], out_specs=[row, _full((1, 128))],
        compiler_params=_cp(("arbitrary",)))(y, target)


def _acc(ref, val):
    @pl.when(pl.program_id(0) == 0)
    def _():
        ref[...] = jnp.zeros(ref.shape, f32)
    ref[...] += val


def merge_bwd(dout, z, ya, yb, yc, proj, wsq, g_post, name):
    tm = min(S, 128)
    nm = len(_row_specs("m_a", tm))

    def body(*refs):
        do_ref, z_ref, ya_ref, yb_ref, yc_ref = refs[:5]
        ms = [_cat(refs[5 + t * nm:5 + (t + 1) * nm]) for t in range(3)]
        w_ref, g_ref = refs[5 + 3 * nm:7 + 3 * nm]
        dh_refs = refs[7 + 3 * nm:10 + 3 * nm]
        dm_refs = refs[10 + 3 * nm:13 + 3 * nm]
        dzb_ref = refs[13 + 3 * nm]
        dyb_refs = refs[14 + 3 * nm:17 + 3 * nm]
        dg_ref = refs[17 + 3 * nm]
        nt = (((1,), (1,)), ((), ()))
        z, dout = z_ref[...], do_ref[...]
        r = lax.rsqrt(jnp.mean(z * z, axis=-1, keepdims=True) + RMS_EPS)
        zh = z * r
        _acc(dg_ref, jnp.sum(dout * zh, axis=0, keepdims=True))
        dzh = dout * g_ref[...]
        dz = (r * (dzh - zh * jnp.mean(dzh * zh, axis=-1, keepdims=True))).astype(bf16)
        dzb_ref[...] = dz
        dy = lax.dot_general(dz, _wmat(w_ref, 3), nt, preferred_element_type=f32)
        for t, yt_ref in enumerate((ya_ref, yb_ref, yc_ref)):
            sg = _sig(ms[t])
            dyt = (dy * sg).astype(bf16)
            dyb_refs[t][...] = dyt
            dm_refs[t][...] = (dy * yt_ref[...] * sg * (1.0 - sg)).astype(bf16)
            dh_refs[t][...] = lax.dot_general(dyt, _wmat(w_ref, t), nt, preferred_element_type=f32)

    row = pl.BlockSpec((tm, D), lambda i: (i, 0))
    sd = lambda dt: jax.ShapeDtypeStruct((S, D), dt)
    return pl.pallas_call(
        body, name=name, grid=(S // tm,),
        out_shape=[sd(f32)] * 3 + [sd(bf16)] * 7 + [jax.ShapeDtypeStruct((1, D), f32)],
        in_specs=[row] * 5 + _row_specs("m_a", tm) + _row_specs("m_b", tm) + _row_specs("m_c", tm)
        + [_full((NDEV, 4, D // NDEV, D)), _full((1, D))],
        out_specs=[row] * 10 + [_full((1, D))],
        compiler_params=_cp(("arbitrary",)))(dout, z, ya, yb, yc, *([proj] * (3 * nm)), wsq, g_post)


def tn_matmul(a, b, name):
    m, n = a.shape[1], b.shape[1]
    tmm = min(m, 512)

    def body(a_ref, b_ref, o_ref):
        o_ref[...] = lax.dot_general(a_ref[...], b_ref[...], (((0,), (0,)), ((), ())), preferred_element_type=f32).astype(bf16)

    return pl.pallas_call(
        body, name=name, grid=(m // tmm,), out_shape=jax.ShapeDtypeStruct((m, n), bf16),
        in_specs=[pl.BlockSpec((S, tmm), lambda i: (0, i)), _full((S, n))],
        out_specs=pl.BlockSpec((tmm, n), lambda i: (i, 0)),
        compiler_params=_cp(("parallel",)))(a, b)


def dwin_parts(h, dproj, name):
    tmm = min(D, 256)

    def body(h_ref, d_ref, o_ref, acc):
        acc[...] = lax.dot_general(h_ref[...], d_ref[...], (((0,), (0,)), ((), ())), preferred_element_type=f32)
        o_ref[0] = acc[:, 0:PSH].astype(bf16)
        o_ref[1] = acc[:, PSH:PAIR].astype(bf16)

    return pl.pallas_call(
        body, name=name, grid=(NDEV // 2, D // tmm), out_shape=jax.ShapeDtypeStruct((NDEV, D, PSH), bf16),
        in_specs=[pl.BlockSpec((S, tmm), lambda p, i: (0, i)), pl.BlockSpec((S, PAIR), lambda p, i: (0, p))],
        out_specs=pl.BlockSpec((2, tmm, PSH), lambda p, i: (p, i, 0)),
        scratch_shapes=[pltpu.VMEM((tmm, PAIR), f32)],
        compiler_params=_cp(("parallel", "arbitrary")))(h, dproj)


def brancha_bwd(dyah, proj, convw, name):
    def body(d_ref, ab, ac, ax, ag, w_ref, dab, dac, dax, dag, dw_ref, padp, padt, accw):
        _fill_pad(padp, 8, lambda rows: ac[rows, :] * ax[rows, :])
        _fill_pad(padt, 8, lambda rows: d_ref[rows, :] * ab[rows, :] * _silu(ag[rows, :]))
        accw[...] = jnp.zeros(accw.shape, f32)
        w = [w_ref[0, k:k + 1, :] for k in range(CA_W)]

        def step(i, carry):
            base = pl.multiple_of(i * RC, RC)
            rows = pl.ds(base, RC)
            ps = [padp[pl.ds(base + 7 + k, RC), :] for k in range(CA_W)]
            t = sum(w[k] * ps[k] for k in range(CA_W))
            dp = sum(w[k] * padt[pl.ds(base + 9 - k, RC), :] for k in range(CA_W))
            d, a_b, a_g = d_ref[rows, :], ab[rows, :], ag[rows, :]
            dab[rows, :] = (d * t * _silu(a_g)).astype(bf16)
            dag[rows, :] = (d * a_b * t * _dsilu(a_g)).astype(bf16)
            dac[rows, :] = (dp * ax[rows, :]).astype(bf16)
            dax[rows, :] = (dp * ac[rows, :]).astype(bf16)
            dt = padt[pl.ds(base + 8, RC), :]
            for k in range(CA_W):
                accw[8 * k:8 * k + 8, :] += jnp.sum((dt * ps[k]).reshape(RC // 8, 8, CT), axis=0)
            return carry
        lax.fori_loop(0, S // RC, step, 0)
        dw_ref[0] = jnp.zeros((8, CT), f32)
        for k in range(CA_W):
            dw_ref[0, k:k + 1, :] = jnp.sum(accw[8 * k:8 * k + 8, :], axis=0, keepdims=True)

    tile = pl.BlockSpec((S, CT), lambda j: (0, j))
    sd = jax.ShapeDtypeStruct((S, D), bf16)
    return pl.pallas_call(
        body, name=name, grid=(D // CT,),
        out_shape=[sd, sd, sd, sd, jax.ShapeDtypeStruct((NDEV, 8, CT), f32)],
        in_specs=[tile, _chan_spec("a_b"), _chan_spec("a_c"), _chan_spec("a_x"), _chan_spec("a_g"),
                  pl.BlockSpec((1, 40, CT), lambda j: (j, 0, 0))],
        out_specs=[tile] * 4 + [pl.BlockSpec((1, 8, CT), lambda j: (j, 0, 0))],
        scratch_shapes=[pltpu.VMEM((S + 16, CT), f32), pltpu.VMEM((S + 16, CT), f32), pltpu.VMEM((8 * CA_W, CT), f32)],
        compiler_params=_cp(("parallel",)))(dyah, proj, proj, proj, proj, convw)


def branchc2_bwd(dych, u1, proj, lng, lnb, name):
    tm = min(S, 256)
    ncg = len(_row_specs("c_g", tm))

    def body(*refs):
        d_ref, u_ref = refs[:2]
        cg = _cat(refs[2:2 + ncg])
        g_ref, b_ref, du_ref, dcg_ref, dlg_ref, dlb_ref, dcb_ref = refs[2 + ncg:]
        d = d_ref[...]
        xh, rstd = _ln_parts(u_ref[...])
        ln = xh * g_ref[...] + b_ref[...]
        dcg_ref[...] = (d * _silu(ln) * _dsilu(cg)).astype(bf16)
        dln = d * _silu(cg) * _dsilu(ln)
        _acc(dlg_ref, jnp.sum(dln * xh, axis=0, keepdims=True))
        _acc(dlb_ref, jnp.sum(dln, axis=0, keepdims=True))
        dxh = dln * g_ref[...]
        du = rstd * (dxh - jnp.mean(dxh, axis=-1, keepdims=True) - xh * jnp.mean(dxh * xh, axis=-1, keepdims=True))
        du_ref[...] = du
        _acc(dcb_ref, jnp.sum(du, axis=0, keepdims=True))

    row = pl.BlockSpec((tm, D), lambda i: (i, 0))
    vec = jax.ShapeDtypeStruct((1, D), f32)
    return pl.pallas_call(
        body, name=name, grid=(S // tm,),
        out_shape=[jax.ShapeDtypeStruct((S, D), f32), jax.ShapeDtypeStruct((S, D), bf16), vec, vec, vec],
        in_specs=[row, row] + _row_specs("c_g", tm) + [_full((1, D)), _full((1, D))],
        out_specs=[row, row, _full((1, D)), _full((1, D)), _full((1, D))],
        compiler_params=_cp(("arbitrary",)))(dych, u1, *([proj] * ncg), lng, lnb)


def branchc1_bwd(du1, proj, convw, name):
    def body(d_ref, cu, cv, w_ref, dcu, dcv, dw_ref, padu, padd, accw):
        _fill_pad(padu, 16, lambda rows: cu[rows, :] * _sig(cv[rows, :]))
        _fill_pad(padd, 16, lambda rows: d_ref[rows, :])
        accw[...] = jnp.zeros(accw.shape, f32)

        def step(i, carry):
            base = pl.multiple_of(i * RC, RC)
            rows = pl.ds(base, RC)
            d = d_ref[rows, :]
            du0 = jnp.zeros((RC, CT), f32)
            for k in range(CC_W):
                du0 = du0 + w_ref[0, 8 + k:9 + k, :] * padd[pl.ds(base + 31 - k, RC), :]
                accw[8 * k:8 * k + 8, :] += jnp.sum((d * padu[pl.ds(base + k + 1, RC), :]).reshape(RC // 8, 8, CT), axis=0)
            sg = _sig(cv[rows, :])
            dcu[rows, :] = (du0 * sg).astype(bf16)
            dcv[rows, :] = (du0 * cu[rows, :] * sg * (1.0 - sg)).astype(bf16)
            return carry
        lax.fori_loop(0, S // RC, step, 0)
        dw_ref[0] = jnp.zeros((32, CT), f32)
        for k in range(CC_W):
            dw_ref[0, k:k + 1, :] = jnp.sum(accw[8 * k:8 * k + 8, :], axis=0, keepdims=True)

    tile = pl.BlockSpec((S, CT), lambda j: (0, j))
    sd = jax.ShapeDtypeStruct((S, D), bf16)
    return pl.pallas_call(
        body, name=name, grid=(D // CT,),
        out_shape=[sd, sd, jax.ShapeDtypeStruct((NDEV, 32, CT), f32)],
        in_specs=[tile, _chan_spec("c_u"), _chan_spec("c_v"), pl.BlockSpec((1, 40, CT), lambda j: (j, 0, 0))],
        out_specs=[tile, tile, pl.BlockSpec((1, 32, CT), lambda j: (j, 0, 0))],
        scratch_shapes=[pltpu.VMEM((S + 32, CT), f32), pltpu.VMEM((S + 32, CT), f32), pltpu.VMEM((8 * 32, CT), f32)],
        compiler_params=_cp(("parallel",)))(du1, proj, proj, convw)


def attn_bwd(dybh, o, qh, kh, vh, proj, name):
    tq = min(S, 256)
    bg_off = _OFF["b_g"][0] // HD

    def body(d_ref, o_ref, q_ref, k_ref, v_ref, bg_ref, dq_ref, dk_ref, dv_ref, dbg_ref):
        @pl.when((pl.program_id(1) == 0) & (pl.program_id(2) == 0))
        def _():
            dk_ref[...] = jnp.zeros(dk_ref.shape, f32)
            dv_ref[...] = jnp.zeros(dv_ref.shape, f32)
        d, bg = d_ref[...], bg_ref[...]
        dbg_ref[...] = (d * o_ref[...] * _dsilu(bg)).astype(bf16)
        do = (d * _silu(bg)).astype(bf16)
        q, k = q_ref[...], k_ref[...]
        pn = _softmax_rows(q, k)
        tn = (((0,), (0,)), ((), ()))
        dv_ref[...] += lax.dot_general(pn.astype(bf16), do, tn, preferred_element_type=f32)
        dp = lax.dot_general(do, v_ref[...], (((1,), (1,)), ((), ())), preferred_element_type=f32)
        ds = (pn * (dp - jnp.sum(pn * dp, axis=-1, keepdims=True)) * (HD ** -0.5)).astype(bf16)
        dq_ref[...] = jnp.dot(ds, k, preferred_element_type=f32)
        dk_ref[...] += lax.dot_general(ds, q, tn, preferred_element_type=f32)

    head = lambda kv, g, i: (i, kv * G + g)
    kvs = pl.BlockSpec((S, HD), lambda kv, g, i: (0, kv))
    return pl.pallas_call(
        body, name=name, grid=(NKV, G, S // tq),
        out_shape=[jax.ShapeDtypeStruct((S, D), f32), jax.ShapeDtypeStruct((S, WKV), f32),
                   jax.ShapeDtypeStruct((S, WKV), f32), jax.ShapeDtypeStruct((S, D), bf16)],
        in_specs=[pl.BlockSpec((tq, HD), head), pl.BlockSpec((tq, HD), head), pl.BlockSpec((tq, HD), head), kvs, kvs,
                  pl.BlockSpec((tq, HD), lambda kv, g, i: (i, bg_off + kv * G + g))],
        out_specs=[pl.BlockSpec((tq, HD), head), kvs, kvs, pl.BlockSpec((tq, HD), head)],
        compiler_params=_cp(("parallel", "arbitrary", "arbitrary")))(dybh, o, qh, kh, vh, proj)


def qkv_bwd(dqh, dkh, dvh, proj, qn, kn, cos, sin, name):
    tm = min(S, 256)
    nq, nk = len(_row_specs("q", tm)), len(_row_specs("k", tm))

    def body(*refs):
        dqh_ref, dkh_ref, dvh_ref = refs[:3]
        q = _cat(refs[3:3 + nq])
        k = _cat(refs[3 + nq:3 + nq + nk])
        qn_ref, kn_ref, cos_ref, sin_ref, dq_ref, dk_ref, dv_ref, dqn_ref, dkn_ref = refs[3 + nq + nk:]
        cos, sin = cos_ref[...], sin_ref[...]

        def heads(xx, dd, gn, out_ref, dgn_ref, n):
            dg = jnp.zeros((1, HD), f32)
            for h in range(n):
                xh = xx[:, h * HD:(h + 1) * HD]
                dh = dd[:, h * HD:(h + 1) * HD]
                r = lax.rsqrt(jnp.mean(xh * xh, axis=-1, keepdims=True) + RMS_EPS)
                xn = xh * r
                dy = dh * cos + _swap32(dh * sin)
                dg = dg + jnp.sum(dy * xn, axis=0, keepdims=True)
                dxn = dy * gn
                out_ref[:, h * HD:(h + 1) * HD] = (r * (dxn - xn * jnp.mean(dxn * xn, axis=-1, keepdims=True))).astype(bf16)
            _acc(dgn_ref, dg)
        heads(q, dqh_ref[...], qn_ref[...], dq_ref, dqn_ref, NQ)
        heads(k, dkh_ref[...], kn_ref[...], dk_ref, dkn_ref, NKV)
        dv_ref[...] = dvh_ref[...].astype(bf16)

    row = lambda w: pl.BlockSpec((tm, w), lambda i: (i, 0))
    vec = jax.ShapeDtypeStruct((1, HD), f32)
    return pl.pallas_call(
        body, name=name, grid=(S // tm,),
        out_shape=[jax.ShapeDtypeStruct((S, D), bf16), jax.ShapeDtypeStruct((S, WKV), bf16),
                   jax.ShapeDtypeStruct((S, WKV), bf16), vec, vec],
        in_specs=[row(D), row(WKV), row(WKV)] + _row_specs("q", tm) + _row_specs("k", tm)
        + [_full((1, HD)), _full((1, HD)), row(HD), row(HD)],
        out_specs=[row(D), row(WKV), row(WKV), _full((1, HD)), _full((1, HD))],
        compiler_params=_cp(("arbitrary",)))(dqh, dkh, dvh, *([proj] * (nq + nk)), qn, kn, cos, sin)


def dh_bwd(dproj, wfull, xin, dout, g_pre, name):
    tm, tk = min(S, 512), 1280
    nk = P // tk

    def body(d_ref, w_ref, x_ref, do_ref, g_ref, dx_ref, dg_ref, acc):
        kk = pl.program_id(1)

        @pl.when(kk == 0)
        def _():
            acc[...] = jnp.zeros(acc.shape, f32)
        acc[...] += lax.dot_general(d_ref[...], w_ref[...], (((1,), (1,)), ((), ())), preferred_element_type=f32)

        @pl.when((kk == 0) & (pl.program_id(0) == 0))
        def _():
            dg_ref[...] = jnp.zeros(dg_ref.shape, f32)

        @pl.when(kk == nk - 1)
        def _():
            x, dh = x_ref[...], acc[...]
            r = lax.rsqrt(jnp.mean(x * x, axis=-1, keepdims=True) + RMS_EPS)
            xn = x * r
            dg_ref[...] += jnp.sum(dh * xn, axis=0, keepdims=True)
            dxn = dh * g_ref[...]
            dx_ref[...] = do_ref[...] + r * (dxn - xn * jnp.mean(dxn * xn, axis=-1, keepdims=True))

    row = pl.BlockSpec((tm, D), lambda i, k: (i, 0))
    return pl.pallas_call(
        body, name=name, grid=(S // tm, nk),
        out_shape=[jax.ShapeDtypeStruct((S, D), f32), jax.ShapeDtypeStruct((1, D), f32)],
        in_specs=[pl.BlockSpec((tm, tk), lambda i, k: (i, k)), pl.BlockSpec((D, tk), lambda i, k: (0, k)), row, row, _full((1, D))],
        out_specs=[row, _full((1, D))],
        scratch_shapes=[pltpu.VMEM((tm, D), f32)],
        compiler_params=_cp(("arbitrary", "arbitrary")))(dproj, wfull, xin, dout, g_pre)


def adam_update(parts, w, m, v, name, own=None, me=None):
    r, c = w.shape
    tr = r if r <= 128 else 128

    def body(*refs):
        if own is None:
            p_ref, w_ref, m_ref, v_ref, g_ref, d_ref, nm_ref, nv_ref = refs
            slot = lambda s: p_ref[s].astype(f32)
        else:
            me_ref, p_ref, own_ref, w_ref, m_ref, v_ref, g_ref, d_ref, nm_ref, nv_ref = refs
            slot = lambda s: jnp.where(me_ref[0] == s, own_ref[0], p_ref[s]).astype(f32)
        g = slot(0)
        for s in range(1, NDEV):
            g = g + slot(s)
        nm = ADAM_B1 * m_ref[...] + (1.0 - ADAM_B1) * g
        nv = ADAM_B2 * v_ref[...] + (1.0 - ADAM_B2) * (g * g)
        m_hat = nm / (1.0 - ADAM_B1 ** ADAM_STEP)
        v_hat = nv / (1.0 - ADAM_B2 ** ADAM_STEP)
        g_ref[...] = g
        d_ref[...] = -ADAM_LR * (m_hat / (jnp.sqrt(v_hat) + ADAM_EPS) + ADAM_WD * w_ref[...])
        nm_ref[...] = nm
        nv_ref[...] = nv

    sd = jax.ShapeDtypeStruct((r, c), f32)
    if own is None:
        blk = pl.BlockSpec((tr, c), lambda i: (i, 0))
        return pl.pallas_call(
            body, name=name, grid=(r // tr,), out_shape=[sd] * 4,
            in_specs=[pl.BlockSpec((NDEV, tr, c), lambda i: (0, i, 0)), blk, blk, blk], out_specs=[blk] * 4,
            compiler_params=_cp(("parallel",)))(parts, w, m, v)
    blk = pl.BlockSpec((tr, c), lambda i, me_ref: (i, 0))
    return pl.pallas_call(
        body, name=name, out_shape=[sd] * 4,
        grid_spec=pltpu.PrefetchScalarGridSpec(
            num_scalar_prefetch=1, grid=(r // tr,),
            in_specs=[pl.BlockSpec((NDEV, tr, c), lambda i, me_ref: (0, i, 0)),
                      pl.BlockSpec((1, tr, c), lambda i, me_ref: (me_ref[0], i, 0)), blk, blk, blk],
            out_specs=[blk] * 4),
        compiler_params=_cp(("parallel",)))(me, parts, own, w, m, v)


def _rope_tables():
    t = jnp.arange(S)
    rows, cols = (t // GRID_W).astype(f32), (t % GRID_W).astype(f32)
    nf = HD // 4
    inv = ROPE_THETA ** (-jnp.arange(nf, dtype=f32) / nf)
    ar, ac = rows[:, None] * inv, cols[:, None] * inv
    cos = jnp.concatenate([jnp.cos(ar), jnp.cos(ar), jnp.cos(ac), jnp.cos(ac)], axis=1)
    sin = jnp.concatenate([-jnp.sin(ar), jnp.sin(ar), -jnp.sin(ac), jnp.sin(ac)], axis=1)
    return cos, sin


def _pack_conv(ca, cc):
    z = lambda n: jnp.zeros((L, n, CT), f32)
    return jnp.concatenate([ca, z(5), cc, z(1)], axis=1)


def _pack_small(npre, npost, ccb, lng, lnb, qn, kn):
    wide = lambda a: jnp.pad(a, ((0, 0), (0, D - HD)))
    return jnp.stack([npre, npost, ccb, lng, lnb, wide(qn), wide(kn), jnp.zeros((L, D), f32)], axis=1).reshape(L * 8, D)


def kernel(x, norm_pre, norm_post, w_in, conv_a_w, q_norm, k_norm, conv_c_w, conv_c_b, ln_c_g, ln_c_b, w_out_a, w_out_b, w_out_c, w_o, loss_target, m_norm_pre, m_norm_post, m_w_in, m_conv_a_w, m_q_norm, m_k_norm, m_conv_c_w, m_conv_c_b, m_ln_c_g, m_ln_c_b, m_w_out_a, m_w_out_b, m_w_out_c, m_w_o, v_norm_pre, v_norm_post, v_w_in, v_conv_a_w, v_q_norm, v_k_norm, v_conv_c_w, v_conv_c_b, v_ln_c_g, v_ln_c_b, v_w_out_a, v_w_out_b, v_w_out_c, v_w_o):
    cos, sin = _rope_tables()
    rs = D // NDEV
    stack_sq = lambda a, b, c, d: jnp.stack([a, b, c, d], axis=1)
    wsq32 = stack_sq(w_out_a, w_out_b, w_out_c, w_o)
    win_bf = cast_bf16(w_in.reshape(L * D, PSH), "cast_win").reshape(L, D, PSH)
    wsq_bf = cast_bf16(wsq32.reshape(L * 4 * rs, D), "cast_wsq").reshape(L, 4, rs, D)
    conv_pack = _pack_conv(conv_a_w, conv_c_w)
    vec = lambda a, l: a[l][None, :]

    xs, saved = x[0], []
    gathers = [gather_start([win_bf[l], wsq_bf[l], conv_pack[l]], f"ag_start{l}") for l in range(L)]
    started = sum(g[4][0, 0] for g in gathers)
    for l in range(L):
        s_sems, r_sems, shards, lands, _ = gathers[l]
        shards, lands = gather_wait(s_sems, r_sems, shards, lands, xs, f"ag_wait{l}")
        wg, wsq, convw = gather_forward(shards, lands, f"ag_fwd{l}")
        wfull = relayout_win(wg, f"relayout{l}")
        g_pre = vec(norm_pre, l) + started if l == 0 else vec(norm_pre, l)
        proj, h = proj_fwd(xs, g_pre, wfull, f"proj{l}")
        yah = brancha_fwd(proj, convw, f"bra{l}")
        u1 = branchc1_fwd(proj, convw, vec(conv_c_b, l), f"brc1_{l}")
        qh, kh, vh = qkv_fwd(proj, vec(q_norm, l), vec(k_norm, l), cos, sin, f"qkv{l}")
        o, ybh = attn_fwd(qh, kh, vh, proj, f"attn{l}")
        ych = branchc2_fwd(u1, proj, vec(ln_c_g, l), vec(ln_c_b, l), f"brc2_{l}")
        ya, yb, yc, y16, z, xo = merge_fwd(xs, yah, ybh, ych, proj, wsq, vec(norm_post, l), f"merge{l}")
        saved.append(dict(x=xs, wfull=wfull, wsq=wsq, convw=convw, proj=proj, h=h, yah=yah, ybh=ybh, ych=ych, u1=u1,
                          qh=qh, kh=kh, vh=vh, o=o, ya=ya, yb=yb, yc=yc, y16=y16, z=z))
        xs = xo
    dx, loss_part = loss_fwd(xs, loss_target[0], "loss")
    loss = lax.psum(loss_part[0, 0], ("x", "y", "c"))

    outs = {k: [None] * L for k in ("w_in", "conv_a_w", "conv_c_w", "w_out_a", "w_out_b", "w_out_c", "w_o")}
    small_parts = [None] * L
    me = (4 * lax.axis_index("x") + 2 * lax.axis_index("y") + lax.axis_index("c")).astype(jnp.int32).reshape(1)
    msq32 = stack_sq(m_w_out_a, m_w_out_b, m_w_out_c, m_w_o)
    vsq32 = stack_sq(v_w_out_a, v_w_out_b, v_w_out_c, v_w_o)
    mconv, vconv = _pack_conv(m_conv_a_w, m_conv_c_w), _pack_conv(v_conv_a_w, v_conv_c_w)

    def finish(l, started, after):
        (s1, r1, p1, z1), (s2, r2, p2, z2) = started
        (gsq_own,), (rsq,) = scatter_wait(s1, r1, p1, z1, after, f"rs_sq_wait{l}")
        (gwin_own, gconv_own), (rwin, rconv) = scatter_wait(s2, r2, p2, z2, after, f"rs_win_wait{l}")
        outs["w_in"][l] = adam_update(rwin, w_in[l], m_w_in[l], v_w_in[l], f"adam_win{l}", own=gwin_own, me=me)
        sq = adam_update(rsq.reshape(NDEV, 4 * rs, D), wsq32[l].reshape(4 * rs, D), msq32[l].reshape(4 * rs, D),
                         vsq32[l].reshape(4 * rs, D), f"adam_wsq{l}", own=gsq_own.reshape(NDEV, 4 * rs, D), me=me)
        for t, nme in enumerate(("w_out_a", "w_out_b", "w_out_c", "w_o")):
            outs[nme][l] = [a[t * rs:(t + 1) * rs] for a in sq]
        cv = adam_update(rconv, conv_pack[l], mconv[l], vconv[l], f"adam_conv{l}", own=gconv_own, me=me)
        outs["conv_a_w"][l] = [a[0:CA_W] for a in cv]
        outs["conv_c_w"][l] = [a[8:8 + CC_W] for a in cv]

    pending = None
    for l in reversed(range(L)):
        sv = saved[l]
        proj = sv["proj"]
        (dyah, dybh, dych, dma, dmb, dmc, dzb, dyab, dybb, dycb, dgpost) = merge_bwd(
            dx, sv["z"], sv["ya"], sv["yb"], sv["yc"], proj, sv["wsq"], vec(norm_post, l), f"merge_bwd{l}")
        gsq = [tn_matmul(a, b, f"dwsq{t}_{l}") for t, (a, b) in enumerate(
            ((sv["yah"], dyab), (sv["ybh"], dybb), (sv["ych"], dycb), (sv["y16"], dzb)))]
        gsq_parts = jnp.stack([g.reshape(NDEV, rs, D) for g in gsq], axis=1)
        s1, r1, p1, z1, tok1 = scatter_start([gsq_parts], f"rs_sq_start{l}")
        convw = sv["convw"] + tok1[0, 0]
        dab, dac, dax, dag, gca = brancha_bwd(dyah, proj, convw, f"bra_bwd{l}")
        du1, dcg, dlg, dlb, dcb = branchc2_bwd(dych, sv["u1"], proj, vec(ln_c_g, l), vec(ln_c_b, l), f"brc2_bwd{l}")
        dcu, dcv, gcc = branchc1_bwd(du1, proj, convw, f"brc1_bwd{l}")
        dqh, dkh, dvh, dbg = attn_bwd(dybh, sv["o"], sv["qh"], sv["kh"], sv["vh"], proj, f"attn_bwd{l}")
        dq, dk, dv, dqn, dkn = qkv_bwd(dqh, dkh, dvh, proj, vec(q_norm, l), vec(k_norm, l), cos, sin, f"qkv_bwd{l}")
        dproj = jnp.concatenate([dab, dac, dax, dag, dq, dk, dv, dbg, dcu, dcv, dcg, dma, dmb, dmc], axis=1)
        gwin = dwin_parts(sv["h"], dproj, f"dwin{l}")
        gconv = jnp.concatenate([gca, gcc], axis=1)
        s2, r2, p2, z2, tok2 = scatter_start([gwin, gconv], f"rs_win_start{l}")
        dx, dgpre = dh_bwd(dproj, sv["wfull"], sv["x"], dx, vec(norm_pre, l) + tok2[0, 0], f"dh{l}")
        wide = lambda a: jnp.pad(a, ((0, 0), (0, D - HD)))
        small_parts[l] = jnp.concatenate([dgpre, dgpost, dcb, dlg, dlb, wide(dqn), wide(dkn), jnp.zeros((1, D), f32)], axis=0)
        if pending is not None:
            finish(*pending, after=dx)
        pending = (l, ((s1, r1, p1, z1), (s2, r2, p2, z2)))
    finish(*pending, after=dx)

    (small_all,) = all_gather([jnp.concatenate(small_parts, axis=0)], "ag_small")
    sm = adam_update(small_all,
                     _pack_small(norm_pre, norm_post, conv_c_b, ln_c_g, ln_c_b, q_norm, k_norm),
                     _pack_small(m_norm_pre, m_norm_post, m_conv_c_b, m_ln_c_g, m_ln_c_b, m_q_norm, m_k_norm),
                     _pack_small(v_norm_pre, v_norm_post, v_conv_c_b, v_ln_c_g, v_ln_c_b, v_q_norm, v_k_norm), "adam_small")
    sm = [a.reshape(L, 8, D) for a in sm]
    small_rows = dict(norm_pre=(0, D), norm_post=(1, D), conv_c_b=(2, D), ln_c_g=(3, D), ln_c_b=(4, D), q_norm=(5, HD), k_norm=(6, HD))

    order = ["norm_pre", "norm_post", "w_in", "conv_a_w", "q_norm", "k_norm", "conv_c_w", "conv_c_b", "ln_c_g", "ln_c_b",
             "w_out_a", "w_out_b", "w_out_c", "w_o"]
    result = [loss, dx[None]]
    for kind in range(4):
        for nme in order:
            if nme in small_rows:
                rw, wd = small_rows[nme]
                result.append(sm[kind][:, rw, :wd])
            else:
                result.append(jnp.stack([outs[nme][l][kind] for l in range(L)], axis=0))
    return tuple(result)
```

```python
import math

import jax
import jax.numpy as jnp
from jax import lax
from jax.experimental import pallas as pl
from jax.experimental.pallas import tpu as pltpu

f32, bf16 = jnp.float32, jnp.bfloat16

D = 1024
S = 2048
L = 4
HD = 128
NQ = D // HD
NKV = NQ // 4
G = NQ // NKV
WKV = NKV * HD
GRID_W = 64
ROPE_THETA = 10000.0
RMS_EPS = 1e-6
LN_EPS = 1e-5
NDEV = 8
CA_W, CC_W = 3, 31
P = 12 * D + 2 * WKV
PSH = P // NDEV
PAIR = 2 * PSH
CT = 128
ADAM_LR, ADAM_B1, ADAM_B2, ADAM_EPS, ADAM_WD, ADAM_STEP = 0.001, 0.9, 0.999, 1e-08, 0.01, 10
VMEM_LIMIT = 56 * 1024 * 1024
MESH = pl.DeviceIdType.MESH

_OFF = {}
_o = 0
for _n, _w in (("a_b", D), ("a_c", D), ("a_x", D), ("a_g", D), ("q", D), ("k", WKV), ("v", WKV), ("b_g", D),
               ("c_u", D), ("c_v", D), ("c_g", D), ("m_a", D), ("m_b", D), ("m_c", D)):
    _OFF[_n] = (_o, _w)
    _o += _w
PIECES = tuple(_OFF)


def _cp(sem=None, **kw):
    return pltpu.CompilerParams(dimension_semantics=sem, vmem_limit_bytes=VMEM_LIMIT, **kw)


def _sig(x):
    return 1.0 / (1.0 + jnp.exp(-x))


def _silu(x):
    return x * _sig(x)


def _dsilu(x):
    s = _sig(x)
    return s * (1.0 + x * (1.0 - s))


def _row_specs(name, tm):
    off, w = _OFF[name]
    bw = math.gcd(off, w) if off else w
    return [pl.BlockSpec((tm, bw), (lambda i, *_, b=off // bw + t: (i, b))) for t in range(w // bw)]


def _cat(refs):
    return refs[0][...] if len(refs) == 1 else jnp.concatenate([r[...] for r in refs], axis=1)


def _chan_spec(name):
    off, _ = _OFF[name]
    return pl.BlockSpec((S, CT), lambda j, b=off // CT: (0, b + j))


def _full(shape):
    return pl.BlockSpec(shape, lambda *_: (0,) * len(shape))


def _coords():
    return lax.axis_index("x"), lax.axis_index("y"), lax.axis_index("c")


def all_gather(shards, name):
    n = len(shards)

    def body(*refs):
        ins, outs = refs[:n], refs[n:2 * n]
        send_sems, recv_sems, local_sems = refs[2 * n:]
        x, y, c = _coords()
        me, sibling = (x, y, c), (x, y, 1 - c)
        chips = [(1 - x, y), (x, 1 - y), (1 - x, 1 - y)]

        def slot(a, p):
            return outs[a].at[4 * p[0] + 2 * p[1] + p[2]]

        def copy(a, k, block, to, src=None):
            return pltpu.make_async_remote_copy(
                src_ref=slot(a, block) if src is None else src, dst_ref=slot(a, block),
                send_sem=send_sems.at[7 * a + k], recv_sem=recv_sems.at[7 * a + k], device_id=to, device_id_type=MESH)

        mine = [pltpu.make_async_copy(ins[a], slot(a, me), local_sems.at[a]) for a in range(n)]
        for cp in mine:
            cp.start()
        first = []
        for a in range(n):
            first.append(copy(a, 0, me, sibling, src=ins[a]))
            first += [copy(a, 1 + j, me, (*chip, c), src=ins[a]) for j, chip in enumerate(chips)]
        for cp in first:
            cp.start()
        passed = []
        for j, chip in enumerate(chips):
            for a in range(n):
                copy(a, 1 + j, (*chip, c), me).wait_recv()
                fw = copy(a, 4 + j, (*chip, c), sibling)
                fw.start()
                passed.append(fw)
        for a in range(n):
            copy(a, 0, sibling, me).wait_recv()
            for j, chip in enumerate(chips):
                copy(a, 4 + j, (*chip, 1 - c), me).wait_recv()
        for cp in first + passed:
            cp.wait_send()
        for cp in mine:
            cp.wait()

    anyspec = pl.BlockSpec(memory_space=pl.ANY)
    return pl.pallas_call(
        body, name=name,
        out_shape=[jax.ShapeDtypeStruct((NDEV,) + s.shape, s.dtype) for s in shards],
        in_specs=[anyspec] * n, out_specs=[anyspec] * n,
        scratch_shapes=[pltpu.SemaphoreType.DMA((7 * n,)), pltpu.SemaphoreType.DMA((7 * n,)), pltpu.SemaphoreType.DMA((n,))],
    )(*shards)


_HBM = pl.BlockSpec(memory_space=pltpu.HBM)
_SEM = pl.BlockSpec(memory_space=pltpu.SEMAPHORE)
_EFFECT = pltpu.SideEffectType.DATAFLOW_SIDE_EFFECTING


def split_start(bufs, make_copies, nsem, name):
    n = len(bufs)

    def body(*refs):
        send_sems, recv_sems = refs[n:n + 2]
        for cp in make_copies(refs[:n], send_sems, recv_sems):
            cp.start()
        refs[-1][...] = jnp.zeros((8, 128), f32)

    res = pl.pallas_call(
        body, name=name,
        out_shape=(pltpu.SemaphoreType.DMA((nsem,)), pltpu.SemaphoreType.DMA((nsem,)),
                   *[pltpu.HBM(b.shape, b.dtype) for b in bufs], jax.ShapeDtypeStruct((8, 128), f32)),
        in_specs=[_HBM] * n,
        out_specs=(_SEM, _SEM, *([_HBM] * n), pl.BlockSpec(memory_space=pltpu.VMEM)),
        input_output_aliases={i: 2 + i for i in range(n)},
        compiler_params=pltpu.CompilerParams(has_side_effects=_EFFECT),
    )(*[pltpu.with_memory_space_constraint(b, pltpu.HBM) for b in bufs])
    return res[0], res[1], list(res[2:2 + n]), res[-1]


def split_wait(send_sems, recv_sems, bufs, make_copies, after, name):
    n = len(bufs)

    def body(*refs):
        for cp in make_copies(refs[:n], refs[n], refs[n + 1]):
            cp.wait_send()
            cp.wait_recv()

    res = pl.pallas_call(
        body, name=name,
        out_shape=tuple(pltpu.HBM(b.shape, b.dtype) for b in bufs),
        in_specs=[_HBM] * n + [_SEM, _SEM, pl.BlockSpec(memory_space=pl.ANY)],
        out_specs=[_HBM] * n,
        input_output_aliases={i: i for i in range(n)},
        compiler_params=pltpu.CompilerParams(has_side_effects=_EFFECT),
    )(*bufs, send_sems, recv_sems, after)
    return list(res)


def _scatter_copies(n):
    def make(refs, send_sems, recv_sems):
        x, y, c = _coords()
        me = 4 * x + 2 * y + c
        copies = []
        for a in range(n):
            for k in range(1, NDEV):
                px = 1 - x if (k >> 2) & 1 else x
                py = 1 - y if (k >> 1) & 1 else y
                pc = 1 - c if k & 1 else c
                copies.append(pltpu.make_async_remote_copy(
                    src_ref=refs[a].at[4 * px + 2 * py + pc], dst_ref=refs[n + a].at[me],
                    send_sem=send_sems.at[7 * a + k - 1], recv_sem=recv_sems.at[7 * a + k - 1],
                    device_id=(px, py, pc), device_id_type=MESH))
        return copies
    return make


def _gather_copies(refs, send_sems, recv_sems):
    x, y, c = _coords()
    me = 4 * x + 2 * y + c
    targets = [(x, y, 1 - c), (1 - x, y, c), (x, 1 - y, c), (1 - x, 1 - y, c)]
    return [pltpu.make_async_remote_copy(
        src_ref=r.at[me], dst_ref=r.at[me], send_sem=send_sems.at[4 * a + k], recv_sem=recv_sems.at[4 * a + k],
        device_id=to, device_id_type=MESH) for a, r in enumerate(refs) for k, to in enumerate(targets)]


def _forward_copies(refs, send_sems, recv_sems):
    x, y, c = _coords()
    chips = [(1 - x, y), (x, 1 - y), (1 - x, 1 - y)]
    return [pltpu.make_async_remote_copy(
        src_ref=r.at[4 * px + 2 * py + c], dst_ref=r.at[4 * px + 2 * py + c], send_sem=send_sems.at[3 * a + j],
        recv_sem=recv_sems.at[3 * a + j], device_id=(x, y, 1 - c), device_id_type=MESH)
        for a, r in enumerate(refs) for j, (px, py) in enumerate(chips)]


def stage_shards(w_in_l, wsq_l, conv_l, me, name):
    rs = D // NDEV
    steps = 8

    def body(me_ref, a_ref, b_ref, c_ref, ao_ref, bo_ref, co_ref):
        ao_ref[0] = a_ref[...].astype(bf16)
        bo_ref[0] = b_ref[...].astype(bf16)
        co_ref[0] = c_ref[...]

    return pl.pallas_call(
        body, name=name,
        out_shape=[jax.ShapeDtypeStruct((NDEV, D, PSH), bf16), jax.ShapeDtypeStruct((NDEV, 4 * rs, D), bf16),
                   jax.ShapeDtypeStruct((NDEV, 40, CT), f32)],
        grid_spec=pltpu.PrefetchScalarGridSpec(
            num_scalar_prefetch=1, grid=(steps,),
            in_specs=[pl.BlockSpec((D // steps, PSH), lambda i, m: (i, 0)), pl.BlockSpec((4 * rs // steps, D), lambda i, m: (i, 0)),
                      pl.BlockSpec((40, CT), lambda i, m: (0, 0))],
            out_specs=[pl.BlockSpec((1, D // steps, PSH), lambda i, m: (m[0], i, 0)),
                       pl.BlockSpec((1, 4 * rs // steps, D), lambda i, m: (m[0], i, 0)),
                       pl.BlockSpec((1, 40, CT), lambda i, m: (m[0], 0, 0))]),
        compiler_params=_cp(("arbitrary",)))(me, w_in_l, wsq_l, conv_l)


def relayout_win(wg, name):
    tr = min(D, 512)

    def body(w_ref, o_ref):
        o_ref[:, 0:PSH] = w_ref[0]
        o_ref[:, PSH:PAIR] = w_ref[1]

    return pl.pallas_call(
        body, name=name, grid=(NDEV // 2, D // tr), out_shape=jax.ShapeDtypeStruct((D, P), bf16),
        in_specs=[pl.BlockSpec((2, tr, PSH), lambda p, i: (p, i, 0))],
        out_specs=pl.BlockSpec((tr, PAIR), lambda p, i: (i, p)),
        compiler_params=_cp(("parallel", "parallel")))(wg)


def proj_fwd(xin, g_pre, wfull, name):
    tm, tn = min(S, 512), 1280

    def body(x_ref, g_ref, w_ref, proj_ref, h_ref, hs):
        @pl.when(pl.program_id(1) == 0)
        def _():
            x = x_ref[...]
            r = lax.rsqrt(jnp.mean(x * x, axis=-1, keepdims=True) + RMS_EPS)
            h = (x * r * g_ref[...]).astype(bf16)
            hs[...] = h
            h_ref[...] = h
        proj_ref[...] = jnp.dot(hs[...], w_ref[...], preferred_element_type=f32)

    return pl.pallas_call(
        body, name=name, grid=(S // tm, P // tn),
        out_shape=[jax.ShapeDtypeStruct((S, P), f32), jax.ShapeDtypeStruct((S, D), bf16)],
        in_specs=[pl.BlockSpec((tm, D), lambda i, j: (i, 0)), _full((1, D)), pl.BlockSpec((D, tn), lambda i, j: (0, j))],
        out_specs=[pl.BlockSpec((tm, tn), lambda i, j: (i, j)), pl.BlockSpec((tm, D), lambda i, j: (i, 0))],
        scratch_shapes=[pltpu.VMEM((tm, D), bf16)],
        compiler_params=_cp(("parallel", "arbitrary")))(xin, g_pre, wfull)


RC = 128


def _fill_pad(pad, halo, val_fn):
    pad[0:halo, :] = jnp.zeros((halo, CT), f32)
    pad[S + halo:S + 2 * halo, :] = jnp.zeros((halo, CT), f32)

    def step(i, carry):
        rows = pl.ds(pl.multiple_of(i * RC, RC), RC)
        pad[pl.ds(pl.multiple_of(i * RC, RC) + halo, RC), :] = val_fn(rows)
        return carry
    lax.fori_loop(0, S // RC, step, 0)


def brancha_fwd(proj, convw, name):
    def body(ab, ac, ax, ag, w_ref, o_ref, pad):
        _fill_pad(pad, 8, lambda rows: ac[rows, :] * ax[rows, :])
        w = [w_ref[0, k:k + 1, :] for k in range(CA_W)]

        def step(i, carry):
            base = pl.multiple_of(i * RC, RC)
            rows = pl.ds(base, RC)
            t = sum(w[k] * pad[pl.ds(base + 7 + k, RC), :] for k in range(CA_W))
            o_ref[rows, :] = (ab[rows, :] * t * _silu(ag[rows, :])).astype(bf16)
            return carry
        lax.fori_loop(0, S // RC, step, 0)

    return pl.pallas_call(
        body, name=name, grid=(D // CT,), out_shape=jax.ShapeDtypeStruct((S, D), bf16),
        in_specs=[_chan_spec("a_b"), _chan_spec("a_c"), _chan_spec("a_x"), _chan_spec("a_g"),
                  pl.BlockSpec((1, 40, CT), lambda j: (j, 0, 0))],
        out_specs=pl.BlockSpec((S, CT), lambda j: (0, j)),
        scratch_shapes=[pltpu.VMEM((S + 16, CT), f32)],
        compiler_params=_cp(("parallel",)))(proj, proj, proj, proj, convw)


def branchc1_fwd(proj, convw, cbias, name):
    def body(cu, cv, w_ref, b_ref, o_ref, pad):
        _fill_pad(pad, 16, lambda rows: cu[rows, :] * _sig(cv[rows, :]))

        def step(i, carry):
            base = pl.multiple_of(i * RC, RC)
            acc = jnp.zeros((RC, CT), f32) + b_ref[...]
            for k in range(CC_W):
                acc = acc + w_ref[0, 8 + k:9 + k, :] * pad[pl.ds(base + k + 1, RC), :]
            o_ref[pl.ds(base, RC), :] = acc
            return carry
        lax.fori_loop(0, S // RC, step, 0)

    return pl.pallas_call(
        body, name=name, grid=(D // CT,), out_shape=jax.ShapeDtypeStruct((S, D), f32),
        in_specs=[_chan_spec("c_u"), _chan_spec("c_v"), pl.BlockSpec((1, 40, CT), lambda j: (j, 0, 0)),
                  pl.BlockSpec((1, CT), lambda j: (0, j))],
        out_specs=pl.BlockSpec((S, CT), lambda j: (0, j)),
        scratch_shapes=[pltpu.VMEM((S + 32, CT), f32)],
        compiler_params=_cp(("parallel",)))(proj, proj, convw, cbias)


def _swap32(x):
    lane = lax.broadcasted_iota(jnp.int32, x.shape, 1)
    return jnp.where((lane // 32) % 2 == 1, pltpu.roll(x, 32, 1), pltpu.roll(x, HD - 32, 1))


def _rope(y, cos, sin):
    return y * cos + _swap32(y) * sin


def qkv_fwd(proj, qn, kn, cos, sin, name):
    tm = min(S, 256)
    nq, nk, nv = len(_row_specs("q", tm)), len(_row_specs("k", tm)), len(_row_specs("v", tm))

    def body(*refs):
        q = _cat(refs[:nq])
        k = _cat(refs[nq:nq + nk])
        v = _cat(refs[nq + nk:nq + nk + nv])
        qn_ref, kn_ref, cos_ref, sin_ref, qh_ref, kh_ref, vh_ref = refs[nq + nk + nv:]
        cos, sin = cos_ref[...], sin_ref[...]

        def heads(xx, gn, out_ref, n):
            for h in range(n):
                xh = xx[:, h * HD:(h + 1) * HD]
                r = lax.rsqrt(jnp.mean(xh * xh, axis=-1, keepdims=True) + RMS_EPS)
                out_ref[:, h * HD:(h + 1) * HD] = _rope(xh * r * gn, cos, sin).astype(bf16)
        heads(q, qn_ref[...], qh_ref, NQ)
        heads(k, kn_ref[...], kh_ref, NKV)
        vh_ref[...] = v.astype(bf16)

    row = lambda w: pl.BlockSpec((tm, w), lambda i: (i, 0))
    return pl.pallas_call(
        body, name=name, grid=(S // tm,),
        out_shape=[jax.ShapeDtypeStruct((S, D), bf16), jax.ShapeDtypeStruct((S, WKV), bf16), jax.ShapeDtypeStruct((S, WKV), bf16)],
        in_specs=_row_specs("q", tm) + _row_specs("k", tm) + _row_specs("v", tm) + [_full((1, HD)), _full((1, HD)), row(HD), row(HD)],
        out_specs=[row(D), row(WKV), row(WKV)],
        compiler_params=_cp(("parallel",)))(*([proj] * (nq + nk + nv)), qn, kn, cos, sin)


def _softmax_rows(q, k):
    s = lax.dot_general(q, k, (((1,), (1,)), ((), ())), preferred_element_type=f32) * (HD ** -0.5)
    p = jnp.exp(s - jnp.max(s, axis=-1, keepdims=True))
    return p / jnp.sum(p, axis=-1, keepdims=True)


def attn_fwd(qh, kh, vh, proj, name):
    tq = min(S, 256)
    bg_off = _OFF["b_g"][0] // HD

    def body(q_ref, k_ref, v_ref, bg_ref, o_ref, y_ref):
        pn = _softmax_rows(q_ref[...], k_ref[...])
        o = jnp.dot(pn.astype(bf16), v_ref[...], preferred_element_type=f32)
        o_ref[...] = o
        y_ref[...] = (o * _silu(bg_ref[...])).astype(bf16)

    head = lambda kv, g, i: (i, kv * G + g)
    return pl.pallas_call(
        body, name=name, grid=(NKV, G, S // tq),
        out_shape=[jax.ShapeDtypeStruct((S, D), f32), jax.ShapeDtypeStruct((S, D), bf16)],
        in_specs=[pl.BlockSpec((tq, HD), head), pl.BlockSpec((S, HD), lambda kv, g, i: (0, kv)),
                  pl.BlockSpec((S, HD), lambda kv, g, i: (0, kv)),
                  pl.BlockSpec((tq, HD), lambda kv, g, i: (i, bg_off + kv * G + g))],
        out_specs=[pl.BlockSpec((tq, HD), head), pl.BlockSpec((tq, HD), head)],
        compiler_params=_cp(("parallel", "parallel", "parallel")))(qh, kh, vh, proj)


def _ln_parts(u1):
    mu = jnp.mean(u1, axis=-1, keepdims=True)
    xc = u1 - mu
    rstd = lax.rsqrt(jnp.mean(xc * xc, axis=-1, keepdims=True) + LN_EPS)
    return xc * rstd, rstd


def branchc2_fwd(u1, proj, lng, lnb, name):
    tm = min(S, 256)
    ncg = len(_row_specs("c_g", tm))

    def body(*refs):
        u_ref = refs[0]
        cg = _cat(refs[1:1 + ncg])
        g_ref, b_ref, o_ref = refs[1 + ncg:]
        xh, _ = _ln_parts(u_ref[...])
        o_ref[...] = (_silu(xh * g_ref[...] + b_ref[...]) * _silu(cg)).astype(bf16)

    row = pl.BlockSpec((tm, D), lambda i: (i, 0))
    return pl.pallas_call(
        body, name=name, grid=(S // tm,), out_shape=jax.ShapeDtypeStruct((S, D), bf16),
        in_specs=[row] + _row_specs("c_g", tm) + [_full((1, D)), _full((1, D))], out_specs=row,
        compiler_params=_cp(("parallel",)))(u1, *([proj] * ncg), lng, lnb)


def _wmat(w_ref, kind):
    return w_ref[:, kind].reshape(D, D)


def merge_fwd(xin, yah, ybh, ych, proj, wsq, g_post, name):
    tm = min(S, 256)
    nm = len(_row_specs("m_a", tm))

    def body(*refs):
        x_ref, a_ref, b_ref, c_ref = refs[:4]
        ms = [_cat(refs[4 + t * nm:4 + (t + 1) * nm]) for t in range(3)]
        w_ref, g_ref, ya_ref, yb_ref, yc_ref, y_ref, z_ref, o_ref = refs[4 + 3 * nm:]
        y = jnp.zeros((tm, D), f32)
        for t, (h_ref, out_ref) in enumerate(((a_ref, ya_ref), (b_ref, yb_ref), (c_ref, yc_ref))):
            yt = jnp.dot(h_ref[...], _wmat(w_ref, t), preferred_element_type=f32)
            out_ref[...] = yt
            y = y + _sig(ms[t]) * yt
        yb16 = y.astype(bf16)
        y_ref[...] = yb16
        z = jnp.dot(yb16, _wmat(w_ref, 3), preferred_element_type=f32)
        z_ref[...] = z
        r = lax.rsqrt(jnp.mean(z * z, axis=-1, keepdims=True) + RMS_EPS)
        o_ref[...] = x_ref[...] + z * r * g_ref[...]

    row = pl.BlockSpec((tm, D), lambda i: (i, 0))
    sd = lambda dt: jax.ShapeDtypeStruct((S, D), dt)
    return pl.pallas_call(
        body, name=name, grid=(S // tm,),
        out_shape=[sd(f32), sd(f32), sd(f32), sd(bf16), sd(f32), sd(f32)],
        in_specs=[row] * 4 + _row_specs("m_a", tm) + _row_specs("m_b", tm) + _row_specs("m_c", tm)
        + [_full((NDEV, 4, D // NDEV, D)), _full((1, D))],
        out_specs=[row] * 6,
        compiler_params=_cp(("parallel",)))(xin, yah, ybh, ych, *([proj] * (3 * nm)), wsq, g_post)


def loss_fwd(y, target, name):
    tm = min(S, 256)

    def body(y_ref, t_ref, dy_ref, l_ref):
        e = y_ref[...] - t_ref[...]
        dy_ref[...] = e / D

        @pl.when(pl.program_id(0) == 0)
        def _():
            l_ref[...] = jnp.zeros((1, 128), f32)
        l_ref[...] += (0.5 / D) * jnp.sum(e * e)

    row = pl.BlockSpec((tm, D), lambda i: (i, 0))
    return pl.pallas_call(
        body, name=name, grid=(S // tm,),
        out_shape=[jax.ShapeDtypeStruct((S, D), f32), jax.ShapeDtypeStruct((1, 128), f32)],
        in_specs=[row, row], out_specs=[row, _full((1, 128))],
        compiler_params=_cp(("arbitrary",)))(y, target)


def _acc(ref, val):
    @pl.when(pl.program_id(0) == 0)
    def _():
        ref[...] = jnp.zeros(ref.shape, f32)
    ref[...] += val


def merge_bwd(dout, z, ya, yb, yc, proj, wsq, g_post, name):
    tm = min(S, 128)
    nm = len(_row_specs("m_a", tm))

    def body(*refs):
        do_ref, z_ref, ya_ref, yb_ref, yc_ref = refs[:5]
        ms = [_cat(refs[5 + t * nm:5 + (t + 1) * nm]) for t in range(3)]
        w_ref, g_ref = refs[5 + 3 * nm:7 + 3 * nm]
        dh_refs = refs[7 + 3 * nm:10 + 3 * nm]
        dm_refs = refs[10 + 3 * nm:13 + 3 * nm]
        dzb_ref = refs[13 + 3 * nm]
        dyb_refs = refs[14 + 3 * nm:17 + 3 * nm]
        dg_ref = refs[17 + 3 * nm]
        nt = (((1,), (1,)), ((), ()))
        z, dout = z_ref[...], do_ref[...]
        r = lax.rsqrt(jnp.mean(z * z, axis=-1, keepdims=True) + RMS_EPS)
        zh = z * r
        _acc(dg_ref, jnp.sum(dout * zh, axis=0, keepdims=True))
        dzh = dout * g_ref[...]
        dz = (r * (dzh - zh * jnp.mean(dzh * zh, axis=-1, keepdims=True))).astype(bf16)
        dzb_ref[...] = dz
        dy = lax.dot_general(dz, _wmat(w_ref, 3), nt, preferred_element_type=f32)
        for t, yt_ref in enumerate((ya_ref, yb_ref, yc_ref)):
            sg = _sig(ms[t])
            dyt = (dy * sg).astype(bf16)
            dyb_refs[t][...] = dyt
            dm_refs[t][...] = (dy * yt_ref[...] * sg * (1.0 - sg)).astype(bf16)
            dh_refs[t][...] = lax.dot_general(dyt, _wmat(w_ref, t), nt, preferred_element_type=f32)

    row = pl.BlockSpec((tm, D), lambda i: (i, 0))
    sd = lambda dt: jax.ShapeDtypeStruct((S, D), dt)
    return pl.pallas_call(
        body, name=name, grid=(S // tm,),
        out_shape=[sd(f32)] * 3 + [sd(bf16)] * 7 + [jax.ShapeDtypeStruct((1, D), f32)],
        in_specs=[row] * 5 + _row_specs("m_a", tm) + _row_specs("m_b", tm) + _row_specs("m_c", tm)
        + [_full((NDEV, 4, D // NDEV, D)), _full((1, D))],
        out_specs=[row] * 10 + [_full((1, D))],
        compiler_params=_cp(("arbitrary",)))(dout, z, ya, yb, yc, *([proj] * (3 * nm)), wsq, g_post)


def tn_matmul(a, b, name):
    m, n = a.shape[1], b.shape[1]
    tmm = min(m, 512)

    def body(a_ref, b_ref, o_ref):
        o_ref[...] = lax.dot_general(a_ref[...], b_ref[...], (((0,), (0,)), ((), ())), preferred_element_type=f32).astype(bf16)

    return pl.pallas_call(
        body, name=name, grid=(m // tmm,), out_shape=jax.ShapeDtypeStruct((m, n), bf16),
        in_specs=[pl.BlockSpec((S, tmm), lambda i: (0, i)), _full((S, n))],
        out_specs=pl.BlockSpec((tmm, n), lambda i: (i, 0)),
        compiler_params=_cp(("parallel",)))(a, b)


def dwin_parts(h, dproj, name):
    tmm = min(D, 256)

    def body(h_ref, d_ref, o_ref, acc):
        acc[...] = lax.dot_general(h_ref[...], d_ref[...], (((0,), (0,)), ((), ())), preferred_element_type=f32)
        o_ref[0] = acc[:, 0:PSH].astype(bf16)
        o_ref[1] = acc[:, PSH:PAIR].astype(bf16)

    return pl.pallas_call(
        body, name=name, grid=(NDEV // 2, D // tmm), out_shape=jax.ShapeDtypeStruct((NDEV, D, PSH), bf16),
        in_specs=[pl.BlockSpec((S, tmm), lambda p, i: (0, i)), pl.BlockSpec((S, PAIR), lambda p, i: (0, p))],
        out_specs=pl.BlockSpec((2, tmm, PSH), lambda p, i: (p, i, 0)),
        scratch_shapes=[pltpu.VMEM((tmm, PAIR), f32)],
        compiler_params=_cp(("parallel", "arbitrary")))(h, dproj)


def brancha_bwd(dyah, proj, convw, name):
    def body(d_ref, ab, ac, ax, ag, w_ref, dab, dac, dax, dag, dw_ref, padp, padt, accw):
        _fill_pad(padp, 8, lambda rows: ac[rows, :] * ax[rows, :])
        _fill_pad(padt, 8, lambda rows: d_ref[rows, :] * ab[rows, :] * _silu(ag[rows, :]))
        accw[...] = jnp.zeros(accw.shape, f32)
        w = [w_ref[0, k:k + 1, :] for k in range(CA_W)]

        def step(i, carry):
            base = pl.multiple_of(i * RC, RC)
            rows = pl.ds(base, RC)
            ps = [padp[pl.ds(base + 7 + k, RC), :] for k in range(CA_W)]
            t = sum(w[k] * ps[k] for k in range(CA_W))
            dp = sum(w[k] * padt[pl.ds(base + 9 - k, RC), :] for k in range(CA_W))
            d, a_b, a_g = d_ref[rows, :], ab[rows, :], ag[rows, :]
            dab[rows, :] = (d * t * _silu(a_g)).astype(bf16)
            dag[rows, :] = (d * a_b * t * _dsilu(a_g)).astype(bf16)
            dac[rows, :] = (dp * ax[rows, :]).astype(bf16)
            dax[rows, :] = (dp * ac[rows, :]).astype(bf16)
            dt = padt[pl.ds(base + 8, RC), :]
            for k in range(CA_W):
                accw[8 * k:8 * k + 8, :] += jnp.sum((dt * ps[k]).reshape(RC // 8, 8, CT), axis=0)
            return carry
        lax.fori_loop(0, S // RC, step, 0)
        dw_ref[0] = jnp.zeros((8, CT), f32)
        for k in range(CA_W):
            dw_ref[0, k:k + 1, :] = jnp.sum(accw[8 * k:8 * k + 8, :], axis=0, keepdims=True)

    tile = pl.BlockSpec((S, CT), lambda j: (0, j))
    sd = jax.ShapeDtypeStruct((S, D), bf16)
    return pl.pallas_call(
        body, name=name, grid=(D // CT,),
        out_shape=[sd, sd, sd, sd, jax.ShapeDtypeStruct((NDEV, 8, CT), f32)],
        in_specs=[tile, _chan_spec("a_b"), _chan_spec("a_c"), _chan_spec("a_x"), _chan_spec("a_g"),
                  pl.BlockSpec((1, 40, CT), lambda j: (j, 0, 0))],
        out_specs=[tile] * 4 + [pl.BlockSpec((1, 8, CT), lambda j: (j, 0, 0))],
        scratch_shapes=[pltpu.VMEM((S + 16, CT), f32), pltpu.VMEM((S + 16, CT), f32), pltpu.VMEM((8 * CA_W, CT), f32)],
        compiler_params=_cp(("parallel",)))(dyah, proj, proj, proj, proj, convw)


def branchc2_bwd(dych, u1, proj, lng, lnb, name):
    tm = min(S, 256)
    ncg = len(_row_specs("c_g", tm))

    def body(*refs):
        d_ref, u_ref = refs[:2]
        cg = _cat(refs[2:2 + ncg])
        g_ref, b_ref, du_ref, dcg_ref, dlg_ref, dlb_ref, dcb_ref = refs[2 + ncg:]
        d = d_ref[...]
        xh, rstd = _ln_parts(u_ref[...])
        ln = xh * g_ref[...] + b_ref[...]
        dcg_ref[...] = (d * _silu(ln) * _dsilu(cg)).astype(bf16)
        dln = d * _silu(cg) * _dsilu(ln)
        _acc(dlg_ref, jnp.sum(dln * xh, axis=0, keepdims=True))
        _acc(dlb_ref, jnp.sum(dln, axis=0, keepdims=True))
        dxh = dln * g_ref[...]
        du = rstd * (dxh - jnp.mean(dxh, axis=-1, keepdims=True) - xh * jnp.mean(dxh * xh, axis=-1, keepdims=True))
        du_ref[...] = du
        _acc(dcb_ref, jnp.sum(du, axis=0, keepdims=True))

    row = pl.BlockSpec((tm, D), lambda i: (i, 0))
    vec = jax.ShapeDtypeStruct((1, D), f32)
    return pl.pallas_call(
        body, name=name, grid=(S // tm,),
        out_shape=[jax.ShapeDtypeStruct((S, D), f32), jax.ShapeDtypeStruct((S, D), bf16), vec, vec, vec],
        in_specs=[row, row] + _row_specs("c_g", tm) + [_full((1, D)), _full((1, D))],
        out_specs=[row, row, _full((1, D)), _full((1, D)), _full((1, D))],
        compiler_params=_cp(("arbitrary",)))(dych, u1, *([proj] * ncg), lng, lnb)


def branchc1_bwd(du1, proj, convw, name):
    def body(d_ref, cu, cv, w_ref, dcu, dcv, dw_ref, padu, padd, accw):
        _fill_pad(padu, 16, lambda rows: cu[rows, :] * _sig(cv[rows, :]))
        _fill_pad(padd, 16, lambda rows: d_ref[rows, :])
        accw[...] = jnp.zeros(accw.shape, f32)

        def step(i, carry):
            base = pl.multiple_of(i * RC, RC)
            rows = pl.ds(base, RC)
            d = d_ref[rows, :]
            du0 = jnp.zeros((RC, CT), f32)
            for k in range(CC_W):
                du0 = du0 + w_ref[0, 8 + k:9 + k, :] * padd[pl.ds(base + 31 - k, RC), :]
                accw[8 * k:8 * k + 8, :] += jnp.sum((d * padu[pl.ds(base + k + 1, RC), :]).reshape(RC // 8, 8, CT), axis=0)
            sg = _sig(cv[rows, :])
            dcu[rows, :] = (du0 * sg).astype(bf16)
            dcv[rows, :] = (du0 * cu[rows, :] * sg * (1.0 - sg)).astype(bf16)
            return carry
        lax.fori_loop(0, S // RC, step, 0)
        dw_ref[0] = jnp.zeros((32, CT), f32)
        for k in range(CC_W):
            dw_ref[0, k:k + 1, :] = jnp.sum(accw[8 * k:8 * k + 8, :], axis=0, keepdims=True)

    tile = pl.BlockSpec((S, CT), lambda j: (0, j))
    sd = jax.ShapeDtypeStruct((S, D), bf16)
    return pl.pallas_call(
        body, name=name, grid=(D // CT,),
        out_shape=[sd, sd, jax.ShapeDtypeStruct((NDEV, 32, CT), f32)],
        in_specs=[tile, _chan_spec("c_u"), _chan_spec("c_v"), pl.BlockSpec((1, 40, CT), lambda j: (j, 0, 0))],
        out_specs=[tile, tile, pl.BlockSpec((1, 32, CT), lambda j: (j, 0, 0))],
        scratch_shapes=[pltpu.VMEM((S + 32, CT), f32), pltpu.VMEM((S + 32, CT), f32), pltpu.VMEM((8 * 32, CT), f32)],
        compiler_params=_cp(("parallel",)))(du1, proj, proj, convw)


def attn_bwd(dybh, o, qh, kh, vh, proj, name):
    tq = min(S, 256)
    bg_off = _OFF["b_g"][0] // HD

    def body(d_ref, o_ref, q_ref, k_ref, v_ref, bg_ref, dq_ref, dk_ref, dv_ref, dbg_ref):
        @pl.when((pl.program_id(1) == 0) & (pl.program_id(2) == 0))
        def _():
            dk_ref[...] = jnp.zeros(dk_ref.shape, f32)
            dv_ref[...] = jnp.zeros(dv_ref.shape, f32)
        d, bg = d_ref[...], bg_ref[...]
        dbg_ref[...] = (d * o_ref[...] * _dsilu(bg)).astype(bf16)
        do = (d * _silu(bg)).astype(bf16)
        q, k = q_ref[...], k_ref[...]
        pn = _softmax_rows(q, k)
        tn = (((0,), (0,)), ((), ()))
        dv_ref[...] += lax.dot_general(pn.astype(bf16), do, tn, preferred_element_type=f32)
        dp = lax.dot_general(do, v_ref[...], (((1,), (1,)), ((), ())), preferred_element_type=f32)
        ds = (pn * (dp - jnp.sum(pn * dp, axis=-1, keepdims=True)) * (HD ** -0.5)).astype(bf16)
        dq_ref[...] = jnp.dot(ds, k, preferred_element_type=f32)
        dk_ref[...] += lax.dot_general(ds, q, tn, preferred_element_type=f32)

    head = lambda kv, g, i: (i, kv * G + g)
    kvs = pl.BlockSpec((S, HD), lambda kv, g, i: (0, kv))
    return pl.pallas_call(
        body, name=name, grid=(NKV, G, S // tq),
        out_shape=[jax.ShapeDtypeStruct((S, D), f32), jax.ShapeDtypeStruct((S, WKV), f32),
                   jax.ShapeDtypeStruct((S, WKV), f32), jax.ShapeDtypeStruct((S, D), bf16)],
        in_specs=[pl.BlockSpec((tq, HD), head), pl.BlockSpec((tq, HD), head), pl.BlockSpec((tq, HD), head), kvs, kvs,
                  pl.BlockSpec((tq, HD), lambda kv, g, i: (i, bg_off + kv * G + g))],
        out_specs=[pl.BlockSpec((tq, HD), head), kvs, kvs, pl.BlockSpec((tq, HD), head)],
        compiler_params=_cp(("parallel", "arbitrary", "arbitrary")))(dybh, o, qh, kh, vh, proj)


def qkv_bwd(dqh, dkh, dvh, proj, qn, kn, cos, sin, name):
    tm = min(S, 256)
    nq, nk = len(_row_specs("q", tm)), len(_row_specs("k", tm))

    def body(*refs):
        dqh_ref, dkh_ref, dvh_ref = refs[:3]
        q = _cat(refs[3:3 + nq])
        k = _cat(refs[3 + nq:3 + nq + nk])
        qn_ref, kn_ref, cos_ref, sin_ref, dq_ref, dk_ref, dv_ref, dqn_ref, dkn_ref = refs[3 + nq + nk:]
        cos, sin = cos_ref[...], sin_ref[...]

        def heads(xx, dd, gn, out_ref, dgn_ref, n):
            dg = jnp.zeros((1, HD), f32)
            for h in range(n):
                xh = xx[:, h * HD:(h + 1) * HD]
                dh = dd[:, h * HD:(h + 1) * HD]
                r = lax.rsqrt(jnp.mean(xh * xh, axis=-1, keepdims=True) + RMS_EPS)
                xn = xh * r
                dy = dh * cos + _swap32(dh * sin)
                dg = dg + jnp.sum(dy * xn, axis=0, keepdims=True)
                dxn = dy * gn
                out_ref[:, h * HD:(h + 1) * HD] = (r * (dxn - xn * jnp.mean(dxn * xn, axis=-1, keepdims=True))).astype(bf16)
            _acc(dgn_ref, dg)
        heads(q, dqh_ref[...], qn_ref[...], dq_ref, dqn_ref, NQ)
        heads(k, dkh_ref[...], kn_ref[...], dk_ref, dkn_ref, NKV)
        dv_ref[...] = dvh_ref[...].astype(bf16)

    row = lambda w: pl.BlockSpec((tm, w), lambda i: (i, 0))
    vec = jax.ShapeDtypeStruct((1, HD), f32)
    return pl.pallas_call(
        body, name=name, grid=(S // tm,),
        out_shape=[jax.ShapeDtypeStruct((S, D), bf16), jax.ShapeDtypeStruct((S, WKV), bf16),
                   jax.ShapeDtypeStruct((S, WKV), bf16), vec, vec],
        in_specs=[row(D), row(WKV), row(WKV)] + _row_specs("q", tm) + _row_specs("k", tm)
        + [_full((1, HD)), _full((1, HD)), row(HD), row(HD)],
        out_specs=[row(D), row(WKV), row(WKV), _full((1, HD)), _full((1, HD))],
        compiler_params=_cp(("arbitrary",)))(dqh, dkh, dvh, *([proj] * (nq + nk)), qn, kn, cos, sin)


def dh_bwd(dproj, wfull, xin, dout, g_pre, name):
    tm, tk = min(S, 512), 1280
    nk = P // tk

    def body(d_ref, w_ref, x_ref, do_ref, g_ref, dx_ref, dg_ref, acc):
        kk = pl.program_id(1)

        @pl.when(kk == 0)
        def _():
            acc[...] = jnp.zeros(acc.shape, f32)
        acc[...] += lax.dot_general(d_ref[...], w_ref[...], (((1,), (1,)), ((), ())), preferred_element_type=f32)

        @pl.when((kk == 0) & (pl.program_id(0) == 0))
        def _():
            dg_ref[...] = jnp.zeros(dg_ref.shape, f32)

        @pl.when(kk == nk - 1)
        def _():
            x, dh = x_ref[...], acc[...]
            r = lax.rsqrt(jnp.mean(x * x, axis=-1, keepdims=True) + RMS_EPS)
            xn = x * r
            dg_ref[...] += jnp.sum(dh * xn, axis=0, keepdims=True)
            dxn = dh * g_ref[...]
            dx_ref[...] = do_ref[...] + r * (dxn - xn * jnp.mean(dxn * xn, axis=-1, keepdims=True))

    row = pl.BlockSpec((tm, D), lambda i, k: (i, 0))
    return pl.pallas_call(
        body, name=name, grid=(S // tm, nk),
        out_shape=[jax.ShapeDtypeStruct((S, D), f32), jax.ShapeDtypeStruct((1, D), f32)],
        in_specs=[pl.BlockSpec((tm, tk), lambda i, k: (i, k)), pl.BlockSpec((D, tk), lambda i, k: (0, k)), row, row, _full((1, D))],
        out_specs=[row, _full((1, D))],
        scratch_shapes=[pltpu.VMEM((tm, D), f32)],
        compiler_params=_cp(("arbitrary", "arbitrary")))(dproj, wfull, xin, dout, g_pre)


def adam_update(parts, w, m, v, name, own=None, me=None):
    r, c = w.shape
    tr = r if r <= 128 else 128

    def body(*refs):
        if own is None:
            p_ref, w_ref, m_ref, v_ref, g_ref, d_ref, nm_ref, nv_ref = refs
            slot = lambda s: p_ref[s].astype(f32)
        else:
            me_ref, p_ref, own_ref, w_ref, m_ref, v_ref, g_ref, d_ref, nm_ref, nv_ref = refs
            slot = lambda s: jnp.where(me_ref[0] == s, own_ref[0], p_ref[s]).astype(f32)
        g = slot(0)
        for s in range(1, NDEV):
            g = g + slot(s)
        nm = ADAM_B1 * m_ref[...] + (1.0 - ADAM_B1) * g
        nv = ADAM_B2 * v_ref[...] + (1.0 - ADAM_B2) * (g * g)
        m_hat = nm / (1.0 - ADAM_B1 ** ADAM_STEP)
        v_hat = nv / (1.0 - ADAM_B2 ** ADAM_STEP)
        g_ref[...] = g
        d_ref[...] = -ADAM_LR * (m_hat / (jnp.sqrt(v_hat) + ADAM_EPS) + ADAM_WD * w_ref[...])
        nm_ref[...] = nm
        nv_ref[...] = nv

    sd = jax.ShapeDtypeStruct((r, c), f32)
    if own is None:
        blk = pl.BlockSpec((tr, c), lambda i: (i, 0))
        return pl.pallas_call(
            body, name=name, grid=(r // tr,), out_shape=[sd] * 4,
            in_specs=[pl.BlockSpec((NDEV, tr, c), lambda i: (0, i, 0)), blk, blk, blk], out_specs=[blk] * 4,
            compiler_params=_cp(("parallel",)))(parts, w, m, v)
    blk = pl.BlockSpec((tr, c), lambda i, me_ref: (i, 0))
    return pl.pallas_call(
        body, name=name, out_shape=[sd] * 4,
        grid_spec=pltpu.PrefetchScalarGridSpec(
            num_scalar_prefetch=1, grid=(r // tr,),
            in_specs=[pl.BlockSpec((NDEV, tr, c), lambda i, me_ref: (0, i, 0)),
                      pl.BlockSpec((1, tr, c), lambda i, me_ref: (me_ref[0], i, 0)), blk, blk, blk],
            out_specs=[blk] * 4),
        compiler_params=_cp(("parallel",)))(me, parts, own, w, m, v)


def _rope_tables():
    t = jnp.arange(S)
    rows, cols = (t // GRID_W).astype(f32), (t % GRID_W).astype(f32)
    nf = HD // 4
    inv = ROPE_THETA ** (-jnp.arange(nf, dtype=f32) / nf)
    ar, ac = rows[:, None] * inv, cols[:, None] * inv
    cos = jnp.concatenate([jnp.cos(ar), jnp.cos(ar), jnp.cos(ac), jnp.cos(ac)], axis=1)
    sin = jnp.concatenate([-jnp.sin(ar), jnp.sin(ar), -jnp.sin(ac), jnp.sin(ac)], axis=1)
    return cos, sin


def _pack_conv(ca, cc):
    z = lambda n: jnp.zeros((L, n, CT), f32)
    return jnp.concatenate([ca, z(5), cc, z(1)], axis=1)


def _pack_small(npre, npost, ccb, lng, lnb, qn, kn):
    wide = lambda a: jnp.pad(a, ((0, 0), (0, D - HD)))
    return jnp.stack([npre, npost, ccb, lng, lnb, wide(qn), wide(kn), jnp.zeros((L, D), f32)], axis=1).reshape(L * 8, D)


def kernel(x, norm_pre, norm_post, w_in, conv_a_w, q_norm, k_norm, conv_c_w, conv_c_b, ln_c_g, ln_c_b, w_out_a, w_out_b, w_out_c, w_o, loss_target, m_norm_pre, m_norm_post, m_w_in, m_conv_a_w, m_q_norm, m_k_norm, m_conv_c_w, m_conv_c_b, m_ln_c_g, m_ln_c_b, m_w_out_a, m_w_out_b, m_w_out_c, m_w_o, v_norm_pre, v_norm_post, v_w_in, v_conv_a_w, v_q_norm, v_k_norm, v_conv_c_w, v_conv_c_b, v_ln_c_g, v_ln_c_b, v_w_out_a, v_w_out_b, v_w_out_c, v_w_o):
    cos, sin = _rope_tables()
    rs = D // NDEV
    stack_sq = lambda a, b, c, d: jnp.stack([a, b, c, d], axis=1)
    wsq32 = stack_sq(w_out_a, w_out_b, w_out_c, w_o)
    conv_pack = _pack_conv(conv_a_w, conv_c_w)
    vec = lambda a, l: a[l][None, :]
    me = (4 * lax.axis_index("x") + 2 * lax.axis_index("y") + lax.axis_index("c")).astype(jnp.int32).reshape(1)

    def forward_start(l, after):
        s_sems, r_sems, bufs, _ = gathers[l]
        bufs = split_wait(s_sems, r_sems, bufs, _gather_copies, after, f"ag_wait{l}")
        return split_start(bufs, _forward_copies, 9, f"ag_fwd_start{l}")

    def forward_wait(fw, after, l):
        s_sems, r_sems, bufs, _ = fw
        return split_wait(s_sems, r_sems, bufs, _forward_copies, after, f"ag_fwd_wait{l}")

    xs, saved = x[0], []
    gathers = [split_start(stage_shards(w_in[l], wsq32[l].reshape(4 * rs, D), conv_pack[l], me, f"stage{l}"),
                           _gather_copies, 12, f"ag_start{l}") for l in range(L)]
    started = sum(g[3][0, 0] for g in gathers)
    fw = forward_start(0, xs)
    wg, wsq, convw = forward_wait(fw, fw[3], 0)
    for l in range(L):
        wsq = wsq.reshape(NDEV, 4, rs, D)
        wfull = relayout_win(wg, f"relayout{l}")
        g_pre = vec(norm_pre, l) + started if l == 0 else vec(norm_pre, l)
        proj, h = proj_fwd(xs, g_pre, wfull, f"proj{l}")
        if l + 1 < L:
            fw = forward_start(l + 1, h)
            convw = convw + fw[3][0, 0]
        yah = brancha_fwd(proj, convw, f"bra{l}")
        u1 = branchc1_fwd(proj, convw, vec(conv_c_b, l), f"brc1_{l}")
        qh, kh, vh = qkv_fwd(proj, vec(q_norm, l), vec(k_norm, l), cos, sin, f"qkv{l}")
        o, ybh = attn_fwd(qh, kh, vh, proj, f"attn{l}")
        ych = branchc2_fwd(u1, proj, vec(ln_c_g, l), vec(ln_c_b, l), f"brc2_{l}")
        ya, yb, yc, y16, z, xo = merge_fwd(xs, yah, ybh, ych, proj, wsq, vec(norm_post, l), f"merge{l}")
        saved.append(dict(x=xs, wfull=wfull, wsq=wsq, convw=convw, proj=proj, h=h, yah=yah, ybh=ybh, ych=ych, u1=u1,
                          qh=qh, kh=kh, vh=vh, o=o, ya=ya, yb=yb, yc=yc, y16=y16, z=z))
        xs = xo
        if l + 1 < L:
            wg, wsq, convw = forward_wait(fw, xs, l + 1)
    dx, loss_part = loss_fwd(xs, loss_target[0], "loss")
    loss = lax.psum(loss_part[0, 0], ("x", "y", "c"))

    outs = {k: [None] * L for k in ("w_in", "conv_a_w", "conv_c_w", "w_out_a", "w_out_b", "w_out_c", "w_o")}
    small_parts = [None] * L
    msq32 = stack_sq(m_w_out_a, m_w_out_b, m_w_out_c, m_w_o)
    vsq32 = stack_sq(v_w_out_a, v_w_out_b, v_w_out_c, v_w_o)
    mconv, vconv = _pack_conv(m_conv_a_w, m_conv_c_w), _pack_conv(v_conv_a_w, v_conv_c_w)

    def scatter_start(parts, name):
        bufs = parts + [lax.empty(p.shape, p.dtype) for p in parts]
        return split_start(bufs, _scatter_copies(len(parts)), 7 * len(parts), name)

    def finish(l, started, after):
        (s1, r1, b1, _), (s2, r2, b2, _) = started
        gsq_own, rsq = split_wait(s1, r1, b1, _scatter_copies(1), after, f"rs_sq_wait{l}")
        gwin_own, gconv_own, rwin, rconv = split_wait(s2, r2, b2, _scatter_copies(2), after, f"rs_win_wait{l}")
        outs["w_in"][l] = adam_update(rwin, w_in[l], m_w_in[l], v_w_in[l], f"adam_win{l}", own=gwin_own, me=me)
        sq = adam_update(rsq.reshape(NDEV, 4 * rs, D), wsq32[l].reshape(4 * rs, D), msq32[l].reshape(4 * rs, D),
                         vsq32[l].reshape(4 * rs, D), f"adam_wsq{l}", own=gsq_own.reshape(NDEV, 4 * rs, D), me=me)
        for t, nme in enumerate(("w_out_a", "w_out_b", "w_out_c", "w_o")):
            outs[nme][l] = [a[t * rs:(t + 1) * rs] for a in sq]
        cv = adam_update(rconv, conv_pack[l], mconv[l], vconv[l], f"adam_conv{l}", own=gconv_own, me=me)
        outs["conv_a_w"][l] = [a[0:CA_W] for a in cv]
        outs["conv_c_w"][l] = [a[8:8 + CC_W] for a in cv]

    pending = None
    for l in reversed(range(L)):
        sv = saved[l]
        proj = sv["proj"]
        (dyah, dybh, dych, dma, dmb, dmc, dzb, dyab, dybb, dycb, dgpost) = merge_bwd(
            dx, sv["z"], sv["ya"], sv["yb"], sv["yc"], proj, sv["wsq"], vec(norm_post, l), f"merge_bwd{l}")
        gsq = [tn_matmul(a, b, f"dwsq{t}_{l}") for t, (a, b) in enumerate(
            ((sv["yah"], dyab), (sv["ybh"], dybb), (sv["ych"], dycb), (sv["y16"], dzb)))]
        gsq_parts = jnp.stack([g.reshape(NDEV, rs, D) for g in gsq], axis=1)
        st1 = scatter_start([gsq_parts], f"rs_sq_start{l}")
        convw = sv["convw"] + st1[3][0, 0]
        dab, dac, dax, dag, gca = brancha_bwd(dyah, proj, convw, f"bra_bwd{l}")
        du1, dcg, dlg, dlb, dcb = branchc2_bwd(dych, sv["u1"], proj, vec(ln_c_g, l), vec(ln_c_b, l), f"brc2_bwd{l}")
        dcu, dcv, gcc = branchc1_bwd(du1, proj, convw, f"brc1_bwd{l}")
        dqh, dkh, dvh, dbg = attn_bwd(dybh, sv["o"], sv["qh"], sv["kh"], sv["vh"], proj, f"attn_bwd{l}")
        dq, dk, dv, dqn, dkn = qkv_bwd(dqh, dkh, dvh, proj, vec(q_norm, l), vec(k_norm, l), cos, sin, f"qkv_bwd{l}")
        dproj = jnp.concatenate([dab, dac, dax, dag, dq, dk, dv, dbg, dcu, dcv, dcg, dma, dmb, dmc], axis=1)
        gwin = dwin_parts(sv["h"], dproj, f"dwin{l}")
        gconv = jnp.concatenate([gca, gcc], axis=1)
        st2 = scatter_start([gwin, gconv], f"rs_win_start{l}")
        dx, dgpre = dh_bwd(dproj, sv["wfull"], sv["x"], dx, vec(norm_pre, l) + st2[3][0, 0], f"dh{l}")
        wide = lambda a: jnp.pad(a, ((0, 0), (0, D - HD)))
        small_parts[l] = jnp.concatenate([dgpre, dgpost, dcb, dlg, dlb, wide(dqn), wide(dkn), jnp.zeros((1, D), f32)], axis=0)
        if pending is not None:
            finish(*pending, after=dx)
        pending = (l, (st1, st2))
    finish(*pending, after=dx)

    (small_all,) = all_gather([jnp.concatenate(small_parts, axis=0)], "ag_small")
    sm = adam_update(small_all,
                     _pack_small(norm_pre, norm_post, conv_c_b, ln_c_g, ln_c_b, q_norm, k_norm),
                     _pack_small(m_norm_pre, m_norm_post, m_conv_c_b, m_ln_c_g, m_ln_c_b, m_q_norm, m_k_norm),
                     _pack_small(v_norm_pre, v_norm_post, v_conv_c_b, v_ln_c_g, v_ln_c_b, v_q_norm, v_k_norm), "adam_small")
    sm = [a.reshape(L, 8, D) for a in sm]
    small_rows = dict(norm_pre=(0, D), norm_post=(1, D), conv_c_b=(2, D), ln_c_g=(3, D), ln_c_b=(4, D), q_norm=(5, HD), k_norm=(6, HD))

    order = ["norm_pre", "norm_post", "w_in", "conv_a_w", "q_norm", "k_norm", "conv_c_w", "conv_c_b", "ln_c_g", "ln_c_b",
             "w_out_a", "w_out_b", "w_out_c", "w_o"]
    result = [loss, dx[None]]
    for kind in range(4):
        for nme in order:
            if nme in small_rows:
                rw, wd = small_rows[nme]
                result.append(sm[kind][:, rw, :wd])
            else:
                result.append(jnp.stack([outs[nme][l][kind] for l in range(L)], axis=0))
    return tuple(result)
```

```python
import math

import jax
import jax.numpy as jnp
from jax import lax
from jax.experimental import pallas as pl
from jax.experimental.pallas import tpu as pltpu

f32, bf16 = jnp.float32, jnp.bfloat16

D = 1024
S = 2048
L = 4
HD = 128
NQ = D // HD
NKV = NQ // 4
G = NQ // NKV
WKV = NKV * HD
GRID_W = 64
ROPE_THETA = 10000.0
RMS_EPS = 1e-6
LN_EPS = 1e-5
NDEV = 8
CA_W, CC_W = 3, 31
P = 12 * D + 2 * WKV
PSH = P // NDEV
PAIR = 2 * PSH
CT = 128
ADAM_LR, ADAM_B1, ADAM_B2, ADAM_EPS, ADAM_WD, ADAM_STEP = 0.001, 0.9, 0.999, 1e-08, 0.01, 10
VMEM_LIMIT = 56 * 1024 * 1024
MESH = pl.DeviceIdType.MESH

_OFF = {}
_o = 0
for _n, _w in (("a_b", D), ("a_c", D), ("a_x", D), ("a_g", D), ("q", D), ("k", WKV), ("v", WKV), ("b_g", D),
               ("c_u", D), ("c_v", D), ("c_g", D), ("m_a", D), ("m_b", D), ("m_c", D)):
    _OFF[_n] = (_o, _w)
    _o += _w
PIECES = tuple(_OFF)


def _cp(sem=None, **kw):
    return pltpu.CompilerParams(dimension_semantics=sem, vmem_limit_bytes=VMEM_LIMIT, **kw)


def _sig(x):
    return 1.0 / (1.0 + jnp.exp(-x))


def _silu(x):
    return x * _sig(x)


def _dsilu(x):
    s = _sig(x)
    return s * (1.0 + x * (1.0 - s))


def _row_specs(name, tm):
    off, w = _OFF[name]
    bw = math.gcd(off, w) if off else w
    return [pl.BlockSpec((tm, bw), (lambda i, *_, b=off // bw + t: (i, b))) for t in range(w // bw)]


def _cat(refs):
    return refs[0][...] if len(refs) == 1 else jnp.concatenate([r[...] for r in refs], axis=1)


def _chan_spec(name):
    off, _ = _OFF[name]
    return pl.BlockSpec((S, CT), lambda j, b=off // CT: (0, b + j))


def _full(shape):
    return pl.BlockSpec(shape, lambda *_: (0,) * len(shape))


def _coords():
    return lax.axis_index("x"), lax.axis_index("y"), lax.axis_index("c")


def all_gather(shards, name):
    n = len(shards)

    def body(*refs):
        ins, outs = refs[:n], refs[n:2 * n]
        send_sems, recv_sems, local_sems = refs[2 * n:]
        x, y, c = _coords()
        me, sibling = (x, y, c), (x, y, 1 - c)
        chips = [(1 - x, y), (x, 1 - y), (1 - x, 1 - y)]

        def slot(a, p):
            return outs[a].at[4 * p[0] + 2 * p[1] + p[2]]

        def copy(a, k, block, to, src=None):
            return pltpu.make_async_remote_copy(
                src_ref=slot(a, block) if src is None else src, dst_ref=slot(a, block),
                send_sem=send_sems.at[7 * a + k], recv_sem=recv_sems.at[7 * a + k], device_id=to, device_id_type=MESH)

        mine = [pltpu.make_async_copy(ins[a], slot(a, me), local_sems.at[a]) for a in range(n)]
        for cp in mine:
            cp.start()
        first = []
        for a in range(n):
            first.append(copy(a, 0, me, sibling, src=ins[a]))
            first += [copy(a, 1 + j, me, (*chip, c), src=ins[a]) for j, chip in enumerate(chips)]
        for cp in first:
            cp.start()
        passed = []
        for j, chip in enumerate(chips):
            for a in range(n):
                copy(a, 1 + j, (*chip, c), me).wait_recv()
                fw = copy(a, 4 + j, (*chip, c), sibling)
                fw.start()
                passed.append(fw)
        for a in range(n):
            copy(a, 0, sibling, me).wait_recv()
            for j, chip in enumerate(chips):
                copy(a, 4 + j, (*chip, 1 - c), me).wait_recv()
        for cp in first + passed:
            cp.wait_send()
        for cp in mine:
            cp.wait()

    anyspec = pl.BlockSpec(memory_space=pl.ANY)
    return pl.pallas_call(
        body, name=name,
        out_shape=[jax.ShapeDtypeStruct((NDEV,) + s.shape, s.dtype) for s in shards],
        in_specs=[anyspec] * n, out_specs=[anyspec] * n,
        scratch_shapes=[pltpu.SemaphoreType.DMA((7 * n,)), pltpu.SemaphoreType.DMA((7 * n,)), pltpu.SemaphoreType.DMA((n,))],
    )(*shards)


_HBM = pl.BlockSpec(memory_space=pltpu.HBM)
_SEM = pl.BlockSpec(memory_space=pltpu.SEMAPHORE)
_EFFECT = pltpu.SideEffectType.DATAFLOW_SIDE_EFFECTING


def split_start(bufs, make_copies, nsem, name):
    n = len(bufs)

    def body(*refs):
        send_sems, recv_sems = refs[n:n + 2]
        for cp in make_copies(refs[:n], send_sems, recv_sems):
            cp.start()
        refs[-1][...] = jnp.zeros((8, 128), f32)

    res = pl.pallas_call(
        body, name=name,
        out_shape=(pltpu.SemaphoreType.DMA((nsem,)), pltpu.SemaphoreType.DMA((nsem,)),
                   *[pltpu.HBM(b.shape, b.dtype) for b in bufs], jax.ShapeDtypeStruct((8, 128), f32)),
        in_specs=[_HBM] * n,
        out_specs=(_SEM, _SEM, *([_HBM] * n), pl.BlockSpec(memory_space=pltpu.VMEM)),
        input_output_aliases={i: 2 + i for i in range(n)},
        compiler_params=pltpu.CompilerParams(has_side_effects=_EFFECT),
    )(*[pltpu.with_memory_space_constraint(b, pltpu.HBM) for b in bufs])
    return res[0], res[1], list(res[2:2 + n]), res[-1]


def split_wait(send_sems, recv_sems, bufs, make_copies, after, name):
    n = len(bufs)

    def body(*refs):
        for cp in make_copies(refs[:n], refs[n], refs[n + 1]):
            cp.wait_send()
            cp.wait_recv()

    res = pl.pallas_call(
        body, name=name,
        out_shape=tuple(pltpu.HBM(b.shape, b.dtype) for b in bufs),
        in_specs=[_HBM] * n + [_SEM, _SEM, pl.BlockSpec(memory_space=pl.ANY)],
        out_specs=[_HBM] * n,
        input_output_aliases={i: i for i in range(n)},
        compiler_params=pltpu.CompilerParams(has_side_effects=_EFFECT),
    )(*bufs, send_sems, recv_sems, after)
    return list(res)


def _scatter_copies(n):
    def make(refs, send_sems, recv_sems):
        x, y, c = _coords()
        me = 4 * x + 2 * y + c
        copies = []
        for a in range(n):
            for k in range(1, NDEV):
                px = 1 - x if (k >> 2) & 1 else x
                py = 1 - y if (k >> 1) & 1 else y
                pc = 1 - c if k & 1 else c
                copies.append(pltpu.make_async_remote_copy(
                    src_ref=refs[a].at[4 * px + 2 * py + pc], dst_ref=refs[n + a].at[me],
                    send_sem=send_sems.at[7 * a + k - 1], recv_sem=recv_sems.at[7 * a + k - 1],
                    device_id=(px, py, pc), device_id_type=MESH))
        return copies
    return make


def _gather_copies(refs, send_sems, recv_sems):
    x, y, c = _coords()
    me = 4 * x + 2 * y + c
    targets = [(x, y, 1 - c), (1 - x, y, c), (x, 1 - y, c), (1 - x, 1 - y, c)]
    return [pltpu.make_async_remote_copy(
        src_ref=r.at[me], dst_ref=r.at[me], send_sem=send_sems.at[4 * a + k], recv_sem=recv_sems.at[4 * a + k],
        device_id=to, device_id_type=MESH) for a, r in enumerate(refs) for k, to in enumerate(targets)]


def _forward_copies(refs, send_sems, recv_sems):
    x, y, c = _coords()
    chips = [(1 - x, y), (x, 1 - y), (1 - x, 1 - y)]
    return [pltpu.make_async_remote_copy(
        src_ref=r.at[4 * px + 2 * py + c], dst_ref=r.at[4 * px + 2 * py + c], send_sem=send_sems.at[3 * a + j],
        recv_sem=recv_sems.at[3 * a + j], device_id=(x, y, 1 - c), device_id_type=MESH)
        for a, r in enumerate(refs) for j, (px, py) in enumerate(chips)]


def stage_shards(w_in_l, wsq_l, conv_l, me, name):
    rs = D // NDEV
    steps = 8

    def body(me_ref, a_ref, b_ref, c_ref, ao_ref, bo_ref, co_ref):
        ao_ref[0] = a_ref[...].astype(bf16)
        bo_ref[0] = b_ref[...].astype(bf16)
        co_ref[0] = c_ref[...]

    return pl.pallas_call(
        body, name=name,
        out_shape=[jax.ShapeDtypeStruct((NDEV, D, PSH), bf16), jax.ShapeDtypeStruct((NDEV, 4 * rs, D), bf16),
                   jax.ShapeDtypeStruct((NDEV, 40, CT), f32)],
        grid_spec=pltpu.PrefetchScalarGridSpec(
            num_scalar_prefetch=1, grid=(steps,),
            in_specs=[pl.BlockSpec((D // steps, PSH), lambda i, m: (i, 0)), pl.BlockSpec((4 * rs // steps, D), lambda i, m: (i, 0)),
                      pl.BlockSpec((40, CT), lambda i, m: (0, 0))],
            out_specs=[pl.BlockSpec((1, D // steps, PSH), lambda i, m: (m[0], i, 0)),
                       pl.BlockSpec((1, 4 * rs // steps, D), lambda i, m: (m[0], i, 0)),
                       pl.BlockSpec((1, 40, CT), lambda i, m: (m[0], 0, 0))]),
        compiler_params=_cp(("arbitrary",)))(me, w_in_l, wsq_l, conv_l)


def relayout_win(wg, name):
    tr = min(D, 512)

    def body(w_ref, o_ref):
        o_ref[:, 0:PSH] = w_ref[0]
        o_ref[:, PSH:PAIR] = w_ref[1]

    return pl.pallas_call(
        body, name=name, grid=(NDEV // 2, D // tr), out_shape=jax.ShapeDtypeStruct((D, P), bf16),
        in_specs=[pl.BlockSpec((2, tr, PSH), lambda p, i: (p, i, 0))],
        out_specs=pl.BlockSpec((tr, PAIR), lambda p, i: (i, p)),
        compiler_params=_cp(("parallel", "parallel")))(wg)


def proj_fwd(xin, g_pre, wfull, name):
    tm, tn = min(S, 512), 1280

    def body(x_ref, g_ref, w_ref, proj_ref, h_ref, hs):
        @pl.when(pl.program_id(1) == 0)
        def _():
            x = x_ref[...]
            r = lax.rsqrt(jnp.mean(x * x, axis=-1, keepdims=True) + RMS_EPS)
            h = (x * r * g_ref[...]).astype(bf16)
            hs[...] = h
            h_ref[...] = h
        proj_ref[...] = jnp.dot(hs[...], w_ref[...], preferred_element_type=f32)

    return pl.pallas_call(
        body, name=name, grid=(S // tm, P // tn),
        out_shape=[jax.ShapeDtypeStruct((S, P), f32), jax.ShapeDtypeStruct((S, D), bf16)],
        in_specs=[pl.BlockSpec((tm, D), lambda i, j: (i, 0)), _full((1, D)), pl.BlockSpec((D, tn), lambda i, j: (0, j))],
        out_specs=[pl.BlockSpec((tm, tn), lambda i, j: (i, j)), pl.BlockSpec((tm, D), lambda i, j: (i, 0))],
        scratch_shapes=[pltpu.VMEM((tm, D), bf16)],
        compiler_params=_cp(("parallel", "arbitrary")))(xin, g_pre, wfull)


RC = 128


def _fill_pad(pad, halo, val_fn):
    pad[0:halo, :] = jnp.zeros((halo, CT), f32)
    pad[S + halo:S + 2 * halo, :] = jnp.zeros((halo, CT), f32)

    def step(i, carry):
        rows = pl.ds(pl.multiple_of(i * RC, RC), RC)
        pad[pl.ds(pl.multiple_of(i * RC, RC) + halo, RC), :] = val_fn(rows)
        return carry
    lax.fori_loop(0, S // RC, step, 0)


def brancha_fwd(proj, convw, name):
    def body(ab, ac, ax, ag, w_ref, o_ref, pad):
        _fill_pad(pad, 8, lambda rows: ac[rows, :] * ax[rows, :])
        w = [w_ref[0, k:k + 1, :] for k in range(CA_W)]

        def step(i, carry):
            base = pl.multiple_of(i * RC, RC)
            rows = pl.ds(base, RC)
            t = sum(w[k] * pad[pl.ds(base + 7 + k, RC), :] for k in range(CA_W))
            o_ref[rows, :] = (ab[rows, :] * t * _silu(ag[rows, :])).astype(bf16)
            return carry
        lax.fori_loop(0, S // RC, step, 0)

    return pl.pallas_call(
        body, name=name, grid=(D // CT,), out_shape=jax.ShapeDtypeStruct((S, D), bf16),
        in_specs=[_chan_spec("a_b"), _chan_spec("a_c"), _chan_spec("a_x"), _chan_spec("a_g"),
                  pl.BlockSpec((1, 40, CT), lambda j: (j, 0, 0))],
        out_specs=pl.BlockSpec((S, CT), lambda j: (0, j)),
        scratch_shapes=[pltpu.VMEM((S + 16, CT), f32)],
        compiler_params=_cp(("parallel",)))(proj, proj, proj, proj, convw)


def branchc1_fwd(proj, convw, cbias, name):
    def body(cu, cv, w_ref, b_ref, o_ref, pad):
        _fill_pad(pad, 16, lambda rows: cu[rows, :] * _sig(cv[rows, :]))

        def step(i, carry):
            base = pl.multiple_of(i * RC, RC)
            acc = jnp.zeros((RC, CT), f32) + b_ref[...]
            for k in range(CC_W):
                acc = acc + w_ref[0, 8 + k:9 + k, :] * pad[pl.ds(base + k + 1, RC), :]
            o_ref[pl.ds(base, RC), :] = acc
            return carry
        lax.fori_loop(0, S // RC, step, 0)

    return pl.pallas_call(
        body, name=name, grid=(D // CT,), out_shape=jax.ShapeDtypeStruct((S, D), f32),
        in_specs=[_chan_spec("c_u"), _chan_spec("c_v"), pl.BlockSpec((1, 40, CT), lambda j: (j, 0, 0)),
                  pl.BlockSpec((1, CT), lambda j: (0, j))],
        out_specs=pl.BlockSpec((S, CT), lambda j: (0, j)),
        scratch_shapes=[pltpu.VMEM((S + 32, CT), f32)],
        compiler_params=_cp(("parallel",)))(proj, proj, convw, cbias)


def _swap32(x):
    lane = lax.broadcasted_iota(jnp.int32, x.shape, 1)
    return jnp.where((lane // 32) % 2 == 1, pltpu.roll(x, 32, 1), pltpu.roll(x, HD - 32, 1))


def _rope(y, cos, sin):
    return y * cos + _swap32(y) * sin


def qkv_fwd(proj, qn, kn, cos, sin, name):
    tm = min(S, 256)
    nq, nk, nv = len(_row_specs("q", tm)), len(_row_specs("k", tm)), len(_row_specs("v", tm))

    def body(*refs):
        q = _cat(refs[:nq])
        k = _cat(refs[nq:nq + nk])
        v = _cat(refs[nq + nk:nq + nk + nv])
        qn_ref, kn_ref, cos_ref, sin_ref, qh_ref, kh_ref, vh_ref = refs[nq + nk + nv:]
        cos, sin = cos_ref[...], sin_ref[...]

        def heads(xx, gn, out_ref, n):
            for h in range(n):
                xh = xx[:, h * HD:(h + 1) * HD]
                r = lax.rsqrt(jnp.mean(xh * xh, axis=-1, keepdims=True) + RMS_EPS)
                out_ref[:, h * HD:(h + 1) * HD] = _rope(xh * r * gn, cos, sin).astype(bf16)
        heads(q, qn_ref[...], qh_ref, NQ)
        heads(k, kn_ref[...], kh_ref, NKV)
        vh_ref[...] = v.astype(bf16)

    row = lambda w: pl.BlockSpec((tm, w), lambda i: (i, 0))
    return pl.pallas_call(
        body, name=name, grid=(S // tm,),
        out_shape=[jax.ShapeDtypeStruct((S, D), bf16), jax.ShapeDtypeStruct((S, WKV), bf16), jax.ShapeDtypeStruct((S, WKV), bf16)],
        in_specs=_row_specs("q", tm) + _row_specs("k", tm) + _row_specs("v", tm) + [_full((1, HD)), _full((1, HD)), row(HD), row(HD)],
        out_specs=[row(D), row(WKV), row(WKV)],
        compiler_params=_cp(("parallel",)))(*([proj] * (nq + nk + nv)), qn, kn, cos, sin)


def _softmax_rows(q, k):
    s = lax.dot_general(q, k, (((1,), (1,)), ((), ())), preferred_element_type=f32)
    p = jnp.exp((s - jnp.max(s, axis=-1, keepdims=True)) * (HD ** -0.5))
    return p, 1.0 / jnp.sum(p, axis=-1, keepdims=True)


def attn_fwd(qh, kh, vh, proj, name):
    tq = min(S, 512)
    bg_off = _OFF["b_g"][0] // HD

    def body(q_ref, k_ref, v_ref, bg_ref, o_ref, y_ref):
        p, rl = _softmax_rows(q_ref[...], k_ref[...])
        o = jnp.dot(p.astype(bf16), v_ref[...], preferred_element_type=f32) * rl
        o_ref[...] = o
        y_ref[...] = (o * _silu(bg_ref[...])).astype(bf16)

    head = lambda kv, g, i: (i, kv * G + g)
    return pl.pallas_call(
        body, name=name, grid=(NKV, G, S // tq),
        out_shape=[jax.ShapeDtypeStruct((S, D), f32), jax.ShapeDtypeStruct((S, D), bf16)],
        in_specs=[pl.BlockSpec((tq, HD), head), pl.BlockSpec((S, HD), lambda kv, g, i: (0, kv)),
                  pl.BlockSpec((S, HD), lambda kv, g, i: (0, kv)),
                  pl.BlockSpec((tq, HD), lambda kv, g, i: (i, bg_off + kv * G + g))],
        out_specs=[pl.BlockSpec((tq, HD), head), pl.BlockSpec((tq, HD), head)],
        compiler_params=_cp(("parallel", "parallel", "parallel")))(qh, kh, vh, proj)


def _ln_parts(u1):
    mu = jnp.mean(u1, axis=-1, keepdims=True)
    xc = u1 - mu
    rstd = lax.rsqrt(jnp.mean(xc * xc, axis=-1, keepdims=True) + LN_EPS)
    return xc * rstd, rstd


def branchc2_fwd(u1, proj, lng, lnb, name):
    tm = min(S, 256)
    ncg = len(_row_specs("c_g", tm))

    def body(*refs):
        u_ref = refs[0]
        cg = _cat(refs[1:1 + ncg])
        g_ref, b_ref, o_ref = refs[1 + ncg:]
        xh, _ = _ln_parts(u_ref[...])
        o_ref[...] = (_silu(xh * g_ref[...] + b_ref[...]) * _silu(cg)).astype(bf16)

    row = pl.BlockSpec((tm, D), lambda i: (i, 0))
    return pl.pallas_call(
        body, name=name, grid=(S // tm,), out_shape=jax.ShapeDtypeStruct((S, D), bf16),
        in_specs=[row] + _row_specs("c_g", tm) + [_full((1, D)), _full((1, D))], out_specs=row,
        compiler_params=_cp(("parallel",)))(u1, *([proj] * ncg), lng, lnb)


def _wmat(w_ref, kind):
    return w_ref[:, kind].reshape(D, D)


def merge_fwd(xin, yah, ybh, ych, proj, wsq, g_post, name):
    tm = min(S, 256)
    nm = len(_row_specs("m_a", tm))

    def body(*refs):
        x_ref, a_ref, b_ref, c_ref = refs[:4]
        ms = [_cat(refs[4 + t * nm:4 + (t + 1) * nm]) for t in range(3)]
        w_ref, g_ref, ya_ref, yb_ref, yc_ref, y_ref, z_ref, o_ref = refs[4 + 3 * nm:]
        y = jnp.zeros((tm, D), f32)
        for t, (h_ref, out_ref) in enumerate(((a_ref, ya_ref), (b_ref, yb_ref), (c_ref, yc_ref))):
            yt = jnp.dot(h_ref[...], _wmat(w_ref, t), preferred_element_type=f32)
            out_ref[...] = yt
            y = y + _sig(ms[t]) * yt
        yb16 = y.astype(bf16)
        y_ref[...] = yb16
        z = jnp.dot(yb16, _wmat(w_ref, 3), preferred_element_type=f32)
        z_ref[...] = z
        r = lax.rsqrt(jnp.mean(z * z, axis=-1, keepdims=True) + RMS_EPS)
        o_ref[...] = x_ref[...] + z * r * g_ref[...]

    row = pl.BlockSpec((tm, D), lambda i: (i, 0))
    sd = lambda dt: jax.ShapeDtypeStruct((S, D), dt)
    return pl.pallas_call(
        body, name=name, grid=(S // tm,),
        out_shape=[sd(f32), sd(f32), sd(f32), sd(bf16), sd(f32), sd(f32)],
        in_specs=[row] * 4 + _row_specs("m_a", tm) + _row_specs("m_b", tm) + _row_specs("m_c", tm)
        + [_full((NDEV, 4, D // NDEV, D)), _full((1, D))],
        out_specs=[row] * 6,
        compiler_params=_cp(("parallel",)))(xin, yah, ybh, ych, *([proj] * (3 * nm)), wsq, g_post)


def loss_fwd(y, target, name):
    tm = min(S, 256)

    def body(y_ref, t_ref, dy_ref, l_ref):
        e = y_ref[...] - t_ref[...]
        dy_ref[...] = e / D

        @pl.when(pl.program_id(0) == 0)
        def _():
            l_ref[...] = jnp.zeros((1, 128), f32)
        l_ref[...] += (0.5 / D) * jnp.sum(e * e)

    row = pl.BlockSpec((tm, D), lambda i: (i, 0))
    return pl.pallas_call(
        body, name=name, grid=(S // tm,),
        out_shape=[jax.ShapeDtypeStruct((S, D), f32), jax.ShapeDtypeStruct((1, 128), f32)],
        in_specs=[row, row], out_specs=[row, _full((1, 128))],
        compiler_params=_cp(("arbitrary",)))(y, target)


def _acc(ref, val):
    @pl.when(pl.program_id(0) == 0)
    def _():
        ref[...] = jnp.zeros(ref.shape, f32)
    ref[...] += val


def merge_bwd(dout, z, ya, yb, yc, proj, wsq, g_post, name):
    tm = min(S, 256)
    nm = len(_row_specs("m_a", tm))

    def body(*refs):
        do_ref, z_ref, ya_ref, yb_ref, yc_ref = refs[:5]
        ms = [_cat(refs[5 + t * nm:5 + (t + 1) * nm]) for t in range(3)]
        w_ref, g_ref = refs[5 + 3 * nm:7 + 3 * nm]
        dh_refs = refs[7 + 3 * nm:10 + 3 * nm]
        dm_refs = refs[10 + 3 * nm:13 + 3 * nm]
        dzb_ref = refs[13 + 3 * nm]
        dyb_refs = refs[14 + 3 * nm:17 + 3 * nm]
        dg_ref = refs[17 + 3 * nm]
        nt = (((1,), (1,)), ((), ()))
        z, dout = z_ref[...], do_ref[...]
        r = lax.rsqrt(jnp.mean(z * z, axis=-1, keepdims=True) + RMS_EPS)
        zh = z * r
        _acc(dg_ref, jnp.sum(dout * zh, axis=0, keepdims=True))
        dzh = dout * g_ref[...]
        dz = (r * (dzh - zh * jnp.mean(dzh * zh, axis=-1, keepdims=True))).astype(bf16)
        dzb_ref[...] = dz
        dy = lax.dot_general(dz, _wmat(w_ref, 3), nt, preferred_element_type=f32)
        for t, yt_ref in enumerate((ya_ref, yb_ref, yc_ref)):
            sg = _sig(ms[t])
            dyt = (dy * sg).astype(bf16)
            dyb_refs[t][...] = dyt
            dm_refs[t][...] = (dy * yt_ref[...] * sg * (1.0 - sg)).astype(bf16)
            dh_refs[t][...] = lax.dot_general(dyt, _wmat(w_ref, t), nt, preferred_element_type=f32)

    row = pl.BlockSpec((tm, D), lambda i: (i, 0))
    sd = lambda dt: jax.ShapeDtypeStruct((S, D), dt)
    return pl.pallas_call(
        body, name=name, grid=(S // tm,),
        out_shape=[sd(f32)] * 3 + [sd(bf16)] * 7 + [jax.ShapeDtypeStruct((1, D), f32)],
        in_specs=[row] * 5 + _row_specs("m_a", tm) + _row_specs("m_b", tm) + _row_specs("m_c", tm)
        + [_full((NDEV, 4, D // NDEV, D)), _full((1, D))],
        out_specs=[row] * 10 + [_full((1, D))],
        compiler_params=_cp(("arbitrary",)))(dout, z, ya, yb, yc, *([proj] * (3 * nm)), wsq, g_post)


def tn_matmul(a, b, name):
    m, n = a.shape[1], b.shape[1]
    tmm = min(m, 512)

    def body(a_ref, b_ref, o_ref):
        o_ref[...] = lax.dot_general(a_ref[...], b_ref[...], (((0,), (0,)), ((), ())), preferred_element_type=f32).astype(bf16)

    return pl.pallas_call(
        body, name=name, grid=(m // tmm,), out_shape=jax.ShapeDtypeStruct((m, n), bf16),
        in_specs=[pl.BlockSpec((S, tmm), lambda i: (0, i)), _full((S, n))],
        out_specs=pl.BlockSpec((tmm, n), lambda i: (i, 0)),
        compiler_params=_cp(("parallel",)))(a, b)


def dwin_parts(h, dproj, name):
    tmm = min(D, 256)

    def body(h_ref, d_ref, o_ref, acc):
        acc[...] = lax.dot_general(h_ref[...], d_ref[...], (((0,), (0,)), ((), ())), preferred_element_type=f32)
        o_ref[0] = acc[:, 0:PSH].astype(bf16)
        o_ref[1] = acc[:, PSH:PAIR].astype(bf16)

    return pl.pallas_call(
        body, name=name, grid=(NDEV // 2, D // tmm), out_shape=jax.ShapeDtypeStruct((NDEV, D, PSH), bf16),
        in_specs=[pl.BlockSpec((S, tmm), lambda p, i: (0, i)), pl.BlockSpec((S, PAIR), lambda p, i: (0, p))],
        out_specs=pl.BlockSpec((2, tmm, PSH), lambda p, i: (p, i, 0)),
        scratch_shapes=[pltpu.VMEM((tmm, PAIR), f32)],
        compiler_params=_cp(("parallel", "arbitrary")))(h, dproj)


def brancha_bwd(dyah, proj, convw, name):
    def body(d_ref, ab, ac, ax, ag, w_ref, dab, dac, dax, dag, dw_ref, padp, padt, accw):
        _fill_pad(padp, 8, lambda rows: ac[rows, :] * ax[rows, :])
        _fill_pad(padt, 8, lambda rows: d_ref[rows, :] * ab[rows, :] * _silu(ag[rows, :]))
        accw[...] = jnp.zeros(accw.shape, f32)
        w = [w_ref[0, k:k + 1, :] for k in range(CA_W)]

        def step(i, carry):
            base = pl.multiple_of(i * RC, RC)
            rows = pl.ds(base, RC)
            ps = [padp[pl.ds(base + 7 + k, RC), :] for k in range(CA_W)]
            t = sum(w[k] * ps[k] for k in range(CA_W))
            dp = sum(w[k] * padt[pl.ds(base + 9 - k, RC), :] for k in range(CA_W))
            d, a_b, a_g = d_ref[rows, :], ab[rows, :], ag[rows, :]
            dab[rows, :] = (d * t * _silu(a_g)).astype(bf16)
            dag[rows, :] = (d * a_b * t * _dsilu(a_g)).astype(bf16)
            dac[rows, :] = (dp * ax[rows, :]).astype(bf16)
            dax[rows, :] = (dp * ac[rows, :]).astype(bf16)
            dt = padt[pl.ds(base + 8, RC), :]
            for k in range(CA_W):
                accw[8 * k:8 * k + 8, :] += jnp.sum((dt * ps[k]).reshape(RC // 8, 8, CT), axis=0)
            return carry
        lax.fori_loop(0, S // RC, step, 0)
        dw_ref[0] = jnp.zeros((8, CT), f32)
        for k in range(CA_W):
            dw_ref[0, k:k + 1, :] = jnp.sum(accw[8 * k:8 * k + 8, :], axis=0, keepdims=True)

    tile = pl.BlockSpec((S, CT), lambda j: (0, j))
    sd = jax.ShapeDtypeStruct((S, D), bf16)
    return pl.pallas_call(
        body, name=name, grid=(D // CT,),
        out_shape=[sd, sd, sd, sd, jax.ShapeDtypeStruct((NDEV, 8, CT), f32)],
        in_specs=[tile, _chan_spec("a_b"), _chan_spec("a_c"), _chan_spec("a_x"), _chan_spec("a_g"),
                  pl.BlockSpec((1, 40, CT), lambda j: (j, 0, 0))],
        out_specs=[tile] * 4 + [pl.BlockSpec((1, 8, CT), lambda j: (j, 0, 0))],
        scratch_shapes=[pltpu.VMEM((S + 16, CT), f32), pltpu.VMEM((S + 16, CT), f32), pltpu.VMEM((8 * CA_W, CT), f32)],
        compiler_params=_cp(("parallel",)))(dyah, proj, proj, proj, proj, convw)


def branchc2_bwd(dych, u1, proj, lng, lnb, name):
    tm = min(S, 256)
    ncg = len(_row_specs("c_g", tm))

    def body(*refs):
        d_ref, u_ref = refs[:2]
        cg = _cat(refs[2:2 + ncg])
        g_ref, b_ref, du_ref, dcg_ref, dlg_ref, dlb_ref, dcb_ref = refs[2 + ncg:]
        d = d_ref[...]
        xh, rstd = _ln_parts(u_ref[...])
        ln = xh * g_ref[...] + b_ref[...]
        dcg_ref[...] = (d * _silu(ln) * _dsilu(cg)).astype(bf16)
        dln = d * _silu(cg) * _dsilu(ln)
        _acc(dlg_ref, jnp.sum(dln * xh, axis=0, keepdims=True))
        _acc(dlb_ref, jnp.sum(dln, axis=0, keepdims=True))
        dxh = dln * g_ref[...]
        du = rstd * (dxh - jnp.mean(dxh, axis=-1, keepdims=True) - xh * jnp.mean(dxh * xh, axis=-1, keepdims=True))
        du_ref[...] = du
        _acc(dcb_ref, jnp.sum(du, axis=0, keepdims=True))

    row = pl.BlockSpec((tm, D), lambda i: (i, 0))
    vec = jax.ShapeDtypeStruct((1, D), f32)
    return pl.pallas_call(
        body, name=name, grid=(S // tm,),
        out_shape=[jax.ShapeDtypeStruct((S, D), f32), jax.ShapeDtypeStruct((S, D), bf16), vec, vec, vec],
        in_specs=[row, row] + _row_specs("c_g", tm) + [_full((1, D)), _full((1, D))],
        out_specs=[row, row, _full((1, D)), _full((1, D)), _full((1, D))],
        compiler_params=_cp(("arbitrary",)))(dych, u1, *([proj] * ncg), lng, lnb)


def branchc1_bwd(du1, proj, convw, name):
    def body(d_ref, cu, cv, w_ref, dcu, dcv, dw_ref, padu, padd, accw):
        _fill_pad(padu, 16, lambda rows: cu[rows, :] * _sig(cv[rows, :]))
        _fill_pad(padd, 16, lambda rows: d_ref[rows, :])
        accw[...] = jnp.zeros(accw.shape, f32)

        def step(i, carry):
            base = pl.multiple_of(i * RC, RC)
            rows = pl.ds(base, RC)
            d = d_ref[rows, :]
            du0 = jnp.zeros((RC, CT), f32)
            for k in range(CC_W):
                du0 = du0 + w_ref[0, 8 + k:9 + k, :] * padd[pl.ds(base + 31 - k, RC), :]
                accw[8 * k:8 * k + 8, :] += jnp.sum((d * padu[pl.ds(base + k + 1, RC), :]).reshape(RC // 8, 8, CT), axis=0)
            sg = _sig(cv[rows, :])
            dcu[rows, :] = (du0 * sg).astype(bf16)
            dcv[rows, :] = (du0 * cu[rows, :] * sg * (1.0 - sg)).astype(bf16)
            return carry
        lax.fori_loop(0, S // RC, step, 0)
        dw_ref[0] = jnp.zeros((32, CT), f32)
        for k in range(CC_W):
            dw_ref[0, k:k + 1, :] = jnp.sum(accw[8 * k:8 * k + 8, :], axis=0, keepdims=True)

    tile = pl.BlockSpec((S, CT), lambda j: (0, j))
    sd = jax.ShapeDtypeStruct((S, D), bf16)
    return pl.pallas_call(
        body, name=name, grid=(D // CT,),
        out_shape=[sd, sd, jax.ShapeDtypeStruct((NDEV, 32, CT), f32)],
        in_specs=[tile, _chan_spec("c_u"), _chan_spec("c_v"), pl.BlockSpec((1, 40, CT), lambda j: (j, 0, 0))],
        out_specs=[tile, tile, pl.BlockSpec((1, 32, CT), lambda j: (j, 0, 0))],
        scratch_shapes=[pltpu.VMEM((S + 32, CT), f32), pltpu.VMEM((S + 32, CT), f32), pltpu.VMEM((8 * 32, CT), f32)],
        compiler_params=_cp(("parallel",)))(du1, proj, proj, convw)


def attn_bwd(dybh, o, qh, kh, vh, proj, name):
    tq = min(S, 512)
    bg_off = _OFF["b_g"][0] // HD

    def body(d_ref, o_ref, q_ref, k_ref, v_ref, bg_ref, dq_ref, dk_ref, dv_ref, dbg_ref):
        @pl.when((pl.program_id(1) == 0) & (pl.program_id(2) == 0))
        def _():
            dk_ref[...] = jnp.zeros(dk_ref.shape, f32)
            dv_ref[...] = jnp.zeros(dv_ref.shape, f32)
        d, bg = d_ref[...], bg_ref[...]
        dbg_ref[...] = (d * o_ref[...] * _dsilu(bg)).astype(bf16)
        do = d * _silu(bg)
        q, k = q_ref[...], k_ref[...]
        p, rl = _softmax_rows(q, k)
        tn = (((0,), (0,)), ((), ()))
        dv_ref[...] += lax.dot_general(p.astype(bf16), (do * rl).astype(bf16), tn, preferred_element_type=f32)
        dp = lax.dot_general(do.astype(bf16), v_ref[...], (((1,), (1,)), ((), ())), preferred_element_type=f32)
        delta = jnp.sum(p * dp, axis=-1, keepdims=True) * rl
        ds = (p * (dp - delta)).astype(bf16)
        rs_ = rl * (HD ** -0.5)
        dq_ref[...] = jnp.dot(ds, k, preferred_element_type=f32) * rs_
        dk_ref[...] += lax.dot_general(ds, (q.astype(f32) * rs_).astype(bf16), tn, preferred_element_type=f32)

    head = lambda kv, g, i: (i, kv * G + g)
    kvs = pl.BlockSpec((S, HD), lambda kv, g, i: (0, kv))
    return pl.pallas_call(
        body, name=name, grid=(NKV, G, S // tq),
        out_shape=[jax.ShapeDtypeStruct((S, D), f32), jax.ShapeDtypeStruct((S, WKV), f32),
                   jax.ShapeDtypeStruct((S, WKV), f32), jax.ShapeDtypeStruct((S, D), bf16)],
        in_specs=[pl.BlockSpec((tq, HD), head), pl.BlockSpec((tq, HD), head), pl.BlockSpec((tq, HD), head), kvs, kvs,
                  pl.BlockSpec((tq, HD), lambda kv, g, i: (i, bg_off + kv * G + g))],
        out_specs=[pl.BlockSpec((tq, HD), head), kvs, kvs, pl.BlockSpec((tq, HD), head)],
        compiler_params=_cp(("parallel", "arbitrary", "arbitrary")))(dybh, o, qh, kh, vh, proj)


def qkv_bwd(dqh, dkh, dvh, proj, qn, kn, cos, sin, name):
    tm = min(S, 256)
    nq, nk = len(_row_specs("q", tm)), len(_row_specs("k", tm))

    def body(*refs):
        dqh_ref, dkh_ref, dvh_ref = refs[:3]
        q = _cat(refs[3:3 + nq])
        k = _cat(refs[3 + nq:3 + nq + nk])
        qn_ref, kn_ref, cos_ref, sin_ref, dq_ref, dk_ref, dv_ref, dqn_ref, dkn_ref = refs[3 + nq + nk:]
        cos, sin = cos_ref[...], sin_ref[...]

        def heads(xx, dd, gn, out_ref, dgn_ref, n):
            dg = jnp.zeros((1, HD), f32)
            for h in range(n):
                xh = xx[:, h * HD:(h + 1) * HD]
                dh = dd[:, h * HD:(h + 1) * HD]
                r = lax.rsqrt(jnp.mean(xh * xh, axis=-1, keepdims=True) + RMS_EPS)
                xn = xh * r
                dy = dh * cos + _swap32(dh * sin)
                dg = dg + jnp.sum(dy * xn, axis=0, keepdims=True)
                dxn = dy * gn
                out_ref[:, h * HD:(h + 1) * HD] = (r * (dxn - xn * jnp.mean(dxn * xn, axis=-1, keepdims=True))).astype(bf16)
            _acc(dgn_ref, dg)
        heads(q, dqh_ref[...], qn_ref[...], dq_ref, dqn_ref, NQ)
        heads(k, dkh_ref[...], kn_ref[...], dk_ref, dkn_ref, NKV)
        dv_ref[...] = dvh_ref[...].astype(bf16)

    row = lambda w: pl.BlockSpec((tm, w), lambda i: (i, 0))
    vec = jax.ShapeDtypeStruct((1, HD), f32)
    return pl.pallas_call(
        body, name=name, grid=(S // tm,),
        out_shape=[jax.ShapeDtypeStruct((S, D), bf16), jax.ShapeDtypeStruct((S, WKV), bf16),
                   jax.ShapeDtypeStruct((S, WKV), bf16), vec, vec],
        in_specs=[row(D), row(WKV), row(WKV)] + _row_specs("q", tm) + _row_specs("k", tm)
        + [_full((1, HD)), _full((1, HD)), row(HD), row(HD)],
        out_specs=[row(D), row(WKV), row(WKV), _full((1, HD)), _full((1, HD))],
        compiler_params=_cp(("arbitrary",)))(dqh, dkh, dvh, *([proj] * (nq + nk)), qn, kn, cos, sin)


def dh_bwd(dproj, wfull, xin, dout, g_pre, name):
    tm, tk = min(S, 512), 1280
    nk = P // tk

    def body(d_ref, w_ref, x_ref, do_ref, g_ref, dx_ref, dg_ref, acc):
        kk = pl.program_id(1)

        @pl.when(kk == 0)
        def _():
            acc[...] = jnp.zeros(acc.shape, f32)
        acc[...] += lax.dot_general(d_ref[...], w_ref[...], (((1,), (1,)), ((), ())), preferred_element_type=f32)

        @pl.when((kk == 0) & (pl.program_id(0) == 0))
        def _():
            dg_ref[...] = jnp.zeros(dg_ref.shape, f32)

        @pl.when(kk == nk - 1)
        def _():
            x, dh = x_ref[...], acc[...]
            r = lax.rsqrt(jnp.mean(x * x, axis=-1, keepdims=True) + RMS_EPS)
            xn = x * r
            dg_ref[...] += jnp.sum(dh * xn, axis=0, keepdims=True)
            dxn = dh * g_ref[...]
            dx_ref[...] = do_ref[...] + r * (dxn - xn * jnp.mean(dxn * xn, axis=-1, keepdims=True))

    row = pl.BlockSpec((tm, D), lambda i, k: (i, 0))
    return pl.pallas_call(
        body, name=name, grid=(S // tm, nk),
        out_shape=[jax.ShapeDtypeStruct((S, D), f32), jax.ShapeDtypeStruct((1, D), f32)],
        in_specs=[pl.BlockSpec((tm, tk), lambda i, k: (i, k)), pl.BlockSpec((D, tk), lambda i, k: (0, k)), row, row, _full((1, D))],
        out_specs=[row, _full((1, D))],
        scratch_shapes=[pltpu.VMEM((tm, D), f32)],
        compiler_params=_cp(("arbitrary", "arbitrary")))(dproj, wfull, xin, dout, g_pre)


def adam_update(parts, own, me, w, m, v, l, acc, name):
    lw, r, c = w.shape
    tr = r if r <= 128 else 128

    def body(me_ref, p_ref, own_ref, w_ref, m_ref, v_ref, *rest):
        g_ref, d_ref, nm_ref, nv_ref = rest[-4:]
        g = None
        for s in range(NDEV):
            part = jnp.where(me_ref[0] == s, own_ref[0], p_ref[s]).astype(f32)
            g = part if g is None else g + part
        nm = ADAM_B1 * m_ref[0] + (1.0 - ADAM_B1) * g
        nv = ADAM_B2 * v_ref[0] + (1.0 - ADAM_B2) * (g * g)
        m_hat = nm / (1.0 - ADAM_B1 ** ADAM_STEP)
        v_hat = nv / (1.0 - ADAM_B2 ** ADAM_STEP)
        g_ref[0] = g
        d_ref[0] = -ADAM_LR * (m_hat / (jnp.sqrt(v_hat) + ADAM_EPS) + ADAM_WD * w_ref[0])
        nm_ref[0] = nm
        nv_ref[0] = nv

    blk = pl.BlockSpec((1, tr, c), lambda i, me_ref: (l, i, 0))
    sd = jax.ShapeDtypeStruct((lw, r, c), f32)
    extra = [] if acc is None else list(acc)
    return pl.pallas_call(
        body, name=name, out_shape=[sd] * 4,
        grid_spec=pltpu.PrefetchScalarGridSpec(
            num_scalar_prefetch=1, grid=(r // tr,),
            in_specs=[pl.BlockSpec((NDEV, tr, c), lambda i, me_ref: (0, i, 0)),
                      pl.BlockSpec((1, tr, c), lambda i, me_ref: (me_ref[0], i, 0)), blk, blk, blk]
            + [pl.BlockSpec(memory_space=pl.ANY)] * len(extra),
            out_specs=[blk] * 4),
        input_output_aliases={6 + t: t for t in range(len(extra))},
        compiler_params=_cp(("parallel",)))(me, parts, own, w, m, v, *extra)


def _rope_tables():
    t = jnp.arange(S)
    rows, cols = (t // GRID_W).astype(f32), (t % GRID_W).astype(f32)
    nf = HD // 4
    inv = ROPE_THETA ** (-jnp.arange(nf, dtype=f32) / nf)
    ar, ac = rows[:, None] * inv, cols[:, None] * inv
    cos = jnp.concatenate([jnp.cos(ar), jnp.cos(ar), jnp.cos(ac), jnp.cos(ac)], axis=1)
    sin = jnp.concatenate([-jnp.sin(ar), jnp.sin(ar), -jnp.sin(ac), jnp.sin(ac)], axis=1)
    return cos, sin


def _pack_conv(ca, cc):
    z = lambda n: jnp.zeros((L, n, CT), f32)
    return jnp.concatenate([ca, z(5), cc, z(1)], axis=1)


def _pack_small(npre, npost, ccb, lng, lnb, qn, kn):
    wide = lambda a: jnp.pad(a, ((0, 0), (0, D - HD)))
    return jnp.stack([npre, npost, ccb, lng, lnb, wide(qn), wide(kn), jnp.zeros((L, D), f32)], axis=1).reshape(L * 8, D)


def kernel(x, norm_pre, norm_post, w_in, conv_a_w, q_norm, k_norm, conv_c_w, conv_c_b, ln_c_g, ln_c_b, w_out_a, w_out_b, w_out_c, w_o, loss_target, m_norm_pre, m_norm_post, m_w_in, m_conv_a_w, m_q_norm, m_k_norm, m_conv_c_w, m_conv_c_b, m_ln_c_g, m_ln_c_b, m_w_out_a, m_w_out_b, m_w_out_c, m_w_o, v_norm_pre, v_norm_post, v_w_in, v_conv_a_w, v_q_norm, v_k_norm, v_conv_c_w, v_conv_c_b, v_ln_c_g, v_ln_c_b, v_w_out_a, v_w_out_b, v_w_out_c, v_w_o):
    cos, sin = _rope_tables()
    rs = D // NDEV
    stack_sq = lambda a, b, c, d: jnp.stack([a, b, c, d], axis=1)
    wsq32 = stack_sq(w_out_a, w_out_b, w_out_c, w_o)
    conv_pack = _pack_conv(conv_a_w, conv_c_w)
    vec = lambda a, l: a[l][None, :]
    me = (4 * lax.axis_index("x") + 2 * lax.axis_index("y") + lax.axis_index("c")).astype(jnp.int32).reshape(1)

    def forward_start(l, after):
        s_sems, r_sems, bufs, _ = gathers[l]
        bufs = split_wait(s_sems, r_sems, bufs, _gather_copies, after, f"ag_wait{l}")
        return split_start(bufs, _forward_copies, 9, f"ag_fwd_start{l}")

    def forward_wait(fw, after, l):
        s_sems, r_sems, bufs, _ = fw
        return split_wait(s_sems, r_sems, bufs, _forward_copies, after, f"ag_fwd_wait{l}")

    xs, saved = x[0], []
    gathers = [split_start(stage_shards(w_in[l], wsq32[l].reshape(4 * rs, D), conv_pack[l], me, f"stage{l}"),
                           _gather_copies, 12, f"ag_start{l}") for l in range(L)]
    started = sum(g[3][0, 0] for g in gathers)
    fw = forward_start(0, xs)
    wg, wsq, convw = forward_wait(fw, fw[3], 0)
    for l in range(L):
        wsq = wsq.reshape(NDEV, 4, rs, D)
        wfull = relayout_win(wg, f"relayout{l}")
        g_pre = vec(norm_pre, l) + started if l == 0 else vec(norm_pre, l)
        proj, h = proj_fwd(xs, g_pre, wfull, f"proj{l}")
        if l + 1 < L:
            fw = forward_start(l + 1, h)
            convw = convw + fw[3][0, 0]
        yah = brancha_fwd(proj, convw, f"bra{l}")
        u1 = branchc1_fwd(proj, convw, vec(conv_c_b, l), f"brc1_{l}")
        qh, kh, vh = qkv_fwd(proj, vec(q_norm, l), vec(k_norm, l), cos, sin, f"qkv{l}")
        o, ybh = attn_fwd(qh, kh, vh, proj, f"attn{l}")
        ych = branchc2_fwd(u1, proj, vec(ln_c_g, l), vec(ln_c_b, l), f"brc2_{l}")
        ya, yb, yc, y16, z, xo = merge_fwd(xs, yah, ybh, ych, proj, wsq, vec(norm_post, l), f"merge{l}")
        saved.append(dict(x=xs, wfull=wfull, wsq=wsq, convw=convw, proj=proj, h=h, yah=yah, ybh=ybh, ych=ych, u1=u1,
                          qh=qh, kh=kh, vh=vh, o=o, ya=ya, yb=yb, yc=yc, y16=y16, z=z))
        xs = xo
        if l + 1 < L:
            wg, wsq, convw = forward_wait(fw, xs, l + 1)
    dx, loss_part = loss_fwd(xs, loss_target[0], "loss")
    loss = lax.psum(loss_part[0, 0], ("x", "y", "c"))

    acc = dict(win=None, sq=None, conv=None)
    small_parts = [None] * L
    msq32 = stack_sq(m_w_out_a, m_w_out_b, m_w_out_c, m_w_o)
    vsq32 = stack_sq(v_w_out_a, v_w_out_b, v_w_out_c, v_w_o)
    mconv, vconv = _pack_conv(m_conv_a_w, m_conv_c_w), _pack_conv(v_conv_a_w, v_conv_c_w)

    def scatter_start(parts, name):
        bufs = parts + [lax.empty(p.shape, p.dtype) for p in parts]
        return split_start(bufs, _scatter_copies(len(parts)), 7 * len(parts), name)

    def finish(l, started, after):
        (s1, r1, b1, _), (s2, r2, b2, _) = started
        gsq_own, rsq = split_wait(s1, r1, b1, _scatter_copies(1), after, f"rs_sq_wait{l}")
        gwin_own, gconv_own, rwin, rconv = split_wait(s2, r2, b2, _scatter_copies(2), after, f"rs_win_wait{l}")
        flat = lambda a: a.reshape(a.shape[0], 4 * rs, D)
        acc["win"] = adam_update(rwin, gwin_own, me, w_in, m_w_in, v_w_in, l, acc["win"], f"adam_win{l}")
        acc["sq"] = adam_update(flat(rsq), flat(gsq_own), me, flat(wsq32), flat(msq32), flat(vsq32), l, acc["sq"], f"adam_wsq{l}")
        acc["conv"] = adam_update(rconv, gconv_own, me, conv_pack, mconv, vconv, l, acc["conv"], f"adam_conv{l}")

    pending = None
    for l in reversed(range(L)):
        sv = saved[l]
        proj = sv["proj"]
        (dyah, dybh, dych, dma, dmb, dmc, dzb, dyab, dybb, dycb, dgpost) = merge_bwd(
            dx, sv["z"], sv["ya"], sv["yb"], sv["yc"], proj, sv["wsq"], vec(norm_post, l), f"merge_bwd{l}")
        gsq = [tn_matmul(a, b, f"dwsq{t}_{l}") for t, (a, b) in enumerate(
            ((sv["yah"], dyab), (sv["ybh"], dybb), (sv["ych"], dycb), (sv["y16"], dzb)))]
        gsq_parts = jnp.stack([g.reshape(NDEV, rs, D) for g in gsq], axis=1)
        st1 = scatter_start([gsq_parts], f"rs_sq_start{l}")
        convw = sv["convw"] + st1[3][0, 0]
        dab, dac, dax, dag, gca = brancha_bwd(dyah, proj, convw, f"bra_bwd{l}")
        du1, dcg, dlg, dlb, dcb = branchc2_bwd(dych, sv["u1"], proj, vec(ln_c_g, l), vec(ln_c_b, l), f"brc2_bwd{l}")
        dcu, dcv, gcc = branchc1_bwd(du1, proj, convw, f"brc1_bwd{l}")
        dqh, dkh, dvh, dbg = attn_bwd(dybh, sv["o"], sv["qh"], sv["kh"], sv["vh"], proj, f"attn_bwd{l}")
        dq, dk, dv, dqn, dkn = qkv_bwd(dqh, dkh, dvh, proj, vec(q_norm, l), vec(k_norm, l), cos, sin, f"qkv_bwd{l}")
        dproj = jnp.concatenate([dab, dac, dax, dag, dq, dk, dv, dbg, dcu, dcv, dcg, dma, dmb, dmc], axis=1)
        gwin = dwin_parts(sv["h"], dproj, f"dwin{l}")
        gconv = jnp.concatenate([gca, gcc], axis=1)
        st2 = scatter_start([gwin, gconv], f"rs_win_start{l}")
        dx, dgpre = dh_bwd(dproj, sv["wfull"], sv["x"], dx, vec(norm_pre, l) + st2[3][0, 0], f"dh{l}")
        wide = lambda a: jnp.pad(a, ((0, 0), (0, D - HD)))
        small_parts[l] = jnp.concatenate([dgpre, dgpost, dcb, dlg, dlb, wide(dqn), wide(dkn), jnp.zeros((1, D), f32)], axis=0)
        if pending is not None:
            finish(*pending, after=dx)
        pending = (l, (st1, st2))
    finish(*pending, after=dx)

    (small_all,) = all_gather([jnp.concatenate(small_parts, axis=0)], "ag_small")
    sm = adam_update(small_all, small_all, me,
                     _pack_small(norm_pre, norm_post, conv_c_b, ln_c_g, ln_c_b, q_norm, k_norm)[None],
                     _pack_small(m_norm_pre, m_norm_post, m_conv_c_b, m_ln_c_g, m_ln_c_b, m_q_norm, m_k_norm)[None],
                     _pack_small(v_norm_pre, v_norm_post, v_conv_c_b, v_ln_c_g, v_ln_c_b, v_q_norm, v_k_norm)[None],
                     0, None, "adam_small")
    sm = [a.reshape(L, 8, D) for a in sm]
    small_rows = dict(norm_pre=(0, D), norm_post=(1, D), conv_c_b=(2, D), ln_c_g=(3, D), ln_c_b=(4, D), q_norm=(5, HD), k_norm=(6, HD))
    sq_rows = dict(w_out_a=0, w_out_b=1, w_out_c=2, w_o=3)

    order = ["norm_pre", "norm_post", "w_in", "conv_a_w", "q_norm", "k_norm", "conv_c_w", "conv_c_b", "ln_c_g", "ln_c_b",
             "w_out_a", "w_out_b", "w_out_c", "w_o"]
    result = [loss, dx[None]]
    for kind in range(4):
        for nme in order:
            if nme in small_rows:
                rw, wd = small_rows[nme]
                result.append(sm[kind][:, rw, :wd])
            elif nme in sq_rows:
                result.append(acc["sq"][kind][:, sq_rows[nme] * rs:(sq_rows[nme] + 1) * rs])
            elif nme == "w_in":
                result.append(acc["win"][kind])
            elif nme == "conv_a_w":
                result.append(acc["conv"][kind][:, 0:CA_W])
            else:
                result.append(acc["conv"][kind][:, 8:8 + CC_W])
    return tuple(result)
```

```python
import math

import jax
import jax.numpy as jnp
from jax import lax
from jax.experimental import pallas as pl
from jax.experimental.pallas import tpu as pltpu

f32, bf16 = jnp.float32, jnp.bfloat16

D = 1024
S = 2048
L = 4
HD = 128
NQ = D // HD
NKV = NQ // 4
G = NQ // NKV
WKV = NKV * HD
GRID_W = 64
ROPE_THETA = 10000.0
RMS_EPS = 1e-6
LN_EPS = 1e-5
NDEV = 8
CA_W, CC_W = 3, 31
P = 12 * D + 2 * WKV
PSH = P // NDEV
CT = 128
ADAM_LR, ADAM_B1, ADAM_B2, ADAM_EPS, ADAM_WD, ADAM_STEP = 0.001, 0.9, 0.999, 1e-08, 0.01, 10
VMEM_LIMIT = 56 * 1024 * 1024
MESH = pl.DeviceIdType.MESH

_OFF = {}
_o = 0
for _n, _w in (("a_b", D), ("a_c", D), ("a_x", D), ("a_g", D), ("q", D), ("k", WKV), ("v", WKV), ("b_g", D),
               ("c_u", D), ("c_v", D), ("c_g", D), ("m_a", D), ("m_b", D), ("m_c", D)):
    _OFF[_n] = (_o, _w)
    _o += _w
PIECES = tuple(_OFF)


def _cp(sem=None, **kw):
    return pltpu.CompilerParams(dimension_semantics=sem, vmem_limit_bytes=VMEM_LIMIT, **kw)


def _sig(x):
    return 1.0 / (1.0 + jnp.exp(-x))


def _silu(x):
    return x * _sig(x)


def _dsilu(x):
    s = _sig(x)
    return s * (1.0 + x * (1.0 - s))


def _row_specs(name, tm):
    off, w = _OFF[name]
    bw = math.gcd(off, w) if off else w
    return [pl.BlockSpec((tm, bw), (lambda i, *_, b=off // bw + t: (i, b))) for t in range(w // bw)]


def _cat(refs):
    return refs[0][...] if len(refs) == 1 else jnp.concatenate([r[...] for r in refs], axis=1)


def _chan_spec(name):
    off, _ = _OFF[name]
    return pl.BlockSpec((S, CT), lambda j, b=off // CT: (0, b + j))


def _full(shape):
    return pl.BlockSpec(shape, lambda *_: (0,) * len(shape))


def _coords():
    return lax.axis_index("x"), lax.axis_index("y"), lax.axis_index("c")


def all_gather(shards, name):
    n = len(shards)

    def body(*refs):
        ins, outs = refs[:n], refs[n:2 * n]
        send_sems, recv_sems, local_sems = refs[2 * n:]
        x, y, c = _coords()
        me, sibling = (x, y, c), (x, y, 1 - c)
        chips = [(1 - x, y), (x, 1 - y), (1 - x, 1 - y)]

        def slot(a, p):
            return outs[a].at[4 * p[0] + 2 * p[1] + p[2]]

        def copy(a, k, block, to, src=None):
            return pltpu.make_async_remote_copy(
                src_ref=slot(a, block) if src is None else src, dst_ref=slot(a, block),
                send_sem=send_sems.at[7 * a + k], recv_sem=recv_sems.at[7 * a + k], device_id=to, device_id_type=MESH)

        mine = [pltpu.make_async_copy(ins[a], slot(a, me), local_sems.at[a]) for a in range(n)]
        for cp in mine:
            cp.start()
        first = []
        for a in range(n):
            first.append(copy(a, 0, me, sibling, src=ins[a]))
            first += [copy(a, 1 + j, me, (*chip, c), src=ins[a]) for j, chip in enumerate(chips)]
        for cp in first:
            cp.start()
        passed = []
        for j, chip in enumerate(chips):
            for a in range(n):
                copy(a, 1 + j, (*chip, c), me).wait_recv()
                fw = copy(a, 4 + j, (*chip, c), sibling)
                fw.start()
                passed.append(fw)
        for a in range(n):
            copy(a, 0, sibling, me).wait_recv()
            for j, chip in enumerate(chips):
                copy(a, 4 + j, (*chip, 1 - c), me).wait_recv()
        for cp in first + passed:
            cp.wait_send()
        for cp in mine:
            cp.wait()

    anyspec = pl.BlockSpec(memory_space=pl.ANY)
    return pl.pallas_call(
        body, name=name,
        out_shape=[jax.ShapeDtypeStruct((NDEV,) + s.shape, s.dtype) for s in shards],
        in_specs=[anyspec] * n, out_specs=[anyspec] * n,
        scratch_shapes=[pltpu.SemaphoreType.DMA((7 * n,)), pltpu.SemaphoreType.DMA((7 * n,)), pltpu.SemaphoreType.DMA((n,))],
    )(*shards)


_HBM = pl.BlockSpec(memory_space=pltpu.HBM)
_SEM = pl.BlockSpec(memory_space=pltpu.SEMAPHORE)
_EFFECT = pltpu.SideEffectType.DATAFLOW_SIDE_EFFECTING


def split_start(bufs, make_copies, nsem, name):
    n = len(bufs)

    def body(*refs):
        send_sems, recv_sems = refs[n:n + 2]
        for cp in make_copies(refs[:n], send_sems, recv_sems):
            cp.start()
        refs[-1][...] = jnp.zeros((8, 128), f32)

    res = pl.pallas_call(
        body, name=name,
        out_shape=(pltpu.SemaphoreType.DMA((nsem,)), pltpu.SemaphoreType.DMA((nsem,)),
                   *[pltpu.HBM(b.shape, b.dtype) for b in bufs], jax.ShapeDtypeStruct((8, 128), f32)),
        in_specs=[_HBM] * n,
        out_specs=(_SEM, _SEM, *([_HBM] * n), pl.BlockSpec(memory_space=pltpu.VMEM)),
        input_output_aliases={i: 2 + i for i in range(n)},
        compiler_params=pltpu.CompilerParams(has_side_effects=_EFFECT),
    )(*[pltpu.with_memory_space_constraint(b, pltpu.HBM) for b in bufs])
    return res[0], res[1], list(res[2:2 + n]), res[-1]


def split_wait(send_sems, recv_sems, bufs, make_copies, after, name):
    n = len(bufs)

    def body(*refs):
        for cp in make_copies(refs[:n], refs[n], refs[n + 1]):
            cp.wait_send()
            cp.wait_recv()

    res = pl.pallas_call(
        body, name=name,
        out_shape=tuple(pltpu.HBM(b.shape, b.dtype) for b in bufs),
        in_specs=[_HBM] * n + [_SEM, _SEM, pl.BlockSpec(memory_space=pl.ANY)],
        out_specs=[_HBM] * n,
        input_output_aliases={i: i for i in range(n)},
        compiler_params=pltpu.CompilerParams(has_side_effects=_EFFECT),
    )(*bufs, send_sems, recv_sems, after)
    return list(res)


def _scatter_copies(n):
    def make(refs, send_sems, recv_sems):
        x, y, c = _coords()
        me = 4 * x + 2 * y + c
        copies = []
        for a in range(n):
            for k in range(1, NDEV):
                px = 1 - x if (k >> 2) & 1 else x
                py = 1 - y if (k >> 1) & 1 else y
                pc = 1 - c if k & 1 else c
                copies.append(pltpu.make_async_remote_copy(
                    src_ref=refs[a].at[4 * px + 2 * py + pc], dst_ref=refs[n + a].at[me],
                    send_sem=send_sems.at[7 * a + k - 1], recv_sem=recv_sems.at[7 * a + k - 1],
                    device_id=(px, py, pc), device_id_type=MESH))
        return copies
    return make


def _gather_copies(refs, send_sems, recv_sems):
    x, y, c = _coords()
    me = 4 * x + 2 * y + c
    targets = [(x, y, 1 - c), (1 - x, y, c), (x, 1 - y, c), (1 - x, 1 - y, c)]
    return [pltpu.make_async_remote_copy(
        src_ref=r.at[me], dst_ref=r.at[me], send_sem=send_sems.at[4 * a + k], recv_sem=recv_sems.at[4 * a + k],
        device_id=to, device_id_type=MESH) for a, r in enumerate(refs) for k, to in enumerate(targets)]


def _forward_copies(refs, send_sems, recv_sems):
    x, y, c = _coords()
    chips = [(1 - x, y), (x, 1 - y), (1 - x, 1 - y)]
    return [pltpu.make_async_remote_copy(
        src_ref=r.at[4 * px + 2 * py + c], dst_ref=r.at[4 * px + 2 * py + c], send_sem=send_sems.at[3 * a + j],
        recv_sem=recv_sems.at[3 * a + j], device_id=(x, y, 1 - c), device_id_type=MESH)
        for a, r in enumerate(refs) for j, (px, py) in enumerate(chips)]


def _row_tile(r):
    return r if r <= 256 else max(t for t in (256, 160, 128) if r % t == 0)


def stage_shards(wt_l, wsq_l, conv_l, me, name):
    outs = []
    for a, dt in ((wt_l, bf16), (wsq_l, bf16), (conv_l, f32)):
        r, c = a.shape
        tr = _row_tile(r)

        def body(me_ref, a_ref, o_ref):
            o_ref[0] = a_ref[...].astype(o_ref.dtype)

        outs.append(pl.pallas_call(
            body, name=f"{name}_{len(outs)}", out_shape=jax.ShapeDtypeStruct((NDEV, r, c), dt),
            grid_spec=pltpu.PrefetchScalarGridSpec(
                num_scalar_prefetch=1, grid=(r // tr,),
                in_specs=[pl.BlockSpec((tr, c), lambda i, m: (i, 0))],
                out_specs=pl.BlockSpec((1, tr, c), lambda i, m: (m[0], i, 0))),
            compiler_params=_cp(("arbitrary",)))(me, a))
    return outs


def proj_fwd(xin, g_pre, wt, name):
    tm, tn = min(S, 512), 1280

    def body(x_ref, g_ref, w_ref, proj_ref, h_ref, hs):
        @pl.when(pl.program_id(1) == 0)
        def _():
            x = x_ref[...]
            r = lax.rsqrt(jnp.mean(x * x, axis=-1, keepdims=True) + RMS_EPS)
            h = (x * r * g_ref[...]).astype(bf16)
            hs[...] = h
            h_ref[...] = h
        proj_ref[...] = lax.dot_general(hs[...], w_ref[...], (((1,), (1,)), ((), ())), preferred_element_type=f32)

    return pl.pallas_call(
        body, name=name, grid=(S // tm, P // tn),
        out_shape=[jax.ShapeDtypeStruct((S, P), f32), jax.ShapeDtypeStruct((S, D), bf16)],
        in_specs=[pl.BlockSpec((tm, D), lambda i, j: (i, 0)), _full((1, D)), pl.BlockSpec((tn, D), lambda i, j: (j, 0))],
        out_specs=[pl.BlockSpec((tm, tn), lambda i, j: (i, j)), pl.BlockSpec((tm, D), lambda i, j: (i, 0))],
        scratch_shapes=[pltpu.VMEM((tm, D), bf16)],
        compiler_params=_cp(("parallel", "arbitrary")))(xin, g_pre, wt)


RC = 128


def _fill_pad(pad, halo, val_fn):
    pad[0:halo, :] = jnp.zeros((halo, CT), f32)
    pad[S + halo:S + 2 * halo, :] = jnp.zeros((halo, CT), f32)

    def step(i, carry):
        rows = pl.ds(pl.multiple_of(i * RC, RC), RC)
        pad[pl.ds(pl.multiple_of(i * RC, RC) + halo, RC), :] = val_fn(rows)
        return carry
    lax.fori_loop(0, S // RC, step, 0)


def brancha_fwd(proj, convw, name):
    def body(ab, ac, ax, ag, w_ref, o_ref, pad):
        _fill_pad(pad, 8, lambda rows: ac[rows, :] * ax[rows, :])
        w = [w_ref[0, k:k + 1, :] for k in range(CA_W)]

        def step(i, carry):
            base = pl.multiple_of(i * RC, RC)
            rows = pl.ds(base, RC)
            t = sum(w[k] * pad[pl.ds(base + 7 + k, RC), :] for k in range(CA_W))
            o_ref[rows, :] = (ab[rows, :] * t * _silu(ag[rows, :])).astype(bf16)
            return carry
        lax.fori_loop(0, S // RC, step, 0)

    return pl.pallas_call(
        body, name=name, grid=(D // CT,), out_shape=jax.ShapeDtypeStruct((S, D), bf16),
        in_specs=[_chan_spec("a_b"), _chan_spec("a_c"), _chan_spec("a_x"), _chan_spec("a_g"),
                  pl.BlockSpec((1, 40, CT), lambda j: (j, 0, 0))],
        out_specs=pl.BlockSpec((S, CT), lambda j: (0, j)),
        scratch_shapes=[pltpu.VMEM((S + 16, CT), f32)],
        compiler_params=_cp(("parallel",)))(proj, proj, proj, proj, convw)


def branchc1_fwd(proj, convw, cbias, name):
    def body(cu, cv, w_ref, b_ref, o_ref, pad):
        _fill_pad(pad, 16, lambda rows: cu[rows, :] * _sig(cv[rows, :]))

        def step(i, carry):
            base = pl.multiple_of(i * RC, RC)
            acc = jnp.zeros((RC, CT), f32) + b_ref[...]
            for k in range(CC_W):
                acc = acc + w_ref[0, 8 + k:9 + k, :] * pad[pl.ds(base + k + 1, RC), :]
            o_ref[pl.ds(base, RC), :] = acc
            return carry
        lax.fori_loop(0, S // RC, step, 0)

    return pl.pallas_call(
        body, name=name, grid=(D // CT,), out_shape=jax.ShapeDtypeStruct((S, D), f32),
        in_specs=[_chan_spec("c_u"), _chan_spec("c_v"), pl.BlockSpec((1, 40, CT), lambda j: (j, 0, 0)),
                  pl.BlockSpec((1, CT), lambda j: (0, j))],
        out_specs=pl.BlockSpec((S, CT), lambda j: (0, j)),
        scratch_shapes=[pltpu.VMEM((S + 32, CT), f32)],
        compiler_params=_cp(("parallel",)))(proj, proj, convw, cbias)


def _swap32(x):
    lane = lax.broadcasted_iota(jnp.int32, x.shape, 1)
    return jnp.where((lane // 32) % 2 == 1, pltpu.roll(x, 32, 1), pltpu.roll(x, HD - 32, 1))


def _rope(y, cos, sin):
    return y * cos + _swap32(y) * sin


def qkv_fwd(proj, qn, kn, cos, sin, name):
    tm = min(S, 256)
    nq, nk, nv = len(_row_specs("q", tm)), len(_row_specs("k", tm)), len(_row_specs("v", tm))

    def body(*refs):
        q = _cat(refs[:nq])
        k = _cat(refs[nq:nq + nk])
        v = _cat(refs[nq + nk:nq + nk + nv])
        qn_ref, kn_ref, cos_ref, sin_ref, qh_ref, kh_ref, vh_ref = refs[nq + nk + nv:]
        cos, sin = cos_ref[...], sin_ref[...]

        def heads(xx, gn, out_ref, n):
            for h in range(n):
                xh = xx[:, h * HD:(h + 1) * HD]
                r = lax.rsqrt(jnp.mean(xh * xh, axis=-1, keepdims=True) + RMS_EPS)
                out_ref[:, h * HD:(h + 1) * HD] = _rope(xh * r * gn, cos, sin).astype(bf16)
        heads(q, qn_ref[...], qh_ref, NQ)
        heads(k, kn_ref[...], kh_ref, NKV)
        vh_ref[...] = v.astype(bf16)

    row = lambda w: pl.BlockSpec((tm, w), lambda i: (i, 0))
    return pl.pallas_call(
        body, name=name, grid=(S // tm,),
        out_shape=[jax.ShapeDtypeStruct((S, D), bf16), jax.ShapeDtypeStruct((S, WKV), bf16), jax.ShapeDtypeStruct((S, WKV), bf16)],
        in_specs=_row_specs("q", tm) + _row_specs("k", tm) + _row_specs("v", tm) + [_full((1, HD)), _full((1, HD)), row(HD), row(HD)],
        out_specs=[row(D), row(WKV), row(WKV)],
        compiler_params=_cp(("parallel",)))(*([proj] * (nq + nk + nv)), qn, kn, cos, sin)


def _softmax_rows(q, k):
    s = lax.dot_general(q, k, (((1,), (1,)), ((), ())), preferred_element_type=f32)
    p = jnp.exp((s - jnp.max(s, axis=-1, keepdims=True)) * (HD ** -0.5))
    return p, 1.0 / jnp.sum(p, axis=-1, keepdims=True)


GW = G * HD


def attn_fwd(qh, kh, vh, proj, name):
    tq = min(S, 256)
    bg_blk = _OFF["b_g"][0] // GW

    def body(q_ref, k_ref, v_ref, bg_ref, o_ref, y_ref):
        k, v = k_ref[...], v_ref[...]
        for g in range(G):
            cols = slice(g * HD, (g + 1) * HD)
            p, rl = _softmax_rows(q_ref[:, cols], k)
            o = jnp.dot(p.astype(bf16), v, preferred_element_type=f32) * rl
            o_ref[:, cols] = o
            y_ref[:, cols] = (o * _silu(bg_ref[:, cols])).astype(bf16)

    grp = pl.BlockSpec((tq, GW), lambda kv, i: (i, kv))
    kvs = pl.BlockSpec((S, HD), lambda kv, i: (0, kv))
    return pl.pallas_call(
        body, name=name, grid=(NKV, S // tq),
        out_shape=[jax.ShapeDtypeStruct((S, D), f32), jax.ShapeDtypeStruct((S, D), bf16)],
        in_specs=[grp, kvs, kvs, pl.BlockSpec((tq, GW), lambda kv, i: (i, bg_blk + kv))],
        out_specs=[grp, grp],
        compiler_params=_cp(("parallel", "parallel")))(qh, kh, vh, proj)


def _ln_parts(u1):
    mu = jnp.mean(u1, axis=-1, keepdims=True)
    xc = u1 - mu
    rstd = lax.rsqrt(jnp.mean(xc * xc, axis=-1, keepdims=True) + LN_EPS)
    return xc * rstd, rstd


def branchc2_fwd(u1, proj, lng, lnb, name):
    tm = min(S, 256)
    ncg = len(_row_specs("c_g", tm))

    def body(*refs):
        u_ref = refs[0]
        cg = _cat(refs[1:1 + ncg])
        g_ref, b_ref, o_ref = refs[1 + ncg:]
        xh, _ = _ln_parts(u_ref[...])
        o_ref[...] = (_silu(xh * g_ref[...] + b_ref[...]) * _silu(cg)).astype(bf16)

    row = pl.BlockSpec((tm, D), lambda i: (i, 0))
    return pl.pallas_call(
        body, name=name, grid=(S // tm,), out_shape=jax.ShapeDtypeStruct((S, D), bf16),
        in_specs=[row] + _row_specs("c_g", tm) + [_full((1, D)), _full((1, D))], out_specs=row,
        compiler_params=_cp(("parallel",)))(u1, *([proj] * ncg), lng, lnb)


def _wmat(w_ref, kind):
    return w_ref[:, kind].reshape(D, D)


def merge_fwd(xin, yah, ybh, ych, proj, wsq, g_post, name):
    tm = min(S, 256)
    nm = len(_row_specs("m_a", tm))

    def body(*refs):
        x_ref, a_ref, b_ref, c_ref = refs[:4]
        ms = [_cat(refs[4 + t * nm:4 + (t + 1) * nm]) for t in range(3)]
        w_ref, g_ref, ya_ref, yb_ref, yc_ref, y_ref, z_ref, o_ref = refs[4 + 3 * nm:]
        y = jnp.zeros((tm, D), f32)
        for t, (h_ref, out_ref) in enumerate(((a_ref, ya_ref), (b_ref, yb_ref), (c_ref, yc_ref))):
            yt = jnp.dot(h_ref[...], _wmat(w_ref, t), preferred_element_type=f32)
            out_ref[...] = yt
            y = y + _sig(ms[t]) * yt
        yb16 = y.astype(bf16)
        y_ref[...] = yb16
        z = jnp.dot(yb16, _wmat(w_ref, 3), preferred_element_type=f32)
        z_ref[...] = z
        r = lax.rsqrt(jnp.mean(z * z, axis=-1, keepdims=True) + RMS_EPS)
        o_ref[...] = x_ref[...] + z * r * g_ref[...]

    row = pl.BlockSpec((tm, D), lambda i: (i, 0))
    sd = lambda dt: jax.ShapeDtypeStruct((S, D), dt)
    return pl.pallas_call(
        body, name=name, grid=(S // tm,),
        out_shape=[sd(f32), sd(f32), sd(f32), sd(bf16), sd(f32), sd(f32)],
        in_specs=[row] * 4 + _row_specs("m_a", tm) + _row_specs("m_b", tm) + _row_specs("m_c", tm)
        + [_full((NDEV, 4, D // NDEV, D)), _full((1, D))],
        out_specs=[row] * 6,
        compiler_params=_cp(("parallel",)))(xin, yah, ybh, ych, *([proj] * (3 * nm)), wsq, g_post)


def loss_fwd(y, target, name):
    tm = min(S, 256)

    def body(y_ref, t_ref, dy_ref, l_ref):
        e = y_ref[...] - t_ref[...]
        dy_ref[...] = e / D

        @pl.when(pl.program_id(0) == 0)
        def _():
            l_ref[...] = jnp.zeros((1, 128), f32)
        l_ref[...] += (0.5 / D) * jnp.sum(e * e)

    row = pl.BlockSpec((tm, D), lambda i: (i, 0))
    return pl.pallas_call(
        body, name=name, grid=(S // tm,),
        out_shape=[jax.ShapeDtypeStruct((S, D), f32), jax.ShapeDtypeStruct((1, 128), f32)],
        in_specs=[row, row], out_specs=[row, _full((1, 128))],
        compiler_params=_cp(("arbitrary",)))(y, target)


def _acc(ref, val):
    @pl.when(pl.program_id(0) == 0)
    def _():
        ref[...] = jnp.zeros(ref.shape, f32)
    ref[...] += val


def merge_bwd(dout, z, ya, yb, yc, proj, wsq, g_post, name):
    tm = min(S, 256)
    nm = len(_row_specs("m_a", tm))

    def body(*refs):
        do_ref, z_ref, ya_ref, yb_ref, yc_ref = refs[:5]
        ms = [_cat(refs[5 + t * nm:5 + (t + 1) * nm]) for t in range(3)]
        w_ref, g_ref = refs[5 + 3 * nm:7 + 3 * nm]
        dh_refs = refs[7 + 3 * nm:10 + 3 * nm]
        dm_refs = refs[10 + 3 * nm:13 + 3 * nm]
        dzb_ref = refs[13 + 3 * nm]
        dyb_refs = refs[14 + 3 * nm:17 + 3 * nm]
        dg_ref = refs[17 + 3 * nm]
        nt = (((1,), (1,)), ((), ()))
        z, dout = z_ref[...], do_ref[...]
        r = lax.rsqrt(jnp.mean(z * z, axis=-1, keepdims=True) + RMS_EPS)
        zh = z * r
        _acc(dg_ref, jnp.sum(dout * zh, axis=0, keepdims=True))
        dzh = dout * g_ref[...]
        dz = (r * (dzh - zh * jnp.mean(dzh * zh, axis=-1, keepdims=True))).astype(bf16)
        dzb_ref[...] = dz
        dy = lax.dot_general(dz, _wmat(w_ref, 3), nt, preferred_element_type=f32)
        for t, yt_ref in enumerate((ya_ref, yb_ref, yc_ref)):
            sg = _sig(ms[t])
            dyt = (dy * sg).astype(bf16)
            dyb_refs[t][...] = dyt
            dm_refs[t][...] = (dy * yt_ref[...] * sg * (1.0 - sg)).astype(bf16)
            dh_refs[t][...] = lax.dot_general(dyt, _wmat(w_ref, t), nt, preferred_element_type=f32)

    row = pl.BlockSpec((tm, D), lambda i: (i, 0))
    sd = lambda dt: jax.ShapeDtypeStruct((S, D), dt)
    return pl.pallas_call(
        body, name=name, grid=(S // tm,),
        out_shape=[sd(f32)] * 3 + [sd(bf16)] * 7 + [jax.ShapeDtypeStruct((1, D), f32)],
        in_specs=[row] * 5 + _row_specs("m_a", tm) + _row_specs("m_b", tm) + _row_specs("m_c", tm)
        + [_full((NDEV, 4, D // NDEV, D)), _full((1, D))],
        out_specs=[row] * 10 + [_full((1, D))],
        compiler_params=_cp(("arbitrary",)))(dout, z, ya, yb, yc, *([proj] * (3 * nm)), wsq, g_post)


def tn_matmul(a, b, name):
    m, n = a.shape[1], b.shape[1]
    tmm = min(m, 512)

    def body(a_ref, b_ref, o_ref):
        o_ref[...] = lax.dot_general(a_ref[...], b_ref[...], (((0,), (0,)), ((), ())), preferred_element_type=f32).astype(bf16)

    return pl.pallas_call(
        body, name=name, grid=(m // tmm,), out_shape=jax.ShapeDtypeStruct((m, n), bf16),
        in_specs=[pl.BlockSpec((S, tmm), lambda i: (0, i)), _full((S, n))],
        out_specs=pl.BlockSpec((tmm, n), lambda i: (i, 0)),
        compiler_params=_cp(("parallel",)))(a, b)


def dwin_parts(h, dproj, name):
    tn = 640

    def body(d_ref, h_ref, o_ref):
        o_ref[...] = lax.dot_general(d_ref[...], h_ref[...], (((0,), (0,)), ((), ())), preferred_element_type=f32).astype(bf16)

    return pl.pallas_call(
        body, name=name, grid=(P // tn,), out_shape=jax.ShapeDtypeStruct((P, D), bf16),
        in_specs=[pl.BlockSpec((S, tn), lambda j: (0, j)), _full((S, D))],
        out_specs=pl.BlockSpec((tn, D), lambda j: (j, 0)),
        compiler_params=_cp(("parallel",)))(dproj, h)


def brancha_bwd(dyah, proj, convw, name):
    def body(d_ref, ab, ac, ax, ag, w_ref, dab, dac, dax, dag, dw_ref, padp, padt, accw):
        _fill_pad(padp, 8, lambda rows: ac[rows, :] * ax[rows, :])
        _fill_pad(padt, 8, lambda rows: d_ref[rows, :] * ab[rows, :] * _silu(ag[rows, :]))
        accw[...] = jnp.zeros(accw.shape, f32)
        w = [w_ref[0, k:k + 1, :] for k in range(CA_W)]

        def step(i, carry):
            base = pl.multiple_of(i * RC, RC)
            rows = pl.ds(base, RC)
            ps = [padp[pl.ds(base + 7 + k, RC), :] for k in range(CA_W)]
            t = sum(w[k] * ps[k] for k in range(CA_W))
            dp = sum(w[k] * padt[pl.ds(base + 9 - k, RC), :] for k in range(CA_W))
            d, a_b, a_g = d_ref[rows, :], ab[rows, :], ag[rows, :]
            dab[rows, :] = (d * t * _silu(a_g)).astype(bf16)
            dag[rows, :] = (d * a_b * t * _dsilu(a_g)).astype(bf16)
            dac[rows, :] = (dp * ax[rows, :]).astype(bf16)
            dax[rows, :] = (dp * ac[rows, :]).astype(bf16)
            dt = padt[pl.ds(base + 8, RC), :]
            for k in range(CA_W):
                accw[8 * k:8 * k + 8, :] += jnp.sum((dt * ps[k]).reshape(RC // 8, 8, CT), axis=0)
            return carry
        lax.fori_loop(0, S // RC, step, 0)
        dw_ref[0] = jnp.zeros((8, CT), f32)
        for k in range(CA_W):
            dw_ref[0, k:k + 1, :] = jnp.sum(accw[8 * k:8 * k + 8, :], axis=0, keepdims=True)

    tile = pl.BlockSpec((S, CT), lambda j: (0, j))
    sd = jax.ShapeDtypeStruct((S, D), bf16)
    return pl.pallas_call(
        body, name=name, grid=(D // CT,),
        out_shape=[sd, sd, sd, sd, jax.ShapeDtypeStruct((NDEV, 8, CT), f32)],
        in_specs=[tile, _chan_spec("a_b"), _chan_spec("a_c"), _chan_spec("a_x"), _chan_spec("a_g"),
                  pl.BlockSpec((1, 40, CT), lambda j: (j, 0, 0))],
        out_specs=[tile] * 4 + [pl.BlockSpec((1, 8, CT), lambda j: (j, 0, 0))],
        scratch_shapes=[pltpu.VMEM((S + 16, CT), f32), pltpu.VMEM((S + 16, CT), f32), pltpu.VMEM((8 * CA_W, CT), f32)],
        compiler_params=_cp(("parallel",)))(dyah, proj, proj, proj, proj, convw)


def branchc2_bwd(dych, u1, proj, lng, lnb, name):
    tm = min(S, 256)
    ncg = len(_row_specs("c_g", tm))

    def body(*refs):
        d_ref, u_ref = refs[:2]
        cg = _cat(refs[2:2 + ncg])
        g_ref, b_ref, du_ref, dcg_ref, dlg_ref, dlb_ref, dcb_ref = refs[2 + ncg:]
        d = d_ref[...]
        xh, rstd = _ln_parts(u_ref[...])
        ln = xh * g_ref[...] + b_ref[...]
        dcg_ref[...] = (d * _silu(ln) * _dsilu(cg)).astype(bf16)
        dln = d * _silu(cg) * _dsilu(ln)
        _acc(dlg_ref, jnp.sum(dln * xh, axis=0, keepdims=True))
        _acc(dlb_ref, jnp.sum(dln, axis=0, keepdims=True))
        dxh = dln * g_ref[...]
        du = rstd * (dxh - jnp.mean(dxh, axis=-1, keepdims=True) - xh * jnp.mean(dxh * xh, axis=-1, keepdims=True))
        du_ref[...] = du
        _acc(dcb_ref, jnp.sum(du, axis=0, keepdims=True))

    row = pl.BlockSpec((tm, D), lambda i: (i, 0))
    vec = jax.ShapeDtypeStruct((1, D), f32)
    return pl.pallas_call(
        body, name=name, grid=(S // tm,),
        out_shape=[jax.ShapeDtypeStruct((S, D), f32), jax.ShapeDtypeStruct((S, D), bf16), vec, vec, vec],
        in_specs=[row, row] + _row_specs("c_g", tm) + [_full((1, D)), _full((1, D))],
        out_specs=[row, row, _full((1, D)), _full((1, D)), _full((1, D))],
        compiler_params=_cp(("arbitrary",)))(dych, u1, *([proj] * ncg), lng, lnb)


def branchc1_bwd(du1, proj, convw, name):
    def body(d_ref, cu, cv, w_ref, dcu, dcv, dw_ref, padu, padd, accw):
        _fill_pad(padu, 16, lambda rows: cu[rows, :] * _sig(cv[rows, :]))
        _fill_pad(padd, 16, lambda rows: d_ref[rows, :])
        accw[...] = jnp.zeros(accw.shape, f32)

        def step(i, carry):
            base = pl.multiple_of(i * RC, RC)
            rows = pl.ds(base, RC)
            d = d_ref[rows, :]
            du0 = jnp.zeros((RC, CT), f32)
            for k in range(CC_W):
                du0 = du0 + w_ref[0, 8 + k:9 + k, :] * padd[pl.ds(base + 31 - k, RC), :]
                accw[8 * k:8 * k + 8, :] += jnp.sum((d * padu[pl.ds(base + k + 1, RC), :]).reshape(RC // 8, 8, CT), axis=0)
            sg = _sig(cv[rows, :])
            dcu[rows, :] = (du0 * sg).astype(bf16)
            dcv[rows, :] = (du0 * cu[rows, :] * sg * (1.0 - sg)).astype(bf16)
            return carry
        lax.fori_loop(0, S // RC, step, 0)
        dw_ref[0] = jnp.zeros((32, CT), f32)
        for k in range(CC_W):
            dw_ref[0, k:k + 1, :] = jnp.sum(accw[8 * k:8 * k + 8, :], axis=0, keepdims=True)

    tile = pl.BlockSpec((S, CT), lambda j: (0, j))
    sd = jax.ShapeDtypeStruct((S, D), bf16)
    return pl.pallas_call(
        body, name=name, grid=(D // CT,),
        out_shape=[sd, sd, jax.ShapeDtypeStruct((NDEV, 32, CT), f32)],
        in_specs=[tile, _chan_spec("c_u"), _chan_spec("c_v"), pl.BlockSpec((1, 40, CT), lambda j: (j, 0, 0))],
        out_specs=[tile, tile, pl.BlockSpec((1, 32, CT), lambda j: (j, 0, 0))],
        scratch_shapes=[pltpu.VMEM((S + 32, CT), f32), pltpu.VMEM((S + 32, CT), f32), pltpu.VMEM((8 * 32, CT), f32)],
        compiler_params=_cp(("parallel",)))(du1, proj, proj, convw)


def attn_bwd(dybh, o, qh, kh, vh, proj, name):
    tq = min(S, 256)
    bg_blk = _OFF["b_g"][0] // GW

    def body(d_ref, o_ref, q_ref, k_ref, v_ref, bg_ref, dq_ref, dk_ref, dv_ref, dbg_ref):
        @pl.when(pl.program_id(1) == 0)
        def _():
            dk_ref[...] = jnp.zeros(dk_ref.shape, f32)
            dv_ref[...] = jnp.zeros(dv_ref.shape, f32)
        k, v = k_ref[...], v_ref[...]
        tn = (((0,), (0,)), ((), ()))
        dk_acc = jnp.zeros((S, HD), f32)
        dv_acc = jnp.zeros((S, HD), f32)
        for g in range(G):
            cols = slice(g * HD, (g + 1) * HD)
            d, bg, q = d_ref[:, cols], bg_ref[:, cols], q_ref[:, cols]
            dbg_ref[:, cols] = (d * o_ref[:, cols] * _dsilu(bg)).astype(bf16)
            do = d * _silu(bg)
            p, rl = _softmax_rows(q, k)
            dv_acc = dv_acc + lax.dot_general(p.astype(bf16), (do * rl).astype(bf16), tn, preferred_element_type=f32)
            dp = lax.dot_general(do.astype(bf16), v, (((1,), (1,)), ((), ())), preferred_element_type=f32)
            delta = jnp.sum(p * dp, axis=-1, keepdims=True) * rl
            ds = (p * (dp - delta)).astype(bf16)
            rs_ = rl * (HD ** -0.5)
            dq_ref[:, cols] = jnp.dot(ds, k, preferred_element_type=f32) * rs_
            dk_acc = dk_acc + lax.dot_general(ds, (q.astype(f32) * rs_).astype(bf16), tn, preferred_element_type=f32)
        dk_ref[...] += dk_acc
        dv_ref[...] += dv_acc

    grp = pl.BlockSpec((tq, GW), lambda kv, i: (i, kv))
    kvs = pl.BlockSpec((S, HD), lambda kv, i: (0, kv))
    return pl.pallas_call(
        body, name=name, grid=(NKV, S // tq),
        out_shape=[jax.ShapeDtypeStruct((S, D), f32), jax.ShapeDtypeStruct((S, WKV), f32),
                   jax.ShapeDtypeStruct((S, WKV), f32), jax.ShapeDtypeStruct((S, D), bf16)],
        in_specs=[grp, grp, grp, kvs, kvs, pl.BlockSpec((tq, GW), lambda kv, i: (i, bg_blk + kv))],
        out_specs=[grp, kvs, kvs, grp],
        compiler_params=_cp(("parallel", "arbitrary")))(dybh, o, qh, kh, vh, proj)


def qkv_bwd(dqh, dkh, dvh, proj, qn, kn, cos, sin, name):
    tm = min(S, 256)
    nq, nk = len(_row_specs("q", tm)), len(_row_specs("k", tm))

    def body(*refs):
        dqh_ref, dkh_ref, dvh_ref = refs[:3]
        q = _cat(refs[3:3 + nq])
        k = _cat(refs[3 + nq:3 + nq + nk])
        qn_ref, kn_ref, cos_ref, sin_ref, dq_ref, dk_ref, dv_ref, dqn_ref, dkn_ref = refs[3 + nq + nk:]
        cos, sin = cos_ref[...], sin_ref[...]

        def heads(xx, dd, gn, out_ref, dgn_ref, n):
            dg = jnp.zeros((1, HD), f32)
            for h in range(n):
                xh = xx[:, h * HD:(h + 1) * HD]
                dh = dd[:, h * HD:(h + 1) * HD]
                r = lax.rsqrt(jnp.mean(xh * xh, axis=-1, keepdims=True) + RMS_EPS)
                xn = xh * r
                dy = dh * cos + _swap32(dh * sin)
                dg = dg + jnp.sum(dy * xn, axis=0, keepdims=True)
                dxn = dy * gn
                out_ref[:, h * HD:(h + 1) * HD] = (r * (dxn - xn * jnp.mean(dxn * xn, axis=-1, keepdims=True))).astype(bf16)
            _acc(dgn_ref, dg)
        heads(q, dqh_ref[...], qn_ref[...], dq_ref, dqn_ref, NQ)
        heads(k, dkh_ref[...], kn_ref[...], dk_ref, dkn_ref, NKV)
        dv_ref[...] = dvh_ref[...].astype(bf16)

    row = lambda w: pl.BlockSpec((tm, w), lambda i: (i, 0))
    vec = jax.ShapeDtypeStruct((1, HD), f32)
    return pl.pallas_call(
        body, name=name, grid=(S // tm,),
        out_shape=[jax.ShapeDtypeStruct((S, D), bf16), jax.ShapeDtypeStruct((S, WKV), bf16),
                   jax.ShapeDtypeStruct((S, WKV), bf16), vec, vec],
        in_specs=[row(D), row(WKV), row(WKV)] + _row_specs("q", tm) + _row_specs("k", tm)
        + [_full((1, HD)), _full((1, HD)), row(HD), row(HD)],
        out_specs=[row(D), row(WKV), row(WKV), _full((1, HD)), _full((1, HD))],
        compiler_params=_cp(("arbitrary",)))(dqh, dkh, dvh, *([proj] * (nq + nk)), qn, kn, cos, sin)


def dh_bwd(dproj, wfull, xin, dout, g_pre, name):
    tm, tk = min(S, 512), 1280
    nk = P // tk

    def body(d_ref, w_ref, x_ref, do_ref, g_ref, dx_ref, dg_ref, acc):
        kk = pl.program_id(1)

        @pl.when(kk == 0)
        def _():
            acc[...] = jnp.zeros(acc.shape, f32)
        acc[...] += jnp.dot(d_ref[...], w_ref[...], preferred_element_type=f32)

        @pl.when((kk == 0) & (pl.program_id(0) == 0))
        def _():
            dg_ref[...] = jnp.zeros(dg_ref.shape, f32)

        @pl.when(kk == nk - 1)
        def _():
            x, dh = x_ref[...], acc[...]
            r = lax.rsqrt(jnp.mean(x * x, axis=-1, keepdims=True) + RMS_EPS)
            xn = x * r
            dg_ref[...] += jnp.sum(dh * xn, axis=0, keepdims=True)
            dxn = dh * g_ref[...]
            dx_ref[...] = do_ref[...] + r * (dxn - xn * jnp.mean(dxn * xn, axis=-1, keepdims=True))

    row = pl.BlockSpec((tm, D), lambda i, k: (i, 0))
    return pl.pallas_call(
        body, name=name, grid=(S // tm, nk),
        out_shape=[jax.ShapeDtypeStruct((S, D), f32), jax.ShapeDtypeStruct((1, D), f32)],
        in_specs=[pl.BlockSpec((tm, tk), lambda i, k: (i, k)), pl.BlockSpec((tk, D), lambda i, k: (k, 0)), row, row, _full((1, D))],
        out_specs=[row, _full((1, D))],
        scratch_shapes=[pltpu.VMEM((tm, D), f32)],
        compiler_params=_cp(("arbitrary", "arbitrary")))(dproj, wfull, xin, dout, g_pre)


def adam_update(parts, own, me, w, m, v, l, acc, name):
    lw, r, c = w.shape
    tr = _row_tile(r)

    def body(me_ref, p_ref, own_ref, w_ref, m_ref, v_ref, *rest):
        g_ref, d_ref, nm_ref, nv_ref = rest[-4:]
        g = None
        for s in range(NDEV):
            part = jnp.where(me_ref[0] == s, own_ref[0], p_ref[s]).astype(f32)
            g = part if g is None else g + part
        nm = ADAM_B1 * m_ref[0] + (1.0 - ADAM_B1) * g
        nv = ADAM_B2 * v_ref[0] + (1.0 - ADAM_B2) * (g * g)
        m_hat = nm / (1.0 - ADAM_B1 ** ADAM_STEP)
        v_hat = nv / (1.0 - ADAM_B2 ** ADAM_STEP)
        g_ref[0] = g
        d_ref[0] = -ADAM_LR * (m_hat / (jnp.sqrt(v_hat) + ADAM_EPS) + ADAM_WD * w_ref[0])
        nm_ref[0] = nm
        nv_ref[0] = nv

    blk = pl.BlockSpec((1, tr, c), lambda i, me_ref: (l, i, 0))
    sd = jax.ShapeDtypeStruct((lw, r, c), f32)
    extra = [] if acc is None else list(acc)
    return pl.pallas_call(
        body, name=name, out_shape=[sd] * 4,
        grid_spec=pltpu.PrefetchScalarGridSpec(
            num_scalar_prefetch=1, grid=(r // tr,),
            in_specs=[pl.BlockSpec((NDEV, tr, c), lambda i, me_ref: (0, i, 0)),
                      pl.BlockSpec((1, tr, c), lambda i, me_ref: (me_ref[0], i, 0)), blk, blk, blk]
            + [pl.BlockSpec(memory_space=pl.ANY)] * len(extra),
            out_specs=[blk] * 4),
        input_output_aliases={6 + t: t for t in range(len(extra))},
        compiler_params=_cp(("parallel",)))(me, parts, own, w, m, v, *extra)


def _rope_tables():
    t = jnp.arange(S)
    rows, cols = (t // GRID_W).astype(f32), (t % GRID_W).astype(f32)
    nf = HD // 4
    inv = ROPE_THETA ** (-jnp.arange(nf, dtype=f32) / nf)
    ar, ac = rows[:, None] * inv, cols[:, None] * inv
    cos = jnp.concatenate([jnp.cos(ar), jnp.cos(ar), jnp.cos(ac), jnp.cos(ac)], axis=1)
    sin = jnp.concatenate([-jnp.sin(ar), jnp.sin(ar), -jnp.sin(ac), jnp.sin(ac)], axis=1)
    return cos, sin


def _pack_conv(ca, cc):
    z = lambda n: jnp.zeros((L, n, CT), f32)
    return jnp.concatenate([ca, z(5), cc, z(1)], axis=1)


def _pack_small(npre, npost, ccb, lng, lnb, qn, kn):
    wide = lambda a: jnp.pad(a, ((0, 0), (0, D - HD)))
    return jnp.stack([npre, npost, ccb, lng, lnb, wide(qn), wide(kn), jnp.zeros((L, D), f32)], axis=1).reshape(L * 8, D)


def kernel(x, norm_pre, norm_post, w_in, conv_a_w, q_norm, k_norm, conv_c_w, conv_c_b, ln_c_g, ln_c_b, w_out_a, w_out_b, w_out_c, w_o, loss_target, m_norm_pre, m_norm_post, m_w_in, m_conv_a_w, m_q_norm, m_k_norm, m_conv_c_w, m_conv_c_b, m_ln_c_g, m_ln_c_b, m_w_out_a, m_w_out_b, m_w_out_c, m_w_o, v_norm_pre, v_norm_post, v_w_in, v_conv_a_w, v_q_norm, v_k_norm, v_conv_c_w, v_conv_c_b, v_ln_c_g, v_ln_c_b, v_w_out_a, v_w_out_b, v_w_out_c, v_w_o):
    cos, sin = _rope_tables()
    rs = D // NDEV
    stack_sq = lambda a, b, c, d: jnp.stack([a, b, c, d], axis=1)
    wsq32 = stack_sq(w_out_a, w_out_b, w_out_c, w_o)
    conv_pack = _pack_conv(conv_a_w, conv_c_w)
    vec = lambda a, l: a[l][None, :]
    me = (4 * lax.axis_index("x") + 2 * lax.axis_index("y") + lax.axis_index("c")).astype(jnp.int32).reshape(1)

    def forward_start(l, after):
        s_sems, r_sems, bufs, _ = gathers[l]
        bufs = split_wait(s_sems, r_sems, bufs, _gather_copies, after, f"ag_wait{l}")
        return split_start(bufs, _forward_copies, 9, f"ag_fwd_start{l}")

    def forward_wait(fw, after, l):
        s_sems, r_sems, bufs, _ = fw
        return split_wait(s_sems, r_sems, bufs, _forward_copies, after, f"ag_fwd_wait{l}")

    wt, m_wt, v_wt = (jnp.swapaxes(a, 1, 2) for a in (w_in, m_w_in, v_w_in))
    xs, saved = x[0], []
    gathers = [split_start(stage_shards(wt[l], wsq32[l].reshape(4 * rs, D), conv_pack[l], me, f"stage{l}"),
                           _gather_copies, 12, f"ag_start{l}") for l in range(L)]
    started = sum(g[3][0, 0] for g in gathers)
    fw = forward_start(0, xs)
    wg, wsq, convw = forward_wait(fw, fw[3], 0)
    for l in range(L):
        wsq = wsq.reshape(NDEV, 4, rs, D)
        wfull = wg.reshape(P, D)
        g_pre = vec(norm_pre, l) + started if l == 0 else vec(norm_pre, l)
        proj, h = proj_fwd(xs, g_pre, wfull, f"proj{l}")
        if l + 1 < L:
            fw = forward_start(l + 1, h)
            convw = convw + fw[3][0, 0]
        yah = brancha_fwd(proj, convw, f"bra{l}")
        u1 = branchc1_fwd(proj, convw, vec(conv_c_b, l), f"brc1_{l}")
        qh, kh, vh = qkv_fwd(proj, vec(q_norm, l), vec(k_norm, l), cos, sin, f"qkv{l}")
        o, ybh = attn_fwd(qh, kh, vh, proj, f"attn{l}")
        ych = branchc2_fwd(u1, proj, vec(ln_c_g, l), vec(ln_c_b, l), f"brc2_{l}")
        ya, yb, yc, y16, z, xo = merge_fwd(xs, yah, ybh, ych, proj, wsq, vec(norm_post, l), f"merge{l}")
        saved.append(dict(x=xs, wfull=wfull, wsq=wsq, convw=convw, proj=proj, h=h, yah=yah, ybh=ybh, ych=ych, u1=u1,
                          qh=qh, kh=kh, vh=vh, o=o, ya=ya, yb=yb, yc=yc, y16=y16, z=z))
        xs = xo
        if l + 1 < L:
            wg, wsq, convw = forward_wait(fw, xs, l + 1)
    dx, loss_part = loss_fwd(xs, loss_target[0], "loss")
    loss = lax.psum(loss_part[0, 0], ("x", "y", "c"))

    acc = dict(win=None, sq=None, conv=None)
    small_parts = [None] * L
    msq32 = stack_sq(m_w_out_a, m_w_out_b, m_w_out_c, m_w_o)
    vsq32 = stack_sq(v_w_out_a, v_w_out_b, v_w_out_c, v_w_o)
    mconv, vconv = _pack_conv(m_conv_a_w, m_conv_c_w), _pack_conv(v_conv_a_w, v_conv_c_w)

    def scatter_start(parts, name):
        bufs = parts + [lax.empty(p.shape, p.dtype) for p in parts]
        return split_start(bufs, _scatter_copies(len(parts)), 7 * len(parts), name)

    def finish(l, started, after):
        (s1, r1, b1, _), (s2, r2, b2, _) = started
        gsq_own, rsq = split_wait(s1, r1, b1, _scatter_copies(1), after, f"rs_sq_wait{l}")
        gwin_own, gconv_own, rwin, rconv = split_wait(s2, r2, b2, _scatter_copies(2), after, f"rs_win_wait{l}")
        flat = lambda a: a.reshape(a.shape[0], 4 * rs, D)
        acc["win"] = adam_update(rwin, gwin_own, me, wt, m_wt, v_wt, l, acc["win"], f"adam_win{l}")
        acc["sq"] = adam_update(flat(rsq), flat(gsq_own), me, flat(wsq32), flat(msq32), flat(vsq32), l, acc["sq"], f"adam_wsq{l}")
        acc["conv"] = adam_update(rconv, gconv_own, me, conv_pack, mconv, vconv, l, acc["conv"], f"adam_conv{l}")

    pending = None
    for l in reversed(range(L)):
        sv = saved[l]
        proj = sv["proj"]
        (dyah, dybh, dych, dma, dmb, dmc, dzb, dyab, dybb, dycb, dgpost) = merge_bwd(
            dx, sv["z"], sv["ya"], sv["yb"], sv["yc"], proj, sv["wsq"], vec(norm_post, l), f"merge_bwd{l}")
        gsq = [tn_matmul(a, b, f"dwsq{t}_{l}") for t, (a, b) in enumerate(
            ((sv["yah"], dyab), (sv["ybh"], dybb), (sv["ych"], dycb), (sv["y16"], dzb)))]
        gsq_parts = jnp.stack([g.reshape(NDEV, rs, D) for g in gsq], axis=1)
        st1 = scatter_start([gsq_parts], f"rs_sq_start{l}")
        convw = sv["convw"] + st1[3][0, 0]
        dab, dac, dax, dag, gca = brancha_bwd(dyah, proj, convw, f"bra_bwd{l}")
        du1, dcg, dlg, dlb, dcb = branchc2_bwd(dych, sv["u1"], proj, vec(ln_c_g, l), vec(ln_c_b, l), f"brc2_bwd{l}")
        dcu, dcv, gcc = branchc1_bwd(du1, proj, convw, f"brc1_bwd{l}")
        dqh, dkh, dvh, dbg = attn_bwd(dybh, sv["o"], sv["qh"], sv["kh"], sv["vh"], proj, f"attn_bwd{l}")
        dq, dk, dv, dqn, dkn = qkv_bwd(dqh, dkh, dvh, proj, vec(q_norm, l), vec(k_norm, l), cos, sin, f"qkv_bwd{l}")
        dproj = jnp.concatenate([dab, dac, dax, dag, dq, dk, dv, dbg, dcu, dcv, dcg, dma, dmb, dmc], axis=1)
        gwin = dwin_parts(sv["h"], dproj, f"dwin{l}").reshape(NDEV, PSH, D)
        gconv = jnp.concatenate([gca, gcc], axis=1)
        st2 = scatter_start([gwin, gconv], f"rs_win_start{l}")
        dx, dgpre = dh_bwd(dproj, sv["wfull"], sv["x"], dx, vec(norm_pre, l) + st2[3][0, 0], f"dh{l}")
        wide = lambda a: jnp.pad(a, ((0, 0), (0, D - HD)))
        small_parts[l] = jnp.concatenate([dgpre, dgpost, dcb, dlg, dlb, wide(dqn), wide(dkn), jnp.zeros((1, D), f32)], axis=0)
        if pending is not None:
            finish(*pending, after=dx)
        pending = (l, (st1, st2))
    finish(*pending, after=dx)

    (small_all,) = all_gather([jnp.concatenate(small_parts, axis=0)], "ag_small")
    sm = adam_update(small_all, small_all, me,
                     _pack_small(norm_pre, norm_post, conv_c_b, ln_c_g, ln_c_b, q_norm, k_norm)[None],
                     _pack_small(m_norm_pre, m_norm_post, m_conv_c_b, m_ln_c_g, m_ln_c_b, m_q_norm, m_k_norm)[None],
                     _pack_small(v_norm_pre, v_norm_post, v_conv_c_b, v_ln_c_g, v_ln_c_b, v_q_norm, v_k_norm)[None],
                     0, None, "adam_small")
    sm = [a.reshape(L, 8, D) for a in sm]
    small_rows = dict(norm_pre=(0, D), norm_post=(1, D), conv_c_b=(2, D), ln_c_g=(3, D), ln_c_b=(4, D), q_norm=(5, HD), k_norm=(6, HD))
    sq_rows = dict(w_out_a=0, w_out_b=1, w_out_c=2, w_o=3)

    order = ["norm_pre", "norm_post", "w_in", "conv_a_w", "q_norm", "k_norm", "conv_c_w", "conv_c_b", "ln_c_g", "ln_c_b",
             "w_out_a", "w_out_b", "w_out_c", "w_o"]
    result = [loss, dx[None]]
    for kind in range(4):
        for nme in order:
            if nme in small_rows:
                rw, wd = small_rows[nme]
                result.append(sm[kind][:, rw, :wd])
            elif nme in sq_rows:
                result.append(acc["sq"][kind][:, sq_rows[nme] * rs:(sq_rows[nme] + 1) * rs])
            elif nme == "w_in":
                result.append(jnp.swapaxes(acc["win"][kind], 1, 2))
            elif nme == "conv_a_w":
                result.append(acc["conv"][kind][:, 0:CA_W])
            else:
                result.append(acc["conv"][kind][:, 8:8 + CC_W])
    return tuple(result)
```

```python
import math

import jax
import jax.numpy as jnp
from jax import lax
from jax.experimental import pallas as pl
from jax.experimental.pallas import tpu as pltpu

f32, bf16 = jnp.float32, jnp.bfloat16

D = 1024
S = 2048
L = 4
HD = 128
NQ = D // HD
NKV = NQ // 4
G = NQ // NKV
WKV = NKV * HD
GRID_W = 64
ROPE_THETA = 10000.0
RMS_EPS = 1e-6
LN_EPS = 1e-5
NDEV = 8
CA_W, CC_W = 3, 31
P = 12 * D + 2 * WKV
PSH = P // NDEV
CT = 128
ADAM_LR, ADAM_B1, ADAM_B2, ADAM_EPS, ADAM_WD, ADAM_STEP = 0.001, 0.9, 0.999, 1e-08, 0.01, 10
VMEM_LIMIT = 56 * 1024 * 1024
MESH = pl.DeviceIdType.MESH

_OFF = {}
_o = 0
for _n, _w in (("a_b", D), ("a_c", D), ("a_x", D), ("a_g", D), ("q", D), ("k", WKV), ("v", WKV), ("b_g", D),
               ("c_u", D), ("c_v", D), ("c_g", D), ("m_a", D), ("m_b", D), ("m_c", D)):
    _OFF[_n] = (_o, _w)
    _o += _w
PIECES = tuple(_OFF)


def _cp(sem=None, **kw):
    return pltpu.CompilerParams(dimension_semantics=sem, vmem_limit_bytes=VMEM_LIMIT, **kw)


def _sig(x):
    return 1.0 / (1.0 + jnp.exp(-x))


def _silu(x):
    return x * _sig(x)


def _dsilu(x):
    s = _sig(x)
    return s * (1.0 + x * (1.0 - s))


def _row_specs(name, tm):
    off, w = _OFF[name]
    bw = math.gcd(off, w) if off else w
    return [pl.BlockSpec((tm, bw), (lambda i, *_, b=off // bw + t: (i, b))) for t in range(w // bw)]


def _cat(refs):
    return refs[0][...] if len(refs) == 1 else jnp.concatenate([r[...] for r in refs], axis=1)


def _chan_spec(name):
    off, _ = _OFF[name]
    return pl.BlockSpec((S, CT), lambda j, b=off // CT: (0, b + j))


def _full(shape):
    return pl.BlockSpec(shape, lambda *_: (0,) * len(shape))


def _coords():
    return lax.axis_index("x"), lax.axis_index("y"), lax.axis_index("c")


def all_gather(shards, name):
    n = len(shards)

    def body(*refs):
        ins, outs = refs[:n], refs[n:2 * n]
        send_sems, recv_sems, local_sems = refs[2 * n:]
        x, y, c = _coords()
        me, sibling = (x, y, c), (x, y, 1 - c)
        chips = [(1 - x, y), (x, 1 - y), (1 - x, 1 - y)]

        def slot(a, p):
            return outs[a].at[4 * p[0] + 2 * p[1] + p[2]]

        def copy(a, k, block, to, src=None):
            return pltpu.make_async_remote_copy(
                src_ref=slot(a, block) if src is None else src, dst_ref=slot(a, block),
                send_sem=send_sems.at[7 * a + k], recv_sem=recv_sems.at[7 * a + k], device_id=to, device_id_type=MESH)

        mine = [pltpu.make_async_copy(ins[a], slot(a, me), local_sems.at[a]) for a in range(n)]
        for cp in mine:
            cp.start()
        first = []
        for a in range(n):
            first.append(copy(a, 0, me, sibling, src=ins[a]))
            first += [copy(a, 1 + j, me, (*chip, c), src=ins[a]) for j, chip in enumerate(chips)]
        for cp in first:
            cp.start()
        passed = []
        for j, chip in enumerate(chips):
            for a in range(n):
                copy(a, 1 + j, (*chip, c), me).wait_recv()
                fw = copy(a, 4 + j, (*chip, c), sibling)
                fw.start()
                passed.append(fw)
        for a in range(n):
            copy(a, 0, sibling, me).wait_recv()
            for j, chip in enumerate(chips):
                copy(a, 4 + j, (*chip, 1 - c), me).wait_recv()
        for cp in first + passed:
            cp.wait_send()
        for cp in mine:
            cp.wait()

    anyspec = pl.BlockSpec(memory_space=pl.ANY)
    return pl.pallas_call(
        body, name=name,
        out_shape=[jax.ShapeDtypeStruct((NDEV,) + s.shape, s.dtype) for s in shards],
        in_specs=[anyspec] * n, out_specs=[anyspec] * n,
        scratch_shapes=[pltpu.SemaphoreType.DMA((7 * n,)), pltpu.SemaphoreType.DMA((7 * n,)), pltpu.SemaphoreType.DMA((n,))],
    )(*shards)


_HBM = pl.BlockSpec(memory_space=pltpu.HBM)
_SEM = pl.BlockSpec(memory_space=pltpu.SEMAPHORE)
_EFFECT = pltpu.SideEffectType.DATAFLOW_SIDE_EFFECTING


def split_start(bufs, make_copies, nsem, name, after=None):
    n = len(bufs)
    extra = [] if after is None else [after]

    def body(*refs):
        send_sems, recv_sems = refs[n + len(extra):n + len(extra) + 2]
        for cp in make_copies(refs[:n], send_sems, recv_sems):
            cp.start()
        refs[-1][...] = jnp.zeros((8, 128), f32)

    res = pl.pallas_call(
        body, name=name,
        out_shape=(pltpu.SemaphoreType.DMA((nsem,)), pltpu.SemaphoreType.DMA((nsem,)),
                   *[pltpu.HBM(b.shape, b.dtype) for b in bufs], jax.ShapeDtypeStruct((8, 128), f32)),
        in_specs=[_HBM] * n + [pl.BlockSpec(memory_space=pl.ANY)] * len(extra),
        out_specs=(_SEM, _SEM, *([_HBM] * n), pl.BlockSpec(memory_space=pltpu.VMEM)),
        input_output_aliases={i: 2 + i for i in range(n)},
        compiler_params=pltpu.CompilerParams(has_side_effects=_EFFECT),
    )(*[pltpu.with_memory_space_constraint(b, pltpu.HBM) for b in bufs], *extra)
    return res[0], res[1], list(res[2:2 + n]), res[-1]


def split_wait(send_sems, recv_sems, bufs, make_copies, after, name):
    n = len(bufs)

    def body(*refs):
        for cp in make_copies(refs[:n], refs[n], refs[n + 1]):
            cp.wait_send()
            cp.wait_recv()

    res = pl.pallas_call(
        body, name=name,
        out_shape=tuple(pltpu.HBM(b.shape, b.dtype) for b in bufs),
        in_specs=[_HBM] * n + [_SEM, _SEM, pl.BlockSpec(memory_space=pl.ANY)],
        out_specs=[_HBM] * n,
        input_output_aliases={i: i for i in range(n)},
        compiler_params=pltpu.CompilerParams(has_side_effects=_EFFECT),
    )(*bufs, send_sems, recv_sems, after)
    return list(res)


def _scatter_copies(n):
    def make(refs, send_sems, recv_sems):
        x, y, c = _coords()
        me = 4 * x + 2 * y + c
        copies = []
        for a in range(n):
            for k in range(1, NDEV):
                px = 1 - x if (k >> 2) & 1 else x
                py = 1 - y if (k >> 1) & 1 else y
                pc = 1 - c if k & 1 else c
                copies.append(pltpu.make_async_remote_copy(
                    src_ref=refs[a].at[4 * px + 2 * py + pc], dst_ref=refs[n + a].at[me],
                    send_sem=send_sems.at[7 * a + k - 1], recv_sem=recv_sems.at[7 * a + k - 1],
                    device_id=(px, py, pc), device_id_type=MESH))
        return copies
    return make


def _gather_copies(refs, send_sems, recv_sems):
    x, y, c = _coords()
    me = 4 * x + 2 * y + c
    targets = [(x, y, 1 - c), (1 - x, y, c), (x, 1 - y, c), (1 - x, 1 - y, c)]
    return [pltpu.make_async_remote_copy(
        src_ref=r.at[me], dst_ref=r.at[me], send_sem=send_sems.at[4 * a + k], recv_sem=recv_sems.at[4 * a + k],
        device_id=to, device_id_type=MESH) for a, r in enumerate(refs) for k, to in enumerate(targets)]


def _forward_copies(refs, send_sems, recv_sems):
    x, y, c = _coords()
    chips = [(1 - x, y), (x, 1 - y), (1 - x, 1 - y)]
    return [pltpu.make_async_remote_copy(
        src_ref=r.at[4 * px + 2 * py + c], dst_ref=r.at[4 * px + 2 * py + c], send_sem=send_sems.at[3 * a + j],
        recv_sem=recv_sems.at[3 * a + j], device_id=(x, y, 1 - c), device_id_type=MESH)
        for a, r in enumerate(refs) for j, (px, py) in enumerate(chips)]


def _row_tile(r):
    return r if r <= 256 else max(t for t in (256, 160, 128) if r % t == 0)


def stage_shards(wt_l, wsq_l, conv_l, me, name):
    outs = []
    for a, dt in ((wt_l, bf16), (wsq_l, bf16), (conv_l, f32)):
        r, c = a.shape
        tr = _row_tile(r)

        def body(me_ref, a_ref, o_ref):
            o_ref[0] = a_ref[...].astype(o_ref.dtype)

        outs.append(pl.pallas_call(
            body, name=f"{name}_{len(outs)}", out_shape=jax.ShapeDtypeStruct((NDEV, r, c), dt),
            grid_spec=pltpu.PrefetchScalarGridSpec(
                num_scalar_prefetch=1, grid=(r // tr,),
                in_specs=[pl.BlockSpec((tr, c), lambda i, m: (i, 0))],
                out_specs=pl.BlockSpec((1, tr, c), lambda i, m: (m[0], i, 0))),
            compiler_params=_cp(("arbitrary",)))(me, a))
    return outs


def proj_fwd(xin, g_pre, wt, name):
    tm, tn = min(S, 1024), 1280

    def body(x_ref, g_ref, w_ref, proj_ref, h_ref, hs):
        @pl.when(pl.program_id(1) == 0)
        def _():
            x = x_ref[...]
            r = lax.rsqrt(jnp.mean(x * x, axis=-1, keepdims=True) + RMS_EPS)
            h = (x * r * g_ref[...]).astype(bf16)
            hs[...] = h
            h_ref[...] = h
        proj_ref[...] = lax.dot_general(hs[...], w_ref[...], (((1,), (1,)), ((), ())), preferred_element_type=f32)

    return pl.pallas_call(
        body, name=name, grid=(S // tm, P // tn),
        out_shape=[jax.ShapeDtypeStruct((S, P), f32), jax.ShapeDtypeStruct((S, D), bf16)],
        in_specs=[pl.BlockSpec((tm, D), lambda i, j: (i, 0)), _full((1, D)), pl.BlockSpec((tn, D), lambda i, j: (j, 0))],
        out_specs=[pl.BlockSpec((tm, tn), lambda i, j: (i, j)), pl.BlockSpec((tm, D), lambda i, j: (i, 0))],
        scratch_shapes=[pltpu.VMEM((tm, D), bf16)],
        compiler_params=_cp(("parallel", "arbitrary")))(xin, g_pre, wt)


RC = 128


def _fill_pad(pad, halo, val_fn):
    pad[0:halo, :] = jnp.zeros((halo, CT), f32)
    pad[S + halo:S + 2 * halo, :] = jnp.zeros((halo, CT), f32)

    def step(i, carry):
        rows = pl.ds(pl.multiple_of(i * RC, RC), RC)
        pad[pl.ds(pl.multiple_of(i * RC, RC) + halo, RC), :] = val_fn(rows)
        return carry
    lax.fori_loop(0, S // RC, step, 0)


def brancha_fwd(proj, convw, name):
    def body(ab, ac, ax, ag, w_ref, o_ref, pad):
        _fill_pad(pad, 8, lambda rows: ac[rows, :] * ax[rows, :])
        w = [w_ref[0, k:k + 1, :] for k in range(CA_W)]

        def step(i, carry):
            base = pl.multiple_of(i * RC, RC)
            rows = pl.ds(base, RC)
            t = sum(w[k] * pad[pl.ds(base + 7 + k, RC), :] for k in range(CA_W))
            o_ref[rows, :] = (ab[rows, :] * t * _silu(ag[rows, :])).astype(bf16)
            return carry
        lax.fori_loop(0, S // RC, step, 0)

    return pl.pallas_call(
        body, name=name, grid=(D // CT,), out_shape=jax.ShapeDtypeStruct((S, D), bf16),
        in_specs=[_chan_spec("a_b"), _chan_spec("a_c"), _chan_spec("a_x"), _chan_spec("a_g"),
                  pl.BlockSpec((1, 40, CT), lambda j: (j, 0, 0))],
        out_specs=pl.BlockSpec((S, CT), lambda j: (0, j)),
        scratch_shapes=[pltpu.VMEM((S + 16, CT), f32)],
        compiler_params=_cp(("parallel",)))(proj, proj, proj, proj, convw)


def branchc1_fwd(proj, convw, cbias, name):
    def body(cu, cv, w_ref, b_ref, o_ref, pad):
        _fill_pad(pad, 16, lambda rows: cu[rows, :] * _sig(cv[rows, :]))

        def step(i, carry):
            base = pl.multiple_of(i * RC, RC)
            acc = jnp.zeros((RC, CT), f32) + b_ref[...]
            for k in range(CC_W):
                acc = acc + w_ref[0, 8 + k:9 + k, :] * pad[pl.ds(base + k + 1, RC), :]
            o_ref[pl.ds(base, RC), :] = acc
            return carry
        lax.fori_loop(0, S // RC, step, 0)

    return pl.pallas_call(
        body, name=name, grid=(D // CT,), out_shape=jax.ShapeDtypeStruct((S, D), f32),
        in_specs=[_chan_spec("c_u"), _chan_spec("c_v"), pl.BlockSpec((1, 40, CT), lambda j: (j, 0, 0)),
                  pl.BlockSpec((1, CT), lambda j: (0, j))],
        out_specs=pl.BlockSpec((S, CT), lambda j: (0, j)),
        scratch_shapes=[pltpu.VMEM((S + 32, CT), f32)],
        compiler_params=_cp(("parallel",)))(proj, proj, convw, cbias)


def _swap32(x):
    lane = lax.broadcasted_iota(jnp.int32, x.shape, 1)
    return jnp.where((lane // 32) % 2 == 1, pltpu.roll(x, 32, 1), pltpu.roll(x, HD - 32, 1))


def _rope(y, cos, sin):
    return y * cos + _swap32(y) * sin


def qkv_fwd(proj, qn, kn, cos, sin, name):
    tm = min(S, 256)
    nq, nk, nv = len(_row_specs("q", tm)), len(_row_specs("k", tm)), len(_row_specs("v", tm))

    def body(*refs):
        q = _cat(refs[:nq])
        k = _cat(refs[nq:nq + nk])
        v = _cat(refs[nq + nk:nq + nk + nv])
        qn_ref, kn_ref, cos_ref, sin_ref, qh_ref, kh_ref, vh_ref = refs[nq + nk + nv:]
        cos, sin = cos_ref[...], sin_ref[...]

        def heads(xx, gn, out_ref, n):
            for h in range(n):
                xh = xx[:, h * HD:(h + 1) * HD]
                r = lax.rsqrt(jnp.mean(xh * xh, axis=-1, keepdims=True) + RMS_EPS)
                out_ref[:, h * HD:(h + 1) * HD] = _rope(xh * r * gn, cos, sin).astype(bf16)
        heads(q, qn_ref[...], qh_ref, NQ)
        heads(k, kn_ref[...], kh_ref, NKV)
        vh_ref[...] = v.astype(bf16)

    row = lambda w: pl.BlockSpec((tm, w), lambda i: (i, 0))
    return pl.pallas_call(
        body, name=name, grid=(S // tm,),
        out_shape=[jax.ShapeDtypeStruct((S, D), bf16), jax.ShapeDtypeStruct((S, WKV), bf16), jax.ShapeDtypeStruct((S, WKV), bf16)],
        in_specs=_row_specs("q", tm) + _row_specs("k", tm) + _row_specs("v", tm) + [_full((1, HD)), _full((1, HD)), row(HD), row(HD)],
        out_specs=[row(D), row(WKV), row(WKV)],
        compiler_params=_cp(("parallel",)))(*([proj] * (nq + nk + nv)), qn, kn, cos, sin)


def _softmax_rows(q, k):
    s = lax.dot_general(q, k, (((1,), (1,)), ((), ())), preferred_element_type=f32)
    p = jnp.exp((s - jnp.max(s, axis=-1, keepdims=True)) * (HD ** -0.5))
    return p, 1.0 / jnp.sum(p, axis=-1, keepdims=True)


GW = G * HD


def attn_fwd(qh, kh, vh, proj, name):
    tq = min(S, 256)
    bg_blk = _OFF["b_g"][0] // GW

    def body(q_ref, k_ref, v_ref, bg_ref, o_ref, y_ref):
        k, v = k_ref[...], v_ref[...]
        for g in range(G):
            cols = slice(g * HD, (g + 1) * HD)
            p, rl = _softmax_rows(q_ref[:, cols], k)
            o = jnp.dot(p.astype(bf16), v, preferred_element_type=f32) * rl
            o_ref[:, cols] = o
            y_ref[:, cols] = (o * _silu(bg_ref[:, cols])).astype(bf16)

    grp = pl.BlockSpec((tq, GW), lambda kv, i: (i, kv))
    kvs = pl.BlockSpec((S, HD), lambda kv, i: (0, kv))
    return pl.pallas_call(
        body, name=name, grid=(NKV, S // tq),
        out_shape=[jax.ShapeDtypeStruct((S, D), f32), jax.ShapeDtypeStruct((S, D), bf16)],
        in_specs=[grp, kvs, kvs, pl.BlockSpec((tq, GW), lambda kv, i: (i, bg_blk + kv))],
        out_specs=[grp, grp],
        compiler_params=_cp(("parallel", "parallel")))(qh, kh, vh, proj)


def _ln_parts(u1):
    mu = jnp.mean(u1, axis=-1, keepdims=True)
    xc = u1 - mu
    rstd = lax.rsqrt(jnp.mean(xc * xc, axis=-1, keepdims=True) + LN_EPS)
    return xc * rstd, rstd


def branchc2_fwd(u1, proj, lng, lnb, name):
    tm = min(S, 256)
    ncg = len(_row_specs("c_g", tm))

    def body(*refs):
        u_ref = refs[0]
        cg = _cat(refs[1:1 + ncg])
        g_ref, b_ref, o_ref = refs[1 + ncg:]
        xh, _ = _ln_parts(u_ref[...])
        o_ref[...] = (_silu(xh * g_ref[...] + b_ref[...]) * _silu(cg)).astype(bf16)

    row = pl.BlockSpec((tm, D), lambda i: (i, 0))
    return pl.pallas_call(
        body, name=name, grid=(S // tm,), out_shape=jax.ShapeDtypeStruct((S, D), bf16),
        in_specs=[row] + _row_specs("c_g", tm) + [_full((1, D)), _full((1, D))], out_specs=row,
        compiler_params=_cp(("parallel",)))(u1, *([proj] * ncg), lng, lnb)


def _wmat(w_ref, kind):
    return w_ref[:, kind].reshape(D, D)


def merge_fwd(xin, yah, ybh, ych, proj, wsq, g_post, name):
    tm = min(S, 256)
    nm = len(_row_specs("m_a", tm))

    def body(*refs):
        x_ref, a_ref, b_ref, c_ref = refs[:4]
        ms = [_cat(refs[4 + t * nm:4 + (t + 1) * nm]) for t in range(3)]
        w_ref, g_ref, ya_ref, yb_ref, yc_ref, y_ref, z_ref, o_ref = refs[4 + 3 * nm:]
        y = jnp.zeros((tm, D), f32)
        for t, (h_ref, out_ref) in enumerate(((a_ref, ya_ref), (b_ref, yb_ref), (c_ref, yc_ref))):
            yt = jnp.dot(h_ref[...], _wmat(w_ref, t), preferred_element_type=f32)
            out_ref[...] = yt
            y = y + _sig(ms[t]) * yt
        yb16 = y.astype(bf16)
        y_ref[...] = yb16
        z = jnp.dot(yb16, _wmat(w_ref, 3), preferred_element_type=f32)
        z_ref[...] = z
        r = lax.rsqrt(jnp.mean(z * z, axis=-1, keepdims=True) + RMS_EPS)
        o_ref[...] = x_ref[...] + z * r * g_ref[...]

    row = pl.BlockSpec((tm, D), lambda i: (i, 0))
    sd = lambda dt: jax.ShapeDtypeStruct((S, D), dt)
    return pl.pallas_call(
        body, name=name, grid=(S // tm,),
        out_shape=[sd(f32), sd(f32), sd(f32), sd(bf16), sd(f32), sd(f32)],
        in_specs=[row] * 4 + _row_specs("m_a", tm) + _row_specs("m_b", tm) + _row_specs("m_c", tm)
        + [_full((NDEV, 4, D // NDEV, D)), _full((1, D))],
        out_specs=[row] * 6,
        compiler_params=_cp(("parallel",)))(xin, yah, ybh, ych, *([proj] * (3 * nm)), wsq, g_post)


def loss_fwd(y, target, name):
    tm = min(S, 256)

    def body(y_ref, t_ref, dy_ref, l_ref):
        e = y_ref[...] - t_ref[...]
        dy_ref[...] = e / D

        @pl.when(pl.program_id(0) == 0)
        def _():
            l_ref[...] = jnp.zeros((1, 128), f32)
        l_ref[...] += (0.5 / D) * jnp.sum(e * e)

    row = pl.BlockSpec((tm, D), lambda i: (i, 0))
    return pl.pallas_call(
        body, name=name, grid=(S // tm,),
        out_shape=[jax.ShapeDtypeStruct((S, D), f32), jax.ShapeDtypeStruct((1, 128), f32)],
        in_specs=[row, row], out_specs=[row, _full((1, 128))],
        compiler_params=_cp(("arbitrary",)))(y, target)


def _acc(ref, val):
    @pl.when(pl.program_id(0) == 0)
    def _():
        ref[...] = jnp.zeros(ref.shape, f32)
    ref[...] += val


def merge_bwd(dout, z, ya, yb, yc, proj, wsq, g_post, name):
    tm = min(S, 256)
    nm = len(_row_specs("m_a", tm))

    def body(*refs):
        do_ref, z_ref, ya_ref, yb_ref, yc_ref = refs[:5]
        ms = [_cat(refs[5 + t * nm:5 + (t + 1) * nm]) for t in range(3)]
        w_ref, g_ref = refs[5 + 3 * nm:7 + 3 * nm]
        dh_refs = refs[7 + 3 * nm:10 + 3 * nm]
        dm_refs = refs[10 + 3 * nm:13 + 3 * nm]
        dzb_ref = refs[13 + 3 * nm]
        dyb_refs = refs[14 + 3 * nm:17 + 3 * nm]
        dg_ref = refs[17 + 3 * nm]
        nt = (((1,), (1,)), ((), ()))
        z, dout = z_ref[...], do_ref[...]
        r = lax.rsqrt(jnp.mean(z * z, axis=-1, keepdims=True) + RMS_EPS)
        zh = z * r
        _acc(dg_ref, jnp.sum(dout * zh, axis=0, keepdims=True))
        dzh = dout * g_ref[...]
        dz = (r * (dzh - zh * jnp.mean(dzh * zh, axis=-1, keepdims=True))).astype(bf16)
        dzb_ref[...] = dz
        dy = lax.dot_general(dz, _wmat(w_ref, 3), nt, preferred_element_type=f32)
        for t, yt_ref in enumerate((ya_ref, yb_ref, yc_ref)):
            sg = _sig(ms[t])
            dyt = (dy * sg).astype(bf16)
            dyb_refs[t][...] = dyt
            dm_refs[t][...] = (dy * yt_ref[...] * sg * (1.0 - sg)).astype(bf16)
            dh_refs[t][...] = lax.dot_general(dyt, _wmat(w_ref, t), nt, preferred_element_type=f32)

    row = pl.BlockSpec((tm, D), lambda i: (i, 0))
    sd = lambda dt: jax.ShapeDtypeStruct((S, D), dt)
    return pl.pallas_call(
        body, name=name, grid=(S // tm,),
        out_shape=[sd(f32)] * 3 + [sd(bf16)] * 7 + [jax.ShapeDtypeStruct((1, D), f32)],
        in_specs=[row] * 5 + _row_specs("m_a", tm) + _row_specs("m_b", tm) + _row_specs("m_c", tm)
        + [_full((NDEV, 4, D // NDEV, D)), _full((1, D))],
        out_specs=[row] * 10 + [_full((1, D))],
        compiler_params=_cp(("arbitrary",)))(dout, z, ya, yb, yc, *([proj] * (3 * nm)), wsq, g_post)


def tn_matmul(a, b, name):
    m, n = a.shape[1], b.shape[1]
    tmm = min(m, 512)

    def body(a_ref, b_ref, o_ref):
        o_ref[...] = lax.dot_general(a_ref[...], b_ref[...], (((0,), (0,)), ((), ())), preferred_element_type=f32).astype(bf16)

    return pl.pallas_call(
        body, name=name, grid=(m // tmm,), out_shape=jax.ShapeDtypeStruct((m, n), bf16),
        in_specs=[pl.BlockSpec((S, tmm), lambda i: (0, i)), _full((S, n))],
        out_specs=pl.BlockSpec((tmm, n), lambda i: (i, 0)),
        compiler_params=_cp(("parallel",)))(a, b)


def dwin_parts(h, dproj, name):
    tn = 640

    def body(d_ref, h_ref, o_ref):
        o_ref[...] = lax.dot_general(d_ref[...], h_ref[...], (((0,), (0,)), ((), ())), preferred_element_type=f32).astype(bf16)

    return pl.pallas_call(
        body, name=name, grid=(P // tn,), out_shape=jax.ShapeDtypeStruct((P, D), bf16),
        in_specs=[pl.BlockSpec((S, tn), lambda j: (0, j)), _full((S, D))],
        out_specs=pl.BlockSpec((tn, D), lambda j: (j, 0)),
        compiler_params=_cp(("parallel",)))(dproj, h)


def brancha_bwd(dyah, proj, convw, name):
    def body(d_ref, ab, ac, ax, ag, w_ref, dab, dac, dax, dag, dw_ref, padp, padt, accw):
        _fill_pad(padp, 8, lambda rows: ac[rows, :] * ax[rows, :])
        _fill_pad(padt, 8, lambda rows: d_ref[rows, :] * ab[rows, :] * _silu(ag[rows, :]))
        accw[...] = jnp.zeros(accw.shape, f32)
        w = [w_ref[0, k:k + 1, :] for k in range(CA_W)]

        def step(i, carry):
            base = pl.multiple_of(i * RC, RC)
            rows = pl.ds(base, RC)
            ps = [padp[pl.ds(base + 7 + k, RC), :] for k in range(CA_W)]
            t = sum(w[k] * ps[k] for k in range(CA_W))
            dp = sum(w[k] * padt[pl.ds(base + 9 - k, RC), :] for k in range(CA_W))
            d, a_b, a_g = d_ref[rows, :], ab[rows, :], ag[rows, :]
            dab[rows, :] = (d * t * _silu(a_g)).astype(bf16)
            dag[rows, :] = (d * a_b * t * _dsilu(a_g)).astype(bf16)
            dac[rows, :] = (dp * ax[rows, :]).astype(bf16)
            dax[rows, :] = (dp * ac[rows, :]).astype(bf16)
            dt = padt[pl.ds(base + 8, RC), :]
            for k in range(CA_W):
                accw[8 * k:8 * k + 8, :] += jnp.sum((dt * ps[k]).reshape(RC // 8, 8, CT), axis=0)
            return carry
        lax.fori_loop(0, S // RC, step, 0)
        dw_ref[0] = jnp.zeros((8, CT), f32)
        for k in range(CA_W):
            dw_ref[0, k:k + 1, :] = jnp.sum(accw[8 * k:8 * k + 8, :], axis=0, keepdims=True)

    tile = pl.BlockSpec((S, CT), lambda j: (0, j))
    sd = jax.ShapeDtypeStruct((S, D), bf16)
    return pl.pallas_call(
        body, name=name, grid=(D // CT,),
        out_shape=[sd, sd, sd, sd, jax.ShapeDtypeStruct((NDEV, 8, CT), f32)],
        in_specs=[tile, _chan_spec("a_b"), _chan_spec("a_c"), _chan_spec("a_x"), _chan_spec("a_g"),
                  pl.BlockSpec((1, 40, CT), lambda j: (j, 0, 0))],
        out_specs=[tile] * 4 + [pl.BlockSpec((1, 8, CT), lambda j: (j, 0, 0))],
        scratch_shapes=[pltpu.VMEM((S + 16, CT), f32), pltpu.VMEM((S + 16, CT), f32), pltpu.VMEM((8 * CA_W, CT), f32)],
        compiler_params=_cp(("parallel",)))(dyah, proj, proj, proj, proj, convw)


def branchc2_bwd(dych, u1, proj, lng, lnb, name):
    tm = min(S, 256)
    ncg = len(_row_specs("c_g", tm))

    def body(*refs):
        d_ref, u_ref = refs[:2]
        cg = _cat(refs[2:2 + ncg])
        g_ref, b_ref, du_ref, dcg_ref, dlg_ref, dlb_ref, dcb_ref = refs[2 + ncg:]
        d = d_ref[...]
        xh, rstd = _ln_parts(u_ref[...])
        ln = xh * g_ref[...] + b_ref[...]
        dcg_ref[...] = (d * _silu(ln) * _dsilu(cg)).astype(bf16)
        dln = d * _silu(cg) * _dsilu(ln)
        _acc(dlg_ref, jnp.sum(dln * xh, axis=0, keepdims=True))
        _acc(dlb_ref, jnp.sum(dln, axis=0, keepdims=True))
        dxh = dln * g_ref[...]
        du = rstd * (dxh - jnp.mean(dxh, axis=-1, keepdims=True) - xh * jnp.mean(dxh * xh, axis=-1, keepdims=True))
        du_ref[...] = du
        _acc(dcb_ref, jnp.sum(du, axis=0, keepdims=True))

    row = pl.BlockSpec((tm, D), lambda i: (i, 0))
    vec = jax.ShapeDtypeStruct((1, D), f32)
    return pl.pallas_call(
        body, name=name, grid=(S // tm,),
        out_shape=[jax.ShapeDtypeStruct((S, D), f32), jax.ShapeDtypeStruct((S, D), bf16), vec, vec, vec],
        in_specs=[row, row] + _row_specs("c_g", tm) + [_full((1, D)), _full((1, D))],
        out_specs=[row, row, _full((1, D)), _full((1, D)), _full((1, D))],
        compiler_params=_cp(("arbitrary",)))(dych, u1, *([proj] * ncg), lng, lnb)


def branchc1_bwd(du1, proj, convw, name):
    def body(d_ref, cu, cv, w_ref, dcu, dcv, dw_ref, padu, padd, accw):
        _fill_pad(padu, 16, lambda rows: cu[rows, :] * _sig(cv[rows, :]))
        _fill_pad(padd, 16, lambda rows: d_ref[rows, :])
        accw[...] = jnp.zeros(accw.shape, f32)

        def step(i, carry):
            base = pl.multiple_of(i * RC, RC)
            rows = pl.ds(base, RC)
            d = d_ref[rows, :]
            du0 = jnp.zeros((RC, CT), f32)
            for k in range(CC_W):
                du0 = du0 + w_ref[0, 8 + k:9 + k, :] * padd[pl.ds(base + 31 - k, RC), :]
                accw[8 * k:8 * k + 8, :] += jnp.sum((d * padu[pl.ds(base + k + 1, RC), :]).reshape(RC // 8, 8, CT), axis=0)
            sg = _sig(cv[rows, :])
            dcu[rows, :] = (du0 * sg).astype(bf16)
            dcv[rows, :] = (du0 * cu[rows, :] * sg * (1.0 - sg)).astype(bf16)
            return carry
        lax.fori_loop(0, S // RC, step, 0)
        dw_ref[0] = jnp.zeros((32, CT), f32)
        for k in range(CC_W):
            dw_ref[0, k:k + 1, :] = jnp.sum(accw[8 * k:8 * k + 8, :], axis=0, keepdims=True)

    tile = pl.BlockSpec((S, CT), lambda j: (0, j))
    sd = jax.ShapeDtypeStruct((S, D), bf16)
    return pl.pallas_call(
        body, name=name, grid=(D // CT,),
        out_shape=[sd, sd, jax.ShapeDtypeStruct((NDEV, 32, CT), f32)],
        in_specs=[tile, _chan_spec("c_u"), _chan_spec("c_v"), pl.BlockSpec((1, 40, CT), lambda j: (j, 0, 0))],
        out_specs=[tile, tile, pl.BlockSpec((1, 32, CT), lambda j: (j, 0, 0))],
        scratch_shapes=[pltpu.VMEM((S + 32, CT), f32), pltpu.VMEM((S + 32, CT), f32), pltpu.VMEM((8 * 32, CT), f32)],
        compiler_params=_cp(("parallel",)))(du1, proj, proj, convw)


def attn_bwd(dybh, o, qh, kh, vh, proj, name):
    tq = min(S, 256)
    bg_blk = _OFF["b_g"][0] // GW

    def body(d_ref, o_ref, q_ref, k_ref, v_ref, bg_ref, dq_ref, dk_ref, dv_ref, dbg_ref):
        @pl.when(pl.program_id(1) == 0)
        def _():
            dk_ref[...] = jnp.zeros(dk_ref.shape, f32)
            dv_ref[...] = jnp.zeros(dv_ref.shape, f32)
        k, v = k_ref[...], v_ref[...]
        tn = (((0,), (0,)), ((), ()))
        dk_acc = jnp.zeros((S, HD), f32)
        dv_acc = jnp.zeros((S, HD), f32)
        for g in range(G):
            cols = slice(g * HD, (g + 1) * HD)
            d, bg, q = d_ref[:, cols], bg_ref[:, cols], q_ref[:, cols]
            dbg_ref[:, cols] = (d * o_ref[:, cols] * _dsilu(bg)).astype(bf16)
            do = d * _silu(bg)
            p, rl = _softmax_rows(q, k)
            dv_acc = dv_acc + lax.dot_general(p.astype(bf16), (do * rl).astype(bf16), tn, preferred_element_type=f32)
            dp = lax.dot_general(do.astype(bf16), v, (((1,), (1,)), ((), ())), preferred_element_type=f32)
            delta = jnp.sum(p * dp, axis=-1, keepdims=True) * rl
            ds = (p * (dp - delta)).astype(bf16)
            rs_ = rl * (HD ** -0.5)
            dq_ref[:, cols] = jnp.dot(ds, k, preferred_element_type=f32) * rs_
            dk_acc = dk_acc + lax.dot_general(ds, (q.astype(f32) * rs_).astype(bf16), tn, preferred_element_type=f32)
        dk_ref[...] += dk_acc
        dv_ref[...] += dv_acc

    grp = pl.BlockSpec((tq, GW), lambda kv, i: (i, kv))
    kvs = pl.BlockSpec((S, HD), lambda kv, i: (0, kv))
    return pl.pallas_call(
        body, name=name, grid=(NKV, S // tq),
        out_shape=[jax.ShapeDtypeStruct((S, D), f32), jax.ShapeDtypeStruct((S, WKV), f32),
                   jax.ShapeDtypeStruct((S, WKV), f32), jax.ShapeDtypeStruct((S, D), bf16)],
        in_specs=[grp, grp, grp, kvs, kvs, pl.BlockSpec((tq, GW), lambda kv, i: (i, bg_blk + kv))],
        out_specs=[grp, kvs, kvs, grp],
        compiler_params=_cp(("parallel", "arbitrary")))(dybh, o, qh, kh, vh, proj)


def qkv_bwd(dqh, dkh, dvh, proj, qn, kn, cos, sin, name):
    tm = min(S, 256)
    nq, nk = len(_row_specs("q", tm)), len(_row_specs("k", tm))

    def body(*refs):
        dqh_ref, dkh_ref, dvh_ref = refs[:3]
        q = _cat(refs[3:3 + nq])
        k = _cat(refs[3 + nq:3 + nq + nk])
        qn_ref, kn_ref, cos_ref, sin_ref, dq_ref, dk_ref, dv_ref, dqn_ref, dkn_ref = refs[3 + nq + nk:]
        cos, sin = cos_ref[...], sin_ref[...]

        def heads(xx, dd, gn, out_ref, dgn_ref, n):
            dg = jnp.zeros((1, HD), f32)
            for h in range(n):
                xh = xx[:, h * HD:(h + 1) * HD]
                dh = dd[:, h * HD:(h + 1) * HD]
                r = lax.rsqrt(jnp.mean(xh * xh, axis=-1, keepdims=True) + RMS_EPS)
                xn = xh * r
                dy = dh * cos + _swap32(dh * sin)
                dg = dg + jnp.sum(dy * xn, axis=0, keepdims=True)
                dxn = dy * gn
                out_ref[:, h * HD:(h + 1) * HD] = (r * (dxn - xn * jnp.mean(dxn * xn, axis=-1, keepdims=True))).astype(bf16)
            _acc(dgn_ref, dg)
        heads(q, dqh_ref[...], qn_ref[...], dq_ref, dqn_ref, NQ)
        heads(k, dkh_ref[...], kn_ref[...], dk_ref, dkn_ref, NKV)
        dv_ref[...] = dvh_ref[...].astype(bf16)

    row = lambda w: pl.BlockSpec((tm, w), lambda i: (i, 0))
    vec = jax.ShapeDtypeStruct((1, HD), f32)
    return pl.pallas_call(
        body, name=name, grid=(S // tm,),
        out_shape=[jax.ShapeDtypeStruct((S, D), bf16), jax.ShapeDtypeStruct((S, WKV), bf16),
                   jax.ShapeDtypeStruct((S, WKV), bf16), vec, vec],
        in_specs=[row(D), row(WKV), row(WKV)] + _row_specs("q", tm) + _row_specs("k", tm)
        + [_full((1, HD)), _full((1, HD)), row(HD), row(HD)],
        out_specs=[row(D), row(WKV), row(WKV), _full((1, HD)), _full((1, HD))],
        compiler_params=_cp(("arbitrary",)))(dqh, dkh, dvh, *([proj] * (nq + nk)), qn, kn, cos, sin)


def dh_bwd(dproj, wfull, xin, dout, g_pre, name):
    tm, tk = min(S, 1024), 1280
    nk = P // tk

    def body(d_ref, w_ref, x_ref, do_ref, g_ref, dx_ref, dg_ref, acc):
        kk = pl.program_id(1)

        @pl.when(kk == 0)
        def _():
            acc[...] = jnp.zeros(acc.shape, f32)
        acc[...] += jnp.dot(d_ref[...], w_ref[...], preferred_element_type=f32)

        @pl.when((kk == 0) & (pl.program_id(0) == 0))
        def _():
            dg_ref[...] = jnp.zeros(dg_ref.shape, f32)

        @pl.when(kk == nk - 1)
        def _():
            x, dh = x_ref[...], acc[...]
            r = lax.rsqrt(jnp.mean(x * x, axis=-1, keepdims=True) + RMS_EPS)
            xn = x * r
            dg_ref[...] += jnp.sum(dh * xn, axis=0, keepdims=True)
            dxn = dh * g_ref[...]
            dx_ref[...] = do_ref[...] + r * (dxn - xn * jnp.mean(dxn * xn, axis=-1, keepdims=True))

    row = pl.BlockSpec((tm, D), lambda i, k: (i, 0))
    return pl.pallas_call(
        body, name=name, grid=(S // tm, nk),
        out_shape=[jax.ShapeDtypeStruct((S, D), f32), jax.ShapeDtypeStruct((1, D), f32)],
        in_specs=[pl.BlockSpec((tm, tk), lambda i, k: (i, k)), pl.BlockSpec((tk, D), lambda i, k: (k, 0)), row, row, _full((1, D))],
        out_specs=[row, _full((1, D))],
        scratch_shapes=[pltpu.VMEM((tm, D), f32)],
        compiler_params=_cp(("arbitrary", "arbitrary")))(dproj, wfull, xin, dout, g_pre)


def adam_update(parts, own, me, w, m, v, l, acc, name):
    lw, r, c = w.shape
    tr = _row_tile(r)

    def body(me_ref, p_ref, own_ref, w_ref, m_ref, v_ref, *rest):
        g_ref, d_ref, nm_ref, nv_ref = rest[-4:]
        g = None
        for s in range(NDEV):
            part = jnp.where(me_ref[0] == s, own_ref[0], p_ref[s]).astype(f32)
            g = part if g is None else g + part
        nm = ADAM_B1 * m_ref[0] + (1.0 - ADAM_B1) * g
        nv = ADAM_B2 * v_ref[0] + (1.0 - ADAM_B2) * (g * g)
        m_hat = nm / (1.0 - ADAM_B1 ** ADAM_STEP)
        v_hat = nv / (1.0 - ADAM_B2 ** ADAM_STEP)
        g_ref[0] = g
        d_ref[0] = -ADAM_LR * (m_hat / (jnp.sqrt(v_hat) + ADAM_EPS) + ADAM_WD * w_ref[0])
        nm_ref[0] = nm
        nv_ref[0] = nv

    blk = pl.BlockSpec((1, tr, c), lambda i, me_ref: (l, i, 0))
    sd = jax.ShapeDtypeStruct((lw, r, c), f32)
    extra = [] if acc is None else list(acc)
    return pl.pallas_call(
        body, name=name, out_shape=[sd] * 4,
        grid_spec=pltpu.PrefetchScalarGridSpec(
            num_scalar_prefetch=1, grid=(r // tr,),
            in_specs=[pl.BlockSpec((NDEV, tr, c), lambda i, me_ref: (0, i, 0)),
                      pl.BlockSpec((1, tr, c), lambda i, me_ref: (me_ref[0], i, 0)), blk, blk, blk]
            + [pl.BlockSpec(memory_space=pl.ANY)] * len(extra),
            out_specs=[blk] * 4),
        input_output_aliases={6 + t: t for t in range(len(extra))},
        compiler_params=_cp(("parallel",)))(me, parts, own, w, m, v, *extra)


def _rope_tables():
    t = jnp.arange(S)
    rows, cols = (t // GRID_W).astype(f32), (t % GRID_W).astype(f32)
    nf = HD // 4
    inv = ROPE_THETA ** (-jnp.arange(nf, dtype=f32) / nf)
    ar, ac = rows[:, None] * inv, cols[:, None] * inv
    cos = jnp.concatenate([jnp.cos(ar), jnp.cos(ar), jnp.cos(ac), jnp.cos(ac)], axis=1)
    sin = jnp.concatenate([-jnp.sin(ar), jnp.sin(ar), -jnp.sin(ac), jnp.sin(ac)], axis=1)
    return cos, sin


def _pack_conv(ca, cc):
    z = lambda n: jnp.zeros((L, n, CT), f32)
    return jnp.concatenate([ca, z(5), cc, z(1)], axis=1)


def _pack_small(npre, npost, ccb, lng, lnb, qn, kn):
    wide = lambda a: jnp.pad(a, ((0, 0), (0, D - HD)))
    return jnp.stack([npre, npost, ccb, lng, lnb, wide(qn), wide(kn), jnp.zeros((L, D), f32)], axis=1).reshape(L * 8, D)


def kernel(x, norm_pre, norm_post, w_in, conv_a_w, q_norm, k_norm, conv_c_w, conv_c_b, ln_c_g, ln_c_b, w_out_a, w_out_b, w_out_c, w_o, loss_target, m_norm_pre, m_norm_post, m_w_in, m_conv_a_w, m_q_norm, m_k_norm, m_conv_c_w, m_conv_c_b, m_ln_c_g, m_ln_c_b, m_w_out_a, m_w_out_b, m_w_out_c, m_w_o, v_norm_pre, v_norm_post, v_w_in, v_conv_a_w, v_q_norm, v_k_norm, v_conv_c_w, v_conv_c_b, v_ln_c_g, v_ln_c_b, v_w_out_a, v_w_out_b, v_w_out_c, v_w_o):
    cos, sin = _rope_tables()
    rs = D // NDEV
    stack_sq = lambda a, b, c, d: jnp.stack([a, b, c, d], axis=1)
    wsq32 = stack_sq(w_out_a, w_out_b, w_out_c, w_o)
    conv_pack = _pack_conv(conv_a_w, conv_c_w)
    vec = lambda a, l: a[l][None, :]
    me = (4 * lax.axis_index("x") + 2 * lax.axis_index("y") + lax.axis_index("c")).astype(jnp.int32).reshape(1)

    def gather_start(l, after):
        return split_start(staged[l], _gather_copies, 12, f"ag_start{l}", after=after)

    def forward_start(l, after):
        s_sems, r_sems, bufs, _ = gathers[l]
        bufs = split_wait(s_sems, r_sems, bufs, _gather_copies, after, f"ag_wait{l}")
        fw = split_start(bufs, _forward_copies, 9, f"ag_fwd_start{l}")
        if l + 1 < L:
            gathers[l + 1] = gather_start(l + 1, fw[3])
            return fw, gathers[l + 1][3]
        return fw, fw[3]

    def forward_wait(fw, after, l):
        s_sems, r_sems, bufs, _ = fw
        return split_wait(s_sems, r_sems, bufs, _forward_copies, after, f"ag_fwd_wait{l}")

    wt, m_wt, v_wt = (jnp.swapaxes(a, 1, 2) for a in (w_in, m_w_in, v_w_in))
    xs, saved = x[0], []
    staged = [stage_shards(wt[l], wsq32[l].reshape(4 * rs, D), conv_pack[l], me, f"stage{l}") for l in range(L)]
    gathers = [gather_start(0, None)] + [None] * (L - 1)
    fw, issued = forward_start(0, xs)
    wg, wsq, convw = forward_wait(fw, issued, 0)
    for l in range(L):
        wsq = wsq.reshape(NDEV, 4, rs, D)
        wfull = wg.reshape(P, D)
        proj, h = proj_fwd(xs, vec(norm_pre, l), wfull, f"proj{l}")
        yah = brancha_fwd(proj, convw, f"bra{l}")
        u1 = branchc1_fwd(proj, convw, vec(conv_c_b, l), f"brc1_{l}")
        qh, kh, vh = qkv_fwd(proj, vec(q_norm, l), vec(k_norm, l), cos, sin, f"qkv{l}")
        o, ybh = attn_fwd(qh, kh, vh, proj, f"attn{l}")
        ln_g = vec(ln_c_g, l)
        if l + 1 < L:
            fw, issued = forward_start(l + 1, o)
            ln_g = ln_g + issued[0, 0]
        ych = branchc2_fwd(u1, proj, ln_g, vec(ln_c_b, l), f"brc2_{l}")
        ya, yb, yc, y16, z, xo = merge_fwd(xs, yah, ybh, ych, proj, wsq, vec(norm_post, l), f"merge{l}")
        saved.append(dict(x=xs, wfull=wfull, wsq=wsq, convw=convw, proj=proj, h=h, yah=yah, ybh=ybh, ych=ych, u1=u1,
                          qh=qh, kh=kh, vh=vh, o=o, ya=ya, yb=yb, yc=yc, y16=y16, z=z))
        xs = xo
        if l + 1 < L:
            wg, wsq, convw = forward_wait(fw, xs, l + 1)
    dx, loss_part = loss_fwd(xs, loss_target[0], "loss")
    loss = lax.psum(loss_part[0, 0], ("x", "y", "c"))

    acc = dict(win=None, sq=None, conv=None)
    small_parts = [None] * L
    msq32 = stack_sq(m_w_out_a, m_w_out_b, m_w_out_c, m_w_o)
    vsq32 = stack_sq(v_w_out_a, v_w_out_b, v_w_out_c, v_w_o)
    mconv, vconv = _pack_conv(m_conv_a_w, m_conv_c_w), _pack_conv(v_conv_a_w, v_conv_c_w)

    def scatter_start(parts, name):
        bufs = parts + [lax.empty(p.shape, p.dtype) for p in parts]
        return split_start(bufs, _scatter_copies(len(parts)), 7 * len(parts), name)

    def finish(l, started, after):
        (s1, r1, b1, _), (s2, r2, b2, _) = started
        gsq_own, rsq = split_wait(s1, r1, b1, _scatter_copies(1), after, f"rs_sq_wait{l}")
        gwin_own, gconv_own, rwin, rconv = split_wait(s2, r2, b2, _scatter_copies(2), after, f"rs_win_wait{l}")
        flat = lambda a: a.reshape(a.shape[0], 4 * rs, D)
        acc["win"] = adam_update(rwin, gwin_own, me, wt, m_wt, v_wt, l, acc["win"], f"adam_win{l}")
        acc["sq"] = adam_update(flat(rsq), flat(gsq_own), me, flat(wsq32), flat(msq32), flat(vsq32), l, acc["sq"], f"adam_wsq{l}")
        acc["conv"] = adam_update(rconv, gconv_own, me, conv_pack, mconv, vconv, l, acc["conv"], f"adam_conv{l}")

    pending = [None] * L
    for l in reversed(range(L)):
        sv = saved[l]
        proj = sv["proj"]
        (dyah, dybh, dych, dma, dmb, dmc, dzb, dyab, dybb, dycb, dgpost) = merge_bwd(
            dx, sv["z"], sv["ya"], sv["yb"], sv["yc"], proj, sv["wsq"], vec(norm_post, l), f"merge_bwd{l}")
        gsq = [tn_matmul(a, b, f"dwsq{t}_{l}") for t, (a, b) in enumerate(
            ((sv["yah"], dyab), (sv["ybh"], dybb), (sv["ych"], dycb), (sv["y16"], dzb)))]
        gsq_parts = jnp.stack([g.reshape(NDEV, rs, D) for g in gsq], axis=1)
        st1 = scatter_start([gsq_parts], f"rs_sq_start{l}")
        convw = sv["convw"] + st1[3][0, 0]
        dab, dac, dax, dag, gca = brancha_bwd(dyah, proj, convw, f"bra_bwd{l}")
        du1, dcg, dlg, dlb, dcb = branchc2_bwd(dych, sv["u1"], proj, vec(ln_c_g, l), vec(ln_c_b, l), f"brc2_bwd{l}")
        dcu, dcv, gcc = branchc1_bwd(du1, proj, convw, f"brc1_bwd{l}")
        dqh, dkh, dvh, dbg = attn_bwd(dybh, sv["o"], sv["qh"], sv["kh"], sv["vh"], proj, f"attn_bwd{l}")
        dq, dk, dv, dqn, dkn = qkv_bwd(dqh, dkh, dvh, proj, vec(q_norm, l), vec(k_norm, l), cos, sin, f"qkv_bwd{l}")
        dproj = jnp.concatenate([dab, dac, dax, dag, dq, dk, dv, dbg, dcu, dcv, dcg, dma, dmb, dmc], axis=1)
        gwin = dwin_parts(sv["h"], dproj, f"dwin{l}").reshape(NDEV, PSH, D)
        gconv = jnp.concatenate([gca, gcc], axis=1)
        st2 = scatter_start([gwin, gconv], f"rs_win_start{l}")
        dx, dgpre = dh_bwd(dproj, sv["wfull"], sv["x"], dx, vec(norm_pre, l) + st2[3][0, 0], f"dh{l}")
        wide = lambda a: jnp.pad(a, ((0, 0), (0, D - HD)))
        small_parts[l] = jnp.concatenate([dgpre, dgpost, dcb, dlg, dlb, wide(dqn), wide(dkn), jnp.zeros((1, D), f32)], axis=0)
        pending[l] = (st1, st2)

    for l in reversed(range(1, L)):
        finish(l, pending[l], after=dx)
    (small_all,) = all_gather([jnp.concatenate(small_parts, axis=0)], "ag_small")
    sm = adam_update(small_all, small_all, me,
                     _pack_small(norm_pre, norm_post, conv_c_b, ln_c_g, ln_c_b, q_norm, k_norm)[None],
                     _pack_small(m_norm_pre, m_norm_post, m_conv_c_b, m_ln_c_g, m_ln_c_b, m_q_norm, m_k_norm)[None],
                     _pack_small(v_norm_pre, v_norm_post, v_conv_c_b, v_ln_c_g, v_ln_c_b, v_q_norm, v_k_norm)[None],
                     0, None, "adam_small")
    finish(0, pending[0], after=sm[0])
    sm = [a.reshape(L, 8, D) for a in sm]
    small_rows = dict(norm_pre=(0, D), norm_post=(1, D), conv_c_b=(2, D), ln_c_g=(3, D), ln_c_b=(4, D), q_norm=(5, HD), k_norm=(6, HD))
    sq_rows = dict(w_out_a=0, w_out_b=1, w_out_c=2, w_o=3)

    order = ["norm_pre", "norm_post", "w_in", "conv_a_w", "q_norm", "k_norm", "conv_c_w", "conv_c_b", "ln_c_g", "ln_c_b",
             "w_out_a", "w_out_b", "w_out_c", "w_o"]
    result = [loss, dx[None]]
    for kind in range(4):
        for nme in order:
            if nme in small_rows:
                rw, wd = small_rows[nme]
                result.append(sm[kind][:, rw, :wd])
            elif nme in sq_rows:
                result.append(acc["sq"][kind][:, sq_rows[nme] * rs:(sq_rows[nme] + 1) * rs])
            elif nme == "w_in":
                result.append(jnp.swapaxes(acc["win"][kind], 1, 2))
            elif nme == "conv_a_w":
                result.append(acc["conv"][kind][:, 0:CA_W])
            else:
                result.append(acc["conv"][kind][:, 8:8 + CC_W])
    return tuple(result)
```

```python
import math

import jax
import jax.numpy as jnp
from jax import lax
from jax.experimental import pallas as pl
from jax.experimental.pallas import tpu as pltpu

f32, bf16 = jnp.float32, jnp.bfloat16

D = 1024
S = 2048
L = 4
HD = 128
NQ = D // HD
NKV = NQ // 4
G = NQ // NKV
WKV = NKV * HD
GRID_W = 64
ROPE_THETA = 10000.0
RMS_EPS = 1e-6
LN_EPS = 1e-5
NDEV = 8
CA_W, CC_W = 3, 31
P = 12 * D + 2 * WKV
PSH = P // NDEV
CT = 128
ADAM_LR, ADAM_B1, ADAM_B2, ADAM_EPS, ADAM_WD, ADAM_STEP = 0.001, 0.9, 0.999, 1e-08, 0.01, 10
VMEM_LIMIT = 56 * 1024 * 1024
MESH = pl.DeviceIdType.MESH

_OFF = {}
_o = 0
for _n, _w in (("a_b", D), ("a_c", D), ("a_x", D), ("a_g", D), ("q", D), ("k", WKV), ("v", WKV), ("b_g", D),
               ("c_u", D), ("c_v", D), ("c_g", D), ("m_a", D), ("m_b", D), ("m_c", D)):
    _OFF[_n] = (_o, _w)
    _o += _w
PIECES = tuple(_OFF)


def _cp(sem=None, **kw):
    return pltpu.CompilerParams(dimension_semantics=sem, vmem_limit_bytes=VMEM_LIMIT, **kw)


def _sig(x):
    return 1.0 / (1.0 + jnp.exp(-x))


def _silu(x):
    return x * _sig(x)


def _dsilu(x):
    s = _sig(x)
    return s * (1.0 + x * (1.0 - s))


def _row_specs(name, tm):
    off, w = _OFF[name]
    bw = math.gcd(off, w) if off else w
    return [pl.BlockSpec((tm, bw), (lambda i, *_, b=off // bw + t: (i, b))) for t in range(w // bw)]


def _cat(refs):
    return refs[0][...] if len(refs) == 1 else jnp.concatenate([r[...] for r in refs], axis=1)


def _chan_spec(name):
    off, _ = _OFF[name]
    return pl.BlockSpec((S, CT), lambda j, b=off // CT: (0, b + j))


def _full(shape):
    return pl.BlockSpec(shape, lambda *_: (0,) * len(shape))


def _coords():
    return lax.axis_index("x"), lax.axis_index("y"), lax.axis_index("c")


def all_gather(shards, name):
    n = len(shards)

    def body(*refs):
        ins, outs = refs[:n], refs[n:2 * n]
        send_sems, recv_sems, local_sems = refs[2 * n:]
        x, y, c = _coords()
        me, sibling = (x, y, c), (x, y, 1 - c)
        chips = [(1 - x, y), (x, 1 - y), (1 - x, 1 - y)]

        def slot(a, p):
            return outs[a].at[4 * p[0] + 2 * p[1] + p[2]]

        def copy(a, k, block, to, src=None):
            return pltpu.make_async_remote_copy(
                src_ref=slot(a, block) if src is None else src, dst_ref=slot(a, block),
                send_sem=send_sems.at[7 * a + k], recv_sem=recv_sems.at[7 * a + k], device_id=to, device_id_type=MESH)

        mine = [pltpu.make_async_copy(ins[a], slot(a, me), local_sems.at[a]) for a in range(n)]
        for cp in mine:
            cp.start()
        first = []
        for a in range(n):
            first.append(copy(a, 0, me, sibling, src=ins[a]))
            first += [copy(a, 1 + j, me, (*chip, c), src=ins[a]) for j, chip in enumerate(chips)]
        for cp in first:
            cp.start()
        passed = []
        for j, chip in enumerate(chips):
            for a in range(n):
                copy(a, 1 + j, (*chip, c), me).wait_recv()
                fw = copy(a, 4 + j, (*chip, c), sibling)
                fw.start()
                passed.append(fw)
        for a in range(n):
            copy(a, 0, sibling, me).wait_recv()
            for j, chip in enumerate(chips):
                copy(a, 4 + j, (*chip, 1 - c), me).wait_recv()
        for cp in first + passed:
            cp.wait_send()
        for cp in mine:
            cp.wait()

    anyspec = pl.BlockSpec(memory_space=pl.ANY)
    return pl.pallas_call(
        body, name=name,
        out_shape=[jax.ShapeDtypeStruct((NDEV,) + s.shape, s.dtype) for s in shards],
        in_specs=[anyspec] * n, out_specs=[anyspec] * n,
        scratch_shapes=[pltpu.SemaphoreType.DMA((7 * n,)), pltpu.SemaphoreType.DMA((7 * n,)), pltpu.SemaphoreType.DMA((n,))],
    )(*shards)


_HBM = pl.BlockSpec(memory_space=pltpu.HBM)
_SEM = pl.BlockSpec(memory_space=pltpu.SEMAPHORE)
_EFFECT = pltpu.SideEffectType.DATAFLOW_SIDE_EFFECTING


def split_start(bufs, make_copies, nsem, name, after=None):
    n = len(bufs)
    extra = [] if after is None else [after]

    def body(*refs):
        send_sems, recv_sems = refs[n + len(extra):n + len(extra) + 2]
        for cp in make_copies(refs[:n], send_sems, recv_sems):
            cp.start()
        refs[-1][...] = jnp.zeros((8, 128), f32)

    res = pl.pallas_call(
        body, name=name,
        out_shape=(pltpu.SemaphoreType.DMA((nsem,)), pltpu.SemaphoreType.DMA((nsem,)),
                   *[pltpu.HBM(b.shape, b.dtype) for b in bufs], jax.ShapeDtypeStruct((8, 128), f32)),
        in_specs=[_HBM] * n + [pl.BlockSpec(memory_space=pl.ANY)] * len(extra),
        out_specs=(_SEM, _SEM, *([_HBM] * n), pl.BlockSpec(memory_space=pltpu.VMEM)),
        input_output_aliases={i: 2 + i for i in range(n)},
        compiler_params=pltpu.CompilerParams(has_side_effects=_EFFECT),
    )(*[pltpu.with_memory_space_constraint(b, pltpu.HBM) for b in bufs], *extra)
    return res[0], res[1], list(res[2:2 + n]), res[-1]


def split_wait(send_sems, recv_sems, bufs, make_copies, after, name):
    n = len(bufs)

    def body(*refs):
        for cp in make_copies(refs[:n], refs[n], refs[n + 1]):
            cp.wait_send()
            cp.wait_recv()

    res = pl.pallas_call(
        body, name=name,
        out_shape=tuple(pltpu.HBM(b.shape, b.dtype) for b in bufs),
        in_specs=[_HBM] * n + [_SEM, _SEM, pl.BlockSpec(memory_space=pl.ANY)],
        out_specs=[_HBM] * n,
        input_output_aliases={i: i for i in range(n)},
        compiler_params=pltpu.CompilerParams(has_side_effects=_EFFECT),
    )(*bufs, send_sems, recv_sems, after)
    return list(res)


def _scatter_copies(n):
    def make(refs, send_sems, recv_sems):
        x, y, c = _coords()
        me = 4 * x + 2 * y + c
        copies = []
        for a in range(n):
            for k in range(1, NDEV):
                px = 1 - x if (k >> 2) & 1 else x
                py = 1 - y if (k >> 1) & 1 else y
                pc = 1 - c if k & 1 else c
                copies.append(pltpu.make_async_remote_copy(
                    src_ref=refs[a].at[4 * px + 2 * py + pc], dst_ref=refs[n + a].at[me],
                    send_sem=send_sems.at[7 * a + k - 1], recv_sem=recv_sems.at[7 * a + k - 1],
                    device_id=(px, py, pc), device_id_type=MESH))
        return copies
    return make


def _gather_copies(refs, send_sems, recv_sems):
    x, y, c = _coords()
    me = 4 * x + 2 * y + c
    targets = [(x, y, 1 - c), (1 - x, y, c), (x, 1 - y, c), (1 - x, 1 - y, c)]
    return [pltpu.make_async_remote_copy(
        src_ref=r.at[me], dst_ref=r.at[me], send_sem=send_sems.at[4 * a + k], recv_sem=recv_sems.at[4 * a + k],
        device_id=to, device_id_type=MESH) for a, r in enumerate(refs) for k, to in enumerate(targets)]


def _forward_copies(refs, send_sems, recv_sems):
    x, y, c = _coords()
    chips = [(1 - x, y), (x, 1 - y), (1 - x, 1 - y)]
    return [pltpu.make_async_remote_copy(
        src_ref=r.at[4 * px + 2 * py + c], dst_ref=r.at[4 * px + 2 * py + c], send_sem=send_sems.at[3 * a + j],
        recv_sem=recv_sems.at[3 * a + j], device_id=(x, y, 1 - c), device_id_type=MESH)
        for a, r in enumerate(refs) for j, (px, py) in enumerate(chips)]


def _row_tile(r):
    return r if r <= 256 else max(t for t in (256, 160, 128) if r % t == 0)


def stage_shards(wt_l, wsq_l, conv_l, me, name):
    outs = []
    for a, dt in ((wt_l, bf16), (wsq_l, bf16), (conv_l, f32)):
        r, c = a.shape
        tr = _row_tile(r)

        def body(me_ref, a_ref, o_ref):
            o_ref[0] = a_ref[...].astype(o_ref.dtype)

        outs.append(pl.pallas_call(
            body, name=f"{name}_{len(outs)}", out_shape=jax.ShapeDtypeStruct((NDEV, r, c), dt),
            grid_spec=pltpu.PrefetchScalarGridSpec(
                num_scalar_prefetch=1, grid=(r // tr,),
                in_specs=[pl.BlockSpec((tr, c), lambda i, m: (i, 0))],
                out_specs=pl.BlockSpec((1, tr, c), lambda i, m: (m[0], i, 0))),
            compiler_params=_cp(("arbitrary",)))(me, a))
    return outs


def proj_fwd(xin, g_pre, wt, name):
    tm, tn = min(S, 1024), 1280

    def body(x_ref, g_ref, w_ref, proj_ref, h_ref, hs):
        @pl.when(pl.program_id(1) == 0)
        def _():
            x = x_ref[...]
            r = lax.rsqrt(jnp.mean(x * x, axis=-1, keepdims=True) + RMS_EPS)
            h = (x * r * g_ref[...]).astype(bf16)
            hs[...] = h
            h_ref[...] = h
        proj_ref[...] = lax.dot_general(hs[...], w_ref[...], (((1,), (1,)), ((), ())), preferred_element_type=f32)

    return pl.pallas_call(
        body, name=name, grid=(S // tm, P // tn),
        out_shape=[jax.ShapeDtypeStruct((S, P), f32), jax.ShapeDtypeStruct((S, D), bf16)],
        in_specs=[pl.BlockSpec((tm, D), lambda i, j: (i, 0)), _full((1, D)), pl.BlockSpec((tn, D), lambda i, j: (j, 0))],
        out_specs=[pl.BlockSpec((tm, tn), lambda i, j: (i, j)), pl.BlockSpec((tm, D), lambda i, j: (i, 0))],
        scratch_shapes=[pltpu.VMEM((tm, D), bf16)],
        compiler_params=_cp(("parallel", "arbitrary")))(xin, g_pre, wt)


RC = 128


def _fill_pad(pad, halo, val_fn):
    pad[0:halo, :] = jnp.zeros((halo, CT), f32)
    pad[S + halo:S + 2 * halo, :] = jnp.zeros((halo, CT), f32)

    def step(i, carry):
        rows = pl.ds(pl.multiple_of(i * RC, RC), RC)
        pad[pl.ds(pl.multiple_of(i * RC, RC) + halo, RC), :] = val_fn(rows)
        return carry
    lax.fori_loop(0, S // RC, step, 0)


def brancha_fwd(proj, convw, name):
    def body(ab, ac, ax, ag, w_ref, o_ref, pad):
        _fill_pad(pad, 8, lambda rows: ac[rows, :] * ax[rows, :])
        w = [w_ref[0, k:k + 1, :] for k in range(CA_W)]

        def step(i, carry):
            base = pl.multiple_of(i * RC, RC)
            rows = pl.ds(base, RC)
            t = sum(w[k] * pad[pl.ds(base + 7 + k, RC), :] for k in range(CA_W))
            o_ref[rows, :] = (ab[rows, :] * t * _silu(ag[rows, :])).astype(bf16)
            return carry
        lax.fori_loop(0, S // RC, step, 0)

    return pl.pallas_call(
        body, name=name, grid=(D // CT,), out_shape=jax.ShapeDtypeStruct((S, D), bf16),
        in_specs=[_chan_spec("a_b"), _chan_spec("a_c"), _chan_spec("a_x"), _chan_spec("a_g"),
                  pl.BlockSpec((1, 40, CT), lambda j: (j, 0, 0))],
        out_specs=pl.BlockSpec((S, CT), lambda j: (0, j)),
        scratch_shapes=[pltpu.VMEM((S + 16, CT), f32)],
        compiler_params=_cp(("parallel",)))(proj, proj, proj, proj, convw)


def branchc1_fwd(proj, convw, cbias, name):
    def body(cu, cv, w_ref, b_ref, o_ref, pad):
        _fill_pad(pad, 16, lambda rows: cu[rows, :] * _sig(cv[rows, :]))

        def step(i, carry):
            base = pl.multiple_of(i * RC, RC)
            acc = jnp.zeros((RC, CT), f32) + b_ref[...]
            for k in range(CC_W):
                acc = acc + w_ref[0, 8 + k:9 + k, :] * pad[pl.ds(base + k + 1, RC), :]
            o_ref[pl.ds(base, RC), :] = acc
            return carry
        lax.fori_loop(0, S // RC, step, 0)

    return pl.pallas_call(
        body, name=name, grid=(D // CT,), out_shape=jax.ShapeDtypeStruct((S, D), f32),
        in_specs=[_chan_spec("c_u"), _chan_spec("c_v"), pl.BlockSpec((1, 40, CT), lambda j: (j, 0, 0)),
                  pl.BlockSpec((1, CT), lambda j: (0, j))],
        out_specs=pl.BlockSpec((S, CT), lambda j: (0, j)),
        scratch_shapes=[pltpu.VMEM((S + 32, CT), f32)],
        compiler_params=_cp(("parallel",)))(proj, proj, convw, cbias)


def _swap32(x):
    lane = lax.broadcasted_iota(jnp.int32, x.shape, 1)
    return jnp.where((lane // 32) % 2 == 1, pltpu.roll(x, 32, 1), pltpu.roll(x, HD - 32, 1))


def _rope(y, cos, sin):
    return y * cos + _swap32(y) * sin


def qkv_fwd(proj, qn, kn, cos, sin, name):
    tm = min(S, 256)
    nq, nk, nv = len(_row_specs("q", tm)), len(_row_specs("k", tm)), len(_row_specs("v", tm))

    def body(*refs):
        q = _cat(refs[:nq])
        k = _cat(refs[nq:nq + nk])
        v = _cat(refs[nq + nk:nq + nk + nv])
        qn_ref, kn_ref, cos_ref, sin_ref, qh_ref, kh_ref, vh_ref = refs[nq + nk + nv:]
        cos, sin = cos_ref[...], sin_ref[...]

        def heads(xx, gn, out_ref, n):
            for h in range(n):
                xh = xx[:, h * HD:(h + 1) * HD]
                r = lax.rsqrt(jnp.mean(xh * xh, axis=-1, keepdims=True) + RMS_EPS)
                out_ref[:, h * HD:(h + 1) * HD] = _rope(xh * r * gn, cos, sin).astype(bf16)
        heads(q, qn_ref[...], qh_ref, NQ)
        heads(k, kn_ref[...], kh_ref, NKV)
        vh_ref[...] = v.astype(bf16)

    row = lambda w: pl.BlockSpec((tm, w), lambda i: (i, 0))
    return pl.pallas_call(
        body, name=name, grid=(S // tm,),
        out_shape=[jax.ShapeDtypeStruct((S, D), bf16), jax.ShapeDtypeStruct((S, WKV), bf16), jax.ShapeDtypeStruct((S, WKV), bf16)],
        in_specs=_row_specs("q", tm) + _row_specs("k", tm) + _row_specs("v", tm) + [_full((1, HD)), _full((1, HD)), row(HD), row(HD)],
        out_specs=[row(D), row(WKV), row(WKV)],
        compiler_params=_cp(("parallel",)))(*([proj] * (nq + nk + nv)), qn, kn, cos, sin)


def _softmax_rows(q, k):
    s = lax.dot_general(q, k, (((1,), (1,)), ((), ())), preferred_element_type=f32)
    p = jnp.exp((s - jnp.max(s, axis=-1, keepdims=True)) * (HD ** -0.5))
    return p, 1.0 / jnp.sum(p, axis=-1, keepdims=True)


GW = G * HD


def attn_fwd(qh, kh, vh, proj, name):
    tq = min(S, 256)
    bg_blk = _OFF["b_g"][0] // GW

    def body(q_ref, k_ref, v_ref, bg_ref, o_ref, y_ref):
        k, v = k_ref[...], v_ref[...]
        for g in range(G):
            cols = slice(g * HD, (g + 1) * HD)
            p, rl = _softmax_rows(q_ref[:, cols], k)
            o = jnp.dot(p.astype(bf16), v, preferred_element_type=f32) * rl
            o_ref[:, cols] = o
            y_ref[:, cols] = (o * _silu(bg_ref[:, cols])).astype(bf16)

    grp = pl.BlockSpec((tq, GW), lambda kv, i: (i, kv))
    kvs = pl.BlockSpec((S, HD), lambda kv, i: (0, kv))
    return pl.pallas_call(
        body, name=name, grid=(NKV, S // tq),
        out_shape=[jax.ShapeDtypeStruct((S, D), f32), jax.ShapeDtypeStruct((S, D), bf16)],
        in_specs=[grp, kvs, kvs, pl.BlockSpec((tq, GW), lambda kv, i: (i, bg_blk + kv))],
        out_specs=[grp, grp],
        compiler_params=_cp(("parallel", "parallel")))(qh, kh, vh, proj)


def _ln_parts(u1):
    mu = jnp.mean(u1, axis=-1, keepdims=True)
    xc = u1 - mu
    rstd = lax.rsqrt(jnp.mean(xc * xc, axis=-1, keepdims=True) + LN_EPS)
    return xc * rstd, rstd


def branchc2_fwd(u1, proj, lng, lnb, name):
    tm = min(S, 256)
    ncg = len(_row_specs("c_g", tm))

    def body(*refs):
        u_ref = refs[0]
        cg = _cat(refs[1:1 + ncg])
        g_ref, b_ref, o_ref = refs[1 + ncg:]
        xh, _ = _ln_parts(u_ref[...])
        o_ref[...] = (_silu(xh * g_ref[...] + b_ref[...]) * _silu(cg)).astype(bf16)

    row = pl.BlockSpec((tm, D), lambda i: (i, 0))
    return pl.pallas_call(
        body, name=name, grid=(S // tm,), out_shape=jax.ShapeDtypeStruct((S, D), bf16),
        in_specs=[row] + _row_specs("c_g", tm) + [_full((1, D)), _full((1, D))], out_specs=row,
        compiler_params=_cp(("parallel",)))(u1, *([proj] * ncg), lng, lnb)


def _wmat(w_ref, kind):
    return w_ref[:, kind].reshape(D, D)


def merge_fwd(xin, yah, ybh, ych, proj, wsq, g_post, name):
    tm = min(S, 256)
    nm = len(_row_specs("m_a", tm))

    def body(*refs):
        x_ref, a_ref, b_ref, c_ref = refs[:4]
        ms = [_cat(refs[4 + t * nm:4 + (t + 1) * nm]) for t in range(3)]
        w_ref, g_ref, ya_ref, yb_ref, yc_ref, y_ref, z_ref, o_ref = refs[4 + 3 * nm:]
        y = jnp.zeros((tm, D), f32)
        for t, (h_ref, out_ref) in enumerate(((a_ref, ya_ref), (b_ref, yb_ref), (c_ref, yc_ref))):
            yt = jnp.dot(h_ref[...], _wmat(w_ref, t), preferred_element_type=f32)
            out_ref[...] = yt
            y = y + _sig(ms[t]) * yt
        yb16 = y.astype(bf16)
        y_ref[...] = yb16
        z = jnp.dot(yb16, _wmat(w_ref, 3), preferred_element_type=f32)
        z_ref[...] = z
        r = lax.rsqrt(jnp.mean(z * z, axis=-1, keepdims=True) + RMS_EPS)
        o_ref[...] = x_ref[...] + z * r * g_ref[...]

    row = pl.BlockSpec((tm, D), lambda i: (i, 0))
    sd = lambda dt: jax.ShapeDtypeStruct((S, D), dt)
    return pl.pallas_call(
        body, name=name, grid=(S // tm,),
        out_shape=[sd(f32), sd(f32), sd(f32), sd(bf16), sd(f32), sd(f32)],
        in_specs=[row] * 4 + _row_specs("m_a", tm) + _row_specs("m_b", tm) + _row_specs("m_c", tm)
        + [_full((NDEV, 4, D // NDEV, D)), _full((1, D))],
        out_specs=[row] * 6,
        compiler_params=_cp(("parallel",)))(xin, yah, ybh, ych, *([proj] * (3 * nm)), wsq, g_post)


def loss_fwd(y, target, name):
    tm = min(S, 256)

    def body(y_ref, t_ref, dy_ref, l_ref):
        e = y_ref[...] - t_ref[...]
        dy_ref[...] = e / D

        @pl.when(pl.program_id(0) == 0)
        def _():
            l_ref[...] = jnp.zeros((1, 128), f32)
        l_ref[...] += (0.5 / D) * jnp.sum(e * e)

    row = pl.BlockSpec((tm, D), lambda i: (i, 0))
    return pl.pallas_call(
        body, name=name, grid=(S // tm,),
        out_shape=[jax.ShapeDtypeStruct((S, D), f32), jax.ShapeDtypeStruct((1, 128), f32)],
        in_specs=[row, row], out_specs=[row, _full((1, 128))],
        compiler_params=_cp(("arbitrary",)))(y, target)


def _acc(ref, val):
    @pl.when(pl.program_id(0) == 0)
    def _():
        ref[...] = jnp.zeros(ref.shape, f32)
    ref[...] += val


def _emit_copies(stash, dst, sems, windows):
    return [pltpu.make_async_copy(stash.at[p], dst.at[w], sems.at[p]) for p, w in enumerate(windows)]


def _emit_drain_previous(copies, step):
    @pl.when(step > 0)
    def _():
        for cp in copies:
            cp.wait()


def _emit_start(copies, step, nsteps):
    for cp in copies:
        cp.start()

    @pl.when(step == nsteps - 1)
    def _():
        for cp in copies:
            cp.wait()


def merge_bwd(dout, z, ya, yb, yc, proj, wsq, g_post, name):
    tm = min(S, 256)
    nm = len(_row_specs("m_a", tm))
    nsteps = S // tm

    def body(*refs):
        do_ref, z_ref, ya_ref, yb_ref, yc_ref = refs[:5]
        ms = [_cat(refs[5 + t * nm:5 + (t + 1) * nm]) for t in range(3)]
        w_ref, g_ref = refs[5 + 3 * nm:7 + 3 * nm]
        dh_refs = refs[7 + 3 * nm:10 + 3 * nm]
        dzb_ref = refs[10 + 3 * nm]
        dyb_refs = refs[11 + 3 * nm:14 + 3 * nm]
        dg_ref = refs[14 + 3 * nm]
        dproj_ref, stash, sems = refs[15 + 3 * nm:]
        i = pl.program_id(0)
        rows = pl.ds(pl.multiple_of(i * tm, tm), tm)
        copies = _emit_copies(stash, dproj_ref, sems, [(rows, pl.ds(_OFF[n][0], D)) for n in ("m_a", "m_b", "m_c")])
        nt = (((1,), (1,)), ((), ()))
        z, dout = z_ref[...], do_ref[...]
        r = lax.rsqrt(jnp.mean(z * z, axis=-1, keepdims=True) + RMS_EPS)
        zh = z * r
        _acc(dg_ref, jnp.sum(dout * zh, axis=0, keepdims=True))
        dzh = dout * g_ref[...]
        dz = (r * (dzh - zh * jnp.mean(dzh * zh, axis=-1, keepdims=True))).astype(bf16)
        dzb_ref[...] = dz
        dy = lax.dot_general(dz, _wmat(w_ref, 3), nt, preferred_element_type=f32)
        dms = []
        for t, yt_ref in enumerate((ya_ref, yb_ref, yc_ref)):
            sg = _sig(ms[t])
            dyt = (dy * sg).astype(bf16)
            dyb_refs[t][...] = dyt
            dms.append((dy * yt_ref[...] * sg * (1.0 - sg)).astype(bf16))
            dh_refs[t][...] = lax.dot_general(dyt, _wmat(w_ref, t), nt, preferred_element_type=f32)
        _emit_drain_previous(copies, i)
        for t in range(3):
            stash[t] = dms[t]
        _emit_start(copies, i, nsteps)

    row = pl.BlockSpec((tm, D), lambda i: (i, 0))
    sd = lambda dt: jax.ShapeDtypeStruct((S, D), dt)
    return pl.pallas_call(
        body, name=name, grid=(nsteps,),
        out_shape=[sd(f32)] * 3 + [sd(bf16)] * 4 + [jax.ShapeDtypeStruct((1, D), f32), jax.ShapeDtypeStruct((S, P), bf16)],
        in_specs=[row] * 5 + _row_specs("m_a", tm) + _row_specs("m_b", tm) + _row_specs("m_c", tm)
        + [_full((NDEV, 4, D // NDEV, D)), _full((1, D))],
        out_specs=[row] * 7 + [_full((1, D)), pl.BlockSpec(memory_space=pl.ANY)],
        scratch_shapes=[pltpu.VMEM((3, tm, D), bf16), pltpu.SemaphoreType.DMA((3,))],
        compiler_params=_cp(("arbitrary",)))(dout, z, ya, yb, yc, *([proj] * (3 * nm)), wsq, g_post)


def tn_matmul(a, b, name):
    m, n = a.shape[1], b.shape[1]
    tmm = min(m, 512)

    def body(a_ref, b_ref, o_ref):
        o_ref[...] = lax.dot_general(a_ref[...], b_ref[...], (((0,), (0,)), ((), ())), preferred_element_type=f32).astype(bf16)

    return pl.pallas_call(
        body, name=name, grid=(m // tmm,), out_shape=jax.ShapeDtypeStruct((m, n), bf16),
        in_specs=[pl.BlockSpec((S, tmm), lambda i: (0, i)), _full((S, n))],
        out_specs=pl.BlockSpec((tmm, n), lambda i: (i, 0)),
        compiler_params=_cp(("parallel",)))(a, b)


def dwin_parts(h, dproj, name):
    tn = 640

    def body(d_ref, h_ref, o_ref):
        o_ref[...] = lax.dot_general(d_ref[...], h_ref[...], (((0,), (0,)), ((), ())), preferred_element_type=f32).astype(bf16)

    return pl.pallas_call(
        body, name=name, grid=(P // tn,), out_shape=jax.ShapeDtypeStruct((P, D), bf16),
        in_specs=[pl.BlockSpec((S, tn), lambda j: (0, j)), _full((S, D))],
        out_specs=pl.BlockSpec((tn, D), lambda j: (j, 0)),
        compiler_params=_cp(("parallel",)))(dproj, h)


def _chan_windows(names, j):
    return [(slice(None), pl.ds(pl.multiple_of(_OFF[n][0] + j * CT, CT), CT)) for n in names]


def brancha_bwd(dyah, proj, convw, dproj, name):
    nsteps = D // CT

    def body(d_ref, ab, ac, ax, ag, w_ref, _, dw_ref, dproj_ref, padp, padt, accw, stash, sems):
        j = pl.program_id(0)
        copies = _emit_copies(stash, dproj_ref, sems, _chan_windows(("a_b", "a_c", "a_x", "a_g"), j))
        _fill_pad(padp, 8, lambda rows: ac[rows, :] * ax[rows, :])
        _fill_pad(padt, 8, lambda rows: d_ref[rows, :] * ab[rows, :] * _silu(ag[rows, :]))
        accw[...] = jnp.zeros(accw.shape, f32)
        w = [w_ref[0, k:k + 1, :] for k in range(CA_W)]
        _emit_drain_previous(copies, j)

        def step(i, carry):
            base = pl.multiple_of(i * RC, RC)
            rows = pl.ds(base, RC)
            ps = [padp[pl.ds(base + 7 + k, RC), :] for k in range(CA_W)]
            t = sum(w[k] * ps[k] for k in range(CA_W))
            dp = sum(w[k] * padt[pl.ds(base + 9 - k, RC), :] for k in range(CA_W))
            d, a_b, a_g = d_ref[rows, :], ab[rows, :], ag[rows, :]
            stash[0, rows, :] = (d * t * _silu(a_g)).astype(bf16)
            stash[1, rows, :] = (dp * ax[rows, :]).astype(bf16)
            stash[2, rows, :] = (dp * ac[rows, :]).astype(bf16)
            stash[3, rows, :] = (d * a_b * t * _dsilu(a_g)).astype(bf16)
            dt = padt[pl.ds(base + 8, RC), :]
            for k in range(CA_W):
                accw[8 * k:8 * k + 8, :] += jnp.sum((dt * ps[k]).reshape(RC // 8, 8, CT), axis=0)
            return carry
        lax.fori_loop(0, S // RC, step, 0)
        _emit_start(copies, j, nsteps)
        dw_ref[0] = jnp.zeros((8, CT), f32)
        for k in range(CA_W):
            dw_ref[0, k:k + 1, :] = jnp.sum(accw[8 * k:8 * k + 8, :], axis=0, keepdims=True)

    tile = pl.BlockSpec((S, CT), lambda j: (0, j))
    anyspec = pl.BlockSpec(memory_space=pl.ANY)
    return pl.pallas_call(
        body, name=name, grid=(nsteps,),
        out_shape=[jax.ShapeDtypeStruct((NDEV, 8, CT), f32), jax.ShapeDtypeStruct((S, P), bf16)],
        in_specs=[tile, _chan_spec("a_b"), _chan_spec("a_c"), _chan_spec("a_x"), _chan_spec("a_g"),
                  pl.BlockSpec((1, 40, CT), lambda j: (j, 0, 0)), anyspec],
        out_specs=[pl.BlockSpec((1, 8, CT), lambda j: (j, 0, 0)), anyspec],
        input_output_aliases={6: 1},
        scratch_shapes=[pltpu.VMEM((S + 16, CT), f32), pltpu.VMEM((S + 16, CT), f32), pltpu.VMEM((8 * CA_W, CT), f32),
                        pltpu.VMEM((4, S, CT), bf16), pltpu.SemaphoreType.DMA((4,))],
        compiler_params=_cp(("arbitrary",)))(dyah, proj, proj, proj, proj, convw, dproj)


def branchc2_bwd(dych, u1, proj, lng, lnb, dproj, name):
    tm = min(S, 256)
    ncg = len(_row_specs("c_g", tm))
    nsteps = S // tm

    def body(*refs):
        d_ref, u_ref = refs[:2]
        cg = _cat(refs[2:2 + ncg])
        g_ref, b_ref, _, du_ref, dlg_ref, dlb_ref, dcb_ref, dproj_ref, stash, sems = refs[2 + ncg:]
        i = pl.program_id(0)
        copies = _emit_copies(stash, dproj_ref, sems, [(pl.ds(pl.multiple_of(i * tm, tm), tm), pl.ds(_OFF["c_g"][0], D))])
        d = d_ref[...]
        xh, rstd = _ln_parts(u_ref[...])
        ln = xh * g_ref[...] + b_ref[...]
        _emit_drain_previous(copies, i)
        stash[0] = (d * _silu(ln) * _dsilu(cg)).astype(bf16)
        _emit_start(copies, i, nsteps)
        dln = d * _silu(cg) * _dsilu(ln)
        _acc(dlg_ref, jnp.sum(dln * xh, axis=0, keepdims=True))
        _acc(dlb_ref, jnp.sum(dln, axis=0, keepdims=True))
        dxh = dln * g_ref[...]
        du = rstd * (dxh - jnp.mean(dxh, axis=-1, keepdims=True) - xh * jnp.mean(dxh * xh, axis=-1, keepdims=True))
        du_ref[...] = du
        _acc(dcb_ref, jnp.sum(du, axis=0, keepdims=True))

    row = pl.BlockSpec((tm, D), lambda i: (i, 0))
    vec = jax.ShapeDtypeStruct((1, D), f32)
    anyspec = pl.BlockSpec(memory_space=pl.ANY)
    return pl.pallas_call(
        body, name=name, grid=(nsteps,),
        out_shape=[jax.ShapeDtypeStruct((S, D), f32), vec, vec, vec, jax.ShapeDtypeStruct((S, P), bf16)],
        in_specs=[row, row] + _row_specs("c_g", tm) + [_full((1, D)), _full((1, D)), anyspec],
        out_specs=[row, _full((1, D)), _full((1, D)), _full((1, D)), anyspec],
        input_output_aliases={4 + ncg: 4},
        scratch_shapes=[pltpu.VMEM((1, tm, D), bf16), pltpu.SemaphoreType.DMA((1,))],
        compiler_params=_cp(("arbitrary",)))(dych, u1, *([proj] * ncg), lng, lnb, dproj)


def branchc1_bwd(du1, proj, convw, dproj, name):
    nsteps = D // CT

    def body(d_ref, cu, cv, w_ref, _, dw_ref, dproj_ref, padu, padd, accw, stash, sems):
        j = pl.program_id(0)
        copies = _emit_copies(stash, dproj_ref, sems, _chan_windows(("c_u", "c_v"), j))
        _fill_pad(padu, 16, lambda rows: cu[rows, :] * _sig(cv[rows, :]))
        _fill_pad(padd, 16, lambda rows: d_ref[rows, :])
        accw[...] = jnp.zeros(accw.shape, f32)
        _emit_drain_previous(copies, j)

        def step(i, carry):
            base = pl.multiple_of(i * RC, RC)
            rows = pl.ds(base, RC)
            d = d_ref[rows, :]
            du0 = jnp.zeros((RC, CT), f32)
            for k in range(CC_W):
                du0 = du0 + w_ref[0, 8 + k:9 + k, :] * padd[pl.ds(base + 31 - k, RC), :]
                accw[8 * k:8 * k + 8, :] += jnp.sum((d * padu[pl.ds(base + k + 1, RC), :]).reshape(RC // 8, 8, CT), axis=0)
            sg = _sig(cv[rows, :])
            stash[0, rows, :] = (du0 * sg).astype(bf16)
            stash[1, rows, :] = (du0 * cu[rows, :] * sg * (1.0 - sg)).astype(bf16)
            return carry
        lax.fori_loop(0, S // RC, step, 0)
        _emit_start(copies, j, nsteps)
        dw_ref[0] = jnp.zeros((32, CT), f32)
        for k in range(CC_W):
            dw_ref[0, k:k + 1, :] = jnp.sum(accw[8 * k:8 * k + 8, :], axis=0, keepdims=True)

    tile = pl.BlockSpec((S, CT), lambda j: (0, j))
    anyspec = pl.BlockSpec(memory_space=pl.ANY)
    return pl.pallas_call(
        body, name=name, grid=(nsteps,),
        out_shape=[jax.ShapeDtypeStruct((NDEV, 32, CT), f32), jax.ShapeDtypeStruct((S, P), bf16)],
        in_specs=[tile, _chan_spec("c_u"), _chan_spec("c_v"), pl.BlockSpec((1, 40, CT), lambda j: (j, 0, 0)), anyspec],
        out_specs=[pl.BlockSpec((1, 32, CT), lambda j: (j, 0, 0)), anyspec],
        input_output_aliases={4: 1},
        scratch_shapes=[pltpu.VMEM((S + 32, CT), f32), pltpu.VMEM((S + 32, CT), f32), pltpu.VMEM((8 * 32, CT), f32),
                        pltpu.VMEM((2, S, CT), bf16), pltpu.SemaphoreType.DMA((2,))],
        compiler_params=_cp(("arbitrary",)))(du1, proj, proj, convw, dproj)


def attn_bwd(dybh, o, qh, kh, vh, proj, dproj, name):
    tq = min(S, 256)
    bg_blk = _OFF["b_g"][0] // GW

    def body(d_ref, o_ref, q_ref, k_ref, v_ref, bg_ref, _, dq_ref, dk_ref, dv_ref, dbg_ref):
        @pl.when(pl.program_id(1) == 0)
        def _():
            dk_ref[...] = jnp.zeros(dk_ref.shape, f32)
            dv_ref[...] = jnp.zeros(dv_ref.shape, f32)
        k, v = k_ref[...], v_ref[...]
        tn = (((0,), (0,)), ((), ()))
        dk_acc = jnp.zeros((S, HD), f32)
        dv_acc = jnp.zeros((S, HD), f32)
        for g in range(G):
            cols = slice(g * HD, (g + 1) * HD)
            d, bg, q, o = d_ref[:, cols], bg_ref[:, cols], q_ref[:, cols], o_ref[:, cols]
            dbg_ref[:, cols] = (d * o * _dsilu(bg)).astype(bf16)
            do = d * _silu(bg)
            p, rl = _softmax_rows(q, k)
            dv_acc = dv_acc + lax.dot_general(p.astype(bf16), (do * rl).astype(bf16), tn, preferred_element_type=f32)
            dp = lax.dot_general(do.astype(bf16), v, (((1,), (1,)), ((), ())), preferred_element_type=f32)
            delta = jnp.sum(do * o, axis=-1, keepdims=True)
            ds = (p * (dp - delta)).astype(bf16)
            rs_ = rl * (HD ** -0.5)
            dq_ref[:, cols] = jnp.dot(ds, k, preferred_element_type=f32) * rs_
            dk_acc = dk_acc + lax.dot_general(ds, (q.astype(f32) * rs_).astype(bf16), tn, preferred_element_type=f32)
        dk_ref[...] += dk_acc
        dv_ref[...] += dv_acc

    grp = pl.BlockSpec((tq, GW), lambda kv, i: (i, kv))
    kvs = pl.BlockSpec((S, HD), lambda kv, i: (0, kv))
    return pl.pallas_call(
        body, name=name, grid=(NKV, S // tq),
        out_shape=[jax.ShapeDtypeStruct((S, D), f32), jax.ShapeDtypeStruct((S, WKV), f32),
                   jax.ShapeDtypeStruct((S, WKV), f32), jax.ShapeDtypeStruct((S, P), bf16)],
        in_specs=[grp, grp, grp, kvs, kvs, pl.BlockSpec((tq, GW), lambda kv, i: (i, bg_blk + kv)),
                  pl.BlockSpec(memory_space=pl.ANY)],
        out_specs=[grp, kvs, kvs, pl.BlockSpec((tq, GW), lambda kv, i: (i, bg_blk + kv))],
        input_output_aliases={6: 3},
        compiler_params=_cp(("parallel", "arbitrary")))(dybh, o, qh, kh, vh, proj, dproj)


def qkv_bwd(dqh, dkh, dvh, proj, qn, kn, cos, sin, dproj, name):
    tm = min(S, 256)
    nq, nk = len(_row_specs("q", tm)), len(_row_specs("k", tm))
    nsteps = S // tm
    wq = D + 2 * WKV

    def body(*refs):
        dqh_ref, dkh_ref, dvh_ref = refs[:3]
        q = _cat(refs[3:3 + nq])
        k = _cat(refs[3 + nq:3 + nq + nk])
        qn_ref, kn_ref, cos_ref, sin_ref, _, dqn_ref, dkn_ref, dproj_ref, stash, sems = refs[3 + nq + nk:]
        i = pl.program_id(0)
        copies = _emit_copies(stash, dproj_ref, sems, [(pl.ds(pl.multiple_of(i * tm, tm), tm), pl.ds(_OFF["q"][0], wq))])
        cos, sin = cos_ref[...], sin_ref[...]
        _emit_drain_previous(copies, i)

        def heads(xx, dd, gn, col0, dgn_ref, n):
            dg = jnp.zeros((1, HD), f32)
            for h in range(n):
                xh = xx[:, h * HD:(h + 1) * HD]
                dh = dd[:, h * HD:(h + 1) * HD]
                r = lax.rsqrt(jnp.mean(xh * xh, axis=-1, keepdims=True) + RMS_EPS)
                xn = xh * r
                dy = dh * cos + _swap32(dh * sin)
                dg = dg + jnp.sum(dy * xn, axis=0, keepdims=True)
                dxn = dy * gn
                stash[0, :, col0 + h * HD:col0 + (h + 1) * HD] = (
                    r * (dxn - xn * jnp.mean(dxn * xn, axis=-1, keepdims=True))).astype(bf16)
            _acc(dgn_ref, dg)
        heads(q, dqh_ref[...], qn_ref[...], 0, dqn_ref, NQ)
        heads(k, dkh_ref[...], kn_ref[...], D, dkn_ref, NKV)
        stash[0, :, D + WKV:wq] = dvh_ref[...].astype(bf16)
        _emit_start(copies, i, nsteps)

    row = lambda w: pl.BlockSpec((tm, w), lambda i: (i, 0))
    vec = jax.ShapeDtypeStruct((1, HD), f32)
    anyspec = pl.BlockSpec(memory_space=pl.ANY)
    return pl.pallas_call(
        body, name=name, grid=(nsteps,),
        out_shape=[vec, vec, jax.ShapeDtypeStruct((S, P), bf16)],
        in_specs=[row(D), row(WKV), row(WKV)] + _row_specs("q", tm) + _row_specs("k", tm)
        + [_full((1, HD)), _full((1, HD)), row(HD), row(HD), anyspec],
        out_specs=[_full((1, HD)), _full((1, HD)), anyspec],
        input_output_aliases={7 + nq + nk: 2},
        scratch_shapes=[pltpu.VMEM((1, tm, wq), bf16), pltpu.SemaphoreType.DMA((1,))],
        compiler_params=_cp(("arbitrary",)))(dqh, dkh, dvh, *([proj] * (nq + nk)), qn, kn, cos, sin, dproj)


def dh_bwd(dproj, wfull, xin, dout, g_pre, name):
    tm, tk = min(S, 1024), 1280
    nk = P // tk

    def body(d_ref, w_ref, x_ref, do_ref, g_ref, dx_ref, dg_ref, acc):
        kk = pl.program_id(1)

        @pl.when(kk == 0)
        def _():
            acc[...] = jnp.zeros(acc.shape, f32)
        acc[...] += jnp.dot(d_ref[...], w_ref[...], preferred_element_type=f32)

        @pl.when((kk == 0) & (pl.program_id(0) == 0))
        def _():
            dg_ref[...] = jnp.zeros(dg_ref.shape, f32)

        @pl.when(kk == nk - 1)
        def _():
            x, dh = x_ref[...], acc[...]
            r = lax.rsqrt(jnp.mean(x * x, axis=-1, keepdims=True) + RMS_EPS)
            xn = x * r
            dg_ref[...] += jnp.sum(dh * xn, axis=0, keepdims=True)
            dxn = dh * g_ref[...]
            dx_ref[...] = do_ref[...] + r * (dxn - xn * jnp.mean(dxn * xn, axis=-1, keepdims=True))

    row = pl.BlockSpec((tm, D), lambda i, k: (i, 0))
    return pl.pallas_call(
        body, name=name, grid=(S // tm, nk),
        out_shape=[jax.ShapeDtypeStruct((S, D), f32), jax.ShapeDtypeStruct((1, D), f32)],
        in_specs=[pl.BlockSpec((tm, tk), lambda i, k: (i, k)), pl.BlockSpec((tk, D), lambda i, k: (k, 0)), row, row, _full((1, D))],
        out_specs=[row, _full((1, D))],
        scratch_shapes=[pltpu.VMEM((tm, D), f32)],
        compiler_params=_cp(("arbitrary", "arbitrary")))(dproj, wfull, xin, dout, g_pre)


def adam_update(parts, own, me, w, m, v, l, acc, name):
    lw, r, c = w.shape
    tr = _row_tile(r)

    def body(me_ref, p_ref, own_ref, w_ref, m_ref, v_ref, *rest):
        g_ref, d_ref, nm_ref, nv_ref = rest[-4:]
        g = None
        for s in range(NDEV):
            part = jnp.where(me_ref[0] == s, own_ref[0], p_ref[s]).astype(f32)
            g = part if g is None else g + part
        nm = ADAM_B1 * m_ref[0] + (1.0 - ADAM_B1) * g
        nv = ADAM_B2 * v_ref[0] + (1.0 - ADAM_B2) * (g * g)
        m_hat = nm / (1.0 - ADAM_B1 ** ADAM_STEP)
        v_hat = nv / (1.0 - ADAM_B2 ** ADAM_STEP)
        g_ref[0] = g
        d_ref[0] = -ADAM_LR * (m_hat / (jnp.sqrt(v_hat) + ADAM_EPS) + ADAM_WD * w_ref[0])
        nm_ref[0] = nm
        nv_ref[0] = nv

    blk = pl.BlockSpec((1, tr, c), lambda i, me_ref: (l, i, 0))
    sd = jax.ShapeDtypeStruct((lw, r, c), f32)
    extra = [] if acc is None else list(acc)
    return pl.pallas_call(
        body, name=name, out_shape=[sd] * 4,
        grid_spec=pltpu.PrefetchScalarGridSpec(
            num_scalar_prefetch=1, grid=(r // tr,),
            in_specs=[pl.BlockSpec((NDEV, tr, c), lambda i, me_ref: (0, i, 0)),
                      pl.BlockSpec((1, tr, c), lambda i, me_ref: (me_ref[0], i, 0)), blk, blk, blk]
            + [pl.BlockSpec(memory_space=pl.ANY)] * len(extra),
            out_specs=[blk] * 4),
        input_output_aliases={6 + t: t for t in range(len(extra))},
        compiler_params=_cp(("parallel",)))(me, parts, own, w, m, v, *extra)


def _rope_tables():
    t = jnp.arange(S)
    rows, cols = (t // GRID_W).astype(f32), (t % GRID_W).astype(f32)
    nf = HD // 4
    inv = ROPE_THETA ** (-jnp.arange(nf, dtype=f32) / nf)
    ar, ac = rows[:, None] * inv, cols[:, None] * inv
    cos = jnp.concatenate([jnp.cos(ar), jnp.cos(ar), jnp.cos(ac), jnp.cos(ac)], axis=1)
    sin = jnp.concatenate([-jnp.sin(ar), jnp.sin(ar), -jnp.sin(ac), jnp.sin(ac)], axis=1)
    return cos, sin


def _pack_conv(ca, cc):
    z = lambda n: jnp.zeros((L, n, CT), f32)
    return jnp.concatenate([ca, z(5), cc, z(1)], axis=1)


def _pack_small(npre, npost, ccb, lng, lnb, qn, kn):
    wide = lambda a: jnp.pad(a, ((0, 0), (0, D - HD)))
    return jnp.stack([npre, npost, ccb, lng, lnb, wide(qn), wide(kn), jnp.zeros((L, D), f32)], axis=1).reshape(L * 8, D)


def kernel(x, norm_pre, norm_post, w_in, conv_a_w, q_norm, k_norm, conv_c_w, conv_c_b, ln_c_g, ln_c_b, w_out_a, w_out_b, w_out_c, w_o, loss_target, m_norm_pre, m_norm_post, m_w_in, m_conv_a_w, m_q_norm, m_k_norm, m_conv_c_w, m_conv_c_b, m_ln_c_g, m_ln_c_b, m_w_out_a, m_w_out_b, m_w_out_c, m_w_o, v_norm_pre, v_norm_post, v_w_in, v_conv_a_w, v_q_norm, v_k_norm, v_conv_c_w, v_conv_c_b, v_ln_c_g, v_ln_c_b, v_w_out_a, v_w_out_b, v_w_out_c, v_w_o):
    cos, sin = _rope_tables()
    rs = D // NDEV
    stack_sq = lambda a, b, c, d: jnp.stack([a, b, c, d], axis=1)
    wsq32 = stack_sq(w_out_a, w_out_b, w_out_c, w_o)
    conv_pack = _pack_conv(conv_a_w, conv_c_w)
    vec = lambda a, l: a[l][None, :]
    me = (4 * lax.axis_index("x") + 2 * lax.axis_index("y") + lax.axis_index("c")).astype(jnp.int32).reshape(1)

    def gather_start(l, after):
        return split_start(staged[l], _gather_copies, 12, f"ag_start{l}", after=after)

    def forward_start(l, after):
        s_sems, r_sems, bufs, _ = gathers[l]
        bufs = split_wait(s_sems, r_sems, bufs, _gather_copies, after, f"ag_wait{l}")
        fw = split_start(bufs, _forward_copies, 9, f"ag_fwd_start{l}")
        if l + 1 < L:
            gathers[l + 1] = gather_start(l + 1, fw[3])
            return fw, gathers[l + 1][3]
        return fw, fw[3]

    def forward_wait(fw, after, l):
        s_sems, r_sems, bufs, _ = fw
        return split_wait(s_sems, r_sems, bufs, _forward_copies, after, f"ag_fwd_wait{l}")

    wt, m_wt, v_wt = (jnp.swapaxes(a, 1, 2) for a in (w_in, m_w_in, v_w_in))
    xs, saved = x[0], []
    staged = [stage_shards(wt[l], wsq32[l].reshape(4 * rs, D), conv_pack[l], me, f"stage{l}") for l in range(L)]
    gathers = [gather_start(0, None)] + [None] * (L - 1)
    fw, issued = forward_start(0, xs)
    wg, wsq, convw = forward_wait(fw, issued, 0)
    for l in range(L):
        wsq = wsq.reshape(NDEV, 4, rs, D)
        wfull = wg.reshape(P, D)
        proj, h = proj_fwd(xs, vec(norm_pre, l), wfull, f"proj{l}")
        yah = brancha_fwd(proj, convw, f"bra{l}")
        u1 = branchc1_fwd(proj, convw, vec(conv_c_b, l), f"brc1_{l}")
        qh, kh, vh = qkv_fwd(proj, vec(q_norm, l), vec(k_norm, l), cos, sin, f"qkv{l}")
        o, ybh = attn_fwd(qh, kh, vh, proj, f"attn{l}")
        ln_g = vec(ln_c_g, l)
        if l + 1 < L:
            fw, issued = forward_start(l + 1, o)
            ln_g = ln_g + issued[0, 0]
        ych = branchc2_fwd(u1, proj, ln_g, vec(ln_c_b, l), f"brc2_{l}")
        ya, yb, yc, y16, z, xo = merge_fwd(xs, yah, ybh, ych, proj, wsq, vec(norm_post, l), f"merge{l}")
        saved.append(dict(x=xs, wfull=wfull, wsq=wsq, convw=convw, proj=proj, h=h, yah=yah, ybh=ybh, ych=ych, u1=u1,
                          qh=qh, kh=kh, vh=vh, o=o, ya=ya, yb=yb, yc=yc, y16=y16, z=z))
        xs = xo
        if l + 1 < L:
            wg, wsq, convw = forward_wait(fw, xs, l + 1)
    dx, loss_part = loss_fwd(xs, loss_target[0], "loss")
    loss = lax.psum(loss_part[0, 0], ("x", "y", "c"))

    acc = dict(win=None, sq=None, conv=None)
    small_parts = [None] * L
    msq32 = stack_sq(m_w_out_a, m_w_out_b, m_w_out_c, m_w_o)
    vsq32 = stack_sq(v_w_out_a, v_w_out_b, v_w_out_c, v_w_o)
    mconv, vconv = _pack_conv(m_conv_a_w, m_conv_c_w), _pack_conv(v_conv_a_w, v_conv_c_w)

    def scatter_start(parts, name):
        bufs = parts + [lax.empty(p.shape, p.dtype) for p in parts]
        return split_start(bufs, _scatter_copies(len(parts)), 7 * len(parts), name)

    def finish(l, started, after):
        (s1, r1, b1, _), (s2, r2, b2, _) = started
        gsq_own, rsq = split_wait(s1, r1, b1, _scatter_copies(1), after, f"rs_sq_wait{l}")
        gwin_own, gconv_own, rwin, rconv = split_wait(s2, r2, b2, _scatter_copies(2), after, f"rs_win_wait{l}")
        flat = lambda a: a.reshape(a.shape[0], 4 * rs, D)
        acc["win"] = adam_update(rwin, gwin_own, me, wt, m_wt, v_wt, l, acc["win"], f"adam_win{l}")
        acc["sq"] = adam_update(flat(rsq), flat(gsq_own), me, flat(wsq32), flat(msq32), flat(vsq32), l, acc["sq"], f"adam_wsq{l}")
        acc["conv"] = adam_update(rconv, gconv_own, me, conv_pack, mconv, vconv, l, acc["conv"], f"adam_conv{l}")

    pending = [None] * L
    for l in reversed(range(L)):
        sv = saved[l]
        proj = sv["proj"]
        (dyah, dybh, dych, dzb, dyab, dybb, dycb, dgpost, dproj) = merge_bwd(
            dx, sv["z"], sv["ya"], sv["yb"], sv["yc"], proj, sv["wsq"], vec(norm_post, l), f"merge_bwd{l}")
        gsq = [tn_matmul(a, b, f"dwsq{t}_{l}") for t, (a, b) in enumerate(
            ((sv["yah"], dyab), (sv["ybh"], dybb), (sv["ych"], dycb), (sv["y16"], dzb)))]
        gsq_parts = jnp.stack([g.reshape(NDEV, rs, D) for g in gsq], axis=1)
        st1 = scatter_start([gsq_parts], f"rs_sq_start{l}")
        convw = sv["convw"] + st1[3][0, 0]
        gca, dproj = brancha_bwd(dyah, proj, convw, dproj, f"bra_bwd{l}")
        du1, dlg, dlb, dcb, dproj = branchc2_bwd(dych, sv["u1"], proj, vec(ln_c_g, l), vec(ln_c_b, l), dproj, f"brc2_bwd{l}")
        gcc, dproj = branchc1_bwd(du1, proj, convw, dproj, f"brc1_bwd{l}")
        dqh, dkh, dvh, dproj = attn_bwd(dybh, sv["o"], sv["qh"], sv["kh"], sv["vh"], proj, dproj, f"attn_bwd{l}")
        dqn, dkn, dproj = qkv_bwd(dqh, dkh, dvh, proj, vec(q_norm, l), vec(k_norm, l), cos, sin, dproj, f"qkv_bwd{l}")
        gwin = dwin_parts(sv["h"], dproj, f"dwin{l}").reshape(NDEV, PSH, D)
        gconv = jnp.concatenate([gca, gcc], axis=1)
        st2 = scatter_start([gwin, gconv], f"rs_win_start{l}")
        dx, dgpre = dh_bwd(dproj, sv["wfull"], sv["x"], dx, vec(norm_pre, l) + st2[3][0, 0], f"dh{l}")
        wide = lambda a: jnp.pad(a, ((0, 0), (0, D - HD)))
        small_parts[l] = jnp.concatenate([dgpre, dgpost, dcb, dlg, dlb, wide(dqn), wide(dkn), jnp.zeros((1, D), f32)], axis=0)
        pending[l] = (st1, st2)

    for l in reversed(range(1, L)):
        finish(l, pending[l], after=dx)
    (small_all,) = all_gather([jnp.concatenate(small_parts, axis=0)], "ag_small")
    sm = adam_update(small_all, small_all, me,
                     _pack_small(norm_pre, norm_post, conv_c_b, ln_c_g, ln_c_b, q_norm, k_norm)[None],
                     _pack_small(m_norm_pre, m_norm_post, m_conv_c_b, m_ln_c_g, m_ln_c_b, m_q_norm, m_k_norm)[None],
                     _pack_small(v_norm_pre, v_norm_post, v_conv_c_b, v_ln_c_g, v_ln_c_b, v_q_norm, v_k_norm)[None],
                     0, None, "adam_small")
    finish(0, pending[0], after=sm[0])
    sm = [a.reshape(L, 8, D) for a in sm]
    small_rows = dict(norm_pre=(0, D), norm_post=(1, D), conv_c_b=(2, D), ln_c_g=(3, D), ln_c_b=(4, D), q_norm=(5, HD), k_norm=(6, HD))
    sq_rows = dict(w_out_a=0, w_out_b=1, w_out_c=2, w_o=3)

    order = ["norm_pre", "norm_post", "w_in", "conv_a_w", "q_norm", "k_norm", "conv_c_w", "conv_c_b", "ln_c_g", "ln_c_b",
             "w_out_a", "w_out_b", "w_out_c", "w_o"]
    result = [loss, dx[None]]
    for kind in range(4):
        for nme in order:
            if nme in small_rows:
                rw, wd = small_rows[nme]
                result.append(sm[kind][:, rw, :wd])
            elif nme in sq_rows:
                result.append(acc["sq"][kind][:, sq_rows[nme] * rs:(sq_rows[nme] + 1) * rs])
            elif nme == "w_in":
                result.append(jnp.swapaxes(acc["win"][kind], 1, 2))
            elif nme == "conv_a_w":
                result.append(acc["conv"][kind][:, 0:CA_W])
            else:
                result.append(acc["conv"][kind][:, 8:8 + CC_W])
    return tuple(result)
```

```python
import math

import jax
import jax.numpy as jnp
from jax import lax
from jax.experimental import pallas as pl
from jax.experimental.pallas import tpu as pltpu

f32, bf16 = jnp.float32, jnp.bfloat16

D = 1024
S = 2048
L = 4
HD = 128
NQ = D // HD
NKV = NQ // 4
G = NQ // NKV
WKV = NKV * HD
GRID_W = 64
ROPE_THETA = 10000.0
RMS_EPS = 1e-6
LN_EPS = 1e-5
NDEV = 8
CA_W, CC_W = 3, 31
P = 12 * D + 2 * WKV
PSH = P // NDEV
CT = 128
ADAM_LR, ADAM_B1, ADAM_B2, ADAM_EPS, ADAM_WD, ADAM_STEP = 0.001, 0.9, 0.999, 1e-08, 0.01, 10
VMEM_LIMIT = 56 * 1024 * 1024
MESH = pl.DeviceIdType.MESH

_OFF = {}
_o = 0
for _n, _w in (("a_b", D), ("a_c", D), ("a_x", D), ("a_g", D), ("q", D), ("k", WKV), ("v", WKV), ("b_g", D),
               ("c_u", D), ("c_v", D), ("c_g", D), ("m_a", D), ("m_b", D), ("m_c", D)):
    _OFF[_n] = (_o, _w)
    _o += _w
PIECES = tuple(_OFF)


def _cp(sem=None, **kw):
    return pltpu.CompilerParams(dimension_semantics=sem, vmem_limit_bytes=VMEM_LIMIT, **kw)


def _sig(x):
    return 1.0 / (1.0 + jnp.exp(-x))


def _silu(x):
    return x * _sig(x)


def _dsilu(x):
    s = _sig(x)
    return s * (1.0 + x * (1.0 - s))


def _row_specs(name, tm):
    off, w = _OFF[name]
    bw = math.gcd(off, w) if off else w
    return [pl.BlockSpec((tm, bw), (lambda i, *_, b=off // bw + t: (i, b))) for t in range(w // bw)]


def _cat(refs):
    return refs[0][...] if len(refs) == 1 else jnp.concatenate([r[...] for r in refs], axis=1)


def _chan_spec(name):
    off, _ = _OFF[name]
    return pl.BlockSpec((S, CT), lambda j, b=off // CT: (0, b + j))


def _full(shape):
    return pl.BlockSpec(shape, lambda *_: (0,) * len(shape))


def _coords():
    return lax.axis_index("x"), lax.axis_index("y"), lax.axis_index("c")


def all_gather(shards, name):
    n = len(shards)

    def body(*refs):
        ins, outs = refs[:n], refs[n:2 * n]
        send_sems, recv_sems, local_sems = refs[2 * n:]
        x, y, c = _coords()
        me, sibling = (x, y, c), (x, y, 1 - c)
        chips = [(1 - x, y), (x, 1 - y), (1 - x, 1 - y)]

        def slot(a, p):
            return outs[a].at[4 * p[0] + 2 * p[1] + p[2]]

        def copy(a, k, block, to, src=None):
            return pltpu.make_async_remote_copy(
                src_ref=slot(a, block) if src is None else src, dst_ref=slot(a, block),
                send_sem=send_sems.at[7 * a + k], recv_sem=recv_sems.at[7 * a + k], device_id=to, device_id_type=MESH)

        mine = [pltpu.make_async_copy(ins[a], slot(a, me), local_sems.at[a]) for a in range(n)]
        for cp in mine:
            cp.start()
        first = []
        for a in range(n):
            first.append(copy(a, 0, me, sibling, src=ins[a]))
            first += [copy(a, 1 + j, me, (*chip, c), src=ins[a]) for j, chip in enumerate(chips)]
        for cp in first:
            cp.start()
        passed = []
        for j, chip in enumerate(chips):
            for a in range(n):
                copy(a, 1 + j, (*chip, c), me).wait_recv()
                fw = copy(a, 4 + j, (*chip, c), sibling)
                fw.start()
                passed.append(fw)
        for a in range(n):
            copy(a, 0, sibling, me).wait_recv()
            for j, chip in enumerate(chips):
                copy(a, 4 + j, (*chip, 1 - c), me).wait_recv()
        for cp in first + passed:
            cp.wait_send()
        for cp in mine:
            cp.wait()

    anyspec = pl.BlockSpec(memory_space=pl.ANY)
    return pl.pallas_call(
        body, name=name,
        out_shape=[jax.ShapeDtypeStruct((NDEV,) + s.shape, s.dtype) for s in shards],
        in_specs=[anyspec] * n, out_specs=[anyspec] * n,
        scratch_shapes=[pltpu.SemaphoreType.DMA((7 * n,)), pltpu.SemaphoreType.DMA((7 * n,)), pltpu.SemaphoreType.DMA((n,))],
    )(*shards)


_HBM = pl.BlockSpec(memory_space=pltpu.HBM)
_SEM = pl.BlockSpec(memory_space=pltpu.SEMAPHORE)
_EFFECT = pltpu.SideEffectType.DATAFLOW_SIDE_EFFECTING


def split_start(bufs, make_copies, nsem, name, after=None):
    n = len(bufs)
    extra = [] if after is None else [after]

    def body(*refs):
        send_sems, recv_sems = refs[n + len(extra):n + len(extra) + 2]
        for cp in make_copies(refs[:n], send_sems, recv_sems):
            cp.start()
        refs[-1][...] = jnp.zeros((8, 128), f32)

    res = pl.pallas_call(
        body, name=name,
        out_shape=(pltpu.SemaphoreType.DMA((nsem,)), pltpu.SemaphoreType.DMA((nsem,)),
                   *[pltpu.HBM(b.shape, b.dtype) for b in bufs], jax.ShapeDtypeStruct((8, 128), f32)),
        in_specs=[_HBM] * n + [pl.BlockSpec(memory_space=pl.ANY)] * len(extra),
        out_specs=(_SEM, _SEM, *([_HBM] * n), pl.BlockSpec(memory_space=pltpu.VMEM)),
        input_output_aliases={i: 2 + i for i in range(n)},
        compiler_params=pltpu.CompilerParams(has_side_effects=_EFFECT),
    )(*[pltpu.with_memory_space_constraint(b, pltpu.HBM) for b in bufs], *extra)
    return res[0], res[1], list(res[2:2 + n]), res[-1]


def split_wait(send_sems, recv_sems, bufs, make_copies, after, name):
    n = len(bufs)

    def body(*refs):
        for cp in make_copies(refs[:n], refs[n], refs[n + 1]):
            cp.wait_send()
            cp.wait_recv()

    res = pl.pallas_call(
        body, name=name,
        out_shape=tuple(pltpu.HBM(b.shape, b.dtype) for b in bufs),
        in_specs=[_HBM] * n + [_SEM, _SEM, pl.BlockSpec(memory_space=pl.ANY)],
        out_specs=[_HBM] * n,
        input_output_aliases={i: i for i in range(n)},
        compiler_params=pltpu.CompilerParams(has_side_effects=_EFFECT),
    )(*bufs, send_sems, recv_sems, after)
    return list(res)


def _scatter_copies(n):
    def make(refs, send_sems, recv_sems):
        x, y, c = _coords()
        me = 4 * x + 2 * y + c
        copies = []
        for a in range(n):
            for k in range(1, NDEV):
                px = 1 - x if (k >> 2) & 1 else x
                py = 1 - y if (k >> 1) & 1 else y
                pc = 1 - c if k & 1 else c
                copies.append(pltpu.make_async_remote_copy(
                    src_ref=refs[a].at[4 * px + 2 * py + pc], dst_ref=refs[n + a].at[me],
                    send_sem=send_sems.at[7 * a + k - 1], recv_sem=recv_sems.at[7 * a + k - 1],
                    device_id=(px, py, pc), device_id_type=MESH))
        return copies
    return make


def _gather_copies(refs, send_sems, recv_sems):
    x, y, c = _coords()
    me = 4 * x + 2 * y + c
    targets = [(x, y, 1 - c), (1 - x, y, c), (x, 1 - y, c), (1 - x, 1 - y, c)]
    return [pltpu.make_async_remote_copy(
        src_ref=r.at[me], dst_ref=r.at[me], send_sem=send_sems.at[4 * a + k], recv_sem=recv_sems.at[4 * a + k],
        device_id=to, device_id_type=MESH) for a, r in enumerate(refs) for k, to in enumerate(targets)]


def _forward_copies(refs, send_sems, recv_sems):
    x, y, c = _coords()
    chips = [(1 - x, y), (x, 1 - y), (1 - x, 1 - y)]
    return [pltpu.make_async_remote_copy(
        src_ref=r.at[4 * px + 2 * py + c], dst_ref=r.at[4 * px + 2 * py + c], send_sem=send_sems.at[3 * a + j],
        recv_sem=recv_sems.at[3 * a + j], device_id=(x, y, 1 - c), device_id_type=MESH)
        for a, r in enumerate(refs) for j, (px, py) in enumerate(chips)]


def _pair_copies(refs, send_sems, recv_sems):
    x, y, c = _coords()
    parts, land = refs
    return [pltpu.make_async_remote_copy(
        src_ref=parts.at[2 * j + 1 - c], dst_ref=land.at[j], send_sem=send_sems.at[j], recv_sem=recv_sems.at[j],
        device_id=(x, y, 1 - c), device_id_type=MESH) for j in range(NDEV // 2)]


def _chip_copies(refs, send_sems, recv_sems):
    x, y, c = _coords()
    summed, land = refs
    copies = []
    for k in range(1, NDEV // 2):
        px = 1 - x if (k >> 1) & 1 else x
        py = 1 - y if k & 1 else y
        copies.append(pltpu.make_async_remote_copy(
            src_ref=summed.at[2 * px + py], dst_ref=land.at[2 * x + y], send_sem=send_sems.at[k - 1],
            recv_sem=recv_sems.at[k - 1], device_id=(px, py, c), device_id_type=MESH))
    return copies


def pair_sum(parts, land, me, name):
    _, r, c = parts.shape
    tr = _row_tile(r)

    def body(me_ref, a_ref, b_ref, o_ref):
        o_ref[...] = (a_ref[...].astype(f32) + b_ref[...].astype(f32)).astype(bf16)

    blk = pl.BlockSpec((1, tr, c), lambda j, i, m: (j, i, 0))
    return pl.pallas_call(
        body, name=name, out_shape=jax.ShapeDtypeStruct((NDEV // 2, r, c), bf16),
        grid_spec=pltpu.PrefetchScalarGridSpec(
            num_scalar_prefetch=1, grid=(NDEV // 2, r // tr),
            in_specs=[pl.BlockSpec((1, tr, c), lambda j, i, m: (2 * j + m[0] % 2, i, 0)), blk], out_specs=blk),
        compiler_params=_cp(("parallel", "parallel")))(me, parts, land)


def _row_tile(r):
    return r if r <= 256 else max(t for t in (256, 160, 128) if r % t == 0)


def stage_shards(wt_l, wsq_l, conv_l, me, name):
    outs = []
    for a, dt in ((wt_l, bf16), (wsq_l, bf16), (conv_l, f32)):
        r, c = a.shape
        tr = _row_tile(r)

        def body(me_ref, a_ref, o_ref):
            o_ref[0] = a_ref[...].astype(o_ref.dtype)

        outs.append(pl.pallas_call(
            body, name=f"{name}_{len(outs)}", out_shape=jax.ShapeDtypeStruct((NDEV, r, c), dt),
            grid_spec=pltpu.PrefetchScalarGridSpec(
                num_scalar_prefetch=1, grid=(r // tr,),
                in_specs=[pl.BlockSpec((tr, c), lambda i, m: (i, 0))],
                out_specs=pl.BlockSpec((1, tr, c), lambda i, m: (m[0], i, 0))),
            compiler_params=_cp(("arbitrary",)))(me, a))
    return outs


def proj_fwd(xin, g_pre, wt, name):
    tm, tn = min(S, 1024), 1280

    def body(x_ref, g_ref, w_ref, proj_ref, h_ref, hs):
        @pl.when(pl.program_id(1) == 0)
        def _():
            x = x_ref[...]
            r = lax.rsqrt(jnp.mean(x * x, axis=-1, keepdims=True) + RMS_EPS)
            h = (x * r * g_ref[...]).astype(bf16)
            hs[...] = h
            h_ref[...] = h
        proj_ref[...] = lax.dot_general(hs[...], w_ref[...], (((1,), (1,)), ((), ())), preferred_element_type=f32)

    return pl.pallas_call(
        body, name=name, grid=(S // tm, P // tn),
        out_shape=[jax.ShapeDtypeStruct((S, P), f32), jax.ShapeDtypeStruct((S, D), bf16)],
        in_specs=[pl.BlockSpec((tm, D), lambda i, j: (i, 0)), _full((1, D)), pl.BlockSpec((tn, D), lambda i, j: (j, 0))],
        out_specs=[pl.BlockSpec((tm, tn), lambda i, j: (i, j)), pl.BlockSpec((tm, D), lambda i, j: (i, 0))],
        scratch_shapes=[pltpu.VMEM((tm, D), bf16)],
        compiler_params=_cp(("parallel", "arbitrary")))(xin, g_pre, wt)


RC = 128


def _fill_pad(pad, halo, val_fn):
    pad[0:halo, :] = jnp.zeros((halo, CT), f32)
    pad[S + halo:S + 2 * halo, :] = jnp.zeros((halo, CT), f32)

    def step(i, carry):
        rows = pl.ds(pl.multiple_of(i * RC, RC), RC)
        pad[pl.ds(pl.multiple_of(i * RC, RC) + halo, RC), :] = val_fn(rows)
        return carry
    lax.fori_loop(0, S // RC, step, 0)


def brancha_fwd(proj, convw, name):
    def body(ab, ac, ax, ag, w_ref, o_ref, pad):
        _fill_pad(pad, 8, lambda rows: ac[rows, :] * ax[rows, :])
        w = [w_ref[0, k:k + 1, :] for k in range(CA_W)]

        def step(i, carry):
            base = pl.multiple_of(i * RC, RC)
            rows = pl.ds(base, RC)
            t = sum(w[k] * pad[pl.ds(base + 7 + k, RC), :] for k in range(CA_W))
            o_ref[rows, :] = (ab[rows, :] * t * _silu(ag[rows, :])).astype(bf16)
            return carry
        lax.fori_loop(0, S // RC, step, 0)

    return pl.pallas_call(
        body, name=name, grid=(D // CT,), out_shape=jax.ShapeDtypeStruct((S, D), bf16),
        in_specs=[_chan_spec("a_b"), _chan_spec("a_c"), _chan_spec("a_x"), _chan_spec("a_g"),
                  pl.BlockSpec((1, 40, CT), lambda j: (j, 0, 0))],
        out_specs=pl.BlockSpec((S, CT), lambda j: (0, j)),
        scratch_shapes=[pltpu.VMEM((S + 16, CT), f32)],
        compiler_params=_cp(("parallel",)))(proj, proj, proj, proj, convw)


def branchc1_fwd(proj, convw, cbias, name):
    def body(cu, cv, w_ref, b_ref, o_ref, pad):
        _fill_pad(pad, 16, lambda rows: cu[rows, :] * _sig(cv[rows, :]))

        def step(i, carry):
            base = pl.multiple_of(i * RC, RC)
            acc = jnp.zeros((RC, CT), f32) + b_ref[...]
            for k in range(CC_W):
                acc = acc + w_ref[0, 8 + k:9 + k, :] * pad[pl.ds(base + k + 1, RC), :]
            o_ref[pl.ds(base, RC), :] = acc
            return carry
        lax.fori_loop(0, S // RC, step, 0)

    return pl.pallas_call(
        body, name=name, grid=(D // CT,), out_shape=jax.ShapeDtypeStruct((S, D), f32),
        in_specs=[_chan_spec("c_u"), _chan_spec("c_v"), pl.BlockSpec((1, 40, CT), lambda j: (j, 0, 0)),
                  pl.BlockSpec((1, CT), lambda j: (0, j))],
        out_specs=pl.BlockSpec((S, CT), lambda j: (0, j)),
        scratch_shapes=[pltpu.VMEM((S + 32, CT), f32)],
        compiler_params=_cp(("parallel",)))(proj, proj, convw, cbias)


def _swap32(x):
    lane = lax.broadcasted_iota(jnp.int32, x.shape, 1)
    return jnp.where((lane // 32) % 2 == 1, pltpu.roll(x, 32, 1), pltpu.roll(x, HD - 32, 1))


def _rope(y, cos, sin):
    return y * cos + _swap32(y) * sin


def qkv_fwd(proj, qn, kn, cos, sin, name):
    tm = min(S, 256)
    nq, nk, nv = len(_row_specs("q", tm)), len(_row_specs("k", tm)), len(_row_specs("v", tm))

    def body(*refs):
        q = _cat(refs[:nq])
        k = _cat(refs[nq:nq + nk])
        v = _cat(refs[nq + nk:nq + nk + nv])
        qn_ref, kn_ref, cos_ref, sin_ref, qh_ref, kh_ref, vh_ref = refs[nq + nk + nv:]
        cos, sin = cos_ref[...], sin_ref[...]

        def heads(xx, gn, out_ref, n):
            for h in range(n):
                xh = xx[:, h * HD:(h + 1) * HD]
                r = lax.rsqrt(jnp.mean(xh * xh, axis=-1, keepdims=True) + RMS_EPS)
                out_ref[:, h * HD:(h + 1) * HD] = _rope(xh * r * gn, cos, sin).astype(bf16)
        heads(q, qn_ref[...], qh_ref, NQ)
        heads(k, kn_ref[...], kh_ref, NKV)
        vh_ref[...] = v.astype(bf16)

    row = lambda w: pl.BlockSpec((tm, w), lambda i: (i, 0))
    return pl.pallas_call(
        body, name=name, grid=(S // tm,),
        out_shape=[jax.ShapeDtypeStruct((S, D), bf16), jax.ShapeDtypeStruct((S, WKV), bf16), jax.ShapeDtypeStruct((S, WKV), bf16)],
        in_specs=_row_specs("q", tm) + _row_specs("k", tm) + _row_specs("v", tm) + [_full((1, HD)), _full((1, HD)), row(HD), row(HD)],
        out_specs=[row(D), row(WKV), row(WKV)],
        compiler_params=_cp(("parallel",)))(*([proj] * (nq + nk + nv)), qn, kn, cos, sin)


def _softmax_rows(q, k):
    s = lax.dot_general(q, k, (((1,), (1,)), ((), ())), preferred_element_type=f32)
    p = jnp.exp((s - jnp.max(s, axis=-1, keepdims=True)) * (HD ** -0.5))
    return p, 1.0 / jnp.sum(p, axis=-1, keepdims=True)


GW = G * HD


def attn_fwd(qh, kh, vh, proj, name):
    tq = min(S, 256)
    bg_blk = _OFF["b_g"][0] // GW

    def body(q_ref, k_ref, v_ref, bg_ref, o_ref, y_ref):
        k, v = k_ref[...], v_ref[...]
        for g in range(G):
            cols = slice(g * HD, (g + 1) * HD)
            p, rl = _softmax_rows(q_ref[:, cols], k)
            o = jnp.dot(p.astype(bf16), v, preferred_element_type=f32) * rl
            o_ref[:, cols] = o
            y_ref[:, cols] = (o * _silu(bg_ref[:, cols])).astype(bf16)

    grp = pl.BlockSpec((tq, GW), lambda kv, i: (i, kv))
    kvs = pl.BlockSpec((S, HD), lambda kv, i: (0, kv))
    return pl.pallas_call(
        body, name=name, grid=(NKV, S // tq),
        out_shape=[jax.ShapeDtypeStruct((S, D), f32), jax.ShapeDtypeStruct((S, D), bf16)],
        in_specs=[grp, kvs, kvs, pl.BlockSpec((tq, GW), lambda kv, i: (i, bg_blk + kv))],
        out_specs=[grp, grp],
        compiler_params=_cp(("parallel", "parallel")))(qh, kh, vh, proj)


def _ln_parts(u1):
    mu = jnp.mean(u1, axis=-1, keepdims=True)
    xc = u1 - mu
    rstd = lax.rsqrt(jnp.mean(xc * xc, axis=-1, keepdims=True) + LN_EPS)
    return xc * rstd, rstd


def branchc2_fwd(u1, proj, lng, lnb, name):
    tm = min(S, 256)
    ncg = len(_row_specs("c_g", tm))

    def body(*refs):
        u_ref = refs[0]
        cg = _cat(refs[1:1 + ncg])
        g_ref, b_ref, o_ref = refs[1 + ncg:]
        xh, _ = _ln_parts(u_ref[...])
        o_ref[...] = (_silu(xh * g_ref[...] + b_ref[...]) * _silu(cg)).astype(bf16)

    row = pl.BlockSpec((tm, D), lambda i: (i, 0))
    return pl.pallas_call(
        body, name=name, grid=(S // tm,), out_shape=jax.ShapeDtypeStruct((S, D), bf16),
        in_specs=[row] + _row_specs("c_g", tm) + [_full((1, D)), _full((1, D))], out_specs=row,
        compiler_params=_cp(("parallel",)))(u1, *([proj] * ncg), lng, lnb)


def _wmat(w_ref, kind):
    return w_ref[:, kind].reshape(D, D)


def merge_fwd(xin, yah, ybh, ych, proj, wsq, g_post, name):
    tm = min(S, 256)
    nm = len(_row_specs("m_a", tm))

    def body(*refs):
        x_ref, a_ref, b_ref, c_ref = refs[:4]
        ms = [_cat(refs[4 + t * nm:4 + (t + 1) * nm]) for t in range(3)]
        w_ref, g_ref, ya_ref, yb_ref, yc_ref, y_ref, z_ref, o_ref = refs[4 + 3 * nm:]
        y = jnp.zeros((tm, D), f32)
        for t, (h_ref, out_ref) in enumerate(((a_ref, ya_ref), (b_ref, yb_ref), (c_ref, yc_ref))):
            yt = jnp.dot(h_ref[...], _wmat(w_ref, t), preferred_element_type=f32)
            out_ref[...] = yt
            y = y + _sig(ms[t]) * yt
        yb16 = y.astype(bf16)
        y_ref[...] = yb16
        z = jnp.dot(yb16, _wmat(w_ref, 3), preferred_element_type=f32)
        z_ref[...] = z
        r = lax.rsqrt(jnp.mean(z * z, axis=-1, keepdims=True) + RMS_EPS)
        o_ref[...] = x_ref[...] + z * r * g_ref[...]

    row = pl.BlockSpec((tm, D), lambda i: (i, 0))
    sd = lambda dt: jax.ShapeDtypeStruct((S, D), dt)
    return pl.pallas_call(
        body, name=name, grid=(S // tm,),
        out_shape=[sd(f32), sd(f32), sd(f32), sd(bf16), sd(f32), sd(f32)],
        in_specs=[row] * 4 + _row_specs("m_a", tm) + _row_specs("m_b", tm) + _row_specs("m_c", tm)
        + [_full((NDEV, 4, D // NDEV, D)), _full((1, D))],
        out_specs=[row] * 6,
        compiler_params=_cp(("parallel",)))(xin, yah, ybh, ych, *([proj] * (3 * nm)), wsq, g_post)


def loss_fwd(y, target, name):
    tm = min(S, 256)

    def body(y_ref, t_ref, dy_ref, l_ref):
        e = y_ref[...] - t_ref[...]
        dy_ref[...] = e / D

        @pl.when(pl.program_id(0) == 0)
        def _():
            l_ref[...] = jnp.zeros((1, 128), f32)
        l_ref[...] += (0.5 / D) * jnp.sum(e * e)

    row = pl.BlockSpec((tm, D), lambda i: (i, 0))
    return pl.pallas_call(
        body, name=name, grid=(S // tm,),
        out_shape=[jax.ShapeDtypeStruct((S, D), f32), jax.ShapeDtypeStruct((1, 128), f32)],
        in_specs=[row, row], out_specs=[row, _full((1, 128))],
        compiler_params=_cp(("arbitrary",)))(y, target)


def _acc(ref, val):
    @pl.when(pl.program_id(0) == 0)
    def _():
        ref[...] = jnp.zeros(ref.shape, f32)
    ref[...] += val


def _emit_copies(stash, dst, sems, windows):
    return [pltpu.make_async_copy(stash.at[p], dst.at[w], sems.at[p]) for p, w in enumerate(windows)]


def _emit_drain_previous(copies, step):
    @pl.when(step > 0)
    def _():
        for cp in copies:
            cp.wait()


def _emit_start(copies, step, nsteps):
    for cp in copies:
        cp.start()

    @pl.when(step == nsteps - 1)
    def _():
        for cp in copies:
            cp.wait()


def merge_bwd(dout, z, ya, yb, yc, proj, wsq, g_post, name):
    tm = min(S, 256)
    nm = len(_row_specs("m_a", tm))
    nsteps = S // tm

    def body(*refs):
        do_ref, z_ref, ya_ref, yb_ref, yc_ref = refs[:5]
        ms = [_cat(refs[5 + t * nm:5 + (t + 1) * nm]) for t in range(3)]
        w_ref, g_ref = refs[5 + 3 * nm:7 + 3 * nm]
        dh_refs = refs[7 + 3 * nm:10 + 3 * nm]
        dzb_ref = refs[10 + 3 * nm]
        dyb_refs = refs[11 + 3 * nm:14 + 3 * nm]
        dg_ref = refs[14 + 3 * nm]
        dproj_ref, stash, sems = refs[15 + 3 * nm:]
        i = pl.program_id(0)
        rows = pl.ds(pl.multiple_of(i * tm, tm), tm)
        copies = _emit_copies(stash, dproj_ref, sems, [(rows, pl.ds(_OFF[n][0], D)) for n in ("m_a", "m_b", "m_c")])
        nt = (((1,), (1,)), ((), ()))
        z, dout = z_ref[...], do_ref[...]
        r = lax.rsqrt(jnp.mean(z * z, axis=-1, keepdims=True) + RMS_EPS)
        zh = z * r
        _acc(dg_ref, jnp.sum(dout * zh, axis=0, keepdims=True))
        dzh = dout * g_ref[...]
        dz = (r * (dzh - zh * jnp.mean(dzh * zh, axis=-1, keepdims=True))).astype(bf16)
        dzb_ref[...] = dz
        dy = lax.dot_general(dz, _wmat(w_ref, 3), nt, preferred_element_type=f32)
        dms = []
        for t, yt_ref in enumerate((ya_ref, yb_ref, yc_ref)):
            sg = _sig(ms[t])
            dyt = (dy * sg).astype(bf16)
            dyb_refs[t][...] = dyt
            dms.append((dy * yt_ref[...] * sg * (1.0 - sg)).astype(bf16))
            dh_refs[t][...] = lax.dot_general(dyt, _wmat(w_ref, t), nt, preferred_element_type=f32)
        _emit_drain_previous(copies, i)
        for t in range(3):
            stash[t] = dms[t]
        _emit_start(copies, i, nsteps)

    row = pl.BlockSpec((tm, D), lambda i: (i, 0))
    sd = lambda dt: jax.ShapeDtypeStruct((S, D), dt)
    return pl.pallas_call(
        body, name=name, grid=(nsteps,),
        out_shape=[sd(f32)] * 3 + [sd(bf16)] * 4 + [jax.ShapeDtypeStruct((1, D), f32), jax.ShapeDtypeStruct((S, P), bf16)],
        in_specs=[row] * 5 + _row_specs("m_a", tm) + _row_specs("m_b", tm) + _row_specs("m_c", tm)
        + [_full((NDEV, 4, D // NDEV, D)), _full((1, D))],
        out_specs=[row] * 7 + [_full((1, D)), pl.BlockSpec(memory_space=pl.ANY)],
        scratch_shapes=[pltpu.VMEM((3, tm, D), bf16), pltpu.SemaphoreType.DMA((3,))],
        compiler_params=_cp(("arbitrary",)))(dout, z, ya, yb, yc, *([proj] * (3 * nm)), wsq, g_post)


def tn_matmul(a, b, name):
    m, n = a.shape[1], b.shape[1]
    tmm = min(m, 512)

    def body(a_ref, b_ref, o_ref):
        o_ref[...] = lax.dot_general(a_ref[...], b_ref[...], (((0,), (0,)), ((), ())), preferred_element_type=f32).astype(bf16)

    return pl.pallas_call(
        body, name=name, grid=(m // tmm,), out_shape=jax.ShapeDtypeStruct((m, n), bf16),
        in_specs=[pl.BlockSpec((S, tmm), lambda i: (0, i)), _full((S, n))],
        out_specs=pl.BlockSpec((tmm, n), lambda i: (i, 0)),
        compiler_params=_cp(("parallel",)))(a, b)


def dwin_parts(h, dproj, name):
    tn = 640

    def body(d_ref, h_ref, o_ref):
        o_ref[...] = lax.dot_general(d_ref[...], h_ref[...], (((0,), (0,)), ((), ())), preferred_element_type=f32).astype(bf16)

    return pl.pallas_call(
        body, name=name, grid=(P // tn,), out_shape=jax.ShapeDtypeStruct((P, D), bf16),
        in_specs=[pl.BlockSpec((S, tn), lambda j: (0, j)), _full((S, D))],
        out_specs=pl.BlockSpec((tn, D), lambda j: (j, 0)),
        compiler_params=_cp(("parallel",)))(dproj, h)


def _chan_windows(names, j):
    return [(slice(None), pl.ds(pl.multiple_of(_OFF[n][0] + j * CT, CT), CT)) for n in names]


def brancha_bwd(dyah, proj, convw, dproj, name):
    nsteps = D // CT

    def body(d_ref, ab, ac, ax, ag, w_ref, _, dw_ref, dproj_ref, padp, padt, accw, stash, sems):
        j = pl.program_id(0)
        copies = _emit_copies(stash, dproj_ref, sems, _chan_windows(("a_b", "a_c", "a_x", "a_g"), j))
        _fill_pad(padp, 8, lambda rows: ac[rows, :] * ax[rows, :])
        _fill_pad(padt, 8, lambda rows: d_ref[rows, :] * ab[rows, :] * _silu(ag[rows, :]))
        accw[...] = jnp.zeros(accw.shape, f32)
        w = [w_ref[0, k:k + 1, :] for k in range(CA_W)]
        _emit_drain_previous(copies, j)

        def step(i, carry):
            base = pl.multiple_of(i * RC, RC)
            rows = pl.ds(base, RC)
            ps = [padp[pl.ds(base + 7 + k, RC), :] for k in range(CA_W)]
            t = sum(w[k] * ps[k] for k in range(CA_W))
            dp = sum(w[k] * padt[pl.ds(base + 9 - k, RC), :] for k in range(CA_W))
            d, a_b, a_g = d_ref[rows, :], ab[rows, :], ag[rows, :]
            stash[0, rows, :] = (d * t * _silu(a_g)).astype(bf16)
            stash[1, rows, :] = (dp * ax[rows, :]).astype(bf16)
            stash[2, rows, :] = (dp * ac[rows, :]).astype(bf16)
            stash[3, rows, :] = (d * a_b * t * _dsilu(a_g)).astype(bf16)
            dt = padt[pl.ds(base + 8, RC), :]
            for k in range(CA_W):
                accw[8 * k:8 * k + 8, :] += jnp.sum((dt * ps[k]).reshape(RC // 8, 8, CT), axis=0)
            return carry
        lax.fori_loop(0, S // RC, step, 0)
        _emit_start(copies, j, nsteps)
        dw_ref[0] = jnp.zeros((8, CT), f32)
        for k in range(CA_W):
            dw_ref[0, k:k + 1, :] = jnp.sum(accw[8 * k:8 * k + 8, :], axis=0, keepdims=True)

    tile = pl.BlockSpec((S, CT), lambda j: (0, j))
    anyspec = pl.BlockSpec(memory_space=pl.ANY)
    return pl.pallas_call(
        body, name=name, grid=(nsteps,),
        out_shape=[jax.ShapeDtypeStruct((NDEV, 8, CT), f32), jax.ShapeDtypeStruct((S, P), bf16)],
        in_specs=[tile, _chan_spec("a_b"), _chan_spec("a_c"), _chan_spec("a_x"), _chan_spec("a_g"),
                  pl.BlockSpec((1, 40, CT), lambda j: (j, 0, 0)), anyspec],
        out_specs=[pl.BlockSpec((1, 8, CT), lambda j: (j, 0, 0)), anyspec],
        input_output_aliases={6: 1},
        scratch_shapes=[pltpu.VMEM((S + 16, CT), f32), pltpu.VMEM((S + 16, CT), f32), pltpu.VMEM((8 * CA_W, CT), f32),
                        pltpu.VMEM((4, S, CT), bf16), pltpu.SemaphoreType.DMA((4,))],
        compiler_params=_cp(("arbitrary",)))(dyah, proj, proj, proj, proj, convw, dproj)


def branchc2_bwd(dych, u1, proj, lng, lnb, dproj, name):
    tm = min(S, 256)
    ncg = len(_row_specs("c_g", tm))
    nsteps = S // tm

    def body(*refs):
        d_ref, u_ref = refs[:2]
        cg = _cat(refs[2:2 + ncg])
        g_ref, b_ref, _, du_ref, dlg_ref, dlb_ref, dcb_ref, dproj_ref, stash, sems = refs[2 + ncg:]
        i = pl.program_id(0)
        copies = _emit_copies(stash, dproj_ref, sems, [(pl.ds(pl.multiple_of(i * tm, tm), tm), pl.ds(_OFF["c_g"][0], D))])
        d = d_ref[...]
        xh, rstd = _ln_parts(u_ref[...])
        ln = xh * g_ref[...] + b_ref[...]
        _emit_drain_previous(copies, i)
        stash[0] = (d * _silu(ln) * _dsilu(cg)).astype(bf16)
        _emit_start(copies, i, nsteps)
        dln = d * _silu(cg) * _dsilu(ln)
        _acc(dlg_ref, jnp.sum(dln * xh, axis=0, keepdims=True))
        _acc(dlb_ref, jnp.sum(dln, axis=0, keepdims=True))
        dxh = dln * g_ref[...]
        du = rstd * (dxh - jnp.mean(dxh, axis=-1, keepdims=True) - xh * jnp.mean(dxh * xh, axis=-1, keepdims=True))
        du_ref[...] = du
        _acc(dcb_ref, jnp.sum(du, axis=0, keepdims=True))

    row = pl.BlockSpec((tm, D), lambda i: (i, 0))
    vec = jax.ShapeDtypeStruct((1, D), f32)
    anyspec = pl.BlockSpec(memory_space=pl.ANY)
    return pl.pallas_call(
        body, name=name, grid=(nsteps,),
        out_shape=[jax.ShapeDtypeStruct((S, D), f32), vec, vec, vec, jax.ShapeDtypeStruct((S, P), bf16)],
        in_specs=[row, row] + _row_specs("c_g", tm) + [_full((1, D)), _full((1, D)), anyspec],
        out_specs=[row, _full((1, D)), _full((1, D)), _full((1, D)), anyspec],
        input_output_aliases={4 + ncg: 4},
        scratch_shapes=[pltpu.VMEM((1, tm, D), bf16), pltpu.SemaphoreType.DMA((1,))],
        compiler_params=_cp(("arbitrary",)))(dych, u1, *([proj] * ncg), lng, lnb, dproj)


def branchc1_bwd(du1, proj, convw, dproj, name):
    nsteps = D // CT

    def body(d_ref, cu, cv, w_ref, _, dw_ref, dproj_ref, padu, padd, accw, stash, sems):
        j = pl.program_id(0)
        copies = _emit_copies(stash, dproj_ref, sems, _chan_windows(("c_u", "c_v"), j))
        _fill_pad(padu, 16, lambda rows: cu[rows, :] * _sig(cv[rows, :]))
        _fill_pad(padd, 16, lambda rows: d_ref[rows, :])
        accw[...] = jnp.zeros(accw.shape, f32)
        _emit_drain_previous(copies, j)

        def step(i, carry):
            base = pl.multiple_of(i * RC, RC)
            rows = pl.ds(base, RC)
            d = d_ref[rows, :]
            du0 = jnp.zeros((RC, CT), f32)
            for k in range(CC_W):
                du0 = du0 + w_ref[0, 8 + k:9 + k, :] * padd[pl.ds(base + 31 - k, RC), :]
                accw[8 * k:8 * k + 8, :] += jnp.sum((d * padu[pl.ds(base + k + 1, RC), :]).reshape(RC // 8, 8, CT), axis=0)
            sg = _sig(cv[rows, :])
            stash[0, rows, :] = (du0 * sg).astype(bf16)
            stash[1, rows, :] = (du0 * cu[rows, :] * sg * (1.0 - sg)).astype(bf16)
            return carry
        lax.fori_loop(0, S // RC, step, 0)
        _emit_start(copies, j, nsteps)
        dw_ref[0] = jnp.zeros((32, CT), f32)
        for k in range(CC_W):
            dw_ref[0, k:k + 1, :] = jnp.sum(accw[8 * k:8 * k + 8, :], axis=0, keepdims=True)

    tile = pl.BlockSpec((S, CT), lambda j: (0, j))
    anyspec = pl.BlockSpec(memory_space=pl.ANY)
    return pl.pallas_call(
        body, name=name, grid=(nsteps,),
        out_shape=[jax.ShapeDtypeStruct((NDEV, 32, CT), f32), jax.ShapeDtypeStruct((S, P), bf16)],
        in_specs=[tile, _chan_spec("c_u"), _chan_spec("c_v"), pl.BlockSpec((1, 40, CT), lambda j: (j, 0, 0)), anyspec],
        out_specs=[pl.BlockSpec((1, 32, CT), lambda j: (j, 0, 0)), anyspec],
        input_output_aliases={4: 1},
        scratch_shapes=[pltpu.VMEM((S + 32, CT), f32), pltpu.VMEM((S + 32, CT), f32), pltpu.VMEM((8 * 32, CT), f32),
                        pltpu.VMEM((2, S, CT), bf16), pltpu.SemaphoreType.DMA((2,))],
        compiler_params=_cp(("arbitrary",)))(du1, proj, proj, convw, dproj)


def attn_bwd(dybh, o, qh, kh, vh, proj, dproj, name):
    tq = min(S, 256)
    bg_blk = _OFF["b_g"][0] // GW

    def body(d_ref, o_ref, q_ref, k_ref, v_ref, bg_ref, _, dq_ref, dk_ref, dv_ref, dbg_ref):
        @pl.when(pl.program_id(1) == 0)
        def _():
            dk_ref[...] = jnp.zeros(dk_ref.shape, f32)
            dv_ref[...] = jnp.zeros(dv_ref.shape, f32)
        k, v = k_ref[...], v_ref[...]
        tn = (((0,), (0,)), ((), ()))
        dk_acc = jnp.zeros((S, HD), f32)
        dv_acc = jnp.zeros((S, HD), f32)
        for g in range(G):
            cols = slice(g * HD, (g + 1) * HD)
            d, bg, q, o = d_ref[:, cols], bg_ref[:, cols], q_ref[:, cols], o_ref[:, cols]
            dbg_ref[:, cols] = (d * o * _dsilu(bg)).astype(bf16)
            do = d * _silu(bg)
            p, rl = _softmax_rows(q, k)
            dv_acc = dv_acc + lax.dot_general(p.astype(bf16), (do * rl).astype(bf16), tn, preferred_element_type=f32)
            dp = lax.dot_general(do.astype(bf16), v, (((1,), (1,)), ((), ())), preferred_element_type=f32)
            delta = jnp.sum(do * o, axis=-1, keepdims=True)
            ds = (p * (dp - delta)).astype(bf16)
            rs_ = rl * (HD ** -0.5)
            dq_ref[:, cols] = jnp.dot(ds, k, preferred_element_type=f32) * rs_
            dk_acc = dk_acc + lax.dot_general(ds, (q.astype(f32) * rs_).astype(bf16), tn, preferred_element_type=f32)
        dk_ref[...] += dk_acc
        dv_ref[...] += dv_acc

    grp = pl.BlockSpec((tq, GW), lambda kv, i: (i, kv))
    kvs = pl.BlockSpec((S, HD), lambda kv, i: (0, kv))
    return pl.pallas_call(
        body, name=name, grid=(NKV, S // tq),
        out_shape=[jax.ShapeDtypeStruct((S, D), f32), jax.ShapeDtypeStruct((S, WKV), f32),
                   jax.ShapeDtypeStruct((S, WKV), f32), jax.ShapeDtypeStruct((S, P), bf16)],
        in_specs=[grp, grp, grp, kvs, kvs, pl.BlockSpec((tq, GW), lambda kv, i: (i, bg_blk + kv)),
                  pl.BlockSpec(memory_space=pl.ANY)],
        out_specs=[grp, kvs, kvs, pl.BlockSpec((tq, GW), lambda kv, i: (i, bg_blk + kv))],
        input_output_aliases={6: 3},
        compiler_params=_cp(("parallel", "arbitrary")))(dybh, o, qh, kh, vh, proj, dproj)


def qkv_bwd(dqh, dkh, dvh, proj, qn, kn, cos, sin, dproj, name):
    tm = min(S, 256)
    nq, nk = len(_row_specs("q", tm)), len(_row_specs("k", tm))
    nsteps = S // tm
    wq = D + 2 * WKV

    def body(*refs):
        dqh_ref, dkh_ref, dvh_ref = refs[:3]
        q = _cat(refs[3:3 + nq])
        k = _cat(refs[3 + nq:3 + nq + nk])
        qn_ref, kn_ref, cos_ref, sin_ref, _, dqn_ref, dkn_ref, dproj_ref, stash, sems = refs[3 + nq + nk:]
        i = pl.program_id(0)
        copies = _emit_copies(stash, dproj_ref, sems, [(pl.ds(pl.multiple_of(i * tm, tm), tm), pl.ds(_OFF["q"][0], wq))])
        cos, sin = cos_ref[...], sin_ref[...]
        _emit_drain_previous(copies, i)

        def heads(xx, dd, gn, col0, dgn_ref, n):
            dg = jnp.zeros((1, HD), f32)
            for h in range(n):
                xh = xx[:, h * HD:(h + 1) * HD]
                dh = dd[:, h * HD:(h + 1) * HD]
                r = lax.rsqrt(jnp.mean(xh * xh, axis=-1, keepdims=True) + RMS_EPS)
                xn = xh * r
                dy = dh * cos + _swap32(dh * sin)
                dg = dg + jnp.sum(dy * xn, axis=0, keepdims=True)
                dxn = dy * gn
                stash[0, :, col0 + h * HD:col0 + (h + 1) * HD] = (
                    r * (dxn - xn * jnp.mean(dxn * xn, axis=-1, keepdims=True))).astype(bf16)
            _acc(dgn_ref, dg)
        heads(q, dqh_ref[...], qn_ref[...], 0, dqn_ref, NQ)
        heads(k, dkh_ref[...], kn_ref[...], D, dkn_ref, NKV)
        stash[0, :, D + WKV:wq] = dvh_ref[...].astype(bf16)
        _emit_start(copies, i, nsteps)

    row = lambda w: pl.BlockSpec((tm, w), lambda i: (i, 0))
    vec = jax.ShapeDtypeStruct((1, HD), f32)
    anyspec = pl.BlockSpec(memory_space=pl.ANY)
    return pl.pallas_call(
        body, name=name, grid=(nsteps,),
        out_shape=[vec, vec, jax.ShapeDtypeStruct((S, P), bf16)],
        in_specs=[row(D), row(WKV), row(WKV)] + _row_specs("q", tm) + _row_specs("k", tm)
        + [_full((1, HD)), _full((1, HD)), row(HD), row(HD), anyspec],
        out_specs=[_full((1, HD)), _full((1, HD)), anyspec],
        input_output_aliases={7 + nq + nk: 2},
        scratch_shapes=[pltpu.VMEM((1, tm, wq), bf16), pltpu.SemaphoreType.DMA((1,))],
        compiler_params=_cp(("arbitrary",)))(dqh, dkh, dvh, *([proj] * (nq + nk)), qn, kn, cos, sin, dproj)


def dh_bwd(dproj, wfull, xin, dout, g_pre, name):
    tm, tk = min(S, 1024), 1280
    nk = P // tk

    def body(d_ref, w_ref, x_ref, do_ref, g_ref, dx_ref, dg_ref, acc):
        kk = pl.program_id(1)

        @pl.when(kk == 0)
        def _():
            acc[...] = jnp.zeros(acc.shape, f32)
        acc[...] += jnp.dot(d_ref[...], w_ref[...], preferred_element_type=f32)

        @pl.when((kk == 0) & (pl.program_id(0) == 0))
        def _():
            dg_ref[...] = jnp.zeros(dg_ref.shape, f32)

        @pl.when(kk == nk - 1)
        def _():
            x, dh = x_ref[...], acc[...]
            r = lax.rsqrt(jnp.mean(x * x, axis=-1, keepdims=True) + RMS_EPS)
            xn = x * r
            dg_ref[...] += jnp.sum(dh * xn, axis=0, keepdims=True)
            dxn = dh * g_ref[...]
            dx_ref[...] = do_ref[...] + r * (dxn - xn * jnp.mean(dxn * xn, axis=-1, keepdims=True))

    row = pl.BlockSpec((tm, D), lambda i, k: (i, 0))
    return pl.pallas_call(
        body, name=name, grid=(S // tm, nk),
        out_shape=[jax.ShapeDtypeStruct((S, D), f32), jax.ShapeDtypeStruct((1, D), f32)],
        in_specs=[pl.BlockSpec((tm, tk), lambda i, k: (i, k)), pl.BlockSpec((tk, D), lambda i, k: (k, 0)), row, row, _full((1, D))],
        out_specs=[row, _full((1, D))],
        scratch_shapes=[pltpu.VMEM((tm, D), f32)],
        compiler_params=_cp(("arbitrary", "arbitrary")))(dproj, wfull, xin, dout, g_pre)


def adam_update(parts, own, me, w, m, v, l, acc, name):
    lw, r, c = w.shape
    tr = _row_tile(r)
    nslots = parts.shape[0]

    def body(me_ref, p_ref, own_ref, w_ref, m_ref, v_ref, *rest):
        g_ref, d_ref, nm_ref, nv_ref = rest[-4:]
        g = None
        for s in range(nslots):
            part = jnp.where(me_ref[0] == s, own_ref[0], p_ref[s]).astype(f32)
            g = part if g is None else g + part
        nm = ADAM_B1 * m_ref[0] + (1.0 - ADAM_B1) * g
        nv = ADAM_B2 * v_ref[0] + (1.0 - ADAM_B2) * (g * g)
        m_hat = nm / (1.0 - ADAM_B1 ** ADAM_STEP)
        v_hat = nv / (1.0 - ADAM_B2 ** ADAM_STEP)
        g_ref[0] = g
        d_ref[0] = -ADAM_LR * (m_hat / (jnp.sqrt(v_hat) + ADAM_EPS) + ADAM_WD * w_ref[0])
        nm_ref[0] = nm
        nv_ref[0] = nv

    blk = pl.BlockSpec((1, tr, c), lambda i, me_ref: (l, i, 0))
    sd = jax.ShapeDtypeStruct((lw, r, c), f32)
    extra = [] if acc is None else list(acc)
    return pl.pallas_call(
        body, name=name, out_shape=[sd] * 4,
        grid_spec=pltpu.PrefetchScalarGridSpec(
            num_scalar_prefetch=1, grid=(r // tr,),
            in_specs=[pl.BlockSpec((nslots, tr, c), lambda i, me_ref: (0, i, 0)),
                      pl.BlockSpec((1, tr, c), lambda i, me_ref: (me_ref[0], i, 0)), blk, blk, blk]
            + [pl.BlockSpec(memory_space=pl.ANY)] * len(extra),
            out_specs=[blk] * 4),
        input_output_aliases={6 + t: t for t in range(len(extra))},
        compiler_params=_cp(("parallel",)))(me, parts, own, w, m, v, *extra)


def _rope_tables():
    t = jnp.arange(S)
    rows, cols = (t // GRID_W).astype(f32), (t % GRID_W).astype(f32)
    nf = HD // 4
    inv = ROPE_THETA ** (-jnp.arange(nf, dtype=f32) / nf)
    ar, ac = rows[:, None] * inv, cols[:, None] * inv
    cos = jnp.concatenate([jnp.cos(ar), jnp.cos(ar), jnp.cos(ac), jnp.cos(ac)], axis=1)
    sin = jnp.concatenate([-jnp.sin(ar), jnp.sin(ar), -jnp.sin(ac), jnp.sin(ac)], axis=1)
    return cos, sin


def _pack_conv(ca, cc):
    z = lambda n: jnp.zeros((L, n, CT), f32)
    return jnp.concatenate([ca, z(5), cc, z(1)], axis=1)


def _pack_small(npre, npost, ccb, lng, lnb, qn, kn):
    wide = lambda a: jnp.pad(a, ((0, 0), (0, D - HD)))
    return jnp.stack([npre, npost, ccb, lng, lnb, wide(qn), wide(kn), jnp.zeros((L, D), f32)], axis=1).reshape(L * 8, D)


def kernel(x, norm_pre, norm_post, w_in, conv_a_w, q_norm, k_norm, conv_c_w, conv_c_b, ln_c_g, ln_c_b, w_out_a, w_out_b, w_out_c, w_o, loss_target, m_norm_pre, m_norm_post, m_w_in, m_conv_a_w, m_q_norm, m_k_norm, m_conv_c_w, m_conv_c_b, m_ln_c_g, m_ln_c_b, m_w_out_a, m_w_out_b, m_w_out_c, m_w_o, v_norm_pre, v_norm_post, v_w_in, v_conv_a_w, v_q_norm, v_k_norm, v_conv_c_w, v_conv_c_b, v_ln_c_g, v_ln_c_b, v_w_out_a, v_w_out_b, v_w_out_c, v_w_o):
    cos, sin = _rope_tables()
    rs = D // NDEV
    stack_sq = lambda a, b, c, d: jnp.stack([a, b, c, d], axis=1)
    wsq32 = stack_sq(w_out_a, w_out_b, w_out_c, w_o)
    conv_pack = _pack_conv(conv_a_w, conv_c_w)
    vec = lambda a, l: a[l][None, :]
    me = (4 * lax.axis_index("x") + 2 * lax.axis_index("y") + lax.axis_index("c")).astype(jnp.int32).reshape(1)

    def gather_start(l, after):
        return split_start(staged[l], _gather_copies, 12, f"ag_start{l}", after=after)

    def forward_start(l, after):
        s_sems, r_sems, bufs, _ = gathers[l]
        bufs = split_wait(s_sems, r_sems, bufs, _gather_copies, after, f"ag_wait{l}")
        fw = split_start(bufs, _forward_copies, 9, f"ag_fwd_start{l}")
        if l + 1 < L:
            gathers[l + 1] = gather_start(l + 1, fw[3])
            return fw, gathers[l + 1][3]
        return fw, fw[3]

    def forward_wait(fw, after, l):
        s_sems, r_sems, bufs, _ = fw
        return split_wait(s_sems, r_sems, bufs, _forward_copies, after, f"ag_fwd_wait{l}")

    wt, m_wt, v_wt = (jnp.swapaxes(a, 1, 2) for a in (w_in, m_w_in, v_w_in))
    xs, saved = x[0], []
    staged = [stage_shards(wt[l], wsq32[l].reshape(4 * rs, D), conv_pack[l], me, f"stage{l}") for l in range(L)]
    gathers = [gather_start(0, None)] + [None] * (L - 1)
    fw, issued = forward_start(0, xs)
    wg, wsq, convw = forward_wait(fw, issued, 0)
    for l in range(L):
        wsq = wsq.reshape(NDEV, 4, rs, D)
        wfull = wg.reshape(P, D)
        proj, h = proj_fwd(xs, vec(norm_pre, l), wfull, f"proj{l}")
        yah = brancha_fwd(proj, convw, f"bra{l}")
        u1 = branchc1_fwd(proj, convw, vec(conv_c_b, l), f"brc1_{l}")
        qh, kh, vh = qkv_fwd(proj, vec(q_norm, l), vec(k_norm, l), cos, sin, f"qkv{l}")
        o, ybh = attn_fwd(qh, kh, vh, proj, f"attn{l}")
        ln_g = vec(ln_c_g, l)
        if l + 1 < L:
            fw, issued = forward_start(l + 1, o)
            ln_g = ln_g + issued[0, 0]
        ych = branchc2_fwd(u1, proj, ln_g, vec(ln_c_b, l), f"brc2_{l}")
        ya, yb, yc, y16, z, xo = merge_fwd(xs, yah, ybh, ych, proj, wsq, vec(norm_post, l), f"merge{l}")
        saved.append(dict(x=xs, wfull=wfull, wsq=wsq, convw=convw, proj=proj, h=h, yah=yah, ybh=ybh, ych=ych, u1=u1,
                          qh=qh, kh=kh, vh=vh, o=o, ya=ya, yb=yb, yc=yc, y16=y16, z=z))
        xs = xo
        if l + 1 < L:
            wg, wsq, convw = forward_wait(fw, xs, l + 1)
    dx, loss_part = loss_fwd(xs, loss_target[0], "loss")
    loss = lax.psum(loss_part[0, 0], ("x", "y", "c"))

    acc = dict(win=None, sq=None, conv=None)
    small_parts = [None] * L
    msq32 = stack_sq(m_w_out_a, m_w_out_b, m_w_out_c, m_w_o)
    vsq32 = stack_sq(v_w_out_a, v_w_out_b, v_w_out_c, v_w_o)
    mconv, vconv = _pack_conv(m_conv_a_w, m_conv_c_w), _pack_conv(v_conv_a_w, v_conv_c_w)

    def scatter_start(parts, name):
        bufs = parts + [lax.empty(p.shape, p.dtype) for p in parts]
        return split_start(bufs, _scatter_copies(len(parts)), 7 * len(parts), name)

    def finish(l, started, after):
        (s1, r1, b1, _), (s2, r2, b2, _) = started
        gsq_own, rsq = split_wait(s1, r1, b1, _scatter_copies(1), after, f"rs_sq_wait{l}")
        gwin_own, gconv_own, rwin, rconv = split_wait(s2, r2, b2, _scatter_copies(2), after, f"rs_win_wait{l}")
        flat = lambda a: a.reshape(a.shape[0], 4 * rs, D)
        acc["win"] = adam_update(rwin, gwin_own, me, wt, m_wt, v_wt, l, acc["win"], f"adam_win{l}")
        acc["sq"] = adam_update(flat(rsq), flat(gsq_own), me, flat(wsq32), flat(msq32), flat(vsq32), l, acc["sq"], f"adam_wsq{l}")
        acc["conv"] = adam_update(rconv, gconv_own, me, conv_pack, mconv, vconv, l, acc["conv"], f"adam_conv{l}")

    pending = [None] * L
    for l in reversed(range(L)):
        sv = saved[l]
        proj = sv["proj"]
        (dyah, dybh, dych, dzb, dyab, dybb, dycb, dgpost, dproj) = merge_bwd(
            dx, sv["z"], sv["ya"], sv["yb"], sv["yc"], proj, sv["wsq"], vec(norm_post, l), f"merge_bwd{l}")
        gsq = [tn_matmul(a, b, f"dwsq{t}_{l}") for t, (a, b) in enumerate(
            ((sv["yah"], dyab), (sv["ybh"], dybb), (sv["ych"], dycb), (sv["y16"], dzb)))]
        gsq_parts = jnp.stack([g.reshape(NDEV, rs, D) for g in gsq], axis=1)
        st1 = scatter_start([gsq_parts], f"rs_sq_start{l}")
        convw = sv["convw"] + st1[3][0, 0]
        gca, dproj = brancha_bwd(dyah, proj, convw, dproj, f"bra_bwd{l}")
        du1, dlg, dlb, dcb, dproj = branchc2_bwd(dych, sv["u1"], proj, vec(ln_c_g, l), vec(ln_c_b, l), dproj, f"brc2_bwd{l}")
        gcc, dproj = branchc1_bwd(du1, proj, convw, dproj, f"brc1_bwd{l}")
        dqh, dkh, dvh, dproj = attn_bwd(dybh, sv["o"], sv["qh"], sv["kh"], sv["vh"], proj, dproj, f"attn_bwd{l}")
        dqn, dkn, dproj = qkv_bwd(dqh, dkh, dvh, proj, vec(q_norm, l), vec(k_norm, l), cos, sin, dproj, f"qkv_bwd{l}")
        gwin = dwin_parts(sv["h"], dproj, f"dwin{l}").reshape(NDEV, PSH, D)
        gconv = jnp.concatenate([gca, gcc], axis=1)
        if l > 0:
            st2 = scatter_start([gwin, gconv], f"rs_win_start{l}")
            issued = st2[3][0, 0]
        else:
            st2 = scatter_start([gconv], "rs_conv_start0")
            pair = split_start([gwin, lax.empty((NDEV // 2, PSH, D), bf16)], _pair_copies, NDEV // 2, "rs_pair_start0")
            issued = st2[3][0, 0] + pair[3][0, 0]
        dx, dgpre = dh_bwd(dproj, sv["wfull"], sv["x"], dx, vec(norm_pre, l) + issued, f"dh{l}")
        wide = lambda a: jnp.pad(a, ((0, 0), (0, D - HD)))
        small_parts[l] = jnp.concatenate([dgpre, dgpost, dcb, dlg, dlb, wide(dqn), wide(dkn), jnp.zeros((1, D), f32)], axis=0)
        pending[l] = (st1, st2)

    gwin0, pair_land = split_wait(pair[0], pair[1], pair[2], _pair_copies, dx, "rs_pair_wait0")
    summed = pair_sum(gwin0, pair_land, me, "rs_pair_sum0")
    (small_all,) = all_gather([jnp.concatenate(small_parts, axis=0)], "ag_small")
    chip = split_start([summed, lax.empty(summed.shape, bf16)], _chip_copies, NDEV // 2 - 1, "rs_chip_start0", after=small_all)
    for l in reversed(range(1, L)):
        finish(l, pending[l], after=chip[3])
    sm = adam_update(small_all, small_all, me,
                     _pack_small(norm_pre, norm_post, conv_c_b, ln_c_g, ln_c_b, q_norm, k_norm)[None],
                     _pack_small(m_norm_pre, m_norm_post, m_conv_c_b, m_ln_c_g, m_ln_c_b, m_q_norm, m_k_norm)[None],
                     _pack_small(v_norm_pre, v_norm_post, v_conv_c_b, v_ln_c_g, v_ln_c_b, v_q_norm, v_k_norm)[None],
                     0, None, "adam_small")
    (s1, r1, b1, _), (s2, r2, b2, _) = pending[0]
    gsq_own, rsq = split_wait(s1, r1, b1, _scatter_copies(1), sm[0], "rs_sq_wait0")
    gconv_own, rconv = split_wait(s2, r2, b2, _scatter_copies(1), sm[0], "rs_conv_wait0")
    flat = lambda a: a.reshape(a.shape[0], 4 * rs, D)
    acc["sq"] = adam_update(flat(rsq), flat(gsq_own), me, flat(wsq32), flat(msq32), flat(vsq32), 0, acc["sq"], "adam_wsq0")
    acc["conv"] = adam_update(rconv, gconv_own, me, conv_pack, mconv, vconv, 0, acc["conv"], "adam_conv0")
    summed, chip_land = split_wait(chip[0], chip[1], chip[2], _chip_copies, acc["sq"][0], "rs_chip_wait0")
    acc["win"] = adam_update(chip_land, summed, me // 2, wt, m_wt, v_wt, 0, acc["win"], "adam_win0")
    sm = [a.reshape(L, 8, D) for a in sm]
    small_rows = dict(norm_pre=(0, D), norm_post=(1, D), conv_c_b=(2, D), ln_c_g=(3, D), ln_c_b=(4, D), q_norm=(5, HD), k_norm=(6, HD))
    sq_rows = dict(w_out_a=0, w_out_b=1, w_out_c=2, w_o=3)

    order = ["norm_pre", "norm_post", "w_in", "conv_a_w", "q_norm", "k_norm", "conv_c_w", "conv_c_b", "ln_c_g", "ln_c_b",
             "w_out_a", "w_out_b", "w_out_c", "w_o"]
    result = [loss, dx[None]]
    for kind in range(4):
        for nme in order:
            if nme in small_rows:
                rw, wd = small_rows[nme]
                result.append(sm[kind][:, rw, :wd])
            elif nme in sq_rows:
                result.append(acc["sq"][kind][:, sq_rows[nme] * rs:(sq_rows[nme] + 1) * rs])
            elif nme == "w_in":
                result.append(jnp.swapaxes(acc["win"][kind], 1, 2))
            elif nme == "conv_a_w":
                result.append(acc["conv"][kind][:, 0:CA_W])
            else:
                result.append(acc["conv"][kind][:, 8:8 + CC_W])
    return tuple(result)
```

```python
import math

import jax
import jax.numpy as jnp
from jax import lax
from jax.experimental import pallas as pl
from jax.experimental.pallas import tpu as pltpu

f32, bf16 = jnp.float32, jnp.bfloat16

D = 1024
S = 2048
L = 4
HD = 128
NQ = D // HD
NKV = NQ // 4
G = NQ // NKV
WKV = NKV * HD
GRID_W = 64
ROPE_THETA = 10000.0
RMS_EPS = 1e-6
LN_EPS = 1e-5
NDEV = 8
CA_W, CC_W = 3, 31
P = 12 * D + 2 * WKV
PSH = P // NDEV
CT = 128
ADAM_LR, ADAM_B1, ADAM_B2, ADAM_EPS, ADAM_WD, ADAM_STEP = 0.001, 0.9, 0.999, 1e-08, 0.01, 10
VMEM_LIMIT = 56 * 1024 * 1024
MESH = pl.DeviceIdType.MESH

_OFF = {}
_o = 0
for _n, _w in (("a_b", D), ("a_c", D), ("a_x", D), ("a_g", D), ("q", D), ("k", WKV), ("v", WKV), ("b_g", D),
               ("c_u", D), ("c_v", D), ("c_g", D), ("m_a", D), ("m_b", D), ("m_c", D)):
    _OFF[_n] = (_o, _w)
    _o += _w
PIECES = tuple(_OFF)


def _cp(sem=None, **kw):
    return pltpu.CompilerParams(dimension_semantics=sem, vmem_limit_bytes=VMEM_LIMIT, **kw)


def _sig(x):
    return 1.0 / (1.0 + jnp.exp(-x))


def _silu(x):
    return x * _sig(x)


def _dsilu(x):
    s = _sig(x)
    return s * (1.0 + x * (1.0 - s))


def _row_specs(name, tm):
    off, w = _OFF[name]
    bw = math.gcd(off, w) if off else w
    return [pl.BlockSpec((tm, bw), (lambda i, *_, b=off // bw + t: (i, b))) for t in range(w // bw)]


def _cat(refs):
    return refs[0][...] if len(refs) == 1 else jnp.concatenate([r[...] for r in refs], axis=1)


def _chan_spec(name):
    off, _ = _OFF[name]
    return pl.BlockSpec((S, CT), lambda j, b=off // CT: (0, b + j))


def _full(shape):
    return pl.BlockSpec(shape, lambda *_: (0,) * len(shape))


def _coords():
    return lax.axis_index("x"), lax.axis_index("y"), lax.axis_index("c")


def all_gather(shards, name):
    n = len(shards)

    def body(*refs):
        ins, outs = refs[:n], refs[n:2 * n]
        send_sems, recv_sems, local_sems = refs[2 * n:]
        x, y, c = _coords()
        me, sibling = (x, y, c), (x, y, 1 - c)
        chips = [(1 - x, y), (x, 1 - y), (1 - x, 1 - y)]

        def slot(a, p):
            return outs[a].at[4 * p[0] + 2 * p[1] + p[2]]

        def copy(a, k, block, to, src=None):
            return pltpu.make_async_remote_copy(
                src_ref=slot(a, block) if src is None else src, dst_ref=slot(a, block),
                send_sem=send_sems.at[7 * a + k], recv_sem=recv_sems.at[7 * a + k], device_id=to, device_id_type=MESH)

        mine = [pltpu.make_async_copy(ins[a], slot(a, me), local_sems.at[a]) for a in range(n)]
        for cp in mine:
            cp.start()
        first = []
        for a in range(n):
            first.append(copy(a, 0, me, sibling, src=ins[a]))
            first += [copy(a, 1 + j, me, (*chip, c), src=ins[a]) for j, chip in enumerate(chips)]
        for cp in first:
            cp.start()
        passed = []
        for j, chip in enumerate(chips):
            for a in range(n):
                copy(a, 1 + j, (*chip, c), me).wait_recv()
                fw = copy(a, 4 + j, (*chip, c), sibling)
                fw.start()
                passed.append(fw)
        for a in range(n):
            copy(a, 0, sibling, me).wait_recv()
            for j, chip in enumerate(chips):
                copy(a, 4 + j, (*chip, 1 - c), me).wait_recv()
        for cp in first + passed:
            cp.wait_send()
        for cp in mine:
            cp.wait()

    anyspec = pl.BlockSpec(memory_space=pl.ANY)
    return pl.pallas_call(
        body, name=name,
        out_shape=[jax.ShapeDtypeStruct((NDEV,) + s.shape, s.dtype) for s in shards],
        in_specs=[anyspec] * n, out_specs=[anyspec] * n,
        scratch_shapes=[pltpu.SemaphoreType.DMA((7 * n,)), pltpu.SemaphoreType.DMA((7 * n,)), pltpu.SemaphoreType.DMA((n,))],
    )(*shards)


_HBM = pl.BlockSpec(memory_space=pltpu.HBM)
_SEM = pl.BlockSpec(memory_space=pltpu.SEMAPHORE)
_EFFECT = pltpu.SideEffectType.DATAFLOW_SIDE_EFFECTING


def split_start(bufs, make_copies, nsem, name, after=None):
    n = len(bufs)
    extra = [] if after is None else [after]

    def body(*refs):
        send_sems, recv_sems = refs[n + len(extra):n + len(extra) + 2]
        for cp in make_copies(refs[:n], send_sems, recv_sems):
            cp.start()
        refs[-1][...] = jnp.zeros((8, 128), f32)

    res = pl.pallas_call(
        body, name=name,
        out_shape=(pltpu.SemaphoreType.DMA((nsem,)), pltpu.SemaphoreType.DMA((nsem,)),
                   *[pltpu.HBM(b.shape, b.dtype) for b in bufs], jax.ShapeDtypeStruct((8, 128), f32)),
        in_specs=[_HBM] * n + [pl.BlockSpec(memory_space=pl.ANY)] * len(extra),
        out_specs=(_SEM, _SEM, *([_HBM] * n), pl.BlockSpec(memory_space=pltpu.VMEM)),
        input_output_aliases={i: 2 + i for i in range(n)},
        compiler_params=pltpu.CompilerParams(has_side_effects=_EFFECT),
    )(*[pltpu.with_memory_space_constraint(b, pltpu.HBM) for b in bufs], *extra)
    return res[0], res[1], list(res[2:2 + n]), res[-1]


def split_wait(send_sems, recv_sems, bufs, make_copies, after, name):
    n = len(bufs)

    def body(*refs):
        for cp in make_copies(refs[:n], refs[n], refs[n + 1]):
            cp.wait_send()
            cp.wait_recv()

    res = pl.pallas_call(
        body, name=name,
        out_shape=tuple(pltpu.HBM(b.shape, b.dtype) for b in bufs),
        in_specs=[_HBM] * n + [_SEM, _SEM, pl.BlockSpec(memory_space=pl.ANY)],
        out_specs=[_HBM] * n,
        input_output_aliases={i: i for i in range(n)},
        compiler_params=pltpu.CompilerParams(has_side_effects=_EFFECT),
    )(*bufs, send_sems, recv_sems, after)
    return list(res)


def _scatter_copies(n):
    def make(refs, send_sems, recv_sems):
        x, y, c = _coords()
        me = 4 * x + 2 * y + c
        copies = []
        for a in range(n):
            for k in range(1, NDEV):
                px = 1 - x if (k >> 2) & 1 else x
                py = 1 - y if (k >> 1) & 1 else y
                pc = 1 - c if k & 1 else c
                copies.append(pltpu.make_async_remote_copy(
                    src_ref=refs[a].at[4 * px + 2 * py + pc], dst_ref=refs[n + a].at[me],
                    send_sem=send_sems.at[7 * a + k - 1], recv_sem=recv_sems.at[7 * a + k - 1],
                    device_id=(px, py, pc), device_id_type=MESH))
        return copies
    return make


def _gather_copies(refs, send_sems, recv_sems):
    x, y, c = _coords()
    me = 4 * x + 2 * y + c
    targets = [(x, y, 1 - c), (1 - x, y, c), (x, 1 - y, c), (1 - x, 1 - y, c)]
    return [pltpu.make_async_remote_copy(
        src_ref=r.at[me], dst_ref=r.at[me], send_sem=send_sems.at[4 * a + k], recv_sem=recv_sems.at[4 * a + k],
        device_id=to, device_id_type=MESH) for a, r in enumerate(refs) for k, to in enumerate(targets)]


def _forward_copies(refs, send_sems, recv_sems):
    x, y, c = _coords()
    chips = [(1 - x, y), (x, 1 - y), (1 - x, 1 - y)]
    return [pltpu.make_async_remote_copy(
        src_ref=r.at[4 * px + 2 * py + c], dst_ref=r.at[4 * px + 2 * py + c], send_sem=send_sems.at[3 * a + j],
        recv_sem=recv_sems.at[3 * a + j], device_id=(x, y, 1 - c), device_id_type=MESH)
        for a, r in enumerate(refs) for j, (px, py) in enumerate(chips)]


def _pair_copies(refs, send_sems, recv_sems):
    x, y, c = _coords()
    parts, land = refs
    return [pltpu.make_async_remote_copy(
        src_ref=parts.at[2 * j + 1 - c], dst_ref=land.at[j], send_sem=send_sems.at[j], recv_sem=recv_sems.at[j],
        device_id=(x, y, 1 - c), device_id_type=MESH) for j in range(NDEV // 2)]


def _chip_copies(refs, send_sems, recv_sems):
    x, y, c = _coords()
    summed, land = refs
    copies = []
    for k in range(1, NDEV // 2):
        px = 1 - x if (k >> 1) & 1 else x
        py = 1 - y if k & 1 else y
        copies.append(pltpu.make_async_remote_copy(
            src_ref=summed.at[2 * px + py], dst_ref=land.at[2 * x + y], send_sem=send_sems.at[k - 1],
            recv_sem=recv_sems.at[k - 1], device_id=(px, py, c), device_id_type=MESH))
    return copies


def pair_sum(parts, land, me, name):
    _, r, c = parts.shape
    tr = _row_tile(r)

    def body(me_ref, a_ref, b_ref, o_ref):
        o_ref[...] = (a_ref[...].astype(f32) + b_ref[...].astype(f32)).astype(bf16)

    blk = pl.BlockSpec((1, tr, c), lambda j, i, m: (j, i, 0))
    return pl.pallas_call(
        body, name=name, out_shape=jax.ShapeDtypeStruct((NDEV // 2, r, c), bf16),
        grid_spec=pltpu.PrefetchScalarGridSpec(
            num_scalar_prefetch=1, grid=(NDEV // 2, r // tr),
            in_specs=[pl.BlockSpec((1, tr, c), lambda j, i, m: (2 * j + m[0] % 2, i, 0)), blk], out_specs=blk),
        compiler_params=_cp(("parallel", "parallel")))(me, parts, land)


def _row_tile(r):
    return r if r <= 256 else max(t for t in (256, 160, 128) if r % t == 0)


def stage_shards(wt_l, wsq_l, conv_l, me, name, after=None):
    outs = []
    extra = [] if after is None else [after]
    for a, dt in ((wt_l, bf16), (wsq_l, bf16), (conv_l, f32)):
        r, c = a.shape
        tr = _row_tile(r)

        def body(me_ref, a_ref, *rest):
            rest[-1][0] = a_ref[...].astype(rest[-1].dtype)

        outs.append(pl.pallas_call(
            body, name=f"{name}_{len(outs)}", out_shape=jax.ShapeDtypeStruct((NDEV, r, c), dt),
            grid_spec=pltpu.PrefetchScalarGridSpec(
                num_scalar_prefetch=1, grid=(r // tr,),
                in_specs=[pl.BlockSpec((tr, c), lambda i, m: (i, 0))] + [pl.BlockSpec(memory_space=pl.ANY)] * len(extra),
                out_specs=pl.BlockSpec((1, tr, c), lambda i, m: (m[0], i, 0))),
            compiler_params=_cp(("arbitrary",)))(me, a, *extra))
    return outs


def proj_fwd(xin, g_pre, wt, name):
    tm, tn = min(S, 1024), 1280

    def body(x_ref, g_ref, w_ref, proj_ref, h_ref, hs):
        @pl.when(pl.program_id(1) == 0)
        def _():
            x = x_ref[...]
            r = lax.rsqrt(jnp.mean(x * x, axis=-1, keepdims=True) + RMS_EPS)
            h = (x * r * g_ref[...]).astype(bf16)
            hs[...] = h
            h_ref[...] = h
        proj_ref[...] = lax.dot_general(hs[...], w_ref[...], (((1,), (1,)), ((), ())), preferred_element_type=f32)

    return pl.pallas_call(
        body, name=name, grid=(S // tm, P // tn),
        out_shape=[jax.ShapeDtypeStruct((S, P), f32), jax.ShapeDtypeStruct((S, D), bf16)],
        in_specs=[pl.BlockSpec((tm, D), lambda i, j: (i, 0)), _full((1, D)), pl.BlockSpec((tn, D), lambda i, j: (j, 0))],
        out_specs=[pl.BlockSpec((tm, tn), lambda i, j: (i, j)), pl.BlockSpec((tm, D), lambda i, j: (i, 0))],
        scratch_shapes=[pltpu.VMEM((tm, D), bf16)],
        compiler_params=_cp(("parallel", "arbitrary")))(xin, g_pre, wt)


RC = 128


def _fill_pad(pad, halo, val_fn):
    pad[0:halo, :] = jnp.zeros((halo, CT), f32)
    pad[S + halo:S + 2 * halo, :] = jnp.zeros((halo, CT), f32)

    def step(i, carry):
        rows = pl.ds(pl.multiple_of(i * RC, RC), RC)
        pad[pl.ds(pl.multiple_of(i * RC, RC) + halo, RC), :] = val_fn(rows)
        return carry
    lax.fori_loop(0, S // RC, step, 0)


def brancha_fwd(proj, convw, name):
    def body(ab, ac, ax, ag, w_ref, o_ref, pad):
        _fill_pad(pad, 8, lambda rows: ac[rows, :] * ax[rows, :])
        w = [w_ref[0, k:k + 1, :] for k in range(CA_W)]

        def step(i, carry):
            base = pl.multiple_of(i * RC, RC)
            rows = pl.ds(base, RC)
            t = sum(w[k] * pad[pl.ds(base + 7 + k, RC), :] for k in range(CA_W))
            o_ref[rows, :] = (ab[rows, :] * t * _silu(ag[rows, :])).astype(bf16)
            return carry
        lax.fori_loop(0, S // RC, step, 0)

    return pl.pallas_call(
        body, name=name, grid=(D // CT,), out_shape=jax.ShapeDtypeStruct((S, D), bf16),
        in_specs=[_chan_spec("a_b"), _chan_spec("a_c"), _chan_spec("a_x"), _chan_spec("a_g"),
                  pl.BlockSpec((1, 40, CT), lambda j: (j, 0, 0))],
        out_specs=pl.BlockSpec((S, CT), lambda j: (0, j)),
        scratch_shapes=[pltpu.VMEM((S + 16, CT), f32)],
        compiler_params=_cp(("parallel",)))(proj, proj, proj, proj, convw)


def branchc1_fwd(proj, convw, cbias, name):
    def body(cu, cv, w_ref, b_ref, o_ref, pad):
        _fill_pad(pad, 16, lambda rows: cu[rows, :] * _sig(cv[rows, :]))

        def step(i, carry):
            base = pl.multiple_of(i * RC, RC)
            acc = jnp.zeros((RC, CT), f32) + b_ref[...]
            for k in range(CC_W):
                acc = acc + w_ref[0, 8 + k:9 + k, :] * pad[pl.ds(base + k + 1, RC), :]
            o_ref[pl.ds(base, RC), :] = acc
            return carry
        lax.fori_loop(0, S // RC, step, 0)

    return pl.pallas_call(
        body, name=name, grid=(D // CT,), out_shape=jax.ShapeDtypeStruct((S, D), f32),
        in_specs=[_chan_spec("c_u"), _chan_spec("c_v"), pl.BlockSpec((1, 40, CT), lambda j: (j, 0, 0)),
                  pl.BlockSpec((1, CT), lambda j: (0, j))],
        out_specs=pl.BlockSpec((S, CT), lambda j: (0, j)),
        scratch_shapes=[pltpu.VMEM((S + 32, CT), f32)],
        compiler_params=_cp(("parallel",)))(proj, proj, convw, cbias)


def _swap32(x):
    lane = lax.broadcasted_iota(jnp.int32, x.shape, 1)
    return jnp.where((lane // 32) % 2 == 1, pltpu.roll(x, 32, 1), pltpu.roll(x, HD - 32, 1))


def _rope(y, cos, sin):
    return y * cos + _swap32(y) * sin


def qkv_fwd(proj, qn, kn, cos, sin, name):
    tm = min(S, 256)
    nq, nk, nv = len(_row_specs("q", tm)), len(_row_specs("k", tm)), len(_row_specs("v", tm))

    def body(*refs):
        q = _cat(refs[:nq])
        k = _cat(refs[nq:nq + nk])
        v = _cat(refs[nq + nk:nq + nk + nv])
        qn_ref, kn_ref, cos_ref, sin_ref, qh_ref, kh_ref, vh_ref = refs[nq + nk + nv:]
        cos, sin = cos_ref[...], sin_ref[...]

        def heads(xx, gn, out_ref, n):
            for h in range(n):
                xh = xx[:, h * HD:(h + 1) * HD]
                r = lax.rsqrt(jnp.mean(xh * xh, axis=-1, keepdims=True) + RMS_EPS)
                out_ref[:, h * HD:(h + 1) * HD] = _rope(xh * r * gn, cos, sin).astype(bf16)
        heads(q, qn_ref[...], qh_ref, NQ)
        heads(k, kn_ref[...], kh_ref, NKV)
        vh_ref[...] = v.astype(bf16)

    row = lambda w: pl.BlockSpec((tm, w), lambda i: (i, 0))
    return pl.pallas_call(
        body, name=name, grid=(S // tm,),
        out_shape=[jax.ShapeDtypeStruct((S, D), bf16), jax.ShapeDtypeStruct((S, WKV), bf16), jax.ShapeDtypeStruct((S, WKV), bf16)],
        in_specs=_row_specs("q", tm) + _row_specs("k", tm) + _row_specs("v", tm) + [_full((1, HD)), _full((1, HD)), row(HD), row(HD)],
        out_specs=[row(D), row(WKV), row(WKV)],
        compiler_params=_cp(("parallel",)))(*([proj] * (nq + nk + nv)), qn, kn, cos, sin)


def _softmax_rows(q, k):
    s = lax.dot_general(q, k, (((1,), (1,)), ((), ())), preferred_element_type=f32)
    p = jnp.exp((s - jnp.max(s, axis=-1, keepdims=True)) * (HD ** -0.5))
    return p, 1.0 / jnp.sum(p, axis=-1, keepdims=True)


GW = G * HD


def attn_fwd(qh, kh, vh, proj, name):
    tq = min(S, 256)
    bg_blk = _OFF["b_g"][0] // GW

    def body(q_ref, k_ref, v_ref, bg_ref, o_ref, y_ref):
        k, v = k_ref[...], v_ref[...]
        for g in range(G):
            cols = slice(g * HD, (g + 1) * HD)
            p, rl = _softmax_rows(q_ref[:, cols], k)
            o = jnp.dot(p.astype(bf16), v, preferred_element_type=f32) * rl
            o_ref[:, cols] = o
            y_ref[:, cols] = (o * _silu(bg_ref[:, cols])).astype(bf16)

    grp = pl.BlockSpec((tq, GW), lambda kv, i: (i, kv))
    kvs = pl.BlockSpec((S, HD), lambda kv, i: (0, kv))
    return pl.pallas_call(
        body, name=name, grid=(NKV, S // tq),
        out_shape=[jax.ShapeDtypeStruct((S, D), f32), jax.ShapeDtypeStruct((S, D), bf16)],
        in_specs=[grp, kvs, kvs, pl.BlockSpec((tq, GW), lambda kv, i: (i, bg_blk + kv))],
        out_specs=[grp, grp],
        compiler_params=_cp(("parallel", "parallel")))(qh, kh, vh, proj)


def _ln_parts(u1):
    mu = jnp.mean(u1, axis=-1, keepdims=True)
    xc = u1 - mu
    rstd = lax.rsqrt(jnp.mean(xc * xc, axis=-1, keepdims=True) + LN_EPS)
    return xc * rstd, rstd


def branchc2_fwd(u1, proj, lng, lnb, name):
    tm = min(S, 256)
    ncg = len(_row_specs("c_g", tm))

    def body(*refs):
        u_ref = refs[0]
        cg = _cat(refs[1:1 + ncg])
        g_ref, b_ref, o_ref = refs[1 + ncg:]
        xh, _ = _ln_parts(u_ref[...])
        o_ref[...] = (_silu(xh * g_ref[...] + b_ref[...]) * _silu(cg)).astype(bf16)

    row = pl.BlockSpec((tm, D), lambda i: (i, 0))
    return pl.pallas_call(
        body, name=name, grid=(S // tm,), out_shape=jax.ShapeDtypeStruct((S, D), bf16),
        in_specs=[row] + _row_specs("c_g", tm) + [_full((1, D)), _full((1, D))], out_specs=row,
        compiler_params=_cp(("parallel",)))(u1, *([proj] * ncg), lng, lnb)


def _wmat(w_ref, kind):
    return w_ref[:, kind].reshape(D, D)


def merge_fwd(xin, yah, ybh, ych, proj, wsq, g_post, name):
    tm = min(S, 256)
    nm = len(_row_specs("m_a", tm))

    def body(*refs):
        x_ref, a_ref, b_ref, c_ref = refs[:4]
        ms = [_cat(refs[4 + t * nm:4 + (t + 1) * nm]) for t in range(3)]
        w_ref, g_ref, ya_ref, yb_ref, yc_ref, y_ref, z_ref, o_ref = refs[4 + 3 * nm:]
        y = jnp.zeros((tm, D), f32)
        for t, (h_ref, out_ref) in enumerate(((a_ref, ya_ref), (b_ref, yb_ref), (c_ref, yc_ref))):
            yt = jnp.dot(h_ref[...], _wmat(w_ref, t), preferred_element_type=f32)
            out_ref[...] = yt
            y = y + _sig(ms[t]) * yt
        yb16 = y.astype(bf16)
        y_ref[...] = yb16
        z = jnp.dot(yb16, _wmat(w_ref, 3), preferred_element_type=f32)
        z_ref[...] = z
        r = lax.rsqrt(jnp.mean(z * z, axis=-1, keepdims=True) + RMS_EPS)
        o_ref[...] = x_ref[...] + z * r * g_ref[...]

    row = pl.BlockSpec((tm, D), lambda i: (i, 0))
    sd = lambda dt: jax.ShapeDtypeStruct((S, D), dt)
    return pl.pallas_call(
        body, name=name, grid=(S // tm,),
        out_shape=[sd(f32), sd(f32), sd(f32), sd(bf16), sd(f32), sd(f32)],
        in_specs=[row] * 4 + _row_specs("m_a", tm) + _row_specs("m_b", tm) + _row_specs("m_c", tm)
        + [_full((NDEV, 4, D // NDEV, D)), _full((1, D))],
        out_specs=[row] * 6,
        compiler_params=_cp(("parallel",)))(xin, yah, ybh, ych, *([proj] * (3 * nm)), wsq, g_post)


def loss_fwd(y, target, name):
    tm = min(S, 256)

    def body(y_ref, t_ref, dy_ref, l_ref):
        e = y_ref[...] - t_ref[...]
        dy_ref[...] = e / D

        @pl.when(pl.program_id(0) == 0)
        def _():
            l_ref[...] = jnp.zeros((1, 128), f32)
        l_ref[...] += (0.5 / D) * jnp.sum(e * e)

    row = pl.BlockSpec((tm, D), lambda i: (i, 0))
    return pl.pallas_call(
        body, name=name, grid=(S // tm,),
        out_shape=[jax.ShapeDtypeStruct((S, D), f32), jax.ShapeDtypeStruct((1, 128), f32)],
        in_specs=[row, row], out_specs=[row, _full((1, 128))],
        compiler_params=_cp(("arbitrary",)))(y, target)


def _acc(ref, val):
    @pl.when(pl.program_id(0) == 0)
    def _():
        ref[...] = jnp.zeros(ref.shape, f32)
    ref[...] += val


def _emit_copies(stash, dst, sems, windows):
    return [pltpu.make_async_copy(stash.at[p], dst.at[w], sems.at[p]) for p, w in enumerate(windows)]


def _emit_drain_previous(copies, step):
    @pl.when(step > 0)
    def _():
        for cp in copies:
            cp.wait()


def _emit_start(copies, step, nsteps):
    for cp in copies:
        cp.start()

    @pl.when(step == nsteps - 1)
    def _():
        for cp in copies:
            cp.wait()


def merge_bwd(dout, z, ya, yb, yc, proj, wsq, g_post, name):
    tm = min(S, 256)
    nm = len(_row_specs("m_a", tm))
    nsteps = S // tm

    def body(*refs):
        do_ref, z_ref, ya_ref, yb_ref, yc_ref = refs[:5]
        ms = [_cat(refs[5 + t * nm:5 + (t + 1) * nm]) for t in range(3)]
        w_ref, g_ref = refs[5 + 3 * nm:7 + 3 * nm]
        dh_refs = refs[7 + 3 * nm:10 + 3 * nm]
        dzb_ref = refs[10 + 3 * nm]
        dyb_refs = refs[11 + 3 * nm:14 + 3 * nm]
        dg_ref = refs[14 + 3 * nm]
        dproj_ref, stash, sems = refs[15 + 3 * nm:]
        i = pl.program_id(0)
        rows = pl.ds(pl.multiple_of(i * tm, tm), tm)
        copies = _emit_copies(stash, dproj_ref, sems, [(rows, pl.ds(_OFF[n][0], D)) for n in ("m_a", "m_b", "m_c")])
        nt = (((1,), (1,)), ((), ()))
        z, dout = z_ref[...], do_ref[...]
        r = lax.rsqrt(jnp.mean(z * z, axis=-1, keepdims=True) + RMS_EPS)
        zh = z * r
        _acc(dg_ref, jnp.sum(dout * zh, axis=0, keepdims=True))
        dzh = dout * g_ref[...]
        dz = (r * (dzh - zh * jnp.mean(dzh * zh, axis=-1, keepdims=True))).astype(bf16)
        dzb_ref[...] = dz
        dy = lax.dot_general(dz, _wmat(w_ref, 3), nt, preferred_element_type=f32)
        dms = []
        for t, yt_ref in enumerate((ya_ref, yb_ref, yc_ref)):
            sg = _sig(ms[t])
            dyt = (dy * sg).astype(bf16)
            dyb_refs[t][...] = dyt
            dms.append((dy * yt_ref[...] * sg * (1.0 - sg)).astype(bf16))
            dh_refs[t][...] = lax.dot_general(dyt, _wmat(w_ref, t), nt, preferred_element_type=f32)
        _emit_drain_previous(copies, i)
        for t in range(3):
            stash[t] = dms[t]
        _emit_start(copies, i, nsteps)

    row = pl.BlockSpec((tm, D), lambda i: (i, 0))
    sd = lambda dt: jax.ShapeDtypeStruct((S, D), dt)
    return pl.pallas_call(
        body, name=name, grid=(nsteps,),
        out_shape=[sd(f32)] * 3 + [sd(bf16)] * 4 + [jax.ShapeDtypeStruct((1, D), f32), jax.ShapeDtypeStruct((S, P), bf16)],
        in_specs=[row] * 5 + _row_specs("m_a", tm) + _row_specs("m_b", tm) + _row_specs("m_c", tm)
        + [_full((NDEV, 4, D // NDEV, D)), _full((1, D))],
        out_specs=[row] * 7 + [_full((1, D)), pl.BlockSpec(memory_space=pl.ANY)],
        scratch_shapes=[pltpu.VMEM((3, tm, D), bf16), pltpu.SemaphoreType.DMA((3,))],
        compiler_params=_cp(("arbitrary",)))(dout, z, ya, yb, yc, *([proj] * (3 * nm)), wsq, g_post)


def tn_matmul(a, b, name):
    m, n = a.shape[1], b.shape[1]
    tmm = min(m, 512)

    def body(a_ref, b_ref, o_ref):
        o_ref[...] = lax.dot_general(a_ref[...], b_ref[...], (((0,), (0,)), ((), ())), preferred_element_type=f32).astype(bf16)

    return pl.pallas_call(
        body, name=name, grid=(m // tmm,), out_shape=jax.ShapeDtypeStruct((m, n), bf16),
        in_specs=[pl.BlockSpec((S, tmm), lambda i: (0, i)), _full((S, n))],
        out_specs=pl.BlockSpec((tmm, n), lambda i: (i, 0)),
        compiler_params=_cp(("parallel",)))(a, b)


def dwin_parts(h, dproj, name):
    tn = 640

    def body(d_ref, h_ref, o_ref):
        o_ref[...] = lax.dot_general(d_ref[...], h_ref[...], (((0,), (0,)), ((), ())), preferred_element_type=f32).astype(bf16)

    return pl.pallas_call(
        body, name=name, grid=(P // tn,), out_shape=jax.ShapeDtypeStruct((P, D), bf16),
        in_specs=[pl.BlockSpec((S, tn), lambda j: (0, j)), _full((S, D))],
        out_specs=pl.BlockSpec((tn, D), lambda j: (j, 0)),
        compiler_params=_cp(("parallel",)))(dproj, h)


def _chan_windows(names, j):
    return [(slice(None), pl.ds(pl.multiple_of(_OFF[n][0] + j * CT, CT), CT)) for n in names]


def brancha_bwd(dyah, proj, convw, dproj, name):
    nsteps = D // CT

    def body(d_ref, ab, ac, ax, ag, w_ref, _, dw_ref, dproj_ref, padp, padt, accw, stash, sems):
        j = pl.program_id(0)
        copies = _emit_copies(stash, dproj_ref, sems, _chan_windows(("a_b", "a_c", "a_x", "a_g"), j))
        _fill_pad(padp, 8, lambda rows: ac[rows, :] * ax[rows, :])
        _fill_pad(padt, 8, lambda rows: d_ref[rows, :] * ab[rows, :] * _silu(ag[rows, :]))
        accw[...] = jnp.zeros(accw.shape, f32)
        w = [w_ref[0, k:k + 1, :] for k in range(CA_W)]
        _emit_drain_previous(copies, j)

        def step(i, carry):
            base = pl.multiple_of(i * RC, RC)
            rows = pl.ds(base, RC)
            ps = [padp[pl.ds(base + 7 + k, RC), :] for k in range(CA_W)]
            t = sum(w[k] * ps[k] for k in range(CA_W))
            dp = sum(w[k] * padt[pl.ds(base + 9 - k, RC), :] for k in range(CA_W))
            d, a_b, a_g = d_ref[rows, :], ab[rows, :], ag[rows, :]
            stash[0, rows, :] = (d * t * _silu(a_g)).astype(bf16)
            stash[1, rows, :] = (dp * ax[rows, :]).astype(bf16)
            stash[2, rows, :] = (dp * ac[rows, :]).astype(bf16)
            stash[3, rows, :] = (d * a_b * t * _dsilu(a_g)).astype(bf16)
            dt = padt[pl.ds(base + 8, RC), :]
            for k in range(CA_W):
                accw[8 * k:8 * k + 8, :] += jnp.sum((dt * ps[k]).reshape(RC // 8, 8, CT), axis=0)
            return carry
        lax.fori_loop(0, S // RC, step, 0)
        _emit_start(copies, j, nsteps)
        dw_ref[0] = jnp.zeros((8, CT), f32)
        for k in range(CA_W):
            dw_ref[0, k:k + 1, :] = jnp.sum(accw[8 * k:8 * k + 8, :], axis=0, keepdims=True)

    tile = pl.BlockSpec((S, CT), lambda j: (0, j))
    anyspec = pl.BlockSpec(memory_space=pl.ANY)
    return pl.pallas_call(
        body, name=name, grid=(nsteps,),
        out_shape=[jax.ShapeDtypeStruct((NDEV, 8, CT), f32), jax.ShapeDtypeStruct((S, P), bf16)],
        in_specs=[tile, _chan_spec("a_b"), _chan_spec("a_c"), _chan_spec("a_x"), _chan_spec("a_g"),
                  pl.BlockSpec((1, 40, CT), lambda j: (j, 0, 0)), anyspec],
        out_specs=[pl.BlockSpec((1, 8, CT), lambda j: (j, 0, 0)), anyspec],
        input_output_aliases={6: 1},
        scratch_shapes=[pltpu.VMEM((S + 16, CT), f32), pltpu.VMEM((S + 16, CT), f32), pltpu.VMEM((8 * CA_W, CT), f32),
                        pltpu.VMEM((4, S, CT), bf16), pltpu.SemaphoreType.DMA((4,))],
        compiler_params=_cp(("arbitrary",)))(dyah, proj, proj, proj, proj, convw, dproj)


def branchc2_bwd(dych, u1, proj, lng, lnb, dproj, name):
    tm = min(S, 256)
    ncg = len(_row_specs("c_g", tm))
    nsteps = S // tm

    def body(*refs):
        d_ref, u_ref = refs[:2]
        cg = _cat(refs[2:2 + ncg])
        g_ref, b_ref, _, du_ref, dlg_ref, dlb_ref, dcb_ref, dproj_ref, stash, sems = refs[2 + ncg:]
        i = pl.program_id(0)
        copies = _emit_copies(stash, dproj_ref, sems, [(pl.ds(pl.multiple_of(i * tm, tm), tm), pl.ds(_OFF["c_g"][0], D))])
        d = d_ref[...]
        xh, rstd = _ln_parts(u_ref[...])
        ln = xh * g_ref[...] + b_ref[...]
        _emit_drain_previous(copies, i)
        stash[0] = (d * _silu(ln) * _dsilu(cg)).astype(bf16)
        _emit_start(copies, i, nsteps)
        dln = d * _silu(cg) * _dsilu(ln)
        _acc(dlg_ref, jnp.sum(dln * xh, axis=0, keepdims=True))
        _acc(dlb_ref, jnp.sum(dln, axis=0, keepdims=True))
        dxh = dln * g_ref[...]
        du = rstd * (dxh - jnp.mean(dxh, axis=-1, keepdims=True) - xh * jnp.mean(dxh * xh, axis=-1, keepdims=True))
        du_ref[...] = du
        _acc(dcb_ref, jnp.sum(du, axis=0, keepdims=True))

    row = pl.BlockSpec((tm, D), lambda i: (i, 0))
    vec = jax.ShapeDtypeStruct((1, D), f32)
    anyspec = pl.BlockSpec(memory_space=pl.ANY)
    return pl.pallas_call(
        body, name=name, grid=(nsteps,),
        out_shape=[jax.ShapeDtypeStruct((S, D), f32), vec, vec, vec, jax.ShapeDtypeStruct((S, P), bf16)],
        in_specs=[row, row] + _row_specs("c_g", tm) + [_full((1, D)), _full((1, D)), anyspec],
        out_specs=[row, _full((1, D)), _full((1, D)), _full((1, D)), anyspec],
        input_output_aliases={4 + ncg: 4},
        scratch_shapes=[pltpu.VMEM((1, tm, D), bf16), pltpu.SemaphoreType.DMA((1,))],
        compiler_params=_cp(("arbitrary",)))(dych, u1, *([proj] * ncg), lng, lnb, dproj)


def branchc1_bwd(du1, proj, convw, dproj, name):
    nsteps = D // CT

    def body(d_ref, cu, cv, w_ref, _, dw_ref, dproj_ref, padu, padd, accw, stash, sems):
        j = pl.program_id(0)
        copies = _emit_copies(stash, dproj_ref, sems, _chan_windows(("c_u", "c_v"), j))
        _fill_pad(padu, 16, lambda rows: cu[rows, :] * _sig(cv[rows, :]))
        _fill_pad(padd, 16, lambda rows: d_ref[rows, :])
        accw[...] = jnp.zeros(accw.shape, f32)
        _emit_drain_previous(copies, j)

        def step(i, carry):
            base = pl.multiple_of(i * RC, RC)
            rows = pl.ds(base, RC)
            d = d_ref[rows, :]
            du0 = jnp.zeros((RC, CT), f32)
            for k in range(CC_W):
                du0 = du0 + w_ref[0, 8 + k:9 + k, :] * padd[pl.ds(base + 31 - k, RC), :]
                accw[8 * k:8 * k + 8, :] += jnp.sum((d * padu[pl.ds(base + k + 1, RC), :]).reshape(RC // 8, 8, CT), axis=0)
            sg = _sig(cv[rows, :])
            stash[0, rows, :] = (du0 * sg).astype(bf16)
            stash[1, rows, :] = (du0 * cu[rows, :] * sg * (1.0 - sg)).astype(bf16)
            return carry
        lax.fori_loop(0, S // RC, step, 0)
        _emit_start(copies, j, nsteps)
        dw_ref[0] = jnp.zeros((32, CT), f32)
        for k in range(CC_W):
            dw_ref[0, k:k + 1, :] = jnp.sum(accw[8 * k:8 * k + 8, :], axis=0, keepdims=True)

    tile = pl.BlockSpec((S, CT), lambda j: (0, j))
    anyspec = pl.BlockSpec(memory_space=pl.ANY)
    return pl.pallas_call(
        body, name=name, grid=(nsteps,),
        out_shape=[jax.ShapeDtypeStruct((NDEV, 32, CT), f32), jax.ShapeDtypeStruct((S, P), bf16)],
        in_specs=[tile, _chan_spec("c_u"), _chan_spec("c_v"), pl.BlockSpec((1, 40, CT), lambda j: (j, 0, 0)), anyspec],
        out_specs=[pl.BlockSpec((1, 32, CT), lambda j: (j, 0, 0)), anyspec],
        input_output_aliases={4: 1},
        scratch_shapes=[pltpu.VMEM((S + 32, CT), f32), pltpu.VMEM((S + 32, CT), f32), pltpu.VMEM((8 * 32, CT), f32),
                        pltpu.VMEM((2, S, CT), bf16), pltpu.SemaphoreType.DMA((2,))],
        compiler_params=_cp(("arbitrary",)))(du1, proj, proj, convw, dproj)


def attn_bwd(dybh, o, qh, kh, vh, proj, dproj, name):
    tq = min(S, 256)
    bg_blk = _OFF["b_g"][0] // GW

    def body(d_ref, o_ref, q_ref, k_ref, v_ref, bg_ref, _, dq_ref, dk_ref, dv_ref, dbg_ref):
        @pl.when(pl.program_id(1) == 0)
        def _():
            dk_ref[...] = jnp.zeros(dk_ref.shape, f32)
            dv_ref[...] = jnp.zeros(dv_ref.shape, f32)
        k, v = k_ref[...], v_ref[...]
        tn = (((0,), (0,)), ((), ()))
        dk_acc = jnp.zeros((S, HD), f32)
        dv_acc = jnp.zeros((S, HD), f32)
        for g in range(G):
            cols = slice(g * HD, (g + 1) * HD)
            d, bg, q, o = d_ref[:, cols], bg_ref[:, cols], q_ref[:, cols], o_ref[:, cols]
            dbg_ref[:, cols] = (d * o * _dsilu(bg)).astype(bf16)
            do = d * _silu(bg)
            p, rl = _softmax_rows(q, k)
            dv_acc = dv_acc + lax.dot_general(p.astype(bf16), (do * rl).astype(bf16), tn, preferred_element_type=f32)
            dp = lax.dot_general(do.astype(bf16), v, (((1,), (1,)), ((), ())), preferred_element_type=f32)
            delta = jnp.sum(do * o, axis=-1, keepdims=True)
            ds = (p * (dp - delta)).astype(bf16)
            rs_ = rl * (HD ** -0.5)
            dq_ref[:, cols] = jnp.dot(ds, k, preferred_element_type=f32) * rs_
            dk_acc = dk_acc + lax.dot_general(ds, (q.astype(f32) * rs_).astype(bf16), tn, preferred_element_type=f32)
        dk_ref[...] += dk_acc
        dv_ref[...] += dv_acc

    grp = pl.BlockSpec((tq, GW), lambda kv, i: (i, kv))
    kvs = pl.BlockSpec((S, HD), lambda kv, i: (0, kv))
    return pl.pallas_call(
        body, name=name, grid=(NKV, S // tq),
        out_shape=[jax.ShapeDtypeStruct((S, D), f32), jax.ShapeDtypeStruct((S, WKV), f32),
                   jax.ShapeDtypeStruct((S, WKV), f32), jax.ShapeDtypeStruct((S, P), bf16)],
        in_specs=[grp, grp, grp, kvs, kvs, pl.BlockSpec((tq, GW), lambda kv, i: (i, bg_blk + kv)),
                  pl.BlockSpec(memory_space=pl.ANY)],
        out_specs=[grp, kvs, kvs, pl.BlockSpec((tq, GW), lambda kv, i: (i, bg_blk + kv))],
        input_output_aliases={6: 3},
        compiler_params=_cp(("parallel", "arbitrary")))(dybh, o, qh, kh, vh, proj, dproj)


def qkv_bwd(dqh, dkh, dvh, proj, qn, kn, cos, sin, dproj, name):
    tm = min(S, 256)
    nq, nk = len(_row_specs("q", tm)), len(_row_specs("k", tm))
    nsteps = S // tm
    wq = D + 2 * WKV

    def body(*refs):
        dqh_ref, dkh_ref, dvh_ref = refs[:3]
        q = _cat(refs[3:3 + nq])
        k = _cat(refs[3 + nq:3 + nq + nk])
        qn_ref, kn_ref, cos_ref, sin_ref, _, dqn_ref, dkn_ref, dproj_ref, stash, sems = refs[3 + nq + nk:]
        i = pl.program_id(0)
        copies = _emit_copies(stash, dproj_ref, sems, [(pl.ds(pl.multiple_of(i * tm, tm), tm), pl.ds(_OFF["q"][0], wq))])
        cos, sin = cos_ref[...], sin_ref[...]
        _emit_drain_previous(copies, i)

        def heads(xx, dd, gn, col0, dgn_ref, n):
            dg = jnp.zeros((1, HD), f32)
            for h in range(n):
                xh = xx[:, h * HD:(h + 1) * HD]
                dh = dd[:, h * HD:(h + 1) * HD]
                r = lax.rsqrt(jnp.mean(xh * xh, axis=-1, keepdims=True) + RMS_EPS)
                xn = xh * r
                dy = dh * cos + _swap32(dh * sin)
                dg = dg + jnp.sum(dy * xn, axis=0, keepdims=True)
                dxn = dy * gn
                stash[0, :, col0 + h * HD:col0 + (h + 1) * HD] = (
                    r * (dxn - xn * jnp.mean(dxn * xn, axis=-1, keepdims=True))).astype(bf16)
            _acc(dgn_ref, dg)
        heads(q, dqh_ref[...], qn_ref[...], 0, dqn_ref, NQ)
        heads(k, dkh_ref[...], kn_ref[...], D, dkn_ref, NKV)
        stash[0, :, D + WKV:wq] = dvh_ref[...].astype(bf16)
        _emit_start(copies, i, nsteps)

    row = lambda w: pl.BlockSpec((tm, w), lambda i: (i, 0))
    vec = jax.ShapeDtypeStruct((1, HD), f32)
    anyspec = pl.BlockSpec(memory_space=pl.ANY)
    return pl.pallas_call(
        body, name=name, grid=(nsteps,),
        out_shape=[vec, vec, jax.ShapeDtypeStruct((S, P), bf16)],
        in_specs=[row(D), row(WKV), row(WKV)] + _row_specs("q", tm) + _row_specs("k", tm)
        + [_full((1, HD)), _full((1, HD)), row(HD), row(HD), anyspec],
        out_specs=[_full((1, HD)), _full((1, HD)), anyspec],
        input_output_aliases={7 + nq + nk: 2},
        scratch_shapes=[pltpu.VMEM((1, tm, wq), bf16), pltpu.SemaphoreType.DMA((1,))],
        compiler_params=_cp(("arbitrary",)))(dqh, dkh, dvh, *([proj] * (nq + nk)), qn, kn, cos, sin, dproj)


def dh_bwd(dproj, wfull, xin, dout, g_pre, name):
    tm, tk = min(S, 1024), 1280
    nk = P // tk

    def body(d_ref, w_ref, x_ref, do_ref, g_ref, dx_ref, dg_ref, acc):
        kk = pl.program_id(1)

        @pl.when(kk == 0)
        def _():
            acc[...] = jnp.zeros(acc.shape, f32)
        acc[...] += jnp.dot(d_ref[...], w_ref[...], preferred_element_type=f32)

        @pl.when((kk == 0) & (pl.program_id(0) == 0))
        def _():
            dg_ref[...] = jnp.zeros(dg_ref.shape, f32)

        @pl.when(kk == nk - 1)
        def _():
            x, dh = x_ref[...], acc[...]
            r = lax.rsqrt(jnp.mean(x * x, axis=-1, keepdims=True) + RMS_EPS)
            xn = x * r
            dg_ref[...] += jnp.sum(dh * xn, axis=0, keepdims=True)
            dxn = dh * g_ref[...]
            dx_ref[...] = do_ref[...] + r * (dxn - xn * jnp.mean(dxn * xn, axis=-1, keepdims=True))

    row = pl.BlockSpec((tm, D), lambda i, k: (i, 0))
    return pl.pallas_call(
        body, name=name, grid=(S // tm, nk),
        out_shape=[jax.ShapeDtypeStruct((S, D), f32), jax.ShapeDtypeStruct((1, D), f32)],
        in_specs=[pl.BlockSpec((tm, tk), lambda i, k: (i, k)), pl.BlockSpec((tk, D), lambda i, k: (k, 0)), row, row, _full((1, D))],
        out_specs=[row, _full((1, D))],
        scratch_shapes=[pltpu.VMEM((tm, D), f32)],
        compiler_params=_cp(("arbitrary", "arbitrary")))(dproj, wfull, xin, dout, g_pre)


def adam_update(parts, own, me, w, m, v, l, acc, name):
    lw, r, c = w.shape
    tr = _row_tile(r)
    nslots = parts.shape[0]

    def body(me_ref, p_ref, own_ref, w_ref, m_ref, v_ref, *rest):
        g_ref, d_ref, nm_ref, nv_ref = rest[-4:]
        g = None
        for s in range(nslots):
            part = jnp.where(me_ref[0] == s, own_ref[0], p_ref[s]).astype(f32)
            g = part if g is None else g + part
        nm = ADAM_B1 * m_ref[0] + (1.0 - ADAM_B1) * g
        nv = ADAM_B2 * v_ref[0] + (1.0 - ADAM_B2) * (g * g)
        m_hat = nm / (1.0 - ADAM_B1 ** ADAM_STEP)
        v_hat = nv / (1.0 - ADAM_B2 ** ADAM_STEP)
        g_ref[0] = g
        d_ref[0] = -ADAM_LR * (m_hat / (jnp.sqrt(v_hat) + ADAM_EPS) + ADAM_WD * w_ref[0])
        nm_ref[0] = nm
        nv_ref[0] = nv

    blk = pl.BlockSpec((1, tr, c), lambda i, me_ref: (l, i, 0))
    sd = jax.ShapeDtypeStruct((lw, r, c), f32)
    extra = [] if acc is None else list(acc)
    return pl.pallas_call(
        body, name=name, out_shape=[sd] * 4,
        grid_spec=pltpu.PrefetchScalarGridSpec(
            num_scalar_prefetch=1, grid=(r // tr,),
            in_specs=[pl.BlockSpec((nslots, tr, c), lambda i, me_ref: (0, i, 0)),
                      pl.BlockSpec((1, tr, c), lambda i, me_ref: (me_ref[0], i, 0)), blk, blk, blk]
            + [pl.BlockSpec(memory_space=pl.ANY)] * len(extra),
            out_specs=[blk] * 4),
        input_output_aliases={6 + t: t for t in range(len(extra))},
        compiler_params=_cp(("parallel",)))(me, parts, own, w, m, v, *extra)


def _rope_tables():
    t = jnp.arange(S)
    rows, cols = (t // GRID_W).astype(f32), (t % GRID_W).astype(f32)
    nf = HD // 4
    inv = ROPE_THETA ** (-jnp.arange(nf, dtype=f32) / nf)
    ar, ac = rows[:, None] * inv, cols[:, None] * inv
    cos = jnp.concatenate([jnp.cos(ar), jnp.cos(ar), jnp.cos(ac), jnp.cos(ac)], axis=1)
    sin = jnp.concatenate([-jnp.sin(ar), jnp.sin(ar), -jnp.sin(ac), jnp.sin(ac)], axis=1)
    return cos, sin


def _pack_conv(ca, cc):
    z = lambda n: jnp.zeros((L, n, CT), f32)
    return jnp.concatenate([ca, z(5), cc, z(1)], axis=1)


def _pack_small(npre, npost, ccb, lng, lnb, qn, kn):
    wide = lambda a: jnp.pad(a, ((0, 0), (0, D - HD)))
    return jnp.stack([npre, npost, ccb, lng, lnb, wide(qn), wide(kn), jnp.zeros((L, D), f32)], axis=1).reshape(L * 8, D)


def kernel(x, norm_pre, norm_post, w_in, conv_a_w, q_norm, k_norm, conv_c_w, conv_c_b, ln_c_g, ln_c_b, w_out_a, w_out_b, w_out_c, w_o, loss_target, m_norm_pre, m_norm_post, m_w_in, m_conv_a_w, m_q_norm, m_k_norm, m_conv_c_w, m_conv_c_b, m_ln_c_g, m_ln_c_b, m_w_out_a, m_w_out_b, m_w_out_c, m_w_o, v_norm_pre, v_norm_post, v_w_in, v_conv_a_w, v_q_norm, v_k_norm, v_conv_c_w, v_conv_c_b, v_ln_c_g, v_ln_c_b, v_w_out_a, v_w_out_b, v_w_out_c, v_w_o):
    cos, sin = _rope_tables()
    rs = D // NDEV
    stack_sq = lambda a, b, c, d: jnp.stack([a, b, c, d], axis=1)
    wsq32 = stack_sq(w_out_a, w_out_b, w_out_c, w_o)
    conv_pack = _pack_conv(conv_a_w, conv_c_w)
    vec = lambda a, l: a[l][None, :]
    me = (4 * lax.axis_index("x") + 2 * lax.axis_index("y") + lax.axis_index("c")).astype(jnp.int32).reshape(1)

    def gather_start(l, after):
        return split_start(staged[l], _gather_copies, 12, f"ag_start{l}", after=after)

    def forward_start(l, after):
        s_sems, r_sems, bufs, _ = gathers[l]
        bufs = split_wait(s_sems, r_sems, bufs, _gather_copies, after, f"ag_wait{l}")
        fw = split_start(bufs, _forward_copies, 9, f"ag_fwd_start{l}")
        if l + 1 < L:
            gathers[l + 1] = gather_start(l + 1, fw[3])
            return fw, gathers[l + 1][3]
        return fw, fw[3]

    def forward_wait(fw, after, l):
        s_sems, r_sems, bufs, _ = fw
        return split_wait(s_sems, r_sems, bufs, _forward_copies, after, f"ag_fwd_wait{l}")

    wt, m_wt, v_wt = (jnp.swapaxes(a, 1, 2) for a in (w_in, m_w_in, v_w_in))
    xs, saved = x.reshape(S, D), []
    stage = lambda l, after: stage_shards(wt[l], wsq32[l].reshape(4 * rs, D), conv_pack[l], me, f"stage{l}", after)
    staged = [stage(0, None)]
    gathers = [gather_start(0, None)] + [None] * (L - 1)
    staged += [stage(l, gathers[0][3]) for l in range(1, L)]
    fw, issued = forward_start(0, staged[L - 1][0] if L > 1 else xs)
    wg, wsq, convw = forward_wait(fw, issued, 0)
    for l in range(L):
        wsq = wsq.reshape(NDEV, 4, rs, D)
        wfull = wg.reshape(P, D)
        proj, h = proj_fwd(xs, vec(norm_pre, l), wfull, f"proj{l}")
        yah = brancha_fwd(proj, convw, f"bra{l}")
        u1 = branchc1_fwd(proj, convw, vec(conv_c_b, l), f"brc1_{l}")
        qh, kh, vh = qkv_fwd(proj, vec(q_norm, l), vec(k_norm, l), cos, sin, f"qkv{l}")
        o, ybh = attn_fwd(qh, kh, vh, proj, f"attn{l}")
        ln_g = vec(ln_c_g, l)
        if l + 1 < L:
            fw, issued = forward_start(l + 1, o)
            ln_g = ln_g + issued[0, 0]
        ych = branchc2_fwd(u1, proj, ln_g, vec(ln_c_b, l), f"brc2_{l}")
        ya, yb, yc, y16, z, xo = merge_fwd(xs, yah, ybh, ych, proj, wsq, vec(norm_post, l), f"merge{l}")
        saved.append(dict(x=xs, wfull=wfull, wsq=wsq, convw=convw, proj=proj, h=h, yah=yah, ybh=ybh, ych=ych, u1=u1,
                          qh=qh, kh=kh, vh=vh, o=o, ya=ya, yb=yb, yc=yc, y16=y16, z=z))
        xs = xo
        if l + 1 < L:
            wg, wsq, convw = forward_wait(fw, xs, l + 1)
    dx, loss_part = loss_fwd(xs, loss_target.reshape(S, D), "loss")
    loss = lax.psum(loss_part[0, 0], ("x", "y", "c"))

    acc = dict(win=None, sq=None, conv=None)
    small_parts = [None] * L
    msq32 = stack_sq(m_w_out_a, m_w_out_b, m_w_out_c, m_w_o)
    vsq32 = stack_sq(v_w_out_a, v_w_out_b, v_w_out_c, v_w_o)
    mconv, vconv = _pack_conv(m_conv_a_w, m_conv_c_w), _pack_conv(v_conv_a_w, v_conv_c_w)

    def scatter_start(parts, name, after=None):
        bufs = parts + [lax.empty(p.shape, p.dtype) for p in parts]
        return split_start(bufs, _scatter_copies(len(parts)), 7 * len(parts), name, after=after)

    def finish(l, started, after):
        (s1, r1, b1, _), (s2, r2, b2, _) = started
        gsq_own, rsq = split_wait(s1, r1, b1, _scatter_copies(1), after, f"rs_sq_wait{l}")
        gwin_own, gconv_own, rwin, rconv = split_wait(s2, r2, b2, _scatter_copies(2), after, f"rs_win_wait{l}")
        flat = lambda a: a.reshape(a.shape[0], 4 * rs, D)
        acc["win"] = adam_update(rwin, gwin_own, me, wt, m_wt, v_wt, l, acc["win"], f"adam_win{l}")
        acc["sq"] = adam_update(flat(rsq), flat(gsq_own), me, flat(wsq32), flat(msq32), flat(vsq32), l, acc["sq"], f"adam_wsq{l}")
        acc["conv"] = adam_update(rconv, gconv_own, me, conv_pack, mconv, vconv, l, acc["conv"], f"adam_conv{l}")

    pending = [None] * L
    for l in reversed(range(L)):
        sv = saved[l]
        proj = sv["proj"]
        (dyah, dybh, dych, dzb, dyab, dybb, dycb, dgpost, dproj) = merge_bwd(
            dx, sv["z"], sv["ya"], sv["yb"], sv["yc"], proj, sv["wsq"], vec(norm_post, l), f"merge_bwd{l}")
        gsq = [tn_matmul(a, b, f"dwsq{t}_{l}") for t, (a, b) in enumerate(
            ((sv["yah"], dyab), (sv["ybh"], dybb), (sv["ych"], dycb), (sv["y16"], dzb)))]
        gsq_parts = jnp.stack([g.reshape(NDEV, rs, D) for g in gsq], axis=1)
        st1 = scatter_start([gsq_parts], f"rs_sq_start{l}", after=loss.reshape(1, 1) if l == L - 1 else None)
        convw = sv["convw"] + st1[3][0, 0]
        gca, dproj = brancha_bwd(dyah, proj, convw, dproj, f"bra_bwd{l}")
        du1, dlg, dlb, dcb, dproj = branchc2_bwd(dych, sv["u1"], proj, vec(ln_c_g, l), vec(ln_c_b, l), dproj, f"brc2_bwd{l}")
        gcc, dproj = branchc1_bwd(du1, proj, convw, dproj, f"brc1_bwd{l}")
        dqh, dkh, dvh, dproj = attn_bwd(dybh, sv["o"], sv["qh"], sv["kh"], sv["vh"], proj, dproj, f"attn_bwd{l}")
        dqn, dkn, dproj = qkv_bwd(dqh, dkh, dvh, proj, vec(q_norm, l), vec(k_norm, l), cos, sin, dproj, f"qkv_bwd{l}")
        gwin = dwin_parts(sv["h"], dproj, f"dwin{l}").reshape(NDEV, PSH, D)
        gconv = jnp.concatenate([gca, gcc], axis=1)
        if l > 0:
            st2 = scatter_start([gwin, gconv], f"rs_win_start{l}")
            issued = st2[3][0, 0]
        else:
            st2 = scatter_start([gconv], "rs_conv_start0")
            pair = split_start([gwin, lax.empty((NDEV // 2, PSH, D), bf16)], _pair_copies, NDEV // 2, "rs_pair_start0")
            issued = st2[3][0, 0] + pair[3][0, 0]
        dx, dgpre = dh_bwd(dproj, sv["wfull"], sv["x"], dx, vec(norm_pre, l) + issued, f"dh{l}")
        wide = lambda a: jnp.pad(a, ((0, 0), (0, D - HD)))
        small_parts[l] = jnp.concatenate([dgpre, dgpost, dcb, dlg, dlb, wide(dqn), wide(dkn), jnp.zeros((1, D), f32)], axis=0)
        pending[l] = (st1, st2)

    gwin0, pair_land = split_wait(pair[0], pair[1], pair[2], _pair_copies, dx, "rs_pair_wait0")
    summed = pair_sum(gwin0, pair_land, me, "rs_pair_sum0")
    (small_all,) = all_gather([jnp.concatenate(small_parts, axis=0)], "ag_small")
    chip = split_start([summed, lax.empty(summed.shape, bf16)], _chip_copies, NDEV // 2 - 1, "rs_chip_start0", after=small_all)
    for l in reversed(range(1, L)):
        finish(l, pending[l], after=chip[3])
    sm = adam_update(small_all, small_all, me,
                     _pack_small(norm_pre, norm_post, conv_c_b, ln_c_g, ln_c_b, q_norm, k_norm)[None],
                     _pack_small(m_norm_pre, m_norm_post, m_conv_c_b, m_ln_c_g, m_ln_c_b, m_q_norm, m_k_norm)[None],
                     _pack_small(v_norm_pre, v_norm_post, v_conv_c_b, v_ln_c_g, v_ln_c_b, v_q_norm, v_k_norm)[None],
                     0, None, "adam_small")
    (s1, r1, b1, _), (s2, r2, b2, _) = pending[0]
    gsq_own, rsq = split_wait(s1, r1, b1, _scatter_copies(1), sm[0], "rs_sq_wait0")
    gconv_own, rconv = split_wait(s2, r2, b2, _scatter_copies(1), sm[0], "rs_conv_wait0")
    flat = lambda a: a.reshape(a.shape[0], 4 * rs, D)
    acc["sq"] = adam_update(flat(rsq), flat(gsq_own), me, flat(wsq32), flat(msq32), flat(vsq32), 0, acc["sq"], "adam_wsq0")
    acc["conv"] = adam_update(rconv, gconv_own, me, conv_pack, mconv, vconv, 0, acc["conv"], "adam_conv0")
    summed, chip_land = split_wait(chip[0], chip[1], chip[2], _chip_copies, acc["sq"][0], "rs_chip_wait0")
    acc["win"] = adam_update(chip_land, summed, me // 2, wt, m_wt, v_wt, 0, acc["win"], "adam_win0")
    sm = [a.reshape(L, 8, D) for a in sm]
    small_rows = dict(norm_pre=(0, D), norm_post=(1, D), conv_c_b=(2, D), ln_c_g=(3, D), ln_c_b=(4, D), q_norm=(5, HD), k_norm=(6, HD))
    sq_rows = dict(w_out_a=0, w_out_b=1, w_out_c=2, w_o=3)

    order = ["norm_pre", "norm_post", "w_in", "conv_a_w", "q_norm", "k_norm", "conv_c_w", "conv_c_b", "ln_c_g", "ln_c_b",
             "w_out_a", "w_out_b", "w_out_c", "w_o"]
    result = [loss, dx.reshape(1, S, D)]
    for kind in range(4):
        for nme in order:
            if nme in small_rows:
                rw, wd = small_rows[nme]
                result.append(sm[kind][:, rw, :wd])
            elif nme in sq_rows:
                result.append(acc["sq"][kind][:, sq_rows[nme] * rs:(sq_rows[nme] + 1) * rs])
            elif nme == "w_in":
                result.append(jnp.swapaxes(acc["win"][kind], 1, 2))
            elif nme == "conv_a_w":
                result.append(acc["conv"][kind][:, 0:CA_W])
            else:
                result.append(acc["conv"][kind][:, 8:8 + CC_W])
    return tuple(result)
```

```python
import math

import jax
import jax.numpy as jnp
from jax import lax
from jax.experimental import pallas as pl
from jax.experimental.pallas import tpu as pltpu

f32, bf16 = jnp.float32, jnp.bfloat16

D = 1024
S = 2048
L = 4
HD = 128
NQ = D // HD
NKV = NQ // 4
G = NQ // NKV
WKV = NKV * HD
GRID_W = 64
ROPE_THETA = 10000.0
RMS_EPS = 1e-6
LN_EPS = 1e-5
NDEV = 8
CA_W, CC_W = 3, 31
P = 12 * D + 2 * WKV
PSH = P // NDEV
CT = 128
ADAM_LR, ADAM_B1, ADAM_B2, ADAM_EPS, ADAM_WD, ADAM_STEP = 0.001, 0.9, 0.999, 1e-08, 0.01, 10
VMEM_LIMIT = 56 * 1024 * 1024
MESH = pl.DeviceIdType.MESH

_OFF = {}
_o = 0
for _n, _w in (("a_b", D), ("a_c", D), ("a_x", D), ("a_g", D), ("q", D), ("k", WKV), ("v", WKV), ("b_g", D),
               ("c_u", D), ("c_v", D), ("c_g", D), ("m_a", D), ("m_b", D), ("m_c", D)):
    _OFF[_n] = (_o, _w)
    _o += _w
PIECES = tuple(_OFF)


def _cp(sem=None, **kw):
    return pltpu.CompilerParams(dimension_semantics=sem, vmem_limit_bytes=VMEM_LIMIT, **kw)


def _sig(x):
    return 1.0 / (1.0 + jnp.exp(-x))


def _silu(x):
    return x * _sig(x)


def _dsilu(x):
    s = _sig(x)
    return s * (1.0 + x * (1.0 - s))


def _row_specs(name, tm):
    off, w = _OFF[name]
    bw = math.gcd(off, w) if off else w
    return [pl.BlockSpec((tm, bw), (lambda i, *_, b=off // bw + t: (i, b))) for t in range(w // bw)]


def _cat(refs):
    return refs[0][...] if len(refs) == 1 else jnp.concatenate([r[...] for r in refs], axis=1)


def _chan_spec(name):
    off, _ = _OFF[name]
    return pl.BlockSpec((S, CT), lambda j, b=off // CT: (0, b + j))


def _full(shape):
    return pl.BlockSpec(shape, lambda *_: (0,) * len(shape))


def _coords():
    return lax.axis_index("x"), lax.axis_index("y"), lax.axis_index("c")


def all_gather(shards, name):
    n = len(shards)

    def body(*refs):
        ins, outs = refs[:n], refs[n:2 * n]
        send_sems, recv_sems, local_sems = refs[2 * n:]
        x, y, c = _coords()
        me, sibling = (x, y, c), (x, y, 1 - c)
        chips = [(1 - x, y), (x, 1 - y), (1 - x, 1 - y)]

        def slot(a, p):
            return outs[a].at[4 * p[0] + 2 * p[1] + p[2]]

        def copy(a, k, block, to, src=None):
            return pltpu.make_async_remote_copy(
                src_ref=slot(a, block) if src is None else src, dst_ref=slot(a, block),
                send_sem=send_sems.at[7 * a + k], recv_sem=recv_sems.at[7 * a + k], device_id=to, device_id_type=MESH)

        mine = [pltpu.make_async_copy(ins[a], slot(a, me), local_sems.at[a]) for a in range(n)]
        for cp in mine:
            cp.start()
        first = []
        for a in range(n):
            first.append(copy(a, 0, me, sibling, src=ins[a]))
            first += [copy(a, 1 + j, me, (*chip, c), src=ins[a]) for j, chip in enumerate(chips)]
        for cp in first:
            cp.start()
        passed = []
        for j, chip in enumerate(chips):
            for a in range(n):
                copy(a, 1 + j, (*chip, c), me).wait_recv()
                fw = copy(a, 4 + j, (*chip, c), sibling)
                fw.start()
                passed.append(fw)
        for a in range(n):
            copy(a, 0, sibling, me).wait_recv()
            for j, chip in enumerate(chips):
                copy(a, 4 + j, (*chip, 1 - c), me).wait_recv()
        for cp in first + passed:
            cp.wait_send()
        for cp in mine:
            cp.wait()

    anyspec = pl.BlockSpec(memory_space=pl.ANY)
    return pl.pallas_call(
        body, name=name,
        out_shape=[jax.ShapeDtypeStruct((NDEV,) + s.shape, s.dtype) for s in shards],
        in_specs=[anyspec] * n, out_specs=[anyspec] * n,
        scratch_shapes=[pltpu.SemaphoreType.DMA((7 * n,)), pltpu.SemaphoreType.DMA((7 * n,)), pltpu.SemaphoreType.DMA((n,))],
    )(*shards)


_HBM = pl.BlockSpec(memory_space=pltpu.HBM)
_SEM = pl.BlockSpec(memory_space=pltpu.SEMAPHORE)
_EFFECT = pltpu.SideEffectType.DATAFLOW_SIDE_EFFECTING


def split_start(bufs, make_copies, nsem, name, after=None):
    n = len(bufs)
    extra = [] if after is None else [after]

    def body(*refs):
        send_sems, recv_sems = refs[n + len(extra):n + len(extra) + 2]
        for cp in make_copies(refs[:n], send_sems, recv_sems):
            cp.start()
        refs[-1][...] = jnp.zeros((8, 128), f32)

    res = pl.pallas_call(
        body, name=name,
        out_shape=(pltpu.SemaphoreType.DMA((nsem,)), pltpu.SemaphoreType.DMA((nsem,)),
                   *[pltpu.HBM(b.shape, b.dtype) for b in bufs], jax.ShapeDtypeStruct((8, 128), f32)),
        in_specs=[_HBM] * n + [pl.BlockSpec(memory_space=pl.ANY)] * len(extra),
        out_specs=(_SEM, _SEM, *([_HBM] * n), pl.BlockSpec(memory_space=pltpu.VMEM)),
        input_output_aliases={i: 2 + i for i in range(n)},
        compiler_params=pltpu.CompilerParams(has_side_effects=_EFFECT),
    )(*[pltpu.with_memory_space_constraint(b, pltpu.HBM) for b in bufs], *extra)
    return res[0], res[1], list(res[2:2 + n]), res[-1]


def split_wait(send_sems, recv_sems, bufs, make_copies, after, name):
    n = len(bufs)

    def body(*refs):
        for cp in make_copies(refs[:n], refs[n], refs[n + 1]):
            cp.wait_send()
            cp.wait_recv()

    res = pl.pallas_call(
        body, name=name,
        out_shape=tuple(pltpu.HBM(b.shape, b.dtype) for b in bufs),
        in_specs=[_HBM] * n + [_SEM, _SEM, pl.BlockSpec(memory_space=pl.ANY)],
        out_specs=[_HBM] * n,
        input_output_aliases={i: i for i in range(n)},
        compiler_params=pltpu.CompilerParams(has_side_effects=_EFFECT),
    )(*bufs, send_sems, recv_sems, after)
    return list(res)


def _scatter_copies(n):
    def make(refs, send_sems, recv_sems):
        x, y, c = _coords()
        me = 4 * x + 2 * y + c
        copies = []
        for a in range(n):
            for k in range(1, NDEV):
                px = 1 - x if (k >> 2) & 1 else x
                py = 1 - y if (k >> 1) & 1 else y
                pc = 1 - c if k & 1 else c
                copies.append(pltpu.make_async_remote_copy(
                    src_ref=refs[a].at[4 * px + 2 * py + pc], dst_ref=refs[n + a].at[me],
                    send_sem=send_sems.at[7 * a + k - 1], recv_sem=recv_sems.at[7 * a + k - 1],
                    device_id=(px, py, pc), device_id_type=MESH))
        return copies
    return make


def _gather_copies(refs, send_sems, recv_sems):
    x, y, c = _coords()
    me = 4 * x + 2 * y + c
    targets = [(x, y, 1 - c), (1 - x, y, c), (x, 1 - y, c), (1 - x, 1 - y, c)]
    return [pltpu.make_async_remote_copy(
        src_ref=r.at[me], dst_ref=r.at[me], send_sem=send_sems.at[4 * a + k], recv_sem=recv_sems.at[4 * a + k],
        device_id=to, device_id_type=MESH) for a, r in enumerate(refs) for k, to in enumerate(targets)]


def _forward_copies(refs, send_sems, recv_sems):
    x, y, c = _coords()
    chips = [(1 - x, y), (x, 1 - y), (1 - x, 1 - y)]
    return [pltpu.make_async_remote_copy(
        src_ref=r.at[4 * px + 2 * py + c], dst_ref=r.at[4 * px + 2 * py + c], send_sem=send_sems.at[3 * a + j],
        recv_sem=recv_sems.at[3 * a + j], device_id=(x, y, 1 - c), device_id_type=MESH)
        for a, r in enumerate(refs) for j, (px, py) in enumerate(chips)]


def _pair_copies(refs, send_sems, recv_sems):
    x, y, c = _coords()
    parts, land = refs
    return [pltpu.make_async_remote_copy(
        src_ref=parts.at[2 * j + 1 - c], dst_ref=land.at[j], send_sem=send_sems.at[j], recv_sem=recv_sems.at[j],
        device_id=(x, y, 1 - c), device_id_type=MESH) for j in range(NDEV // 2)]


def _chip_copies(refs, send_sems, recv_sems):
    x, y, c = _coords()
    summed, land = refs
    copies = []
    for k in range(1, NDEV // 2):
        px = 1 - x if (k >> 1) & 1 else x
        py = 1 - y if k & 1 else y
        copies.append(pltpu.make_async_remote_copy(
            src_ref=summed.at[2 * px + py], dst_ref=land.at[2 * x + y], send_sem=send_sems.at[k - 1],
            recv_sem=recv_sems.at[k - 1], device_id=(px, py, c), device_id_type=MESH))
    return copies


def pair_sum(parts, land, me, name):
    _, r, c = parts.shape
    tr = _row_tile(r)

    def body(me_ref, a_ref, b_ref, o_ref):
        o_ref[...] = (a_ref[...].astype(f32) + b_ref[...].astype(f32)).astype(bf16)

    blk = pl.BlockSpec((1, tr, c), lambda j, i, m: (j, i, 0))
    return pl.pallas_call(
        body, name=name, out_shape=jax.ShapeDtypeStruct((NDEV // 2, r, c), bf16),
        grid_spec=pltpu.PrefetchScalarGridSpec(
            num_scalar_prefetch=1, grid=(NDEV // 2, r // tr),
            in_specs=[pl.BlockSpec((1, tr, c), lambda j, i, m: (2 * j + m[0] % 2, i, 0)), blk], out_specs=blk),
        compiler_params=_cp(("parallel", "parallel")))(me, parts, land)


def _row_tile(r):
    return r if r <= 256 else max(t for t in (256, 160, 128) if r % t == 0)


def stage_shards(wt_l, wsq_l, conv_l, me, name, after=None):
    outs = []
    extra = [] if after is None else [after]
    for a, dt in ((wt_l, bf16), (wsq_l, bf16), (conv_l, f32)):
        r, c = a.shape
        tr = _row_tile(r)

        def body(me_ref, a_ref, *rest):
            rest[-1][0] = a_ref[...].astype(rest[-1].dtype)

        outs.append(pl.pallas_call(
            body, name=f"{name}_{len(outs)}", out_shape=jax.ShapeDtypeStruct((NDEV, r, c), dt),
            grid_spec=pltpu.PrefetchScalarGridSpec(
                num_scalar_prefetch=1, grid=(r // tr,),
                in_specs=[pl.BlockSpec((tr, c), lambda i, m: (i, 0))] + [pl.BlockSpec(memory_space=pl.ANY)] * len(extra),
                out_specs=pl.BlockSpec((1, tr, c), lambda i, m: (m[0], i, 0))),
            compiler_params=_cp(("arbitrary",)))(me, a, *extra))
    return outs


def proj_fwd(xin, g_pre, wt, name):
    tm, tn = min(S, 1024), 2560

    def body(x_ref, g_ref, w_ref, proj_ref, h_ref, hs):
        @pl.when(pl.program_id(1) == 0)
        def _():
            x = x_ref[...]
            r = lax.rsqrt(jnp.mean(x * x, axis=-1, keepdims=True) + RMS_EPS)
            h = (x * r * g_ref[...]).astype(bf16)
            hs[...] = h
            h_ref[...] = h
        proj_ref[...] = lax.dot_general(hs[...], w_ref[...], (((1,), (1,)), ((), ())), preferred_element_type=f32)

    return pl.pallas_call(
        body, name=name, grid=(S // tm, P // tn),
        out_shape=[jax.ShapeDtypeStruct((S, P), f32), jax.ShapeDtypeStruct((S, D), bf16)],
        in_specs=[pl.BlockSpec((tm, D), lambda i, j: (i, 0)), _full((1, D)), pl.BlockSpec((tn, D), lambda i, j: (j, 0))],
        out_specs=[pl.BlockSpec((tm, tn), lambda i, j: (i, j)), pl.BlockSpec((tm, D), lambda i, j: (i, 0))],
        scratch_shapes=[pltpu.VMEM((tm, D), bf16)],
        compiler_params=_cp(("parallel", "arbitrary")))(xin, g_pre, wt)


RC = 128


def _fill_pad(pad, halo, val_fn):
    pad[0:halo, :] = jnp.zeros((halo, CT), f32)
    pad[S + halo:S + 2 * halo, :] = jnp.zeros((halo, CT), f32)

    def step(i, carry):
        rows = pl.ds(pl.multiple_of(i * RC, RC), RC)
        pad[pl.ds(pl.multiple_of(i * RC, RC) + halo, RC), :] = val_fn(rows)
        return carry
    lax.fori_loop(0, S // RC, step, 0)


def brancha_fwd(proj, convw, name):
    def body(ab, ac, ax, ag, w_ref, o_ref, pad):
        _fill_pad(pad, 8, lambda rows: ac[rows, :] * ax[rows, :])
        w = [w_ref[0, k:k + 1, :] for k in range(CA_W)]

        def step(i, carry):
            base = pl.multiple_of(i * RC, RC)
            rows = pl.ds(base, RC)
            t = sum(w[k] * pad[pl.ds(base + 7 + k, RC), :] for k in range(CA_W))
            o_ref[rows, :] = (ab[rows, :] * t * _silu(ag[rows, :])).astype(bf16)
            return carry
        lax.fori_loop(0, S // RC, step, 0)

    return pl.pallas_call(
        body, name=name, grid=(D // CT,), out_shape=jax.ShapeDtypeStruct((S, D), bf16),
        in_specs=[_chan_spec("a_b"), _chan_spec("a_c"), _chan_spec("a_x"), _chan_spec("a_g"),
                  pl.BlockSpec((1, 40, CT), lambda j: (j, 0, 0))],
        out_specs=pl.BlockSpec((S, CT), lambda j: (0, j)),
        scratch_shapes=[pltpu.VMEM((S + 16, CT), f32)],
        compiler_params=_cp(("parallel",)))(proj, proj, proj, proj, convw)


def branchc1_fwd(proj, convw, cbias, name):
    def body(cu, cv, w_ref, b_ref, o_ref, pad):
        _fill_pad(pad, 16, lambda rows: cu[rows, :] * _sig(cv[rows, :]))

        def step(i, carry):
            base = pl.multiple_of(i * RC, RC)
            acc = jnp.zeros((RC, CT), f32) + b_ref[...]
            for k in range(CC_W):
                acc = acc + w_ref[0, 8 + k:9 + k, :] * pad[pl.ds(base + k + 1, RC), :]
            o_ref[pl.ds(base, RC), :] = acc
            return carry
        lax.fori_loop(0, S // RC, step, 0)

    return pl.pallas_call(
        body, name=name, grid=(D // CT,), out_shape=jax.ShapeDtypeStruct((S, D), f32),
        in_specs=[_chan_spec("c_u"), _chan_spec("c_v"), pl.BlockSpec((1, 40, CT), lambda j: (j, 0, 0)),
                  pl.BlockSpec((1, CT), lambda j: (0, j))],
        out_specs=pl.BlockSpec((S, CT), lambda j: (0, j)),
        scratch_shapes=[pltpu.VMEM((S + 32, CT), f32)],
        compiler_params=_cp(("parallel",)))(proj, proj, convw, cbias)


def _swap32(x):
    lane = lax.broadcasted_iota(jnp.int32, x.shape, 1)
    return jnp.where((lane // 32) % 2 == 1, pltpu.roll(x, 32, 1), pltpu.roll(x, HD - 32, 1))


def _rope(y, cos, sin):
    return y * cos + _swap32(y) * sin


def qkv_fwd(proj, qn, kn, cos, sin, name):
    tm = min(S, 256)
    nq, nk, nv = len(_row_specs("q", tm)), len(_row_specs("k", tm)), len(_row_specs("v", tm))

    def body(*refs):
        q = _cat(refs[:nq])
        k = _cat(refs[nq:nq + nk])
        v = _cat(refs[nq + nk:nq + nk + nv])
        qn_ref, kn_ref, cos_ref, sin_ref, qh_ref, kh_ref, vh_ref = refs[nq + nk + nv:]
        cos, sin = cos_ref[...], sin_ref[...]

        def heads(xx, gn, out_ref, n):
            for h in range(n):
                xh = xx[:, h * HD:(h + 1) * HD]
                r = lax.rsqrt(jnp.mean(xh * xh, axis=-1, keepdims=True) + RMS_EPS)
                out_ref[:, h * HD:(h + 1) * HD] = _rope(xh * r * gn, cos, sin).astype(bf16)
        heads(q, qn_ref[...], qh_ref, NQ)
        heads(k, kn_ref[...], kh_ref, NKV)
        vh_ref[...] = v.astype(bf16)

    row = lambda w: pl.BlockSpec((tm, w), lambda i: (i, 0))
    return pl.pallas_call(
        body, name=name, grid=(S // tm,),
        out_shape=[jax.ShapeDtypeStruct((S, D), bf16), jax.ShapeDtypeStruct((S, WKV), bf16), jax.ShapeDtypeStruct((S, WKV), bf16)],
        in_specs=_row_specs("q", tm) + _row_specs("k", tm) + _row_specs("v", tm) + [_full((1, HD)), _full((1, HD)), row(HD), row(HD)],
        out_specs=[row(D), row(WKV), row(WKV)],
        compiler_params=_cp(("parallel",)))(*([proj] * (nq + nk + nv)), qn, kn, cos, sin)


def _softmax_rows(q, k):
    s = lax.dot_general(q, k, (((1,), (1,)), ((), ())), preferred_element_type=f32)
    p = jnp.exp((s - jnp.max(s, axis=-1, keepdims=True)) * (HD ** -0.5))
    return p, 1.0 / jnp.sum(p, axis=-1, keepdims=True)


GW = G * HD


def attn_fwd(qh, kh, vh, proj, name):
    tq = min(S, 512)
    bg_blk = _OFF["b_g"][0] // GW

    def body(q_ref, k_ref, v_ref, bg_ref, o_ref, y_ref):
        k, v = k_ref[...], v_ref[...]
        for g in range(G):
            cols = slice(g * HD, (g + 1) * HD)
            p, rl = _softmax_rows(q_ref[:, cols], k)
            o = jnp.dot(p.astype(bf16), v, preferred_element_type=f32) * rl
            o_ref[:, cols] = o
            y_ref[:, cols] = (o * _silu(bg_ref[:, cols])).astype(bf16)

    grp = pl.BlockSpec((tq, GW), lambda kv, i: (i, kv))
    kvs = pl.BlockSpec((S, HD), lambda kv, i: (0, kv))
    return pl.pallas_call(
        body, name=name, grid=(NKV, S // tq),
        out_shape=[jax.ShapeDtypeStruct((S, D), f32), jax.ShapeDtypeStruct((S, D), bf16)],
        in_specs=[grp, kvs, kvs, pl.BlockSpec((tq, GW), lambda kv, i: (i, bg_blk + kv))],
        out_specs=[grp, grp],
        compiler_params=_cp(("parallel", "parallel")))(qh, kh, vh, proj)


def _ln_parts(u1):
    mu = jnp.mean(u1, axis=-1, keepdims=True)
    xc = u1 - mu
    rstd = lax.rsqrt(jnp.mean(xc * xc, axis=-1, keepdims=True) + LN_EPS)
    return xc * rstd, rstd


def branchc2_fwd(u1, proj, lng, lnb, name):
    tm = min(S, 256)
    ncg = len(_row_specs("c_g", tm))

    def body(*refs):
        u_ref = refs[0]
        cg = _cat(refs[1:1 + ncg])
        g_ref, b_ref, o_ref = refs[1 + ncg:]
        xh, _ = _ln_parts(u_ref[...])
        o_ref[...] = (_silu(xh * g_ref[...] + b_ref[...]) * _silu(cg)).astype(bf16)

    row = pl.BlockSpec((tm, D), lambda i: (i, 0))
    return pl.pallas_call(
        body, name=name, grid=(S // tm,), out_shape=jax.ShapeDtypeStruct((S, D), bf16),
        in_specs=[row] + _row_specs("c_g", tm) + [_full((1, D)), _full((1, D))], out_specs=row,
        compiler_params=_cp(("parallel",)))(u1, *([proj] * ncg), lng, lnb)


def _wmat(w_ref, kind):
    return w_ref[:, kind].reshape(D, D)


def merge_fwd(xin, yah, ybh, ych, proj, wsq, g_post, name):
    tm = min(S, 256)
    nm = len(_row_specs("m_a", tm))

    def body(*refs):
        x_ref, a_ref, b_ref, c_ref = refs[:4]
        ms = [_cat(refs[4 + t * nm:4 + (t + 1) * nm]) for t in range(3)]
        w_ref, g_ref, ya_ref, yb_ref, yc_ref, y_ref, z_ref, o_ref = refs[4 + 3 * nm:]
        y = jnp.zeros((tm, D), f32)
        for t, (h_ref, out_ref) in enumerate(((a_ref, ya_ref), (b_ref, yb_ref), (c_ref, yc_ref))):
            yt = jnp.dot(h_ref[...], _wmat(w_ref, t), preferred_element_type=f32)
            out_ref[...] = yt
            y = y + _sig(ms[t]) * yt
        yb16 = y.astype(bf16)
        y_ref[...] = yb16
        z = jnp.dot(yb16, _wmat(w_ref, 3), preferred_element_type=f32)
        z_ref[...] = z
        r = lax.rsqrt(jnp.mean(z * z, axis=-1, keepdims=True) + RMS_EPS)
        o_ref[...] = x_ref[...] + z * r * g_ref[...]

    row = pl.BlockSpec((tm, D), lambda i: (i, 0))
    sd = lambda dt: jax.ShapeDtypeStruct((S, D), dt)
    return pl.pallas_call(
        body, name=name, grid=(S // tm,),
        out_shape=[sd(f32), sd(f32), sd(f32), sd(bf16), sd(f32), sd(f32)],
        in_specs=[row] * 4 + _row_specs("m_a", tm) + _row_specs("m_b", tm) + _row_specs("m_c", tm)
        + [_full((NDEV, 4, D // NDEV, D)), _full((1, D))],
        out_specs=[row] * 6,
        compiler_params=_cp(("parallel",)))(xin, yah, ybh, ych, *([proj] * (3 * nm)), wsq, g_post)


def loss_fwd(y, target, name):
    tm = min(S, 256)

    def body(y_ref, t_ref, dy_ref, l_ref):
        e = y_ref[...] - t_ref[...]
        dy_ref[...] = e / D

        @pl.when(pl.program_id(0) == 0)
        def _():
            l_ref[...] = jnp.zeros((1, 128), f32)
        l_ref[...] += (0.5 / D) * jnp.sum(e * e)

    row = pl.BlockSpec((tm, D), lambda i: (i, 0))
    return pl.pallas_call(
        body, name=name, grid=(S // tm,),
        out_shape=[jax.ShapeDtypeStruct((S, D), f32), jax.ShapeDtypeStruct((1, 128), f32)],
        in_specs=[row, row], out_specs=[row, _full((1, 128))],
        compiler_params=_cp(("arbitrary",)))(y, target)


def _acc(ref, val):
    @pl.when(pl.program_id(0) == 0)
    def _():
        ref[...] = jnp.zeros(ref.shape, f32)
    ref[...] += val


def _emit_copies(stash, dst, sems, windows):
    return [pltpu.make_async_copy(stash.at[p], dst.at[w], sems.at[p]) for p, w in enumerate(windows)]


def _emit_drain_previous(copies, step):
    @pl.when(step > 0)
    def _():
        for cp in copies:
            cp.wait()


def _emit_start(copies, step, nsteps):
    for cp in copies:
        cp.start()

    @pl.when(step == nsteps - 1)
    def _():
        for cp in copies:
            cp.wait()


def merge_bwd(dout, z, ya, yb, yc, proj, wsq, g_post, name):
    tm = min(S, 256)
    nm = len(_row_specs("m_a", tm))
    nsteps = S // tm

    def body(*refs):
        do_ref, z_ref, ya_ref, yb_ref, yc_ref = refs[:5]
        ms = [_cat(refs[5 + t * nm:5 + (t + 1) * nm]) for t in range(3)]
        w_ref, g_ref = refs[5 + 3 * nm:7 + 3 * nm]
        dh_refs = refs[7 + 3 * nm:10 + 3 * nm]
        dzb_ref = refs[10 + 3 * nm]
        dyb_refs = refs[11 + 3 * nm:14 + 3 * nm]
        dg_ref = refs[14 + 3 * nm]
        dproj_ref, stash, sems = refs[15 + 3 * nm:]
        i = pl.program_id(0)
        rows = pl.ds(pl.multiple_of(i * tm, tm), tm)
        copies = _emit_copies(stash, dproj_ref, sems, [(rows, pl.ds(_OFF[n][0], D)) for n in ("m_a", "m_b", "m_c")])
        nt = (((1,), (1,)), ((), ()))
        z, dout = z_ref[...], do_ref[...]
        r = lax.rsqrt(jnp.mean(z * z, axis=-1, keepdims=True) + RMS_EPS)
        zh = z * r
        _acc(dg_ref, jnp.sum(dout * zh, axis=0, keepdims=True))
        dzh = dout * g_ref[...]
        dz = (r * (dzh - zh * jnp.mean(dzh * zh, axis=-1, keepdims=True))).astype(bf16)
        dzb_ref[...] = dz
        dy = lax.dot_general(dz, _wmat(w_ref, 3), nt, preferred_element_type=f32)
        dms = []
        for t, yt_ref in enumerate((ya_ref, yb_ref, yc_ref)):
            sg = _sig(ms[t])
            dyt = (dy * sg).astype(bf16)
            dyb_refs[t][...] = dyt
            dms.append((dy * yt_ref[...] * sg * (1.0 - sg)).astype(bf16))
            dh_refs[t][...] = lax.dot_general(dyt, _wmat(w_ref, t), nt, preferred_element_type=f32)
        _emit_drain_previous(copies, i)
        for t in range(3):
            stash[t] = dms[t]
        _emit_start(copies, i, nsteps)

    row = pl.BlockSpec((tm, D), lambda i: (i, 0))
    sd = lambda dt: jax.ShapeDtypeStruct((S, D), dt)
    return pl.pallas_call(
        body, name=name, grid=(nsteps,),
        out_shape=[sd(f32)] * 3 + [sd(bf16)] * 4 + [jax.ShapeDtypeStruct((1, D), f32), jax.ShapeDtypeStruct((S, P), bf16)],
        in_specs=[row] * 5 + _row_specs("m_a", tm) + _row_specs("m_b", tm) + _row_specs("m_c", tm)
        + [_full((NDEV, 4, D // NDEV, D)), _full((1, D))],
        out_specs=[row] * 7 + [_full((1, D)), pl.BlockSpec(memory_space=pl.ANY)],
        scratch_shapes=[pltpu.VMEM((3, tm, D), bf16), pltpu.SemaphoreType.DMA((3,))],
        compiler_params=_cp(("arbitrary",)))(dout, z, ya, yb, yc, *([proj] * (3 * nm)), wsq, g_post)


def tn_matmul(a, b, name):
    m, n = a.shape[1], b.shape[1]
    tmm = min(m, 512)

    def body(a_ref, b_ref, o_ref):
        o_ref[...] = lax.dot_general(a_ref[...], b_ref[...], (((0,), (0,)), ((), ())), preferred_element_type=f32).astype(bf16)

    return pl.pallas_call(
        body, name=name, grid=(m // tmm,), out_shape=jax.ShapeDtypeStruct((m, n), bf16),
        in_specs=[pl.BlockSpec((S, tmm), lambda i: (0, i)), _full((S, n))],
        out_specs=pl.BlockSpec((tmm, n), lambda i: (i, 0)),
        compiler_params=_cp(("parallel",)))(a, b)


def dwin_parts(h, dproj, name):
    tn = 1280

    def body(d_ref, h_ref, o_ref):
        o_ref[...] = lax.dot_general(d_ref[...], h_ref[...], (((0,), (0,)), ((), ())), preferred_element_type=f32).astype(bf16)

    return pl.pallas_call(
        body, name=name, grid=(P // tn,), out_shape=jax.ShapeDtypeStruct((P, D), bf16),
        in_specs=[pl.BlockSpec((S, tn), lambda j: (0, j)), _full((S, D))],
        out_specs=pl.BlockSpec((tn, D), lambda j: (j, 0)),
        compiler_params=_cp(("parallel",)))(dproj, h)


def _chan_windows(names, j):
    return [(slice(None), pl.ds(pl.multiple_of(_OFF[n][0] + j * CT, CT), CT)) for n in names]


def brancha_bwd(dyah, proj, convw, dproj, name):
    nsteps = D // CT

    def body(d_ref, ab, ac, ax, ag, w_ref, _, dw_ref, dproj_ref, padp, padt, accw, stash, sems):
        j = pl.program_id(0)
        copies = _emit_copies(stash, dproj_ref, sems, _chan_windows(("a_b", "a_c", "a_x", "a_g"), j))
        _fill_pad(padp, 8, lambda rows: ac[rows, :] * ax[rows, :])
        _fill_pad(padt, 8, lambda rows: d_ref[rows, :] * ab[rows, :] * _silu(ag[rows, :]))
        accw[...] = jnp.zeros(accw.shape, f32)
        w = [w_ref[0, k:k + 1, :] for k in range(CA_W)]
        _emit_drain_previous(copies, j)

        def step(i, carry):
            base = pl.multiple_of(i * RC, RC)
            rows = pl.ds(base, RC)
            ps = [padp[pl.ds(base + 7 + k, RC), :] for k in range(CA_W)]
            t = sum(w[k] * ps[k] for k in range(CA_W))
            dp = sum(w[k] * padt[pl.ds(base + 9 - k, RC), :] for k in range(CA_W))
            d, a_b, a_g = d_ref[rows, :], ab[rows, :], ag[rows, :]
            stash[0, rows, :] = (d * t * _silu(a_g)).astype(bf16)
            stash[1, rows, :] = (dp * ax[rows, :]).astype(bf16)
            stash[2, rows, :] = (dp * ac[rows, :]).astype(bf16)
            stash[3, rows, :] = (d * a_b * t * _dsilu(a_g)).astype(bf16)
            dt = padt[pl.ds(base + 8, RC), :]
            for k in range(CA_W):
                accw[8 * k:8 * k + 8, :] += jnp.sum((dt * ps[k]).reshape(RC // 8, 8, CT), axis=0)
            return carry
        lax.fori_loop(0, S // RC, step, 0)
        _emit_start(copies, j, nsteps)
        dw_ref[0] = jnp.zeros((8, CT), f32)
        for k in range(CA_W):
            dw_ref[0, k:k + 1, :] = jnp.sum(accw[8 * k:8 * k + 8, :], axis=0, keepdims=True)

    tile = pl.BlockSpec((S, CT), lambda j: (0, j))
    anyspec = pl.BlockSpec(memory_space=pl.ANY)
    return pl.pallas_call(
        body, name=name, grid=(nsteps,),
        out_shape=[jax.ShapeDtypeStruct((NDEV, 8, CT), f32), jax.ShapeDtypeStruct((S, P), bf16)],
        in_specs=[tile, _chan_spec("a_b"), _chan_spec("a_c"), _chan_spec("a_x"), _chan_spec("a_g"),
                  pl.BlockSpec((1, 40, CT), lambda j: (j, 0, 0)), anyspec],
        out_specs=[pl.BlockSpec((1, 8, CT), lambda j: (j, 0, 0)), anyspec],
        input_output_aliases={6: 1},
        scratch_shapes=[pltpu.VMEM((S + 16, CT), f32), pltpu.VMEM((S + 16, CT), f32), pltpu.VMEM((8 * CA_W, CT), f32),
                        pltpu.VMEM((4, S, CT), bf16), pltpu.SemaphoreType.DMA((4,))],
        compiler_params=_cp(("arbitrary",)))(dyah, proj, proj, proj, proj, convw, dproj)


def branchc2_bwd(dych, u1, proj, lng, lnb, dproj, name):
    tm = min(S, 256)
    ncg = len(_row_specs("c_g", tm))
    nsteps = S // tm

    def body(*refs):
        d_ref, u_ref = refs[:2]
        cg = _cat(refs[2:2 + ncg])
        g_ref, b_ref, _, du_ref, dlg_ref, dlb_ref, dcb_ref, dproj_ref, stash, sems = refs[2 + ncg:]
        i = pl.program_id(0)
        copies = _emit_copies(stash, dproj_ref, sems, [(pl.ds(pl.multiple_of(i * tm, tm), tm), pl.ds(_OFF["c_g"][0], D))])
        d = d_ref[...]
        xh, rstd = _ln_parts(u_ref[...])
        ln = xh * g_ref[...] + b_ref[...]
        _emit_drain_previous(copies, i)
        stash[0] = (d * _silu(ln) * _dsilu(cg)).astype(bf16)
        _emit_start(copies, i, nsteps)
        dln = d * _silu(cg) * _dsilu(ln)
        _acc(dlg_ref, jnp.sum(dln * xh, axis=0, keepdims=True))
        _acc(dlb_ref, jnp.sum(dln, axis=0, keepdims=True))
        dxh = dln * g_ref[...]
        du = rstd * (dxh - jnp.mean(dxh, axis=-1, keepdims=True) - xh * jnp.mean(dxh * xh, axis=-1, keepdims=True))
        du_ref[...] = du
        _acc(dcb_ref, jnp.sum(du, axis=0, keepdims=True))

    row = pl.BlockSpec((tm, D), lambda i: (i, 0))
    vec = jax.ShapeDtypeStruct((1, D), f32)
    anyspec = pl.BlockSpec(memory_space=pl.ANY)
    return pl.pallas_call(
        body, name=name, grid=(nsteps,),
        out_shape=[jax.ShapeDtypeStruct((S, D), f32), vec, vec, vec, jax.ShapeDtypeStruct((S, P), bf16)],
        in_specs=[row, row] + _row_specs("c_g", tm) + [_full((1, D)), _full((1, D)), anyspec],
        out_specs=[row, _full((1, D)), _full((1, D)), _full((1, D)), anyspec],
        input_output_aliases={4 + ncg: 4},
        scratch_shapes=[pltpu.VMEM((1, tm, D), bf16), pltpu.SemaphoreType.DMA((1,))],
        compiler_params=_cp(("arbitrary",)))(dych, u1, *([proj] * ncg), lng, lnb, dproj)


def branchc1_bwd(du1, proj, convw, dproj, name):
    nsteps = D // CT

    def body(d_ref, cu, cv, w_ref, _, dw_ref, dproj_ref, padu, padd, accw, stash, sems):
        j = pl.program_id(0)
        copies = _emit_copies(stash, dproj_ref, sems, _chan_windows(("c_u", "c_v"), j))
        _fill_pad(padu, 16, lambda rows: cu[rows, :] * _sig(cv[rows, :]))
        _fill_pad(padd, 16, lambda rows: d_ref[rows, :])
        accw[...] = jnp.zeros(accw.shape, f32)
        _emit_drain_previous(copies, j)

        def step(i, carry):
            base = pl.multiple_of(i * RC, RC)
            rows = pl.ds(base, RC)
            d = d_ref[rows, :]
            du0 = jnp.zeros((RC, CT), f32)
            for k in range(CC_W):
                du0 = du0 + w_ref[0, 8 + k:9 + k, :] * padd[pl.ds(base + 31 - k, RC), :]
                accw[8 * k:8 * k + 8, :] += jnp.sum((d * padu[pl.ds(base + k + 1, RC), :]).reshape(RC // 8, 8, CT), axis=0)
            sg = _sig(cv[rows, :])
            stash[0, rows, :] = (du0 * sg).astype(bf16)
            stash[1, rows, :] = (du0 * cu[rows, :] * sg * (1.0 - sg)).astype(bf16)
            return carry
        lax.fori_loop(0, S // RC, step, 0)
        _emit_start(copies, j, nsteps)
        dw_ref[0] = jnp.zeros((32, CT), f32)
        for k in range(CC_W):
            dw_ref[0, k:k + 1, :] = jnp.sum(accw[8 * k:8 * k + 8, :], axis=0, keepdims=True)

    tile = pl.BlockSpec((S, CT), lambda j: (0, j))
    anyspec = pl.BlockSpec(memory_space=pl.ANY)
    return pl.pallas_call(
        body, name=name, grid=(nsteps,),
        out_shape=[jax.ShapeDtypeStruct((NDEV, 32, CT), f32), jax.ShapeDtypeStruct((S, P), bf16)],
        in_specs=[tile, _chan_spec("c_u"), _chan_spec("c_v"), pl.BlockSpec((1, 40, CT), lambda j: (j, 0, 0)), anyspec],
        out_specs=[pl.BlockSpec((1, 32, CT), lambda j: (j, 0, 0)), anyspec],
        input_output_aliases={4: 1},
        scratch_shapes=[pltpu.VMEM((S + 32, CT), f32), pltpu.VMEM((S + 32, CT), f32), pltpu.VMEM((8 * 32, CT), f32),
                        pltpu.VMEM((2, S, CT), bf16), pltpu.SemaphoreType.DMA((2,))],
        compiler_params=_cp(("arbitrary",)))(du1, proj, proj, convw, dproj)


def attn_bwd(dybh, o, qh, kh, vh, proj, dproj, name):
    tq = min(S, 512)
    bg_blk = _OFF["b_g"][0] // GW

    def body(d_ref, o_ref, q_ref, k_ref, v_ref, bg_ref, _, dq_ref, dk_ref, dv_ref, dbg_ref):
        @pl.when(pl.program_id(1) == 0)
        def _():
            dk_ref[...] = jnp.zeros(dk_ref.shape, f32)
            dv_ref[...] = jnp.zeros(dv_ref.shape, f32)
        k, v = k_ref[...], v_ref[...]
        tn = (((0,), (0,)), ((), ()))
        dk_acc = jnp.zeros((S, HD), f32)
        dv_acc = jnp.zeros((S, HD), f32)
        for g in range(G):
            cols = slice(g * HD, (g + 1) * HD)
            d, bg, q, o = d_ref[:, cols], bg_ref[:, cols], q_ref[:, cols], o_ref[:, cols]
            dbg_ref[:, cols] = (d * o * _dsilu(bg)).astype(bf16)
            do = d * _silu(bg)
            p, rl = _softmax_rows(q, k)
            dv_acc = dv_acc + lax.dot_general(p.astype(bf16), (do * rl).astype(bf16), tn, preferred_element_type=f32)
            dp = lax.dot_general(do.astype(bf16), v, (((1,), (1,)), ((), ())), preferred_element_type=f32)
            delta = jnp.sum(do * o, axis=-1, keepdims=True)
            ds = (p * (dp - delta)).astype(bf16)
            rs_ = rl * (HD ** -0.5)
            dq_ref[:, cols] = jnp.dot(ds, k, preferred_element_type=f32) * rs_
            dk_acc = dk_acc + lax.dot_general(ds, (q.astype(f32) * rs_).astype(bf16), tn, preferred_element_type=f32)
        dk_ref[...] += dk_acc
        dv_ref[...] += dv_acc

    grp = pl.BlockSpec((tq, GW), lambda kv, i: (i, kv))
    kvs = pl.BlockSpec((S, HD), lambda kv, i: (0, kv))
    return pl.pallas_call(
        body, name=name, grid=(NKV, S // tq),
        out_shape=[jax.ShapeDtypeStruct((S, D), f32), jax.ShapeDtypeStruct((S, WKV), f32),
                   jax.ShapeDtypeStruct((S, WKV), f32), jax.ShapeDtypeStruct((S, P), bf16)],
        in_specs=[grp, grp, grp, kvs, kvs, pl.BlockSpec((tq, GW), lambda kv, i: (i, bg_blk + kv)),
                  pl.BlockSpec(memory_space=pl.ANY)],
        out_specs=[grp, kvs, kvs, pl.BlockSpec((tq, GW), lambda kv, i: (i, bg_blk + kv))],
        input_output_aliases={6: 3},
        compiler_params=_cp(("parallel", "arbitrary")))(dybh, o, qh, kh, vh, proj, dproj)


def qkv_bwd(dqh, dkh, dvh, proj, qn, kn, cos, sin, dproj, name):
    tm = min(S, 256)
    nq, nk = len(_row_specs("q", tm)), len(_row_specs("k", tm))
    nsteps = S // tm
    wq = D + 2 * WKV

    def body(*refs):
        dqh_ref, dkh_ref, dvh_ref = refs[:3]
        q = _cat(refs[3:3 + nq])
        k = _cat(refs[3 + nq:3 + nq + nk])
        qn_ref, kn_ref, cos_ref, sin_ref, _, dqn_ref, dkn_ref, dproj_ref, stash, sems = refs[3 + nq + nk:]
        i = pl.program_id(0)
        copies = _emit_copies(stash, dproj_ref, sems, [(pl.ds(pl.multiple_of(i * tm, tm), tm), pl.ds(_OFF["q"][0], wq))])
        cos, sin = cos_ref[...], sin_ref[...]
        _emit_drain_previous(copies, i)

        def heads(xx, dd, gn, col0, dgn_ref, n):
            dg = jnp.zeros((1, HD), f32)
            for h in range(n):
                xh = xx[:, h * HD:(h + 1) * HD]
                dh = dd[:, h * HD:(h + 1) * HD]
                r = lax.rsqrt(jnp.mean(xh * xh, axis=-1, keepdims=True) + RMS_EPS)
                xn = xh * r
                dy = dh * cos + _swap32(dh * sin)
                dg = dg + jnp.sum(dy * xn, axis=0, keepdims=True)
                dxn = dy * gn
                stash[0, :, col0 + h * HD:col0 + (h + 1) * HD] = (
                    r * (dxn - xn * jnp.mean(dxn * xn, axis=-1, keepdims=True))).astype(bf16)
            _acc(dgn_ref, dg)
        heads(q, dqh_ref[...], qn_ref[...], 0, dqn_ref, NQ)
        heads(k, dkh_ref[...], kn_ref[...], D, dkn_ref, NKV)
        stash[0, :, D + WKV:wq] = dvh_ref[...].astype(bf16)
        _emit_start(copies, i, nsteps)

    row = lambda w: pl.BlockSpec((tm, w), lambda i: (i, 0))
    vec = jax.ShapeDtypeStruct((1, HD), f32)
    anyspec = pl.BlockSpec(memory_space=pl.ANY)
    return pl.pallas_call(
        body, name=name, grid=(nsteps,),
        out_shape=[vec, vec, jax.ShapeDtypeStruct((S, P), bf16)],
        in_specs=[row(D), row(WKV), row(WKV)] + _row_specs("q", tm) + _row_specs("k", tm)
        + [_full((1, HD)), _full((1, HD)), row(HD), row(HD), anyspec],
        out_specs=[_full((1, HD)), _full((1, HD)), anyspec],
        input_output_aliases={7 + nq + nk: 2},
        scratch_shapes=[pltpu.VMEM((1, tm, wq), bf16), pltpu.SemaphoreType.DMA((1,))],
        compiler_params=_cp(("arbitrary",)))(dqh, dkh, dvh, *([proj] * (nq + nk)), qn, kn, cos, sin, dproj)


def dh_bwd(dproj, wfull, xin, dout, g_pre, name):
    tm, tk = min(S, 1024), 2560
    nk = P // tk

    def body(d_ref, w_ref, x_ref, do_ref, g_ref, dx_ref, dg_ref, acc):
        kk = pl.program_id(1)

        @pl.when(kk == 0)
        def _():
            acc[...] = jnp.zeros(acc.shape, f32)
        acc[...] += jnp.dot(d_ref[...], w_ref[...], preferred_element_type=f32)

        @pl.when((kk == 0) & (pl.program_id(0) == 0))
        def _():
            dg_ref[...] = jnp.zeros(dg_ref.shape, f32)

        @pl.when(kk == nk - 1)
        def _():
            x, dh = x_ref[...], acc[...]
            r = lax.rsqrt(jnp.mean(x * x, axis=-1, keepdims=True) + RMS_EPS)
            xn = x * r
            dg_ref[...] += jnp.sum(dh * xn, axis=0, keepdims=True)
            dxn = dh * g_ref[...]
            dx_ref[...] = do_ref[...] + r * (dxn - xn * jnp.mean(dxn * xn, axis=-1, keepdims=True))

    row = pl.BlockSpec((tm, D), lambda i, k: (i, 0))
    return pl.pallas_call(
        body, name=name, grid=(S // tm, nk),
        out_shape=[jax.ShapeDtypeStruct((S, D), f32), jax.ShapeDtypeStruct((1, D), f32)],
        in_specs=[pl.BlockSpec((tm, tk), lambda i, k: (i, k)), pl.BlockSpec((tk, D), lambda i, k: (k, 0)), row, row, _full((1, D))],
        out_specs=[row, _full((1, D))],
        scratch_shapes=[pltpu.VMEM((tm, D), f32)],
        compiler_params=_cp(("arbitrary", "arbitrary")))(dproj, wfull, xin, dout, g_pre)


def adam_update(parts, own, me, w, m, v, l, acc, name):
    lw, r, c = w.shape
    tr = _row_tile(r)
    nslots = parts.shape[0]

    def body(me_ref, p_ref, own_ref, w_ref, m_ref, v_ref, *rest):
        g_ref, d_ref, nm_ref, nv_ref = rest[-4:]
        g = None
        for s in range(nslots):
            part = jnp.where(me_ref[0] == s, own_ref[0], p_ref[s]).astype(f32)
            g = part if g is None else g + part
        nm = ADAM_B1 * m_ref[0] + (1.0 - ADAM_B1) * g
        nv = ADAM_B2 * v_ref[0] + (1.0 - ADAM_B2) * (g * g)
        m_hat = nm / (1.0 - ADAM_B1 ** ADAM_STEP)
        v_hat = nv / (1.0 - ADAM_B2 ** ADAM_STEP)
        g_ref[0] = g
        d_ref[0] = -ADAM_LR * (m_hat / (jnp.sqrt(v_hat) + ADAM_EPS) + ADAM_WD * w_ref[0])
        nm_ref[0] = nm
        nv_ref[0] = nv

    blk = pl.BlockSpec((1, tr, c), lambda i, me_ref: (l, i, 0))
    sd = jax.ShapeDtypeStruct((lw, r, c), f32)
    extra = [] if acc is None else list(acc)
    return pl.pallas_call(
        body, name=name, out_shape=[sd] * 4,
        grid_spec=pltpu.PrefetchScalarGridSpec(
            num_scalar_prefetch=1, grid=(r // tr,),
            in_specs=[pl.BlockSpec((nslots, tr, c), lambda i, me_ref: (0, i, 0)),
                      pl.BlockSpec((1, tr, c), lambda i, me_ref: (me_ref[0], i, 0)), blk, blk, blk]
            + [pl.BlockSpec(memory_space=pl.ANY)] * len(extra),
            out_specs=[blk] * 4),
        input_output_aliases={6 + t: t for t in range(len(extra))},
        compiler_params=_cp(("parallel",)))(me, parts, own, w, m, v, *extra)


def _rope_tables():
    t = jnp.arange(S)
    rows, cols = (t // GRID_W).astype(f32), (t % GRID_W).astype(f32)
    nf = HD // 4
    inv = ROPE_THETA ** (-jnp.arange(nf, dtype=f32) / nf)
    ar, ac = rows[:, None] * inv, cols[:, None] * inv
    cos = jnp.concatenate([jnp.cos(ar), jnp.cos(ar), jnp.cos(ac), jnp.cos(ac)], axis=1)
    sin = jnp.concatenate([-jnp.sin(ar), jnp.sin(ar), -jnp.sin(ac), jnp.sin(ac)], axis=1)
    return cos, sin


def _pack_conv(ca, cc):
    z = lambda n: jnp.zeros((L, n, CT), f32)
    return jnp.concatenate([ca, z(5), cc, z(1)], axis=1)


def _pack_small(npre, npost, ccb, lng, lnb, qn, kn):
    wide = lambda a: jnp.pad(a, ((0, 0), (0, D - HD)))
    return jnp.stack([npre, npost, ccb, lng, lnb, wide(qn), wide(kn), jnp.zeros((L, D), f32)], axis=1).reshape(L * 8, D)


def kernel(x, norm_pre, norm_post, w_in, conv_a_w, q_norm, k_norm, conv_c_w, conv_c_b, ln_c_g, ln_c_b, w_out_a, w_out_b, w_out_c, w_o, loss_target, m_norm_pre, m_norm_post, m_w_in, m_conv_a_w, m_q_norm, m_k_norm, m_conv_c_w, m_conv_c_b, m_ln_c_g, m_ln_c_b, m_w_out_a, m_w_out_b, m_w_out_c, m_w_o, v_norm_pre, v_norm_post, v_w_in, v_conv_a_w, v_q_norm, v_k_norm, v_conv_c_w, v_conv_c_b, v_ln_c_g, v_ln_c_b, v_w_out_a, v_w_out_b, v_w_out_c, v_w_o):
    cos, sin = _rope_tables()
    rs = D // NDEV
    stack_sq = lambda a, b, c, d: jnp.stack([a, b, c, d], axis=1)
    wsq32 = stack_sq(w_out_a, w_out_b, w_out_c, w_o)
    conv_pack = _pack_conv(conv_a_w, conv_c_w)
    vec = lambda a, l: a[l][None, :]
    me = (4 * lax.axis_index("x") + 2 * lax.axis_index("y") + lax.axis_index("c")).astype(jnp.int32).reshape(1)

    def gather_start(l, after):
        return split_start(staged[l], _gather_copies, 12, f"ag_start{l}", after=after)

    def forward_start(l, after):
        s_sems, r_sems, bufs, _ = gathers[l]
        bufs = split_wait(s_sems, r_sems, bufs, _gather_copies, after, f"ag_wait{l}")
        fw = split_start(bufs, _forward_copies, 9, f"ag_fwd_start{l}")
        if l + 1 < L:
            gathers[l + 1] = gather_start(l + 1, fw[3])
            return fw, gathers[l + 1][3]
        return fw, fw[3]

    def forward_wait(fw, after, l):
        s_sems, r_sems, bufs, _ = fw
        return split_wait(s_sems, r_sems, bufs, _forward_copies, after, f"ag_fwd_wait{l}")

    wt, m_wt, v_wt = (jnp.swapaxes(a, 1, 2) for a in (w_in, m_w_in, v_w_in))
    xs, saved = x.reshape(S, D), []
    stage = lambda l, after: stage_shards(wt[l], wsq32[l].reshape(4 * rs, D), conv_pack[l], me, f"stage{l}", after)
    staged = [stage(0, None)]
    gathers = [gather_start(0, None)] + [None] * (L - 1)
    staged += [stage(l, gathers[0][3]) for l in range(1, L)]
    fw, issued = forward_start(0, staged[L - 1][0] if L > 1 else xs)
    wg, wsq, convw = forward_wait(fw, issued, 0)
    for l in range(L):
        wsq = wsq.reshape(NDEV, 4, rs, D)
        wfull = wg.reshape(P, D)
        proj, h = proj_fwd(xs, vec(norm_pre, l), wfull, f"proj{l}")
        yah = brancha_fwd(proj, convw, f"bra{l}")
        u1 = branchc1_fwd(proj, convw, vec(conv_c_b, l), f"brc1_{l}")
        qh, kh, vh = qkv_fwd(proj, vec(q_norm, l), vec(k_norm, l), cos, sin, f"qkv{l}")
        o, ybh = attn_fwd(qh, kh, vh, proj, f"attn{l}")
        ln_g = vec(ln_c_g, l)
        if l + 1 < L:
            fw, issued = forward_start(l + 1, o)
            ln_g = ln_g + issued[0, 0]
        ych = branchc2_fwd(u1, proj, ln_g, vec(ln_c_b, l), f"brc2_{l}")
        ya, yb, yc, y16, z, xo = merge_fwd(xs, yah, ybh, ych, proj, wsq, vec(norm_post, l), f"merge{l}")
        saved.append(dict(x=xs, wfull=wfull, wsq=wsq, convw=convw, proj=proj, h=h, yah=yah, ybh=ybh, ych=ych, u1=u1,
                          qh=qh, kh=kh, vh=vh, o=o, ya=ya, yb=yb, yc=yc, y16=y16, z=z))
        xs = xo
        if l + 1 < L:
            wg, wsq, convw = forward_wait(fw, xs, l + 1)
    dx, loss_part = loss_fwd(xs, loss_target.reshape(S, D), "loss")
    loss = lax.psum(loss_part[0, 0], ("x", "y", "c"))

    acc = dict(win=None, sq=None, conv=None)
    small_parts = [None] * L
    msq32 = stack_sq(m_w_out_a, m_w_out_b, m_w_out_c, m_w_o)
    vsq32 = stack_sq(v_w_out_a, v_w_out_b, v_w_out_c, v_w_o)
    mconv, vconv = _pack_conv(m_conv_a_w, m_conv_c_w), _pack_conv(v_conv_a_w, v_conv_c_w)

    def scatter_start(parts, name, after=None):
        bufs = parts + [lax.empty(p.shape, p.dtype) for p in parts]
        return split_start(bufs, _scatter_copies(len(parts)), 7 * len(parts), name, after=after)

    def finish(l, started, after):
        (s1, r1, b1, _), (s2, r2, b2, _) = started
        gsq_own, rsq = split_wait(s1, r1, b1, _scatter_copies(1), after, f"rs_sq_wait{l}")
        gwin_own, gconv_own, rwin, rconv = split_wait(s2, r2, b2, _scatter_copies(2), after, f"rs_win_wait{l}")
        flat = lambda a: a.reshape(a.shape[0], 4 * rs, D)
        acc["win"] = adam_update(rwin, gwin_own, me, wt, m_wt, v_wt, l, acc["win"], f"adam_win{l}")
        acc["sq"] = adam_update(flat(rsq), flat(gsq_own), me, flat(wsq32), flat(msq32), flat(vsq32), l, acc["sq"], f"adam_wsq{l}")
        acc["conv"] = adam_update(rconv, gconv_own, me, conv_pack, mconv, vconv, l, acc["conv"], f"adam_conv{l}")

    pending = [None] * L
    for l in reversed(range(L)):
        sv = saved[l]
        proj = sv["proj"]
        (dyah, dybh, dych, dzb, dyab, dybb, dycb, dgpost, dproj) = merge_bwd(
            dx, sv["z"], sv["ya"], sv["yb"], sv["yc"], proj, sv["wsq"], vec(norm_post, l), f"merge_bwd{l}")
        gsq = [tn_matmul(a, b, f"dwsq{t}_{l}") for t, (a, b) in enumerate(
            ((sv["yah"], dyab), (sv["ybh"], dybb), (sv["ych"], dycb), (sv["y16"], dzb)))]
        gsq_parts = jnp.stack([g.reshape(NDEV, rs, D) for g in gsq], axis=1)
        st1 = scatter_start([gsq_parts], f"rs_sq_start{l}", after=loss.reshape(1, 1) if l == L - 1 else None)
        convw = sv["convw"] + st1[3][0, 0]
        gca, dproj = brancha_bwd(dyah, proj, convw, dproj, f"bra_bwd{l}")
        du1, dlg, dlb, dcb, dproj = branchc2_bwd(dych, sv["u1"], proj, vec(ln_c_g, l), vec(ln_c_b, l), dproj, f"brc2_bwd{l}")
        gcc, dproj = branchc1_bwd(du1, proj, convw, dproj, f"brc1_bwd{l}")
        dqh, dkh, dvh, dproj = attn_bwd(dybh, sv["o"], sv["qh"], sv["kh"], sv["vh"], proj, dproj, f"attn_bwd{l}")
        dqn, dkn, dproj = qkv_bwd(dqh, dkh, dvh, proj, vec(q_norm, l), vec(k_norm, l), cos, sin, dproj, f"qkv_bwd{l}")
        gwin = dwin_parts(sv["h"], dproj, f"dwin{l}").reshape(NDEV, PSH, D)
        gconv = jnp.concatenate([gca, gcc], axis=1)
        if l > 0:
            st2 = scatter_start([gwin, gconv], f"rs_win_start{l}")
            issued = st2[3][0, 0]
        else:
            st2 = scatter_start([gconv], "rs_conv_start0")
            pair = split_start([gwin, lax.empty((NDEV // 2, PSH, D), bf16)], _pair_copies, NDEV // 2, "rs_pair_start0")
            issued = st2[3][0, 0] + pair[3][0, 0]
        dx, dgpre = dh_bwd(dproj, sv["wfull"], sv["x"], dx, vec(norm_pre, l) + issued, f"dh{l}")
        wide = lambda a: jnp.pad(a, ((0, 0), (0, D - HD)))
        small_parts[l] = jnp.concatenate([dgpre, dgpost, dcb, dlg, dlb, wide(dqn), wide(dkn), jnp.zeros((1, D), f32)], axis=0)
        pending[l] = (st1, st2)

    gwin0, pair_land = split_wait(pair[0], pair[1], pair[2], _pair_copies, dx, "rs_pair_wait0")
    summed = pair_sum(gwin0, pair_land, me, "rs_pair_sum0")
    (small_all,) = all_gather([jnp.concatenate(small_parts, axis=0)], "ag_small")
    chip = split_start([summed, lax.empty(summed.shape, bf16)], _chip_copies, NDEV // 2 - 1, "rs_chip_start0", after=small_all)
    for l in reversed(range(1, L)):
        finish(l, pending[l], after=chip[3])
    sm = adam_update(small_all, small_all, me,
                     _pack_small(norm_pre, norm_post, conv_c_b, ln_c_g, ln_c_b, q_norm, k_norm)[None],
                     _pack_small(m_norm_pre, m_norm_post, m_conv_c_b, m_ln_c_g, m_ln_c_b, m_q_norm, m_k_norm)[None],
                     _pack_small(v_norm_pre, v_norm_post, v_conv_c_b, v_ln_c_g, v_ln_c_b, v_q_norm, v_k_norm)[None],
                     0, None, "adam_small")
    (s1, r1, b1, _), (s2, r2, b2, _) = pending[0]
    gsq_own, rsq = split_wait(s1, r1, b1, _scatter_copies(1), sm[0], "rs_sq_wait0")
    gconv_own, rconv = split_wait(s2, r2, b2, _scatter_copies(1), sm[0], "rs_conv_wait0")
    flat = lambda a: a.reshape(a.shape[0], 4 * rs, D)
    acc["sq"] = adam_update(flat(rsq), flat(gsq_own), me, flat(wsq32), flat(msq32), flat(vsq32), 0, acc["sq"], "adam_wsq0")
    acc["conv"] = adam_update(rconv, gconv_own, me, conv_pack, mconv, vconv, 0, acc["conv"], "adam_conv0")
    summed, chip_land = split_wait(chip[0], chip[1], chip[2], _chip_copies, acc["sq"][0], "rs_chip_wait0")
    acc["win"] = adam_update(chip_land, summed, me // 2, wt, m_wt, v_wt, 0, acc["win"], "adam_win0")
    sm = [a.reshape(L, 8, D) for a in sm]
    small_rows = dict(norm_pre=(0, D), norm_post=(1, D), conv_c_b=(2, D), ln_c_g=(3, D), ln_c_b=(4, D), q_norm=(5, HD), k_norm=(6, HD))
    sq_rows = dict(w_out_a=0, w_out_b=1, w_out_c=2, w_o=3)

    order = ["norm_pre", "norm_post", "w_in", "conv_a_w", "q_norm", "k_norm", "conv_c_w", "conv_c_b", "ln_c_g", "ln_c_b",
             "w_out_a", "w_out_b", "w_out_c", "w_o"]
    result = [loss, dx.reshape(1, S, D)]
    for kind in range(4):
        for nme in order:
            if nme in small_rows:
                rw, wd = small_rows[nme]
                result.append(sm[kind][:, rw, :wd])
            elif nme in sq_rows:
                result.append(acc["sq"][kind][:, sq_rows[nme] * rs:(sq_rows[nme] + 1) * rs])
            elif nme == "w_in":
                result.append(jnp.swapaxes(acc["win"][kind], 1, 2))
            elif nme == "conv_a_w":
                result.append(acc["conv"][kind][:, 0:CA_W])
            else:
                result.append(acc["conv"][kind][:, 8:8 + CC_W])
    return tuple(result)
```

```python
import math

import jax
import jax.numpy as jnp
from jax import lax
from jax.experimental import pallas as pl
from jax.experimental.pallas import tpu as pltpu

f32, bf16 = jnp.float32, jnp.bfloat16

D = 1024
S = 2048
L = 4
HD = 128
NQ = D // HD
NKV = NQ // 4
G = NQ // NKV
WKV = NKV * HD
GRID_W = 64
ROPE_THETA = 10000.0
RMS_EPS = 1e-6
LN_EPS = 1e-5
NDEV = 8
CA_W, CC_W = 3, 31
P = 12 * D + 2 * WKV
PSH = P // NDEV
CT = 128
ADAM_LR, ADAM_B1, ADAM_B2, ADAM_EPS, ADAM_WD, ADAM_STEP = 0.001, 0.9, 0.999, 1e-08, 0.01, 10
VMEM_LIMIT = 56 * 1024 * 1024
MESH = pl.DeviceIdType.MESH

_OFF = {}
_o = 0
for _n, _w in (("a_b", D), ("a_c", D), ("a_x", D), ("a_g", D), ("q", D), ("k", WKV), ("v", WKV), ("b_g", D),
               ("c_u", D), ("c_v", D), ("c_g", D), ("m_a", D), ("m_b", D), ("m_c", D)):
    _OFF[_n] = (_o, _w)
    _o += _w
PIECES = tuple(_OFF)


def _cp(sem=None, **kw):
    return pltpu.CompilerParams(dimension_semantics=sem, vmem_limit_bytes=VMEM_LIMIT, **kw)


def _sig(x):
    return 1.0 / (1.0 + jnp.exp(-x))


def _silu(x):
    return x * _sig(x)


def _dsilu(x):
    s = _sig(x)
    return s * (1.0 + x * (1.0 - s))


def _row_specs(name, tm):
    off, w = _OFF[name]
    bw = math.gcd(off, w) if off else w
    return [pl.BlockSpec((tm, bw), (lambda i, *_, b=off // bw + t: (i, b))) for t in range(w // bw)]


def _cat(refs):
    return refs[0][...] if len(refs) == 1 else jnp.concatenate([r[...] for r in refs], axis=1)


def _chan_spec(name):
    off, _ = _OFF[name]
    return pl.BlockSpec((S, CT), lambda j, b=off // CT: (0, b + j))


def _full(shape):
    return pl.BlockSpec(shape, lambda *_: (0,) * len(shape))


def _coords():
    return lax.axis_index("x"), lax.axis_index("y"), lax.axis_index("c")


def all_gather(shards, name):
    n = len(shards)

    def body(*refs):
        ins, outs = refs[:n], refs[n:2 * n]
        send_sems, recv_sems, local_sems = refs[2 * n:]
        x, y, c = _coords()
        me, sibling = (x, y, c), (x, y, 1 - c)
        chips = [(1 - x, y), (x, 1 - y), (1 - x, 1 - y)]

        def slot(a, p):
            return outs[a].at[4 * p[0] + 2 * p[1] + p[2]]

        def copy(a, k, block, to, src=None):
            return pltpu.make_async_remote_copy(
                src_ref=slot(a, block) if src is None else src, dst_ref=slot(a, block),
                send_sem=send_sems.at[7 * a + k], recv_sem=recv_sems.at[7 * a + k], device_id=to, device_id_type=MESH)

        mine = [pltpu.make_async_copy(ins[a], slot(a, me), local_sems.at[a]) for a in range(n)]
        for cp in mine:
            cp.start()
        first = []
        for a in range(n):
            first.append(copy(a, 0, me, sibling, src=ins[a]))
            first += [copy(a, 1 + j, me, (*chip, c), src=ins[a]) for j, chip in enumerate(chips)]
        for cp in first:
            cp.start()
        passed = []
        for j, chip in enumerate(chips):
            for a in range(n):
                copy(a, 1 + j, (*chip, c), me).wait_recv()
                fw = copy(a, 4 + j, (*chip, c), sibling)
                fw.start()
                passed.append(fw)
        for a in range(n):
            copy(a, 0, sibling, me).wait_recv()
            for j, chip in enumerate(chips):
                copy(a, 4 + j, (*chip, 1 - c), me).wait_recv()
        for cp in first + passed:
            cp.wait_send()
        for cp in mine:
            cp.wait()

    anyspec = pl.BlockSpec(memory_space=pl.ANY)
    return pl.pallas_call(
        body, name=name,
        out_shape=[jax.ShapeDtypeStruct((NDEV,) + s.shape, s.dtype) for s in shards],
        in_specs=[anyspec] * n, out_specs=[anyspec] * n,
        scratch_shapes=[pltpu.SemaphoreType.DMA((7 * n,)), pltpu.SemaphoreType.DMA((7 * n,)), pltpu.SemaphoreType.DMA((n,))],
    )(*shards)


_HBM = pl.BlockSpec(memory_space=pltpu.HBM)
_SEM = pl.BlockSpec(memory_space=pltpu.SEMAPHORE)
_EFFECT = pltpu.SideEffectType.DATAFLOW_SIDE_EFFECTING


def split_start(bufs, make_copies, nsem, name, after=None):
    n = len(bufs)
    extra = [] if after is None else [after]

    def body(*refs):
        send_sems, recv_sems = refs[n + len(extra):n + len(extra) + 2]
        for cp in make_copies(refs[:n], send_sems, recv_sems):
            cp.start()
        refs[-1][...] = jnp.zeros((8, 128), f32)

    res = pl.pallas_call(
        body, name=name,
        out_shape=(pltpu.SemaphoreType.DMA((nsem,)), pltpu.SemaphoreType.DMA((nsem,)),
                   *[pltpu.HBM(b.shape, b.dtype) for b in bufs], jax.ShapeDtypeStruct((8, 128), f32)),
        in_specs=[_HBM] * n + [pl.BlockSpec(memory_space=pl.ANY)] * len(extra),
        out_specs=(_SEM, _SEM, *([_HBM] * n), pl.BlockSpec(memory_space=pltpu.VMEM)),
        input_output_aliases={i: 2 + i for i in range(n)},
        compiler_params=pltpu.CompilerParams(has_side_effects=_EFFECT),
    )(*[pltpu.with_memory_space_constraint(b, pltpu.HBM) for b in bufs], *extra)
    return res[0], res[1], list(res[2:2 + n]), res[-1]


def split_wait(send_sems, recv_sems, bufs, make_copies, after, name):
    n = len(bufs)

    def body(*refs):
        for cp in make_copies(refs[:n], refs[n], refs[n + 1]):
            cp.wait_send()
            cp.wait_recv()

    res = pl.pallas_call(
        body, name=name,
        out_shape=tuple(pltpu.HBM(b.shape, b.dtype) for b in bufs),
        in_specs=[_HBM] * n + [_SEM, _SEM, pl.BlockSpec(memory_space=pl.ANY)],
        out_specs=[_HBM] * n,
        input_output_aliases={i: i for i in range(n)},
        compiler_params=pltpu.CompilerParams(has_side_effects=_EFFECT),
    )(*bufs, send_sems, recv_sems, after)
    return list(res)


def _scatter_copies(n):
    def make(refs, send_sems, recv_sems):
        x, y, c = _coords()
        me = 4 * x + 2 * y + c
        copies = []
        for a in range(n):
            for k in range(1, NDEV):
                px = 1 - x if (k >> 2) & 1 else x
                py = 1 - y if (k >> 1) & 1 else y
                pc = 1 - c if k & 1 else c
                copies.append(pltpu.make_async_remote_copy(
                    src_ref=refs[a].at[4 * px + 2 * py + pc], dst_ref=refs[n + a].at[me],
                    send_sem=send_sems.at[7 * a + k - 1], recv_sem=recv_sems.at[7 * a + k - 1],
                    device_id=(px, py, pc), device_id_type=MESH))
        return copies
    return make


def _gather_copies(refs, send_sems, recv_sems):
    x, y, c = _coords()
    me = 4 * x + 2 * y + c
    targets = [(x, y, 1 - c), (1 - x, y, c), (x, 1 - y, c), (1 - x, 1 - y, c)]
    return [pltpu.make_async_remote_copy(
        src_ref=r.at[me], dst_ref=r.at[me], send_sem=send_sems.at[4 * a + k], recv_sem=recv_sems.at[4 * a + k],
        device_id=to, device_id_type=MESH) for a, r in enumerate(refs) for k, to in enumerate(targets)]


def _forward_copies(refs, send_sems, recv_sems):
    x, y, c = _coords()
    chips = [(1 - x, y), (x, 1 - y), (1 - x, 1 - y)]
    return [pltpu.make_async_remote_copy(
        src_ref=r.at[4 * px + 2 * py + c], dst_ref=r.at[4 * px + 2 * py + c], send_sem=send_sems.at[3 * a + j],
        recv_sem=recv_sems.at[3 * a + j], device_id=(x, y, 1 - c), device_id_type=MESH)
        for a, r in enumerate(refs) for j, (px, py) in enumerate(chips)]


def _pair_copies(refs, send_sems, recv_sems):
    x, y, c = _coords()
    parts, land = refs
    return [pltpu.make_async_remote_copy(
        src_ref=parts.at[2 * j + 1 - c], dst_ref=land.at[j], send_sem=send_sems.at[j], recv_sem=recv_sems.at[j],
        device_id=(x, y, 1 - c), device_id_type=MESH) for j in range(NDEV // 2)]


def _chip_copies(refs, send_sems, recv_sems):
    x, y, c = _coords()
    summed, land = refs
    copies = []
    for k in range(1, NDEV // 2):
        px = 1 - x if (k >> 1) & 1 else x
        py = 1 - y if k & 1 else y
        copies.append(pltpu.make_async_remote_copy(
            src_ref=summed.at[2 * px + py], dst_ref=land.at[2 * x + y], send_sem=send_sems.at[k - 1],
            recv_sem=recv_sems.at[k - 1], device_id=(px, py, c), device_id_type=MESH))
    return copies


def pair_sum(parts, land, me, name):
    _, r, c = parts.shape
    tr = _row_tile(r)

    def body(me_ref, a_ref, b_ref, o_ref):
        o_ref[...] = (a_ref[...].astype(f32) + b_ref[...].astype(f32)).astype(bf16)

    blk = pl.BlockSpec((1, tr, c), lambda j, i, m: (j, i, 0))
    return pl.pallas_call(
        body, name=name, out_shape=jax.ShapeDtypeStruct((NDEV // 2, r, c), bf16),
        grid_spec=pltpu.PrefetchScalarGridSpec(
            num_scalar_prefetch=1, grid=(NDEV // 2, r // tr),
            in_specs=[pl.BlockSpec((1, tr, c), lambda j, i, m: (2 * j + m[0] % 2, i, 0)), blk], out_specs=blk),
        compiler_params=_cp(("parallel", "parallel")))(me, parts, land)


def _row_tile(r):
    return r if r <= 256 else max(t for t in (256, 160, 128) if r % t == 0)


def stage_shards(wt_l, wsq_l, conv_l, me, name, after=None):
    outs = []
    extra = [] if after is None else [after]
    for a, dt in ((wt_l, bf16), (wsq_l, bf16), (conv_l, f32)):
        r, c = a.shape
        tr = _row_tile(r)

        def body(me_ref, a_ref, *rest):
            rest[-1][0] = a_ref[...].astype(rest[-1].dtype)

        outs.append(pl.pallas_call(
            body, name=f"{name}_{len(outs)}", out_shape=jax.ShapeDtypeStruct((NDEV, r, c), dt),
            grid_spec=pltpu.PrefetchScalarGridSpec(
                num_scalar_prefetch=1, grid=(r // tr,),
                in_specs=[pl.BlockSpec((tr, c), lambda i, m: (i, 0))] + [pl.BlockSpec(memory_space=pl.ANY)] * len(extra),
                out_specs=pl.BlockSpec((1, tr, c), lambda i, m: (m[0], i, 0))),
            compiler_params=_cp(("arbitrary",)))(me, a, *extra))
    return outs


def proj_fwd(xin, g_pre, wt, name):
    tm, tn = min(S, 1024), 2560

    def body(x_ref, g_ref, w_ref, proj_ref, h_ref, hs):
        @pl.when(pl.program_id(1) == 0)
        def _():
            x = x_ref[...]
            r = lax.rsqrt(jnp.mean(x * x, axis=-1, keepdims=True) + RMS_EPS)
            h = (x * r * g_ref[...]).astype(bf16)
            hs[...] = h
            h_ref[...] = h
        proj_ref[...] = lax.dot_general(hs[...], w_ref[...], (((1,), (1,)), ((), ())), preferred_element_type=f32)

    return pl.pallas_call(
        body, name=name, grid=(S // tm, P // tn),
        out_shape=[jax.ShapeDtypeStruct((S, P), f32), jax.ShapeDtypeStruct((S, D), bf16)],
        in_specs=[pl.BlockSpec((tm, D), lambda i, j: (i, 0)), _full((1, D)), pl.BlockSpec((tn, D), lambda i, j: (j, 0))],
        out_specs=[pl.BlockSpec((tm, tn), lambda i, j: (i, j)), pl.BlockSpec((tm, D), lambda i, j: (i, 0))],
        scratch_shapes=[pltpu.VMEM((tm, D), bf16)],
        compiler_params=_cp(("parallel", "arbitrary")))(xin, g_pre, wt)


RC = 256


def _fill_pad(pad, halo, val_fn):
    pad[0:halo, :] = jnp.zeros((halo, CT), f32)
    pad[S + halo:S + 2 * halo, :] = jnp.zeros((halo, CT), f32)

    def step(i, carry):
        rows = pl.ds(pl.multiple_of(i * RC, RC), RC)
        pad[pl.ds(pl.multiple_of(i * RC, RC) + halo, RC), :] = val_fn(rows)
        return carry
    lax.fori_loop(0, S // RC, step, 0)


def brancha_fwd(proj, convw, name):
    def body(ab, ac, ax, ag, w_ref, o_ref, pad):
        _fill_pad(pad, 8, lambda rows: ac[rows, :] * ax[rows, :])
        w = [w_ref[0, k:k + 1, :] for k in range(CA_W)]

        def step(i, carry):
            base = pl.multiple_of(i * RC, RC)
            rows = pl.ds(base, RC)
            t = sum(w[k] * pad[pl.ds(base + 7 + k, RC), :] for k in range(CA_W))
            o_ref[rows, :] = (ab[rows, :] * t * _silu(ag[rows, :])).astype(bf16)
            return carry
        lax.fori_loop(0, S // RC, step, 0)

    return pl.pallas_call(
        body, name=name, grid=(D // CT,), out_shape=jax.ShapeDtypeStruct((S, D), bf16),
        in_specs=[_chan_spec("a_b"), _chan_spec("a_c"), _chan_spec("a_x"), _chan_spec("a_g"),
                  pl.BlockSpec((1, 40, CT), lambda j: (j, 0, 0))],
        out_specs=pl.BlockSpec((S, CT), lambda j: (0, j)),
        scratch_shapes=[pltpu.VMEM((S + 16, CT), f32)],
        compiler_params=_cp(("parallel",)))(proj, proj, proj, proj, convw)


def branchc1_fwd(proj, convw, cbias, name):
    def body(cu, cv, w_ref, b_ref, o_ref, pad):
        _fill_pad(pad, 16, lambda rows: cu[rows, :] * _sig(cv[rows, :]))

        def step(i, carry):
            base = pl.multiple_of(i * RC, RC)
            acc = jnp.zeros((RC, CT), f32) + b_ref[...]
            for k in range(CC_W):
                acc = acc + w_ref[0, 8 + k:9 + k, :] * pad[pl.ds(base + k + 1, RC), :]
            o_ref[pl.ds(base, RC), :] = acc
            return carry
        lax.fori_loop(0, S // RC, step, 0)

    return pl.pallas_call(
        body, name=name, grid=(D // CT,), out_shape=jax.ShapeDtypeStruct((S, D), f32),
        in_specs=[_chan_spec("c_u"), _chan_spec("c_v"), pl.BlockSpec((1, 40, CT), lambda j: (j, 0, 0)),
                  pl.BlockSpec((1, CT), lambda j: (0, j))],
        out_specs=pl.BlockSpec((S, CT), lambda j: (0, j)),
        scratch_shapes=[pltpu.VMEM((S + 32, CT), f32)],
        compiler_params=_cp(("parallel",)))(proj, proj, convw, cbias)


def _swap32(x):
    lane = lax.broadcasted_iota(jnp.int32, x.shape, 1)
    return jnp.where((lane // 32) % 2 == 1, pltpu.roll(x, 32, 1), pltpu.roll(x, HD - 32, 1))


def _rope(y, cos, sin):
    return y * cos + _swap32(y) * sin


def qkv_fwd(proj, qn, kn, cos, sin, name):
    tm = min(S, 512)
    nq, nk, nv = len(_row_specs("q", tm)), len(_row_specs("k", tm)), len(_row_specs("v", tm))

    def body(*refs):
        q = _cat(refs[:nq])
        k = _cat(refs[nq:nq + nk])
        v = _cat(refs[nq + nk:nq + nk + nv])
        qn_ref, kn_ref, cos_ref, sin_ref, qh_ref, kh_ref, vh_ref = refs[nq + nk + nv:]
        cos, sin = cos_ref[...], sin_ref[...]

        def heads(xx, gn, out_ref, n):
            for h in range(n):
                xh = xx[:, h * HD:(h + 1) * HD]
                r = lax.rsqrt(jnp.mean(xh * xh, axis=-1, keepdims=True) + RMS_EPS)
                out_ref[:, h * HD:(h + 1) * HD] = _rope(xh * r * gn, cos, sin).astype(bf16)
        heads(q, qn_ref[...], qh_ref, NQ)
        heads(k, kn_ref[...], kh_ref, NKV)
        vh_ref[...] = v.astype(bf16)

    row = lambda w: pl.BlockSpec((tm, w), lambda i: (i, 0))
    return pl.pallas_call(
        body, name=name, grid=(S // tm,),
        out_shape=[jax.ShapeDtypeStruct((S, D), bf16), jax.ShapeDtypeStruct((S, WKV), bf16), jax.ShapeDtypeStruct((S, WKV), bf16)],
        in_specs=_row_specs("q", tm) + _row_specs("k", tm) + _row_specs("v", tm) + [_full((1, HD)), _full((1, HD)), row(HD), row(HD)],
        out_specs=[row(D), row(WKV), row(WKV)],
        compiler_params=_cp(("parallel",)))(*([proj] * (nq + nk + nv)), qn, kn, cos, sin)


def _softmax_rows(q, k):
    s = lax.dot_general(q, k, (((1,), (1,)), ((), ())), preferred_element_type=f32)
    p = jnp.exp((s - jnp.max(s, axis=-1, keepdims=True)) * (HD ** -0.5))
    return p, 1.0 / jnp.sum(p, axis=-1, keepdims=True)


GW = G * HD


def attn_fwd(qh, kh, vh, proj, name):
    tq = min(S, 512)
    bg_blk = _OFF["b_g"][0] // GW

    def body(q_ref, k_ref, v_ref, bg_ref, o_ref, y_ref):
        k, v = k_ref[...], v_ref[...]
        for g in range(G):
            cols = slice(g * HD, (g + 1) * HD)
            p, rl = _softmax_rows(q_ref[:, cols], k)
            o = jnp.dot(p.astype(bf16), v, preferred_element_type=f32) * rl
            o_ref[:, cols] = o
            y_ref[:, cols] = (o * _silu(bg_ref[:, cols])).astype(bf16)

    grp = pl.BlockSpec((tq, GW), lambda kv, i: (i, kv))
    kvs = pl.BlockSpec((S, HD), lambda kv, i: (0, kv))
    return pl.pallas_call(
        body, name=name, grid=(NKV, S // tq),
        out_shape=[jax.ShapeDtypeStruct((S, D), f32), jax.ShapeDtypeStruct((S, D), bf16)],
        in_specs=[grp, kvs, kvs, pl.BlockSpec((tq, GW), lambda kv, i: (i, bg_blk + kv))],
        out_specs=[grp, grp],
        compiler_params=_cp(("parallel", "parallel")))(qh, kh, vh, proj)


def _ln_parts(u1):
    mu = jnp.mean(u1, axis=-1, keepdims=True)
    xc = u1 - mu
    rstd = lax.rsqrt(jnp.mean(xc * xc, axis=-1, keepdims=True) + LN_EPS)
    return xc * rstd, rstd


def branchc2_fwd(u1, proj, lng, lnb, name):
    tm = min(S, 512)
    ncg = len(_row_specs("c_g", tm))

    def body(*refs):
        u_ref = refs[0]
        cg = _cat(refs[1:1 + ncg])
        g_ref, b_ref, o_ref = refs[1 + ncg:]
        xh, _ = _ln_parts(u_ref[...])
        o_ref[...] = (_silu(xh * g_ref[...] + b_ref[...]) * _silu(cg)).astype(bf16)

    row = pl.BlockSpec((tm, D), lambda i: (i, 0))
    return pl.pallas_call(
        body, name=name, grid=(S // tm,), out_shape=jax.ShapeDtypeStruct((S, D), bf16),
        in_specs=[row] + _row_specs("c_g", tm) + [_full((1, D)), _full((1, D))], out_specs=row,
        compiler_params=_cp(("parallel",)))(u1, *([proj] * ncg), lng, lnb)


def _wmat(w_ref, kind):
    return w_ref[:, kind].reshape(D, D)


def merge_fwd(xin, yah, ybh, ych, proj, wsq, g_post, name):
    tm = min(S, 512)
    nm = len(_row_specs("m_a", tm))

    def body(*refs):
        x_ref, a_ref, b_ref, c_ref = refs[:4]
        ms = [_cat(refs[4 + t * nm:4 + (t + 1) * nm]) for t in range(3)]
        w_ref, g_ref, ya_ref, yb_ref, yc_ref, y_ref, z_ref, o_ref = refs[4 + 3 * nm:]
        y = jnp.zeros((tm, D), f32)
        for t, (h_ref, out_ref) in enumerate(((a_ref, ya_ref), (b_ref, yb_ref), (c_ref, yc_ref))):
            yt = jnp.dot(h_ref[...], _wmat(w_ref, t), preferred_element_type=f32)
            out_ref[...] = yt
            y = y + _sig(ms[t]) * yt
        yb16 = y.astype(bf16)
        y_ref[...] = yb16
        z = jnp.dot(yb16, _wmat(w_ref, 3), preferred_element_type=f32)
        z_ref[...] = z
        r = lax.rsqrt(jnp.mean(z * z, axis=-1, keepdims=True) + RMS_EPS)
        o_ref[...] = x_ref[...] + z * r * g_ref[...]

    row = pl.BlockSpec((tm, D), lambda i: (i, 0))
    sd = lambda dt: jax.ShapeDtypeStruct((S, D), dt)
    return pl.pallas_call(
        body, name=name, grid=(S // tm,),
        out_shape=[sd(f32), sd(f32), sd(f32), sd(bf16), sd(f32), sd(f32)],
        in_specs=[row] * 4 + _row_specs("m_a", tm) + _row_specs("m_b", tm) + _row_specs("m_c", tm)
        + [pl.BlockSpec((NDEV, 4, D // NDEV, D), lambda *_: (0, 0, 0, 0), pipeline_mode=pl.Buffered(1)), _full((1, D))],
        out_specs=[row] * 6,
        compiler_params=_cp(("parallel",)))(xin, yah, ybh, ych, *([proj] * (3 * nm)), wsq, g_post)


def loss_fwd(y, target, name):
    tm = min(S, 256)

    def body(y_ref, t_ref, dy_ref, l_ref):
        e = y_ref[...] - t_ref[...]
        dy_ref[...] = e / D

        @pl.when(pl.program_id(0) == 0)
        def _():
            l_ref[...] = jnp.zeros((1, 128), f32)
        l_ref[...] += (0.5 / D) * jnp.sum(e * e)

    row = pl.BlockSpec((tm, D), lambda i: (i, 0))
    return pl.pallas_call(
        body, name=name, grid=(S // tm,),
        out_shape=[jax.ShapeDtypeStruct((S, D), f32), jax.ShapeDtypeStruct((1, 128), f32)],
        in_specs=[row, row], out_specs=[row, _full((1, 128))],
        compiler_params=_cp(("arbitrary",)))(y, target)


def _acc(ref, val):
    @pl.when(pl.program_id(0) == 0)
    def _():
        ref[...] = jnp.zeros(ref.shape, f32)
    ref[...] += val


def _emit_copies(stash, dst, sems, windows):
    return [pltpu.make_async_copy(stash.at[p], dst.at[w], sems.at[p]) for p, w in enumerate(windows)]


def _emit_drain_previous(copies, step):
    @pl.when(step > 0)
    def _():
        for cp in copies:
            cp.wait()


def _emit_start(copies, step, nsteps):
    for cp in copies:
        cp.start()

    @pl.when(step == nsteps - 1)
    def _():
        for cp in copies:
            cp.wait()


def merge_bwd(dout, z, ya, yb, yc, proj, wsq, g_post, name):
    tm = min(S, 256)
    nm = len(_row_specs("m_a", tm))
    nsteps = S // tm

    def body(*refs):
        do_ref, z_ref, ya_ref, yb_ref, yc_ref = refs[:5]
        ms = [_cat(refs[5 + t * nm:5 + (t + 1) * nm]) for t in range(3)]
        w_ref, g_ref = refs[5 + 3 * nm:7 + 3 * nm]
        dh_refs = refs[7 + 3 * nm:10 + 3 * nm]
        dzb_ref = refs[10 + 3 * nm]
        dyb_refs = refs[11 + 3 * nm:14 + 3 * nm]
        dg_ref = refs[14 + 3 * nm]
        dproj_ref, stash, sems = refs[15 + 3 * nm:]
        i = pl.program_id(0)
        rows = pl.ds(pl.multiple_of(i * tm, tm), tm)
        copies = _emit_copies(stash, dproj_ref, sems, [(rows, pl.ds(_OFF[n][0], D)) for n in ("m_a", "m_b", "m_c")])
        nt = (((1,), (1,)), ((), ()))
        z, dout = z_ref[...], do_ref[...]
        r = lax.rsqrt(jnp.mean(z * z, axis=-1, keepdims=True) + RMS_EPS)
        zh = z * r
        _acc(dg_ref, jnp.sum(dout * zh, axis=0, keepdims=True))
        dzh = dout * g_ref[...]
        dz = (r * (dzh - zh * jnp.mean(dzh * zh, axis=-1, keepdims=True))).astype(bf16)
        dzb_ref[...] = dz
        dy = lax.dot_general(dz, _wmat(w_ref, 3), nt, preferred_element_type=f32)
        dms = []
        for t, yt_ref in enumerate((ya_ref, yb_ref, yc_ref)):
            sg = _sig(ms[t])
            dyt = (dy * sg).astype(bf16)
            dyb_refs[t][...] = dyt
            dms.append((dy * yt_ref[...] * sg * (1.0 - sg)).astype(bf16))
            dh_refs[t][...] = lax.dot_general(dyt, _wmat(w_ref, t), nt, preferred_element_type=f32)
        _emit_drain_previous(copies, i)
        for t in range(3):
            stash[t] = dms[t]
        _emit_start(copies, i, nsteps)

    row = pl.BlockSpec((tm, D), lambda i: (i, 0))
    sd = lambda dt: jax.ShapeDtypeStruct((S, D), dt)
    return pl.pallas_call(
        body, name=name, grid=(nsteps,),
        out_shape=[sd(f32)] * 3 + [sd(bf16)] * 4 + [jax.ShapeDtypeStruct((1, D), f32), jax.ShapeDtypeStruct((S, P), bf16)],
        in_specs=[row] * 5 + _row_specs("m_a", tm) + _row_specs("m_b", tm) + _row_specs("m_c", tm)
        + [pl.BlockSpec((NDEV, 4, D // NDEV, D), lambda *_: (0, 0, 0, 0), pipeline_mode=pl.Buffered(1)), _full((1, D))],
        out_specs=[row] * 7 + [_full((1, D)), pl.BlockSpec(memory_space=pl.ANY)],
        scratch_shapes=[pltpu.VMEM((3, tm, D), bf16), pltpu.SemaphoreType.DMA((3,))],
        compiler_params=_cp(("arbitrary",)))(dout, z, ya, yb, yc, *([proj] * (3 * nm)), wsq, g_post)


def tn_matmul(a, b, name):
    m, n = a.shape[1], b.shape[1]
    tmm = min(m, 512)

    def body(a_ref, b_ref, o_ref):
        o_ref[...] = lax.dot_general(a_ref[...], b_ref[...], (((0,), (0,)), ((), ())), preferred_element_type=f32).astype(bf16)

    return pl.pallas_call(
        body, name=name, grid=(m // tmm,), out_shape=jax.ShapeDtypeStruct((m, n), bf16),
        in_specs=[pl.BlockSpec((S, tmm), lambda i: (0, i)), _full((S, n))],
        out_specs=pl.BlockSpec((tmm, n), lambda i: (i, 0)),
        compiler_params=_cp(("parallel",)))(a, b)


def dwin_parts(h, dproj, name):
    tn = 1280

    def body(d_ref, h_ref, o_ref):
        o_ref[...] = lax.dot_general(d_ref[...], h_ref[...], (((0,), (0,)), ((), ())), preferred_element_type=f32).astype(bf16)

    return pl.pallas_call(
        body, name=name, grid=(P // tn,), out_shape=jax.ShapeDtypeStruct((P, D), bf16),
        in_specs=[pl.BlockSpec((S, tn), lambda j: (0, j)), _full((S, D))],
        out_specs=pl.BlockSpec((tn, D), lambda j: (j, 0)),
        compiler_params=_cp(("parallel",)))(dproj, h)


def _chan_windows(names, j):
    return [(slice(None), pl.ds(pl.multiple_of(_OFF[n][0] + j * CT, CT), CT)) for n in names]


def brancha_bwd(dyah, proj, convw, dproj, name):
    nsteps = D // CT

    def body(d_ref, ab, ac, ax, ag, w_ref, _, dw_ref, dproj_ref, padp, padt, accw, stash, sems):
        j = pl.program_id(0)
        copies = _emit_copies(stash, dproj_ref, sems, _chan_windows(("a_b", "a_c", "a_x", "a_g"), j))
        _fill_pad(padp, 8, lambda rows: ac[rows, :] * ax[rows, :])
        _fill_pad(padt, 8, lambda rows: d_ref[rows, :] * ab[rows, :] * _silu(ag[rows, :]))
        accw[...] = jnp.zeros(accw.shape, f32)
        w = [w_ref[0, k:k + 1, :] for k in range(CA_W)]
        _emit_drain_previous(copies, j)

        def step(i, carry):
            base = pl.multiple_of(i * RC, RC)
            rows = pl.ds(base, RC)
            ps = [padp[pl.ds(base + 7 + k, RC), :] for k in range(CA_W)]
            t = sum(w[k] * ps[k] for k in range(CA_W))
            dp = sum(w[k] * padt[pl.ds(base + 9 - k, RC), :] for k in range(CA_W))
            d, a_b, a_g = d_ref[rows, :], ab[rows, :], ag[rows, :]
            stash[0, rows, :] = (d * t * _silu(a_g)).astype(bf16)
            stash[1, rows, :] = (dp * ax[rows, :]).astype(bf16)
            stash[2, rows, :] = (dp * ac[rows, :]).astype(bf16)
            stash[3, rows, :] = (d * a_b * t * _dsilu(a_g)).astype(bf16)
            dt = padt[pl.ds(base + 8, RC), :]
            for k in range(CA_W):
                accw[8 * k:8 * k + 8, :] += jnp.sum((dt * ps[k]).reshape(RC // 8, 8, CT), axis=0)
            return carry
        lax.fori_loop(0, S // RC, step, 0)
        _emit_start(copies, j, nsteps)
        dw_ref[0] = jnp.zeros((8, CT), f32)
        for k in range(CA_W):
            dw_ref[0, k:k + 1, :] = jnp.sum(accw[8 * k:8 * k + 8, :], axis=0, keepdims=True)

    tile = pl.BlockSpec((S, CT), lambda j: (0, j))
    anyspec = pl.BlockSpec(memory_space=pl.ANY)
    return pl.pallas_call(
        body, name=name, grid=(nsteps,),
        out_shape=[jax.ShapeDtypeStruct((NDEV, 8, CT), f32), jax.ShapeDtypeStruct((S, P), bf16)],
        in_specs=[tile, _chan_spec("a_b"), _chan_spec("a_c"), _chan_spec("a_x"), _chan_spec("a_g"),
                  pl.BlockSpec((1, 40, CT), lambda j: (j, 0, 0)), anyspec],
        out_specs=[pl.BlockSpec((1, 8, CT), lambda j: (j, 0, 0)), anyspec],
        input_output_aliases={6: 1},
        scratch_shapes=[pltpu.VMEM((S + 16, CT), f32), pltpu.VMEM((S + 16, CT), f32), pltpu.VMEM((8 * CA_W, CT), f32),
                        pltpu.VMEM((4, S, CT), bf16), pltpu.SemaphoreType.DMA((4,))],
        compiler_params=_cp(("arbitrary",)))(dyah, proj, proj, proj, proj, convw, dproj)


def branchc2_bwd(dych, u1, proj, lng, lnb, dproj, name):
    tm = min(S, 512)
    ncg = len(_row_specs("c_g", tm))
    nsteps = S // tm

    def body(*refs):
        d_ref, u_ref = refs[:2]
        cg = _cat(refs[2:2 + ncg])
        g_ref, b_ref, _, du_ref, dlg_ref, dlb_ref, dcb_ref, dproj_ref, stash, sems = refs[2 + ncg:]
        i = pl.program_id(0)
        copies = _emit_copies(stash, dproj_ref, sems, [(pl.ds(pl.multiple_of(i * tm, tm), tm), pl.ds(_OFF["c_g"][0], D))])
        d = d_ref[...]
        xh, rstd = _ln_parts(u_ref[...])
        ln = xh * g_ref[...] + b_ref[...]
        _emit_drain_previous(copies, i)
        stash[0] = (d * _silu(ln) * _dsilu(cg)).astype(bf16)
        _emit_start(copies, i, nsteps)
        dln = d * _silu(cg) * _dsilu(ln)
        _acc(dlg_ref, jnp.sum(dln * xh, axis=0, keepdims=True))
        _acc(dlb_ref, jnp.sum(dln, axis=0, keepdims=True))
        dxh = dln * g_ref[...]
        du = rstd * (dxh - jnp.mean(dxh, axis=-1, keepdims=True) - xh * jnp.mean(dxh * xh, axis=-1, keepdims=True))
        du_ref[...] = du
        _acc(dcb_ref, jnp.sum(du, axis=0, keepdims=True))

    row = pl.BlockSpec((tm, D), lambda i: (i, 0))
    vec = jax.ShapeDtypeStruct((1, D), f32)
    anyspec = pl.BlockSpec(memory_space=pl.ANY)
    return pl.pallas_call(
        body, name=name, grid=(nsteps,),
        out_shape=[jax.ShapeDtypeStruct((S, D), f32), vec, vec, vec, jax.ShapeDtypeStruct((S, P), bf16)],
        in_specs=[row, row] + _row_specs("c_g", tm) + [_full((1, D)), _full((1, D)), anyspec],
        out_specs=[row, _full((1, D)), _full((1, D)), _full((1, D)), anyspec],
        input_output_aliases={4 + ncg: 4},
        scratch_shapes=[pltpu.VMEM((1, tm, D), bf16), pltpu.SemaphoreType.DMA((1,))],
        compiler_params=_cp(("arbitrary",)))(dych, u1, *([proj] * ncg), lng, lnb, dproj)


def branchc1_bwd(du1, proj, convw, dproj, name):
    nsteps = D // CT

    def body(d_ref, cu, cv, w_ref, _, dw_ref, dproj_ref, padu, padd, accw, stash, sems):
        j = pl.program_id(0)
        copies = _emit_copies(stash, dproj_ref, sems, _chan_windows(("c_u", "c_v"), j))
        _fill_pad(padu, 16, lambda rows: cu[rows, :] * _sig(cv[rows, :]))
        _fill_pad(padd, 16, lambda rows: d_ref[rows, :])
        accw[...] = jnp.zeros(accw.shape, f32)
        _emit_drain_previous(copies, j)

        rc = min(S, 128)

        def step(i, carry):
            base = pl.multiple_of(i * rc, rc)
            rows = pl.ds(base, rc)
            d = d_ref[rows, :]
            du0 = jnp.zeros((rc, CT), f32)
            for k in range(CC_W):
                du0 = du0 + w_ref[0, 8 + k:9 + k, :] * padd[pl.ds(base + 31 - k, rc), :]
                accw[8 * k:8 * k + 8, :] += jnp.sum((d * padu[pl.ds(base + k + 1, rc), :]).reshape(rc // 8, 8, CT), axis=0)
            sg = _sig(cv[rows, :])
            stash[0, rows, :] = (du0 * sg).astype(bf16)
            stash[1, rows, :] = (du0 * cu[rows, :] * sg * (1.0 - sg)).astype(bf16)
            return carry
        lax.fori_loop(0, S // rc, step, 0)
        _emit_start(copies, j, nsteps)
        dw_ref[0] = jnp.zeros((32, CT), f32)
        for k in range(CC_W):
            dw_ref[0, k:k + 1, :] = jnp.sum(accw[8 * k:8 * k + 8, :], axis=0, keepdims=True)

    tile = pl.BlockSpec((S, CT), lambda j: (0, j))
    anyspec = pl.BlockSpec(memory_space=pl.ANY)
    return pl.pallas_call(
        body, name=name, grid=(nsteps,),
        out_shape=[jax.ShapeDtypeStruct((NDEV, 32, CT), f32), jax.ShapeDtypeStruct((S, P), bf16)],
        in_specs=[tile, _chan_spec("c_u"), _chan_spec("c_v"), pl.BlockSpec((1, 40, CT), lambda j: (j, 0, 0)), anyspec],
        out_specs=[pl.BlockSpec((1, 32, CT), lambda j: (j, 0, 0)), anyspec],
        input_output_aliases={4: 1},
        scratch_shapes=[pltpu.VMEM((S + 32, CT), f32), pltpu.VMEM((S + 32, CT), f32), pltpu.VMEM((8 * 32, CT), f32),
                        pltpu.VMEM((2, S, CT), bf16), pltpu.SemaphoreType.DMA((2,))],
        compiler_params=_cp(("arbitrary",)))(du1, proj, proj, convw, dproj)


def attn_bwd(dybh, o, qh, kh, vh, proj, dproj, name):
    tq = min(S, 512)
    bg_blk = _OFF["b_g"][0] // GW

    def body(d_ref, o_ref, q_ref, k_ref, v_ref, bg_ref, _, dq_ref, dk_ref, dv_ref, dbg_ref):
        @pl.when(pl.program_id(1) == 0)
        def _():
            dk_ref[...] = jnp.zeros(dk_ref.shape, f32)
            dv_ref[...] = jnp.zeros(dv_ref.shape, f32)
        k, v = k_ref[...], v_ref[...]
        tn = (((0,), (0,)), ((), ()))
        dk_acc = jnp.zeros((S, HD), f32)
        dv_acc = jnp.zeros((S, HD), f32)
        for g in range(G):
            cols = slice(g * HD, (g + 1) * HD)
            d, bg, q, o = d_ref[:, cols], bg_ref[:, cols], q_ref[:, cols], o_ref[:, cols]
            dbg_ref[:, cols] = (d * o * _dsilu(bg)).astype(bf16)
            do = d * _silu(bg)
            p, rl = _softmax_rows(q, k)
            dv_acc = dv_acc + lax.dot_general(p.astype(bf16), (do * rl).astype(bf16), tn, preferred_element_type=f32)
            dp = lax.dot_general(do.astype(bf16), v, (((1,), (1,)), ((), ())), preferred_element_type=f32)
            delta = jnp.sum(do * o, axis=-1, keepdims=True)
            ds = (p * (dp - delta)).astype(bf16)
            rs_ = rl * (HD ** -0.5)
            dq_ref[:, cols] = jnp.dot(ds, k, preferred_element_type=f32) * rs_
            dk_acc = dk_acc + lax.dot_general(ds, (q.astype(f32) * rs_).astype(bf16), tn, preferred_element_type=f32)
        dk_ref[...] += dk_acc
        dv_ref[...] += dv_acc

    grp = pl.BlockSpec((tq, GW), lambda kv, i: (i, kv))
    kvs = pl.BlockSpec((S, HD), lambda kv, i: (0, kv))
    return pl.pallas_call(
        body, name=name, grid=(NKV, S // tq),
        out_shape=[jax.ShapeDtypeStruct((S, D), f32), jax.ShapeDtypeStruct((S, WKV), f32),
                   jax.ShapeDtypeStruct((S, WKV), f32), jax.ShapeDtypeStruct((S, P), bf16)],
        in_specs=[grp, grp, grp, kvs, kvs, pl.BlockSpec((tq, GW), lambda kv, i: (i, bg_blk + kv)),
                  pl.BlockSpec(memory_space=pl.ANY)],
        out_specs=[grp, kvs, kvs, pl.BlockSpec((tq, GW), lambda kv, i: (i, bg_blk + kv))],
        input_output_aliases={6: 3},
        compiler_params=_cp(("parallel", "arbitrary")))(dybh, o, qh, kh, vh, proj, dproj)


def qkv_bwd(dqh, dkh, dvh, proj, qn, kn, cos, sin, dproj, name):
    tm = min(S, 512)
    nq, nk = len(_row_specs("q", tm)), len(_row_specs("k", tm))
    nsteps = S // tm
    wq = D + 2 * WKV

    def body(*refs):
        dqh_ref, dkh_ref, dvh_ref = refs[:3]
        q = _cat(refs[3:3 + nq])
        k = _cat(refs[3 + nq:3 + nq + nk])
        qn_ref, kn_ref, cos_ref, sin_ref, _, dqn_ref, dkn_ref, dproj_ref, stash, sems = refs[3 + nq + nk:]
        i = pl.program_id(0)
        copies = _emit_copies(stash, dproj_ref, sems, [(pl.ds(pl.multiple_of(i * tm, tm), tm), pl.ds(_OFF["q"][0], wq))])
        cos, sin = cos_ref[...], sin_ref[...]
        _emit_drain_previous(copies, i)

        def heads(xx, dd, gn, col0, dgn_ref, n):
            dg = jnp.zeros((1, HD), f32)
            for h in range(n):
                xh = xx[:, h * HD:(h + 1) * HD]
                dh = dd[:, h * HD:(h + 1) * HD]
                r = lax.rsqrt(jnp.mean(xh * xh, axis=-1, keepdims=True) + RMS_EPS)
                xn = xh * r
                dy = dh * cos + _swap32(dh * sin)
                dg = dg + jnp.sum(dy * xn, axis=0, keepdims=True)
                dxn = dy * gn
                stash[0, :, col0 + h * HD:col0 + (h + 1) * HD] = (
                    r * (dxn - xn * jnp.mean(dxn * xn, axis=-1, keepdims=True))).astype(bf16)
            _acc(dgn_ref, dg)
        heads(q, dqh_ref[...], qn_ref[...], 0, dqn_ref, NQ)
        heads(k, dkh_ref[...], kn_ref[...], D, dkn_ref, NKV)
        stash[0, :, D + WKV:wq] = dvh_ref[...].astype(bf16)
        _emit_start(copies, i, nsteps)

    row = lambda w: pl.BlockSpec((tm, w), lambda i: (i, 0))
    vec = jax.ShapeDtypeStruct((1, HD), f32)
    anyspec = pl.BlockSpec(memory_space=pl.ANY)
    return pl.pallas_call(
        body, name=name, grid=(nsteps,),
        out_shape=[vec, vec, jax.ShapeDtypeStruct((S, P), bf16)],
        in_specs=[row(D), row(WKV), row(WKV)] + _row_specs("q", tm) + _row_specs("k", tm)
        + [_full((1, HD)), _full((1, HD)), row(HD), row(HD), anyspec],
        out_specs=[_full((1, HD)), _full((1, HD)), anyspec],
        input_output_aliases={7 + nq + nk: 2},
        scratch_shapes=[pltpu.VMEM((1, tm, wq), bf16), pltpu.SemaphoreType.DMA((1,))],
        compiler_params=_cp(("arbitrary",)))(dqh, dkh, dvh, *([proj] * (nq + nk)), qn, kn, cos, sin, dproj)


def dh_bwd(dproj, wfull, xin, dout, g_pre, name):
    tm, tk = min(S, 1024), 2560
    nk = P // tk

    def body(d_ref, w_ref, x_ref, do_ref, g_ref, dx_ref, dg_ref, acc):
        kk = pl.program_id(1)

        @pl.when(kk == 0)
        def _():
            acc[...] = jnp.zeros(acc.shape, f32)
        acc[...] += jnp.dot(d_ref[...], w_ref[...], preferred_element_type=f32)

        @pl.when((kk == 0) & (pl.program_id(0) == 0))
        def _():
            dg_ref[...] = jnp.zeros(dg_ref.shape, f32)

        @pl.when(kk == nk - 1)
        def _():
            x, dh = x_ref[...], acc[...]
            r = lax.rsqrt(jnp.mean(x * x, axis=-1, keepdims=True) + RMS_EPS)
            xn = x * r
            dg_ref[...] += jnp.sum(dh * xn, axis=0, keepdims=True)
            dxn = dh * g_ref[...]
            dx_ref[...] = do_ref[...] + r * (dxn - xn * jnp.mean(dxn * xn, axis=-1, keepdims=True))

    row = pl.BlockSpec((tm, D), lambda i, k: (i, 0))
    return pl.pallas_call(
        body, name=name, grid=(S // tm, nk),
        out_shape=[jax.ShapeDtypeStruct((S, D), f32), jax.ShapeDtypeStruct((1, D), f32)],
        in_specs=[pl.BlockSpec((tm, tk), lambda i, k: (i, k)), pl.BlockSpec((tk, D), lambda i, k: (k, 0)), row, row, _full((1, D))],
        out_specs=[row, _full((1, D))],
        scratch_shapes=[pltpu.VMEM((tm, D), f32)],
        compiler_params=_cp(("arbitrary", "arbitrary")))(dproj, wfull, xin, dout, g_pre)


def adam_update(parts, own, me, w, m, v, l, acc, name):
    lw, r, c = w.shape
    tr = _row_tile(r)
    nslots = parts.shape[0]

    def body(me_ref, p_ref, own_ref, w_ref, m_ref, v_ref, *rest):
        g_ref, d_ref, nm_ref, nv_ref = rest[-4:]
        g = None
        for s in range(nslots):
            part = jnp.where(me_ref[0] == s, own_ref[0], p_ref[s]).astype(f32)
            g = part if g is None else g + part
        nm = ADAM_B1 * m_ref[0] + (1.0 - ADAM_B1) * g
        nv = ADAM_B2 * v_ref[0] + (1.0 - ADAM_B2) * (g * g)
        m_hat = nm / (1.0 - ADAM_B1 ** ADAM_STEP)
        v_hat = nv / (1.0 - ADAM_B2 ** ADAM_STEP)
        g_ref[0] = g
        d_ref[0] = -ADAM_LR * (m_hat / (jnp.sqrt(v_hat) + ADAM_EPS) + ADAM_WD * w_ref[0])
        nm_ref[0] = nm
        nv_ref[0] = nv

    blk = pl.BlockSpec((1, tr, c), lambda i, me_ref: (l, i, 0))
    sd = jax.ShapeDtypeStruct((lw, r, c), f32)
    extra = [] if acc is None else list(acc)
    return pl.pallas_call(
        body, name=name, out_shape=[sd] * 4,
        grid_spec=pltpu.PrefetchScalarGridSpec(
            num_scalar_prefetch=1, grid=(r // tr,),
            in_specs=[pl.BlockSpec((nslots, tr, c), lambda i, me_ref: (0, i, 0)),
                      pl.BlockSpec((1, tr, c), lambda i, me_ref: (me_ref[0], i, 0)), blk, blk, blk]
            + [pl.BlockSpec(memory_space=pl.ANY)] * len(extra),
            out_specs=[blk] * 4),
        input_output_aliases={6 + t: t for t in range(len(extra))},
        compiler_params=_cp(("parallel",)))(me, parts, own, w, m, v, *extra)


def _rope_tables():
    t = jnp.arange(S)
    rows, cols = (t // GRID_W).astype(f32), (t % GRID_W).astype(f32)
    nf = HD // 4
    inv = ROPE_THETA ** (-jnp.arange(nf, dtype=f32) / nf)
    ar, ac = rows[:, None] * inv, cols[:, None] * inv
    cos = jnp.concatenate([jnp.cos(ar), jnp.cos(ar), jnp.cos(ac), jnp.cos(ac)], axis=1)
    sin = jnp.concatenate([-jnp.sin(ar), jnp.sin(ar), -jnp.sin(ac), jnp.sin(ac)], axis=1)
    return cos, sin


def _pack_conv(ca, cc):
    z = lambda n: jnp.zeros((L, n, CT), f32)
    return jnp.concatenate([ca, z(5), cc, z(1)], axis=1)


def _pack_small(npre, npost, ccb, lng, lnb, qn, kn):
    wide = lambda a: jnp.pad(a, ((0, 0), (0, D - HD)))
    return jnp.stack([npre, npost, ccb, lng, lnb, wide(qn), wide(kn), jnp.zeros((L, D), f32)], axis=1).reshape(L * 8, D)


def kernel(x, norm_pre, norm_post, w_in, conv_a_w, q_norm, k_norm, conv_c_w, conv_c_b, ln_c_g, ln_c_b, w_out_a, w_out_b, w_out_c, w_o, loss_target, m_norm_pre, m_norm_post, m_w_in, m_conv_a_w, m_q_norm, m_k_norm, m_conv_c_w, m_conv_c_b, m_ln_c_g, m_ln_c_b, m_w_out_a, m_w_out_b, m_w_out_c, m_w_o, v_norm_pre, v_norm_post, v_w_in, v_conv_a_w, v_q_norm, v_k_norm, v_conv_c_w, v_conv_c_b, v_ln_c_g, v_ln_c_b, v_w_out_a, v_w_out_b, v_w_out_c, v_w_o):
    cos, sin = _rope_tables()
    rs = D // NDEV
    stack_sq = lambda a, b, c, d: jnp.stack([a, b, c, d], axis=1)
    wsq32 = stack_sq(w_out_a, w_out_b, w_out_c, w_o)
    conv_pack = _pack_conv(conv_a_w, conv_c_w)
    vec = lambda a, l: a[l][None, :]
    me = (4 * lax.axis_index("x") + 2 * lax.axis_index("y") + lax.axis_index("c")).astype(jnp.int32).reshape(1)

    def gather_start(l, after):
        return split_start(staged[l], _gather_copies, 12, f"ag_start{l}", after=after)

    def forward_start(l, after):
        s_sems, r_sems, bufs, _ = gathers[l]
        bufs = split_wait(s_sems, r_sems, bufs, _gather_copies, after, f"ag_wait{l}")
        fw = split_start(bufs, _forward_copies, 9, f"ag_fwd_start{l}")
        if l + 1 < L:
            gathers[l + 1] = gather_start(l + 1, fw[3])
            return fw, gathers[l + 1][3]
        return fw, fw[3]

    def forward_wait(fw, after, l):
        s_sems, r_sems, bufs, _ = fw
        return split_wait(s_sems, r_sems, bufs, _forward_copies, after, f"ag_fwd_wait{l}")

    wt, m_wt, v_wt = (jnp.swapaxes(a, 1, 2) for a in (w_in, m_w_in, v_w_in))
    xs, saved = x.reshape(S, D), []
    stage = lambda l, after: stage_shards(wt[l], wsq32[l].reshape(4 * rs, D), conv_pack[l], me, f"stage{l}", after)
    staged = [stage(0, None)]
    gathers = [gather_start(0, None)] + [None] * (L - 1)
    staged += [stage(l, gathers[0][3]) for l in range(1, L)]
    fw, issued = forward_start(0, staged[L - 1][0] if L > 1 else xs)
    wg, wsq, convw = forward_wait(fw, issued, 0)
    for l in range(L):
        wsq = wsq.reshape(NDEV, 4, rs, D)
        wfull = wg.reshape(P, D)
        proj, h = proj_fwd(xs, vec(norm_pre, l), wfull, f"proj{l}")
        yah = brancha_fwd(proj, convw, f"bra{l}")
        u1 = branchc1_fwd(proj, convw, vec(conv_c_b, l), f"brc1_{l}")
        qh, kh, vh = qkv_fwd(proj, vec(q_norm, l), vec(k_norm, l), cos, sin, f"qkv{l}")
        o, ybh = attn_fwd(qh, kh, vh, proj, f"attn{l}")
        ln_g = vec(ln_c_g, l)
        if l + 1 < L:
            fw, issued = forward_start(l + 1, o)
            ln_g = ln_g + issued[0, 0]
        ych = branchc2_fwd(u1, proj, ln_g, vec(ln_c_b, l), f"brc2_{l}")
        ya, yb, yc, y16, z, xo = merge_fwd(xs, yah, ybh, ych, proj, wsq, vec(norm_post, l), f"merge{l}")
        saved.append(dict(x=xs, wfull=wfull, wsq=wsq, convw=convw, proj=proj, h=h, yah=yah, ybh=ybh, ych=ych, u1=u1,
                          qh=qh, kh=kh, vh=vh, o=o, ya=ya, yb=yb, yc=yc, y16=y16, z=z))
        xs = xo
        if l + 1 < L:
            wg, wsq, convw = forward_wait(fw, xs, l + 1)
    dx, loss_part = loss_fwd(xs, loss_target.reshape(S, D), "loss")
    loss = lax.psum(loss_part[0, 0], ("x", "y", "c"))

    acc = dict(win=None, sq=None, conv=None)
    small_parts = [None] * L
    msq32 = stack_sq(m_w_out_a, m_w_out_b, m_w_out_c, m_w_o)
    vsq32 = stack_sq(v_w_out_a, v_w_out_b, v_w_out_c, v_w_o)
    mconv, vconv = _pack_conv(m_conv_a_w, m_conv_c_w), _pack_conv(v_conv_a_w, v_conv_c_w)

    def scatter_start(parts, name, after=None):
        bufs = parts + [lax.empty(p.shape, p.dtype) for p in parts]
        return split_start(bufs, _scatter_copies(len(parts)), 7 * len(parts), name, after=after)

    def finish(l, started, after):
        (s1, r1, b1, _), (s2, r2, b2, _) = started
        gsq_own, rsq = split_wait(s1, r1, b1, _scatter_copies(1), after, f"rs_sq_wait{l}")
        gwin_own, gconv_own, rwin, rconv = split_wait(s2, r2, b2, _scatter_copies(2), after, f"rs_win_wait{l}")
        flat = lambda a: a.reshape(a.shape[0], 4 * rs, D)
        acc["win"] = adam_update(rwin, gwin_own, me, wt, m_wt, v_wt, l, acc["win"], f"adam_win{l}")
        acc["sq"] = adam_update(flat(rsq), flat(gsq_own), me, flat(wsq32), flat(msq32), flat(vsq32), l, acc["sq"], f"adam_wsq{l}")
        acc["conv"] = adam_update(rconv, gconv_own, me, conv_pack, mconv, vconv, l, acc["conv"], f"adam_conv{l}")

    pending = [None] * L
    for l in reversed(range(L)):
        sv = saved[l]
        proj = sv["proj"]
        (dyah, dybh, dych, dzb, dyab, dybb, dycb, dgpost, dproj) = merge_bwd(
            dx, sv["z"], sv["ya"], sv["yb"], sv["yc"], proj, sv["wsq"], vec(norm_post, l), f"merge_bwd{l}")
        gsq = [tn_matmul(a, b, f"dwsq{t}_{l}") for t, (a, b) in enumerate(
            ((sv["yah"], dyab), (sv["ybh"], dybb), (sv["ych"], dycb), (sv["y16"], dzb)))]
        gsq_parts = jnp.stack([g.reshape(NDEV, rs, D) for g in gsq], axis=1)
        st1 = scatter_start([gsq_parts], f"rs_sq_start{l}", after=loss.reshape(1, 1) if l == L - 1 else None)
        convw = sv["convw"] + st1[3][0, 0]
        gca, dproj = brancha_bwd(dyah, proj, convw, dproj, f"bra_bwd{l}")
        du1, dlg, dlb, dcb, dproj = branchc2_bwd(dych, sv["u1"], proj, vec(ln_c_g, l), vec(ln_c_b, l), dproj, f"brc2_bwd{l}")
        gcc, dproj = branchc1_bwd(du1, proj, convw, dproj, f"brc1_bwd{l}")
        dqh, dkh, dvh, dproj = attn_bwd(dybh, sv["o"], sv["qh"], sv["kh"], sv["vh"], proj, dproj, f"attn_bwd{l}")
        dqn, dkn, dproj = qkv_bwd(dqh, dkh, dvh, proj, vec(q_norm, l), vec(k_norm, l), cos, sin, dproj, f"qkv_bwd{l}")
        gwin = dwin_parts(sv["h"], dproj, f"dwin{l}").reshape(NDEV, PSH, D)
        gconv = jnp.concatenate([gca, gcc], axis=1)
        if l > 0:
            st2 = scatter_start([gwin, gconv], f"rs_win_start{l}")
            issued = st2[3][0, 0]
        else:
            st2 = scatter_start([gconv], "rs_conv_start0")
            pair = split_start([gwin, lax.empty((NDEV // 2, PSH, D), bf16)], _pair_copies, NDEV // 2, "rs_pair_start0")
            issued = st2[3][0, 0] + pair[3][0, 0]
        dx, dgpre = dh_bwd(dproj, sv["wfull"], sv["x"], dx, vec(norm_pre, l) + issued, f"dh{l}")
        wide = lambda a: jnp.pad(a, ((0, 0), (0, D - HD)))
        small_parts[l] = jnp.concatenate([dgpre, dgpost, dcb, dlg, dlb, wide(dqn), wide(dkn), jnp.zeros((1, D), f32)], axis=0)
        pending[l] = (st1, st2)

    gwin0, pair_land = split_wait(pair[0], pair[1], pair[2], _pair_copies, dx, "rs_pair_wait0")
    summed = pair_sum(gwin0, pair_land, me, "rs_pair_sum0")
    (small_all,) = all_gather([jnp.concatenate(small_parts, axis=0)], "ag_small")
    chip = split_start([summed, lax.empty(summed.shape, bf16)], _chip_copies, NDEV // 2 - 1, "rs_chip_start0", after=small_all)
    for l in reversed(range(1, L)):
        finish(l, pending[l], after=chip[3])
    sm = adam_update(small_all, small_all, me,
                     _pack_small(norm_pre, norm_post, conv_c_b, ln_c_g, ln_c_b, q_norm, k_norm)[None],
                     _pack_small(m_norm_pre, m_norm_post, m_conv_c_b, m_ln_c_g, m_ln_c_b, m_q_norm, m_k_norm)[None],
                     _pack_small(v_norm_pre, v_norm_post, v_conv_c_b, v_ln_c_g, v_ln_c_b, v_q_norm, v_k_norm)[None],
                     0, None, "adam_small")
    (s1, r1, b1, _), (s2, r2, b2, _) = pending[0]
    gsq_own, rsq = split_wait(s1, r1, b1, _scatter_copies(1), sm[0], "rs_sq_wait0")
    gconv_own, rconv = split_wait(s2, r2, b2, _scatter_copies(1), sm[0], "rs_conv_wait0")
    flat = lambda a: a.reshape(a.shape[0], 4 * rs, D)
    acc["sq"] = adam_update(flat(rsq), flat(gsq_own), me, flat(wsq32), flat(msq32), flat(vsq32), 0, acc["sq"], "adam_wsq0")
    acc["conv"] = adam_update(rconv, gconv_own, me, conv_pack, mconv, vconv, 0, acc["conv"], "adam_conv0")
    summed, chip_land = split_wait(chip[0], chip[1], chip[2], _chip_copies, acc["sq"][0], "rs_chip_wait0")
    acc["win"] = adam_update(chip_land, summed, me // 2, wt, m_wt, v_wt, 0, acc["win"], "adam_win0")
    sm = [a.reshape(L, 8, D) for a in sm]
    small_rows = dict(norm_pre=(0, D), norm_post=(1, D), conv_c_b=(2, D), ln_c_g=(3, D), ln_c_b=(4, D), q_norm=(5, HD), k_norm=(6, HD))
    sq_rows = dict(w_out_a=0, w_out_b=1, w_out_c=2, w_o=3)

    order = ["norm_pre", "norm_post", "w_in", "conv_a_w", "q_norm", "k_norm", "conv_c_w", "conv_c_b", "ln_c_g", "ln_c_b",
             "w_out_a", "w_out_b", "w_out_c", "w_o"]
    result = [loss, dx.reshape(1, S, D)]
    for kind in range(4):
        for nme in order:
            if nme in small_rows:
                rw, wd = small_rows[nme]
                result.append(sm[kind][:, rw, :wd])
            elif nme in sq_rows:
                result.append(acc["sq"][kind][:, sq_rows[nme] * rs:(sq_rows[nme] + 1) * rs])
            elif nme == "w_in":
                result.append(jnp.swapaxes(acc["win"][kind], 1, 2))
            elif nme == "conv_a_w":
                result.append(acc["conv"][kind][:, 0:CA_W])
            else:
                result.append(acc["conv"][kind][:, 8:8 + CC_W])
    return tuple(result)
```

```python
import math

import jax
import jax.numpy as jnp
from jax import lax
from jax.experimental import pallas as pl
from jax.experimental.pallas import tpu as pltpu

f32, bf16 = jnp.float32, jnp.bfloat16

D = 1024
S = 2048
L = 4
HD = 128
NQ = D // HD
NKV = NQ // 4
G = NQ // NKV
WKV = NKV * HD
GRID_W = 64
ROPE_THETA = 10000.0
RMS_EPS = 1e-6
LN_EPS = 1e-5
NDEV = 8
CA_W, CC_W = 3, 31
P = 12 * D + 2 * WKV
PSH = P // NDEV
CT = 128
ADAM_LR, ADAM_B1, ADAM_B2, ADAM_EPS, ADAM_WD, ADAM_STEP = 0.001, 0.9, 0.999, 1e-08, 0.01, 10
VMEM_LIMIT = 56 * 1024 * 1024
MESH = pl.DeviceIdType.MESH

_OFF = {}
_o = 0
for _n, _w in (("a_b", D), ("a_c", D), ("a_x", D), ("a_g", D), ("q", D), ("k", WKV), ("v", WKV), ("b_g", D),
               ("c_u", D), ("c_v", D), ("c_g", D), ("m_a", D), ("m_b", D), ("m_c", D)):
    _OFF[_n] = (_o, _w)
    _o += _w
PIECES = tuple(_OFF)


def _cp(sem=None, **kw):
    return pltpu.CompilerParams(dimension_semantics=sem, vmem_limit_bytes=VMEM_LIMIT, **kw)


def _sig(x):
    return 1.0 / (1.0 + jnp.exp(-x))


def _silu(x):
    return x * _sig(x)


def _dsilu(x):
    s = _sig(x)
    return s * (1.0 + x * (1.0 - s))


def _row_specs(name, tm):
    off, w = _OFF[name]
    bw = math.gcd(off, w) if off else w
    return [pl.BlockSpec((tm, bw), (lambda i, *_, b=off // bw + t: (i, b))) for t in range(w // bw)]


def _cat(refs):
    return refs[0][...] if len(refs) == 1 else jnp.concatenate([r[...] for r in refs], axis=1)


def _chan_spec(name):
    off, _ = _OFF[name]
    return pl.BlockSpec((S, CT), lambda j, b=off // CT: (0, b + j))


def _full(shape):
    return pl.BlockSpec(shape, lambda *_: (0,) * len(shape))


def _coords():
    return lax.axis_index("x"), lax.axis_index("y"), lax.axis_index("c")


def all_gather(shards, name):
    n = len(shards)

    def body(*refs):
        ins, outs = refs[:n], refs[n:2 * n]
        send_sems, recv_sems, local_sems = refs[2 * n:]
        x, y, c = _coords()
        me, sibling = (x, y, c), (x, y, 1 - c)
        chips = [(1 - x, y), (x, 1 - y), (1 - x, 1 - y)]

        def slot(a, p):
            return outs[a].at[4 * p[0] + 2 * p[1] + p[2]]

        def copy(a, k, block, to, src=None):
            return pltpu.make_async_remote_copy(
                src_ref=slot(a, block) if src is None else src, dst_ref=slot(a, block),
                send_sem=send_sems.at[7 * a + k], recv_sem=recv_sems.at[7 * a + k], device_id=to, device_id_type=MESH)

        mine = [pltpu.make_async_copy(ins[a], slot(a, me), local_sems.at[a]) for a in range(n)]
        for cp in mine:
            cp.start()
        first = []
        for a in range(n):
            first.append(copy(a, 0, me, sibling, src=ins[a]))
            first += [copy(a, 1 + j, me, (*chip, c), src=ins[a]) for j, chip in enumerate(chips)]
        for cp in first:
            cp.start()
        passed = []
        for j, chip in enumerate(chips):
            for a in range(n):
                copy(a, 1 + j, (*chip, c), me).wait_recv()
                fw = copy(a, 4 + j, (*chip, c), sibling)
                fw.start()
                passed.append(fw)
        for a in range(n):
            copy(a, 0, sibling, me).wait_recv()
            for j, chip in enumerate(chips):
                copy(a, 4 + j, (*chip, 1 - c), me).wait_recv()
        for cp in first + passed:
            cp.wait_send()
        for cp in mine:
            cp.wait()

    anyspec = pl.BlockSpec(memory_space=pl.ANY)
    return pl.pallas_call(
        body, name=name,
        out_shape=[jax.ShapeDtypeStruct((NDEV,) + s.shape, s.dtype) for s in shards],
        in_specs=[anyspec] * n, out_specs=[anyspec] * n,
        scratch_shapes=[pltpu.SemaphoreType.DMA((7 * n,)), pltpu.SemaphoreType.DMA((7 * n,)), pltpu.SemaphoreType.DMA((n,))],
    )(*shards)


_HBM = pl.BlockSpec(memory_space=pltpu.HBM)
_SEM = pl.BlockSpec(memory_space=pltpu.SEMAPHORE)
_EFFECT = pltpu.SideEffectType.DATAFLOW_SIDE_EFFECTING


def split_start(bufs, make_copies, nsem, name, after=None):
    n = len(bufs)
    extra = [] if after is None else [after]

    def body(*refs):
        send_sems, recv_sems = refs[n + len(extra):n + len(extra) + 2]
        for cp in make_copies(refs[:n], send_sems, recv_sems):
            cp.start()
        refs[-1][...] = jnp.zeros((8, 128), f32)

    res = pl.pallas_call(
        body, name=name,
        out_shape=(pltpu.SemaphoreType.DMA((nsem,)), pltpu.SemaphoreType.DMA((nsem,)),
                   *[pltpu.HBM(b.shape, b.dtype) for b in bufs], jax.ShapeDtypeStruct((8, 128), f32)),
        in_specs=[_HBM] * n + [pl.BlockSpec(memory_space=pl.ANY)] * len(extra),
        out_specs=(_SEM, _SEM, *([_HBM] * n), pl.BlockSpec(memory_space=pltpu.VMEM)),
        input_output_aliases={i: 2 + i for i in range(n)},
        compiler_params=pltpu.CompilerParams(has_side_effects=_EFFECT),
    )(*[pltpu.with_memory_space_constraint(b, pltpu.HBM) for b in bufs], *extra)
    return res[0], res[1], list(res[2:2 + n]), res[-1]


def split_wait(send_sems, recv_sems, bufs, make_copies, after, name):
    n = len(bufs)

    def body(*refs):
        for cp in make_copies(refs[:n], refs[n], refs[n + 1]):
            cp.wait_send()
            cp.wait_recv()

    res = pl.pallas_call(
        body, name=name,
        out_shape=tuple(pltpu.HBM(b.shape, b.dtype) for b in bufs),
        in_specs=[_HBM] * n + [_SEM, _SEM, pl.BlockSpec(memory_space=pl.ANY)],
        out_specs=[_HBM] * n,
        input_output_aliases={i: i for i in range(n)},
        compiler_params=pltpu.CompilerParams(has_side_effects=_EFFECT),
    )(*bufs, send_sems, recv_sems, after)
    return list(res)


def _scatter_copies(n):
    def make(refs, send_sems, recv_sems):
        x, y, c = _coords()
        me = 4 * x + 2 * y + c
        copies = []
        for a in range(n):
            for k in range(1, NDEV):
                px = 1 - x if (k >> 2) & 1 else x
                py = 1 - y if (k >> 1) & 1 else y
                pc = 1 - c if k & 1 else c
                copies.append(pltpu.make_async_remote_copy(
                    src_ref=refs[a].at[4 * px + 2 * py + pc], dst_ref=refs[n + a].at[me],
                    send_sem=send_sems.at[7 * a + k - 1], recv_sem=recv_sems.at[7 * a + k - 1],
                    device_id=(px, py, pc), device_id_type=MESH))
        return copies
    return make


def _gather_copies(refs, send_sems, recv_sems):
    x, y, c = _coords()
    me = 4 * x + 2 * y + c
    targets = [(x, y, 1 - c), (1 - x, y, c), (x, 1 - y, c), (1 - x, 1 - y, c)]
    return [pltpu.make_async_remote_copy(
        src_ref=r.at[me], dst_ref=r.at[me], send_sem=send_sems.at[4 * a + k], recv_sem=recv_sems.at[4 * a + k],
        device_id=to, device_id_type=MESH) for a, r in enumerate(refs) for k, to in enumerate(targets)]


def _forward_copies(refs, send_sems, recv_sems):
    x, y, c = _coords()
    chips = [(1 - x, y), (x, 1 - y), (1 - x, 1 - y)]
    return [pltpu.make_async_remote_copy(
        src_ref=r.at[4 * px + 2 * py + c], dst_ref=r.at[4 * px + 2 * py + c], send_sem=send_sems.at[3 * a + j],
        recv_sem=recv_sems.at[3 * a + j], device_id=(x, y, 1 - c), device_id_type=MESH)
        for a, r in enumerate(refs) for j, (px, py) in enumerate(chips)]


def _pair_copies(refs, send_sems, recv_sems):
    x, y, c = _coords()
    parts, land = refs
    return [pltpu.make_async_remote_copy(
        src_ref=parts.at[2 * j + 1 - c], dst_ref=land.at[j], send_sem=send_sems.at[j], recv_sem=recv_sems.at[j],
        device_id=(x, y, 1 - c), device_id_type=MESH) for j in range(NDEV // 2)]


def _chip_copies(refs, send_sems, recv_sems):
    x, y, c = _coords()
    summed, land = refs
    copies = []
    for k in range(1, NDEV // 2):
        px = 1 - x if (k >> 1) & 1 else x
        py = 1 - y if k & 1 else y
        copies.append(pltpu.make_async_remote_copy(
            src_ref=summed.at[2 * px + py], dst_ref=land.at[2 * x + y], send_sem=send_sems.at[k - 1],
            recv_sem=recv_sems.at[k - 1], device_id=(px, py, c), device_id_type=MESH))
    return copies


def pair_sum(parts, land, me, name):
    _, r, c = parts.shape
    tr = _row_tile(r)

    def body(me_ref, a_ref, b_ref, o_ref):
        o_ref[...] = (a_ref[...].astype(f32) + b_ref[...].astype(f32)).astype(bf16)

    blk = pl.BlockSpec((1, tr, c), lambda j, i, m: (j, i, 0))
    return pl.pallas_call(
        body, name=name, out_shape=jax.ShapeDtypeStruct((NDEV // 2, r, c), bf16),
        grid_spec=pltpu.PrefetchScalarGridSpec(
            num_scalar_prefetch=1, grid=(NDEV // 2, r // tr),
            in_specs=[pl.BlockSpec((1, tr, c), lambda j, i, m: (2 * j + m[0] % 2, i, 0)), blk], out_specs=blk),
        compiler_params=_cp(("parallel", "parallel")))(me, parts, land)


def _row_tile(r):
    return r if r <= 256 else max(t for t in (256, 160, 128) if r % t == 0)


def stage_shards(wt_l, wsq_l, conv_l, me, name, after=None):
    outs = []
    extra = [] if after is None else [after]
    for a, dt in ((wt_l, bf16), (wsq_l, bf16), (conv_l, f32)):
        r, c = a.shape
        tr = _row_tile(r)

        def body(me_ref, a_ref, *rest):
            rest[-1][0] = a_ref[...].astype(rest[-1].dtype)

        outs.append(pl.pallas_call(
            body, name=f"{name}_{len(outs)}", out_shape=jax.ShapeDtypeStruct((NDEV, r, c), dt),
            grid_spec=pltpu.PrefetchScalarGridSpec(
                num_scalar_prefetch=1, grid=(r // tr,),
                in_specs=[pl.BlockSpec((tr, c), lambda i, m: (i, 0))] + [pl.BlockSpec(memory_space=pl.ANY)] * len(extra),
                out_specs=pl.BlockSpec((1, tr, c), lambda i, m: (m[0], i, 0))),
            compiler_params=_cp(("arbitrary",)))(me, a, *extra))
    return outs


def proj_fwd(xin, g_pre, wt, name):
    tm, tn = min(S, 1024), 2560

    def body(x_ref, g_ref, w_ref, proj_ref, h_ref, hs):
        @pl.when(pl.program_id(1) == 0)
        def _():
            x = x_ref[...]
            r = lax.rsqrt(jnp.mean(x * x, axis=-1, keepdims=True) + RMS_EPS)
            h = (x * r * g_ref[...]).astype(bf16)
            hs[...] = h
            h_ref[...] = h
        proj_ref[...] = lax.dot_general(hs[...], w_ref[...], (((1,), (1,)), ((), ())), preferred_element_type=f32)

    return pl.pallas_call(
        body, name=name, grid=(S // tm, P // tn),
        out_shape=[jax.ShapeDtypeStruct((S, P), f32), jax.ShapeDtypeStruct((S, D), bf16)],
        in_specs=[pl.BlockSpec((tm, D), lambda i, j: (i, 0)), _full((1, D)), pl.BlockSpec((tn, D), lambda i, j: (j, 0))],
        out_specs=[pl.BlockSpec((tm, tn), lambda i, j: (i, j)), pl.BlockSpec((tm, D), lambda i, j: (i, 0))],
        scratch_shapes=[pltpu.VMEM((tm, D), bf16)],
        compiler_params=_cp(("parallel", "arbitrary")))(xin, g_pre, wt)


RC = 256


def _fill_pad(pad, halo, val_fn):
    pad[0:halo, :] = jnp.zeros((halo, CT), f32)
    pad[S + halo:S + 2 * halo, :] = jnp.zeros((halo, CT), f32)

    def step(i, carry):
        rows = pl.ds(pl.multiple_of(i * RC, RC), RC)
        pad[pl.ds(pl.multiple_of(i * RC, RC) + halo, RC), :] = val_fn(rows)
        return carry
    lax.fori_loop(0, S // RC, step, 0)


def brancha_fwd(proj, convw, name):
    def body(ab, ac, ax, ag, w_ref, o_ref, pad):
        _fill_pad(pad, 8, lambda rows: ac[rows, :] * ax[rows, :])
        w = [w_ref[0, k:k + 1, :] for k in range(CA_W)]

        def step(i, carry):
            base = pl.multiple_of(i * RC, RC)
            rows = pl.ds(base, RC)
            t = sum(w[k] * pad[pl.ds(base + 7 + k, RC), :] for k in range(CA_W))
            o_ref[rows, :] = (ab[rows, :] * t * _silu(ag[rows, :])).astype(bf16)
            return carry
        lax.fori_loop(0, S // RC, step, 0)

    return pl.pallas_call(
        body, name=name, grid=(D // CT,), out_shape=jax.ShapeDtypeStruct((S, D), bf16),
        in_specs=[_chan_spec("a_b"), _chan_spec("a_c"), _chan_spec("a_x"), _chan_spec("a_g"),
                  pl.BlockSpec((1, 40, CT), lambda j: (j, 0, 0))],
        out_specs=pl.BlockSpec((S, CT), lambda j: (0, j)),
        scratch_shapes=[pltpu.VMEM((S + 16, CT), f32)],
        compiler_params=_cp(("parallel",)))(proj, proj, proj, proj, convw)


def branchc1_fwd(proj, convw, cbias, name):
    def body(cu, cv, w_ref, b_ref, o_ref, pad):
        _fill_pad(pad, 16, lambda rows: cu[rows, :] * _sig(cv[rows, :]))

        def step(i, carry):
            base = pl.multiple_of(i * RC, RC)
            acc = jnp.zeros((RC, CT), f32) + b_ref[...]
            for k in range(CC_W):
                acc = acc + w_ref[0, 8 + k:9 + k, :] * pad[pl.ds(base + k + 1, RC), :]
            o_ref[pl.ds(base, RC), :] = acc
            return carry
        lax.fori_loop(0, S // RC, step, 0)

    return pl.pallas_call(
        body, name=name, grid=(D // CT,), out_shape=jax.ShapeDtypeStruct((S, D), f32),
        in_specs=[_chan_spec("c_u"), _chan_spec("c_v"), pl.BlockSpec((1, 40, CT), lambda j: (j, 0, 0)),
                  pl.BlockSpec((1, CT), lambda j: (0, j))],
        out_specs=pl.BlockSpec((S, CT), lambda j: (0, j)),
        scratch_shapes=[pltpu.VMEM((S + 32, CT), f32)],
        compiler_params=_cp(("parallel",)))(proj, proj, convw, cbias)


def _swap32(x):
    lane = lax.broadcasted_iota(jnp.int32, x.shape, 1)
    return jnp.where((lane // 32) % 2 == 1, pltpu.roll(x, 32, 1), pltpu.roll(x, HD - 32, 1))


def _rope(y, cos, sin):
    return y * cos + _swap32(y) * sin


def qkv_fwd(proj, qn, kn, cos, sin, name):
    tm = min(S, 512)
    nq, nk, nv = len(_row_specs("q", tm)), len(_row_specs("k", tm)), len(_row_specs("v", tm))

    def body(*refs):
        q = _cat(refs[:nq])
        k = _cat(refs[nq:nq + nk])
        v = _cat(refs[nq + nk:nq + nk + nv])
        qn_ref, kn_ref, cos_ref, sin_ref, qh_ref, kh_ref, vh_ref = refs[nq + nk + nv:]
        cos, sin = cos_ref[...], sin_ref[...]

        def heads(xx, gn, out_ref, n):
            for h in range(n):
                xh = xx[:, h * HD:(h + 1) * HD]
                r = lax.rsqrt(jnp.mean(xh * xh, axis=-1, keepdims=True) + RMS_EPS)
                out_ref[:, h * HD:(h + 1) * HD] = _rope(xh * r * gn, cos, sin).astype(bf16)
        heads(q, qn_ref[...], qh_ref, NQ)
        heads(k, kn_ref[...], kh_ref, NKV)
        vh_ref[...] = v.astype(bf16)

    row = lambda w: pl.BlockSpec((tm, w), lambda i: (i, 0))
    return pl.pallas_call(
        body, name=name, grid=(S // tm,),
        out_shape=[jax.ShapeDtypeStruct((S, D), bf16), jax.ShapeDtypeStruct((S, WKV), bf16), jax.ShapeDtypeStruct((S, WKV), bf16)],
        in_specs=_row_specs("q", tm) + _row_specs("k", tm) + _row_specs("v", tm) + [_full((1, HD)), _full((1, HD)), row(HD), row(HD)],
        out_specs=[row(D), row(WKV), row(WKV)],
        compiler_params=_cp(("parallel",)))(*([proj] * (nq + nk + nv)), qn, kn, cos, sin)


def _softmax_rows(q, k):
    s = lax.dot_general(q, k, (((1,), (1,)), ((), ())), preferred_element_type=f32)
    p = jnp.exp((s - jnp.max(s, axis=-1, keepdims=True)) * (HD ** -0.5))
    return p, 1.0 / jnp.sum(p, axis=-1, keepdims=True)


GW = G * HD


def attn_fwd(qh, kh, vh, proj, name):
    tq = min(S, 512)
    bg_blk = _OFF["b_g"][0] // GW

    def body(q_ref, k_ref, v_ref, bg_ref, o_ref, y_ref):
        k, v = k_ref[...], v_ref[...]
        for g in range(G):
            cols = slice(g * HD, (g + 1) * HD)
            p, rl = _softmax_rows(q_ref[:, cols], k)
            o = jnp.dot(p.astype(bf16), v, preferred_element_type=f32) * rl
            o_ref[:, cols] = o
            y_ref[:, cols] = (o * _silu(bg_ref[:, cols])).astype(bf16)

    grp = pl.BlockSpec((tq, GW), lambda kv, i: (i, kv))
    kvs = pl.BlockSpec((S, HD), lambda kv, i: (0, kv))
    return pl.pallas_call(
        body, name=name, grid=(NKV, S // tq),
        out_shape=[jax.ShapeDtypeStruct((S, D), f32), jax.ShapeDtypeStruct((S, D), bf16)],
        in_specs=[grp, kvs, kvs, pl.BlockSpec((tq, GW), lambda kv, i: (i, bg_blk + kv))],
        out_specs=[grp, grp],
        compiler_params=_cp(("parallel", "parallel")))(qh, kh, vh, proj)


def _ln_parts(u1):
    mu = jnp.mean(u1, axis=-1, keepdims=True)
    xc = u1 - mu
    rstd = lax.rsqrt(jnp.mean(xc * xc, axis=-1, keepdims=True) + LN_EPS)
    return xc * rstd, rstd


def _after(after):
    return ([], []) if after is None else ([after], [pl.BlockSpec(memory_space=pl.ANY)])


def branchc2_fwd(u1, proj, lng, lnb, name, after=None):
    tm = min(S, 512)
    ncg = len(_row_specs("c_g", tm))
    extra, extra_specs = _after(after)

    def body(*refs):
        u_ref = refs[0]
        cg = _cat(refs[1:1 + ncg])
        g_ref, b_ref = refs[1 + ncg:3 + ncg]
        xh, _ = _ln_parts(u_ref[...])
        refs[-1][...] = (_silu(xh * g_ref[...] + b_ref[...]) * _silu(cg)).astype(bf16)

    row = pl.BlockSpec((tm, D), lambda i: (i, 0))
    return pl.pallas_call(
        body, name=name, grid=(S // tm,), out_shape=jax.ShapeDtypeStruct((S, D), bf16),
        in_specs=[row] + _row_specs("c_g", tm) + [_full((1, D)), _full((1, D))] + extra_specs, out_specs=row,
        compiler_params=_cp(("parallel",)))(u1, *([proj] * ncg), lng, lnb, *extra)


def _wmat(w_ref, kind):
    return w_ref[:, kind].reshape(D, D)


def merge_fwd(xin, yah, ybh, ych, proj, wsq, g_post, name):
    tm = min(S, 512)
    nm = len(_row_specs("m_a", tm))

    def body(*refs):
        x_ref, a_ref, b_ref, c_ref = refs[:4]
        ms = [_cat(refs[4 + t * nm:4 + (t + 1) * nm]) for t in range(3)]
        w_ref, g_ref, ya_ref, yb_ref, yc_ref, y_ref, z_ref, o_ref = refs[4 + 3 * nm:]
        y = jnp.zeros((tm, D), f32)
        for t, (h_ref, out_ref) in enumerate(((a_ref, ya_ref), (b_ref, yb_ref), (c_ref, yc_ref))):
            yt = jnp.dot(h_ref[...], _wmat(w_ref, t), preferred_element_type=f32)
            out_ref[...] = yt
            y = y + _sig(ms[t]) * yt
        yb16 = y.astype(bf16)
        y_ref[...] = yb16
        z = jnp.dot(yb16, _wmat(w_ref, 3), preferred_element_type=f32)
        z_ref[...] = z
        r = lax.rsqrt(jnp.mean(z * z, axis=-1, keepdims=True) + RMS_EPS)
        o_ref[...] = x_ref[...] + z * r * g_ref[...]

    row = pl.BlockSpec((tm, D), lambda i: (i, 0))
    sd = lambda dt: jax.ShapeDtypeStruct((S, D), dt)
    return pl.pallas_call(
        body, name=name, grid=(S // tm,),
        out_shape=[sd(f32), sd(f32), sd(f32), sd(bf16), sd(f32), sd(f32)],
        in_specs=[row] * 4 + _row_specs("m_a", tm) + _row_specs("m_b", tm) + _row_specs("m_c", tm)
        + [pl.BlockSpec((NDEV, 4, D // NDEV, D), lambda *_: (0, 0, 0, 0), pipeline_mode=pl.Buffered(1)), _full((1, D))],
        out_specs=[row] * 6,
        compiler_params=_cp(("parallel",)))(xin, yah, ybh, ych, *([proj] * (3 * nm)), wsq, g_post)


def loss_fwd(y, target, name):
    tm = min(S, 256)

    def body(y_ref, t_ref, dy_ref, l_ref):
        e = y_ref[...] - t_ref[...]
        dy_ref[...] = e / D

        @pl.when(pl.program_id(0) == 0)
        def _():
            l_ref[...] = jnp.zeros((1, 128), f32)
        l_ref[...] += (0.5 / D) * jnp.sum(e * e)

    row = pl.BlockSpec((tm, D), lambda i: (i, 0))
    return pl.pallas_call(
        body, name=name, grid=(S // tm,),
        out_shape=[jax.ShapeDtypeStruct((S, D), f32), jax.ShapeDtypeStruct((1, 128), f32)],
        in_specs=[row, row], out_specs=[row, _full((1, 128))],
        compiler_params=_cp(("arbitrary",)))(y, target)


def _acc(ref, val):
    @pl.when(pl.program_id(0) == 0)
    def _():
        ref[...] = jnp.zeros(ref.shape, f32)
    ref[...] += val


def _emit_copies(stash, dst, sems, windows):
    return [pltpu.make_async_copy(stash.at[p], dst.at[w], sems.at[p]) for p, w in enumerate(windows)]


def _emit_drain_previous(copies, step):
    @pl.when(step > 0)
    def _():
        for cp in copies:
            cp.wait()


def _emit_start(copies, step, nsteps):
    for cp in copies:
        cp.start()

    @pl.when(step == nsteps - 1)
    def _():
        for cp in copies:
            cp.wait()


def merge_bwd(dout, z, ya, yb, yc, proj, wsq, g_post, name):
    tm = min(S, 256)
    nm = len(_row_specs("m_a", tm))
    nsteps = S // tm

    def body(*refs):
        do_ref, z_ref, ya_ref, yb_ref, yc_ref = refs[:5]
        ms = [_cat(refs[5 + t * nm:5 + (t + 1) * nm]) for t in range(3)]
        w_ref, g_ref = refs[5 + 3 * nm:7 + 3 * nm]
        dh_refs = refs[7 + 3 * nm:10 + 3 * nm]
        dzb_ref = refs[10 + 3 * nm]
        dyb_refs = refs[11 + 3 * nm:14 + 3 * nm]
        dg_ref = refs[14 + 3 * nm]
        dproj_ref, stash, sems = refs[15 + 3 * nm:]
        i = pl.program_id(0)
        rows = pl.ds(pl.multiple_of(i * tm, tm), tm)
        copies = _emit_copies(stash, dproj_ref, sems, [(rows, pl.ds(_OFF[n][0], D)) for n in ("m_a", "m_b", "m_c")])
        nt = (((1,), (1,)), ((), ()))
        z, dout = z_ref[...], do_ref[...]
        r = lax.rsqrt(jnp.mean(z * z, axis=-1, keepdims=True) + RMS_EPS)
        zh = z * r
        _acc(dg_ref, jnp.sum(dout * zh, axis=0, keepdims=True))
        dzh = dout * g_ref[...]
        dz = (r * (dzh - zh * jnp.mean(dzh * zh, axis=-1, keepdims=True))).astype(bf16)
        dzb_ref[...] = dz
        dy = lax.dot_general(dz, _wmat(w_ref, 3), nt, preferred_element_type=f32)
        dms = []
        for t, yt_ref in enumerate((ya_ref, yb_ref, yc_ref)):
            sg = _sig(ms[t])
            dyt = (dy * sg).astype(bf16)
            dyb_refs[t][...] = dyt
            dms.append((dy * yt_ref[...] * sg * (1.0 - sg)).astype(bf16))
            dh_refs[t][...] = lax.dot_general(dyt, _wmat(w_ref, t), nt, preferred_element_type=f32)
        _emit_drain_previous(copies, i)
        for t in range(3):
            stash[t] = dms[t]
        _emit_start(copies, i, nsteps)

    row = pl.BlockSpec((tm, D), lambda i: (i, 0))
    sd = lambda dt: jax.ShapeDtypeStruct((S, D), dt)
    return pl.pallas_call(
        body, name=name, grid=(nsteps,),
        out_shape=[sd(f32)] * 3 + [sd(bf16)] * 4 + [jax.ShapeDtypeStruct((1, D), f32), jax.ShapeDtypeStruct((S, P), bf16)],
        in_specs=[row] * 5 + _row_specs("m_a", tm) + _row_specs("m_b", tm) + _row_specs("m_c", tm)
        + [pl.BlockSpec((NDEV, 4, D // NDEV, D), lambda *_: (0, 0, 0, 0), pipeline_mode=pl.Buffered(1)), _full((1, D))],
        out_specs=[row] * 7 + [_full((1, D)), pl.BlockSpec(memory_space=pl.ANY)],
        scratch_shapes=[pltpu.VMEM((3, tm, D), bf16), pltpu.SemaphoreType.DMA((3,))],
        compiler_params=_cp(("arbitrary",)))(dout, z, ya, yb, yc, *([proj] * (3 * nm)), wsq, g_post)


def tn_matmul(a, b, name):
    m, n = a.shape[1], b.shape[1]
    tmm = min(m, 512)

    def body(a_ref, b_ref, o_ref):
        o_ref[...] = lax.dot_general(a_ref[...], b_ref[...], (((0,), (0,)), ((), ())), preferred_element_type=f32).astype(bf16)

    return pl.pallas_call(
        body, name=name, grid=(m // tmm,), out_shape=jax.ShapeDtypeStruct((m, n), bf16),
        in_specs=[pl.BlockSpec((S, tmm), lambda i: (0, i)), _full((S, n))],
        out_specs=pl.BlockSpec((tmm, n), lambda i: (i, 0)),
        compiler_params=_cp(("parallel",)))(a, b)


def dwin_parts(h, dproj, name):
    tn = 1280

    def body(d_ref, h_ref, o_ref):
        o_ref[...] = lax.dot_general(d_ref[...], h_ref[...], (((0,), (0,)), ((), ())), preferred_element_type=f32).astype(bf16)

    return pl.pallas_call(
        body, name=name, grid=(P // tn,), out_shape=jax.ShapeDtypeStruct((P, D), bf16),
        in_specs=[pl.BlockSpec((S, tn), lambda j: (0, j)), _full((S, D))],
        out_specs=pl.BlockSpec((tn, D), lambda j: (j, 0)),
        compiler_params=_cp(("parallel",)))(dproj, h)


def _chan_windows(names, j):
    return [(slice(None), pl.ds(pl.multiple_of(_OFF[n][0] + j * CT, CT), CT)) for n in names]


def brancha_bwd(dyah, proj, convw, dproj, name, after=None):
    nsteps = D // CT
    extra, extra_specs = _after(after)

    def body(d_ref, ab, ac, ax, ag, w_ref, *rest):
        dw_ref, dproj_ref, padp, padt, accw, stash, sems = rest[-7:]
        j = pl.program_id(0)
        copies = _emit_copies(stash, dproj_ref, sems, _chan_windows(("a_b", "a_c", "a_x", "a_g"), j))
        _fill_pad(padp, 8, lambda rows: ac[rows, :] * ax[rows, :])
        _fill_pad(padt, 8, lambda rows: d_ref[rows, :] * ab[rows, :] * _silu(ag[rows, :]))
        accw[...] = jnp.zeros(accw.shape, f32)
        w = [w_ref[0, k:k + 1, :] for k in range(CA_W)]
        _emit_drain_previous(copies, j)

        def step(i, carry):
            base = pl.multiple_of(i * RC, RC)
            rows = pl.ds(base, RC)
            ps = [padp[pl.ds(base + 7 + k, RC), :] for k in range(CA_W)]
            t = sum(w[k] * ps[k] for k in range(CA_W))
            dp = sum(w[k] * padt[pl.ds(base + 9 - k, RC), :] for k in range(CA_W))
            d, a_b, a_g = d_ref[rows, :], ab[rows, :], ag[rows, :]
            stash[0, rows, :] = (d * t * _silu(a_g)).astype(bf16)
            stash[1, rows, :] = (dp * ax[rows, :]).astype(bf16)
            stash[2, rows, :] = (dp * ac[rows, :]).astype(bf16)
            stash[3, rows, :] = (d * a_b * t * _dsilu(a_g)).astype(bf16)
            dt = padt[pl.ds(base + 8, RC), :]
            for k in range(CA_W):
                accw[8 * k:8 * k + 8, :] += jnp.sum((dt * ps[k]).reshape(RC // 8, 8, CT), axis=0)
            return carry
        lax.fori_loop(0, S // RC, step, 0)
        _emit_start(copies, j, nsteps)
        dw_ref[0] = jnp.zeros((8, CT), f32)
        for k in range(CA_W):
            dw_ref[0, k:k + 1, :] = jnp.sum(accw[8 * k:8 * k + 8, :], axis=0, keepdims=True)

    tile = pl.BlockSpec((S, CT), lambda j: (0, j))
    anyspec = pl.BlockSpec(memory_space=pl.ANY)
    return pl.pallas_call(
        body, name=name, grid=(nsteps,),
        out_shape=[jax.ShapeDtypeStruct((NDEV, 8, CT), f32), jax.ShapeDtypeStruct((S, P), bf16)],
        in_specs=[tile, _chan_spec("a_b"), _chan_spec("a_c"), _chan_spec("a_x"), _chan_spec("a_g"),
                  pl.BlockSpec((1, 40, CT), lambda j: (j, 0, 0)), anyspec] + extra_specs,
        out_specs=[pl.BlockSpec((1, 8, CT), lambda j: (j, 0, 0)), anyspec],
        input_output_aliases={6: 1},
        scratch_shapes=[pltpu.VMEM((S + 16, CT), f32), pltpu.VMEM((S + 16, CT), f32), pltpu.VMEM((8 * CA_W, CT), f32),
                        pltpu.VMEM((4, S, CT), bf16), pltpu.SemaphoreType.DMA((4,))],
        compiler_params=_cp(("arbitrary",)))(dyah, proj, proj, proj, proj, convw, dproj, *extra)


def branchc2_bwd(dych, u1, proj, lng, lnb, dproj, name):
    tm = min(S, 512)
    ncg = len(_row_specs("c_g", tm))
    nsteps = S // tm

    def body(*refs):
        d_ref, u_ref = refs[:2]
        cg = _cat(refs[2:2 + ncg])
        g_ref, b_ref, _, du_ref, dlg_ref, dlb_ref, dcb_ref, dproj_ref, stash, sems = refs[2 + ncg:]
        i = pl.program_id(0)
        copies = _emit_copies(stash, dproj_ref, sems, [(pl.ds(pl.multiple_of(i * tm, tm), tm), pl.ds(_OFF["c_g"][0], D))])
        d = d_ref[...]
        xh, rstd = _ln_parts(u_ref[...])
        ln = xh * g_ref[...] + b_ref[...]
        _emit_drain_previous(copies, i)
        stash[0] = (d * _silu(ln) * _dsilu(cg)).astype(bf16)
        _emit_start(copies, i, nsteps)
        dln = d * _silu(cg) * _dsilu(ln)
        _acc(dlg_ref, jnp.sum(dln * xh, axis=0, keepdims=True))
        _acc(dlb_ref, jnp.sum(dln, axis=0, keepdims=True))
        dxh = dln * g_ref[...]
        du = rstd * (dxh - jnp.mean(dxh, axis=-1, keepdims=True) - xh * jnp.mean(dxh * xh, axis=-1, keepdims=True))
        du_ref[...] = du
        _acc(dcb_ref, jnp.sum(du, axis=0, keepdims=True))

    row = pl.BlockSpec((tm, D), lambda i: (i, 0))
    vec = jax.ShapeDtypeStruct((1, D), f32)
    anyspec = pl.BlockSpec(memory_space=pl.ANY)
    return pl.pallas_call(
        body, name=name, grid=(nsteps,),
        out_shape=[jax.ShapeDtypeStruct((S, D), f32), vec, vec, vec, jax.ShapeDtypeStruct((S, P), bf16)],
        in_specs=[row, row] + _row_specs("c_g", tm) + [_full((1, D)), _full((1, D)), anyspec],
        out_specs=[row, _full((1, D)), _full((1, D)), _full((1, D)), anyspec],
        input_output_aliases={4 + ncg: 4},
        scratch_shapes=[pltpu.VMEM((1, tm, D), bf16), pltpu.SemaphoreType.DMA((1,))],
        compiler_params=_cp(("arbitrary",)))(dych, u1, *([proj] * ncg), lng, lnb, dproj)


def branchc1_bwd(du1, proj, convw, dproj, name):
    nsteps = D // CT

    def body(d_ref, cu, cv, w_ref, _, dw_ref, dproj_ref, padu, padd, accw, stash, sems):
        j = pl.program_id(0)
        copies = _emit_copies(stash, dproj_ref, sems, _chan_windows(("c_u", "c_v"), j))
        _fill_pad(padu, 16, lambda rows: cu[rows, :] * _sig(cv[rows, :]))
        _fill_pad(padd, 16, lambda rows: d_ref[rows, :])
        accw[...] = jnp.zeros(accw.shape, f32)
        _emit_drain_previous(copies, j)

        rc = min(S, 128)

        def step(i, carry):
            base = pl.multiple_of(i * rc, rc)
            rows = pl.ds(base, rc)
            d = d_ref[rows, :]
            du0 = jnp.zeros((rc, CT), f32)
            for k in range(CC_W):
                du0 = du0 + w_ref[0, 8 + k:9 + k, :] * padd[pl.ds(base + 31 - k, rc), :]
                accw[8 * k:8 * k + 8, :] += jnp.sum((d * padu[pl.ds(base + k + 1, rc), :]).reshape(rc // 8, 8, CT), axis=0)
            sg = _sig(cv[rows, :])
            stash[0, rows, :] = (du0 * sg).astype(bf16)
            stash[1, rows, :] = (du0 * cu[rows, :] * sg * (1.0 - sg)).astype(bf16)
            return carry
        lax.fori_loop(0, S // rc, step, 0)
        _emit_start(copies, j, nsteps)
        dw_ref[0] = jnp.zeros((32, CT), f32)
        for k in range(CC_W):
            dw_ref[0, k:k + 1, :] = jnp.sum(accw[8 * k:8 * k + 8, :], axis=0, keepdims=True)

    tile = pl.BlockSpec((S, CT), lambda j: (0, j))
    anyspec = pl.BlockSpec(memory_space=pl.ANY)
    return pl.pallas_call(
        body, name=name, grid=(nsteps,),
        out_shape=[jax.ShapeDtypeStruct((NDEV, 32, CT), f32), jax.ShapeDtypeStruct((S, P), bf16)],
        in_specs=[tile, _chan_spec("c_u"), _chan_spec("c_v"), pl.BlockSpec((1, 40, CT), lambda j: (j, 0, 0)), anyspec],
        out_specs=[pl.BlockSpec((1, 32, CT), lambda j: (j, 0, 0)), anyspec],
        input_output_aliases={4: 1},
        scratch_shapes=[pltpu.VMEM((S + 32, CT), f32), pltpu.VMEM((S + 32, CT), f32), pltpu.VMEM((8 * 32, CT), f32),
                        pltpu.VMEM((2, S, CT), bf16), pltpu.SemaphoreType.DMA((2,))],
        compiler_params=_cp(("arbitrary",)))(du1, proj, proj, convw, dproj)


def attn_bwd(dybh, o, qh, kh, vh, proj, dproj, name):
    tq = min(S, 512)
    bg_blk = _OFF["b_g"][0] // GW

    def body(d_ref, o_ref, q_ref, k_ref, v_ref, bg_ref, _, dq_ref, dk_ref, dv_ref, dbg_ref):
        @pl.when(pl.program_id(1) == 0)
        def _():
            dk_ref[...] = jnp.zeros(dk_ref.shape, f32)
            dv_ref[...] = jnp.zeros(dv_ref.shape, f32)
        k, v = k_ref[...], v_ref[...]
        tn = (((0,), (0,)), ((), ()))
        dk_acc = jnp.zeros((S, HD), f32)
        dv_acc = jnp.zeros((S, HD), f32)
        for g in range(G):
            cols = slice(g * HD, (g + 1) * HD)
            d, bg, q, o = d_ref[:, cols], bg_ref[:, cols], q_ref[:, cols], o_ref[:, cols]
            dbg_ref[:, cols] = (d * o * _dsilu(bg)).astype(bf16)
            do = d * _silu(bg)
            p, rl = _softmax_rows(q, k)
            dv_acc = dv_acc + lax.dot_general(p.astype(bf16), (do * rl).astype(bf16), tn, preferred_element_type=f32)
            dp = lax.dot_general(do.astype(bf16), v, (((1,), (1,)), ((), ())), preferred_element_type=f32)
            delta = jnp.sum(do * o, axis=-1, keepdims=True)
            ds = (p * (dp - delta)).astype(bf16)
            rs_ = rl * (HD ** -0.5)
            dq_ref[:, cols] = jnp.dot(ds, k, preferred_element_type=f32) * rs_
            dk_acc = dk_acc + lax.dot_general(ds, (q.astype(f32) * rs_).astype(bf16), tn, preferred_element_type=f32)
        dk_ref[...] += dk_acc
        dv_ref[...] += dv_acc

    grp = pl.BlockSpec((tq, GW), lambda kv, i: (i, kv))
    kvs = pl.BlockSpec((S, HD), lambda kv, i: (0, kv))
    return pl.pallas_call(
        body, name=name, grid=(NKV, S // tq),
        out_shape=[jax.ShapeDtypeStruct((S, D), f32), jax.ShapeDtypeStruct((S, WKV), f32),
                   jax.ShapeDtypeStruct((S, WKV), f32), jax.ShapeDtypeStruct((S, P), bf16)],
        in_specs=[grp, grp, grp, kvs, kvs, pl.BlockSpec((tq, GW), lambda kv, i: (i, bg_blk + kv)),
                  pl.BlockSpec(memory_space=pl.ANY)],
        out_specs=[grp, kvs, kvs, pl.BlockSpec((tq, GW), lambda kv, i: (i, bg_blk + kv))],
        input_output_aliases={6: 3},
        compiler_params=_cp(("parallel", "arbitrary")))(dybh, o, qh, kh, vh, proj, dproj)


def qkv_bwd(dqh, dkh, dvh, proj, qn, kn, cos, sin, dproj, name):
    tm = min(S, 512)
    nq, nk = len(_row_specs("q", tm)), len(_row_specs("k", tm))
    nsteps = S // tm
    wq = D + 2 * WKV

    def body(*refs):
        dqh_ref, dkh_ref, dvh_ref = refs[:3]
        q = _cat(refs[3:3 + nq])
        k = _cat(refs[3 + nq:3 + nq + nk])
        qn_ref, kn_ref, cos_ref, sin_ref, _, dqn_ref, dkn_ref, dproj_ref, stash, sems = refs[3 + nq + nk:]
        i = pl.program_id(0)
        copies = _emit_copies(stash, dproj_ref, sems, [(pl.ds(pl.multiple_of(i * tm, tm), tm), pl.ds(_OFF["q"][0], wq))])
        cos, sin = cos_ref[...], sin_ref[...]
        _emit_drain_previous(copies, i)

        def heads(xx, dd, gn, col0, dgn_ref, n):
            dg = jnp.zeros((1, HD), f32)
            for h in range(n):
                xh = xx[:, h * HD:(h + 1) * HD]
                dh = dd[:, h * HD:(h + 1) * HD]
                r = lax.rsqrt(jnp.mean(xh * xh, axis=-1, keepdims=True) + RMS_EPS)
                xn = xh * r
                dy = dh * cos + _swap32(dh * sin)
                dg = dg + jnp.sum(dy * xn, axis=0, keepdims=True)
                dxn = dy * gn
                stash[0, :, col0 + h * HD:col0 + (h + 1) * HD] = (
                    r * (dxn - xn * jnp.mean(dxn * xn, axis=-1, keepdims=True))).astype(bf16)
            _acc(dgn_ref, dg)
        heads(q, dqh_ref[...], qn_ref[...], 0, dqn_ref, NQ)
        heads(k, dkh_ref[...], kn_ref[...], D, dkn_ref, NKV)
        stash[0, :, D + WKV:wq] = dvh_ref[...].astype(bf16)
        _emit_start(copies, i, nsteps)

    row = lambda w: pl.BlockSpec((tm, w), lambda i: (i, 0))
    vec = jax.ShapeDtypeStruct((1, HD), f32)
    anyspec = pl.BlockSpec(memory_space=pl.ANY)
    return pl.pallas_call(
        body, name=name, grid=(nsteps,),
        out_shape=[vec, vec, jax.ShapeDtypeStruct((S, P), bf16)],
        in_specs=[row(D), row(WKV), row(WKV)] + _row_specs("q", tm) + _row_specs("k", tm)
        + [_full((1, HD)), _full((1, HD)), row(HD), row(HD), anyspec],
        out_specs=[_full((1, HD)), _full((1, HD)), anyspec],
        input_output_aliases={7 + nq + nk: 2},
        scratch_shapes=[pltpu.VMEM((1, tm, wq), bf16), pltpu.SemaphoreType.DMA((1,))],
        compiler_params=_cp(("arbitrary",)))(dqh, dkh, dvh, *([proj] * (nq + nk)), qn, kn, cos, sin, dproj)


def dh_bwd(dproj, wfull, xin, dout, g_pre, name, after=None):
    tm, tk = min(S, 1024), 2560
    nk = P // tk
    extra, extra_specs = _after(after)

    def body(d_ref, w_ref, x_ref, do_ref, g_ref, *rest):
        dx_ref, dg_ref, acc = rest[-3:]
        kk = pl.program_id(1)

        @pl.when(kk == 0)
        def _():
            acc[...] = jnp.zeros(acc.shape, f32)
        acc[...] += jnp.dot(d_ref[...], w_ref[...], preferred_element_type=f32)

        @pl.when((kk == 0) & (pl.program_id(0) == 0))
        def _():
            dg_ref[...] = jnp.zeros(dg_ref.shape, f32)

        @pl.when(kk == nk - 1)
        def _():
            x, dh = x_ref[...], acc[...]
            r = lax.rsqrt(jnp.mean(x * x, axis=-1, keepdims=True) + RMS_EPS)
            xn = x * r
            dg_ref[...] += jnp.sum(dh * xn, axis=0, keepdims=True)
            dxn = dh * g_ref[...]
            dx_ref[...] = do_ref[...] + r * (dxn - xn * jnp.mean(dxn * xn, axis=-1, keepdims=True))

    row = pl.BlockSpec((tm, D), lambda i, k: (i, 0))
    return pl.pallas_call(
        body, name=name, grid=(S // tm, nk),
        out_shape=[jax.ShapeDtypeStruct((S, D), f32), jax.ShapeDtypeStruct((1, D), f32)],
        in_specs=[pl.BlockSpec((tm, tk), lambda i, k: (i, k)), pl.BlockSpec((tk, D), lambda i, k: (k, 0)), row, row, _full((1, D))]
        + extra_specs,
        out_specs=[row, _full((1, D))],
        scratch_shapes=[pltpu.VMEM((tm, D), f32)],
        compiler_params=_cp(("arbitrary", "arbitrary")))(dproj, wfull, xin, dout, g_pre, *extra)


def adam_update(parts, own, me, w, m, v, l, acc, name):
    lw, r, c = w.shape
    tr = _row_tile(r)
    nslots = parts.shape[0]

    def body(me_ref, p_ref, own_ref, w_ref, m_ref, v_ref, *rest):
        g_ref, d_ref, nm_ref, nv_ref = rest[-4:]
        g = None
        for s in range(nslots):
            part = jnp.where(me_ref[0] == s, own_ref[0], p_ref[s]).astype(f32)
            g = part if g is None else g + part
        nm = ADAM_B1 * m_ref[0] + (1.0 - ADAM_B1) * g
        nv = ADAM_B2 * v_ref[0] + (1.0 - ADAM_B2) * (g * g)
        m_hat = nm / (1.0 - ADAM_B1 ** ADAM_STEP)
        v_hat = nv / (1.0 - ADAM_B2 ** ADAM_STEP)
        g_ref[0] = g
        d_ref[0] = -ADAM_LR * (m_hat / (jnp.sqrt(v_hat) + ADAM_EPS) + ADAM_WD * w_ref[0])
        nm_ref[0] = nm
        nv_ref[0] = nv

    blk = pl.BlockSpec((1, tr, c), lambda i, me_ref: (l, i, 0))
    sd = jax.ShapeDtypeStruct((lw, r, c), f32)
    extra = [] if acc is None else list(acc)
    return pl.pallas_call(
        body, name=name, out_shape=[sd] * 4,
        grid_spec=pltpu.PrefetchScalarGridSpec(
            num_scalar_prefetch=1, grid=(r // tr,),
            in_specs=[pl.BlockSpec((nslots, tr, c), lambda i, me_ref: (0, i, 0)),
                      pl.BlockSpec((1, tr, c), lambda i, me_ref: (me_ref[0], i, 0)), blk, blk, blk]
            + [pl.BlockSpec(memory_space=pl.ANY)] * len(extra),
            out_specs=[blk] * 4),
        input_output_aliases={6 + t: t for t in range(len(extra))},
        compiler_params=_cp(("parallel",)))(me, parts, own, w, m, v, *extra)


def _rope_tables():
    t = jnp.arange(S)
    rows, cols = (t // GRID_W).astype(f32), (t % GRID_W).astype(f32)
    nf = HD // 4
    inv = ROPE_THETA ** (-jnp.arange(nf, dtype=f32) / nf)
    ar, ac = rows[:, None] * inv, cols[:, None] * inv
    cos = jnp.concatenate([jnp.cos(ar), jnp.cos(ar), jnp.cos(ac), jnp.cos(ac)], axis=1)
    sin = jnp.concatenate([-jnp.sin(ar), jnp.sin(ar), -jnp.sin(ac), jnp.sin(ac)], axis=1)
    return cos, sin


def _pack_conv(ca, cc):
    z = lambda n: jnp.zeros((L, n, CT), f32)
    return jnp.concatenate([ca, z(5), cc, z(1)], axis=1)


def _pack_small(npre, npost, ccb, lng, lnb, qn, kn):
    wide = lambda a: jnp.pad(a, ((0, 0), (0, D - HD)))
    return jnp.stack([npre, npost, ccb, lng, lnb, wide(qn), wide(kn), jnp.zeros((L, D), f32)], axis=1).reshape(L * 8, D)


def kernel(x, norm_pre, norm_post, w_in, conv_a_w, q_norm, k_norm, conv_c_w, conv_c_b, ln_c_g, ln_c_b, w_out_a, w_out_b, w_out_c, w_o, loss_target, m_norm_pre, m_norm_post, m_w_in, m_conv_a_w, m_q_norm, m_k_norm, m_conv_c_w, m_conv_c_b, m_ln_c_g, m_ln_c_b, m_w_out_a, m_w_out_b, m_w_out_c, m_w_o, v_norm_pre, v_norm_post, v_w_in, v_conv_a_w, v_q_norm, v_k_norm, v_conv_c_w, v_conv_c_b, v_ln_c_g, v_ln_c_b, v_w_out_a, v_w_out_b, v_w_out_c, v_w_o):
    cos, sin = _rope_tables()
    rs = D // NDEV
    stack_sq = lambda a, b, c, d: jnp.stack([a, b, c, d], axis=1)
    wsq32 = stack_sq(w_out_a, w_out_b, w_out_c, w_o)
    conv_pack = _pack_conv(conv_a_w, conv_c_w)
    vec = lambda a, l: a[l][None, :]
    me = (4 * lax.axis_index("x") + 2 * lax.axis_index("y") + lax.axis_index("c")).astype(jnp.int32).reshape(1)

    def gather_start(l, after):
        return split_start(staged[l], _gather_copies, 12, f"ag_start{l}", after=after)

    def forward_start(l, after):
        s_sems, r_sems, bufs, _ = gathers[l]
        bufs = split_wait(s_sems, r_sems, bufs, _gather_copies, after, f"ag_wait{l}")
        fw = split_start(bufs, _forward_copies, 9, f"ag_fwd_start{l}")
        if l + 1 < L:
            gathers[l + 1] = gather_start(l + 1, fw[3])
            return fw, gathers[l + 1][3]
        return fw, fw[3]

    def forward_wait(fw, after, l):
        s_sems, r_sems, bufs, _ = fw
        return split_wait(s_sems, r_sems, bufs, _forward_copies, after, f"ag_fwd_wait{l}")

    wt, m_wt, v_wt = (jnp.swapaxes(a, 1, 2) for a in (w_in, m_w_in, v_w_in))
    xs, saved = x.reshape(S, D), []
    stage = lambda l, after: stage_shards(wt[l], wsq32[l].reshape(4 * rs, D), conv_pack[l], me, f"stage{l}", after)
    staged = [stage(0, None)]
    gathers = [gather_start(0, None)] + [None] * (L - 1)
    staged += [stage(l, gathers[0][3]) for l in range(1, L)]
    fw, issued = forward_start(0, staged[L - 1][0] if L > 1 else xs)
    wg, wsq, convw = forward_wait(fw, issued, 0)
    for l in range(L):
        wsq = wsq.reshape(NDEV, 4, rs, D)
        wfull = wg.reshape(P, D)
        proj, h = proj_fwd(xs, vec(norm_pre, l), wfull, f"proj{l}")
        yah = brancha_fwd(proj, convw, f"bra{l}")
        u1 = branchc1_fwd(proj, convw, vec(conv_c_b, l), f"brc1_{l}")
        qh, kh, vh = qkv_fwd(proj, vec(q_norm, l), vec(k_norm, l), cos, sin, f"qkv{l}")
        o, ybh = attn_fwd(qh, kh, vh, proj, f"attn{l}")
        issued = None
        if l + 1 < L:
            fw, issued = forward_start(l + 1, o)
        ych = branchc2_fwd(u1, proj, vec(ln_c_g, l), vec(ln_c_b, l), f"brc2_{l}", after=issued)
        ya, yb, yc, y16, z, xo = merge_fwd(xs, yah, ybh, ych, proj, wsq, vec(norm_post, l), f"merge{l}")
        saved.append(dict(x=xs, wfull=wfull, wsq=wsq, convw=convw, proj=proj, h=h, yah=yah, ybh=ybh, ych=ych, u1=u1,
                          qh=qh, kh=kh, vh=vh, o=o, ya=ya, yb=yb, yc=yc, y16=y16, z=z))
        xs = xo
        if l + 1 < L:
            wg, wsq, convw = forward_wait(fw, xs, l + 1)
    dx, loss_part = loss_fwd(xs, loss_target.reshape(S, D), "loss")
    loss = lax.psum(loss_part[0, 0], ("x", "y", "c"))

    acc = dict(win=None, sq=None, conv=None)
    small_parts = [None] * L
    msq32 = stack_sq(m_w_out_a, m_w_out_b, m_w_out_c, m_w_o)
    vsq32 = stack_sq(v_w_out_a, v_w_out_b, v_w_out_c, v_w_o)
    mconv, vconv = _pack_conv(m_conv_a_w, m_conv_c_w), _pack_conv(v_conv_a_w, v_conv_c_w)

    def scatter_start(parts, name, after=None):
        bufs = parts + [lax.empty(p.shape, p.dtype) for p in parts]
        return split_start(bufs, _scatter_copies(len(parts)), 7 * len(parts), name, after=after)

    def finish(l, started, after):
        (s1, r1, b1, _), (s2, r2, b2, _) = started
        gsq_own, rsq = split_wait(s1, r1, b1, _scatter_copies(1), after, f"rs_sq_wait{l}")
        gwin_own, gconv_own, rwin, rconv = split_wait(s2, r2, b2, _scatter_copies(2), after, f"rs_win_wait{l}")
        flat = lambda a: a.reshape(a.shape[0], 4 * rs, D)
        acc["win"] = adam_update(rwin, gwin_own, me, wt, m_wt, v_wt, l, acc["win"], f"adam_win{l}")
        acc["sq"] = adam_update(flat(rsq), flat(gsq_own), me, flat(wsq32), flat(msq32), flat(vsq32), l, acc["sq"], f"adam_wsq{l}")
        acc["conv"] = adam_update(rconv, gconv_own, me, conv_pack, mconv, vconv, l, acc["conv"], f"adam_conv{l}")

    pending = [None] * L
    for l in reversed(range(L)):
        sv = saved[l]
        proj = sv["proj"]
        (dyah, dybh, dych, dzb, dyab, dybb, dycb, dgpost, dproj) = merge_bwd(
            dx, sv["z"], sv["ya"], sv["yb"], sv["yc"], proj, sv["wsq"], vec(norm_post, l), f"merge_bwd{l}")
        gsq = [tn_matmul(a, b, f"dwsq{t}_{l}") for t, (a, b) in enumerate(
            ((sv["yah"], dyab), (sv["ybh"], dybb), (sv["ych"], dycb), (sv["y16"], dzb)))]
        gsq_parts = jnp.stack([g.reshape(NDEV, rs, D) for g in gsq], axis=1)
        st1 = scatter_start([gsq_parts], f"rs_sq_start{l}", after=loss.reshape(1, 1) if l == L - 1 else None)
        convw = sv["convw"]
        gca, dproj = brancha_bwd(dyah, proj, convw, dproj, f"bra_bwd{l}", after=st1[3])
        du1, dlg, dlb, dcb, dproj = branchc2_bwd(dych, sv["u1"], proj, vec(ln_c_g, l), vec(ln_c_b, l), dproj, f"brc2_bwd{l}")
        gcc, dproj = branchc1_bwd(du1, proj, convw, dproj, f"brc1_bwd{l}")
        dqh, dkh, dvh, dproj = attn_bwd(dybh, sv["o"], sv["qh"], sv["kh"], sv["vh"], proj, dproj, f"attn_bwd{l}")
        dqn, dkn, dproj = qkv_bwd(dqh, dkh, dvh, proj, vec(q_norm, l), vec(k_norm, l), cos, sin, dproj, f"qkv_bwd{l}")
        gwin = dwin_parts(sv["h"], dproj, f"dwin{l}").reshape(NDEV, PSH, D)
        gconv = jnp.concatenate([gca, gcc], axis=1)
        if l > 0:
            st2 = scatter_start([gwin, gconv], f"rs_win_start{l}")
            issued = st2[3]
        else:
            st2 = scatter_start([gconv], "rs_conv_start0")
            pair = split_start([gwin, lax.empty((NDEV // 2, PSH, D), bf16)], _pair_copies, NDEV // 2, "rs_pair_start0",
                               after=st2[3])
            issued = pair[3]
        dx, dgpre = dh_bwd(dproj, sv["wfull"], sv["x"], dx, vec(norm_pre, l), f"dh{l}", after=issued)
        wide = lambda a: jnp.pad(a, ((0, 0), (0, D - HD)))
        small_parts[l] = jnp.concatenate([dgpre, dgpost, dcb, dlg, dlb, wide(dqn), wide(dkn), jnp.zeros((1, D), f32)], axis=0)
        pending[l] = (st1, st2)

    gwin0, pair_land = split_wait(pair[0], pair[1], pair[2], _pair_copies, dx, "rs_pair_wait0")
    summed = pair_sum(gwin0, pair_land, me, "rs_pair_sum0")
    (small_all,) = all_gather([jnp.concatenate(small_parts, axis=0)], "ag_small")
    chip = split_start([summed, lax.empty(summed.shape, bf16)], _chip_copies, NDEV // 2 - 1, "rs_chip_start0", after=small_all)
    for l in reversed(range(1, L)):
        finish(l, pending[l], after=chip[3])
    sm = adam_update(small_all, small_all, me,
                     _pack_small(norm_pre, norm_post, conv_c_b, ln_c_g, ln_c_b, q_norm, k_norm)[None],
                     _pack_small(m_norm_pre, m_norm_post, m_conv_c_b, m_ln_c_g, m_ln_c_b, m_q_norm, m_k_norm)[None],
                     _pack_small(v_norm_pre, v_norm_post, v_conv_c_b, v_ln_c_g, v_ln_c_b, v_q_norm, v_k_norm)[None],
                     0, None, "adam_small")
    (s1, r1, b1, _), (s2, r2, b2, _) = pending[0]
    gsq_own, rsq = split_wait(s1, r1, b1, _scatter_copies(1), sm[0], "rs_sq_wait0")
    gconv_own, rconv = split_wait(s2, r2, b2, _scatter_copies(1), sm[0], "rs_conv_wait0")
    flat = lambda a: a.reshape(a.shape[0], 4 * rs, D)
    acc["sq"] = adam_update(flat(rsq), flat(gsq_own), me, flat(wsq32), flat(msq32), flat(vsq32), 0, acc["sq"], "adam_wsq0")
    acc["conv"] = adam_update(rconv, gconv_own, me, conv_pack, mconv, vconv, 0, acc["conv"], "adam_conv0")
    summed, chip_land = split_wait(chip[0], chip[1], chip[2], _chip_copies, acc["sq"][0], "rs_chip_wait0")
    acc["win"] = adam_update(chip_land, summed, me // 2, wt, m_wt, v_wt, 0, acc["win"], "adam_win0")
    sm = [a.reshape(L, 8, D) for a in sm]
    small_rows = dict(norm_pre=(0, D), norm_post=(1, D), conv_c_b=(2, D), ln_c_g=(3, D), ln_c_b=(4, D), q_norm=(5, HD), k_norm=(6, HD))
    sq_rows = dict(w_out_a=0, w_out_b=1, w_out_c=2, w_o=3)

    order = ["norm_pre", "norm_post", "w_in", "conv_a_w", "q_norm", "k_norm", "conv_c_w", "conv_c_b", "ln_c_g", "ln_c_b",
             "w_out_a", "w_out_b", "w_out_c", "w_o"]
    result = [loss, dx.reshape(1, S, D)]
    for kind in range(4):
        for nme in order:
            if nme in small_rows:
                rw, wd = small_rows[nme]
                result.append(sm[kind][:, rw, :wd])
            elif nme in sq_rows:
                result.append(acc["sq"][kind][:, sq_rows[nme] * rs:(sq_rows[nme] + 1) * rs])
            elif nme == "w_in":
                result.append(jnp.swapaxes(acc["win"][kind], 1, 2))
            elif nme == "conv_a_w":
                result.append(acc["conv"][kind][:, 0:CA_W])
            else:
                result.append(acc["conv"][kind][:, 8:8 + CC_W])
    return tuple(result)
```

```python
import math

import jax
import jax.numpy as jnp
from jax import lax
from jax.experimental import pallas as pl
from jax.experimental.pallas import tpu as pltpu

f32, bf16 = jnp.float32, jnp.bfloat16

D = 1024
S = 2048
L = 4
HD = 128
NQ = D // HD
NKV = NQ // 4
G = NQ // NKV
WKV = NKV * HD
GRID_W = 64
ROPE_THETA = 10000.0
RMS_EPS = 1e-6
LN_EPS = 1e-5
NDEV = 8
CA_W, CC_W = 3, 31
P = 12 * D + 2 * WKV
PSH = P // NDEV
CT = 128
ADAM_LR, ADAM_B1, ADAM_B2, ADAM_EPS, ADAM_WD, ADAM_STEP = 0.001, 0.9, 0.999, 1e-08, 0.01, 10
VMEM_LIMIT = 56 * 1024 * 1024
MESH = pl.DeviceIdType.MESH

_OFF = {}
_o = 0
for _n, _w in (("a_b", D), ("a_c", D), ("a_x", D), ("a_g", D), ("q", D), ("k", WKV), ("v", WKV), ("b_g", D),
               ("c_u", D), ("c_v", D), ("c_g", D), ("m_a", D), ("m_b", D), ("m_c", D)):
    _OFF[_n] = (_o, _w)
    _o += _w
PIECES = tuple(_OFF)


def _cp(sem=None, **kw):
    return pltpu.CompilerParams(dimension_semantics=sem, vmem_limit_bytes=VMEM_LIMIT, **kw)


def _sig(x):
    return 1.0 / (1.0 + jnp.exp(-x))


def _silu(x):
    return x * _sig(x)


def _dsilu(x):
    s = _sig(x)
    return s * (1.0 + x * (1.0 - s))


def _row_specs(name, tm):
    off, w = _OFF[name]
    bw = math.gcd(off, w) if off else w
    return [pl.BlockSpec((tm, bw), (lambda i, *_, b=off // bw + t: (i, b))) for t in range(w // bw)]


def _cat(refs):
    return refs[0][...] if len(refs) == 1 else jnp.concatenate([r[...] for r in refs], axis=1)


def _chan_spec(name):
    off, _ = _OFF[name]
    return pl.BlockSpec((S, CT), lambda j, b=off // CT: (0, b + j))


def _full(shape):
    return pl.BlockSpec(shape, lambda *_: (0,) * len(shape))


def _coords():
    return lax.axis_index("x"), lax.axis_index("y"), lax.axis_index("c")


def all_gather(shards, name):
    n = len(shards)

    def body(*refs):
        ins, outs = refs[:n], refs[n:2 * n]
        send_sems, recv_sems, local_sems = refs[2 * n:]
        x, y, c = _coords()
        me, sibling = (x, y, c), (x, y, 1 - c)
        chips = [(1 - x, y), (x, 1 - y), (1 - x, 1 - y)]

        def slot(a, p):
            return outs[a].at[4 * p[0] + 2 * p[1] + p[2]]

        def copy(a, k, block, to, src=None):
            return pltpu.make_async_remote_copy(
                src_ref=slot(a, block) if src is None else src, dst_ref=slot(a, block),
                send_sem=send_sems.at[7 * a + k], recv_sem=recv_sems.at[7 * a + k], device_id=to, device_id_type=MESH)

        mine = [pltpu.make_async_copy(ins[a], slot(a, me), local_sems.at[a]) for a in range(n)]
        for cp in mine:
            cp.start()
        first = []
        for a in range(n):
            first.append(copy(a, 0, me, sibling, src=ins[a]))
            first += [copy(a, 1 + j, me, (*chip, c), src=ins[a]) for j, chip in enumerate(chips)]
        for cp in first:
            cp.start()
        passed = []
        for j, chip in enumerate(chips):
            for a in range(n):
                copy(a, 1 + j, (*chip, c), me).wait_recv()
                fw = copy(a, 4 + j, (*chip, c), sibling)
                fw.start()
                passed.append(fw)
        for a in range(n):
            copy(a, 0, sibling, me).wait_recv()
            for j, chip in enumerate(chips):
                copy(a, 4 + j, (*chip, 1 - c), me).wait_recv()
        for cp in first + passed:
            cp.wait_send()
        for cp in mine:
            cp.wait()

    anyspec = pl.BlockSpec(memory_space=pl.ANY)
    return pl.pallas_call(
        body, name=name,
        out_shape=[jax.ShapeDtypeStruct((NDEV,) + s.shape, s.dtype) for s in shards],
        in_specs=[anyspec] * n, out_specs=[anyspec] * n,
        scratch_shapes=[pltpu.SemaphoreType.DMA((7 * n,)), pltpu.SemaphoreType.DMA((7 * n,)), pltpu.SemaphoreType.DMA((n,))],
    )(*shards)


_HBM = pl.BlockSpec(memory_space=pltpu.HBM)
_SEM = pl.BlockSpec(memory_space=pltpu.SEMAPHORE)
_EFFECT = pltpu.SideEffectType.DATAFLOW_SIDE_EFFECTING


def split_start(bufs, make_copies, nsem, name, after=None):
    n = len(bufs)
    extra = [] if after is None else [after]

    def body(*refs):
        send_sems, recv_sems = refs[n + len(extra):n + len(extra) + 2]
        for cp in make_copies(refs[:n], send_sems, recv_sems):
            cp.start()
        refs[-1][...] = jnp.zeros((8, 128), f32)

    res = pl.pallas_call(
        body, name=name,
        out_shape=(pltpu.SemaphoreType.DMA((nsem,)), pltpu.SemaphoreType.DMA((nsem,)),
                   *[pltpu.HBM(b.shape, b.dtype) for b in bufs], jax.ShapeDtypeStruct((8, 128), f32)),
        in_specs=[_HBM] * n + [pl.BlockSpec(memory_space=pl.ANY)] * len(extra),
        out_specs=(_SEM, _SEM, *([_HBM] * n), pl.BlockSpec(memory_space=pltpu.VMEM)),
        input_output_aliases={i: 2 + i for i in range(n)},
        compiler_params=pltpu.CompilerParams(has_side_effects=_EFFECT),
    )(*[pltpu.with_memory_space_constraint(b, pltpu.HBM) for b in bufs], *extra)
    return res[0], res[1], list(res[2:2 + n]), res[-1]


def split_wait(send_sems, recv_sems, bufs, make_copies, after, name):
    n = len(bufs)

    def body(*refs):
        for cp in make_copies(refs[:n], refs[n], refs[n + 1]):
            cp.wait_send()
            cp.wait_recv()

    res = pl.pallas_call(
        body, name=name,
        out_shape=tuple(pltpu.HBM(b.shape, b.dtype) for b in bufs),
        in_specs=[_HBM] * n + [_SEM, _SEM, pl.BlockSpec(memory_space=pl.ANY)],
        out_specs=[_HBM] * n,
        input_output_aliases={i: i for i in range(n)},
        compiler_params=pltpu.CompilerParams(has_side_effects=_EFFECT),
    )(*bufs, send_sems, recv_sems, after)
    return list(res)


def _scatter_copies(n):
    def make(refs, send_sems, recv_sems):
        x, y, c = _coords()
        me = 4 * x + 2 * y + c
        copies = []
        for a in range(n):
            for k in range(1, NDEV):
                px = 1 - x if (k >> 2) & 1 else x
                py = 1 - y if (k >> 1) & 1 else y
                pc = 1 - c if k & 1 else c
                copies.append(pltpu.make_async_remote_copy(
                    src_ref=refs[a].at[4 * px + 2 * py + pc], dst_ref=refs[n + a].at[me],
                    send_sem=send_sems.at[7 * a + k - 1], recv_sem=recv_sems.at[7 * a + k - 1],
                    device_id=(px, py, pc), device_id_type=MESH))
        return copies
    return make


def _gather_copies(refs, send_sems, recv_sems):
    x, y, c = _coords()
    me = 4 * x + 2 * y + c
    targets = [(x, y, 1 - c), (1 - x, y, c), (x, 1 - y, c), (1 - x, 1 - y, c)]
    return [pltpu.make_async_remote_copy(
        src_ref=r.at[me], dst_ref=r.at[me], send_sem=send_sems.at[4 * a + k], recv_sem=recv_sems.at[4 * a + k],
        device_id=to, device_id_type=MESH) for a, r in enumerate(refs) for k, to in enumerate(targets)]


def _forward_copies(refs, send_sems, recv_sems):
    x, y, c = _coords()
    chips = [(1 - x, y), (x, 1 - y), (1 - x, 1 - y)]
    return [pltpu.make_async_remote_copy(
        src_ref=r.at[4 * px + 2 * py + c], dst_ref=r.at[4 * px + 2 * py + c], send_sem=send_sems.at[3 * a + j],
        recv_sem=recv_sems.at[3 * a + j], device_id=(x, y, 1 - c), device_id_type=MESH)
        for a, r in enumerate(refs) for j, (px, py) in enumerate(chips)]


def _pair_copies(refs, send_sems, recv_sems):
    x, y, c = _coords()
    parts, land = refs
    return [pltpu.make_async_remote_copy(
        src_ref=parts.at[2 * j + 1 - c], dst_ref=land.at[j], send_sem=send_sems.at[j], recv_sem=recv_sems.at[j],
        device_id=(x, y, 1 - c), device_id_type=MESH) for j in range(NDEV // 2)]


def _chip_copies(refs, send_sems, recv_sems):
    x, y, c = _coords()
    summed, land = refs
    copies = []
    for k in range(1, NDEV // 2):
        px = 1 - x if (k >> 1) & 1 else x
        py = 1 - y if k & 1 else y
        copies.append(pltpu.make_async_remote_copy(
            src_ref=summed.at[2 * px + py], dst_ref=land.at[2 * x + y], send_sem=send_sems.at[k - 1],
            recv_sem=recv_sems.at[k - 1], device_id=(px, py, c), device_id_type=MESH))
    return copies


def pair_sum(parts, land, me, name):
    _, r, c = parts.shape
    tr = _row_tile(r)

    def body(me_ref, a_ref, b_ref, o_ref):
        o_ref[...] = (a_ref[...].astype(f32) + b_ref[...].astype(f32)).astype(bf16)

    blk = pl.BlockSpec((1, tr, c), lambda j, i, m: (j, i, 0))
    return pl.pallas_call(
        body, name=name, out_shape=jax.ShapeDtypeStruct((NDEV // 2, r, c), bf16),
        grid_spec=pltpu.PrefetchScalarGridSpec(
            num_scalar_prefetch=1, grid=(NDEV // 2, r // tr),
            in_specs=[pl.BlockSpec((1, tr, c), lambda j, i, m: (2 * j + m[0] % 2, i, 0)), blk], out_specs=blk),
        compiler_params=_cp(("parallel", "parallel")))(me, parts, land)


def _row_tile(r):
    return r if r <= 256 else max(t for t in (256, 160, 128) if r % t == 0)


def stage_shards(wt, wsq, conv, l, me, name, after=None):
    outs = []
    extra = [] if after is None else [after]
    for a, dt in ((wt, bf16), (wsq, bf16), (conv, f32)):
        _, r, c = a.shape
        tr = _row_tile(r)

        def body(me_ref, a_ref, *rest):
            rest[-1][...] = a_ref[...].astype(rest[-1].dtype)

        outs.append(pl.pallas_call(
            body, name=f"{name}_{len(outs)}", out_shape=jax.ShapeDtypeStruct((NDEV, r, c), dt),
            grid_spec=pltpu.PrefetchScalarGridSpec(
                num_scalar_prefetch=1, grid=(r // tr,),
                in_specs=[pl.BlockSpec((1, tr, c), lambda i, m: (l, i, 0))] + [pl.BlockSpec(memory_space=pl.ANY)] * len(extra),
                out_specs=pl.BlockSpec((1, tr, c), lambda i, m: (m[0], i, 0))),
            compiler_params=_cp(("arbitrary",)))(me, a, *extra))
    return outs


def proj_fwd(xin, g_pre, wt, name):
    tm, tn = min(S, 1024), 2560

    def body(x_ref, g_ref, w_ref, proj_ref, h_ref, hs):
        @pl.when(pl.program_id(1) == 0)
        def _():
            x = x_ref[...]
            r = lax.rsqrt(jnp.mean(x * x, axis=-1, keepdims=True) + RMS_EPS)
            h = (x * r * g_ref[...]).astype(bf16)
            hs[...] = h
            h_ref[...] = h
        proj_ref[...] = lax.dot_general(hs[...], w_ref[...], (((1,), (1,)), ((), ())), preferred_element_type=f32)

    return pl.pallas_call(
        body, name=name, grid=(S // tm, P // tn),
        out_shape=[jax.ShapeDtypeStruct((S, P), f32), jax.ShapeDtypeStruct((S, D), bf16)],
        in_specs=[pl.BlockSpec((tm, D), lambda i, j: (i, 0)), _full((1, D)), pl.BlockSpec((tn, D), lambda i, j: (j, 0))],
        out_specs=[pl.BlockSpec((tm, tn), lambda i, j: (i, j)), pl.BlockSpec((tm, D), lambda i, j: (i, 0))],
        scratch_shapes=[pltpu.VMEM((tm, D), bf16)],
        compiler_params=_cp(("parallel", "arbitrary")))(xin, g_pre, wt)


RC = 256


def _fill_pad(pad, halo, val_fn):
    pad[0:halo, :] = jnp.zeros((halo, CT), f32)
    pad[S + halo:S + 2 * halo, :] = jnp.zeros((halo, CT), f32)

    def step(i, carry):
        rows = pl.ds(pl.multiple_of(i * RC, RC), RC)
        pad[pl.ds(pl.multiple_of(i * RC, RC) + halo, RC), :] = val_fn(rows)
        return carry
    lax.fori_loop(0, S // RC, step, 0)


def brancha_fwd(proj, convw, name):
    def body(ab, ac, ax, ag, w_ref, o_ref, pad):
        _fill_pad(pad, 8, lambda rows: ac[rows, :] * ax[rows, :])
        w = [w_ref[0, k:k + 1, :] for k in range(CA_W)]

        def step(i, carry):
            base = pl.multiple_of(i * RC, RC)
            rows = pl.ds(base, RC)
            t = sum(w[k] * pad[pl.ds(base + 7 + k, RC), :] for k in range(CA_W))
            o_ref[rows, :] = (ab[rows, :] * t * _silu(ag[rows, :])).astype(bf16)
            return carry
        lax.fori_loop(0, S // RC, step, 0)

    return pl.pallas_call(
        body, name=name, grid=(D // CT,), out_shape=jax.ShapeDtypeStruct((S, D), bf16),
        in_specs=[_chan_spec("a_b"), _chan_spec("a_c"), _chan_spec("a_x"), _chan_spec("a_g"),
                  pl.BlockSpec((1, 40, CT), lambda j: (j, 0, 0))],
        out_specs=pl.BlockSpec((S, CT), lambda j: (0, j)),
        scratch_shapes=[pltpu.VMEM((S + 16, CT), f32)],
        compiler_params=_cp(("parallel",)))(proj, proj, proj, proj, convw)


def branchc1_fwd(proj, convw, cbias, name):
    def body(cu, cv, w_ref, b_ref, o_ref, pad):
        _fill_pad(pad, 16, lambda rows: cu[rows, :] * _sig(cv[rows, :]))

        def step(i, carry):
            base = pl.multiple_of(i * RC, RC)
            acc = jnp.zeros((RC, CT), f32) + b_ref[...]
            for k in range(CC_W):
                acc = acc + w_ref[0, 8 + k:9 + k, :] * pad[pl.ds(base + k + 1, RC), :]
            o_ref[pl.ds(base, RC), :] = acc
            return carry
        lax.fori_loop(0, S // RC, step, 0)

    return pl.pallas_call(
        body, name=name, grid=(D // CT,), out_shape=jax.ShapeDtypeStruct((S, D), f32),
        in_specs=[_chan_spec("c_u"), _chan_spec("c_v"), pl.BlockSpec((1, 40, CT), lambda j: (j, 0, 0)),
                  pl.BlockSpec((1, CT), lambda j: (0, j))],
        out_specs=pl.BlockSpec((S, CT), lambda j: (0, j)),
        scratch_shapes=[pltpu.VMEM((S + 32, CT), f32)],
        compiler_params=_cp(("parallel",)))(proj, proj, convw, cbias)


def _swap32(x):
    lane = lax.broadcasted_iota(jnp.int32, x.shape, 1)
    return jnp.where((lane // 32) % 2 == 1, pltpu.roll(x, 32, 1), pltpu.roll(x, HD - 32, 1))


def _rope(y, cos, sin):
    return y * cos + _swap32(y) * sin


def qkv_fwd(proj, qn, kn, cos, sin, name):
    tm = min(S, 512)
    nq, nk, nv = len(_row_specs("q", tm)), len(_row_specs("k", tm)), len(_row_specs("v", tm))

    def body(*refs):
        q = _cat(refs[:nq])
        k = _cat(refs[nq:nq + nk])
        v = _cat(refs[nq + nk:nq + nk + nv])
        qn_ref, kn_ref, cos_ref, sin_ref, qh_ref, kh_ref, vh_ref = refs[nq + nk + nv:]
        cos, sin = cos_ref[...], sin_ref[...]

        def heads(xx, gn, out_ref, n):
            for h in range(n):
                xh = xx[:, h * HD:(h + 1) * HD]
                r = lax.rsqrt(jnp.mean(xh * xh, axis=-1, keepdims=True) + RMS_EPS)
                out_ref[:, h * HD:(h + 1) * HD] = _rope(xh * r * gn, cos, sin).astype(bf16)
        heads(q, qn_ref[...], qh_ref, NQ)
        heads(k, kn_ref[...], kh_ref, NKV)
        vh_ref[...] = v.astype(bf16)

    row = lambda w: pl.BlockSpec((tm, w), lambda i: (i, 0))
    return pl.pallas_call(
        body, name=name, grid=(S // tm,),
        out_shape=[jax.ShapeDtypeStruct((S, D), bf16), jax.ShapeDtypeStruct((S, WKV), bf16), jax.ShapeDtypeStruct((S, WKV), bf16)],
        in_specs=_row_specs("q", tm) + _row_specs("k", tm) + _row_specs("v", tm) + [_full((1, HD)), _full((1, HD)), row(HD), row(HD)],
        out_specs=[row(D), row(WKV), row(WKV)],
        compiler_params=_cp(("parallel",)))(*([proj] * (nq + nk + nv)), qn, kn, cos, sin)


def _softmax_rows(q, k):
    s = lax.dot_general(q, k, (((1,), (1,)), ((), ())), preferred_element_type=f32)
    p = jnp.exp((s - jnp.max(s, axis=-1, keepdims=True)) * (HD ** -0.5))
    return p, 1.0 / jnp.sum(p, axis=-1, keepdims=True)


GW = G * HD


def attn_fwd(qh, kh, vh, proj, name):
    tq = min(S, 512)
    bg_blk = _OFF["b_g"][0] // GW

    def body(q_ref, k_ref, v_ref, bg_ref, o_ref, y_ref):
        k, v = k_ref[...], v_ref[...]
        for g in range(G):
            cols = slice(g * HD, (g + 1) * HD)
            p, rl = _softmax_rows(q_ref[:, cols], k)
            o = jnp.dot(p.astype(bf16), v, preferred_element_type=f32) * rl
            o_ref[:, cols] = o
            y_ref[:, cols] = (o * _silu(bg_ref[:, cols])).astype(bf16)

    grp = pl.BlockSpec((tq, GW), lambda kv, i: (i, kv))
    kvs = pl.BlockSpec((S, HD), lambda kv, i: (0, kv))
    return pl.pallas_call(
        body, name=name, grid=(NKV, S // tq),
        out_shape=[jax.ShapeDtypeStruct((S, D), f32), jax.ShapeDtypeStruct((S, D), bf16)],
        in_specs=[grp, kvs, kvs, pl.BlockSpec((tq, GW), lambda kv, i: (i, bg_blk + kv))],
        out_specs=[grp, grp],
        compiler_params=_cp(("parallel", "parallel")))(qh, kh, vh, proj)


def _ln_parts(u1):
    mu = jnp.mean(u1, axis=-1, keepdims=True)
    xc = u1 - mu
    rstd = lax.rsqrt(jnp.mean(xc * xc, axis=-1, keepdims=True) + LN_EPS)
    return xc * rstd, rstd


def _after(after):
    return ([], []) if after is None else ([after], [pl.BlockSpec(memory_space=pl.ANY)])


def branchc2_fwd(u1, proj, lng, lnb, name, after=None):
    tm = min(S, 512)
    ncg = len(_row_specs("c_g", tm))
    extra, extra_specs = _after(after)

    def body(*refs):
        u_ref = refs[0]
        cg = _cat(refs[1:1 + ncg])
        g_ref, b_ref = refs[1 + ncg:3 + ncg]
        xh, _ = _ln_parts(u_ref[...])
        refs[-1][...] = (_silu(xh * g_ref[...] + b_ref[...]) * _silu(cg)).astype(bf16)

    row = pl.BlockSpec((tm, D), lambda i: (i, 0))
    return pl.pallas_call(
        body, name=name, grid=(S // tm,), out_shape=jax.ShapeDtypeStruct((S, D), bf16),
        in_specs=[row] + _row_specs("c_g", tm) + [_full((1, D)), _full((1, D))] + extra_specs, out_specs=row,
        compiler_params=_cp(("parallel",)))(u1, *([proj] * ncg), lng, lnb, *extra)


def _wmat(w_ref, kind):
    return w_ref[:, kind].reshape(D, D)


def merge_fwd(xin, yah, ybh, ych, proj, wsq, g_post, name):
    tm = min(S, 512)
    nm = len(_row_specs("m_a", tm))

    def body(*refs):
        x_ref, a_ref, b_ref, c_ref = refs[:4]
        ms = [_cat(refs[4 + t * nm:4 + (t + 1) * nm]) for t in range(3)]
        w_ref, g_ref, ya_ref, yb_ref, yc_ref, y_ref, z_ref, o_ref = refs[4 + 3 * nm:]
        y = jnp.zeros((tm, D), f32)
        for t, (h_ref, out_ref) in enumerate(((a_ref, ya_ref), (b_ref, yb_ref), (c_ref, yc_ref))):
            yt = jnp.dot(h_ref[...], _wmat(w_ref, t), preferred_element_type=f32)
            out_ref[...] = yt
            y = y + _sig(ms[t]) * yt
        yb16 = y.astype(bf16)
        y_ref[...] = yb16
        z = jnp.dot(yb16, _wmat(w_ref, 3), preferred_element_type=f32)
        z_ref[...] = z
        r = lax.rsqrt(jnp.mean(z * z, axis=-1, keepdims=True) + RMS_EPS)
        o_ref[...] = x_ref[...] + z * r * g_ref[...]

    row = pl.BlockSpec((tm, D), lambda i: (i, 0))
    sd = lambda dt: jax.ShapeDtypeStruct((S, D), dt)
    return pl.pallas_call(
        body, name=name, grid=(S // tm,),
        out_shape=[sd(f32), sd(f32), sd(f32), sd(bf16), sd(f32), sd(f32)],
        in_specs=[row] * 4 + _row_specs("m_a", tm) + _row_specs("m_b", tm) + _row_specs("m_c", tm)
        + [pl.BlockSpec((NDEV, 4, D // NDEV, D), lambda *_: (0, 0, 0, 0), pipeline_mode=pl.Buffered(1)), _full((1, D))],
        out_specs=[row] * 6,
        compiler_params=_cp(("parallel",)))(xin, yah, ybh, ych, *([proj] * (3 * nm)), wsq, g_post)


def loss_fwd(y, target, name):
    tm = min(S, 256)

    def body(y_ref, t_ref, dy_ref, l_ref):
        e = y_ref[...] - t_ref[...]
        dy_ref[...] = e / D

        @pl.when(pl.program_id(0) == 0)
        def _():
            l_ref[...] = jnp.zeros((1, 128), f32)
        l_ref[...] += (0.5 / D) * jnp.sum(e * e)

    row = pl.BlockSpec((tm, D), lambda i: (i, 0))
    return pl.pallas_call(
        body, name=name, grid=(S // tm,),
        out_shape=[jax.ShapeDtypeStruct((S, D), f32), jax.ShapeDtypeStruct((1, 128), f32)],
        in_specs=[row, row], out_specs=[row, _full((1, 128))],
        compiler_params=_cp(("arbitrary",)))(y, target)


def _acc(ref, val):
    @pl.when(pl.program_id(0) == 0)
    def _():
        ref[...] = jnp.zeros(ref.shape, f32)
    ref[...] += val


def _emit_copies(stash, dst, sems, windows):
    return [pltpu.make_async_copy(stash.at[p], dst.at[w], sems.at[p]) for p, w in enumerate(windows)]


def _emit_drain_previous(copies, step):
    @pl.when(step > 0)
    def _():
        for cp in copies:
            cp.wait()


def _emit_start(copies, step, nsteps):
    for cp in copies:
        cp.start()

    @pl.when(step == nsteps - 1)
    def _():
        for cp in copies:
            cp.wait()


def merge_bwd(dout, z, ya, yb, yc, proj, wsq, g_post, name):
    tm = min(S, 256)
    nm = len(_row_specs("m_a", tm))
    nsteps = S // tm

    def body(*refs):
        do_ref, z_ref, ya_ref, yb_ref, yc_ref = refs[:5]
        ms = [_cat(refs[5 + t * nm:5 + (t + 1) * nm]) for t in range(3)]
        w_ref, g_ref = refs[5 + 3 * nm:7 + 3 * nm]
        dh_refs = refs[7 + 3 * nm:10 + 3 * nm]
        dzb_ref = refs[10 + 3 * nm]
        dyb_refs = refs[11 + 3 * nm:14 + 3 * nm]
        dg_ref = refs[14 + 3 * nm]
        dproj_ref, stash, sems = refs[15 + 3 * nm:]
        i = pl.program_id(0)
        rows = pl.ds(pl.multiple_of(i * tm, tm), tm)
        copies = _emit_copies(stash, dproj_ref, sems, [(rows, pl.ds(_OFF[n][0], D)) for n in ("m_a", "m_b", "m_c")])
        nt = (((1,), (1,)), ((), ()))
        z, dout = z_ref[...], do_ref[...]
        r = lax.rsqrt(jnp.mean(z * z, axis=-1, keepdims=True) + RMS_EPS)
        zh = z * r
        _acc(dg_ref, jnp.sum(dout * zh, axis=0, keepdims=True))
        dzh = dout * g_ref[...]
        dz = (r * (dzh - zh * jnp.mean(dzh * zh, axis=-1, keepdims=True))).astype(bf16)
        dzb_ref[...] = dz
        dy = lax.dot_general(dz, _wmat(w_ref, 3), nt, preferred_element_type=f32)
        dms = []
        for t, yt_ref in enumerate((ya_ref, yb_ref, yc_ref)):
            sg = _sig(ms[t])
            dyt = (dy * sg).astype(bf16)
            dyb_refs[t][...] = dyt
            dms.append((dy * yt_ref[...] * sg * (1.0 - sg)).astype(bf16))
            dh_refs[t][...] = lax.dot_general(dyt, _wmat(w_ref, t), nt, preferred_element_type=f32)
        _emit_drain_previous(copies, i)
        for t in range(3):
            stash[t] = dms[t]
        _emit_start(copies, i, nsteps)

    row = pl.BlockSpec((tm, D), lambda i: (i, 0))
    sd = lambda dt: jax.ShapeDtypeStruct((S, D), dt)
    return pl.pallas_call(
        body, name=name, grid=(nsteps,),
        out_shape=[sd(f32)] * 3 + [sd(bf16)] * 4 + [jax.ShapeDtypeStruct((1, D), f32), jax.ShapeDtypeStruct((S, P), bf16)],
        in_specs=[row] * 5 + _row_specs("m_a", tm) + _row_specs("m_b", tm) + _row_specs("m_c", tm)
        + [pl.BlockSpec((NDEV, 4, D // NDEV, D), lambda *_: (0, 0, 0, 0), pipeline_mode=pl.Buffered(1)), _full((1, D))],
        out_specs=[row] * 7 + [_full((1, D)), pl.BlockSpec(memory_space=pl.ANY)],
        scratch_shapes=[pltpu.VMEM((3, tm, D), bf16), pltpu.SemaphoreType.DMA((3,))],
        compiler_params=_cp(("arbitrary",)))(dout, z, ya, yb, yc, *([proj] * (3 * nm)), wsq, g_post)


def tn_matmul(a, b, name):
    m, n = a.shape[1], b.shape[1]
    tmm = min(m, 512)

    def body(a_ref, b_ref, o_ref):
        o_ref[...] = lax.dot_general(a_ref[...], b_ref[...], (((0,), (0,)), ((), ())), preferred_element_type=f32).astype(bf16)

    return pl.pallas_call(
        body, name=name, grid=(m // tmm,), out_shape=jax.ShapeDtypeStruct((m, n), bf16),
        in_specs=[pl.BlockSpec((S, tmm), lambda i: (0, i)), _full((S, n))],
        out_specs=pl.BlockSpec((tmm, n), lambda i: (i, 0)),
        compiler_params=_cp(("parallel",)))(a, b)


def dwin_parts(h, dproj, name):
    tn = 1280

    def body(d_ref, h_ref, o_ref):
        o_ref[...] = lax.dot_general(d_ref[...], h_ref[...], (((0,), (0,)), ((), ())), preferred_element_type=f32).astype(bf16)

    return pl.pallas_call(
        body, name=name, grid=(P // tn,), out_shape=jax.ShapeDtypeStruct((P, D), bf16),
        in_specs=[pl.BlockSpec((S, tn), lambda j: (0, j)), _full((S, D))],
        out_specs=pl.BlockSpec((tn, D), lambda j: (j, 0)),
        compiler_params=_cp(("parallel",)))(dproj, h)


def _chan_windows(names, j):
    return [(slice(None), pl.ds(pl.multiple_of(_OFF[n][0] + j * CT, CT), CT)) for n in names]


def brancha_bwd(dyah, proj, convw, dproj, name, after=None):
    nsteps = D // CT
    extra, extra_specs = _after(after)

    def body(d_ref, ab, ac, ax, ag, w_ref, *rest):
        dw_ref, dproj_ref, padp, padt, accw, stash, sems = rest[-7:]
        j = pl.program_id(0)
        copies = _emit_copies(stash, dproj_ref, sems, _chan_windows(("a_b", "a_c", "a_x", "a_g"), j))
        _fill_pad(padp, 8, lambda rows: ac[rows, :] * ax[rows, :])
        _fill_pad(padt, 8, lambda rows: d_ref[rows, :] * ab[rows, :] * _silu(ag[rows, :]))
        accw[...] = jnp.zeros(accw.shape, f32)
        w = [w_ref[0, k:k + 1, :] for k in range(CA_W)]
        _emit_drain_previous(copies, j)

        def step(i, carry):
            base = pl.multiple_of(i * RC, RC)
            rows = pl.ds(base, RC)
            ps = [padp[pl.ds(base + 7 + k, RC), :] for k in range(CA_W)]
            t = sum(w[k] * ps[k] for k in range(CA_W))
            dp = sum(w[k] * padt[pl.ds(base + 9 - k, RC), :] for k in range(CA_W))
            d, a_b, a_g = d_ref[rows, :], ab[rows, :], ag[rows, :]
            stash[0, rows, :] = (d * t * _silu(a_g)).astype(bf16)
            stash[1, rows, :] = (dp * ax[rows, :]).astype(bf16)
            stash[2, rows, :] = (dp * ac[rows, :]).astype(bf16)
            stash[3, rows, :] = (d * a_b * t * _dsilu(a_g)).astype(bf16)
            dt = padt[pl.ds(base + 8, RC), :]
            for k in range(CA_W):
                accw[8 * k:8 * k + 8, :] += jnp.sum((dt * ps[k]).reshape(RC // 8, 8, CT), axis=0)
            return carry
        lax.fori_loop(0, S // RC, step, 0)
        _emit_start(copies, j, nsteps)
        dw_ref[0] = jnp.zeros((8, CT), f32)
        for k in range(CA_W):
            dw_ref[0, k:k + 1, :] = jnp.sum(accw[8 * k:8 * k + 8, :], axis=0, keepdims=True)

    tile = pl.BlockSpec((S, CT), lambda j: (0, j))
    anyspec = pl.BlockSpec(memory_space=pl.ANY)
    return pl.pallas_call(
        body, name=name, grid=(nsteps,),
        out_shape=[jax.ShapeDtypeStruct((NDEV, 8, CT), f32), jax.ShapeDtypeStruct((S, P), bf16)],
        in_specs=[tile, _chan_spec("a_b"), _chan_spec("a_c"), _chan_spec("a_x"), _chan_spec("a_g"),
                  pl.BlockSpec((1, 40, CT), lambda j: (j, 0, 0)), anyspec] + extra_specs,
        out_specs=[pl.BlockSpec((1, 8, CT), lambda j: (j, 0, 0)), anyspec],
        input_output_aliases={6: 1},
        scratch_shapes=[pltpu.VMEM((S + 16, CT), f32), pltpu.VMEM((S + 16, CT), f32), pltpu.VMEM((8 * CA_W, CT), f32),
                        pltpu.VMEM((4, S, CT), bf16), pltpu.SemaphoreType.DMA((4,))],
        compiler_params=_cp(("arbitrary",)))(dyah, proj, proj, proj, proj, convw, dproj, *extra)


def branchc2_bwd(dych, u1, proj, lng, lnb, dproj, name):
    tm = min(S, 512)
    ncg = len(_row_specs("c_g", tm))
    nsteps = S // tm

    def body(*refs):
        d_ref, u_ref = refs[:2]
        cg = _cat(refs[2:2 + ncg])
        g_ref, b_ref, _, du_ref, dlg_ref, dlb_ref, dcb_ref, dproj_ref, stash, sems = refs[2 + ncg:]
        i = pl.program_id(0)
        copies = _emit_copies(stash, dproj_ref, sems, [(pl.ds(pl.multiple_of(i * tm, tm), tm), pl.ds(_OFF["c_g"][0], D))])
        d = d_ref[...]
        xh, rstd = _ln_parts(u_ref[...])
        ln = xh * g_ref[...] + b_ref[...]
        _emit_drain_previous(copies, i)
        stash[0] = (d * _silu(ln) * _dsilu(cg)).astype(bf16)
        _emit_start(copies, i, nsteps)
        dln = d * _silu(cg) * _dsilu(ln)
        _acc(dlg_ref, jnp.sum(dln * xh, axis=0, keepdims=True))
        _acc(dlb_ref, jnp.sum(dln, axis=0, keepdims=True))
        dxh = dln * g_ref[...]
        du = rstd * (dxh - jnp.mean(dxh, axis=-1, keepdims=True) - xh * jnp.mean(dxh * xh, axis=-1, keepdims=True))
        du_ref[...] = du
        _acc(dcb_ref, jnp.sum(du, axis=0, keepdims=True))

    row = pl.BlockSpec((tm, D), lambda i: (i, 0))
    vec = jax.ShapeDtypeStruct((1, D), f32)
    anyspec = pl.BlockSpec(memory_space=pl.ANY)
    return pl.pallas_call(
        body, name=name, grid=(nsteps,),
        out_shape=[jax.ShapeDtypeStruct((S, D), f32), vec, vec, vec, jax.ShapeDtypeStruct((S, P), bf16)],
        in_specs=[row, row] + _row_specs("c_g", tm) + [_full((1, D)), _full((1, D)), anyspec],
        out_specs=[row, _full((1, D)), _full((1, D)), _full((1, D)), anyspec],
        input_output_aliases={4 + ncg: 4},
        scratch_shapes=[pltpu.VMEM((1, tm, D), bf16), pltpu.SemaphoreType.DMA((1,))],
        compiler_params=_cp(("arbitrary",)))(dych, u1, *([proj] * ncg), lng, lnb, dproj)


def branchc1_bwd(du1, proj, convw, dproj, name):
    nsteps = D // CT

    def body(d_ref, cu, cv, w_ref, _, dw_ref, dproj_ref, padu, padd, accw, stash, sems):
        j = pl.program_id(0)
        copies = _emit_copies(stash, dproj_ref, sems, _chan_windows(("c_u", "c_v"), j))
        _fill_pad(padu, 16, lambda rows: cu[rows, :] * _sig(cv[rows, :]))
        _fill_pad(padd, 16, lambda rows: d_ref[rows, :])
        accw[...] = jnp.zeros(accw.shape, f32)
        _emit_drain_previous(copies, j)

        rc = min(S, 128)

        def step(i, carry):
            base = pl.multiple_of(i * rc, rc)
            rows = pl.ds(base, rc)
            d = d_ref[rows, :]
            du0 = jnp.zeros((rc, CT), f32)
            for k in range(CC_W):
                du0 = du0 + w_ref[0, 8 + k:9 + k, :] * padd[pl.ds(base + 31 - k, rc), :]
                accw[8 * k:8 * k + 8, :] += jnp.sum((d * padu[pl.ds(base + k + 1, rc), :]).reshape(rc // 8, 8, CT), axis=0)
            sg = _sig(cv[rows, :])
            stash[0, rows, :] = (du0 * sg).astype(bf16)
            stash[1, rows, :] = (du0 * cu[rows, :] * sg * (1.0 - sg)).astype(bf16)
            return carry
        lax.fori_loop(0, S // rc, step, 0)
        _emit_start(copies, j, nsteps)
        dw_ref[0] = jnp.zeros((32, CT), f32)
        for k in range(CC_W):
            dw_ref[0, k:k + 1, :] = jnp.sum(accw[8 * k:8 * k + 8, :], axis=0, keepdims=True)

    tile = pl.BlockSpec((S, CT), lambda j: (0, j))
    anyspec = pl.BlockSpec(memory_space=pl.ANY)
    return pl.pallas_call(
        body, name=name, grid=(nsteps,),
        out_shape=[jax.ShapeDtypeStruct((NDEV, 32, CT), f32), jax.ShapeDtypeStruct((S, P), bf16)],
        in_specs=[tile, _chan_spec("c_u"), _chan_spec("c_v"), pl.BlockSpec((1, 40, CT), lambda j: (j, 0, 0)), anyspec],
        out_specs=[pl.BlockSpec((1, 32, CT), lambda j: (j, 0, 0)), anyspec],
        input_output_aliases={4: 1},
        scratch_shapes=[pltpu.VMEM((S + 32, CT), f32), pltpu.VMEM((S + 32, CT), f32), pltpu.VMEM((8 * 32, CT), f32),
                        pltpu.VMEM((2, S, CT), bf16), pltpu.SemaphoreType.DMA((2,))],
        compiler_params=_cp(("arbitrary",)))(du1, proj, proj, convw, dproj)


def attn_bwd(dybh, o, qh, kh, vh, proj, dproj, name):
    tq = min(S, 512)
    bg_blk = _OFF["b_g"][0] // GW

    def body(d_ref, o_ref, q_ref, k_ref, v_ref, bg_ref, _, dq_ref, dk_ref, dv_ref, dbg_ref):
        @pl.when(pl.program_id(1) == 0)
        def _():
            dk_ref[...] = jnp.zeros(dk_ref.shape, f32)
            dv_ref[...] = jnp.zeros(dv_ref.shape, f32)
        k, v = k_ref[...], v_ref[...]
        tn = (((0,), (0,)), ((), ()))
        dk_acc = jnp.zeros((S, HD), f32)
        dv_acc = jnp.zeros((S, HD), f32)
        for g in range(G):
            cols = slice(g * HD, (g + 1) * HD)
            d, bg, q, o = d_ref[:, cols], bg_ref[:, cols], q_ref[:, cols], o_ref[:, cols]
            dbg_ref[:, cols] = (d * o * _dsilu(bg)).astype(bf16)
            do = d * _silu(bg)
            p, rl = _softmax_rows(q, k)
            dv_acc = dv_acc + lax.dot_general(p.astype(bf16), (do * rl).astype(bf16), tn, preferred_element_type=f32)
            dp = lax.dot_general(do.astype(bf16), v, (((1,), (1,)), ((), ())), preferred_element_type=f32)
            delta = jnp.sum(do * o, axis=-1, keepdims=True)
            ds = (p * (dp - delta)).astype(bf16)
            rs_ = rl * (HD ** -0.5)
            dq_ref[:, cols] = jnp.dot(ds, k, preferred_element_type=f32) * rs_
            dk_acc = dk_acc + lax.dot_general(ds, (q.astype(f32) * rs_).astype(bf16), tn, preferred_element_type=f32)
        dk_ref[...] += dk_acc
        dv_ref[...] += dv_acc

    grp = pl.BlockSpec((tq, GW), lambda kv, i: (i, kv))
    kvs = pl.BlockSpec((S, HD), lambda kv, i: (0, kv))
    return pl.pallas_call(
        body, name=name, grid=(NKV, S // tq),
        out_shape=[jax.ShapeDtypeStruct((S, D), f32), jax.ShapeDtypeStruct((S, WKV), f32),
                   jax.ShapeDtypeStruct((S, WKV), f32), jax.ShapeDtypeStruct((S, P), bf16)],
        in_specs=[grp, grp, grp, kvs, kvs, pl.BlockSpec((tq, GW), lambda kv, i: (i, bg_blk + kv)),
                  pl.BlockSpec(memory_space=pl.ANY)],
        out_specs=[grp, kvs, kvs, pl.BlockSpec((tq, GW), lambda kv, i: (i, bg_blk + kv))],
        input_output_aliases={6: 3},
        compiler_params=_cp(("parallel", "arbitrary")))(dybh, o, qh, kh, vh, proj, dproj)


def qkv_bwd(dqh, dkh, dvh, proj, qn, kn, cos, sin, dproj, name):
    tm = min(S, 512)
    nq, nk = len(_row_specs("q", tm)), len(_row_specs("k", tm))
    nsteps = S // tm
    wq = D + 2 * WKV

    def body(*refs):
        dqh_ref, dkh_ref, dvh_ref = refs[:3]
        q = _cat(refs[3:3 + nq])
        k = _cat(refs[3 + nq:3 + nq + nk])
        qn_ref, kn_ref, cos_ref, sin_ref, _, dqn_ref, dkn_ref, dproj_ref, stash, sems = refs[3 + nq + nk:]
        i = pl.program_id(0)
        copies = _emit_copies(stash, dproj_ref, sems, [(pl.ds(pl.multiple_of(i * tm, tm), tm), pl.ds(_OFF["q"][0], wq))])
        cos, sin = cos_ref[...], sin_ref[...]
        _emit_drain_previous(copies, i)

        def heads(xx, dd, gn, col0, dgn_ref, n):
            dg = jnp.zeros((1, HD), f32)
            for h in range(n):
                xh = xx[:, h * HD:(h + 1) * HD]
                dh = dd[:, h * HD:(h + 1) * HD]
                r = lax.rsqrt(jnp.mean(xh * xh, axis=-1, keepdims=True) + RMS_EPS)
                xn = xh * r
                dy = dh * cos + _swap32(dh * sin)
                dg = dg + jnp.sum(dy * xn, axis=0, keepdims=True)
                dxn = dy * gn
                stash[0, :, col0 + h * HD:col0 + (h + 1) * HD] = (
                    r * (dxn - xn * jnp.mean(dxn * xn, axis=-1, keepdims=True))).astype(bf16)
            _acc(dgn_ref, dg)
        heads(q, dqh_ref[...], qn_ref[...], 0, dqn_ref, NQ)
        heads(k, dkh_ref[...], kn_ref[...], D, dkn_ref, NKV)
        stash[0, :, D + WKV:wq] = dvh_ref[...].astype(bf16)
        _emit_start(copies, i, nsteps)

    row = lambda w: pl.BlockSpec((tm, w), lambda i: (i, 0))
    vec = jax.ShapeDtypeStruct((1, HD), f32)
    anyspec = pl.BlockSpec(memory_space=pl.ANY)
    return pl.pallas_call(
        body, name=name, grid=(nsteps,),
        out_shape=[vec, vec, jax.ShapeDtypeStruct((S, P), bf16)],
        in_specs=[row(D), row(WKV), row(WKV)] + _row_specs("q", tm) + _row_specs("k", tm)
        + [_full((1, HD)), _full((1, HD)), row(HD), row(HD), anyspec],
        out_specs=[_full((1, HD)), _full((1, HD)), anyspec],
        input_output_aliases={7 + nq + nk: 2},
        scratch_shapes=[pltpu.VMEM((1, tm, wq), bf16), pltpu.SemaphoreType.DMA((1,))],
        compiler_params=_cp(("arbitrary",)))(dqh, dkh, dvh, *([proj] * (nq + nk)), qn, kn, cos, sin, dproj)


def dh_bwd(dproj, wfull, xin, dout, g_pre, name, after=None):
    tm, tk = min(S, 1024), 2560
    nk = P // tk
    extra, extra_specs = _after(after)

    def body(d_ref, w_ref, x_ref, do_ref, g_ref, *rest):
        dx_ref, dg_ref, acc = rest[-3:]
        kk = pl.program_id(1)

        @pl.when(kk == 0)
        def _():
            acc[...] = jnp.zeros(acc.shape, f32)
        acc[...] += jnp.dot(d_ref[...], w_ref[...], preferred_element_type=f32)

        @pl.when((kk == 0) & (pl.program_id(0) == 0))
        def _():
            dg_ref[...] = jnp.zeros(dg_ref.shape, f32)

        @pl.when(kk == nk - 1)
        def _():
            x, dh = x_ref[...], acc[...]
            r = lax.rsqrt(jnp.mean(x * x, axis=-1, keepdims=True) + RMS_EPS)
            xn = x * r
            dg_ref[...] += jnp.sum(dh * xn, axis=0, keepdims=True)
            dxn = dh * g_ref[...]
            dx_ref[...] = do_ref[...] + r * (dxn - xn * jnp.mean(dxn * xn, axis=-1, keepdims=True))

    row = pl.BlockSpec((tm, D), lambda i, k: (i, 0))
    return pl.pallas_call(
        body, name=name, grid=(S // tm, nk),
        out_shape=[jax.ShapeDtypeStruct((S, D), f32), jax.ShapeDtypeStruct((1, D), f32)],
        in_specs=[pl.BlockSpec((tm, tk), lambda i, k: (i, k)), pl.BlockSpec((tk, D), lambda i, k: (k, 0)), row, row, _full((1, D))]
        + extra_specs,
        out_specs=[row, _full((1, D))],
        scratch_shapes=[pltpu.VMEM((tm, D), f32)],
        compiler_params=_cp(("arbitrary", "arbitrary")))(dproj, wfull, xin, dout, g_pre, *extra)


def adam_update(parts, own, me, w, m, v, l, acc, name):
    lw, r, c = w.shape
    tr = _row_tile(r)
    nslots = parts.shape[0]

    def body(me_ref, p_ref, own_ref, w_ref, m_ref, v_ref, *rest):
        g_ref, d_ref, nm_ref, nv_ref = rest[-4:]
        g = None
        for s in range(nslots):
            part = jnp.where(me_ref[0] == s, own_ref[0], p_ref[s]).astype(f32)
            g = part if g is None else g + part
        nm = ADAM_B1 * m_ref[0] + (1.0 - ADAM_B1) * g
        nv = ADAM_B2 * v_ref[0] + (1.0 - ADAM_B2) * (g * g)
        m_hat = nm / (1.0 - ADAM_B1 ** ADAM_STEP)
        v_hat = nv / (1.0 - ADAM_B2 ** ADAM_STEP)
        g_ref[0] = g
        d_ref[0] = -ADAM_LR * (m_hat / (jnp.sqrt(v_hat) + ADAM_EPS) + ADAM_WD * w_ref[0])
        nm_ref[0] = nm
        nv_ref[0] = nv

    blk = pl.BlockSpec((1, tr, c), lambda i, me_ref: (l, i, 0))
    sd = jax.ShapeDtypeStruct((lw, r, c), f32)
    extra = [] if acc is None else list(acc)
    return pl.pallas_call(
        body, name=name, out_shape=[sd] * 4,
        grid_spec=pltpu.PrefetchScalarGridSpec(
            num_scalar_prefetch=1, grid=(r // tr,),
            in_specs=[pl.BlockSpec((nslots, tr, c), lambda i, me_ref: (0, i, 0)),
                      pl.BlockSpec((1, tr, c), lambda i, me_ref: (me_ref[0], i, 0)), blk, blk, blk]
            + [pl.BlockSpec(memory_space=pl.ANY)] * len(extra),
            out_specs=[blk] * 4),
        input_output_aliases={6 + t: t for t in range(len(extra))},
        compiler_params=_cp(("parallel",)))(me, parts, own, w, m, v, *extra)


def _rope_tables():
    t = jnp.arange(S)
    rows, cols = (t // GRID_W).astype(f32), (t % GRID_W).astype(f32)
    nf = HD // 4
    inv = ROPE_THETA ** (-jnp.arange(nf, dtype=f32) / nf)
    ar, ac = rows[:, None] * inv, cols[:, None] * inv
    cos = jnp.concatenate([jnp.cos(ar), jnp.cos(ar), jnp.cos(ac), jnp.cos(ac)], axis=1)
    sin = jnp.concatenate([-jnp.sin(ar), jnp.sin(ar), -jnp.sin(ac), jnp.sin(ac)], axis=1)
    return cos, sin


def _pack_conv(ca, cc):
    z = lambda n: jnp.zeros((L, n, CT), f32)
    return jnp.concatenate([ca, z(5), cc, z(1)], axis=1)


def _pack_small(npre, npost, ccb, lng, lnb, qn, kn):
    wide = lambda a: jnp.pad(a, ((0, 0), (0, D - HD)))
    return jnp.stack([npre, npost, ccb, lng, lnb, wide(qn), wide(kn), jnp.zeros((L, D), f32)], axis=1).reshape(L * 8, D)


def kernel(x, norm_pre, norm_post, w_in, conv_a_w, q_norm, k_norm, conv_c_w, conv_c_b, ln_c_g, ln_c_b, w_out_a, w_out_b, w_out_c, w_o, loss_target, m_norm_pre, m_norm_post, m_w_in, m_conv_a_w, m_q_norm, m_k_norm, m_conv_c_w, m_conv_c_b, m_ln_c_g, m_ln_c_b, m_w_out_a, m_w_out_b, m_w_out_c, m_w_o, v_norm_pre, v_norm_post, v_w_in, v_conv_a_w, v_q_norm, v_k_norm, v_conv_c_w, v_conv_c_b, v_ln_c_g, v_ln_c_b, v_w_out_a, v_w_out_b, v_w_out_c, v_w_o):
    cos, sin = _rope_tables()
    rs = D // NDEV
    stack_sq = lambda a, b, c, d: jnp.stack([a, b, c, d], axis=1)
    wsq32 = stack_sq(w_out_a, w_out_b, w_out_c, w_o)
    conv_pack = _pack_conv(conv_a_w, conv_c_w)
    vec = lambda a, l: a[l][None, :]
    me = (4 * lax.axis_index("x") + 2 * lax.axis_index("y") + lax.axis_index("c")).astype(jnp.int32).reshape(1)

    def gather_start(l, after):
        return split_start(staged[l], _gather_copies, 12, f"ag_start{l}", after=after)

    def forward_start(l, after):
        s_sems, r_sems, bufs, _ = gathers[l]
        bufs = split_wait(s_sems, r_sems, bufs, _gather_copies, after, f"ag_wait{l}")
        fw = split_start(bufs, _forward_copies, 9, f"ag_fwd_start{l}")
        if l + 1 < L:
            gathers[l + 1] = gather_start(l + 1, fw[3])
            return fw, gathers[l + 1][3]
        return fw, fw[3]

    def forward_wait(fw, after, l):
        s_sems, r_sems, bufs, _ = fw
        return split_wait(s_sems, r_sems, bufs, _forward_copies, after, f"ag_fwd_wait{l}")

    wt, m_wt, v_wt = (jnp.swapaxes(a, 1, 2) for a in (w_in, m_w_in, v_w_in))
    xs, saved = x.reshape(S, D), []
    stage = lambda l, after: stage_shards(wt, wsq32.reshape(L, 4 * rs, D), conv_pack, l, me, f"stage{l}", after)
    staged = [stage(0, None)]
    gathers = [gather_start(0, None)] + [None] * (L - 1)
    staged += [stage(l, gathers[0][3]) for l in range(1, L)]
    fw, issued = forward_start(0, staged[L - 1][0] if L > 1 else xs)
    wg, wsq, convw = forward_wait(fw, issued, 0)
    for l in range(L):
        wsq = wsq.reshape(NDEV, 4, rs, D)
        wfull = wg.reshape(P, D)
        proj, h = proj_fwd(xs, vec(norm_pre, l), wfull, f"proj{l}")
        yah = brancha_fwd(proj, convw, f"bra{l}")
        u1 = branchc1_fwd(proj, convw, vec(conv_c_b, l), f"brc1_{l}")
        qh, kh, vh = qkv_fwd(proj, vec(q_norm, l), vec(k_norm, l), cos, sin, f"qkv{l}")
        o, ybh = attn_fwd(qh, kh, vh, proj, f"attn{l}")
        issued = None
        if l + 1 < L:
            fw, issued = forward_start(l + 1, o)
        ych = branchc2_fwd(u1, proj, vec(ln_c_g, l), vec(ln_c_b, l), f"brc2_{l}", after=issued)
        ya, yb, yc, y16, z, xo = merge_fwd(xs, yah, ybh, ych, proj, wsq, vec(norm_post, l), f"merge{l}")
        saved.append(dict(x=xs, wfull=wfull, wsq=wsq, convw=convw, proj=proj, h=h, yah=yah, ybh=ybh, ych=ych, u1=u1,
                          qh=qh, kh=kh, vh=vh, o=o, ya=ya, yb=yb, yc=yc, y16=y16, z=z))
        xs = xo
        if l + 1 < L:
            wg, wsq, convw = forward_wait(fw, xs, l + 1)
    dx, loss_part = loss_fwd(xs, loss_target.reshape(S, D), "loss")
    loss = lax.psum(loss_part[0, 0], ("x", "y", "c"))

    acc = dict(win=None, sq=None, conv=None)
    small_parts = [None] * L
    msq32 = stack_sq(m_w_out_a, m_w_out_b, m_w_out_c, m_w_o)
    vsq32 = stack_sq(v_w_out_a, v_w_out_b, v_w_out_c, v_w_o)
    mconv, vconv = _pack_conv(m_conv_a_w, m_conv_c_w), _pack_conv(v_conv_a_w, v_conv_c_w)

    def scatter_start(parts, name, after=None):
        bufs = parts + [lax.empty(p.shape, p.dtype) for p in parts]
        return split_start(bufs, _scatter_copies(len(parts)), 7 * len(parts), name, after=after)

    def finish(l, started, after):
        (s1, r1, b1, _), (s2, r2, b2, _) = started
        gsq_own, rsq = split_wait(s1, r1, b1, _scatter_copies(1), after, f"rs_sq_wait{l}")
        gwin_own, gconv_own, rwin, rconv = split_wait(s2, r2, b2, _scatter_copies(2), after, f"rs_win_wait{l}")
        flat = lambda a: a.reshape(a.shape[0], 4 * rs, D)
        acc["win"] = adam_update(rwin, gwin_own, me, wt, m_wt, v_wt, l, acc["win"], f"adam_win{l}")
        acc["sq"] = adam_update(flat(rsq), flat(gsq_own), me, flat(wsq32), flat(msq32), flat(vsq32), l, acc["sq"], f"adam_wsq{l}")
        acc["conv"] = adam_update(rconv, gconv_own, me, conv_pack, mconv, vconv, l, acc["conv"], f"adam_conv{l}")

    pending = [None] * L
    for l in reversed(range(L)):
        sv = saved[l]
        proj = sv["proj"]
        (dyah, dybh, dych, dzb, dyab, dybb, dycb, dgpost, dproj) = merge_bwd(
            dx, sv["z"], sv["ya"], sv["yb"], sv["yc"], proj, sv["wsq"], vec(norm_post, l), f"merge_bwd{l}")
        gsq = [tn_matmul(a, b, f"dwsq{t}_{l}") for t, (a, b) in enumerate(
            ((sv["yah"], dyab), (sv["ybh"], dybb), (sv["ych"], dycb), (sv["y16"], dzb)))]
        gsq_parts = jnp.stack([g.reshape(NDEV, rs, D) for g in gsq], axis=1)
        st1 = scatter_start([gsq_parts], f"rs_sq_start{l}", after=loss.reshape(1, 1) if l == L - 1 else None)
        convw = sv["convw"]
        gca, dproj = brancha_bwd(dyah, proj, convw, dproj, f"bra_bwd{l}", after=st1[3])
        du1, dlg, dlb, dcb, dproj = branchc2_bwd(dych, sv["u1"], proj, vec(ln_c_g, l), vec(ln_c_b, l), dproj, f"brc2_bwd{l}")
        gcc, dproj = branchc1_bwd(du1, proj, convw, dproj, f"brc1_bwd{l}")
        dqh, dkh, dvh, dproj = attn_bwd(dybh, sv["o"], sv["qh"], sv["kh"], sv["vh"], proj, dproj, f"attn_bwd{l}")
        dqn, dkn, dproj = qkv_bwd(dqh, dkh, dvh, proj, vec(q_norm, l), vec(k_norm, l), cos, sin, dproj, f"qkv_bwd{l}")
        gwin = dwin_parts(sv["h"], dproj, f"dwin{l}").reshape(NDEV, PSH, D)
        gconv = jnp.concatenate([gca, gcc], axis=1)
        if l > 0:
            st2 = scatter_start([gwin, gconv], f"rs_win_start{l}")
            issued = st2[3]
        else:
            st2 = scatter_start([gconv], "rs_conv_start0")
            pair = split_start([gwin, lax.empty((NDEV // 2, PSH, D), bf16)], _pair_copies, NDEV // 2, "rs_pair_start0",
                               after=st2[3])
            issued = pair[3]
        dx, dgpre = dh_bwd(dproj, sv["wfull"], sv["x"], dx, vec(norm_pre, l), f"dh{l}", after=issued)
        wide = lambda a: jnp.pad(a, ((0, 0), (0, D - HD)))
        small_parts[l] = jnp.concatenate([dgpre, dgpost, dcb, dlg, dlb, wide(dqn), wide(dkn), jnp.zeros((1, D), f32)], axis=0)
        pending[l] = (st1, st2)

    gwin0, pair_land = split_wait(pair[0], pair[1], pair[2], _pair_copies, dx, "rs_pair_wait0")
    summed = pair_sum(gwin0, pair_land, me, "rs_pair_sum0")
    (small_all,) = all_gather([jnp.concatenate(small_parts, axis=0)], "ag_small")
    chip = split_start([summed, lax.empty(summed.shape, bf16)], _chip_copies, NDEV // 2 - 1, "rs_chip_start0", after=small_all)
    for l in reversed(range(1, L)):
        finish(l, pending[l], after=chip[3])
    sm = adam_update(small_all, small_all, me,
                     _pack_small(norm_pre, norm_post, conv_c_b, ln_c_g, ln_c_b, q_norm, k_norm)[None],
                     _pack_small(m_norm_pre, m_norm_post, m_conv_c_b, m_ln_c_g, m_ln_c_b, m_q_norm, m_k_norm)[None],
                     _pack_small(v_norm_pre, v_norm_post, v_conv_c_b, v_ln_c_g, v_ln_c_b, v_q_norm, v_k_norm)[None],
                     0, None, "adam_small")
    (s1, r1, b1, _), (s2, r2, b2, _) = pending[0]
    gsq_own, rsq = split_wait(s1, r1, b1, _scatter_copies(1), sm[0], "rs_sq_wait0")
    gconv_own, rconv = split_wait(s2, r2, b2, _scatter_copies(1), sm[0], "rs_conv_wait0")
    flat = lambda a: a.reshape(a.shape[0], 4 * rs, D)
    acc["sq"] = adam_update(flat(rsq), flat(gsq_own), me, flat(wsq32), flat(msq32), flat(vsq32), 0, acc["sq"], "adam_wsq0")
    acc["conv"] = adam_update(rconv, gconv_own, me, conv_pack, mconv, vconv, 0, acc["conv"], "adam_conv0")
    summed, chip_land = split_wait(chip[0], chip[1], chip[2], _chip_copies, acc["sq"][0], "rs_chip_wait0")
    acc["win"] = adam_update(chip_land, summed, me // 2, wt, m_wt, v_wt, 0, acc["win"], "adam_win0")
    sm = [a.reshape(L, 8, D) for a in sm]
    small_rows = dict(norm_pre=(0, D), norm_post=(1, D), conv_c_b=(2, D), ln_c_g=(3, D), ln_c_b=(4, D), q_norm=(5, HD), k_norm=(6, HD))
    sq_rows = dict(w_out_a=0, w_out_b=1, w_out_c=2, w_o=3)

    order = ["norm_pre", "norm_post", "w_in", "conv_a_w", "q_norm", "k_norm", "conv_c_w", "conv_c_b", "ln_c_g", "ln_c_b",
             "w_out_a", "w_out_b", "w_out_c", "w_o"]
    result = [loss, dx.reshape(1, S, D)]
    for kind in range(4):
        for nme in order:
            if nme in small_rows:
                rw, wd = small_rows[nme]
                result.append(sm[kind][:, rw, :wd])
            elif nme in sq_rows:
                result.append(acc["sq"][kind][:, sq_rows[nme] * rs:(sq_rows[nme] + 1) * rs])
            elif nme == "w_in":
                result.append(jnp.swapaxes(acc["win"][kind], 1, 2))
            elif nme == "conv_a_w":
                result.append(acc["conv"][kind][:, 0:CA_W])
            else:
                result.append(acc["conv"][kind][:, 8:8 + CC_W])
    return tuple(result)
```

```python
import math

import jax
import jax.numpy as jnp
from jax import lax
from jax.experimental import pallas as pl
from jax.experimental.pallas import tpu as pltpu

f32, bf16 = jnp.float32, jnp.bfloat16

D = 1024
S = 2048
L = 4
HD = 128
NQ = D // HD
NKV = NQ // 4
G = NQ // NKV
WKV = NKV * HD
GRID_W = 64
ROPE_THETA = 10000.0
RMS_EPS = 1e-6
LN_EPS = 1e-5
NDEV = 8
CA_W, CC_W = 3, 31
P = 12 * D + 2 * WKV
PSH = P // NDEV
CT = 128
ADAM_LR, ADAM_B1, ADAM_B2, ADAM_EPS, ADAM_WD, ADAM_STEP = 0.001, 0.9, 0.999, 1e-08, 0.01, 10
VMEM_LIMIT = 56 * 1024 * 1024
MESH = pl.DeviceIdType.MESH

_OFF = {}
_o = 0
for _n, _w in (("a_b", D), ("a_c", D), ("a_x", D), ("a_g", D), ("q", D), ("k", WKV), ("v", WKV), ("b_g", D),
               ("c_u", D), ("c_v", D), ("c_g", D), ("m_a", D), ("m_b", D), ("m_c", D)):
    _OFF[_n] = (_o, _w)
    _o += _w
PIECES = tuple(_OFF)


def _cp(sem=None, **kw):
    return pltpu.CompilerParams(dimension_semantics=sem, vmem_limit_bytes=VMEM_LIMIT, **kw)


def _sig(x):
    return 1.0 / (1.0 + jnp.exp(-x))


def _silu(x):
    return x * _sig(x)


def _dsilu(x):
    s = _sig(x)
    return s * (1.0 + x * (1.0 - s))


def _row_specs(name, tm):
    off, w = _OFF[name]
    bw = math.gcd(off, w) if off else w
    return [pl.BlockSpec((tm, bw), (lambda i, *_, b=off // bw + t: (i, b))) for t in range(w // bw)]


def _cat(refs):
    return refs[0][...] if len(refs) == 1 else jnp.concatenate([r[...] for r in refs], axis=1)


def _chan_spec(name):
    off, _ = _OFF[name]
    return pl.BlockSpec((S, CT), lambda j, b=off // CT: (0, b + j))


def _full(shape):
    return pl.BlockSpec(shape, lambda *_: (0,) * len(shape))


def _coords():
    return lax.axis_index("x"), lax.axis_index("y"), lax.axis_index("c")


def all_gather(shards, name):
    n = len(shards)

    def body(*refs):
        ins, outs = refs[:n], refs[n:2 * n]
        send_sems, recv_sems, local_sems = refs[2 * n:]
        x, y, c = _coords()
        me, sibling = (x, y, c), (x, y, 1 - c)
        chips = [(1 - x, y), (x, 1 - y), (1 - x, 1 - y)]

        def slot(a, p):
            return outs[a].at[4 * p[0] + 2 * p[1] + p[2]]

        def copy(a, k, block, to, src=None):
            return pltpu.make_async_remote_copy(
                src_ref=slot(a, block) if src is None else src, dst_ref=slot(a, block),
                send_sem=send_sems.at[7 * a + k], recv_sem=recv_sems.at[7 * a + k], device_id=to, device_id_type=MESH)

        mine = [pltpu.make_async_copy(ins[a], slot(a, me), local_sems.at[a]) for a in range(n)]
        for cp in mine:
            cp.start()
        first = []
        for a in range(n):
            first.append(copy(a, 0, me, sibling, src=ins[a]))
            first += [copy(a, 1 + j, me, (*chip, c), src=ins[a]) for j, chip in enumerate(chips)]
        for cp in first:
            cp.start()
        passed = []
        for j, chip in enumerate(chips):
            for a in range(n):
                copy(a, 1 + j, (*chip, c), me).wait_recv()
                fw = copy(a, 4 + j, (*chip, c), sibling)
                fw.start()
                passed.append(fw)
        for a in range(n):
            copy(a, 0, sibling, me).wait_recv()
            for j, chip in enumerate(chips):
                copy(a, 4 + j, (*chip, 1 - c), me).wait_recv()
        for cp in first + passed:
            cp.wait_send()
        for cp in mine:
            cp.wait()

    anyspec = pl.BlockSpec(memory_space=pl.ANY)
    return pl.pallas_call(
        body, name=name,
        out_shape=[jax.ShapeDtypeStruct((NDEV,) + s.shape, s.dtype) for s in shards],
        in_specs=[anyspec] * n, out_specs=[anyspec] * n,
        scratch_shapes=[pltpu.SemaphoreType.DMA((7 * n,)), pltpu.SemaphoreType.DMA((7 * n,)), pltpu.SemaphoreType.DMA((n,))],
    )(*shards)


_HBM = pl.BlockSpec(memory_space=pltpu.HBM)
_SEM = pl.BlockSpec(memory_space=pltpu.SEMAPHORE)
_EFFECT = pltpu.SideEffectType.DATAFLOW_SIDE_EFFECTING


def split_start(bufs, make_copies, nsem, name, after=None):
    n = len(bufs)
    extra = [] if after is None else [after]

    def body(*refs):
        send_sems, recv_sems = refs[n + len(extra):n + len(extra) + 2]
        for cp in make_copies(refs[:n], send_sems, recv_sems):
            cp.start()
        refs[-1][...] = jnp.zeros((8, 128), f32)

    res = pl.pallas_call(
        body, name=name,
        out_shape=(pltpu.SemaphoreType.DMA((nsem,)), pltpu.SemaphoreType.DMA((nsem,)),
                   *[pltpu.HBM(b.shape, b.dtype) for b in bufs], jax.ShapeDtypeStruct((8, 128), f32)),
        in_specs=[_HBM] * n + [pl.BlockSpec(memory_space=pl.ANY)] * len(extra),
        out_specs=(_SEM, _SEM, *([_HBM] * n), pl.BlockSpec(memory_space=pltpu.VMEM)),
        input_output_aliases={i: 2 + i for i in range(n)},
        compiler_params=pltpu.CompilerParams(has_side_effects=_EFFECT),
    )(*[pltpu.with_memory_space_constraint(b, pltpu.HBM) for b in bufs], *extra)
    return res[0], res[1], list(res[2:2 + n]), res[-1]


def split_wait(send_sems, recv_sems, bufs, make_copies, after, name):
    n = len(bufs)

    def body(*refs):
        for cp in make_copies(refs[:n], refs[n], refs[n + 1]):
            cp.wait_send()
            cp.wait_recv()

    res = pl.pallas_call(
        body, name=name,
        out_shape=tuple(pltpu.HBM(b.shape, b.dtype) for b in bufs),
        in_specs=[_HBM] * n + [_SEM, _SEM, pl.BlockSpec(memory_space=pl.ANY)],
        out_specs=[_HBM] * n,
        input_output_aliases={i: i for i in range(n)},
        compiler_params=pltpu.CompilerParams(has_side_effects=_EFFECT),
    )(*bufs, send_sems, recv_sems, after)
    return list(res)


def _scatter_copies(n):
    def make(refs, send_sems, recv_sems):
        x, y, c = _coords()
        me = 4 * x + 2 * y + c
        copies = []
        for a in range(n):
            for k in range(1, NDEV):
                px = 1 - x if (k >> 2) & 1 else x
                py = 1 - y if (k >> 1) & 1 else y
                pc = 1 - c if k & 1 else c
                copies.append(pltpu.make_async_remote_copy(
                    src_ref=refs[a].at[4 * px + 2 * py + pc], dst_ref=refs[n + a].at[me],
                    send_sem=send_sems.at[7 * a + k - 1], recv_sem=recv_sems.at[7 * a + k - 1],
                    device_id=(px, py, pc), device_id_type=MESH))
        return copies
    return make


def _gather_copies(refs, send_sems, recv_sems):
    x, y, c = _coords()
    me = 4 * x + 2 * y + c
    targets = [(x, y, 1 - c), (1 - x, y, c), (x, 1 - y, c), (1 - x, 1 - y, c)]
    return [pltpu.make_async_remote_copy(
        src_ref=r.at[me], dst_ref=r.at[me], send_sem=send_sems.at[4 * a + k], recv_sem=recv_sems.at[4 * a + k],
        device_id=to, device_id_type=MESH) for a, r in enumerate(refs) for k, to in enumerate(targets)]


def _forward_copies(refs, send_sems, recv_sems):
    x, y, c = _coords()
    chips = [(1 - x, y), (x, 1 - y), (1 - x, 1 - y)]
    return [pltpu.make_async_remote_copy(
        src_ref=r.at[4 * px + 2 * py + c], dst_ref=r.at[4 * px + 2 * py + c], send_sem=send_sems.at[3 * a + j],
        recv_sem=recv_sems.at[3 * a + j], device_id=(x, y, 1 - c), device_id_type=MESH)
        for a, r in enumerate(refs) for j, (px, py) in enumerate(chips)]


def _pair_copies(refs, send_sems, recv_sems):
    x, y, c = _coords()
    parts, land = refs
    return [pltpu.make_async_remote_copy(
        src_ref=parts.at[2 * j + 1 - c], dst_ref=land.at[j], send_sem=send_sems.at[j], recv_sem=recv_sems.at[j],
        device_id=(x, y, 1 - c), device_id_type=MESH) for j in range(NDEV // 2)]


def _chip_copies(refs, send_sems, recv_sems):
    x, y, c = _coords()
    summed, land = refs
    copies = []
    for k in range(1, NDEV // 2):
        px = 1 - x if (k >> 1) & 1 else x
        py = 1 - y if k & 1 else y
        copies.append(pltpu.make_async_remote_copy(
            src_ref=summed.at[2 * px + py], dst_ref=land.at[2 * x + y], send_sem=send_sems.at[k - 1],
            recv_sem=recv_sems.at[k - 1], device_id=(px, py, c), device_id_type=MESH))
    return copies


def pair_sum(parts, land, me, name):
    _, r, c = parts.shape
    tr = _row_tile(r, 800)

    def body(me_ref, a_ref, b_ref, o_ref):
        o_ref[...] = (a_ref[...].astype(f32) + b_ref[...].astype(f32)).astype(bf16)

    blk = pl.BlockSpec((1, tr, c), lambda j, i, m: (j, i, 0))
    return pl.pallas_call(
        body, name=name, out_shape=jax.ShapeDtypeStruct((NDEV // 2, r, c), bf16),
        grid_spec=pltpu.PrefetchScalarGridSpec(
            num_scalar_prefetch=1, grid=(NDEV // 2, r // tr),
            in_specs=[pl.BlockSpec((1, tr, c), lambda j, i, m: (2 * j + m[0] % 2, i, 0)), blk], out_specs=blk),
        compiler_params=_cp(("parallel", "parallel")))(me, parts, land)


def _row_tile(r, cap=256):
    return r if r <= cap else max(t for t in (800, 512, 256, 160, 128) if t <= cap and r % t == 0)


def stage_shards(wt, wsq, conv, l, me, name, after=None):
    outs = []
    extra = [] if after is None else [after]
    for a, dt in ((wt, bf16), (wsq, bf16), (conv, f32)):
        _, r, c = a.shape
        tr = _row_tile(r, 800)

        def body(me_ref, a_ref, *rest):
            rest[-1][...] = a_ref[...].astype(rest[-1].dtype)

        outs.append(pl.pallas_call(
            body, name=f"{name}_{len(outs)}", out_shape=jax.ShapeDtypeStruct((NDEV, r, c), dt),
            grid_spec=pltpu.PrefetchScalarGridSpec(
                num_scalar_prefetch=1, grid=(r // tr,),
                in_specs=[pl.BlockSpec((1, tr, c), lambda i, m: (l, i, 0))] + [pl.BlockSpec(memory_space=pl.ANY)] * len(extra),
                out_specs=pl.BlockSpec((1, tr, c), lambda i, m: (m[0], i, 0))),
            compiler_params=_cp(("arbitrary",)))(me, a, *extra))
    return outs


def proj_fwd(xin, g_pre, wt, name):
    tm, tn = min(S, 1024), 2560

    def body(x_ref, g_ref, w_ref, proj_ref, h_ref, hs):
        @pl.when(pl.program_id(1) == 0)
        def _():
            x = x_ref[...]
            r = lax.rsqrt(jnp.mean(x * x, axis=-1, keepdims=True) + RMS_EPS)
            h = (x * r * g_ref[...]).astype(bf16)
            hs[...] = h
            h_ref[...] = h
        proj_ref[...] = lax.dot_general(hs[...], w_ref[...], (((1,), (1,)), ((), ())), preferred_element_type=f32)

    return pl.pallas_call(
        body, name=name, grid=(S // tm, P // tn),
        out_shape=[jax.ShapeDtypeStruct((S, P), f32), jax.ShapeDtypeStruct((S, D), bf16)],
        in_specs=[pl.BlockSpec((tm, D), lambda i, j: (i, 0)), _full((1, D)), pl.BlockSpec((tn, D), lambda i, j: (j, 0))],
        out_specs=[pl.BlockSpec((tm, tn), lambda i, j: (i, j)), pl.BlockSpec((tm, D), lambda i, j: (i, 0))],
        scratch_shapes=[pltpu.VMEM((tm, D), bf16)],
        compiler_params=_cp(("parallel", "arbitrary")))(xin, g_pre, wt)


RC = 256


def _fill_pad(pad, halo, val_fn):
    pad[0:halo, :] = jnp.zeros((halo, CT), f32)
    pad[S + halo:S + 2 * halo, :] = jnp.zeros((halo, CT), f32)

    def step(i, carry):
        rows = pl.ds(pl.multiple_of(i * RC, RC), RC)
        pad[pl.ds(pl.multiple_of(i * RC, RC) + halo, RC), :] = val_fn(rows)
        return carry
    lax.fori_loop(0, S // RC, step, 0)


def brancha_fwd(proj, convw, name):
    def body(ab, ac, ax, ag, w_ref, o_ref, pad):
        _fill_pad(pad, 8, lambda rows: ac[rows, :] * ax[rows, :])
        w = [w_ref[0, k:k + 1, :] for k in range(CA_W)]

        def step(i, carry):
            base = pl.multiple_of(i * RC, RC)
            rows = pl.ds(base, RC)
            t = sum(w[k] * pad[pl.ds(base + 7 + k, RC), :] for k in range(CA_W))
            o_ref[rows, :] = (ab[rows, :] * t * _silu(ag[rows, :])).astype(bf16)
            return carry
        lax.fori_loop(0, S // RC, step, 0)

    return pl.pallas_call(
        body, name=name, grid=(D // CT,), out_shape=jax.ShapeDtypeStruct((S, D), bf16),
        in_specs=[_chan_spec("a_b"), _chan_spec("a_c"), _chan_spec("a_x"), _chan_spec("a_g"),
                  pl.BlockSpec((1, 40, CT), lambda j: (j, 0, 0))],
        out_specs=pl.BlockSpec((S, CT), lambda j: (0, j)),
        scratch_shapes=[pltpu.VMEM((S + 16, CT), f32)],
        compiler_params=_cp(("parallel",)))(proj, proj, proj, proj, convw)


def branchc1_fwd(proj, convw, cbias, name):
    def body(cu, cv, w_ref, b_ref, o_ref, pad):
        _fill_pad(pad, 16, lambda rows: cu[rows, :] * _sig(cv[rows, :]))

        def step(i, carry):
            base = pl.multiple_of(i * RC, RC)
            acc = jnp.zeros((RC, CT), f32) + b_ref[...]
            for k in range(CC_W):
                acc = acc + w_ref[0, 8 + k:9 + k, :] * pad[pl.ds(base + k + 1, RC), :]
            o_ref[pl.ds(base, RC), :] = acc
            return carry
        lax.fori_loop(0, S // RC, step, 0)

    return pl.pallas_call(
        body, name=name, grid=(D // CT,), out_shape=jax.ShapeDtypeStruct((S, D), f32),
        in_specs=[_chan_spec("c_u"), _chan_spec("c_v"), pl.BlockSpec((1, 40, CT), lambda j: (j, 0, 0)),
                  pl.BlockSpec((1, CT), lambda j: (0, j))],
        out_specs=pl.BlockSpec((S, CT), lambda j: (0, j)),
        scratch_shapes=[pltpu.VMEM((S + 32, CT), f32)],
        compiler_params=_cp(("parallel",)))(proj, proj, convw, cbias)


def _swap32(x):
    lane = lax.broadcasted_iota(jnp.int32, x.shape, 1)
    return jnp.where((lane // 32) % 2 == 1, pltpu.roll(x, 32, 1), pltpu.roll(x, HD - 32, 1))


def _rope(y, cos, sin):
    return y * cos + _swap32(y) * sin


def qkv_fwd(proj, qn, kn, cos, sin, name):
    tm = min(S, 512)
    nq, nk, nv = len(_row_specs("q", tm)), len(_row_specs("k", tm)), len(_row_specs("v", tm))

    def body(*refs):
        q = _cat(refs[:nq])
        k = _cat(refs[nq:nq + nk])
        v = _cat(refs[nq + nk:nq + nk + nv])
        qn_ref, kn_ref, cos_ref, sin_ref, qh_ref, kh_ref, vh_ref = refs[nq + nk + nv:]
        cos, sin = cos_ref[...], sin_ref[...]

        def heads(xx, gn, out_ref, n):
            for h in range(n):
                xh = xx[:, h * HD:(h + 1) * HD]
                r = lax.rsqrt(jnp.mean(xh * xh, axis=-1, keepdims=True) + RMS_EPS)
                out_ref[:, h * HD:(h + 1) * HD] = _rope(xh * r * gn, cos, sin).astype(bf16)
        heads(q, qn_ref[...], qh_ref, NQ)
        heads(k, kn_ref[...], kh_ref, NKV)
        vh_ref[...] = v.astype(bf16)

    row = lambda w: pl.BlockSpec((tm, w), lambda i: (i, 0))
    return pl.pallas_call(
        body, name=name, grid=(S // tm,),
        out_shape=[jax.ShapeDtypeStruct((S, D), bf16), jax.ShapeDtypeStruct((S, WKV), bf16), jax.ShapeDtypeStruct((S, WKV), bf16)],
        in_specs=_row_specs("q", tm) + _row_specs("k", tm) + _row_specs("v", tm) + [_full((1, HD)), _full((1, HD)), row(HD), row(HD)],
        out_specs=[row(D), row(WKV), row(WKV)],
        compiler_params=_cp(("parallel",)))(*([proj] * (nq + nk + nv)), qn, kn, cos, sin)


def _softmax_rows(q, k):
    s = lax.dot_general(q, k, (((1,), (1,)), ((), ())), preferred_element_type=f32)
    p = jnp.exp((s - jnp.max(s, axis=-1, keepdims=True)) * (HD ** -0.5))
    return p, 1.0 / jnp.sum(p, axis=-1, keepdims=True)


GW = G * HD


def attn_fwd(qh, kh, vh, proj, name):
    tq = min(S, 512)
    bg_blk = _OFF["b_g"][0] // GW

    def body(q_ref, k_ref, v_ref, bg_ref, o_ref, y_ref):
        k, v = k_ref[...], v_ref[...]
        for g in range(G):
            cols = slice(g * HD, (g + 1) * HD)
            p, rl = _softmax_rows(q_ref[:, cols], k)
            o = jnp.dot(p.astype(bf16), v, preferred_element_type=f32) * rl
            o_ref[:, cols] = o
            y_ref[:, cols] = (o * _silu(bg_ref[:, cols])).astype(bf16)

    grp = pl.BlockSpec((tq, GW), lambda kv, i: (i, kv))
    kvs = pl.BlockSpec((S, HD), lambda kv, i: (0, kv))
    return pl.pallas_call(
        body, name=name, grid=(NKV, S // tq),
        out_shape=[jax.ShapeDtypeStruct((S, D), f32), jax.ShapeDtypeStruct((S, D), bf16)],
        in_specs=[grp, kvs, kvs, pl.BlockSpec((tq, GW), lambda kv, i: (i, bg_blk + kv))],
        out_specs=[grp, grp],
        compiler_params=_cp(("parallel", "parallel")))(qh, kh, vh, proj)


def _ln_parts(u1):
    mu = jnp.mean(u1, axis=-1, keepdims=True)
    xc = u1 - mu
    rstd = lax.rsqrt(jnp.mean(xc * xc, axis=-1, keepdims=True) + LN_EPS)
    return xc * rstd, rstd


def _after(after):
    return ([], []) if after is None else ([after], [pl.BlockSpec(memory_space=pl.ANY)])


def branchc2_fwd(u1, proj, lng, lnb, name, after=None):
    tm = min(S, 512)
    ncg = len(_row_specs("c_g", tm))
    extra, extra_specs = _after(after)

    def body(*refs):
        u_ref = refs[0]
        cg = _cat(refs[1:1 + ncg])
        g_ref, b_ref = refs[1 + ncg:3 + ncg]
        xh, _ = _ln_parts(u_ref[...])
        refs[-1][...] = (_silu(xh * g_ref[...] + b_ref[...]) * _silu(cg)).astype(bf16)

    row = pl.BlockSpec((tm, D), lambda i: (i, 0))
    return pl.pallas_call(
        body, name=name, grid=(S // tm,), out_shape=jax.ShapeDtypeStruct((S, D), bf16),
        in_specs=[row] + _row_specs("c_g", tm) + [_full((1, D)), _full((1, D))] + extra_specs, out_specs=row,
        compiler_params=_cp(("parallel",)))(u1, *([proj] * ncg), lng, lnb, *extra)


def _wmat(w_ref, kind):
    return w_ref[:, kind].reshape(D, D)


def merge_fwd(xin, yah, ybh, ych, proj, wsq, g_post, name):
    tm = min(S, 512)
    nm = len(_row_specs("m_a", tm))

    def body(*refs):
        x_ref, a_ref, b_ref, c_ref = refs[:4]
        ms = [_cat(refs[4 + t * nm:4 + (t + 1) * nm]) for t in range(3)]
        w_ref, g_ref, ya_ref, yb_ref, yc_ref, y_ref, z_ref, o_ref = refs[4 + 3 * nm:]
        y = jnp.zeros((tm, D), f32)
        for t, (h_ref, out_ref) in enumerate(((a_ref, ya_ref), (b_ref, yb_ref), (c_ref, yc_ref))):
            yt = jnp.dot(h_ref[...], _wmat(w_ref, t), preferred_element_type=f32)
            out_ref[...] = yt
            y = y + _sig(ms[t]) * yt
        yb16 = y.astype(bf16)
        y_ref[...] = yb16
        z = jnp.dot(yb16, _wmat(w_ref, 3), preferred_element_type=f32)
        z_ref[...] = z
        r = lax.rsqrt(jnp.mean(z * z, axis=-1, keepdims=True) + RMS_EPS)
        o_ref[...] = x_ref[...] + z * r * g_ref[...]

    row = pl.BlockSpec((tm, D), lambda i: (i, 0))
    sd = lambda dt: jax.ShapeDtypeStruct((S, D), dt)
    return pl.pallas_call(
        body, name=name, grid=(S // tm,),
        out_shape=[sd(f32), sd(f32), sd(f32), sd(bf16), sd(f32), sd(f32)],
        in_specs=[row] * 4 + _row_specs("m_a", tm) + _row_specs("m_b", tm) + _row_specs("m_c", tm)
        + [pl.BlockSpec((NDEV, 4, D // NDEV, D), lambda *_: (0, 0, 0, 0), pipeline_mode=pl.Buffered(1)), _full((1, D))],
        out_specs=[row] * 6,
        compiler_params=_cp(("parallel",)))(xin, yah, ybh, ych, *([proj] * (3 * nm)), wsq, g_post)


def loss_fwd(y, target, name):
    tm = min(S, 256)

    def body(y_ref, t_ref, dy_ref, l_ref):
        e = y_ref[...] - t_ref[...]
        dy_ref[...] = e / D

        @pl.when(pl.program_id(0) == 0)
        def _():
            l_ref[...] = jnp.zeros((1, 128), f32)
        l_ref[...] += (0.5 / D) * jnp.sum(e * e)

    row = pl.BlockSpec((tm, D), lambda i: (i, 0))
    return pl.pallas_call(
        body, name=name, grid=(S // tm,),
        out_shape=[jax.ShapeDtypeStruct((S, D), f32), jax.ShapeDtypeStruct((1, 128), f32)],
        in_specs=[row, row], out_specs=[row, _full((1, 128))],
        compiler_params=_cp(("arbitrary",)))(y, target)


def _acc(ref, val):
    @pl.when(pl.program_id(0) == 0)
    def _():
        ref[...] = jnp.zeros(ref.shape, f32)
    ref[...] += val


def _emit_copies(stash, dst, sems, windows):
    return [pltpu.make_async_copy(stash.at[p], dst.at[w], sems.at[p]) for p, w in enumerate(windows)]


def _emit_drain_previous(copies, step):
    @pl.when(step > 0)
    def _():
        for cp in copies:
            cp.wait()


def _emit_start(copies, step, nsteps):
    for cp in copies:
        cp.start()

    @pl.when(step == nsteps - 1)
    def _():
        for cp in copies:
            cp.wait()


def merge_bwd(dout, z, ya, yb, yc, proj, wsq, g_post, name):
    tm = min(S, 256)
    nm = len(_row_specs("m_a", tm))
    nsteps = S // tm

    def body(*refs):
        do_ref, z_ref, ya_ref, yb_ref, yc_ref = refs[:5]
        ms = [_cat(refs[5 + t * nm:5 + (t + 1) * nm]) for t in range(3)]
        w_ref, g_ref = refs[5 + 3 * nm:7 + 3 * nm]
        dh_refs = refs[7 + 3 * nm:10 + 3 * nm]
        dzb_ref = refs[10 + 3 * nm]
        dyb_refs = refs[11 + 3 * nm:14 + 3 * nm]
        dg_ref = refs[14 + 3 * nm]
        dproj_ref, stash, sems = refs[15 + 3 * nm:]
        i = pl.program_id(0)
        rows = pl.ds(pl.multiple_of(i * tm, tm), tm)
        copies = _emit_copies(stash, dproj_ref, sems, [(rows, pl.ds(_OFF[n][0], D)) for n in ("m_a", "m_b", "m_c")])
        nt = (((1,), (1,)), ((), ()))
        z, dout = z_ref[...], do_ref[...]
        r = lax.rsqrt(jnp.mean(z * z, axis=-1, keepdims=True) + RMS_EPS)
        zh = z * r
        _acc(dg_ref, jnp.sum(dout * zh, axis=0, keepdims=True))
        dzh = dout * g_ref[...]
        dz = (r * (dzh - zh * jnp.mean(dzh * zh, axis=-1, keepdims=True))).astype(bf16)
        dzb_ref[...] = dz
        dy = lax.dot_general(dz, _wmat(w_ref, 3), nt, preferred_element_type=f32)
        dms = []
        for t, yt_ref in enumerate((ya_ref, yb_ref, yc_ref)):
            sg = _sig(ms[t])
            dyt = (dy * sg).astype(bf16)
            dyb_refs[t][...] = dyt
            dms.append((dy * yt_ref[...] * sg * (1.0 - sg)).astype(bf16))
            dh_refs[t][...] = lax.dot_general(dyt, _wmat(w_ref, t), nt, preferred_element_type=f32)
        _emit_drain_previous(copies, i)
        for t in range(3):
            stash[t] = dms[t]
        _emit_start(copies, i, nsteps)

    row = pl.BlockSpec((tm, D), lambda i: (i, 0))
    sd = lambda dt: jax.ShapeDtypeStruct((S, D), dt)
    return pl.pallas_call(
        body, name=name, grid=(nsteps,),
        out_shape=[sd(f32)] * 3 + [sd(bf16)] * 4 + [jax.ShapeDtypeStruct((1, D), f32), jax.ShapeDtypeStruct((S, P), bf16)],
        in_specs=[row] * 5 + _row_specs("m_a", tm) + _row_specs("m_b", tm) + _row_specs("m_c", tm)
        + [pl.BlockSpec((NDEV, 4, D // NDEV, D), lambda *_: (0, 0, 0, 0), pipeline_mode=pl.Buffered(1)), _full((1, D))],
        out_specs=[row] * 7 + [_full((1, D)), pl.BlockSpec(memory_space=pl.ANY)],
        scratch_shapes=[pltpu.VMEM((3, tm, D), bf16), pltpu.SemaphoreType.DMA((3,))],
        compiler_params=_cp(("arbitrary",)))(dout, z, ya, yb, yc, *([proj] * (3 * nm)), wsq, g_post)


def tn_matmul(a, b, name):
    m, n = a.shape[1], b.shape[1]
    tmm = min(m, 512)

    def body(a_ref, b_ref, o_ref):
        o_ref[...] = lax.dot_general(a_ref[...], b_ref[...], (((0,), (0,)), ((), ())), preferred_element_type=f32).astype(bf16)

    return pl.pallas_call(
        body, name=name, grid=(m // tmm,), out_shape=jax.ShapeDtypeStruct((m, n), bf16),
        in_specs=[pl.BlockSpec((S, tmm), lambda i: (0, i)), _full((S, n))],
        out_specs=pl.BlockSpec((tmm, n), lambda i: (i, 0)),
        compiler_params=_cp(("parallel",)))(a, b)


def dwin_parts(h, dproj, name):
    tn = 1280

    def body(d_ref, h_ref, o_ref):
        o_ref[...] = lax.dot_general(d_ref[...], h_ref[...], (((0,), (0,)), ((), ())), preferred_element_type=f32).astype(bf16)

    return pl.pallas_call(
        body, name=name, grid=(P // tn,), out_shape=jax.ShapeDtypeStruct((P, D), bf16),
        in_specs=[pl.BlockSpec((S, tn), lambda j: (0, j)), _full((S, D))],
        out_specs=pl.BlockSpec((tn, D), lambda j: (j, 0)),
        compiler_params=_cp(("parallel",)))(dproj, h)


def _chan_windows(names, j):
    return [(slice(None), pl.ds(pl.multiple_of(_OFF[n][0] + j * CT, CT), CT)) for n in names]


def brancha_bwd(dyah, proj, convw, dproj, name, after=None):
    nsteps = D // CT
    extra, extra_specs = _after(after)

    def body(d_ref, ab, ac, ax, ag, w_ref, *rest):
        dw_ref, dproj_ref, padp, padt, accw, stash, sems = rest[-7:]
        j = pl.program_id(0)
        copies = _emit_copies(stash, dproj_ref, sems, _chan_windows(("a_b", "a_c", "a_x", "a_g"), j))
        _fill_pad(padp, 8, lambda rows: ac[rows, :] * ax[rows, :])
        _fill_pad(padt, 8, lambda rows: d_ref[rows, :] * ab[rows, :] * _silu(ag[rows, :]))
        accw[...] = jnp.zeros(accw.shape, f32)
        w = [w_ref[0, k:k + 1, :] for k in range(CA_W)]
        _emit_drain_previous(copies, j)

        def step(i, carry):
            base = pl.multiple_of(i * RC, RC)
            rows = pl.ds(base, RC)
            ps = [padp[pl.ds(base + 7 + k, RC), :] for k in range(CA_W)]
            t = sum(w[k] * ps[k] for k in range(CA_W))
            dp = sum(w[k] * padt[pl.ds(base + 9 - k, RC), :] for k in range(CA_W))
            d, a_b, a_g = d_ref[rows, :], ab[rows, :], ag[rows, :]
            stash[0, rows, :] = (d * t * _silu(a_g)).astype(bf16)
            stash[1, rows, :] = (dp * ax[rows, :]).astype(bf16)
            stash[2, rows, :] = (dp * ac[rows, :]).astype(bf16)
            stash[3, rows, :] = (d * a_b * t * _dsilu(a_g)).astype(bf16)
            dt = padt[pl.ds(base + 8, RC), :]
            for k in range(CA_W):
                accw[8 * k:8 * k + 8, :] += jnp.sum((dt * ps[k]).reshape(RC // 8, 8, CT), axis=0)
            return carry
        lax.fori_loop(0, S // RC, step, 0)
        _emit_start(copies, j, nsteps)
        dw_ref[0] = jnp.zeros((8, CT), f32)
        for k in range(CA_W):
            dw_ref[0, k:k + 1, :] = jnp.sum(accw[8 * k:8 * k + 8, :], axis=0, keepdims=True)

    tile = pl.BlockSpec((S, CT), lambda j: (0, j))
    anyspec = pl.BlockSpec(memory_space=pl.ANY)
    return pl.pallas_call(
        body, name=name, grid=(nsteps,),
        out_shape=[jax.ShapeDtypeStruct((NDEV, 8, CT), f32), jax.ShapeDtypeStruct((S, P), bf16)],
        in_specs=[tile, _chan_spec("a_b"), _chan_spec("a_c"), _chan_spec("a_x"), _chan_spec("a_g"),
                  pl.BlockSpec((1, 40, CT), lambda j: (j, 0, 0)), anyspec] + extra_specs,
        out_specs=[pl.BlockSpec((1, 8, CT), lambda j: (j, 0, 0)), anyspec],
        input_output_aliases={6: 1},
        scratch_shapes=[pltpu.VMEM((S + 16, CT), f32), pltpu.VMEM((S + 16, CT), f32), pltpu.VMEM((8 * CA_W, CT), f32),
                        pltpu.VMEM((4, S, CT), bf16), pltpu.SemaphoreType.DMA((4,))],
        compiler_params=_cp(("arbitrary",)))(dyah, proj, proj, proj, proj, convw, dproj, *extra)


def branchc2_bwd(dych, u1, proj, lng, lnb, dproj, name):
    tm = min(S, 512)
    ncg = len(_row_specs("c_g", tm))
    nsteps = S // tm

    def body(*refs):
        d_ref, u_ref = refs[:2]
        cg = _cat(refs[2:2 + ncg])
        g_ref, b_ref, _, du_ref, dlg_ref, dlb_ref, dcb_ref, dproj_ref, stash, sems = refs[2 + ncg:]
        i = pl.program_id(0)
        copies = _emit_copies(stash, dproj_ref, sems, [(pl.ds(pl.multiple_of(i * tm, tm), tm), pl.ds(_OFF["c_g"][0], D))])
        d = d_ref[...]
        xh, rstd = _ln_parts(u_ref[...])
        ln = xh * g_ref[...] + b_ref[...]
        _emit_drain_previous(copies, i)
        stash[0] = (d * _silu(ln) * _dsilu(cg)).astype(bf16)
        _emit_start(copies, i, nsteps)
        dln = d * _silu(cg) * _dsilu(ln)
        _acc(dlg_ref, jnp.sum(dln * xh, axis=0, keepdims=True))
        _acc(dlb_ref, jnp.sum(dln, axis=0, keepdims=True))
        dxh = dln * g_ref[...]
        du = rstd * (dxh - jnp.mean(dxh, axis=-1, keepdims=True) - xh * jnp.mean(dxh * xh, axis=-1, keepdims=True))
        du_ref[...] = du
        _acc(dcb_ref, jnp.sum(du, axis=0, keepdims=True))

    row = pl.BlockSpec((tm, D), lambda i: (i, 0))
    vec = jax.ShapeDtypeStruct((1, D), f32)
    anyspec = pl.BlockSpec(memory_space=pl.ANY)
    return pl.pallas_call(
        body, name=name, grid=(nsteps,),
        out_shape=[jax.ShapeDtypeStruct((S, D), f32), vec, vec, vec, jax.ShapeDtypeStruct((S, P), bf16)],
        in_specs=[row, row] + _row_specs("c_g", tm) + [_full((1, D)), _full((1, D)), anyspec],
        out_specs=[row, _full((1, D)), _full((1, D)), _full((1, D)), anyspec],
        input_output_aliases={4 + ncg: 4},
        scratch_shapes=[pltpu.VMEM((1, tm, D), bf16), pltpu.SemaphoreType.DMA((1,))],
        compiler_params=_cp(("arbitrary",)))(dych, u1, *([proj] * ncg), lng, lnb, dproj)


def branchc1_bwd(du1, proj, convw, dproj, name):
    nsteps = D // CT

    def body(d_ref, cu, cv, w_ref, _, dw_ref, dproj_ref, padu, padd, accw, stash, sems):
        j = pl.program_id(0)
        copies = _emit_copies(stash, dproj_ref, sems, _chan_windows(("c_u", "c_v"), j))
        _fill_pad(padu, 16, lambda rows: cu[rows, :] * _sig(cv[rows, :]))
        _fill_pad(padd, 16, lambda rows: d_ref[rows, :])
        accw[...] = jnp.zeros(accw.shape, f32)
        _emit_drain_previous(copies, j)

        rc = min(S, 128)

        def step(i, carry):
            base = pl.multiple_of(i * rc, rc)
            rows = pl.ds(base, rc)
            d = d_ref[rows, :]
            du0 = jnp.zeros((rc, CT), f32)
            for k in range(CC_W):
                du0 = du0 + w_ref[0, 8 + k:9 + k, :] * padd[pl.ds(base + 31 - k, rc), :]
                accw[8 * k:8 * k + 8, :] += jnp.sum((d * padu[pl.ds(base + k + 1, rc), :]).reshape(rc // 8, 8, CT), axis=0)
            sg = _sig(cv[rows, :])
            stash[0, rows, :] = (du0 * sg).astype(bf16)
            stash[1, rows, :] = (du0 * cu[rows, :] * sg * (1.0 - sg)).astype(bf16)
            return carry
        lax.fori_loop(0, S // rc, step, 0)
        _emit_start(copies, j, nsteps)
        dw_ref[0] = jnp.zeros((32, CT), f32)
        for k in range(CC_W):
            dw_ref[0, k:k + 1, :] = jnp.sum(accw[8 * k:8 * k + 8, :], axis=0, keepdims=True)

    tile = pl.BlockSpec((S, CT), lambda j: (0, j))
    anyspec = pl.BlockSpec(memory_space=pl.ANY)
    return pl.pallas_call(
        body, name=name, grid=(nsteps,),
        out_shape=[jax.ShapeDtypeStruct((NDEV, 32, CT), f32), jax.ShapeDtypeStruct((S, P), bf16)],
        in_specs=[tile, _chan_spec("c_u"), _chan_spec("c_v"), pl.BlockSpec((1, 40, CT), lambda j: (j, 0, 0)), anyspec],
        out_specs=[pl.BlockSpec((1, 32, CT), lambda j: (j, 0, 0)), anyspec],
        input_output_aliases={4: 1},
        scratch_shapes=[pltpu.VMEM((S + 32, CT), f32), pltpu.VMEM((S + 32, CT), f32), pltpu.VMEM((8 * 32, CT), f32),
                        pltpu.VMEM((2, S, CT), bf16), pltpu.SemaphoreType.DMA((2,))],
        compiler_params=_cp(("arbitrary",)))(du1, proj, proj, convw, dproj)


def attn_bwd(dybh, o, qh, kh, vh, proj, dproj, name):
    tq = min(S, 512)
    bg_blk = _OFF["b_g"][0] // GW

    def body(d_ref, o_ref, q_ref, k_ref, v_ref, bg_ref, _, dq_ref, dk_ref, dv_ref, dbg_ref):
        @pl.when(pl.program_id(1) == 0)
        def _():
            dk_ref[...] = jnp.zeros(dk_ref.shape, f32)
            dv_ref[...] = jnp.zeros(dv_ref.shape, f32)
        k, v = k_ref[...], v_ref[...]
        tn = (((0,), (0,)), ((), ()))
        dk_acc = jnp.zeros((S, HD), f32)
        dv_acc = jnp.zeros((S, HD), f32)
        for g in range(G):
            cols = slice(g * HD, (g + 1) * HD)
            d, bg, q, o = d_ref[:, cols], bg_ref[:, cols], q_ref[:, cols], o_ref[:, cols]
            dbg_ref[:, cols] = (d * o * _dsilu(bg)).astype(bf16)
            do = d * _silu(bg)
            p, rl = _softmax_rows(q, k)
            dv_acc = dv_acc + lax.dot_general(p.astype(bf16), (do * rl).astype(bf16), tn, preferred_element_type=f32)
            dp = lax.dot_general(do.astype(bf16), v, (((1,), (1,)), ((), ())), preferred_element_type=f32)
            delta = jnp.sum(do * o, axis=-1, keepdims=True)
            ds = (p * (dp - delta)).astype(bf16)
            rs_ = rl * (HD ** -0.5)
            dq_ref[:, cols] = jnp.dot(ds, k, preferred_element_type=f32) * rs_
            dk_acc = dk_acc + lax.dot_general(ds, (q.astype(f32) * rs_).astype(bf16), tn, preferred_element_type=f32)
        dk_ref[...] += dk_acc
        dv_ref[...] += dv_acc

    grp = pl.BlockSpec((tq, GW), lambda kv, i: (i, kv))
    kvs = pl.BlockSpec((S, HD), lambda kv, i: (0, kv))
    return pl.pallas_call(
        body, name=name, grid=(NKV, S // tq),
        out_shape=[jax.ShapeDtypeStruct((S, D), f32), jax.ShapeDtypeStruct((S, WKV), f32),
                   jax.ShapeDtypeStruct((S, WKV), f32), jax.ShapeDtypeStruct((S, P), bf16)],
        in_specs=[grp, grp, grp, kvs, kvs, pl.BlockSpec((tq, GW), lambda kv, i: (i, bg_blk + kv)),
                  pl.BlockSpec(memory_space=pl.ANY)],
        out_specs=[grp, kvs, kvs, pl.BlockSpec((tq, GW), lambda kv, i: (i, bg_blk + kv))],
        input_output_aliases={6: 3},
        compiler_params=_cp(("parallel", "arbitrary")))(dybh, o, qh, kh, vh, proj, dproj)


def qkv_bwd(dqh, dkh, dvh, proj, qn, kn, cos, sin, dproj, name):
    tm = min(S, 512)
    nq, nk = len(_row_specs("q", tm)), len(_row_specs("k", tm))
    nsteps = S // tm
    wq = D + 2 * WKV

    def body(*refs):
        dqh_ref, dkh_ref, dvh_ref = refs[:3]
        q = _cat(refs[3:3 + nq])
        k = _cat(refs[3 + nq:3 + nq + nk])
        qn_ref, kn_ref, cos_ref, sin_ref, _, dqn_ref, dkn_ref, dproj_ref, stash, sems = refs[3 + nq + nk:]
        i = pl.program_id(0)
        copies = _emit_copies(stash, dproj_ref, sems, [(pl.ds(pl.multiple_of(i * tm, tm), tm), pl.ds(_OFF["q"][0], wq))])
        cos, sin = cos_ref[...], sin_ref[...]
        _emit_drain_previous(copies, i)

        def heads(xx, dd, gn, col0, dgn_ref, n):
            dg = jnp.zeros((1, HD), f32)
            for h in range(n):
                xh = xx[:, h * HD:(h + 1) * HD]
                dh = dd[:, h * HD:(h + 1) * HD]
                r = lax.rsqrt(jnp.mean(xh * xh, axis=-1, keepdims=True) + RMS_EPS)
                xn = xh * r
                dy = dh * cos + _swap32(dh * sin)
                dg = dg + jnp.sum(dy * xn, axis=0, keepdims=True)
                dxn = dy * gn
                stash[0, :, col0 + h * HD:col0 + (h + 1) * HD] = (
                    r * (dxn - xn * jnp.mean(dxn * xn, axis=-1, keepdims=True))).astype(bf16)
            _acc(dgn_ref, dg)
        heads(q, dqh_ref[...], qn_ref[...], 0, dqn_ref, NQ)
        heads(k, dkh_ref[...], kn_ref[...], D, dkn_ref, NKV)
        stash[0, :, D + WKV:wq] = dvh_ref[...].astype(bf16)
        _emit_start(copies, i, nsteps)

    row = lambda w: pl.BlockSpec((tm, w), lambda i: (i, 0))
    vec = jax.ShapeDtypeStruct((1, HD), f32)
    anyspec = pl.BlockSpec(memory_space=pl.ANY)
    return pl.pallas_call(
        body, name=name, grid=(nsteps,),
        out_shape=[vec, vec, jax.ShapeDtypeStruct((S, P), bf16)],
        in_specs=[row(D), row(WKV), row(WKV)] + _row_specs("q", tm) + _row_specs("k", tm)
        + [_full((1, HD)), _full((1, HD)), row(HD), row(HD), anyspec],
        out_specs=[_full((1, HD)), _full((1, HD)), anyspec],
        input_output_aliases={7 + nq + nk: 2},
        scratch_shapes=[pltpu.VMEM((1, tm, wq), bf16), pltpu.SemaphoreType.DMA((1,))],
        compiler_params=_cp(("arbitrary",)))(dqh, dkh, dvh, *([proj] * (nq + nk)), qn, kn, cos, sin, dproj)


def dh_bwd(dproj, wfull, xin, dout, g_pre, name, after=None):
    tm, tk = min(S, 1024), 2560
    nk = P // tk
    extra, extra_specs = _after(after)

    def body(d_ref, w_ref, x_ref, do_ref, g_ref, *rest):
        dx_ref, dg_ref, acc = rest[-3:]
        kk = pl.program_id(1)

        @pl.when(kk == 0)
        def _():
            acc[...] = jnp.zeros(acc.shape, f32)
        acc[...] += jnp.dot(d_ref[...], w_ref[...], preferred_element_type=f32)

        @pl.when((kk == 0) & (pl.program_id(0) == 0))
        def _():
            dg_ref[...] = jnp.zeros(dg_ref.shape, f32)

        @pl.when(kk == nk - 1)
        def _():
            x, dh = x_ref[...], acc[...]
            r = lax.rsqrt(jnp.mean(x * x, axis=-1, keepdims=True) + RMS_EPS)
            xn = x * r
            dg_ref[...] += jnp.sum(dh * xn, axis=0, keepdims=True)
            dxn = dh * g_ref[...]
            dx_ref[...] = do_ref[...] + r * (dxn - xn * jnp.mean(dxn * xn, axis=-1, keepdims=True))

    row = pl.BlockSpec((tm, D), lambda i, k: (i, 0))
    return pl.pallas_call(
        body, name=name, grid=(S // tm, nk),
        out_shape=[jax.ShapeDtypeStruct((S, D), f32), jax.ShapeDtypeStruct((1, D), f32)],
        in_specs=[pl.BlockSpec((tm, tk), lambda i, k: (i, k)), pl.BlockSpec((tk, D), lambda i, k: (k, 0)), row, row, _full((1, D))]
        + extra_specs,
        out_specs=[row, _full((1, D))],
        scratch_shapes=[pltpu.VMEM((tm, D), f32)],
        compiler_params=_cp(("arbitrary", "arbitrary")))(dproj, wfull, xin, dout, g_pre, *extra)


def adam_update(parts, own, me, w, m, v, l, acc, name):
    lw, r, c = w.shape
    tr = _row_tile(r)
    nslots = parts.shape[0]

    def body(me_ref, p_ref, own_ref, w_ref, m_ref, v_ref, *rest):
        g_ref, d_ref, nm_ref, nv_ref = rest[-4:]
        g = None
        for s in range(nslots):
            part = jnp.where(me_ref[0] == s, own_ref[0], p_ref[s]).astype(f32)
            g = part if g is None else g + part
        nm = ADAM_B1 * m_ref[0] + (1.0 - ADAM_B1) * g
        nv = ADAM_B2 * v_ref[0] + (1.0 - ADAM_B2) * (g * g)
        m_hat = nm / (1.0 - ADAM_B1 ** ADAM_STEP)
        v_hat = nv / (1.0 - ADAM_B2 ** ADAM_STEP)
        g_ref[0] = g
        d_ref[0] = -ADAM_LR * (m_hat / (jnp.sqrt(v_hat) + ADAM_EPS) + ADAM_WD * w_ref[0])
        nm_ref[0] = nm
        nv_ref[0] = nv

    blk = pl.BlockSpec((1, tr, c), lambda i, me_ref: (l, i, 0))
    sd = jax.ShapeDtypeStruct((lw, r, c), f32)
    extra = [] if acc is None else list(acc)
    return pl.pallas_call(
        body, name=name, out_shape=[sd] * 4,
        grid_spec=pltpu.PrefetchScalarGridSpec(
            num_scalar_prefetch=1, grid=(r // tr,),
            in_specs=[pl.BlockSpec((nslots, tr, c), lambda i, me_ref: (0, i, 0)),
                      pl.BlockSpec((1, tr, c), lambda i, me_ref: (me_ref[0], i, 0)), blk, blk, blk]
            + [pl.BlockSpec(memory_space=pl.ANY)] * len(extra),
            out_specs=[blk] * 4),
        input_output_aliases={6 + t: t for t in range(len(extra))},
        compiler_params=_cp(("parallel",)))(me, parts, own, w, m, v, *extra)


def _rope_tables():
    t = jnp.arange(S)
    rows, cols = (t // GRID_W).astype(f32), (t % GRID_W).astype(f32)
    nf = HD // 4
    inv = ROPE_THETA ** (-jnp.arange(nf, dtype=f32) / nf)
    ar, ac = rows[:, None] * inv, cols[:, None] * inv
    cos = jnp.concatenate([jnp.cos(ar), jnp.cos(ar), jnp.cos(ac), jnp.cos(ac)], axis=1)
    sin = jnp.concatenate([-jnp.sin(ar), jnp.sin(ar), -jnp.sin(ac), jnp.sin(ac)], axis=1)
    return cos, sin


def _pack_conv(ca, cc):
    z = lambda n: jnp.zeros((L, n, CT), f32)
    return jnp.concatenate([ca, z(5), cc, z(1)], axis=1)


def _pack_small(npre, npost, ccb, lng, lnb, qn, kn):
    wide = lambda a: jnp.pad(a, ((0, 0), (0, D - HD)))
    return jnp.stack([npre, npost, ccb, lng, lnb, wide(qn), wide(kn), jnp.zeros((L, D), f32)], axis=1).reshape(L * 8, D)


def kernel(x, norm_pre, norm_post, w_in, conv_a_w, q_norm, k_norm, conv_c_w, conv_c_b, ln_c_g, ln_c_b, w_out_a, w_out_b, w_out_c, w_o, loss_target, m_norm_pre, m_norm_post, m_w_in, m_conv_a_w, m_q_norm, m_k_norm, m_conv_c_w, m_conv_c_b, m_ln_c_g, m_ln_c_b, m_w_out_a, m_w_out_b, m_w_out_c, m_w_o, v_norm_pre, v_norm_post, v_w_in, v_conv_a_w, v_q_norm, v_k_norm, v_conv_c_w, v_conv_c_b, v_ln_c_g, v_ln_c_b, v_w_out_a, v_w_out_b, v_w_out_c, v_w_o):
    cos, sin = _rope_tables()
    rs = D // NDEV
    stack_sq = lambda a, b, c, d: jnp.stack([a, b, c, d], axis=1)
    wsq32 = stack_sq(w_out_a, w_out_b, w_out_c, w_o)
    conv_pack = _pack_conv(conv_a_w, conv_c_w)
    vec = lambda a, l: a[l][None, :]
    me = (4 * lax.axis_index("x") + 2 * lax.axis_index("y") + lax.axis_index("c")).astype(jnp.int32).reshape(1)

    def gather_start(l, after):
        return split_start(staged[l], _gather_copies, 12, f"ag_start{l}", after=after)

    def forward_start(l, after):
        s_sems, r_sems, bufs, _ = gathers[l]
        bufs = split_wait(s_sems, r_sems, bufs, _gather_copies, after, f"ag_wait{l}")
        fw = split_start(bufs, _forward_copies, 9, f"ag_fwd_start{l}")
        if l + 1 < L:
            gathers[l + 1] = gather_start(l + 1, fw[3])
            return fw, gathers[l + 1][3]
        return fw, fw[3]

    def forward_wait(fw, after, l):
        s_sems, r_sems, bufs, _ = fw
        return split_wait(s_sems, r_sems, bufs, _forward_copies, after, f"ag_fwd_wait{l}")

    wt, m_wt, v_wt = (jnp.swapaxes(a, 1, 2) for a in (w_in, m_w_in, v_w_in))
    xs, saved = x.reshape(S, D), []
    stage = lambda l, after: stage_shards(wt, wsq32.reshape(L, 4 * rs, D), conv_pack, l, me, f"stage{l}", after)
    staged = [stage(0, None)]
    gathers = [gather_start(0, None)] + [None] * (L - 1)
    staged += [stage(l, gathers[0][3]) for l in range(1, L)]
    fw, issued = forward_start(0, staged[L - 1][0] if L > 1 else xs)
    wg, wsq, convw = forward_wait(fw, issued, 0)
    for l in range(L):
        wsq = wsq.reshape(NDEV, 4, rs, D)
        wfull = wg.reshape(P, D)
        proj, h = proj_fwd(xs, vec(norm_pre, l), wfull, f"proj{l}")
        yah = brancha_fwd(proj, convw, f"bra{l}")
        u1 = branchc1_fwd(proj, convw, vec(conv_c_b, l), f"brc1_{l}")
        qh, kh, vh = qkv_fwd(proj, vec(q_norm, l), vec(k_norm, l), cos, sin, f"qkv{l}")
        o, ybh = attn_fwd(qh, kh, vh, proj, f"attn{l}")
        issued = None
        if l + 1 < L:
            fw, issued = forward_start(l + 1, o)
        ych = branchc2_fwd(u1, proj, vec(ln_c_g, l), vec(ln_c_b, l), f"brc2_{l}", after=issued)
        ya, yb, yc, y16, z, xo = merge_fwd(xs, yah, ybh, ych, proj, wsq, vec(norm_post, l), f"merge{l}")
        saved.append(dict(x=xs, wfull=wfull, wsq=wsq, convw=convw, proj=proj, h=h, yah=yah, ybh=ybh, ych=ych, u1=u1,
                          qh=qh, kh=kh, vh=vh, o=o, ya=ya, yb=yb, yc=yc, y16=y16, z=z))
        xs = xo
        if l + 1 < L:
            wg, wsq, convw = forward_wait(fw, xs, l + 1)
    dx, loss_part = loss_fwd(xs, loss_target.reshape(S, D), "loss")
    loss = lax.psum(loss_part[0, 0], ("x", "y", "c"))

    acc = dict(win=None, sq=None, conv=None)
    small_parts = [None] * L
    msq32 = stack_sq(m_w_out_a, m_w_out_b, m_w_out_c, m_w_o)
    vsq32 = stack_sq(v_w_out_a, v_w_out_b, v_w_out_c, v_w_o)
    mconv, vconv = _pack_conv(m_conv_a_w, m_conv_c_w), _pack_conv(v_conv_a_w, v_conv_c_w)

    def scatter_start(parts, name, after=None):
        bufs = parts + [lax.empty(p.shape, p.dtype) for p in parts]
        return split_start(bufs, _scatter_copies(len(parts)), 7 * len(parts), name, after=after)

    def finish(l, started, after):
        (s1, r1, b1, _), (s2, r2, b2, _) = started
        gsq_own, rsq = split_wait(s1, r1, b1, _scatter_copies(1), after, f"rs_sq_wait{l}")
        gwin_own, gconv_own, rwin, rconv = split_wait(s2, r2, b2, _scatter_copies(2), after, f"rs_win_wait{l}")
        flat = lambda a: a.reshape(a.shape[0], 4 * rs, D)
        acc["win"] = adam_update(rwin, gwin_own, me, wt, m_wt, v_wt, l, acc["win"], f"adam_win{l}")
        acc["sq"] = adam_update(flat(rsq), flat(gsq_own), me, flat(wsq32), flat(msq32), flat(vsq32), l, acc["sq"], f"adam_wsq{l}")
        acc["conv"] = adam_update(rconv, gconv_own, me, conv_pack, mconv, vconv, l, acc["conv"], f"adam_conv{l}")

    pending = [None] * L
    for l in reversed(range(L)):
        sv = saved[l]
        proj = sv["proj"]
        (dyah, dybh, dych, dzb, dyab, dybb, dycb, dgpost, dproj) = merge_bwd(
            dx, sv["z"], sv["ya"], sv["yb"], sv["yc"], proj, sv["wsq"], vec(norm_post, l), f"merge_bwd{l}")
        gsq = [tn_matmul(a, b, f"dwsq{t}_{l}") for t, (a, b) in enumerate(
            ((sv["yah"], dyab), (sv["ybh"], dybb), (sv["ych"], dycb), (sv["y16"], dzb)))]
        gsq_parts = jnp.stack([g.reshape(NDEV, rs, D) for g in gsq], axis=1)
        st1 = scatter_start([gsq_parts], f"rs_sq_start{l}", after=loss.reshape(1, 1) if l == L - 1 else None)
        convw = sv["convw"]
        gca, dproj = brancha_bwd(dyah, proj, convw, dproj, f"bra_bwd{l}", after=st1[3])
        du1, dlg, dlb, dcb, dproj = branchc2_bwd(dych, sv["u1"], proj, vec(ln_c_g, l), vec(ln_c_b, l), dproj, f"brc2_bwd{l}")
        gcc, dproj = branchc1_bwd(du1, proj, convw, dproj, f"brc1_bwd{l}")
        dqh, dkh, dvh, dproj = attn_bwd(dybh, sv["o"], sv["qh"], sv["kh"], sv["vh"], proj, dproj, f"attn_bwd{l}")
        dqn, dkn, dproj = qkv_bwd(dqh, dkh, dvh, proj, vec(q_norm, l), vec(k_norm, l), cos, sin, dproj, f"qkv_bwd{l}")
        gwin = dwin_parts(sv["h"], dproj, f"dwin{l}").reshape(NDEV, PSH, D)
        gconv = jnp.concatenate([gca, gcc], axis=1)
        if l > 0:
            st2 = scatter_start([gwin, gconv], f"rs_win_start{l}")
            issued = st2[3]
        else:
            st2 = scatter_start([gconv], "rs_conv_start0")
            pair = split_start([gwin, lax.empty((NDEV // 2, PSH, D), bf16)], _pair_copies, NDEV // 2, "rs_pair_start0",
                               after=st2[3])
            issued = pair[3]
        dx, dgpre = dh_bwd(dproj, sv["wfull"], sv["x"], dx, vec(norm_pre, l), f"dh{l}", after=issued)
        wide = lambda a: jnp.pad(a, ((0, 0), (0, D - HD)))
        small_parts[l] = jnp.concatenate([dgpre, dgpost, dcb, dlg, dlb, wide(dqn), wide(dkn), jnp.zeros((1, D), f32)], axis=0)
        pending[l] = (st1, st2)

    gwin0, pair_land = split_wait(pair[0], pair[1], pair[2], _pair_copies, dx, "rs_pair_wait0")
    summed = pair_sum(gwin0, pair_land, me, "rs_pair_sum0")
    (small_all,) = all_gather([jnp.concatenate(small_parts, axis=0)], "ag_small")
    chip = split_start([summed, lax.empty(summed.shape, bf16)], _chip_copies, NDEV // 2 - 1, "rs_chip_start0", after=small_all)
    for l in reversed(range(1, L)):
        finish(l, pending[l], after=chip[3])
    sm = adam_update(small_all, small_all, me,
                     _pack_small(norm_pre, norm_post, conv_c_b, ln_c_g, ln_c_b, q_norm, k_norm)[None],
                     _pack_small(m_norm_pre, m_norm_post, m_conv_c_b, m_ln_c_g, m_ln_c_b, m_q_norm, m_k_norm)[None],
                     _pack_small(v_norm_pre, v_norm_post, v_conv_c_b, v_ln_c_g, v_ln_c_b, v_q_norm, v_k_norm)[None],
                     0, None, "adam_small")
    (s1, r1, b1, _), (s2, r2, b2, _) = pending[0]
    gsq_own, rsq = split_wait(s1, r1, b1, _scatter_copies(1), sm[0], "rs_sq_wait0")
    gconv_own, rconv = split_wait(s2, r2, b2, _scatter_copies(1), sm[0], "rs_conv_wait0")
    flat = lambda a: a.reshape(a.shape[0], 4 * rs, D)
    acc["sq"] = adam_update(flat(rsq), flat(gsq_own), me, flat(wsq32), flat(msq32), flat(vsq32), 0, acc["sq"], "adam_wsq0")
    acc["conv"] = adam_update(rconv, gconv_own, me, conv_pack, mconv, vconv, 0, acc["conv"], "adam_conv0")
    summed, chip_land = split_wait(chip[0], chip[1], chip[2], _chip_copies, acc["sq"][0], "rs_chip_wait0")
    acc["win"] = adam_update(chip_land, summed, me // 2, wt, m_wt, v_wt, 0, acc["win"], "adam_win0")
    sm = [a.reshape(L, 8, D) for a in sm]
    small_rows = dict(norm_pre=(0, D), norm_post=(1, D), conv_c_b=(2, D), ln_c_g=(3, D), ln_c_b=(4, D), q_norm=(5, HD), k_norm=(6, HD))
    sq_rows = dict(w_out_a=0, w_out_b=1, w_out_c=2, w_o=3)

    order = ["norm_pre", "norm_post", "w_in", "conv_a_w", "q_norm", "k_norm", "conv_c_w", "conv_c_b", "ln_c_g", "ln_c_b",
             "w_out_a", "w_out_b", "w_out_c", "w_o"]
    result = [loss, dx.reshape(1, S, D)]
    for kind in range(4):
        for nme in order:
            if nme in small_rows:
                rw, wd = small_rows[nme]
                result.append(sm[kind][:, rw, :wd])
            elif nme in sq_rows:
                result.append(acc["sq"][kind][:, sq_rows[nme] * rs:(sq_rows[nme] + 1) * rs])
            elif nme == "w_in":
                result.append(jnp.swapaxes(acc["win"][kind], 1, 2))
            elif nme == "conv_a_w":
                result.append(acc["conv"][kind][:, 0:CA_W])
            else:
                result.append(acc["conv"][kind][:, 8:8 + CC_W])
    return tuple(result)
```

```python
import math

import jax
import jax.numpy as jnp
from jax import lax
from jax.experimental import pallas as pl
from jax.experimental.pallas import tpu as pltpu

f32, bf16 = jnp.float32, jnp.bfloat16

D = 1024
S = 2048
L = 4
HD = 128
NQ = D // HD
NKV = NQ // 4
G = NQ // NKV
WKV = NKV * HD
GRID_W = 64
ROPE_THETA = 10000.0
RMS_EPS = 1e-6
LN_EPS = 1e-5
NDEV = 8
CA_W, CC_W = 3, 31
P = 12 * D + 2 * WKV
PSH = P // NDEV
CT = 128
ADAM_LR, ADAM_B1, ADAM_B2, ADAM_EPS, ADAM_WD, ADAM_STEP = 0.001, 0.9, 0.999, 1e-08, 0.01, 10
VMEM_LIMIT = 56 * 1024 * 1024
MESH = pl.DeviceIdType.MESH

_OFF = {}
_o = 0
for _n, _w in (("a_b", D), ("a_c", D), ("a_x", D), ("a_g", D), ("q", D), ("k", WKV), ("v", WKV), ("b_g", D),
               ("c_u", D), ("c_v", D), ("c_g", D), ("m_a", D), ("m_b", D), ("m_c", D)):
    _OFF[_n] = (_o, _w)
    _o += _w
PIECES = tuple(_OFF)


def _cp(sem=None, **kw):
    return pltpu.CompilerParams(dimension_semantics=sem, vmem_limit_bytes=VMEM_LIMIT, **kw)


def _sig(x):
    return 1.0 / (1.0 + jnp.exp(-x))


def _silu(x):
    return x * _sig(x)


def _dsilu(x):
    s = _sig(x)
    return s * (1.0 + x * (1.0 - s))


def _row_specs(name, tm):
    off, w = _OFF[name]
    bw = math.gcd(off, w) if off else w
    return [pl.BlockSpec((tm, bw), (lambda i, *_, b=off // bw + t: (i, b))) for t in range(w // bw)]


def _cat(refs):
    return refs[0][...] if len(refs) == 1 else jnp.concatenate([r[...] for r in refs], axis=1)


def _chan_spec(name):
    off, _ = _OFF[name]
    return pl.BlockSpec((S, CT), lambda j, b=off // CT: (0, b + j))


def _full(shape):
    return pl.BlockSpec(shape, lambda *_: (0,) * len(shape))


def _coords():
    return lax.axis_index("x"), lax.axis_index("y"), lax.axis_index("c")


def all_gather(shards, name):
    n = len(shards)

    def body(*refs):
        ins, outs = refs[:n], refs[n:2 * n]
        send_sems, recv_sems, local_sems = refs[2 * n:]
        x, y, c = _coords()
        me, sibling = (x, y, c), (x, y, 1 - c)
        chips = [(1 - x, y), (x, 1 - y), (1 - x, 1 - y)]

        def slot(a, p):
            return outs[a].at[4 * p[0] + 2 * p[1] + p[2]]

        def copy(a, k, block, to, src=None):
            return pltpu.make_async_remote_copy(
                src_ref=slot(a, block) if src is None else src, dst_ref=slot(a, block),
                send_sem=send_sems.at[7 * a + k], recv_sem=recv_sems.at[7 * a + k], device_id=to, device_id_type=MESH)

        mine = [pltpu.make_async_copy(ins[a], slot(a, me), local_sems.at[a]) for a in range(n)]
        for cp in mine:
            cp.start()
        first = []
        for a in range(n):
            first.append(copy(a, 0, me, sibling, src=ins[a]))
            first += [copy(a, 1 + j, me, (*chip, c), src=ins[a]) for j, chip in enumerate(chips)]
        for cp in first:
            cp.start()
        passed = []
        for j, chip in enumerate(chips):
            for a in range(n):
                copy(a, 1 + j, (*chip, c), me).wait_recv()
                fw = copy(a, 4 + j, (*chip, c), sibling)
                fw.start()
                passed.append(fw)
        for a in range(n):
            copy(a, 0, sibling, me).wait_recv()
            for j, chip in enumerate(chips):
                copy(a, 4 + j, (*chip, 1 - c), me).wait_recv()
        for cp in first + passed:
            cp.wait_send()
        for cp in mine:
            cp.wait()

    anyspec = pl.BlockSpec(memory_space=pl.ANY)
    return pl.pallas_call(
        body, name=name,
        out_shape=[jax.ShapeDtypeStruct((NDEV,) + s.shape, s.dtype) for s in shards],
        in_specs=[anyspec] * n, out_specs=[anyspec] * n,
        scratch_shapes=[pltpu.SemaphoreType.DMA((7 * n,)), pltpu.SemaphoreType.DMA((7 * n,)), pltpu.SemaphoreType.DMA((n,))],
    )(*shards)


_HBM = pl.BlockSpec(memory_space=pltpu.HBM)
_SEM = pl.BlockSpec(memory_space=pltpu.SEMAPHORE)
_EFFECT = pltpu.SideEffectType.DATAFLOW_SIDE_EFFECTING


def split_start(bufs, make_copies, nsem, name, after=None):
    n = len(bufs)
    extra = [] if after is None else [after]

    def body(*refs):
        send_sems, recv_sems = refs[n + len(extra):n + len(extra) + 2]
        for cp in make_copies(refs[:n], send_sems, recv_sems):
            cp.start()
        refs[-1][...] = jnp.zeros((8, 128), f32)

    res = pl.pallas_call(
        body, name=name,
        out_shape=(pltpu.SemaphoreType.DMA((nsem,)), pltpu.SemaphoreType.DMA((nsem,)),
                   *[pltpu.HBM(b.shape, b.dtype) for b in bufs], jax.ShapeDtypeStruct((8, 128), f32)),
        in_specs=[_HBM] * n + [pl.BlockSpec(memory_space=pl.ANY)] * len(extra),
        out_specs=(_SEM, _SEM, *([_HBM] * n), pl.BlockSpec(memory_space=pltpu.VMEM)),
        input_output_aliases={i: 2 + i for i in range(n)},
        compiler_params=pltpu.CompilerParams(has_side_effects=_EFFECT),
    )(*[pltpu.with_memory_space_constraint(b, pltpu.HBM) for b in bufs], *extra)
    return res[0], res[1], list(res[2:2 + n]), res[-1]


def split_wait(send_sems, recv_sems, bufs, make_copies, after, name):
    n = len(bufs)

    def body(*refs):
        for cp in make_copies(refs[:n], refs[n], refs[n + 1]):
            cp.wait_send()
            cp.wait_recv()

    res = pl.pallas_call(
        body, name=name,
        out_shape=tuple(pltpu.HBM(b.shape, b.dtype) for b in bufs),
        in_specs=[_HBM] * n + [_SEM, _SEM, pl.BlockSpec(memory_space=pl.ANY)],
        out_specs=[_HBM] * n,
        input_output_aliases={i: i for i in range(n)},
        compiler_params=pltpu.CompilerParams(has_side_effects=_EFFECT),
    )(*bufs, send_sems, recv_sems, after)
    return list(res)


def _scatter_copies(n):
    def make(refs, send_sems, recv_sems):
        x, y, c = _coords()
        me = 4 * x + 2 * y + c
        copies = []
        for a in range(n):
            for k in range(1, NDEV):
                px = 1 - x if (k >> 2) & 1 else x
                py = 1 - y if (k >> 1) & 1 else y
                pc = 1 - c if k & 1 else c
                copies.append(pltpu.make_async_remote_copy(
                    src_ref=refs[a].at[4 * px + 2 * py + pc], dst_ref=refs[n + a].at[me],
                    send_sem=send_sems.at[7 * a + k - 1], recv_sem=recv_sems.at[7 * a + k - 1],
                    device_id=(px, py, pc), device_id_type=MESH))
        return copies
    return make


def _gather_copies(refs, send_sems, recv_sems):
    x, y, c = _coords()
    me = 4 * x + 2 * y + c
    targets = [(x, y, 1 - c), (1 - x, y, c), (x, 1 - y, c), (1 - x, 1 - y, c)]
    return [pltpu.make_async_remote_copy(
        src_ref=r.at[me], dst_ref=r.at[me], send_sem=send_sems.at[4 * a + k], recv_sem=recv_sems.at[4 * a + k],
        device_id=to, device_id_type=MESH) for a, r in enumerate(refs) for k, to in enumerate(targets)]


def _forward_copies(refs, send_sems, recv_sems):
    x, y, c = _coords()
    chips = [(1 - x, y), (x, 1 - y), (1 - x, 1 - y)]
    return [pltpu.make_async_remote_copy(
        src_ref=r.at[4 * px + 2 * py + c], dst_ref=r.at[4 * px + 2 * py + c], send_sem=send_sems.at[3 * a + j],
        recv_sem=recv_sems.at[3 * a + j], device_id=(x, y, 1 - c), device_id_type=MESH)
        for a, r in enumerate(refs) for j, (px, py) in enumerate(chips)]


def _pair_copies(refs, send_sems, recv_sems):
    x, y, c = _coords()
    parts, land = refs
    return [pltpu.make_async_remote_copy(
        src_ref=parts.at[2 * j + 1 - c], dst_ref=land.at[j], send_sem=send_sems.at[j], recv_sem=recv_sems.at[j],
        device_id=(x, y, 1 - c), device_id_type=MESH) for j in range(NDEV // 2)]


def _chip_copies(refs, send_sems, recv_sems):
    x, y, c = _coords()
    summed, land = refs
    copies = []
    for k in range(1, NDEV // 2):
        px = 1 - x if (k >> 1) & 1 else x
        py = 1 - y if k & 1 else y
        copies.append(pltpu.make_async_remote_copy(
            src_ref=summed.at[2 * px + py], dst_ref=land.at[2 * x + y], send_sem=send_sems.at[k - 1],
            recv_sem=recv_sems.at[k - 1], device_id=(px, py, c), device_id_type=MESH))
    return copies


def pair_sum(parts, land, me, name):
    _, r, c = parts.shape
    tr = _row_tile(r, 800)

    def body(me_ref, a_ref, b_ref, o_ref):
        o_ref[...] = (a_ref[...].astype(f32) + b_ref[...].astype(f32)).astype(bf16)

    blk = pl.BlockSpec((1, tr, c), lambda j, i, m: (j, i, 0))
    return pl.pallas_call(
        body, name=name, out_shape=jax.ShapeDtypeStruct((NDEV // 2, r, c), bf16),
        grid_spec=pltpu.PrefetchScalarGridSpec(
            num_scalar_prefetch=1, grid=(NDEV // 2, r // tr),
            in_specs=[pl.BlockSpec((1, tr, c), lambda j, i, m: (2 * j + m[0] % 2, i, 0)), blk], out_specs=blk),
        compiler_params=_cp(("parallel", "parallel")))(me, parts, land)


def _row_tile(r, cap=256):
    return r if r <= cap else max(t for t in (800, 512, 400, 256, 160, 128) if t <= cap and r % t == 0)


def stage_shards(wt, wsq, conv, l, me, name, after=None):
    outs = []
    extra = [] if after is None else [after]
    for a, dt in ((wt, bf16), (wsq, bf16), (conv, f32)):
        _, r, c = a.shape
        tr = _row_tile(r, 800)

        def body(me_ref, a_ref, *rest):
            rest[-1][...] = a_ref[...].astype(rest[-1].dtype)

        outs.append(pl.pallas_call(
            body, name=f"{name}_{len(outs)}", out_shape=jax.ShapeDtypeStruct((NDEV, r, c), dt),
            grid_spec=pltpu.PrefetchScalarGridSpec(
                num_scalar_prefetch=1, grid=(r // tr,),
                in_specs=[pl.BlockSpec((1, tr, c), lambda i, m: (l, i, 0))] + [pl.BlockSpec(memory_space=pl.ANY)] * len(extra),
                out_specs=pl.BlockSpec((1, tr, c), lambda i, m: (m[0], i, 0))),
            compiler_params=_cp(("arbitrary",)))(me, a, *extra))
    return outs


def proj_fwd(xin, g_pre, wt, name):
    tm, tn = min(S, 1024), 2560

    def body(x_ref, g_ref, w_ref, proj_ref, h_ref, hs):
        @pl.when(pl.program_id(1) == 0)
        def _():
            x = x_ref[...]
            r = lax.rsqrt(jnp.mean(x * x, axis=-1, keepdims=True) + RMS_EPS)
            h = (x * r * g_ref[...]).astype(bf16)
            hs[...] = h
            h_ref[...] = h
        proj_ref[...] = lax.dot_general(hs[...], w_ref[...], (((1,), (1,)), ((), ())), preferred_element_type=f32)

    return pl.pallas_call(
        body, name=name, grid=(S // tm, P // tn),
        out_shape=[jax.ShapeDtypeStruct((S, P), f32), jax.ShapeDtypeStruct((S, D), bf16)],
        in_specs=[pl.BlockSpec((tm, D), lambda i, j: (i, 0)), _full((1, D)), pl.BlockSpec((tn, D), lambda i, j: (j, 0))],
        out_specs=[pl.BlockSpec((tm, tn), lambda i, j: (i, j)), pl.BlockSpec((tm, D), lambda i, j: (i, 0))],
        scratch_shapes=[pltpu.VMEM((tm, D), bf16)],
        compiler_params=_cp(("parallel", "arbitrary")))(xin, g_pre, wt)


RC = 256


def _fill_pad(pad, halo, val_fn):
    pad[0:halo, :] = jnp.zeros((halo, CT), f32)
    pad[S + halo:S + 2 * halo, :] = jnp.zeros((halo, CT), f32)

    def step(i, carry):
        rows = pl.ds(pl.multiple_of(i * RC, RC), RC)
        pad[pl.ds(pl.multiple_of(i * RC, RC) + halo, RC), :] = val_fn(rows)
        return carry
    lax.fori_loop(0, S // RC, step, 0)


def brancha_fwd(proj, convw, name):
    def body(ab, ac, ax, ag, w_ref, o_ref, pad):
        _fill_pad(pad, 8, lambda rows: ac[rows, :] * ax[rows, :])
        w = [w_ref[0, k:k + 1, :] for k in range(CA_W)]

        def step(i, carry):
            base = pl.multiple_of(i * RC, RC)
            rows = pl.ds(base, RC)
            t = sum(w[k] * pad[pl.ds(base + 7 + k, RC), :] for k in range(CA_W))
            o_ref[rows, :] = (ab[rows, :] * t * _silu(ag[rows, :])).astype(bf16)
            return carry
        lax.fori_loop(0, S // RC, step, 0)

    return pl.pallas_call(
        body, name=name, grid=(D // CT,), out_shape=jax.ShapeDtypeStruct((S, D), bf16),
        in_specs=[_chan_spec("a_b"), _chan_spec("a_c"), _chan_spec("a_x"), _chan_spec("a_g"),
                  pl.BlockSpec((1, 40, CT), lambda j: (j, 0, 0))],
        out_specs=pl.BlockSpec((S, CT), lambda j: (0, j)),
        scratch_shapes=[pltpu.VMEM((S + 16, CT), f32)],
        compiler_params=_cp(("parallel",)))(proj, proj, proj, proj, convw)


def branchc1_fwd(proj, convw, cbias, name):
    def body(cu, cv, w_ref, b_ref, o_ref, pad):
        _fill_pad(pad, 16, lambda rows: cu[rows, :] * _sig(cv[rows, :]))

        def step(i, carry):
            base = pl.multiple_of(i * RC, RC)
            acc = jnp.zeros((RC, CT), f32) + b_ref[...]
            for k in range(CC_W):
                acc = acc + w_ref[0, 8 + k:9 + k, :] * pad[pl.ds(base + k + 1, RC), :]
            o_ref[pl.ds(base, RC), :] = acc
            return carry
        lax.fori_loop(0, S // RC, step, 0)

    return pl.pallas_call(
        body, name=name, grid=(D // CT,), out_shape=jax.ShapeDtypeStruct((S, D), f32),
        in_specs=[_chan_spec("c_u"), _chan_spec("c_v"), pl.BlockSpec((1, 40, CT), lambda j: (j, 0, 0)),
                  pl.BlockSpec((1, CT), lambda j: (0, j))],
        out_specs=pl.BlockSpec((S, CT), lambda j: (0, j)),
        scratch_shapes=[pltpu.VMEM((S + 32, CT), f32)],
        compiler_params=_cp(("parallel",)))(proj, proj, convw, cbias)


def _swap32(x):
    lane = lax.broadcasted_iota(jnp.int32, x.shape, 1)
    return jnp.where((lane // 32) % 2 == 1, pltpu.roll(x, 32, 1), pltpu.roll(x, HD - 32, 1))


def _rope(y, cos, sin):
    return y * cos + _swap32(y) * sin


def qkv_fwd(proj, qn, kn, cos, sin, name):
    tm = min(S, 512)
    nq, nk, nv = len(_row_specs("q", tm)), len(_row_specs("k", tm)), len(_row_specs("v", tm))

    def body(*refs):
        q = _cat(refs[:nq])
        k = _cat(refs[nq:nq + nk])
        v = _cat(refs[nq + nk:nq + nk + nv])
        qn_ref, kn_ref, cos_ref, sin_ref, qh_ref, kh_ref, vh_ref = refs[nq + nk + nv:]
        cos, sin = cos_ref[...], sin_ref[...]

        def heads(xx, gn, out_ref, n):
            for h in range(n):
                xh = xx[:, h * HD:(h + 1) * HD]
                r = lax.rsqrt(jnp.mean(xh * xh, axis=-1, keepdims=True) + RMS_EPS)
                out_ref[:, h * HD:(h + 1) * HD] = _rope(xh * r * gn, cos, sin).astype(bf16)
        heads(q, qn_ref[...], qh_ref, NQ)
        heads(k, kn_ref[...], kh_ref, NKV)
        vh_ref[...] = v.astype(bf16)

    row = lambda w: pl.BlockSpec((tm, w), lambda i: (i, 0))
    return pl.pallas_call(
        body, name=name, grid=(S // tm,),
        out_shape=[jax.ShapeDtypeStruct((S, D), bf16), jax.ShapeDtypeStruct((S, WKV), bf16), jax.ShapeDtypeStruct((S, WKV), bf16)],
        in_specs=_row_specs("q", tm) + _row_specs("k", tm) + _row_specs("v", tm) + [_full((1, HD)), _full((1, HD)), row(HD), row(HD)],
        out_specs=[row(D), row(WKV), row(WKV)],
        compiler_params=_cp(("parallel",)))(*([proj] * (nq + nk + nv)), qn, kn, cos, sin)


def _softmax_rows(q, k):
    s = lax.dot_general(q, k, (((1,), (1,)), ((), ())), preferred_element_type=f32)
    p = jnp.exp((s - jnp.max(s, axis=-1, keepdims=True)) * (HD ** -0.5))
    return p, 1.0 / jnp.sum(p, axis=-1, keepdims=True)


GW = G * HD


def attn_fwd(qh, kh, vh, proj, name):
    tq = min(S, 512)
    bg_blk = _OFF["b_g"][0] // GW

    def body(q_ref, k_ref, v_ref, bg_ref, o_ref, y_ref):
        k, v = k_ref[...], v_ref[...]
        for g in range(G):
            cols = slice(g * HD, (g + 1) * HD)
            p, rl = _softmax_rows(q_ref[:, cols], k)
            o = jnp.dot(p.astype(bf16), v, preferred_element_type=f32) * rl
            o_ref[:, cols] = o
            y_ref[:, cols] = (o * _silu(bg_ref[:, cols])).astype(bf16)

    grp = pl.BlockSpec((tq, GW), lambda kv, i: (i, kv))
    kvs = pl.BlockSpec((S, HD), lambda kv, i: (0, kv))
    return pl.pallas_call(
        body, name=name, grid=(NKV, S // tq),
        out_shape=[jax.ShapeDtypeStruct((S, D), f32), jax.ShapeDtypeStruct((S, D), bf16)],
        in_specs=[grp, kvs, kvs, pl.BlockSpec((tq, GW), lambda kv, i: (i, bg_blk + kv))],
        out_specs=[grp, grp],
        compiler_params=_cp(("parallel", "parallel")))(qh, kh, vh, proj)


def _ln_parts(u1):
    mu = jnp.mean(u1, axis=-1, keepdims=True)
    xc = u1 - mu
    rstd = lax.rsqrt(jnp.mean(xc * xc, axis=-1, keepdims=True) + LN_EPS)
    return xc * rstd, rstd


def _after(after):
    return ([], []) if after is None else ([after], [pl.BlockSpec(memory_space=pl.ANY)])


def branchc2_fwd(u1, proj, lng, lnb, name, after=None):
    tm = min(S, 512)
    ncg = len(_row_specs("c_g", tm))
    extra, extra_specs = _after(after)

    def body(*refs):
        u_ref = refs[0]
        cg = _cat(refs[1:1 + ncg])
        g_ref, b_ref = refs[1 + ncg:3 + ncg]
        xh, _ = _ln_parts(u_ref[...])
        refs[-1][...] = (_silu(xh * g_ref[...] + b_ref[...]) * _silu(cg)).astype(bf16)

    row = pl.BlockSpec((tm, D), lambda i: (i, 0))
    return pl.pallas_call(
        body, name=name, grid=(S // tm,), out_shape=jax.ShapeDtypeStruct((S, D), bf16),
        in_specs=[row] + _row_specs("c_g", tm) + [_full((1, D)), _full((1, D))] + extra_specs, out_specs=row,
        compiler_params=_cp(("parallel",)))(u1, *([proj] * ncg), lng, lnb, *extra)


def _wmat(w_ref, kind):
    return w_ref[:, kind].reshape(D, D)


def merge_fwd(xin, yah, ybh, ych, proj, wsq, g_post, name):
    tm = min(S, 512)
    nm = len(_row_specs("m_a", tm))

    def body(*refs):
        x_ref, a_ref, b_ref, c_ref = refs[:4]
        ms = [_cat(refs[4 + t * nm:4 + (t + 1) * nm]) for t in range(3)]
        w_ref, g_ref, ya_ref, yb_ref, yc_ref, y_ref, z_ref, o_ref = refs[4 + 3 * nm:]
        y = jnp.zeros((tm, D), f32)
        for t, (h_ref, out_ref) in enumerate(((a_ref, ya_ref), (b_ref, yb_ref), (c_ref, yc_ref))):
            yt = jnp.dot(h_ref[...], _wmat(w_ref, t), preferred_element_type=f32)
            out_ref[...] = yt
            y = y + _sig(ms[t]) * yt
        yb16 = y.astype(bf16)
        y_ref[...] = yb16
        z = jnp.dot(yb16, _wmat(w_ref, 3), preferred_element_type=f32)
        z_ref[...] = z
        r = lax.rsqrt(jnp.mean(z * z, axis=-1, keepdims=True) + RMS_EPS)
        o_ref[...] = x_ref[...] + z * r * g_ref[...]

    row = pl.BlockSpec((tm, D), lambda i: (i, 0))
    sd = lambda dt: jax.ShapeDtypeStruct((S, D), dt)
    return pl.pallas_call(
        body, name=name, grid=(S // tm,),
        out_shape=[sd(f32), sd(f32), sd(f32), sd(bf16), sd(f32), sd(f32)],
        in_specs=[row] * 4 + _row_specs("m_a", tm) + _row_specs("m_b", tm) + _row_specs("m_c", tm)
        + [pl.BlockSpec((NDEV, 4, D // NDEV, D), lambda *_: (0, 0, 0, 0), pipeline_mode=pl.Buffered(1)), _full((1, D))],
        out_specs=[row] * 6,
        compiler_params=_cp(("parallel",)))(xin, yah, ybh, ych, *([proj] * (3 * nm)), wsq, g_post)


def loss_fwd(y, target, name):
    tm = min(S, 256)

    def body(y_ref, t_ref, dy_ref, l_ref):
        e = y_ref[...] - t_ref[...]
        dy_ref[...] = e / D

        @pl.when(pl.program_id(0) == 0)
        def _():
            l_ref[...] = jnp.zeros((1, 128), f32)
        l_ref[...] += (0.5 / D) * jnp.sum(e * e)

    row = pl.BlockSpec((tm, D), lambda i: (i, 0))
    return pl.pallas_call(
        body, name=name, grid=(S // tm,),
        out_shape=[jax.ShapeDtypeStruct((S, D), f32), jax.ShapeDtypeStruct((1, 128), f32)],
        in_specs=[row, row], out_specs=[row, _full((1, 128))],
        compiler_params=_cp(("arbitrary",)))(y, target)


def _acc(ref, val):
    @pl.when(pl.program_id(0) == 0)
    def _():
        ref[...] = jnp.zeros(ref.shape, f32)
    ref[...] += val


def _emit_copies(stash, dst, sems, windows):
    return [pltpu.make_async_copy(stash.at[p], dst.at[w], sems.at[p]) for p, w in enumerate(windows)]


def _emit_drain_previous(copies, step):
    @pl.when(step > 0)
    def _():
        for cp in copies:
            cp.wait()


def _emit_start(copies, step, nsteps):
    for cp in copies:
        cp.start()

    @pl.when(step == nsteps - 1)
    def _():
        for cp in copies:
            cp.wait()


def merge_bwd(dout, z, ya, yb, yc, proj, wsq, g_post, name):
    tm = min(S, 256)
    nm = len(_row_specs("m_a", tm))
    nsteps = S // tm

    def body(*refs):
        do_ref, z_ref, ya_ref, yb_ref, yc_ref = refs[:5]
        ms = [_cat(refs[5 + t * nm:5 + (t + 1) * nm]) for t in range(3)]
        w_ref, g_ref = refs[5 + 3 * nm:7 + 3 * nm]
        dh_refs = refs[7 + 3 * nm:10 + 3 * nm]
        dzb_ref = refs[10 + 3 * nm]
        dyb_refs = refs[11 + 3 * nm:14 + 3 * nm]
        dg_ref = refs[14 + 3 * nm]
        dproj_ref, stash, sems = refs[15 + 3 * nm:]
        i = pl.program_id(0)
        rows = pl.ds(pl.multiple_of(i * tm, tm), tm)
        copies = _emit_copies(stash, dproj_ref, sems, [(rows, pl.ds(_OFF[n][0], D)) for n in ("m_a", "m_b", "m_c")])
        nt = (((1,), (1,)), ((), ()))
        z, dout = z_ref[...], do_ref[...]
        r = lax.rsqrt(jnp.mean(z * z, axis=-1, keepdims=True) + RMS_EPS)
        zh = z * r
        _acc(dg_ref, jnp.sum(dout * zh, axis=0, keepdims=True))
        dzh = dout * g_ref[...]
        dz = (r * (dzh - zh * jnp.mean(dzh * zh, axis=-1, keepdims=True))).astype(bf16)
        dzb_ref[...] = dz
        dy = lax.dot_general(dz, _wmat(w_ref, 3), nt, preferred_element_type=f32)
        dms = []
        for t, yt_ref in enumerate((ya_ref, yb_ref, yc_ref)):
            sg = _sig(ms[t])
            dyt = (dy * sg).astype(bf16)
            dyb_refs[t][...] = dyt
            dms.append((dy * yt_ref[...] * sg * (1.0 - sg)).astype(bf16))
            dh_refs[t][...] = lax.dot_general(dyt, _wmat(w_ref, t), nt, preferred_element_type=f32)
        _emit_drain_previous(copies, i)
        for t in range(3):
            stash[t] = dms[t]
        _emit_start(copies, i, nsteps)

    row = pl.BlockSpec((tm, D), lambda i: (i, 0))
    sd = lambda dt: jax.ShapeDtypeStruct((S, D), dt)
    return pl.pallas_call(
        body, name=name, grid=(nsteps,),
        out_shape=[sd(f32)] * 3 + [sd(bf16)] * 4 + [jax.ShapeDtypeStruct((1, D), f32), jax.ShapeDtypeStruct((S, P), bf16)],
        in_specs=[row] * 5 + _row_specs("m_a", tm) + _row_specs("m_b", tm) + _row_specs("m_c", tm)
        + [pl.BlockSpec((NDEV, 4, D // NDEV, D), lambda *_: (0, 0, 0, 0), pipeline_mode=pl.Buffered(1)), _full((1, D))],
        out_specs=[row] * 7 + [_full((1, D)), pl.BlockSpec(memory_space=pl.ANY)],
        scratch_shapes=[pltpu.VMEM((3, tm, D), bf16), pltpu.SemaphoreType.DMA((3,))],
        compiler_params=_cp(("arbitrary",)))(dout, z, ya, yb, yc, *([proj] * (3 * nm)), wsq, g_post)


def tn_matmul(a, b, name):
    m, n = a.shape[1], b.shape[1]
    tmm = min(m, 512)

    def body(a_ref, b_ref, o_ref):
        o_ref[...] = lax.dot_general(a_ref[...], b_ref[...], (((0,), (0,)), ((), ())), preferred_element_type=f32).astype(bf16)

    return pl.pallas_call(
        body, name=name, grid=(m // tmm,), out_shape=jax.ShapeDtypeStruct((m, n), bf16),
        in_specs=[pl.BlockSpec((S, tmm), lambda i: (0, i)), _full((S, n))],
        out_specs=pl.BlockSpec((tmm, n), lambda i: (i, 0)),
        compiler_params=_cp(("parallel",)))(a, b)


def dwin_parts(h, dproj, name):
    tn = 1280

    def body(d_ref, h_ref, o_ref):
        o_ref[...] = lax.dot_general(d_ref[...], h_ref[...], (((0,), (0,)), ((), ())), preferred_element_type=f32).astype(bf16)

    return pl.pallas_call(
        body, name=name, grid=(P // tn,), out_shape=jax.ShapeDtypeStruct((P, D), bf16),
        in_specs=[pl.BlockSpec((S, tn), lambda j: (0, j)), _full((S, D))],
        out_specs=pl.BlockSpec((tn, D), lambda j: (j, 0)),
        compiler_params=_cp(("parallel",)))(dproj, h)


def _chan_windows(names, j):
    return [(slice(None), pl.ds(pl.multiple_of(_OFF[n][0] + j * CT, CT), CT)) for n in names]


def brancha_bwd(dyah, proj, convw, dproj, name, after=None):
    nsteps = D // CT
    extra, extra_specs = _after(after)

    def body(d_ref, ab, ac, ax, ag, w_ref, *rest):
        dw_ref, dproj_ref, padp, padt, accw, stash, sems = rest[-7:]
        j = pl.program_id(0)
        copies = _emit_copies(stash, dproj_ref, sems, _chan_windows(("a_b", "a_c", "a_x", "a_g"), j))
        _fill_pad(padp, 8, lambda rows: ac[rows, :] * ax[rows, :])
        _fill_pad(padt, 8, lambda rows: d_ref[rows, :] * ab[rows, :] * _silu(ag[rows, :]))
        accw[...] = jnp.zeros(accw.shape, f32)
        w = [w_ref[0, k:k + 1, :] for k in range(CA_W)]
        _emit_drain_previous(copies, j)

        def step(i, carry):
            base = pl.multiple_of(i * RC, RC)
            rows = pl.ds(base, RC)
            ps = [padp[pl.ds(base + 7 + k, RC), :] for k in range(CA_W)]
            t = sum(w[k] * ps[k] for k in range(CA_W))
            dp = sum(w[k] * padt[pl.ds(base + 9 - k, RC), :] for k in range(CA_W))
            d, a_b, a_g = d_ref[rows, :], ab[rows, :], ag[rows, :]
            stash[0, rows, :] = (d * t * _silu(a_g)).astype(bf16)
            stash[1, rows, :] = (dp * ax[rows, :]).astype(bf16)
            stash[2, rows, :] = (dp * ac[rows, :]).astype(bf16)
            stash[3, rows, :] = (d * a_b * t * _dsilu(a_g)).astype(bf16)
            dt = padt[pl.ds(base + 8, RC), :]
            for k in range(CA_W):
                accw[8 * k:8 * k + 8, :] += jnp.sum((dt * ps[k]).reshape(RC // 8, 8, CT), axis=0)
            return carry
        lax.fori_loop(0, S // RC, step, 0)
        _emit_start(copies, j, nsteps)
        dw_ref[0] = jnp.zeros((8, CT), f32)
        for k in range(CA_W):
            dw_ref[0, k:k + 1, :] = jnp.sum(accw[8 * k:8 * k + 8, :], axis=0, keepdims=True)

    tile = pl.BlockSpec((S, CT), lambda j: (0, j))
    anyspec = pl.BlockSpec(memory_space=pl.ANY)
    return pl.pallas_call(
        body, name=name, grid=(nsteps,),
        out_shape=[jax.ShapeDtypeStruct((NDEV, 8, CT), f32), jax.ShapeDtypeStruct((S, P), bf16)],
        in_specs=[tile, _chan_spec("a_b"), _chan_spec("a_c"), _chan_spec("a_x"), _chan_spec("a_g"),
                  pl.BlockSpec((1, 40, CT), lambda j: (j, 0, 0)), anyspec] + extra_specs,
        out_specs=[pl.BlockSpec((1, 8, CT), lambda j: (j, 0, 0)), anyspec],
        input_output_aliases={6: 1},
        scratch_shapes=[pltpu.VMEM((S + 16, CT), f32), pltpu.VMEM((S + 16, CT), f32), pltpu.VMEM((8 * CA_W, CT), f32),
                        pltpu.VMEM((4, S, CT), bf16), pltpu.SemaphoreType.DMA((4,))],
        compiler_params=_cp(("arbitrary",)))(dyah, proj, proj, proj, proj, convw, dproj, *extra)


def branchc2_bwd(dych, u1, proj, lng, lnb, dproj, name):
    tm = min(S, 512)
    ncg = len(_row_specs("c_g", tm))
    nsteps = S // tm

    def body(*refs):
        d_ref, u_ref = refs[:2]
        cg = _cat(refs[2:2 + ncg])
        g_ref, b_ref, _, du_ref, dlg_ref, dlb_ref, dcb_ref, dproj_ref, stash, sems = refs[2 + ncg:]
        i = pl.program_id(0)
        copies = _emit_copies(stash, dproj_ref, sems, [(pl.ds(pl.multiple_of(i * tm, tm), tm), pl.ds(_OFF["c_g"][0], D))])
        d = d_ref[...]
        xh, rstd = _ln_parts(u_ref[...])
        ln = xh * g_ref[...] + b_ref[...]
        _emit_drain_previous(copies, i)
        stash[0] = (d * _silu(ln) * _dsilu(cg)).astype(bf16)
        _emit_start(copies, i, nsteps)
        dln = d * _silu(cg) * _dsilu(ln)
        _acc(dlg_ref, jnp.sum(dln * xh, axis=0, keepdims=True))
        _acc(dlb_ref, jnp.sum(dln, axis=0, keepdims=True))
        dxh = dln * g_ref[...]
        du = rstd * (dxh - jnp.mean(dxh, axis=-1, keepdims=True) - xh * jnp.mean(dxh * xh, axis=-1, keepdims=True))
        du_ref[...] = du
        _acc(dcb_ref, jnp.sum(du, axis=0, keepdims=True))

    row = pl.BlockSpec((tm, D), lambda i: (i, 0))
    vec = jax.ShapeDtypeStruct((1, D), f32)
    anyspec = pl.BlockSpec(memory_space=pl.ANY)
    return pl.pallas_call(
        body, name=name, grid=(nsteps,),
        out_shape=[jax.ShapeDtypeStruct((S, D), f32), vec, vec, vec, jax.ShapeDtypeStruct((S, P), bf16)],
        in_specs=[row, row] + _row_specs("c_g", tm) + [_full((1, D)), _full((1, D)), anyspec],
        out_specs=[row, _full((1, D)), _full((1, D)), _full((1, D)), anyspec],
        input_output_aliases={4 + ncg: 4},
        scratch_shapes=[pltpu.VMEM((1, tm, D), bf16), pltpu.SemaphoreType.DMA((1,))],
        compiler_params=_cp(("arbitrary",)))(dych, u1, *([proj] * ncg), lng, lnb, dproj)


def branchc1_bwd(du1, proj, convw, dproj, name):
    nsteps = D // CT

    def body(d_ref, cu, cv, w_ref, _, dw_ref, dproj_ref, padu, padd, accw, stash, sems):
        j = pl.program_id(0)
        copies = _emit_copies(stash, dproj_ref, sems, _chan_windows(("c_u", "c_v"), j))
        _fill_pad(padu, 16, lambda rows: cu[rows, :] * _sig(cv[rows, :]))
        _fill_pad(padd, 16, lambda rows: d_ref[rows, :])
        accw[...] = jnp.zeros(accw.shape, f32)
        _emit_drain_previous(copies, j)

        rc = min(S, 128)

        def step(i, carry):
            base = pl.multiple_of(i * rc, rc)
            rows = pl.ds(base, rc)
            d = d_ref[rows, :]
            du0 = jnp.zeros((rc, CT), f32)
            for k in range(CC_W):
                du0 = du0 + w_ref[0, 8 + k:9 + k, :] * padd[pl.ds(base + 31 - k, rc), :]
                accw[8 * k:8 * k + 8, :] += jnp.sum((d * padu[pl.ds(base + k + 1, rc), :]).reshape(rc // 8, 8, CT), axis=0)
            sg = _sig(cv[rows, :])
            stash[0, rows, :] = (du0 * sg).astype(bf16)
            stash[1, rows, :] = (du0 * cu[rows, :] * sg * (1.0 - sg)).astype(bf16)
            return carry
        lax.fori_loop(0, S // rc, step, 0)
        _emit_start(copies, j, nsteps)
        dw_ref[0] = jnp.zeros((32, CT), f32)
        for k in range(CC_W):
            dw_ref[0, k:k + 1, :] = jnp.sum(accw[8 * k:8 * k + 8, :], axis=0, keepdims=True)

    tile = pl.BlockSpec((S, CT), lambda j: (0, j))
    anyspec = pl.BlockSpec(memory_space=pl.ANY)
    return pl.pallas_call(
        body, name=name, grid=(nsteps,),
        out_shape=[jax.ShapeDtypeStruct((NDEV, 32, CT), f32), jax.ShapeDtypeStruct((S, P), bf16)],
        in_specs=[tile, _chan_spec("c_u"), _chan_spec("c_v"), pl.BlockSpec((1, 40, CT), lambda j: (j, 0, 0)), anyspec],
        out_specs=[pl.BlockSpec((1, 32, CT), lambda j: (j, 0, 0)), anyspec],
        input_output_aliases={4: 1},
        scratch_shapes=[pltpu.VMEM((S + 32, CT), f32), pltpu.VMEM((S + 32, CT), f32), pltpu.VMEM((8 * 32, CT), f32),
                        pltpu.VMEM((2, S, CT), bf16), pltpu.SemaphoreType.DMA((2,))],
        compiler_params=_cp(("arbitrary",)))(du1, proj, proj, convw, dproj)


def attn_bwd(dybh, o, qh, kh, vh, proj, dproj, name):
    tq = min(S, 512)
    bg_blk = _OFF["b_g"][0] // GW

    def body(d_ref, o_ref, q_ref, k_ref, v_ref, bg_ref, _, dq_ref, dk_ref, dv_ref, dbg_ref):
        @pl.when(pl.program_id(1) == 0)
        def _():
            dk_ref[...] = jnp.zeros(dk_ref.shape, f32)
            dv_ref[...] = jnp.zeros(dv_ref.shape, f32)
        k, v = k_ref[...], v_ref[...]
        tn = (((0,), (0,)), ((), ()))
        dk_acc = jnp.zeros((S, HD), f32)
        dv_acc = jnp.zeros((S, HD), f32)
        for g in range(G):
            cols = slice(g * HD, (g + 1) * HD)
            d, bg, q, o = d_ref[:, cols], bg_ref[:, cols], q_ref[:, cols], o_ref[:, cols]
            dbg_ref[:, cols] = (d * o * _dsilu(bg)).astype(bf16)
            do = d * _silu(bg)
            p, rl = _softmax_rows(q, k)
            dv_acc = dv_acc + lax.dot_general(p.astype(bf16), (do * rl).astype(bf16), tn, preferred_element_type=f32)
            dp = lax.dot_general(do.astype(bf16), v, (((1,), (1,)), ((), ())), preferred_element_type=f32)
            delta = jnp.sum(do * o, axis=-1, keepdims=True)
            ds = (p * (dp - delta)).astype(bf16)
            rs_ = rl * (HD ** -0.5)
            dq_ref[:, cols] = jnp.dot(ds, k, preferred_element_type=f32) * rs_
            dk_acc = dk_acc + lax.dot_general(ds, (q.astype(f32) * rs_).astype(bf16), tn, preferred_element_type=f32)
        dk_ref[...] += dk_acc
        dv_ref[...] += dv_acc

    grp = pl.BlockSpec((tq, GW), lambda kv, i: (i, kv))
    kvs = pl.BlockSpec((S, HD), lambda kv, i: (0, kv))
    return pl.pallas_call(
        body, name=name, grid=(NKV, S // tq),
        out_shape=[jax.ShapeDtypeStruct((S, D), f32), jax.ShapeDtypeStruct((S, WKV), f32),
                   jax.ShapeDtypeStruct((S, WKV), f32), jax.ShapeDtypeStruct((S, P), bf16)],
        in_specs=[grp, grp, grp, kvs, kvs, pl.BlockSpec((tq, GW), lambda kv, i: (i, bg_blk + kv)),
                  pl.BlockSpec(memory_space=pl.ANY)],
        out_specs=[grp, kvs, kvs, pl.BlockSpec((tq, GW), lambda kv, i: (i, bg_blk + kv))],
        input_output_aliases={6: 3},
        compiler_params=_cp(("parallel", "arbitrary")))(dybh, o, qh, kh, vh, proj, dproj)


def qkv_bwd(dqh, dkh, dvh, proj, qn, kn, cos, sin, dproj, name):
    tm = min(S, 512)
    nq, nk = len(_row_specs("q", tm)), len(_row_specs("k", tm))
    nsteps = S // tm
    wq = D + 2 * WKV

    def body(*refs):
        dqh_ref, dkh_ref, dvh_ref = refs[:3]
        q = _cat(refs[3:3 + nq])
        k = _cat(refs[3 + nq:3 + nq + nk])
        qn_ref, kn_ref, cos_ref, sin_ref, _, dqn_ref, dkn_ref, dproj_ref, stash, sems = refs[3 + nq + nk:]
        i = pl.program_id(0)
        copies = _emit_copies(stash, dproj_ref, sems, [(pl.ds(pl.multiple_of(i * tm, tm), tm), pl.ds(_OFF["q"][0], wq))])
        cos, sin = cos_ref[...], sin_ref[...]
        _emit_drain_previous(copies, i)

        def heads(xx, dd, gn, col0, dgn_ref, n):
            dg = jnp.zeros((1, HD), f32)
            for h in range(n):
                xh = xx[:, h * HD:(h + 1) * HD]
                dh = dd[:, h * HD:(h + 1) * HD]
                r = lax.rsqrt(jnp.mean(xh * xh, axis=-1, keepdims=True) + RMS_EPS)
                xn = xh * r
                dy = dh * cos + _swap32(dh * sin)
                dg = dg + jnp.sum(dy * xn, axis=0, keepdims=True)
                dxn = dy * gn
                stash[0, :, col0 + h * HD:col0 + (h + 1) * HD] = (
                    r * (dxn - xn * jnp.mean(dxn * xn, axis=-1, keepdims=True))).astype(bf16)
            _acc(dgn_ref, dg)
        heads(q, dqh_ref[...], qn_ref[...], 0, dqn_ref, NQ)
        heads(k, dkh_ref[...], kn_ref[...], D, dkn_ref, NKV)
        stash[0, :, D + WKV:wq] = dvh_ref[...].astype(bf16)
        _emit_start(copies, i, nsteps)

    row = lambda w: pl.BlockSpec((tm, w), lambda i: (i, 0))
    vec = jax.ShapeDtypeStruct((1, HD), f32)
    anyspec = pl.BlockSpec(memory_space=pl.ANY)
    return pl.pallas_call(
        body, name=name, grid=(nsteps,),
        out_shape=[vec, vec, jax.ShapeDtypeStruct((S, P), bf16)],
        in_specs=[row(D), row(WKV), row(WKV)] + _row_specs("q", tm) + _row_specs("k", tm)
        + [_full((1, HD)), _full((1, HD)), row(HD), row(HD), anyspec],
        out_specs=[_full((1, HD)), _full((1, HD)), anyspec],
        input_output_aliases={7 + nq + nk: 2},
        scratch_shapes=[pltpu.VMEM((1, tm, wq), bf16), pltpu.SemaphoreType.DMA((1,))],
        compiler_params=_cp(("arbitrary",)))(dqh, dkh, dvh, *([proj] * (nq + nk)), qn, kn, cos, sin, dproj)


def dh_bwd(dproj, wfull, xin, dout, g_pre, name, after=None):
    tm, tk = min(S, 1024), 2560
    nk = P // tk
    extra, extra_specs = _after(after)

    def body(d_ref, w_ref, x_ref, do_ref, g_ref, *rest):
        dx_ref, dg_ref, acc = rest[-3:]
        kk = pl.program_id(1)

        @pl.when(kk == 0)
        def _():
            acc[...] = jnp.zeros(acc.shape, f32)
        acc[...] += jnp.dot(d_ref[...], w_ref[...], preferred_element_type=f32)

        @pl.when((kk == 0) & (pl.program_id(0) == 0))
        def _():
            dg_ref[...] = jnp.zeros(dg_ref.shape, f32)

        @pl.when(kk == nk - 1)
        def _():
            x, dh = x_ref[...], acc[...]
            r = lax.rsqrt(jnp.mean(x * x, axis=-1, keepdims=True) + RMS_EPS)
            xn = x * r
            dg_ref[...] += jnp.sum(dh * xn, axis=0, keepdims=True)
            dxn = dh * g_ref[...]
            dx_ref[...] = do_ref[...] + r * (dxn - xn * jnp.mean(dxn * xn, axis=-1, keepdims=True))

    row = pl.BlockSpec((tm, D), lambda i, k: (i, 0))
    return pl.pallas_call(
        body, name=name, grid=(S // tm, nk),
        out_shape=[jax.ShapeDtypeStruct((S, D), f32), jax.ShapeDtypeStruct((1, D), f32)],
        in_specs=[pl.BlockSpec((tm, tk), lambda i, k: (i, k)), pl.BlockSpec((tk, D), lambda i, k: (k, 0)), row, row, _full((1, D))]
        + extra_specs,
        out_specs=[row, _full((1, D))],
        scratch_shapes=[pltpu.VMEM((tm, D), f32)],
        compiler_params=_cp(("arbitrary", "arbitrary")))(dproj, wfull, xin, dout, g_pre, *extra)


def adam_update(parts, own, me, w, m, v, l, acc, name):
    lw, r, c = w.shape
    tr = _row_tile(r, 400)
    nslots = parts.shape[0]

    def body(me_ref, p_ref, own_ref, w_ref, m_ref, v_ref, *rest):
        g_ref, d_ref, nm_ref, nv_ref = rest[-4:]
        g = None
        for s in range(nslots):
            part = jnp.where(me_ref[0] == s, own_ref[0], p_ref[s]).astype(f32)
            g = part if g is None else g + part
        nm = ADAM_B1 * m_ref[0] + (1.0 - ADAM_B1) * g
        nv = ADAM_B2 * v_ref[0] + (1.0 - ADAM_B2) * (g * g)
        m_hat = nm / (1.0 - ADAM_B1 ** ADAM_STEP)
        v_hat = nv / (1.0 - ADAM_B2 ** ADAM_STEP)
        g_ref[0] = g
        d_ref[0] = -ADAM_LR * (m_hat / (jnp.sqrt(v_hat) + ADAM_EPS) + ADAM_WD * w_ref[0])
        nm_ref[0] = nm
        nv_ref[0] = nv

    blk = pl.BlockSpec((1, tr, c), lambda i, me_ref: (l, i, 0))
    sd = jax.ShapeDtypeStruct((lw, r, c), f32)
    extra = [] if acc is None else list(acc)
    return pl.pallas_call(
        body, name=name, out_shape=[sd] * 4,
        grid_spec=pltpu.PrefetchScalarGridSpec(
            num_scalar_prefetch=1, grid=(r // tr,),
            in_specs=[pl.BlockSpec((nslots, tr, c), lambda i, me_ref: (0, i, 0)),
                      pl.BlockSpec((1, tr, c), lambda i, me_ref: (me_ref[0], i, 0)), blk, blk, blk]
            + [pl.BlockSpec(memory_space=pl.ANY)] * len(extra),
            out_specs=[blk] * 4),
        input_output_aliases={6 + t: t for t in range(len(extra))},
        compiler_params=_cp(("parallel",)))(me, parts, own, w, m, v, *extra)


def _rope_tables():
    t = jnp.arange(S)
    rows, cols = (t // GRID_W).astype(f32), (t % GRID_W).astype(f32)
    nf = HD // 4
    inv = ROPE_THETA ** (-jnp.arange(nf, dtype=f32) / nf)
    ar, ac = rows[:, None] * inv, cols[:, None] * inv
    cos = jnp.concatenate([jnp.cos(ar), jnp.cos(ar), jnp.cos(ac), jnp.cos(ac)], axis=1)
    sin = jnp.concatenate([-jnp.sin(ar), jnp.sin(ar), -jnp.sin(ac), jnp.sin(ac)], axis=1)
    return cos, sin


def _pack_conv(ca, cc):
    z = lambda n: jnp.zeros((L, n, CT), f32)
    return jnp.concatenate([ca, z(5), cc, z(1)], axis=1)


def _pack_small(npre, npost, ccb, lng, lnb, qn, kn):
    wide = lambda a: jnp.pad(a, ((0, 0), (0, D - HD)))
    return jnp.stack([npre, npost, ccb, lng, lnb, wide(qn), wide(kn), jnp.zeros((L, D), f32)], axis=1).reshape(L * 8, D)


def kernel(x, norm_pre, norm_post, w_in, conv_a_w, q_norm, k_norm, conv_c_w, conv_c_b, ln_c_g, ln_c_b, w_out_a, w_out_b, w_out_c, w_o, loss_target, m_norm_pre, m_norm_post, m_w_in, m_conv_a_w, m_q_norm, m_k_norm, m_conv_c_w, m_conv_c_b, m_ln_c_g, m_ln_c_b, m_w_out_a, m_w_out_b, m_w_out_c, m_w_o, v_norm_pre, v_norm_post, v_w_in, v_conv_a_w, v_q_norm, v_k_norm, v_conv_c_w, v_conv_c_b, v_ln_c_g, v_ln_c_b, v_w_out_a, v_w_out_b, v_w_out_c, v_w_o):
    cos, sin = _rope_tables()
    rs = D // NDEV
    stack_sq = lambda a, b, c, d: jnp.stack([a, b, c, d], axis=1)
    wsq32 = stack_sq(w_out_a, w_out_b, w_out_c, w_o)
    conv_pack = _pack_conv(conv_a_w, conv_c_w)
    vec = lambda a, l: a[l][None, :]
    me = (4 * lax.axis_index("x") + 2 * lax.axis_index("y") + lax.axis_index("c")).astype(jnp.int32).reshape(1)

    def gather_start(l, after):
        return split_start(staged[l], _gather_copies, 12, f"ag_start{l}", after=after)

    def forward_start(l, after):
        s_sems, r_sems, bufs, _ = gathers[l]
        bufs = split_wait(s_sems, r_sems, bufs, _gather_copies, after, f"ag_wait{l}")
        fw = split_start(bufs, _forward_copies, 9, f"ag_fwd_start{l}")
        if l + 1 < L:
            gathers[l + 1] = gather_start(l + 1, fw[3])
            return fw, gathers[l + 1][3]
        return fw, fw[3]

    def forward_wait(fw, after, l):
        s_sems, r_sems, bufs, _ = fw
        return split_wait(s_sems, r_sems, bufs, _forward_copies, after, f"ag_fwd_wait{l}")

    wt, m_wt, v_wt = (jnp.swapaxes(a, 1, 2) for a in (w_in, m_w_in, v_w_in))
    xs, saved = x.reshape(S, D), []
    stage = lambda l, after: stage_shards(wt, wsq32.reshape(L, 4 * rs, D), conv_pack, l, me, f"stage{l}", after)
    staged = [stage(0, None)]
    gathers = [gather_start(0, None)] + [None] * (L - 1)
    staged += [stage(l, gathers[0][3]) for l in range(1, L)]
    fw, issued = forward_start(0, staged[L - 1][0] if L > 1 else xs)
    wg, wsq, convw = forward_wait(fw, issued, 0)
    for l in range(L):
        wsq = wsq.reshape(NDEV, 4, rs, D)
        wfull = wg.reshape(P, D)
        proj, h = proj_fwd(xs, vec(norm_pre, l), wfull, f"proj{l}")
        yah = brancha_fwd(proj, convw, f"bra{l}")
        u1 = branchc1_fwd(proj, convw, vec(conv_c_b, l), f"brc1_{l}")
        qh, kh, vh = qkv_fwd(proj, vec(q_norm, l), vec(k_norm, l), cos, sin, f"qkv{l}")
        o, ybh = attn_fwd(qh, kh, vh, proj, f"attn{l}")
        issued = None
        if l + 1 < L:
            fw, issued = forward_start(l + 1, o)
        ych = branchc2_fwd(u1, proj, vec(ln_c_g, l), vec(ln_c_b, l), f"brc2_{l}", after=issued)
        ya, yb, yc, y16, z, xo = merge_fwd(xs, yah, ybh, ych, proj, wsq, vec(norm_post, l), f"merge{l}")
        saved.append(dict(x=xs, wfull=wfull, wsq=wsq, convw=convw, proj=proj, h=h, yah=yah, ybh=ybh, ych=ych, u1=u1,
                          qh=qh, kh=kh, vh=vh, o=o, ya=ya, yb=yb, yc=yc, y16=y16, z=z))
        xs = xo
        if l + 1 < L:
            wg, wsq, convw = forward_wait(fw, xs, l + 1)
    dx, loss_part = loss_fwd(xs, loss_target.reshape(S, D), "loss")
    loss = lax.psum(loss_part[0, 0], ("x", "y", "c"))

    acc = dict(win=None, sq=None, conv=None)
    small_parts = [None] * L
    msq32 = stack_sq(m_w_out_a, m_w_out_b, m_w_out_c, m_w_o)
    vsq32 = stack_sq(v_w_out_a, v_w_out_b, v_w_out_c, v_w_o)
    mconv, vconv = _pack_conv(m_conv_a_w, m_conv_c_w), _pack_conv(v_conv_a_w, v_conv_c_w)

    def scatter_start(parts, name, after=None):
        bufs = parts + [lax.empty(p.shape, p.dtype) for p in parts]
        return split_start(bufs, _scatter_copies(len(parts)), 7 * len(parts), name, after=after)

    def finish(l, started, after):
        (s1, r1, b1, _), (s2, r2, b2, _) = started
        gsq_own, rsq = split_wait(s1, r1, b1, _scatter_copies(1), after, f"rs_sq_wait{l}")
        gwin_own, gconv_own, rwin, rconv = split_wait(s2, r2, b2, _scatter_copies(2), after, f"rs_win_wait{l}")
        flat = lambda a: a.reshape(a.shape[0], 4 * rs, D)
        acc["win"] = adam_update(rwin, gwin_own, me, wt, m_wt, v_wt, l, acc["win"], f"adam_win{l}")
        acc["sq"] = adam_update(flat(rsq), flat(gsq_own), me, flat(wsq32), flat(msq32), flat(vsq32), l, acc["sq"], f"adam_wsq{l}")
        acc["conv"] = adam_update(rconv, gconv_own, me, conv_pack, mconv, vconv, l, acc["conv"], f"adam_conv{l}")

    pending = [None] * L
    for l in reversed(range(L)):
        sv = saved[l]
        proj = sv["proj"]
        (dyah, dybh, dych, dzb, dyab, dybb, dycb, dgpost, dproj) = merge_bwd(
            dx, sv["z"], sv["ya"], sv["yb"], sv["yc"], proj, sv["wsq"], vec(norm_post, l), f"merge_bwd{l}")
        gsq = [tn_matmul(a, b, f"dwsq{t}_{l}") for t, (a, b) in enumerate(
            ((sv["yah"], dyab), (sv["ybh"], dybb), (sv["ych"], dycb), (sv["y16"], dzb)))]
        gsq_parts = jnp.stack([g.reshape(NDEV, rs, D) for g in gsq], axis=1)
        st1 = scatter_start([gsq_parts], f"rs_sq_start{l}", after=loss.reshape(1, 1) if l == L - 1 else None)
        convw = sv["convw"]
        gca, dproj = brancha_bwd(dyah, proj, convw, dproj, f"bra_bwd{l}", after=st1[3])
        du1, dlg, dlb, dcb, dproj = branchc2_bwd(dych, sv["u1"], proj, vec(ln_c_g, l), vec(ln_c_b, l), dproj, f"brc2_bwd{l}")
        gcc, dproj = branchc1_bwd(du1, proj, convw, dproj, f"brc1_bwd{l}")
        dqh, dkh, dvh, dproj = attn_bwd(dybh, sv["o"], sv["qh"], sv["kh"], sv["vh"], proj, dproj, f"attn_bwd{l}")
        dqn, dkn, dproj = qkv_bwd(dqh, dkh, dvh, proj, vec(q_norm, l), vec(k_norm, l), cos, sin, dproj, f"qkv_bwd{l}")
        gwin = dwin_parts(sv["h"], dproj, f"dwin{l}").reshape(NDEV, PSH, D)
        gconv = jnp.concatenate([gca, gcc], axis=1)
        if l > 0:
            st2 = scatter_start([gwin, gconv], f"rs_win_start{l}")
            issued = st2[3]
        else:
            st2 = scatter_start([gconv], "rs_conv_start0")
            pair = split_start([gwin, lax.empty((NDEV // 2, PSH, D), bf16)], _pair_copies, NDEV // 2, "rs_pair_start0",
                               after=st2[3])
            issued = pair[3]
        dx, dgpre = dh_bwd(dproj, sv["wfull"], sv["x"], dx, vec(norm_pre, l), f"dh{l}", after=issued)
        wide = lambda a: jnp.pad(a, ((0, 0), (0, D - HD)))
        small_parts[l] = jnp.concatenate([dgpre, dgpost, dcb, dlg, dlb, wide(dqn), wide(dkn), jnp.zeros((1, D), f32)], axis=0)
        pending[l] = (st1, st2)

    gwin0, pair_land = split_wait(pair[0], pair[1], pair[2], _pair_copies, dx, "rs_pair_wait0")
    summed = pair_sum(gwin0, pair_land, me, "rs_pair_sum0")
    (small_all,) = all_gather([jnp.concatenate(small_parts, axis=0)], "ag_small")
    chip = split_start([summed, lax.empty(summed.shape, bf16)], _chip_copies, NDEV // 2 - 1, "rs_chip_start0", after=small_all)
    for l in reversed(range(1, L)):
        finish(l, pending[l], after=chip[3])
    sm = adam_update(small_all, small_all, me,
                     _pack_small(norm_pre, norm_post, conv_c_b, ln_c_g, ln_c_b, q_norm, k_norm)[None],
                     _pack_small(m_norm_pre, m_norm_post, m_conv_c_b, m_ln_c_g, m_ln_c_b, m_q_norm, m_k_norm)[None],
                     _pack_small(v_norm_pre, v_norm_post, v_conv_c_b, v_ln_c_g, v_ln_c_b, v_q_norm, v_k_norm)[None],
                     0, None, "adam_small")
    (s1, r1, b1, _), (s2, r2, b2, _) = pending[0]
    gsq_own, rsq = split_wait(s1, r1, b1, _scatter_copies(1), sm[0], "rs_sq_wait0")
    gconv_own, rconv = split_wait(s2, r2, b2, _scatter_copies(1), sm[0], "rs_conv_wait0")
    flat = lambda a: a.reshape(a.shape[0], 4 * rs, D)
    acc["sq"] = adam_update(flat(rsq), flat(gsq_own), me, flat(wsq32), flat(msq32), flat(vsq32), 0, acc["sq"], "adam_wsq0")
    acc["conv"] = adam_update(rconv, gconv_own, me, conv_pack, mconv, vconv, 0, acc["conv"], "adam_conv0")
    summed, chip_land = split_wait(chip[0], chip[1], chip[2], _chip_copies, acc["sq"][0], "rs_chip_wait0")
    acc["win"] = adam_update(chip_land, summed, me // 2, wt, m_wt, v_wt, 0, acc["win"], "adam_win0")
    sm = [a.reshape(L, 8, D) for a in sm]
    small_rows = dict(norm_pre=(0, D), norm_post=(1, D), conv_c_b=(2, D), ln_c_g=(3, D), ln_c_b=(4, D), q_norm=(5, HD), k_norm=(6, HD))
    sq_rows = dict(w_out_a=0, w_out_b=1, w_out_c=2, w_o=3)

    order = ["norm_pre", "norm_post", "w_in", "conv_a_w", "q_norm", "k_norm", "conv_c_w", "conv_c_b", "ln_c_g", "ln_c_b",
             "w_out_a", "w_out_b", "w_out_c", "w_o"]
    result = [loss, dx.reshape(1, S, D)]
    for kind in range(4):
        for nme in order:
            if nme in small_rows:
                rw, wd = small_rows[nme]
                result.append(sm[kind][:, rw, :wd])
            elif nme in sq_rows:
                result.append(acc["sq"][kind][:, sq_rows[nme] * rs:(sq_rows[nme] + 1) * rs])
            elif nme == "w_in":
                result.append(jnp.swapaxes(acc["win"][kind], 1, 2))
            elif nme == "conv_a_w":
                result.append(acc["conv"][kind][:, 0:CA_W])
            else:
                result.append(acc["conv"][kind][:, 8:8 + CC_W])
    return tuple(result)
```

```python
import math

import jax
import jax.numpy as jnp
from jax import lax
from jax.experimental import pallas as pl
from jax.experimental.pallas import tpu as pltpu

f32, bf16 = jnp.float32, jnp.bfloat16

D = 1024
S = 2048
L = 4
HD = 128
NQ = D // HD
NKV = NQ // 4
G = NQ // NKV
WKV = NKV * HD
GRID_W = 64
ROPE_THETA = 10000.0
RMS_EPS = 1e-6
LN_EPS = 1e-5
NDEV = 8
CA_W, CC_W = 3, 31
P = 12 * D + 2 * WKV
PSH = P // NDEV
CT = 128
ADAM_LR, ADAM_B1, ADAM_B2, ADAM_EPS, ADAM_WD, ADAM_STEP = 0.001, 0.9, 0.999, 1e-08, 0.01, 10
VMEM_LIMIT = 56 * 1024 * 1024
MESH = pl.DeviceIdType.MESH

_OFF = {}
_o = 0
for _n, _w in (("a_b", D), ("a_c", D), ("a_x", D), ("a_g", D), ("q", D), ("k", WKV), ("v", WKV), ("b_g", D),
               ("c_u", D), ("c_v", D), ("c_g", D), ("m_a", D), ("m_b", D), ("m_c", D)):
    _OFF[_n] = (_o, _w)
    _o += _w
PIECES = tuple(_OFF)


def _cp(sem=None, **kw):
    return pltpu.CompilerParams(dimension_semantics=sem, vmem_limit_bytes=VMEM_LIMIT, **kw)


def _sig(x):
    return 1.0 / (1.0 + jnp.exp(-x))


def _silu(x):
    return x * _sig(x)


def _dsilu(x):
    s = _sig(x)
    return s * (1.0 + x * (1.0 - s))


def _row_specs(name, tm):
    off, w = _OFF[name]
    bw = math.gcd(off, w) if off else w
    return [pl.BlockSpec((tm, bw), (lambda i, *_, b=off // bw + t: (i, b))) for t in range(w // bw)]


def _cat(refs):
    return refs[0][...] if len(refs) == 1 else jnp.concatenate([r[...] for r in refs], axis=1)


def _chan_spec(name):
    off, _ = _OFF[name]
    return pl.BlockSpec((S, CT), lambda j, b=off // CT: (0, b + j))


def _full(shape):
    return pl.BlockSpec(shape, lambda *_: (0,) * len(shape))


def _coords():
    return lax.axis_index("x"), lax.axis_index("y"), lax.axis_index("c")


def all_gather(shards, name):
    n = len(shards)

    def body(*refs):
        ins, outs = refs[:n], refs[n:2 * n]
        send_sems, recv_sems, local_sems = refs[2 * n:]
        x, y, c = _coords()
        me, sibling = (x, y, c), (x, y, 1 - c)
        chips = [(1 - x, y), (x, 1 - y), (1 - x, 1 - y)]

        def slot(a, p):
            return outs[a].at[4 * p[0] + 2 * p[1] + p[2]]

        def copy(a, k, block, to, src=None):
            return pltpu.make_async_remote_copy(
                src_ref=slot(a, block) if src is None else src, dst_ref=slot(a, block),
                send_sem=send_sems.at[7 * a + k], recv_sem=recv_sems.at[7 * a + k], device_id=to, device_id_type=MESH)

        mine = [pltpu.make_async_copy(ins[a], slot(a, me), local_sems.at[a]) for a in range(n)]
        for cp in mine:
            cp.start()
        first = []
        for a in range(n):
            first.append(copy(a, 0, me, sibling, src=ins[a]))
            first += [copy(a, 1 + j, me, (*chip, c), src=ins[a]) for j, chip in enumerate(chips)]
        for cp in first:
            cp.start()
        passed = []
        for j, chip in enumerate(chips):
            for a in range(n):
                copy(a, 1 + j, (*chip, c), me).wait_recv()
                fw = copy(a, 4 + j, (*chip, c), sibling)
                fw.start()
                passed.append(fw)
        for a in range(n):
            copy(a, 0, sibling, me).wait_recv()
            for j, chip in enumerate(chips):
                copy(a, 4 + j, (*chip, 1 - c), me).wait_recv()
        for cp in first + passed:
            cp.wait_send()
        for cp in mine:
            cp.wait()

    anyspec = pl.BlockSpec(memory_space=pl.ANY)
    return pl.pallas_call(
        body, name=name,
        out_shape=[jax.ShapeDtypeStruct((NDEV,) + s.shape, s.dtype) for s in shards],
        in_specs=[anyspec] * n, out_specs=[anyspec] * n,
        scratch_shapes=[pltpu.SemaphoreType.DMA((7 * n,)), pltpu.SemaphoreType.DMA((7 * n,)), pltpu.SemaphoreType.DMA((n,))],
    )(*shards)


_HBM = pl.BlockSpec(memory_space=pltpu.HBM)
_SEM = pl.BlockSpec(memory_space=pltpu.SEMAPHORE)
_EFFECT = pltpu.SideEffectType.DATAFLOW_SIDE_EFFECTING


def split_start(bufs, make_copies, nsem, name, after=None):
    n = len(bufs)
    extra = [] if after is None else [after]

    def body(*refs):
        send_sems, recv_sems = refs[n + len(extra):n + len(extra) + 2]
        for cp in make_copies(refs[:n], send_sems, recv_sems):
            cp.start()
        refs[-1][...] = jnp.zeros((8, 128), f32)

    res = pl.pallas_call(
        body, name=name,
        out_shape=(pltpu.SemaphoreType.DMA((nsem,)), pltpu.SemaphoreType.DMA((nsem,)),
                   *[pltpu.HBM(b.shape, b.dtype) for b in bufs], jax.ShapeDtypeStruct((8, 128), f32)),
        in_specs=[_HBM] * n + [pl.BlockSpec(memory_space=pl.ANY)] * len(extra),
        out_specs=(_SEM, _SEM, *([_HBM] * n), pl.BlockSpec(memory_space=pltpu.VMEM)),
        input_output_aliases={i: 2 + i for i in range(n)},
        compiler_params=pltpu.CompilerParams(has_side_effects=_EFFECT),
    )(*[pltpu.with_memory_space_constraint(b, pltpu.HBM) for b in bufs], *extra)
    return res[0], res[1], list(res[2:2 + n]), res[-1]


def split_wait(send_sems, recv_sems, bufs, make_copies, after, name):
    n = len(bufs)

    def body(*refs):
        for cp in make_copies(refs[:n], refs[n], refs[n + 1]):
            cp.wait_send()
            cp.wait_recv()

    res = pl.pallas_call(
        body, name=name,
        out_shape=tuple(pltpu.HBM(b.shape, b.dtype) for b in bufs),
        in_specs=[_HBM] * n + [_SEM, _SEM, pl.BlockSpec(memory_space=pl.ANY)],
        out_specs=[_HBM] * n,
        input_output_aliases={i: i for i in range(n)},
        compiler_params=pltpu.CompilerParams(has_side_effects=_EFFECT),
    )(*bufs, send_sems, recv_sems, after)
    return list(res)


def _scatter_copies(n):
    def make(refs, send_sems, recv_sems):
        x, y, c = _coords()
        me = 4 * x + 2 * y + c
        copies = []
        for a in range(n):
            for k in range(1, NDEV):
                px = 1 - x if (k >> 2) & 1 else x
                py = 1 - y if (k >> 1) & 1 else y
                pc = 1 - c if k & 1 else c
                copies.append(pltpu.make_async_remote_copy(
                    src_ref=refs[a].at[4 * px + 2 * py + pc], dst_ref=refs[n + a].at[me],
                    send_sem=send_sems.at[7 * a + k - 1], recv_sem=recv_sems.at[7 * a + k - 1],
                    device_id=(px, py, pc), device_id_type=MESH))
        return copies
    return make


def _gather_copies(refs, send_sems, recv_sems):
    x, y, c = _coords()
    me = 4 * x + 2 * y + c
    targets = [(x, y, 1 - c), (1 - x, y, c), (x, 1 - y, c), (1 - x, 1 - y, c)]
    return [pltpu.make_async_remote_copy(
        src_ref=r.at[me], dst_ref=r.at[me], send_sem=send_sems.at[4 * a + k], recv_sem=recv_sems.at[4 * a + k],
        device_id=to, device_id_type=MESH) for a, r in enumerate(refs) for k, to in enumerate(targets)]


def _forward_copies(refs, send_sems, recv_sems):
    x, y, c = _coords()
    chips = [(1 - x, y), (x, 1 - y), (1 - x, 1 - y)]
    return [pltpu.make_async_remote_copy(
        src_ref=r.at[4 * px + 2 * py + c], dst_ref=r.at[4 * px + 2 * py + c], send_sem=send_sems.at[3 * a + j],
        recv_sem=recv_sems.at[3 * a + j], device_id=(x, y, 1 - c), device_id_type=MESH)
        for a, r in enumerate(refs) for j, (px, py) in enumerate(chips)]


def _pair_copies(refs, send_sems, recv_sems):
    x, y, c = _coords()
    parts, land = refs
    return [pltpu.make_async_remote_copy(
        src_ref=parts.at[2 * j + 1 - c], dst_ref=land.at[j], send_sem=send_sems.at[j], recv_sem=recv_sems.at[j],
        device_id=(x, y, 1 - c), device_id_type=MESH) for j in range(NDEV // 2)]


def _chip_copies(refs, send_sems, recv_sems):
    x, y, c = _coords()
    summed, land = refs
    copies = []
    for k in range(1, NDEV // 2):
        px = 1 - x if (k >> 1) & 1 else x
        py = 1 - y if k & 1 else y
        copies.append(pltpu.make_async_remote_copy(
            src_ref=summed.at[2 * px + py], dst_ref=land.at[2 * x + y], send_sem=send_sems.at[k - 1],
            recv_sem=recv_sems.at[k - 1], device_id=(px, py, c), device_id_type=MESH))
    return copies


def pair_sum(parts, land, me, name):
    _, r, c = parts.shape
    tr = _row_tile(r, 800)

    def body(me_ref, a_ref, b_ref, o_ref):
        o_ref[...] = (a_ref[...].astype(f32) + b_ref[...].astype(f32)).astype(bf16)

    blk = pl.BlockSpec((1, tr, c), lambda j, i, m: (j, i, 0))
    return pl.pallas_call(
        body, name=name, out_shape=jax.ShapeDtypeStruct((NDEV // 2, r, c), bf16),
        grid_spec=pltpu.PrefetchScalarGridSpec(
            num_scalar_prefetch=1, grid=(NDEV // 2, r // tr),
            in_specs=[pl.BlockSpec((1, tr, c), lambda j, i, m: (2 * j + m[0] % 2, i, 0)), blk], out_specs=blk),
        compiler_params=_cp(("parallel", "parallel")))(me, parts, land)


def _row_tile(r, cap=256):
    return r if r <= cap else max(t for t in (800, 512, 256, 160, 128) if t <= cap and r % t == 0)


def stage_shards(wt, wsq, conv, l, me, name, after=None):
    outs = []
    extra = [] if after is None else [after]
    for a, dt in ((wt, bf16), (wsq, bf16), (conv, f32)):
        _, r, c = a.shape
        tr = _row_tile(r, 800)

        def body(me_ref, a_ref, *rest):
            rest[-1][...] = a_ref[...].astype(rest[-1].dtype)

        outs.append(pl.pallas_call(
            body, name=f"{name}_{len(outs)}", out_shape=jax.ShapeDtypeStruct((NDEV, r, c), dt),
            grid_spec=pltpu.PrefetchScalarGridSpec(
                num_scalar_prefetch=1, grid=(r // tr,),
                in_specs=[pl.BlockSpec((1, tr, c), lambda i, m: (l, i, 0))] + [pl.BlockSpec(memory_space=pl.ANY)] * len(extra),
                out_specs=pl.BlockSpec((1, tr, c), lambda i, m: (m[0], i, 0))),
            compiler_params=_cp(("arbitrary",)))(me, a, *extra))
    return outs


def proj_fwd(xin, g_pre, wt, name):
    tm, tn = min(S, 1024), 2560

    def body(x_ref, g_ref, w_ref, proj_ref, h_ref, hs):
        @pl.when(pl.program_id(1) == 0)
        def _():
            x = x_ref[...]
            r = lax.rsqrt(jnp.mean(x * x, axis=-1, keepdims=True) + RMS_EPS)
            h = (x * r * g_ref[...]).astype(bf16)
            hs[...] = h
            h_ref[...] = h
        proj_ref[...] = lax.dot_general(hs[...], w_ref[...], (((1,), (1,)), ((), ())), preferred_element_type=f32)

    return pl.pallas_call(
        body, name=name, grid=(S // tm, P // tn),
        out_shape=[jax.ShapeDtypeStruct((S, P), f32), jax.ShapeDtypeStruct((S, D), bf16)],
        in_specs=[pl.BlockSpec((tm, D), lambda i, j: (i, 0)), _full((1, D)), pl.BlockSpec((tn, D), lambda i, j: (j, 0))],
        out_specs=[pl.BlockSpec((tm, tn), lambda i, j: (i, j)), pl.BlockSpec((tm, D), lambda i, j: (i, 0))],
        scratch_shapes=[pltpu.VMEM((tm, D), bf16)],
        compiler_params=_cp(("parallel", "arbitrary")))(xin, g_pre, wt)


RC = 256


def _fill_pad(pad, halo, val_fn):
    pad[0:halo, :] = jnp.zeros((halo, CT), f32)
    pad[S + halo:S + 2 * halo, :] = jnp.zeros((halo, CT), f32)

    def step(i, carry):
        rows = pl.ds(pl.multiple_of(i * RC, RC), RC)
        pad[pl.ds(pl.multiple_of(i * RC, RC) + halo, RC), :] = val_fn(rows)
        return carry
    lax.fori_loop(0, S // RC, step, 0)


def brancha_fwd(proj, convw, name):
    def body(ab, ac, ax, ag, w_ref, o_ref, pad):
        _fill_pad(pad, 8, lambda rows: ac[rows, :] * ax[rows, :])
        w = [w_ref[0, k:k + 1, :] for k in range(CA_W)]

        def step(i, carry):
            base = pl.multiple_of(i * RC, RC)
            rows = pl.ds(base, RC)
            t = sum(w[k] * pad[pl.ds(base + 7 + k, RC), :] for k in range(CA_W))
            o_ref[rows, :] = (ab[rows, :] * t * _silu(ag[rows, :])).astype(bf16)
            return carry
        lax.fori_loop(0, S // RC, step, 0)

    return pl.pallas_call(
        body, name=name, grid=(D // CT,), out_shape=jax.ShapeDtypeStruct((S, D), bf16),
        in_specs=[_chan_spec("a_b"), _chan_spec("a_c"), _chan_spec("a_x"), _chan_spec("a_g"),
                  pl.BlockSpec((1, 40, CT), lambda j: (j, 0, 0))],
        out_specs=pl.BlockSpec((S, CT), lambda j: (0, j)),
        scratch_shapes=[pltpu.VMEM((S + 16, CT), f32)],
        compiler_params=_cp(("parallel",)))(proj, proj, proj, proj, convw)


def branchc1_fwd(proj, convw, cbias, name):
    def body(cu, cv, w_ref, b_ref, o_ref, pad):
        _fill_pad(pad, 16, lambda rows: cu[rows, :] * _sig(cv[rows, :]))

        def step(i, carry):
            base = pl.multiple_of(i * RC, RC)
            acc = jnp.zeros((RC, CT), f32) + b_ref[...]
            for k in range(CC_W):
                acc = acc + w_ref[0, 8 + k:9 + k, :] * pad[pl.ds(base + k + 1, RC), :]
            o_ref[pl.ds(base, RC), :] = acc
            return carry
        lax.fori_loop(0, S // RC, step, 0)

    return pl.pallas_call(
        body, name=name, grid=(D // CT,), out_shape=jax.ShapeDtypeStruct((S, D), f32),
        in_specs=[_chan_spec("c_u"), _chan_spec("c_v"), pl.BlockSpec((1, 40, CT), lambda j: (j, 0, 0)),
                  pl.BlockSpec((1, CT), lambda j: (0, j))],
        out_specs=pl.BlockSpec((S, CT), lambda j: (0, j)),
        scratch_shapes=[pltpu.VMEM((S + 32, CT), f32)],
        compiler_params=_cp(("parallel",)))(proj, proj, convw, cbias)


def _swap32(x):
    lane = lax.broadcasted_iota(jnp.int32, x.shape, 1)
    return jnp.where((lane // 32) % 2 == 1, pltpu.roll(x, 32, 1), pltpu.roll(x, HD - 32, 1))


def _rope(y, cos, sin):
    return y * cos + _swap32(y) * sin


def qkv_fwd(proj, qn, kn, cos, sin, name):
    tm = min(S, 512)
    nq, nk, nv = len(_row_specs("q", tm)), len(_row_specs("k", tm)), len(_row_specs("v", tm))

    def body(*refs):
        q = _cat(refs[:nq])
        k = _cat(refs[nq:nq + nk])
        v = _cat(refs[nq + nk:nq + nk + nv])
        qn_ref, kn_ref, cos_ref, sin_ref, qh_ref, kh_ref, vh_ref = refs[nq + nk + nv:]
        cos, sin = cos_ref[...], sin_ref[...]

        def heads(xx, gn, out_ref, n):
            for h in range(n):
                xh = xx[:, h * HD:(h + 1) * HD]
                r = lax.rsqrt(jnp.mean(xh * xh, axis=-1, keepdims=True) + RMS_EPS)
                out_ref[:, h * HD:(h + 1) * HD] = _rope(xh * r * gn, cos, sin).astype(bf16)
        heads(q, qn_ref[...], qh_ref, NQ)
        heads(k, kn_ref[...], kh_ref, NKV)
        vh_ref[...] = v.astype(bf16)

    row = lambda w: pl.BlockSpec((tm, w), lambda i: (i, 0))
    return pl.pallas_call(
        body, name=name, grid=(S // tm,),
        out_shape=[jax.ShapeDtypeStruct((S, D), bf16), jax.ShapeDtypeStruct((S, WKV), bf16), jax.ShapeDtypeStruct((S, WKV), bf16)],
        in_specs=_row_specs("q", tm) + _row_specs("k", tm) + _row_specs("v", tm) + [_full((1, HD)), _full((1, HD)), row(HD), row(HD)],
        out_specs=[row(D), row(WKV), row(WKV)],
        compiler_params=_cp(("parallel",)))(*([proj] * (nq + nk + nv)), qn, kn, cos, sin)


def _softmax_rows(q, k):
    s = lax.dot_general(q, k, (((1,), (1,)), ((), ())), preferred_element_type=f32)
    p = jnp.exp((s - jnp.max(s, axis=-1, keepdims=True)) * (HD ** -0.5))
    return p, 1.0 / jnp.sum(p, axis=-1, keepdims=True)


GW = G * HD


def attn_fwd(qh, kh, vh, proj, name):
    tq = min(S, 512)
    bg_blk = _OFF["b_g"][0] // GW

    def body(q_ref, k_ref, v_ref, bg_ref, o_ref, y_ref):
        k, v = k_ref[...], v_ref[...]
        for g in range(G):
            cols = slice(g * HD, (g + 1) * HD)
            p, rl = _softmax_rows(q_ref[:, cols], k)
            o = jnp.dot(p.astype(bf16), v, preferred_element_type=f32) * rl
            o_ref[:, cols] = o
            y_ref[:, cols] = (o * _silu(bg_ref[:, cols])).astype(bf16)

    grp = pl.BlockSpec((tq, GW), lambda kv, i: (i, kv))
    kvs = pl.BlockSpec((S, HD), lambda kv, i: (0, kv))
    return pl.pallas_call(
        body, name=name, grid=(NKV, S // tq),
        out_shape=[jax.ShapeDtypeStruct((S, D), f32), jax.ShapeDtypeStruct((S, D), bf16)],
        in_specs=[grp, kvs, kvs, pl.BlockSpec((tq, GW), lambda kv, i: (i, bg_blk + kv))],
        out_specs=[grp, grp],
        compiler_params=_cp(("parallel", "parallel")))(qh, kh, vh, proj)


def _ln_parts(u1):
    mu = jnp.mean(u1, axis=-1, keepdims=True)
    xc = u1 - mu
    rstd = lax.rsqrt(jnp.mean(xc * xc, axis=-1, keepdims=True) + LN_EPS)
    return xc * rstd, rstd


def _after(after):
    return ([], []) if after is None else ([after], [pl.BlockSpec(memory_space=pl.ANY)])


def branchc2_fwd(u1, proj, lng, lnb, name, after=None):
    tm = min(S, 512)
    ncg = len(_row_specs("c_g", tm))
    extra, extra_specs = _after(after)

    def body(*refs):
        u_ref = refs[0]
        cg = _cat(refs[1:1 + ncg])
        g_ref, b_ref = refs[1 + ncg:3 + ncg]
        xh, _ = _ln_parts(u_ref[...])
        refs[-1][...] = (_silu(xh * g_ref[...] + b_ref[...]) * _silu(cg)).astype(bf16)

    row = pl.BlockSpec((tm, D), lambda i: (i, 0))
    return pl.pallas_call(
        body, name=name, grid=(S // tm,), out_shape=jax.ShapeDtypeStruct((S, D), bf16),
        in_specs=[row] + _row_specs("c_g", tm) + [_full((1, D)), _full((1, D))] + extra_specs, out_specs=row,
        compiler_params=_cp(("parallel",)))(u1, *([proj] * ncg), lng, lnb, *extra)


def _wmat(w_ref, kind):
    return w_ref[:, kind].reshape(D, D)


def merge_fwd(xin, yah, ybh, ych, proj, wsq, g_post, name):
    tm = min(S, 512)
    nm = len(_row_specs("m_a", tm))

    def body(*refs):
        x_ref, a_ref, b_ref, c_ref = refs[:4]
        ms = [_cat(refs[4 + t * nm:4 + (t + 1) * nm]) for t in range(3)]
        w_ref, g_ref, ya_ref, yb_ref, yc_ref, y_ref, z_ref, o_ref = refs[4 + 3 * nm:]
        y = jnp.zeros((tm, D), f32)
        for t, (h_ref, out_ref) in enumerate(((a_ref, ya_ref), (b_ref, yb_ref), (c_ref, yc_ref))):
            yt = jnp.dot(h_ref[...], _wmat(w_ref, t), preferred_element_type=f32)
            out_ref[...] = yt
            y = y + _sig(ms[t]) * yt
        yb16 = y.astype(bf16)
        y_ref[...] = yb16
        z = jnp.dot(yb16, _wmat(w_ref, 3), preferred_element_type=f32)
        z_ref[...] = z
        r = lax.rsqrt(jnp.mean(z * z, axis=-1, keepdims=True) + RMS_EPS)
        o_ref[...] = x_ref[...] + z * r * g_ref[...]

    row = pl.BlockSpec((tm, D), lambda i: (i, 0))
    sd = lambda dt: jax.ShapeDtypeStruct((S, D), dt)
    return pl.pallas_call(
        body, name=name, grid=(S // tm,),
        out_shape=[sd(f32), sd(f32), sd(f32), sd(bf16), sd(f32), sd(f32)],
        in_specs=[row] * 4 + _row_specs("m_a", tm) + _row_specs("m_b", tm) + _row_specs("m_c", tm)
        + [pl.BlockSpec((NDEV, 4, D // NDEV, D), lambda *_: (0, 0, 0, 0), pipeline_mode=pl.Buffered(1)), _full((1, D))],
        out_specs=[row] * 6,
        compiler_params=_cp(("parallel",)))(xin, yah, ybh, ych, *([proj] * (3 * nm)), wsq, g_post)


def loss_fwd(y, target, name):
    tm = min(S, 256)

    def body(y_ref, t_ref, dy_ref, l_ref):
        e = y_ref[...] - t_ref[...]
        dy_ref[...] = e / D

        @pl.when(pl.program_id(0) == 0)
        def _():
            l_ref[...] = jnp.zeros((1, 128), f32)
        l_ref[...] += (0.5 / D) * jnp.sum(e * e)

    row = pl.BlockSpec((tm, D), lambda i: (i, 0))
    return pl.pallas_call(
        body, name=name, grid=(S // tm,),
        out_shape=[jax.ShapeDtypeStruct((S, D), f32), jax.ShapeDtypeStruct((1, 128), f32)],
        in_specs=[row, row], out_specs=[row, _full((1, 128))],
        compiler_params=_cp(("arbitrary",)))(y, target)


def _acc(ref, val):
    @pl.when(pl.program_id(0) == 0)
    def _():
        ref[...] = jnp.zeros(ref.shape, f32)
    ref[...] += val


def _emit_copies(stash, dst, sems, windows):
    return [pltpu.make_async_copy(stash.at[p], dst.at[w], sems.at[p]) for p, w in enumerate(windows)]


def _emit_drain_previous(copies, step):
    @pl.when(step > 0)
    def _():
        for cp in copies:
            cp.wait()


def _emit_start(copies, step, nsteps):
    for cp in copies:
        cp.start()

    @pl.when(step == nsteps - 1)
    def _():
        for cp in copies:
            cp.wait()


def merge_bwd(dout, z, ya, yb, yc, proj, wsq, g_post, name):
    tm = min(S, 256)
    nm = len(_row_specs("m_a", tm))
    nsteps = S // tm

    def body(*refs):
        do_ref, z_ref, ya_ref, yb_ref, yc_ref = refs[:5]
        ms = [_cat(refs[5 + t * nm:5 + (t + 1) * nm]) for t in range(3)]
        w_ref, g_ref = refs[5 + 3 * nm:7 + 3 * nm]
        dh_refs = refs[7 + 3 * nm:10 + 3 * nm]
        dzb_ref = refs[10 + 3 * nm]
        dyb_refs = refs[11 + 3 * nm:14 + 3 * nm]
        dg_ref = refs[14 + 3 * nm]
        dproj_ref, stash, sems = refs[15 + 3 * nm:]
        i = pl.program_id(0)
        rows = pl.ds(pl.multiple_of(i * tm, tm), tm)
        copies = _emit_copies(stash, dproj_ref, sems, [(rows, pl.ds(_OFF[n][0], D)) for n in ("m_a", "m_b", "m_c")])
        nt = (((1,), (1,)), ((), ()))
        z, dout = z_ref[...], do_ref[...]
        r = lax.rsqrt(jnp.mean(z * z, axis=-1, keepdims=True) + RMS_EPS)
        zh = z * r
        _acc(dg_ref, jnp.sum(dout * zh, axis=0, keepdims=True))
        dzh = dout * g_ref[...]
        dz = (r * (dzh - zh * jnp.mean(dzh * zh, axis=-1, keepdims=True))).astype(bf16)
        dzb_ref[...] = dz
        dy = lax.dot_general(dz, _wmat(w_ref, 3), nt, preferred_element_type=f32)
        dms = []
        for t, yt_ref in enumerate((ya_ref, yb_ref, yc_ref)):
            sg = _sig(ms[t])
            dyt = (dy * sg).astype(bf16)
            dyb_refs[t][...] = dyt
            dms.append((dy * yt_ref[...] * sg * (1.0 - sg)).astype(bf16))
            dh_refs[t][...] = lax.dot_general(dyt, _wmat(w_ref, t), nt, preferred_element_type=f32)
        _emit_drain_previous(copies, i)
        for t in range(3):
            stash[t] = dms[t]
        _emit_start(copies, i, nsteps)

    row = pl.BlockSpec((tm, D), lambda i: (i, 0))
    sd = lambda dt: jax.ShapeDtypeStruct((S, D), dt)
    return pl.pallas_call(
        body, name=name, grid=(nsteps,),
        out_shape=[sd(f32)] * 3 + [sd(bf16)] * 4 + [jax.ShapeDtypeStruct((1, D), f32), jax.ShapeDtypeStruct((S, P), bf16)],
        in_specs=[row] * 5 + _row_specs("m_a", tm) + _row_specs("m_b", tm) + _row_specs("m_c", tm)
        + [pl.BlockSpec((NDEV, 4, D // NDEV, D), lambda *_: (0, 0, 0, 0), pipeline_mode=pl.Buffered(1)), _full((1, D))],
        out_specs=[row] * 7 + [_full((1, D)), pl.BlockSpec(memory_space=pl.ANY)],
        scratch_shapes=[pltpu.VMEM((3, tm, D), bf16), pltpu.SemaphoreType.DMA((3,))],
        compiler_params=_cp(("arbitrary",)))(dout, z, ya, yb, yc, *([proj] * (3 * nm)), wsq, g_post)


def tn_matmul(a, b, name):
    m, n = a.shape[1], b.shape[1]
    tmm = min(m, 512)

    def body(a_ref, b_ref, o_ref):
        o_ref[...] = lax.dot_general(a_ref[...], b_ref[...], (((0,), (0,)), ((), ())), preferred_element_type=f32).astype(bf16)

    return pl.pallas_call(
        body, name=name, grid=(m // tmm,), out_shape=jax.ShapeDtypeStruct((m, n), bf16),
        in_specs=[pl.BlockSpec((S, tmm), lambda i: (0, i)), _full((S, n))],
        out_specs=pl.BlockSpec((tmm, n), lambda i: (i, 0)),
        compiler_params=_cp(("parallel",)))(a, b)


def dwin_parts(h, dproj, name):
    tn = 1280

    def body(d_ref, h_ref, o_ref):
        o_ref[...] = lax.dot_general(d_ref[...], h_ref[...], (((0,), (0,)), ((), ())), preferred_element_type=f32).astype(bf16)

    return pl.pallas_call(
        body, name=name, grid=(P // tn,), out_shape=jax.ShapeDtypeStruct((P, D), bf16),
        in_specs=[pl.BlockSpec((S, tn), lambda j: (0, j)), _full((S, D))],
        out_specs=pl.BlockSpec((tn, D), lambda j: (j, 0)),
        compiler_params=_cp(("parallel",)))(dproj, h)


def _chan_windows(names, j):
    return [(slice(None), pl.ds(pl.multiple_of(_OFF[n][0] + j * CT, CT), CT)) for n in names]


def brancha_bwd(dyah, proj, convw, dproj, name, after=None):
    nsteps = D // CT
    extra, extra_specs = _after(after)

    def body(d_ref, ab, ac, ax, ag, w_ref, *rest):
        dw_ref, dproj_ref, padp, padt, accw, stash, sems = rest[-7:]
        j = pl.program_id(0)
        copies = _emit_copies(stash, dproj_ref, sems, _chan_windows(("a_b", "a_c", "a_x", "a_g"), j))
        _fill_pad(padp, 8, lambda rows: ac[rows, :] * ax[rows, :])
        _fill_pad(padt, 8, lambda rows: d_ref[rows, :] * ab[rows, :] * _silu(ag[rows, :]))
        accw[...] = jnp.zeros(accw.shape, f32)
        w = [w_ref[0, k:k + 1, :] for k in range(CA_W)]
        _emit_drain_previous(copies, j)

        def step(i, carry):
            base = pl.multiple_of(i * RC, RC)
            rows = pl.ds(base, RC)
            ps = [padp[pl.ds(base + 7 + k, RC), :] for k in range(CA_W)]
            t = sum(w[k] * ps[k] for k in range(CA_W))
            dp = sum(w[k] * padt[pl.ds(base + 9 - k, RC), :] for k in range(CA_W))
            d, a_b, a_g = d_ref[rows, :], ab[rows, :], ag[rows, :]
            stash[0, rows, :] = (d * t * _silu(a_g)).astype(bf16)
            stash[1, rows, :] = (dp * ax[rows, :]).astype(bf16)
            stash[2, rows, :] = (dp * ac[rows, :]).astype(bf16)
            stash[3, rows, :] = (d * a_b * t * _dsilu(a_g)).astype(bf16)
            dt = padt[pl.ds(base + 8, RC), :]
            for k in range(CA_W):
                accw[8 * k:8 * k + 8, :] += jnp.sum((dt * ps[k]).reshape(RC // 8, 8, CT), axis=0)
            return carry
        lax.fori_loop(0, S // RC, step, 0)
        _emit_start(copies, j, nsteps)
        dw_ref[0] = jnp.zeros((8, CT), f32)
        for k in range(CA_W):
            dw_ref[0, k:k + 1, :] = jnp.sum(accw[8 * k:8 * k + 8, :], axis=0, keepdims=True)

    tile = pl.BlockSpec((S, CT), lambda j: (0, j))
    anyspec = pl.BlockSpec(memory_space=pl.ANY)
    return pl.pallas_call(
        body, name=name, grid=(nsteps,),
        out_shape=[jax.ShapeDtypeStruct((NDEV, 8, CT), f32), jax.ShapeDtypeStruct((S, P), bf16)],
        in_specs=[tile, _chan_spec("a_b"), _chan_spec("a_c"), _chan_spec("a_x"), _chan_spec("a_g"),
                  pl.BlockSpec((1, 40, CT), lambda j: (j, 0, 0)), anyspec] + extra_specs,
        out_specs=[pl.BlockSpec((1, 8, CT), lambda j: (j, 0, 0)), anyspec],
        input_output_aliases={6: 1},
        scratch_shapes=[pltpu.VMEM((S + 16, CT), f32), pltpu.VMEM((S + 16, CT), f32), pltpu.VMEM((8 * CA_W, CT), f32),
                        pltpu.VMEM((4, S, CT), bf16), pltpu.SemaphoreType.DMA((4,))],
        compiler_params=_cp(("arbitrary",)))(dyah, proj, proj, proj, proj, convw, dproj, *extra)


def branchc2_bwd(dych, u1, proj, lng, lnb, dproj, name):
    tm = min(S, 512)
    ncg = len(_row_specs("c_g", tm))
    nsteps = S // tm

    def body(*refs):
        d_ref, u_ref = refs[:2]
        cg = _cat(refs[2:2 + ncg])
        g_ref, b_ref, _, du_ref, dlg_ref, dlb_ref, dcb_ref, dproj_ref, stash, sems = refs[2 + ncg:]
        i = pl.program_id(0)
        copies = _emit_copies(stash, dproj_ref, sems, [(pl.ds(pl.multiple_of(i * tm, tm), tm), pl.ds(_OFF["c_g"][0], D))])
        d = d_ref[...]
        xh, rstd = _ln_parts(u_ref[...])
        ln = xh * g_ref[...] + b_ref[...]
        _emit_drain_previous(copies, i)
        stash[0] = (d * _silu(ln) * _dsilu(cg)).astype(bf16)
        _emit_start(copies, i, nsteps)
        dln = d * _silu(cg) * _dsilu(ln)
        _acc(dlg_ref, jnp.sum(dln * xh, axis=0, keepdims=True))
        _acc(dlb_ref, jnp.sum(dln, axis=0, keepdims=True))
        dxh = dln * g_ref[...]
        du = rstd * (dxh - jnp.mean(dxh, axis=-1, keepdims=True) - xh * jnp.mean(dxh * xh, axis=-1, keepdims=True))
        du_ref[...] = du
        _acc(dcb_ref, jnp.sum(du, axis=0, keepdims=True))

    row = pl.BlockSpec((tm, D), lambda i: (i, 0))
    vec = jax.ShapeDtypeStruct((1, D), f32)
    anyspec = pl.BlockSpec(memory_space=pl.ANY)
    return pl.pallas_call(
        body, name=name, grid=(nsteps,),
        out_shape=[jax.ShapeDtypeStruct((S, D), f32), vec, vec, vec, jax.ShapeDtypeStruct((S, P), bf16)],
        in_specs=[row, row] + _row_specs("c_g", tm) + [_full((1, D)), _full((1, D)), anyspec],
        out_specs=[row, _full((1, D)), _full((1, D)), _full((1, D)), anyspec],
        input_output_aliases={4 + ncg: 4},
        scratch_shapes=[pltpu.VMEM((1, tm, D), bf16), pltpu.SemaphoreType.DMA((1,))],
        compiler_params=_cp(("arbitrary",)))(dych, u1, *([proj] * ncg), lng, lnb, dproj)


def branchc1_bwd(du1, proj, convw, dproj, name):
    nsteps = D // CT

    def body(d_ref, cu, cv, w_ref, _, dw_ref, dproj_ref, padu, padd, accw, stash, sems):
        j = pl.program_id(0)
        copies = _emit_copies(stash, dproj_ref, sems, _chan_windows(("c_u", "c_v"), j))
        _fill_pad(padu, 16, lambda rows: cu[rows, :] * _sig(cv[rows, :]))
        _fill_pad(padd, 16, lambda rows: d_ref[rows, :])
        accw[...] = jnp.zeros(accw.shape, f32)
        _emit_drain_previous(copies, j)

        rc = min(S, 128)

        def step(i, carry):
            base = pl.multiple_of(i * rc, rc)
            rows = pl.ds(base, rc)
            d = d_ref[rows, :]
            du0 = jnp.zeros((rc, CT), f32)
            for k in range(CC_W):
                du0 = du0 + w_ref[0, 8 + k:9 + k, :] * padd[pl.ds(base + 31 - k, rc), :]
                accw[8 * k:8 * k + 8, :] += jnp.sum((d * padu[pl.ds(base + k + 1, rc), :]).reshape(rc // 8, 8, CT), axis=0)
            sg = _sig(cv[rows, :])
            stash[0, rows, :] = (du0 * sg).astype(bf16)
            stash[1, rows, :] = (du0 * cu[rows, :] * sg * (1.0 - sg)).astype(bf16)
            return carry
        lax.fori_loop(0, S // rc, step, 0)
        _emit_start(copies, j, nsteps)
        dw_ref[0] = jnp.zeros((32, CT), f32)
        for k in range(CC_W):
            dw_ref[0, k:k + 1, :] = jnp.sum(accw[8 * k:8 * k + 8, :], axis=0, keepdims=True)

    tile = pl.BlockSpec((S, CT), lambda j: (0, j))
    anyspec = pl.BlockSpec(memory_space=pl.ANY)
    return pl.pallas_call(
        body, name=name, grid=(nsteps,),
        out_shape=[jax.ShapeDtypeStruct((NDEV, 32, CT), f32), jax.ShapeDtypeStruct((S, P), bf16)],
        in_specs=[tile, _chan_spec("c_u"), _chan_spec("c_v"), pl.BlockSpec((1, 40, CT), lambda j: (j, 0, 0)), anyspec],
        out_specs=[pl.BlockSpec((1, 32, CT), lambda j: (j, 0, 0)), anyspec],
        input_output_aliases={4: 1},
        scratch_shapes=[pltpu.VMEM((S + 32, CT), f32), pltpu.VMEM((S + 32, CT), f32), pltpu.VMEM((8 * 32, CT), f32),
                        pltpu.VMEM((2, S, CT), bf16), pltpu.SemaphoreType.DMA((2,))],
        compiler_params=_cp(("arbitrary",)))(du1, proj, proj, convw, dproj)


def attn_bwd(dybh, o, qh, kh, vh, proj, dproj, name):
    tq = min(S, 512)
    bg_blk = _OFF["b_g"][0] // GW

    def body(d_ref, o_ref, q_ref, k_ref, v_ref, bg_ref, _, dq_ref, dk_ref, dv_ref, dbg_ref):
        @pl.when(pl.program_id(1) == 0)
        def _():
            dk_ref[...] = jnp.zeros(dk_ref.shape, f32)
            dv_ref[...] = jnp.zeros(dv_ref.shape, f32)
        k, v = k_ref[...], v_ref[...]
        tn = (((0,), (0,)), ((), ()))
        dk_acc = jnp.zeros((S, HD), f32)
        dv_acc = jnp.zeros((S, HD), f32)
        for g in range(G):
            cols = slice(g * HD, (g + 1) * HD)
            d, bg, q, o = d_ref[:, cols], bg_ref[:, cols], q_ref[:, cols], o_ref[:, cols]
            dbg_ref[:, cols] = (d * o * _dsilu(bg)).astype(bf16)
            do = d * _silu(bg)
            p, rl = _softmax_rows(q, k)
            dv_acc = dv_acc + lax.dot_general(p.astype(bf16), (do * rl).astype(bf16), tn, preferred_element_type=f32)
            dp = lax.dot_general(do.astype(bf16), v, (((1,), (1,)), ((), ())), preferred_element_type=f32)
            delta = jnp.sum(do * o, axis=-1, keepdims=True)
            ds = (p * (dp - delta)).astype(bf16)
            rs_ = rl * (HD ** -0.5)
            dq_ref[:, cols] = jnp.dot(ds, k, preferred_element_type=f32) * rs_
            dk_acc = dk_acc + lax.dot_general(ds, (q.astype(f32) * rs_).astype(bf16), tn, preferred_element_type=f32)
        dk_ref[...] += dk_acc
        dv_ref[...] += dv_acc

    grp = pl.BlockSpec((tq, GW), lambda kv, i: (i, kv))
    kvs = pl.BlockSpec((S, HD), lambda kv, i: (0, kv))
    return pl.pallas_call(
        body, name=name, grid=(NKV, S // tq),
        out_shape=[jax.ShapeDtypeStruct((S, D), f32), jax.ShapeDtypeStruct((S, WKV), f32),
                   jax.ShapeDtypeStruct((S, WKV), f32), jax.ShapeDtypeStruct((S, P), bf16)],
        in_specs=[grp, grp, grp, kvs, kvs, pl.BlockSpec((tq, GW), lambda kv, i: (i, bg_blk + kv)),
                  pl.BlockSpec(memory_space=pl.ANY)],
        out_specs=[grp, kvs, kvs, pl.BlockSpec((tq, GW), lambda kv, i: (i, bg_blk + kv))],
        input_output_aliases={6: 3},
        compiler_params=_cp(("parallel", "arbitrary")))(dybh, o, qh, kh, vh, proj, dproj)


def qkv_bwd(dqh, dkh, dvh, proj, qn, kn, cos, sin, dproj, name):
    tm = min(S, 512)
    nq, nk = len(_row_specs("q", tm)), len(_row_specs("k", tm))
    nsteps = S // tm
    wq = D + 2 * WKV

    def body(*refs):
        dqh_ref, dkh_ref, dvh_ref = refs[:3]
        q = _cat(refs[3:3 + nq])
        k = _cat(refs[3 + nq:3 + nq + nk])
        qn_ref, kn_ref, cos_ref, sin_ref, _, dqn_ref, dkn_ref, dproj_ref, stash, sems = refs[3 + nq + nk:]
        i = pl.program_id(0)
        copies = _emit_copies(stash, dproj_ref, sems, [(pl.ds(pl.multiple_of(i * tm, tm), tm), pl.ds(_OFF["q"][0], wq))])
        cos, sin = cos_ref[...], sin_ref[...]
        _emit_drain_previous(copies, i)

        def heads(xx, dd, gn, col0, dgn_ref, n):
            dg = jnp.zeros((1, HD), f32)
            for h in range(n):
                xh = xx[:, h * HD:(h + 1) * HD]
                dh = dd[:, h * HD:(h + 1) * HD]
                r = lax.rsqrt(jnp.mean(xh * xh, axis=-1, keepdims=True) + RMS_EPS)
                xn = xh * r
                dy = dh * cos + _swap32(dh * sin)
                dg = dg + jnp.sum(dy * xn, axis=0, keepdims=True)
                dxn = dy * gn
                stash[0, :, col0 + h * HD:col0 + (h + 1) * HD] = (
                    r * (dxn - xn * jnp.mean(dxn * xn, axis=-1, keepdims=True))).astype(bf16)
            _acc(dgn_ref, dg)
        heads(q, dqh_ref[...], qn_ref[...], 0, dqn_ref, NQ)
        heads(k, dkh_ref[...], kn_ref[...], D, dkn_ref, NKV)
        stash[0, :, D + WKV:wq] = dvh_ref[...].astype(bf16)
        _emit_start(copies, i, nsteps)

    row = lambda w: pl.BlockSpec((tm, w), lambda i: (i, 0))
    vec = jax.ShapeDtypeStruct((1, HD), f32)
    anyspec = pl.BlockSpec(memory_space=pl.ANY)
    return pl.pallas_call(
        body, name=name, grid=(nsteps,),
        out_shape=[vec, vec, jax.ShapeDtypeStruct((S, P), bf16)],
        in_specs=[row(D), row(WKV), row(WKV)] + _row_specs("q", tm) + _row_specs("k", tm)
        + [_full((1, HD)), _full((1, HD)), row(HD), row(HD), anyspec],
        out_specs=[_full((1, HD)), _full((1, HD)), anyspec],
        input_output_aliases={7 + nq + nk: 2},
        scratch_shapes=[pltpu.VMEM((1, tm, wq), bf16), pltpu.SemaphoreType.DMA((1,))],
        compiler_params=_cp(("arbitrary",)))(dqh, dkh, dvh, *([proj] * (nq + nk)), qn, kn, cos, sin, dproj)


def dh_bwd(dproj, wfull, xin, dout, g_pre, name, after=None):
    tm, tk = min(S, 1024), 2560
    nk = P // tk
    extra, extra_specs = _after(after)

    def body(d_ref, w_ref, x_ref, do_ref, g_ref, *rest):
        dx_ref, dg_ref, acc = rest[-3:]
        kk = pl.program_id(1)

        @pl.when(kk == 0)
        def _():
            acc[...] = jnp.zeros(acc.shape, f32)
        acc[...] += jnp.dot(d_ref[...], w_ref[...], preferred_element_type=f32)

        @pl.when((kk == 0) & (pl.program_id(0) == 0))
        def _():
            dg_ref[...] = jnp.zeros(dg_ref.shape, f32)

        @pl.when(kk == nk - 1)
        def _():
            x, dh = x_ref[...], acc[...]
            r = lax.rsqrt(jnp.mean(x * x, axis=-1, keepdims=True) + RMS_EPS)
            xn = x * r
            dg_ref[...] += jnp.sum(dh * xn, axis=0, keepdims=True)
            dxn = dh * g_ref[...]
            dx_ref[...] = do_ref[...] + r * (dxn - xn * jnp.mean(dxn * xn, axis=-1, keepdims=True))

    row = pl.BlockSpec((tm, D), lambda i, k: (i, 0))
    return pl.pallas_call(
        body, name=name, grid=(S // tm, nk),
        out_shape=[jax.ShapeDtypeStruct((S, D), f32), jax.ShapeDtypeStruct((1, D), f32)],
        in_specs=[pl.BlockSpec((tm, tk), lambda i, k: (i, k)), pl.BlockSpec((tk, D), lambda i, k: (k, 0)), row, row, _full((1, D))]
        + extra_specs,
        out_specs=[row, _full((1, D))],
        scratch_shapes=[pltpu.VMEM((tm, D), f32)],
        compiler_params=_cp(("arbitrary", "arbitrary")))(dproj, wfull, xin, dout, g_pre, *extra)


def adam_update(parts, own, me, w, m, v, l, acc, name):
    lw, r, c = w.shape
    tr = _row_tile(r)
    nslots = parts.shape[0]

    def body(me_ref, p_ref, own_ref, w_ref, m_ref, v_ref, *rest):
        g_ref, d_ref, nm_ref, nv_ref = rest[-4:]
        g = None
        for s in range(nslots):
            part = jnp.where(me_ref[0] == s, own_ref[0], p_ref[s]).astype(f32)
            g = part if g is None else g + part
        nm = ADAM_B1 * m_ref[0] + (1.0 - ADAM_B1) * g
        nv = ADAM_B2 * v_ref[0] + (1.0 - ADAM_B2) * (g * g)
        m_hat = nm / (1.0 - ADAM_B1 ** ADAM_STEP)
        v_hat = nv / (1.0 - ADAM_B2 ** ADAM_STEP)
        g_ref[0] = g
        d_ref[0] = -ADAM_LR * (m_hat / (jnp.sqrt(v_hat) + ADAM_EPS) + ADAM_WD * w_ref[0])
        nm_ref[0] = nm
        nv_ref[0] = nv

    blk = pl.BlockSpec((1, tr, c), lambda i, me_ref: (l, i, 0))
    sd = jax.ShapeDtypeStruct((lw, r, c), f32)
    extra = [] if acc is None else list(acc)
    return pl.pallas_call(
        body, name=name, out_shape=[sd] * 4,
        grid_spec=pltpu.PrefetchScalarGridSpec(
            num_scalar_prefetch=1, grid=(r // tr,),
            in_specs=[pl.BlockSpec((nslots, tr, c), lambda i, me_ref: (0, i, 0)),
                      pl.BlockSpec((1, tr, c), lambda i, me_ref: (me_ref[0], i, 0)), blk, blk, blk]
            + [pl.BlockSpec(memory_space=pl.ANY)] * len(extra),
            out_specs=[blk] * 4),
        input_output_aliases={6 + t: t for t in range(len(extra))},
        compiler_params=_cp(("parallel",)))(me, parts, own, w, m, v, *extra)


def _rope_tables():
    t = jnp.arange(S)
    rows, cols = (t // GRID_W).astype(f32), (t % GRID_W).astype(f32)
    nf = HD // 4
    inv = ROPE_THETA ** (-jnp.arange(nf, dtype=f32) / nf)
    ar, ac = rows[:, None] * inv, cols[:, None] * inv
    cos = jnp.concatenate([jnp.cos(ar), jnp.cos(ar), jnp.cos(ac), jnp.cos(ac)], axis=1)
    sin = jnp.concatenate([-jnp.sin(ar), jnp.sin(ar), -jnp.sin(ac), jnp.sin(ac)], axis=1)
    return cos, sin


def _pack_conv(ca, cc):
    z = lambda n: jnp.zeros((L, n, CT), f32)
    return jnp.concatenate([ca, z(5), cc, z(1)], axis=1)


def _pack_small(npre, npost, ccb, lng, lnb, qn, kn):
    wide = lambda a: jnp.pad(a, ((0, 0), (0, D - HD)))
    return jnp.stack([npre, npost, ccb, lng, lnb, wide(qn), wide(kn), jnp.zeros((L, D), f32)], axis=1).reshape(L * 8, D)


def kernel(x, norm_pre, norm_post, w_in, conv_a_w, q_norm, k_norm, conv_c_w, conv_c_b, ln_c_g, ln_c_b, w_out_a, w_out_b, w_out_c, w_o, loss_target, m_norm_pre, m_norm_post, m_w_in, m_conv_a_w, m_q_norm, m_k_norm, m_conv_c_w, m_conv_c_b, m_ln_c_g, m_ln_c_b, m_w_out_a, m_w_out_b, m_w_out_c, m_w_o, v_norm_pre, v_norm_post, v_w_in, v_conv_a_w, v_q_norm, v_k_norm, v_conv_c_w, v_conv_c_b, v_ln_c_g, v_ln_c_b, v_w_out_a, v_w_out_b, v_w_out_c, v_w_o):
    cos, sin = _rope_tables()
    rs = D // NDEV
    stack_sq = lambda a, b, c, d: jnp.stack([a, b, c, d], axis=1)
    wsq32 = stack_sq(w_out_a, w_out_b, w_out_c, w_o)
    conv_pack = _pack_conv(conv_a_w, conv_c_w)
    vec = lambda a, l: a[l][None, :]
    me = (4 * lax.axis_index("x") + 2 * lax.axis_index("y") + lax.axis_index("c")).astype(jnp.int32).reshape(1)

    def gather_start(l, after):
        return split_start(staged[l], _gather_copies, 12, f"ag_start{l}", after=after)

    def forward_start(l, after):
        s_sems, r_sems, bufs, _ = gathers[l]
        bufs = split_wait(s_sems, r_sems, bufs, _gather_copies, after, f"ag_wait{l}")
        fw = split_start(bufs, _forward_copies, 9, f"ag_fwd_start{l}")
        if l + 1 < L:
            gathers[l + 1] = gather_start(l + 1, fw[3])
            return fw, gathers[l + 1][3]
        return fw, fw[3]

    def forward_wait(fw, after, l):
        s_sems, r_sems, bufs, _ = fw
        return split_wait(s_sems, r_sems, bufs, _forward_copies, after, f"ag_fwd_wait{l}")

    wt, m_wt, v_wt = (jnp.swapaxes(a, 1, 2) for a in (w_in, m_w_in, v_w_in))
    xs, saved = x.reshape(S, D), []
    stage = lambda l, after: stage_shards(wt, wsq32.reshape(L, 4 * rs, D), conv_pack, l, me, f"stage{l}", after)
    staged = [stage(0, None)]
    first = split_start([staged[0][0], staged[0][2]], _gather_copies, 8, "ag_start0")
    second = split_start([staged[0][1]], _gather_copies, 4, "ag_startb0", after=first[3])
    gathers = [None] * L
    staged += [stage(l, second[3]) for l in range(1, L)]
    bufs = split_wait(first[0], first[1], first[2], _gather_copies, staged[L - 1][0] if L > 1 else xs, "ag_wait0")
    fw = split_start(bufs, _forward_copies, 6, "ag_fwd_start0")
    issued = fw[3]
    if L > 1:
        gathers[1] = gather_start(1, fw[3])
        issued = gathers[1][3]
    wg, convw = split_wait(fw[0], fw[1], fw[2], _forward_copies, issued, "ag_fwd_wait0")
    wsq = None
    for l in range(L):
        wfull = wg.reshape(P, D)
        proj, h = proj_fwd(xs, vec(norm_pre, l), wfull, f"proj{l}")
        yah = brancha_fwd(proj, convw, f"bra{l}")
        u1 = branchc1_fwd(proj, convw, vec(conv_c_b, l), f"brc1_{l}")
        qh, kh, vh = qkv_fwd(proj, vec(q_norm, l), vec(k_norm, l), cos, sin, f"qkv{l}")
        o, ybh = attn_fwd(qh, kh, vh, proj, f"attn{l}")
        issued, after = None, o
        if l == 0:
            bufs = split_wait(second[0], second[1], second[2], _gather_copies, o, "ag_waitb0")
            fwb = split_start(bufs, _forward_copies, 3, "ag_fwd_startb0")
            issued = after = fwb[3]
        if l + 1 < L:
            fw, issued = forward_start(l + 1, after)
        ych = branchc2_fwd(u1, proj, vec(ln_c_g, l), vec(ln_c_b, l), f"brc2_{l}", after=issued)
        if l == 0:
            (wsq,) = split_wait(fwb[0], fwb[1], fwb[2], _forward_copies, ych, "ag_fwd_waitb0")
        wsq = wsq.reshape(NDEV, 4, rs, D)
        ya, yb, yc, y16, z, xo = merge_fwd(xs, yah, ybh, ych, proj, wsq, vec(norm_post, l), f"merge{l}")
        saved.append(dict(x=xs, wfull=wfull, wsq=wsq, convw=convw, proj=proj, h=h, yah=yah, ybh=ybh, ych=ych, u1=u1,
                          qh=qh, kh=kh, vh=vh, o=o, ya=ya, yb=yb, yc=yc, y16=y16, z=z))
        xs = xo
        if l + 1 < L:
            wg, wsq, convw = forward_wait(fw, xs, l + 1)
    dx, loss_part = loss_fwd(xs, loss_target.reshape(S, D), "loss")
    loss = lax.psum(loss_part[0, 0], ("x", "y", "c"))

    acc = dict(win=None, sq=None, conv=None)
    small_parts = [None] * L
    msq32 = stack_sq(m_w_out_a, m_w_out_b, m_w_out_c, m_w_o)
    vsq32 = stack_sq(v_w_out_a, v_w_out_b, v_w_out_c, v_w_o)
    mconv, vconv = _pack_conv(m_conv_a_w, m_conv_c_w), _pack_conv(v_conv_a_w, v_conv_c_w)

    def scatter_start(parts, name, after=None):
        bufs = parts + [lax.empty(p.shape, p.dtype) for p in parts]
        return split_start(bufs, _scatter_copies(len(parts)), 7 * len(parts), name, after=after)

    def finish(l, started, after):
        (s1, r1, b1, _), (s2, r2, b2, _) = started
        gsq_own, rsq = split_wait(s1, r1, b1, _scatter_copies(1), after, f"rs_sq_wait{l}")
        gwin_own, gconv_own, rwin, rconv = split_wait(s2, r2, b2, _scatter_copies(2), after, f"rs_win_wait{l}")
        flat = lambda a: a.reshape(a.shape[0], 4 * rs, D)
        acc["win"] = adam_update(rwin, gwin_own, me, wt, m_wt, v_wt, l, acc["win"], f"adam_win{l}")
        acc["sq"] = adam_update(flat(rsq), flat(gsq_own), me, flat(wsq32), flat(msq32), flat(vsq32), l, acc["sq"], f"adam_wsq{l}")
        acc["conv"] = adam_update(rconv, gconv_own, me, conv_pack, mconv, vconv, l, acc["conv"], f"adam_conv{l}")

    pending = [None] * L
    for l in reversed(range(L)):
        sv = saved[l]
        proj = sv["proj"]
        (dyah, dybh, dych, dzb, dyab, dybb, dycb, dgpost, dproj) = merge_bwd(
            dx, sv["z"], sv["ya"], sv["yb"], sv["yc"], proj, sv["wsq"], vec(norm_post, l), f"merge_bwd{l}")
        gsq = [tn_matmul(a, b, f"dwsq{t}_{l}") for t, (a, b) in enumerate(
            ((sv["yah"], dyab), (sv["ybh"], dybb), (sv["ych"], dycb), (sv["y16"], dzb)))]
        gsq_parts = jnp.stack([g.reshape(NDEV, rs, D) for g in gsq], axis=1)
        st1 = scatter_start([gsq_parts], f"rs_sq_start{l}", after=loss.reshape(1, 1) if l == L - 1 else None)
        convw = sv["convw"]
        gca, dproj = brancha_bwd(dyah, proj, convw, dproj, f"bra_bwd{l}", after=st1[3])
        du1, dlg, dlb, dcb, dproj = branchc2_bwd(dych, sv["u1"], proj, vec(ln_c_g, l), vec(ln_c_b, l), dproj, f"brc2_bwd{l}")
        gcc, dproj = branchc1_bwd(du1, proj, convw, dproj, f"brc1_bwd{l}")
        dqh, dkh, dvh, dproj = attn_bwd(dybh, sv["o"], sv["qh"], sv["kh"], sv["vh"], proj, dproj, f"attn_bwd{l}")
        dqn, dkn, dproj = qkv_bwd(dqh, dkh, dvh, proj, vec(q_norm, l), vec(k_norm, l), cos, sin, dproj, f"qkv_bwd{l}")
        gwin = dwin_parts(sv["h"], dproj, f"dwin{l}").reshape(NDEV, PSH, D)
        gconv = jnp.concatenate([gca, gcc], axis=1)
        if l > 0:
            st2 = scatter_start([gwin, gconv], f"rs_win_start{l}")
            issued = st2[3]
        else:
            st2 = scatter_start([gconv], "rs_conv_start0")
            pair = split_start([gwin, lax.empty((NDEV // 2, PSH, D), bf16)], _pair_copies, NDEV // 2, "rs_pair_start0",
                               after=st2[3])
            issued = pair[3]
        dx, dgpre = dh_bwd(dproj, sv["wfull"], sv["x"], dx, vec(norm_pre, l), f"dh{l}", after=issued)
        wide = lambda a: jnp.pad(a, ((0, 0), (0, D - HD)))
        small_parts[l] = jnp.concatenate([dgpre, dgpost, dcb, dlg, dlb, wide(dqn), wide(dkn), jnp.zeros((1, D), f32)], axis=0)
        pending[l] = (st1, st2)

    gwin0, pair_land = split_wait(pair[0], pair[1], pair[2], _pair_copies, dx, "rs_pair_wait0")
    summed = pair_sum(gwin0, pair_land, me, "rs_pair_sum0")
    (small_all,) = all_gather([jnp.concatenate(small_parts, axis=0)], "ag_small")
    chip = split_start([summed, lax.empty(summed.shape, bf16)], _chip_copies, NDEV // 2 - 1, "rs_chip_start0", after=small_all)
    for l in reversed(range(1, L)):
        finish(l, pending[l], after=chip[3])
    sm = adam_update(small_all, small_all, me,
                     _pack_small(norm_pre, norm_post, conv_c_b, ln_c_g, ln_c_b, q_norm, k_norm)[None],
                     _pack_small(m_norm_pre, m_norm_post, m_conv_c_b, m_ln_c_g, m_ln_c_b, m_q_norm, m_k_norm)[None],
                     _pack_small(v_norm_pre, v_norm_post, v_conv_c_b, v_ln_c_g, v_ln_c_b, v_q_norm, v_k_norm)[None],
                     0, None, "adam_small")
    (s1, r1, b1, _), (s2, r2, b2, _) = pending[0]
    gsq_own, rsq = split_wait(s1, r1, b1, _scatter_copies(1), sm[0], "rs_sq_wait0")
    gconv_own, rconv = split_wait(s2, r2, b2, _scatter_copies(1), sm[0], "rs_conv_wait0")
    flat = lambda a: a.reshape(a.shape[0], 4 * rs, D)
    acc["sq"] = adam_update(flat(rsq), flat(gsq_own), me, flat(wsq32), flat(msq32), flat(vsq32), 0, acc["sq"], "adam_wsq0")
    acc["conv"] = adam_update(rconv, gconv_own, me, conv_pack, mconv, vconv, 0, acc["conv"], "adam_conv0")
    summed, chip_land = split_wait(chip[0], chip[1], chip[2], _chip_copies, acc["sq"][0], "rs_chip_wait0")
    acc["win"] = adam_update(chip_land, summed, me // 2, wt, m_wt, v_wt, 0, acc["win"], "adam_win0")
    sm = [a.reshape(L, 8, D) for a in sm]
    small_rows = dict(norm_pre=(0, D), norm_post=(1, D), conv_c_b=(2, D), ln_c_g=(3, D), ln_c_b=(4, D), q_norm=(5, HD), k_norm=(6, HD))
    sq_rows = dict(w_out_a=0, w_out_b=1, w_out_c=2, w_o=3)

    order = ["norm_pre", "norm_post", "w_in", "conv_a_w", "q_norm", "k_norm", "conv_c_w", "conv_c_b", "ln_c_g", "ln_c_b",
             "w_out_a", "w_out_b", "w_out_c", "w_o"]
    result = [loss, dx.reshape(1, S, D)]
    for kind in range(4):
        for nme in order:
            if nme in small_rows:
                rw, wd = small_rows[nme]
                result.append(sm[kind][:, rw, :wd])
            elif nme in sq_rows:
                result.append(acc["sq"][kind][:, sq_rows[nme] * rs:(sq_rows[nme] + 1) * rs])
            elif nme == "w_in":
                result.append(jnp.swapaxes(acc["win"][kind], 1, 2))
            elif nme == "conv_a_w":
                result.append(acc["conv"][kind][:, 0:CA_W])
            else:
                result.append(acc["conv"][kind][:, 8:8 + CC_W])
    return tuple(result)
```
